```python
import math
import jax, jax.numpy as jnp
from jax import lax
import numpy as np

D_MODEL = 2048
BATCH = 2
SEQ = 4096
DEPTH = 2
DEC_BATCH = 128
DEC_SEQ = 4
PAST_LEN = 8192
PAGE_SIZE = 128

N_MIXERS = 2
N_A_LAYERS = (DEPTH + 1) // 2
N_B_LAYERS = DEPTH // 2
RMS_EPS = 1e-6
D_FF = 5632
N_HEADS = 32
N_KV_HEADS = 8
HEAD_DIM = 64
GROUP = N_HEADS // N_KV_HEADS
WINDOW = 128
BLOCK = WINDOW
NUM_BUCKETS = 32
MAX_DISTANCE = 128
NEG_INF = -1e30
GLA_HEADS = 4
GLA_DK = D_MODEL // 2 // GLA_HEADS
GLA_DV = D_MODEL // GLA_HEADS
GLA_KEY_DIM = GLA_HEADS * GLA_DK
GLA_VAL_DIM = GLA_HEADS * GLA_DV
GATE_RANK = 16
GATE_NORMALIZER = 16.0
GLA_CHUNK = 32
GLA_IN_DIM = 2 * GLA_KEY_DIM + 2 * GLA_VAL_DIM + GATE_RANK

kernel_name = 'hybrid_swa_sink_gla_macaron_step'


def _rmsnorm(x, w):
    xf = x.astype(jnp.float32)
    y = xf * lax.rsqrt(jnp.mean(xf * xf, axis=-1, keepdims=True) + RMS_EPS)
    return (y * w.astype(jnp.float32)).astype(x.dtype)


def _half_ffn(x, norm_w, w_gate, w_up, w_down):
    h = _rmsnorm(x, norm_w)
    return x + 0.5 * ((jax.nn.silu(h @ w_gate) * (h @ w_up)) @ w_down)


def _t5_bucket(dist):
    n = jnp.maximum(dist, 0)
    max_exact = NUM_BUCKETS // 2
    nf = jnp.maximum(n, 1).astype(jnp.float32)
    large = max_exact + (jnp.log(nf / max_exact) / math.log(MAX_DISTANCE / max_exact)
                         * (NUM_BUCKETS - max_exact)).astype(jnp.int32)
    large = jnp.minimum(large, NUM_BUCKETS - 1)
    return jnp.where(n < max_exact, n, large)


def _position_bias(rel_bias, dist):
    b = rel_bias[_t5_bucket(dist)].astype(jnp.float32)
    return jnp.transpose(b, (2, 0, 1)).reshape(N_KV_HEADS, GROUP, dist.shape[0], dist.shape[1])


def _sink_attend(q, k, v, bias, valid, sink):
    s = jnp.einsum('...qhgd,...khd->...hgqk', q, k).astype(jnp.float32) * HEAD_DIM ** -0.5 + bias
    s = jnp.where(valid, s, NEG_INF)
    sink_col = jnp.broadcast_to(sink.astype(jnp.float32).reshape(N_KV_HEADS, GROUP, 1, 1), s.shape[:-1] + (1,))
    p = jax.nn.softmax(jnp.concatenate([s, sink_col], axis=-1), axis=-1)[..., :-1]
    return jnp.einsum('...hgqk,...khd->...qhgd', p.astype(v.dtype), v)


def _swa_qkv(h, w_qkv, b_qkv):
    B, T, _ = h.shape
    qkv = h @ w_qkv + b_qkv
    q, k, v = jnp.split(qkv, [N_HEADS * HEAD_DIM, (N_HEADS + N_KV_HEADS) * HEAD_DIM], axis=-1)
    return (q.reshape(B, T, N_KV_HEADS, GROUP, HEAD_DIM),
            k.reshape(B, T, N_KV_HEADS, HEAD_DIM),
            v.reshape(B, T, N_KV_HEADS, HEAD_DIM))


def _swa_prompt(h, w_qkv, b_qkv, w_o, b_o, sink, rel_bias):
    B, T, _ = h.shape
    nb = T // BLOCK
    q, k, v = _swa_qkv(h, w_qkv, b_qkv)
    qb = q.reshape(B, nb, BLOCK, N_KV_HEADS, GROUP, HEAD_DIM)

    def band(t):
        tb = t.reshape(B, nb, BLOCK, N_KV_HEADS, HEAD_DIM)
        prev = jnp.pad(tb, ((0, 0), (1, 0), (0, 0), (0, 0), (0, 0)))[:, :-1]
        return jnp.concatenate([prev, tb], axis=2)

    i = jnp.arange(BLOCK)[:, None]
    j = jnp.arange(2 * BLOCK)[None, :]
    dist = BLOCK + i - j
    in_window = (dist >= 0) & (dist < WINDOW)
    real_key = (jnp.arange(nb) > 0)[:, None, None] | (j >= BLOCK)
    valid = (in_window[None] & real_key)[:, None, None]
    o = _sink_attend(qb, band(k), band(v), _position_bias(rel_bias, dist), valid, sink)
    y = o.reshape(B, T, N_HEADS * HEAD_DIM) @ w_o + b_o
    return y, k[:, -WINDOW:], v[:, -WINDOW:]


def _swa_sample(h, k_buf, v_buf, w_qkv, b_qkv, w_o, b_o, sink, rel_bias):
    B, T, _ = h.shape
    q, k, v = _swa_qkv(h, w_qkv, b_qkv)
    kk = jnp.concatenate([k_buf.astype(k.dtype), k], axis=1)
    vv = jnp.concatenate([v_buf.astype(v.dtype), v], axis=1)
    i = jnp.arange(T)[:, None]
    j = jnp.arange(WINDOW + T)[None, :]
    dist = WINDOW + i - j
    valid = (dist >= 0) & (dist < WINDOW)
    o = _sink_attend(q, kk, vv, _position_bias(rel_bias, dist), valid, sink)
    y = o.reshape(B, T, N_HEADS * HEAD_DIM) @ w_o + b_o
    return y, kk[:, -WINDOW:], vv[:, -WINDOW:]


def _gla_chunk_step(S, inp):
    q, k, v, g = inp
    b = jnp.cumsum(g, axis=-2)
    qe = q * jnp.exp(b)
    ke = k * jnp.exp(-b)
    C = q.shape[-2]
    causal = jnp.tril(jnp.ones((C, C), dtype=bool))
    A = jnp.where(causal, jnp.einsum('bhtd,bhsd->bhts', qe, ke), 0.0)
    o = jnp.einsum('bhts,bhsv->bhtv', A, v) + jnp.einsum('bhtd,bhdv->bhtv', qe, S)
    b_last = b[..., -1:, :]
    S_new = jnp.exp(b_last[..., 0, :])[..., None] * S + jnp.einsum('bhsd,bhsv->bhdv', k * jnp.exp(b_last - b), v)
    return S_new, o


def _gla_recurrence(q, k, v, g, S0, chunk):
    B, T, H, _ = q.shape
    nc = T // chunk

    def to_chunks(t):
        t = t.astype(jnp.float32).reshape(B, nc, chunk, H, t.shape[-1])
        return jnp.moveaxis(t, 1, 0).transpose(0, 1, 3, 2, 4)

    S, o = lax.scan(_gla_chunk_step, S0.astype(jnp.float32), (to_chunks(q), to_chunks(k), to_chunks(v), to_chunks(g)))
    o = o.transpose(1, 0, 3, 2, 4).reshape(B, T, H, v.shape[-1])
    return o, S


def _gla_mixer(h, S0, chunk, w_in, w_gk2, b_gk, norm_w, w_o):
    B, T, _ = h.shape
    proj = h @ w_in
    q, k, v, gate, gk = jnp.split(proj, [GLA_KEY_DIM, 2 * GLA_KEY_DIM, 2 * GLA_KEY_DIM + GLA_VAL_DIM,
                                         2 * GLA_KEY_DIM + 2 * GLA_VAL_DIM], axis=-1)
    log_a = jax.nn.log_sigmoid((gk @ w_gk2 + b_gk).astype(jnp.float32)) / GATE_NORMALIZER
    q = q.reshape(B, T, GLA_HEADS, GLA_DK) * GLA_DK ** -0.5
    k = k.reshape(B, T, GLA_HEADS, GLA_DK)
    v = v.reshape(B, T, GLA_HEADS, GLA_DV)
    o, S = _gla_recurrence(q, k, v, log_a.reshape(B, T, GLA_HEADS, GLA_DK), S0, chunk)
    o = _rmsnorm(o, norm_w) * jax.nn.silu(gate.reshape(B, T, GLA_HEADS, GLA_DV).astype(jnp.float32))
    y = o.reshape(B, T, GLA_VAL_DIM).astype(h.dtype) @ w_o
    return y, S.astype(h.dtype)


def setup_inputs(seed: int = 0) -> dict:
    key = jax.random.key(seed)
    ks = iter(jax.random.split(key, 32))
    f32 = jnp.float32

    def nrm(shape, scale):
        return jax.random.normal(next(ks), shape, f32) * scale

    def gain(shape):
        return 1.0 + nrm(shape, 0.02)

    qkv_dim = (N_HEADS + 2 * N_KV_HEADS) * HEAD_DIM
    return {
        'x_prompt': nrm((BATCH, SEQ, D_MODEL), 1.0),
        'x_sample': nrm((DEC_BATCH, DEC_SEQ, D_MODEL), 1.0),
        'cache_swa_k': nrm((N_A_LAYERS, DEC_BATCH, WINDOW, N_KV_HEADS, HEAD_DIM), 1.0),
        'cache_swa_v': nrm((N_A_LAYERS, DEC_BATCH, WINDOW, N_KV_HEADS, HEAD_DIM), 1.0),
        'state_gla': nrm((N_B_LAYERS, DEC_BATCH, GLA_HEADS, GLA_DK, GLA_DV), GLA_DK ** -0.5),
        'norm_ffn1': gain((DEPTH, D_MODEL)),
        'ffn1_w_gate': nrm((DEPTH, D_MODEL, D_FF), D_MODEL ** -0.5),
        'ffn1_w_up': nrm((DEPTH, D_MODEL, D_FF), D_MODEL ** -0.5),
        'ffn1_w_down': nrm((DEPTH, D_FF, D_MODEL), D_FF ** -0.5),
        'norm_mix': gain((DEPTH, D_MODEL)),
        'norm_ffn2': gain((DEPTH, D_MODEL)),
        'ffn2_w_gate': nrm((DEPTH, D_MODEL, D_FF), D_MODEL ** -0.5),
        'ffn2_w_up': nrm((DEPTH, D_MODEL, D_FF), D_MODEL ** -0.5),
        'ffn2_w_down': nrm((DEPTH, D_FF, D_MODEL), D_FF ** -0.5),
        'norm_final': gain((D_MODEL,)),
        'rel_bias': nrm((NUM_BUCKETS, N_HEADS), 0.5),
        'swa_w_qkv': nrm((N_A_LAYERS, D_MODEL, qkv_dim), D_MODEL ** -0.5),
        'swa_b_qkv': nrm((N_A_LAYERS, qkv_dim), 0.02),
        'swa_w_o': nrm((N_A_LAYERS, N_HEADS * HEAD_DIM, D_MODEL), (N_HEADS * HEAD_DIM) ** -0.5),
        'swa_b_o': nrm((N_A_LAYERS, D_MODEL), 0.02),
        'swa_sinks': nrm((N_A_LAYERS, N_HEADS), 0.5),
        'gla_w_in': nrm((N_B_LAYERS, D_MODEL, GLA_IN_DIM), D_MODEL ** -0.5),
        'gla_w_gk2': nrm((N_B_LAYERS, GATE_RANK, GLA_KEY_DIM), GATE_RANK ** -0.5),
        'gla_b_gk': nrm((N_B_LAYERS, GLA_KEY_DIM), 0.1),
        'gla_norm': gain((N_B_LAYERS, GLA_DV)),
        'gla_w_o': nrm((N_B_LAYERS, GLA_VAL_DIM, D_MODEL), GLA_VAL_DIM ** -0.5),
    }


def reference(x_prompt, x_sample, cache_swa_k, cache_swa_v, state_gla,
              norm_ffn1, ffn1_w_gate, ffn1_w_up, ffn1_w_down, norm_mix,
              norm_ffn2, ffn2_w_gate, ffn2_w_up, ffn2_w_down, norm_final, rel_bias,
              swa_w_qkv, swa_b_qkv, swa_w_o, swa_b_o, swa_sinks,
              gla_w_in, gla_w_gk2, gla_b_gk, gla_norm, gla_w_o):
    xp, xs = x_prompt, x_sample
    swa_kp, swa_vp, swa_ks, swa_vs, gla_sp, gla_ss = [], [], [], [], [], []
    for i in range(DEPTH):
        xp = _half_ffn(xp, norm_ffn1[i], ffn1_w_gate[i], ffn1_w_up[i], ffn1_w_down[i])
        xs = _half_ffn(xs, norm_ffn1[i], ffn1_w_gate[i], ffn1_w_up[i], ffn1_w_down[i])
        hp = _rmsnorm(xp, norm_mix[i])
        hs = _rmsnorm(xs, norm_mix[i])
        j = i // N_MIXERS
        if i % N_MIXERS == 0:
            yp, kp, vp = _swa_prompt(hp, swa_w_qkv[j], swa_b_qkv[j], swa_w_o[j], swa_b_o[j], swa_sinks[j], rel_bias)
            ys, kn, vn = _swa_sample(hs, cache_swa_k[j], cache_swa_v[j], swa_w_qkv[j], swa_b_qkv[j],
                                     swa_w_o[j], swa_b_o[j], swa_sinks[j], rel_bias)
            swa_kp.append(kp)
            swa_vp.append(vp)
            swa_ks.append(kn)
            swa_vs.append(vn)
        else:
            S0 = jnp.zeros((hp.shape[0], GLA_HEADS, GLA_DK, GLA_DV), jnp.float32)
            yp, sp = _gla_mixer(hp, S0, GLA_CHUNK, gla_w_in[j], gla_w_gk2[j], gla_b_gk[j], gla_norm[j], gla_w_o[j])
            ys, sn = _gla_mixer(hs, state_gla[j], hs.shape[1], gla_w_in[j], gla_w_gk2[j], gla_b_gk[j],
                                gla_norm[j], gla_w_o[j])
            gla_sp.append(sp)
            gla_ss.append(sn)
        xp = xp + yp
        xs = xs + ys
        xp = _half_ffn(xp, norm_ffn2[i], ffn2_w_gate[i], ffn2_w_up[i], ffn2_w_down[i])
        xs = _half_ffn(xs, norm_ffn2[i], ffn2_w_gate[i], ffn2_w_up[i], ffn2_w_down[i])
    y_prompt = _rmsnorm(xp, norm_final)
    y_sample = _rmsnorm(xs, norm_final)
    return (y_prompt, y_sample, jnp.stack(swa_kp), jnp.stack(swa_vp), jnp.stack(swa_ks), jnp.stack(swa_vs),
            jnp.stack(gla_sp), jnp.stack(gla_ss))
```

```python
import functools
import math

import numpy as np
import jax
import jax.numpy as jnp
from jax import lax
from jax.experimental import pallas as pl
from jax.experimental.pallas import tpu as pltpu

F32 = jnp.float32
BF16 = jnp.bfloat16

D_MODEL = 2048
BATCH = 2
SEQ = 4096
DEPTH = 2
DEC_BATCH = 128
DEC_SEQ = 4
RMS_EPS = 1e-6
D_FF = 5632
N_HEADS = 32
N_KV_HEADS = 8
HEAD_DIM = 64
GROUP = N_HEADS // N_KV_HEADS
WINDOW = 128
NUM_BUCKETS = 32
MAX_DISTANCE = 128
NEG_INF = -1e30
GLA_HEADS = 4
GLA_DK = 256
GLA_DV = 512
GLA_KEY_DIM = GLA_HEADS * GLA_DK
GLA_VAL_DIM = GLA_HEADS * GLA_DV
GATE_RANK = 16
GATE_NORMALIZER = 16.0
GLA_MAIN_DIM = 2 * GLA_KEY_DIM + 2 * GLA_VAL_DIM
QKV_DIM = (N_HEADS + 2 * N_KV_HEADS) * HEAD_DIM

N_PROMPT = BATCH * SEQ
N_SAMPLE = DEC_BATCH * DEC_SEQ
N_TOK = N_PROMPT + N_SAMPLE

LANES = 128
SUBLANES = 8
VMEM_LIMIT = 56 * 1024 * 1024

TM = 512
TF = 512
TN = 512
GLA_C = 64
NB = SEQ // WINDOW


def _rms(x, w):
    return x * lax.rsqrt(jnp.mean(x * x, axis=-1, keepdims=True) + RMS_EPS) * w


def _silu(x):
    return x * jax.nn.sigmoid(x)


def _cparams(sem):
    return pltpu.CompilerParams(dimension_semantics=sem, vmem_limit_bytes=VMEM_LIMIT)


def _ffn_body(final_norm, x_ref, nw_ref, wg_ref, wu_ref, wd_ref, *rest):
    if final_norm:
        fw_ref, o_ref, h_ref, acc_ref = rest
    else:
        o_ref, h_ref, acc_ref = rest
    j = pl.program_id(1)

    @pl.when(j == 0)
    def _():
        h_ref[...] = _rms(x_ref[...], nw_ref[...]).astype(BF16)
        acc_ref[...] = jnp.zeros_like(acc_ref)

    h = h_ref[...]
    g = jnp.dot(h, wg_ref[...], preferred_element_type=F32)
    u = jnp.dot(h, wu_ref[...], preferred_element_type=F32)
    a = (_silu(g) * u).astype(BF16)
    acc_ref[...] += jnp.dot(a, wd_ref[...], preferred_element_type=F32)

    @pl.when(j == pl.num_programs(1) - 1)
    def _():
        y = x_ref[...] + 0.5 * acc_ref[...]
        if final_norm:
            y = _rms(y, fw_ref[...])
        o_ref[...] = y


def _ffn(x, nw, wg, wu, wd, final_w=None):
    m = x.shape[0]
    final_norm = final_w is not None
    in_specs = [
        pl.BlockSpec((TM, D_MODEL), lambda i, j: (i, 0)),
        pl.BlockSpec((1, D_MODEL), lambda i, j: (0, 0)),
        pl.BlockSpec((D_MODEL, TF), lambda i, j: (0, j)),
        pl.BlockSpec((D_MODEL, TF), lambda i, j: (0, j)),
        pl.BlockSpec((TF, D_MODEL), lambda i, j: (j, 0)),
    ]
    args = [x, nw.reshape(1, D_MODEL), wg, wu, wd]
    if final_norm:
        in_specs.append(pl.BlockSpec((1, D_MODEL), lambda i, j: (0, 0)))
        args.append(final_w.reshape(1, D_MODEL))
    return pl.pallas_call(
        functools.partial(_ffn_body, final_norm),
        grid=(m // TM, D_FF // TF),
        in_specs=in_specs,
        out_specs=pl.BlockSpec((TM, D_MODEL), lambda i, j: (i, 0)),
        out_shape=jax.ShapeDtypeStruct((m, D_MODEL), F32),
        scratch_shapes=[pltpu.VMEM((TM, D_MODEL), BF16), pltpu.VMEM((TM, D_MODEL), F32)],
        compiler_params=_cparams(("parallel", "arbitrary")),
        name="ffn",
    )(*args)


def _norm_matmul_body(x_ref, nw_ref, w_ref, b_ref, o_ref, h_ref):
    @pl.when(pl.program_id(1) == 0)
    def _():
        h_ref[...] = _rms(x_ref[...], nw_ref[...]).astype(BF16)

    acc = jnp.dot(h_ref[...], w_ref[...], preferred_element_type=F32)
    o_ref[...] = (acc + b_ref[...]).astype(o_ref.dtype)


def _norm_matmul(x, nw, w, b, out_dtype, name):
    m = x.shape[0]
    n = w.shape[1]
    return pl.pallas_call(
        _norm_matmul_body,
        grid=(m // TM, n // TN),
        in_specs=[
            pl.BlockSpec((TM, D_MODEL), lambda i, j: (i, 0)),
            pl.BlockSpec((1, D_MODEL), lambda i, j: (0, 0)),
            pl.BlockSpec((D_MODEL, TN), lambda i, j: (0, j)),
            pl.BlockSpec((1, TN), lambda i, j: (0, j)),
        ],
        out_specs=pl.BlockSpec((TM, TN), lambda i, j: (i, j)),
        out_shape=jax.ShapeDtypeStruct((m, n), out_dtype),
        scratch_shapes=[pltpu.VMEM((TM, D_MODEL), BF16)],
        compiler_params=_cparams(("parallel", "arbitrary")),
        name=name,
    )(x, nw.reshape(1, D_MODEL), w, b.reshape(1, n))


def _matmul_res_body(a_ref, w_ref, b_ref, r_ref, o_ref):
    acc = jnp.dot(a_ref[...].astype(BF16), w_ref[...], preferred_element_type=F32)
    o_ref[...] = r_ref[...] + acc + b_ref[...]


def _matmul_res(a, w, b, res, name):
    m, k = a.shape
    n = w.shape[1]
    return pl.pallas_call(
        _matmul_res_body,
        grid=(m // TM, n // TN),
        in_specs=[
            pl.BlockSpec((TM, k), lambda i, j: (i, 0)),
            pl.BlockSpec((k, TN), lambda i, j: (0, j)),
            pl.BlockSpec((1, TN), lambda i, j: (0, j)),
            pl.BlockSpec((TM, TN), lambda i, j: (i, j)),
        ],
        out_specs=pl.BlockSpec((TM, TN), lambda i, j: (i, j)),
        out_shape=jax.ShapeDtypeStruct((m, n), F32),
        compiler_params=_cparams(("parallel", "arbitrary")),
        name=name,
    )(a, w, b.reshape(1, n), res)


def _t5_bucket_table():
    i = np.arange(WINDOW)[:, None]
    j = np.arange(2 * WINDOW)[None, :]
    n = np.maximum(WINDOW + i - j, 0)
    max_exact = NUM_BUCKETS // 2
    nf = np.maximum(n, 1).astype(np.float32)
    large = max_exact + (np.log(nf / np.float32(max_exact)) / np.float32(math.log(MAX_DISTANCE / max_exact))
                         * np.float32(NUM_BUCKETS - max_exact)).astype(np.int32)
    large = np.minimum(large, NUM_BUCKETS - 1)
    return np.where(n < max_exact, n, large).astype(np.int32)


def _bias_table_body(bucket_ref, rb_ref, o_ref):
    h = pl.program_id(0)
    bucket = bucket_ref[...]
    acc = jnp.zeros((WINDOW, 2 * WINDOW), F32)
    for b in range(NUM_BUCKETS):
        acc = jnp.where(bucket == b, rb_ref[b, h], acc)
    i = lax.broadcasted_iota(jnp.int32, (WINDOW, 2 * WINDOW), 0)
    j = lax.broadcasted_iota(jnp.int32, (WINDOW, 2 * WINDOW), 1)
    dist = WINDOW + i - j
    o_ref[0] = jnp.where((dist >= 0) & (dist < WINDOW), acc, NEG_INF)


def _bias_table(rel_bias):
    return pl.pallas_call(
        _bias_table_body,
        grid=(N_HEADS,),
        in_specs=[
            pl.BlockSpec((WINDOW, 2 * WINDOW), lambda h: (0, 0)),
            pl.BlockSpec(memory_space=pltpu.SMEM),
        ],
        out_specs=pl.BlockSpec((1, WINDOW, 2 * WINDOW), lambda h: (h, 0, 0)),
        out_shape=jax.ShapeDtypeStruct((N_HEADS, WINDOW, 2 * WINDOW), F32),
        name="bias_table",
    )(jnp.asarray(_t5_bucket_table()), rel_bias)


def _softmax_with_sink(s, sink_col):
    m = jnp.maximum(jnp.max(s, axis=-1, keepdims=True), sink_col)
    p = jnp.exp(s - m)
    denom = jnp.sum(p, axis=-1, keepdims=True) + jnp.exp(sink_col - m)
    return p, 1.0 / denom


KV_PER_STEP = LANES // HEAD_DIM
Q_COLS_PER_STEP = KV_PER_STEP * GROUP * HEAD_DIM


def _swa_prompt_body(sink_ref, q_ref, kp_ref, ko_ref, vp_ref, vo_ref, bias_ref, o_ref):
    blk = pl.program_id(1)
    pair = pl.program_id(2)
    rows = GROUP * WINDOW
    col = lax.broadcasted_iota(jnp.int32, (rows, 2 * WINDOW), 1)
    real_key = (blk > 0) | (col >= WINDOW)
    outs = []
    for hh in range(KV_PER_STEP):
        ks = slice(hh * HEAD_DIM, (hh + 1) * HEAD_DIM)
        k = jnp.concatenate([kp_ref[:, ks], ko_ref[:, ks]], axis=0).astype(BF16)
        v = jnp.concatenate([vp_ref[:, ks], vo_ref[:, ks]], axis=0).astype(BF16)
        q0 = hh * GROUP * HEAD_DIM
        qs = jnp.concatenate(
            [q_ref[:, q0 + g * HEAD_DIM:q0 + (g + 1) * HEAD_DIM] for g in range(GROUP)], axis=0)
        qs = (qs * HEAD_DIM ** -0.5).astype(BF16)
        s = lax.dot_general(qs, k, (((1,), (1,)), ((), ())), preferred_element_type=F32)
        s = s + bias_ref[hh * GROUP:(hh + 1) * GROUP].reshape(rows, 2 * WINDOW)
        s = jnp.where(real_key, s, NEG_INF)
        head0 = (pair * KV_PER_STEP + hh) * GROUP
        sink_col = jnp.concatenate(
            [jnp.full((WINDOW, 1), sink_ref[head0 + g], F32) for g in range(GROUP)], axis=0)
        p, inv = _softmax_with_sink(s, sink_col)
        o = jnp.dot(p.astype(BF16), v, preferred_element_type=F32) * inv
        outs += [o[g * WINDOW:(g + 1) * WINDOW] for g in range(GROUP)]
    o_ref[...] = jnp.concatenate(outs, axis=1).astype(o_ref.dtype)


def _swa_prompt(qkv, bias_tbl, sinks):
    k_col0 = N_HEADS * HEAD_DIM // LANES
    v_col0 = (N_HEADS + N_KV_HEADS) * HEAD_DIM // LANES

    def prev(b, i, p):
        return b * NB + jnp.maximum(i - 1, 0)

    return pl.pallas_call(
        _swa_prompt_body,
        grid=(BATCH, NB, N_KV_HEADS // KV_PER_STEP),
        in_specs=[
            pl.BlockSpec(memory_space=pltpu.SMEM),
            pl.BlockSpec((WINDOW, Q_COLS_PER_STEP), lambda b, i, p: (b * NB + i, p)),
            pl.BlockSpec((WINDOW, LANES), lambda b, i, p: (prev(b, i, p), k_col0 + p)),
            pl.BlockSpec((WINDOW, LANES), lambda b, i, p: (b * NB + i, k_col0 + p)),
            pl.BlockSpec((WINDOW, LANES), lambda b, i, p: (prev(b, i, p), v_col0 + p)),
            pl.BlockSpec((WINDOW, LANES), lambda b, i, p: (b * NB + i, v_col0 + p)),
            pl.BlockSpec((KV_PER_STEP * GROUP, WINDOW, 2 * WINDOW), lambda b, i, p: (p, 0, 0)),
        ],
        out_specs=pl.BlockSpec((WINDOW, Q_COLS_PER_STEP), lambda b, i, p: (b * NB + i, p)),
        out_shape=jax.ShapeDtypeStruct((N_PROMPT, N_HEADS * HEAD_DIM), BF16),
        compiler_params=_cparams(("parallel", "parallel", "arbitrary")),
        name="swa_prompt",
    )(sinks, qkv, qkv, qkv, qkv, qkv, bias_tbl)


S_ROWS = N_KV_HEADS * GROUP * DEC_SEQ
KV_COLS = N_KV_HEADS * HEAD_DIM


def _swa_sample_body(q_ref, kn_ref, vn_ref, ck_ref, cv_ref, bias_ref, sink_ref,
                     o_ref, ko_ref, vo_ref):
    x = (q_ref[0] * HEAD_DIM ** -0.5).astype(BF16)
    xt = jnp.concatenate([x] * N_KV_HEADS, axis=1)
    row_kv = lax.broadcasted_iota(jnp.int32, (S_ROWS, KV_COLS), 0) // (GROUP * DEC_SEQ)
    col_kv = lax.broadcasted_iota(jnp.int32, (S_ROWS, KV_COLS), 1) // HEAD_DIM
    own = row_kv == col_kv
    qbd = jnp.where(own, xt, jnp.zeros_like(xt))
    pad = jnp.zeros((WINDOW - SUBLANES, KV_COLS), F32)
    kk = jnp.concatenate([ck_ref[0], kn_ref[0], pad], axis=0).astype(BF16)
    vv = jnp.concatenate([cv_ref[0], vn_ref[0], pad], axis=0).astype(BF16)
    s = lax.dot_general(qbd, kk, (((1,), (1,)), ((), ())), preferred_element_type=F32)
    s = s + bias_ref[...]
    p, inv = _softmax_with_sink(s, sink_ref[...])
    of = jnp.dot(p.astype(BF16), vv, preferred_element_type=F32)
    of = jnp.where(own, of, 0.0)
    o = of[:, 0:HEAD_DIM]
    for c in range(1, N_KV_HEADS):
        o = o + of[:, c * HEAD_DIM:(c + 1) * HEAD_DIM]
    o_ref[0] = o * inv
    keep = WINDOW - DEC_SEQ
    ko_ref[0, 0:keep, :] = ck_ref[0, DEC_SEQ:WINDOW, :]
    ko_ref[0, keep:WINDOW, :] = kn_ref[0, 0:DEC_SEQ, :]
    vo_ref[0, 0:keep, :] = cv_ref[0, DEC_SEQ:WINDOW, :]
    vo_ref[0, keep:WINDOW, :] = vn_ref[0, 0:DEC_SEQ, :]


def _swa_sample(q_rows, k_new8, v_new8, cache_k, cache_v, bias_s, sink_col):
    seq3 = lambda s: (s, 0, 0)
    full2 = lambda s: (0, 0)
    return pl.pallas_call(
        _swa_sample_body,
        grid=(DEC_BATCH,),
        in_specs=[
            pl.BlockSpec((1, S_ROWS, HEAD_DIM), seq3),
            pl.BlockSpec((1, SUBLANES, KV_COLS), seq3),
            pl.BlockSpec((1, SUBLANES, KV_COLS), seq3),
            pl.BlockSpec((1, WINDOW, KV_COLS), seq3),
            pl.BlockSpec((1, WINDOW, KV_COLS), seq3),
            pl.BlockSpec((S_ROWS, 2 * WINDOW), full2),
            pl.BlockSpec((S_ROWS, 1), full2),
        ],
        out_specs=[
            pl.BlockSpec((1, S_ROWS, HEAD_DIM), seq3),
            pl.BlockSpec((1, WINDOW, KV_COLS), seq3),
            pl.BlockSpec((1, WINDOW, KV_COLS), seq3),
        ],
        out_shape=[
            jax.ShapeDtypeStruct((DEC_BATCH, S_ROWS, HEAD_DIM), F32),
            jax.ShapeDtypeStruct((DEC_BATCH, WINDOW, KV_COLS), F32),
            jax.ShapeDtypeStruct((DEC_BATCH, WINDOW, KV_COLS), F32),
        ],
        compiler_params=_cparams(("parallel",)),
        name="swa_sample",
    )(q_rows, k_new8, v_new8, cache_k, cache_v, bias_s, sink_col)


def _log_sigmoid(x):
    return jnp.minimum(x, 0.0) - jnp.log1p(jnp.exp(-jnp.abs(x)))


def _gla_gate_body(x_ref, nw_ref, w1_ref, w2_ref, b_ref, o_ref):
    h = _rms(x_ref[...], nw_ref[...]).astype(BF16)
    gk = jnp.dot(h, w1_ref[...], preferred_element_type=F32)
    z = jnp.dot(gk.astype(BF16), w2_ref[...], preferred_element_type=F32) + b_ref[...]
    o_ref[...] = _log_sigmoid(z) / GATE_NORMALIZER


def _gla_gate(x, nw, w1p, w2p, b):
    m = x.shape[0]
    return pl.pallas_call(
        _gla_gate_body,
        grid=(m // TM,),
        in_specs=[
            pl.BlockSpec((TM, D_MODEL), lambda i: (i, 0)),
            pl.BlockSpec((1, D_MODEL), lambda i: (0, 0)),
            pl.BlockSpec((D_MODEL, LANES), lambda i: (0, 0)),
            pl.BlockSpec((LANES, GLA_KEY_DIM), lambda i: (0, 0)),
            pl.BlockSpec((1, GLA_KEY_DIM), lambda i: (0, 0)),
        ],
        out_specs=pl.BlockSpec((TM, GLA_KEY_DIM), lambda i: (i, 0)),
        out_shape=jax.ShapeDtypeStruct((m, GLA_KEY_DIM), F32),
        compiler_params=_cparams(("parallel",)),
        name="gla_gate",
    )(x, nw.reshape(1, D_MODEL), w1p, w2p, b.reshape(1, GLA_KEY_DIM))


def _gla_out(o, gate, norm_w):
    return _rms(o, norm_w) * _silu(gate)


def _cumsum_rows(g):
    c = g.shape[0]
    tri = (lax.broadcasted_iota(jnp.int32, (c, c), 0) >= lax.broadcasted_iota(jnp.int32, (c, c), 1)).astype(F32)
    return jnp.dot(tri, g, precision=lax.Precision.HIGHEST, preferred_element_type=F32)


def _causal(a):
    c = a.shape[0]
    keep = lax.broadcasted_iota(jnp.int32, (c, c), 0) >= lax.broadcasted_iota(jnp.int32, (c, c), 1)
    return jnp.where(keep, a, 0.0)


_NT = (((1,), (1,)), ((), ()))
_TN = (((0,), (0,)), ((), ()))


def _gla_prompt_body(q_ref, k_ref, v_ref, gate_ref, la_ref, nw_ref, o_ref, s_ref, st_ref):
    c = pl.program_id(2)

    @pl.when(c == 0)
    def _():
        st_ref[...] = jnp.zeros_like(st_ref)

    b = _cumsum_rows(la_ref[...])
    k = k_ref[...]
    v = v_ref[...].astype(BF16)
    qe = (q_ref[...] * GLA_DK ** -0.5 * jnp.exp(b)).astype(BF16)
    ke = (k * jnp.exp(-b)).astype(BF16)
    a = _causal(lax.dot_general(qe, ke, _NT, preferred_element_type=F32))
    st = st_ref[...]
    o = (jnp.dot(a.astype(BF16), v, preferred_element_type=F32)
         + lax.dot_general(qe, st.astype(BF16), _NT, preferred_element_type=F32))
    b_last = b[GLA_C - 1:GLA_C, :]
    kd = (k * jnp.exp(b_last - b)).astype(BF16)
    st_new = st * jnp.exp(b_last) + lax.dot_general(v, kd, _TN, preferred_element_type=F32)
    st_ref[...] = st_new
    o_ref[...] = _gla_out(o, gate_ref[...], nw_ref[...]).astype(o_ref.dtype)

    @pl.when(c == pl.num_programs(2) - 1)
    def _():
        s_ref[0, 0] = st_new.T


def _gla_prompt(proj, log_a, norm_w):
    nc = SEQ // GLA_C
    kb = GLA_KEY_DIM // GLA_DK
    vb = 2 * GLA_KEY_DIM // GLA_DV
    gb = vb + GLA_VAL_DIM // GLA_DV
    row = lambda b, h, c: b * nc + c
    return pl.pallas_call(
        _gla_prompt_body,
        grid=(BATCH, GLA_HEADS, nc),
        in_specs=[
            pl.BlockSpec((GLA_C, GLA_DK), lambda b, h, c: (row(b, h, c), h)),
            pl.BlockSpec((GLA_C, GLA_DK), lambda b, h, c: (row(b, h, c), kb + h)),
            pl.BlockSpec((GLA_C, GLA_DV), lambda b, h, c: (row(b, h, c), vb + h)),
            pl.BlockSpec((GLA_C, GLA_DV), lambda b, h, c: (row(b, h, c), gb + h)),
            pl.BlockSpec((GLA_C, GLA_DK), lambda b, h, c: (row(b, h, c), h)),
            pl.BlockSpec((1, GLA_DV), lambda b, h, c: (0, 0)),
        ],
        out_specs=[
            pl.BlockSpec((GLA_C, GLA_DV), lambda b, h, c: (row(b, h, c), h)),
            pl.BlockSpec((1, 1, GLA_DK, GLA_DV), lambda b, h, c: (b, h, 0, 0)),
        ],
        out_shape=[
            jax.ShapeDtypeStruct((N_PROMPT, GLA_VAL_DIM), BF16),
            jax.ShapeDtypeStruct((BATCH, GLA_HEADS, GLA_DK, GLA_DV), F32),
        ],
        scratch_shapes=[pltpu.VMEM((GLA_DV, GLA_DK), F32)],
        compiler_params=_cparams(("parallel", "parallel", "arbitrary")),
        name="gla_prompt",
    )(proj, proj, proj, proj, log_a, norm_w.reshape(1, GLA_DV))


def _gla_sample_body(proj_ref, la_ref, s0_ref, nw_ref, o_ref, s_ref):
    ones = jnp.ones((DEC_SEQ, LANES), F32)
    for h in range(GLA_HEADS):
        q = proj_ref[0, :, h * GLA_DK:(h + 1) * GLA_DK]
        k = proj_ref[0, :, GLA_KEY_DIM + h * GLA_DK:GLA_KEY_DIM + (h + 1) * GLA_DK]
        v0 = 2 * GLA_KEY_DIM + h * GLA_DV
        v = proj_ref[0, :, v0:v0 + GLA_DV].astype(BF16)
        gate = proj_ref[0, :, v0 + GLA_VAL_DIM:v0 + GLA_VAL_DIM + GLA_DV]
        g = la_ref[0, :, h * GLA_DK:(h + 1) * GLA_DK]
        b = _cumsum_rows(g)
        qe = (q * GLA_DK ** -0.5 * jnp.exp(b)).astype(BF16)
        ke = (k * jnp.exp(-b)).astype(BF16)
        a = _causal(lax.dot_general(qe, ke, _NT, preferred_element_type=F32))
        s0 = s0_ref[0, h]
        o = (jnp.dot(a.astype(BF16), v, preferred_element_type=F32)
             + jnp.dot(qe, s0.astype(BF16), preferred_element_type=F32))
        b_last = b[DEC_SEQ - 1:DEC_SEQ, :]
        kd = (k * jnp.exp(b_last - b)).astype(BF16)
        dcol = jnp.exp(lax.dot_general(g, ones, _TN, precision=lax.Precision.HIGHEST,
                                       preferred_element_type=F32))
        decay = jnp.concatenate([dcol] * (GLA_DV // LANES), axis=1)
        s_ref[0, h] = s0 * decay + lax.dot_general(kd, v, _TN, preferred_element_type=F32)
        o_ref[0, :, h * GLA_DV:(h + 1) * GLA_DV] = _gla_out(o, gate, nw_ref[...])


def _gla_sample(proj3, log_a3, state, norm_w):
    row0 = N_PROMPT // DEC_SEQ
    return pl.pallas_call(
        _gla_sample_body,
        grid=(DEC_BATCH,),
        in_specs=[
            pl.BlockSpec((1, DEC_SEQ, GLA_MAIN_DIM), lambda s: (row0 + s, 0, 0)),
            pl.BlockSpec((1, DEC_SEQ, GLA_KEY_DIM), lambda s: (row0 + s, 0, 0)),
            pl.BlockSpec((1, GLA_HEADS, GLA_DK, GLA_DV), lambda s: (s, 0, 0, 0)),
            pl.BlockSpec((1, GLA_DV), lambda s: (0, 0)),
        ],
        out_specs=[
            pl.BlockSpec((1, DEC_SEQ, GLA_VAL_DIM), lambda s: (s, 0, 0)),
            pl.BlockSpec((1, GLA_HEADS, GLA_DK, GLA_DV), lambda s: (s, 0, 0, 0)),
        ],
        out_shape=[
            jax.ShapeDtypeStruct((DEC_BATCH, DEC_SEQ, GLA_VAL_DIM), F32),
            jax.ShapeDtypeStruct((DEC_BATCH, GLA_HEADS, GLA_DK, GLA_DV), F32),
        ],
        compiler_params=_cparams(("parallel",)),
        name="gla_sample",
    )(proj3, log_a3, state, norm_w.reshape(1, GLA_DV))


def _swa_layer(x, cache_k, cache_v, norm_w, w_qkv, b_qkv, w_o, b_o, sinks, rel_bias):
    qkv = _norm_matmul(x, norm_w, w_qkv.astype(BF16), b_qkv, F32, "swa_qkv")
    bias_tbl = _bias_table(rel_bias)
    o_p = _swa_prompt(qkv, bias_tbl, sinks)

    nq = N_HEADS * HEAD_DIM
    qkv_s = qkv[N_PROMPT:]
    q_rows = (qkv_s[:, :nq].reshape(DEC_BATCH, DEC_SEQ, N_KV_HEADS, GROUP, HEAD_DIM)
              .transpose(0, 2, 3, 1, 4).reshape(DEC_BATCH, S_ROWS, HEAD_DIM))
    pad8 = ((0, 0), (0, SUBLANES - DEC_SEQ), (0, 0))
    k_new8 = jnp.pad(qkv_s[:, nq:nq + KV_COLS].reshape(DEC_BATCH, DEC_SEQ, KV_COLS), pad8)
    v_new8 = jnp.pad(qkv_s[:, nq + KV_COLS:].reshape(DEC_BATCH, DEC_SEQ, KV_COLS), pad8)
    bias_s = bias_tbl[:, :DEC_SEQ, :].reshape(S_ROWS, 2 * WINDOW)
    sink_col = jnp.repeat(sinks, DEC_SEQ).reshape(S_ROWS, 1)
    o_s, k_s, v_s = _swa_sample(q_rows, k_new8, v_new8,
                                cache_k.reshape(DEC_BATCH, WINDOW, KV_COLS),
                                cache_v.reshape(DEC_BATCH, WINDOW, KV_COLS), bias_s, sink_col)
    o_s = (o_s.reshape(DEC_BATCH, N_KV_HEADS, GROUP, DEC_SEQ, HEAD_DIM)
           .transpose(0, 3, 1, 2, 4).reshape(N_SAMPLE, nq))
    o_all = jnp.concatenate([o_p, o_s.astype(BF16)], axis=0)
    x = _matmul_res(o_all, w_o.astype(BF16), b_o, x, "swa_out")

    kv_p = qkv[:N_PROMPT].reshape(BATCH, SEQ, QKV_DIM)[:, SEQ - WINDOW:, nq:]
    k_p = kv_p[..., :KV_COLS].reshape(BATCH, WINDOW, N_KV_HEADS, HEAD_DIM)
    v_p = kv_p[..., KV_COLS:].reshape(BATCH, WINDOW, N_KV_HEADS, HEAD_DIM)
    shape_s = (DEC_BATCH, WINDOW, N_KV_HEADS, HEAD_DIM)
    return x, k_p, v_p, k_s.reshape(shape_s), v_s.reshape(shape_s)


def _gla_layer(x, state, norm_w, w_in, w_gk2, b_gk, gnorm, w_o):
    w_main = w_in[:, :GLA_MAIN_DIM].astype(BF16)
    w1p = jnp.pad(w_in[:, GLA_MAIN_DIM:], ((0, 0), (0, LANES - GATE_RANK))).astype(BF16)
    w2p = jnp.pad(w_gk2, ((0, LANES - GATE_RANK), (0, 0))).astype(BF16)
    proj = _norm_matmul(x, norm_w, w_main, jnp.zeros((GLA_MAIN_DIM,), F32), F32, "gla_in")
    log_a = _gla_gate(x, norm_w, w1p, w2p, b_gk)
    o_p, s_p = _gla_prompt(proj, log_a, gnorm)
    o_s, s_s = _gla_sample(proj.reshape(N_TOK // DEC_SEQ, DEC_SEQ, GLA_MAIN_DIM),
                           log_a.reshape(N_TOK // DEC_SEQ, DEC_SEQ, GLA_KEY_DIM), state, gnorm)
    o_all = jnp.concatenate([o_p, o_s.reshape(N_SAMPLE, GLA_VAL_DIM).astype(BF16)], axis=0)
    x = _matmul_res(o_all, w_o.astype(BF16), jnp.zeros((D_MODEL,), F32), x, "gla_out")
    return x, s_p, s_s


def kernel(x_prompt, x_sample, cache_swa_k, cache_swa_v, state_gla, norm_ffn1, ffn1_w_gate, ffn1_w_up,
           ffn1_w_down, norm_mix, norm_ffn2, ffn2_w_gate, ffn2_w_up, ffn2_w_down, norm_final, rel_bias,
           swa_w_qkv, swa_b_qkv, swa_w_o, swa_b_o, swa_sinks, gla_w_in, gla_w_gk2, gla_b_gk, gla_norm,
           gla_w_o):
    x = jnp.concatenate([x_prompt.reshape(N_PROMPT, D_MODEL), x_sample.reshape(N_SAMPLE, D_MODEL)], axis=0)
    swa_kp, swa_vp, swa_ks, swa_vs, gla_sp, gla_ss = [], [], [], [], [], []
    for i in range(DEPTH):
        x = _ffn(x, norm_ffn1[i], ffn1_w_gate[i].astype(BF16), ffn1_w_up[i].astype(BF16),
                 ffn1_w_down[i].astype(BF16))
        j = i // 2
        if i % 2 == 0:
            x, kp, vp, ks, vs = _swa_layer(x, cache_swa_k[j], cache_swa_v[j], norm_mix[i], swa_w_qkv[j],
                                           swa_b_qkv[j], swa_w_o[j], swa_b_o[j], swa_sinks[j], rel_bias)
            swa_kp.append(kp)
            swa_vp.append(vp)
            swa_ks.append(ks)
            swa_vs.append(vs)
        else:
            x, sp, ss = _gla_layer(x, state_gla[j], norm_mix[i], gla_w_in[j], gla_w_gk2[j], gla_b_gk[j],
                                   gla_norm[j], gla_w_o[j])
            gla_sp.append(sp)
            gla_ss.append(ss)
        final_w = norm_final if i == DEPTH - 1 else None
        x = _ffn(x, norm_ffn2[i], ffn2_w_gate[i].astype(BF16), ffn2_w_up[i].astype(BF16),
                 ffn2_w_down[i].astype(BF16), final_w)
    y_prompt = x[:N_PROMPT].reshape(BATCH, SEQ, D_MODEL)
    y_sample = x[N_PROMPT:].reshape(DEC_BATCH, DEC_SEQ, D_MODEL)
    return (y_prompt, y_sample, jnp.stack(swa_kp), jnp.stack(swa_vp), jnp.stack(swa_ks), jnp.stack(swa_vs),
            jnp.stack(gla_sp), jnp.stack(gla_ss))
```

```python
import functools
import math

import numpy as np
import jax
import jax.numpy as jnp
from jax import lax
from jax.experimental import pallas as pl
from jax.experimental.pallas import tpu as pltpu

F32 = jnp.float32
BF16 = jnp.bfloat16

D_MODEL = 2048
BATCH = 2
SEQ = 4096
DEPTH = 2
DEC_BATCH = 128
DEC_SEQ = 4
RMS_EPS = 1e-6
D_FF = 5632
N_HEADS = 32
N_KV_HEADS = 8
HEAD_DIM = 64
GROUP = N_HEADS // N_KV_HEADS
WINDOW = 128
NUM_BUCKETS = 32
MAX_DISTANCE = 128
NEG_INF = -1e30
GLA_HEADS = 4
GLA_DK = 256
GLA_DV = 512
GLA_KEY_DIM = GLA_HEADS * GLA_DK
GLA_VAL_DIM = GLA_HEADS * GLA_DV
GATE_RANK = 16
GATE_NORMALIZER = 16.0
GLA_MAIN_DIM = 2 * GLA_KEY_DIM + 2 * GLA_VAL_DIM
Q_DIM = N_HEADS * HEAD_DIM
KV_COLS = N_KV_HEADS * HEAD_DIM

N_PROMPT = BATCH * SEQ
N_SAMPLE = DEC_BATCH * DEC_SEQ
N_TOK = N_PROMPT + N_SAMPLE

LANES = 128
SUBLANES = 8
VMEM_LIMIT = 56 * 1024 * 1024

TM = 512
TF = 512
TN = 512
GLA_C = 64
NB = SEQ // WINDOW
N_PROMPT_TILES = N_PROMPT // TM


def _rms(x, w):
    return x * lax.rsqrt(jnp.mean(x * x, axis=-1, keepdims=True) + RMS_EPS) * w


def _silu(x):
    return x * jax.nn.sigmoid(x)


def _cparams(sem):
    return pltpu.CompilerParams(dimension_semantics=sem, vmem_limit_bytes=VMEM_LIMIT)


def _row_specs(split, width):
    if not split:
        return [pl.BlockSpec((TM, width), lambda i, *_: (i, 0))]
    return [pl.BlockSpec((TM, width), lambda i, *_: (jnp.minimum(i, N_PROMPT_TILES - 1), 0)),
            pl.BlockSpec((TM, width), lambda i, *_: (jnp.maximum(i - N_PROMPT_TILES, 0), 0))]


def _on_row_source(fn, *ref_groups):
    if all(len(g) == 1 for g in ref_groups):
        fn(*[g[0] for g in ref_groups])
        return
    i = pl.program_id(0)
    pl.when(i < N_PROMPT_TILES)(lambda: fn(*[g[0] for g in ref_groups]))
    pl.when(i >= N_PROMPT_TILES)(lambda: fn(*[g[-1] for g in ref_groups]))


def _ffn_body(n_in, n_out, final_norm, *refs):
    x_refs = refs[:n_in]
    nw_ref, wg_ref, wu_ref, wd_ref = refs[n_in:n_in + 4]
    rest = refs[n_in + 4:]
    if final_norm:
        fw_ref, rest = rest[0], rest[1:]
    o_refs = rest[:n_out]
    h_ref, acc_ref = rest[n_out:]
    j = pl.program_id(1)

    @pl.when(j == 0)
    def _():
        def start(x_ref):
            h_ref[...] = _rms(x_ref[...], nw_ref[...]).astype(BF16)
        _on_row_source(start, x_refs)
        acc_ref[...] = jnp.zeros_like(acc_ref)

    h = h_ref[...]
    g = jnp.dot(h, wg_ref[...], preferred_element_type=F32)
    u = jnp.dot(h, wu_ref[...], preferred_element_type=F32)
    a = (_silu(g) * u).astype(BF16)
    acc_ref[...] += jnp.dot(a, wd_ref[...], preferred_element_type=F32)

    @pl.when(j == pl.num_programs(1) - 1)
    def _():
        def finish(x_ref, o_ref):
            y = x_ref[...] + 0.5 * acc_ref[...]
            if final_norm:
                y = _rms(y, fw_ref[...])
            o_ref[...] = y
        _on_row_source(finish, x_refs, o_refs)


def _ffn(xs, nw, wg, wu, wd, final_w=None, split_out=False):
    final_norm = final_w is not None
    vec = pl.BlockSpec((1, D_MODEL), lambda i, j: (0, 0))
    in_specs = _row_specs(len(xs) == 2, D_MODEL) + [
        vec,
        pl.BlockSpec((D_MODEL, TF), lambda i, j: (0, j)),
        pl.BlockSpec((D_MODEL, TF), lambda i, j: (0, j)),
        pl.BlockSpec((TF, D_MODEL), lambda i, j: (j, 0)),
    ]
    args = list(xs) + [nw.reshape(1, D_MODEL), wg, wu, wd]
    if final_norm:
        in_specs.append(vec)
        args.append(final_w.reshape(1, D_MODEL))
    if split_out:
        out_shape = [jax.ShapeDtypeStruct((N_PROMPT, D_MODEL), F32), jax.ShapeDtypeStruct((N_SAMPLE, D_MODEL), F32)]
    else:
        out_shape = [jax.ShapeDtypeStruct((N_TOK, D_MODEL), F32)]
    return pl.pallas_call(
        functools.partial(_ffn_body, len(xs), len(out_shape), final_norm),
        grid=(N_TOK // TM, D_FF // TF),
        in_specs=in_specs,
        out_specs=_row_specs(split_out, D_MODEL),
        out_shape=out_shape,
        scratch_shapes=[pltpu.VMEM((TM, D_MODEL), BF16), pltpu.VMEM((TM, D_MODEL), F32)],
        compiler_params=_cparams(("arbitrary", "arbitrary")),
        name="ffn",
    )(*args)


def _norm_proj_body(segments, x_ref, nw_ref, w_ref, b_ref, *o_refs):
    h = _rms(x_ref[...], nw_ref[...]).astype(BF16)
    col = 0
    for (width, _), o_ref in zip(segments, o_refs):
        for c in range(0, width, TN):
            acc = jnp.dot(h, w_ref[:, col + c:col + c + TN], preferred_element_type=F32)
            o_ref[:, c:c + TN] = (acc + b_ref[:, col + c:col + c + TN]).astype(o_ref.dtype)
        col += width


def _norm_proj(x, nw, w, b, segments, name):
    n = w.shape[1]
    return pl.pallas_call(
        functools.partial(_norm_proj_body, segments),
        grid=(N_TOK // TM,),
        in_specs=[
            pl.BlockSpec((TM, D_MODEL), lambda i: (i, 0)),
            pl.BlockSpec((1, D_MODEL), lambda i: (0, 0)),
            pl.BlockSpec((D_MODEL, n), lambda i: (0, 0), pipeline_mode=pl.Buffered(1)),
            pl.BlockSpec((1, n), lambda i: (0, 0)),
        ],
        out_specs=[pl.BlockSpec((TM, width), lambda i: (i, 0)) for width, _ in segments],
        out_shape=[jax.ShapeDtypeStruct((N_TOK, width), dtype) for width, dtype in segments],
        compiler_params=_cparams(("arbitrary",)),
        name=name,
    )(x, nw.reshape(1, D_MODEL), w, b.reshape(1, n))


def _proj_res_body(ap_ref, as_ref, w_ref, b_ref, r_ref, o_ref):
    def run(a_ref):
        a = a_ref[...].astype(BF16)
        for c in range(0, D_MODEL, TN):
            acc = jnp.dot(a, w_ref[:, c:c + TN], preferred_element_type=F32)
            o_ref[:, c:c + TN] = r_ref[:, c:c + TN] + acc + b_ref[:, c:c + TN]
    _on_row_source(run, (ap_ref, as_ref))


def _proj_res(a_prompt, a_sample, w, b, res, name):
    k = w.shape[0]
    return pl.pallas_call(
        _proj_res_body,
        grid=(N_TOK // TM,),
        in_specs=_row_specs(True, k) + [
            pl.BlockSpec((k, D_MODEL), lambda i: (0, 0), pipeline_mode=pl.Buffered(1)),
            pl.BlockSpec((1, D_MODEL), lambda i: (0, 0)),
            pl.BlockSpec((TM, D_MODEL), lambda i: (i, 0)),
        ],
        out_specs=pl.BlockSpec((TM, D_MODEL), lambda i: (i, 0)),
        out_shape=jax.ShapeDtypeStruct((N_TOK, D_MODEL), F32),
        compiler_params=_cparams(("arbitrary",)),
        name=name,
    )(a_prompt, a_sample, w, b.reshape(1, D_MODEL), res)


def _t5_bucket_table():
    i = np.arange(WINDOW)[:, None]
    j = np.arange(2 * WINDOW)[None, :]
    n = np.maximum(WINDOW + i - j, 0)
    max_exact = NUM_BUCKETS // 2
    nf = np.maximum(n, 1).astype(np.float32)
    large = max_exact + (np.log(nf / np.float32(max_exact)) / np.float32(math.log(MAX_DISTANCE / max_exact))
                         * np.float32(NUM_BUCKETS - max_exact)).astype(np.int32)
    large = np.minimum(large, NUM_BUCKETS - 1)
    return np.where(n < max_exact, n, large).astype(np.int32)


def _bias_table_body(bucket_ref, rb_ref, o_ref):
    h = pl.program_id(0)
    bucket = bucket_ref[...]
    acc = jnp.zeros((WINDOW, 2 * WINDOW), F32)
    for b in range(NUM_BUCKETS):
        acc = jnp.where(bucket == b, rb_ref[b, h], acc)
    i = lax.broadcasted_iota(jnp.int32, (WINDOW, 2 * WINDOW), 0)
    j = lax.broadcasted_iota(jnp.int32, (WINDOW, 2 * WINDOW), 1)
    dist = WINDOW + i - j
    o_ref[0] = jnp.where((dist >= 0) & (dist < WINDOW), acc, NEG_INF)


def _bias_table(rel_bias):
    return pl.pallas_call(
        _bias_table_body,
        grid=(N_HEADS,),
        in_specs=[
            pl.BlockSpec((WINDOW, 2 * WINDOW), lambda h: (0, 0)),
            pl.BlockSpec(memory_space=pltpu.SMEM),
        ],
        out_specs=pl.BlockSpec((1, WINDOW, 2 * WINDOW), lambda h: (h, 0, 0)),
        out_shape=jax.ShapeDtypeStruct((N_HEADS, WINDOW, 2 * WINDOW), F32),
        name="bias_table",
    )(jnp.asarray(_t5_bucket_table()), rel_bias)


def _softmax_with_sink(s, sink_col):
    m = jnp.maximum(jnp.max(s, axis=-1, keepdims=True), sink_col)
    p = jnp.exp(s - m)
    denom = jnp.sum(p, axis=-1, keepdims=True) + jnp.exp(sink_col - m)
    return p, 1.0 / denom


KV_PER_STEP = LANES // HEAD_DIM
Q_COLS_PER_STEP = KV_PER_STEP * GROUP * HEAD_DIM


def _swa_prompt_body(sink_ref, q_ref, kp_ref, ko_ref, vp_ref, vo_ref, bias_ref, o_ref):
    blk = pl.program_id(1)
    pair = pl.program_id(2)
    rows = GROUP * WINDOW
    col = lax.broadcasted_iota(jnp.int32, (rows, 2 * WINDOW), 1)
    real_key = (blk > 0) | (col >= WINDOW)
    outs = []
    for hh in range(KV_PER_STEP):
        ks = slice(hh * HEAD_DIM, (hh + 1) * HEAD_DIM)
        k = jnp.concatenate([kp_ref[:, ks], ko_ref[:, ks]], axis=0).astype(BF16)
        v = jnp.concatenate([vp_ref[:, ks], vo_ref[:, ks]], axis=0).astype(BF16)
        q0 = hh * GROUP * HEAD_DIM
        qs = jnp.concatenate(
            [q_ref[:, q0 + g * HEAD_DIM:q0 + (g + 1) * HEAD_DIM] for g in range(GROUP)], axis=0)
        qs = qs * HEAD_DIM ** -0.5
        s = lax.dot_general(qs, k, (((1,), (1,)), ((), ())), preferred_element_type=F32)
        s = s + bias_ref[hh * GROUP:(hh + 1) * GROUP].reshape(rows, 2 * WINDOW)
        s = jnp.where(real_key, s, NEG_INF)
        head0 = (pair * KV_PER_STEP + hh) * GROUP
        sink_col = jnp.concatenate(
            [jnp.full((WINDOW, 1), sink_ref[head0 + g], F32) for g in range(GROUP)], axis=0)
        p, inv = _softmax_with_sink(s, sink_col)
        o = jnp.dot(p.astype(BF16), v, preferred_element_type=F32) * inv
        outs += [o[g * WINDOW:(g + 1) * WINDOW] for g in range(GROUP)]
    o_ref[...] = jnp.concatenate(outs, axis=1).astype(o_ref.dtype)


def _swa_prompt(q, kv, bias_tbl, sinks):
    v_col0 = KV_COLS // LANES

    def prev(b, i, p):
        return b * NB + jnp.maximum(i - 1, 0)

    return pl.pallas_call(
        _swa_prompt_body,
        grid=(BATCH, NB, N_KV_HEADS // KV_PER_STEP),
        in_specs=[
            pl.BlockSpec(memory_space=pltpu.SMEM),
            pl.BlockSpec((WINDOW, Q_COLS_PER_STEP), lambda b, i, p: (b * NB + i, p)),
            pl.BlockSpec((WINDOW, LANES), lambda b, i, p: (prev(b, i, p), p)),
            pl.BlockSpec((WINDOW, LANES), lambda b, i, p: (b * NB + i, p)),
            pl.BlockSpec((WINDOW, LANES), lambda b, i, p: (prev(b, i, p), v_col0 + p)),
            pl.BlockSpec((WINDOW, LANES), lambda b, i, p: (b * NB + i, v_col0 + p)),
            pl.BlockSpec((KV_PER_STEP * GROUP, WINDOW, 2 * WINDOW), lambda b, i, p: (p, 0, 0)),
        ],
        out_specs=pl.BlockSpec((WINDOW, Q_COLS_PER_STEP), lambda b, i, p: (b * NB + i, p)),
        out_shape=jax.ShapeDtypeStruct((N_PROMPT, Q_DIM), BF16),
        compiler_params=_cparams(("parallel", "parallel", "arbitrary")),
        name="swa_prompt",
    )(sinks, q, kv, kv, kv, kv, bias_tbl)


S_ROWS = N_KV_HEADS * GROUP * DEC_SEQ


def _swa_sample_body(q_ref, kn_ref, vn_ref, ck_ref, cv_ref, bias_ref, sink_ref,
                     o_ref, ko_ref, vo_ref):
    x = (q_ref[0] * HEAD_DIM ** -0.5).astype(BF16)
    xt = jnp.concatenate([x] * N_KV_HEADS, axis=1)
    row_kv = lax.broadcasted_iota(jnp.int32, (S_ROWS, KV_COLS), 0) // (GROUP * DEC_SEQ)
    col_kv = lax.broadcasted_iota(jnp.int32, (S_ROWS, KV_COLS), 1) // HEAD_DIM
    own = row_kv == col_kv
    qbd = jnp.where(own, xt, jnp.zeros_like(xt))
    pad = jnp.zeros((WINDOW - SUBLANES, KV_COLS), F32)
    kk = jnp.concatenate([ck_ref[0], kn_ref[0], pad], axis=0).astype(BF16)
    vv = jnp.concatenate([cv_ref[0], vn_ref[0], pad], axis=0).astype(BF16)
    s = lax.dot_general(qbd, kk, (((1,), (1,)), ((), ())), preferred_element_type=F32)
    s = s + bias_ref[...]
    p, inv = _softmax_with_sink(s, sink_ref[...])
    of = jnp.dot(p.astype(BF16), vv, preferred_element_type=F32)
    of = jnp.where(own, of, 0.0)
    o = of[:, 0:HEAD_DIM]
    for c in range(1, N_KV_HEADS):
        o = o + of[:, c * HEAD_DIM:(c + 1) * HEAD_DIM]
    o_ref[0] = o * inv
    keep = WINDOW - DEC_SEQ
    ko_ref[0, 0:keep, :] = ck_ref[0, DEC_SEQ:WINDOW, :]
    ko_ref[0, keep:WINDOW, :] = kn_ref[0, 0:DEC_SEQ, :]
    vo_ref[0, 0:keep, :] = cv_ref[0, DEC_SEQ:WINDOW, :]
    vo_ref[0, keep:WINDOW, :] = vn_ref[0, 0:DEC_SEQ, :]


def _swa_sample(q_rows, k_new8, v_new8, cache_k, cache_v, bias_s, sink_col):
    seq3 = lambda s: (s, 0, 0)
    full2 = lambda s: (0, 0)
    return pl.pallas_call(
        _swa_sample_body,
        grid=(DEC_BATCH,),
        in_specs=[
            pl.BlockSpec((1, S_ROWS, HEAD_DIM), seq3),
            pl.BlockSpec((1, SUBLANES, KV_COLS), seq3),
            pl.BlockSpec((1, SUBLANES, KV_COLS), seq3),
            pl.BlockSpec((1, WINDOW, KV_COLS), seq3),
            pl.BlockSpec((1, WINDOW, KV_COLS), seq3),
            pl.BlockSpec((S_ROWS, 2 * WINDOW), full2),
            pl.BlockSpec((S_ROWS, 1), full2),
        ],
        out_specs=[
            pl.BlockSpec((1, S_ROWS, HEAD_DIM), seq3),
            pl.BlockSpec((1, WINDOW, KV_COLS), seq3),
            pl.BlockSpec((1, WINDOW, KV_COLS), seq3),
        ],
        out_shape=[
            jax.ShapeDtypeStruct((DEC_BATCH, S_ROWS, HEAD_DIM), F32),
            jax.ShapeDtypeStruct((DEC_BATCH, WINDOW, KV_COLS), F32),
            jax.ShapeDtypeStruct((DEC_BATCH, WINDOW, KV_COLS), F32),
        ],
        compiler_params=_cparams(("parallel",)),
        name="swa_sample",
    )(q_rows, k_new8, v_new8, cache_k, cache_v, bias_s, sink_col)


def _log_sigmoid(x):
    return jnp.minimum(x, 0.0) - jnp.log1p(jnp.exp(-jnp.abs(x)))


def _gla_gate_body(x_ref, nw_ref, w1_ref, w2_ref, b_ref, o_ref):
    h = _rms(x_ref[...], nw_ref[...]).astype(BF16)
    gk = jnp.dot(h, w1_ref[...], preferred_element_type=F32)
    z = jnp.dot(gk.astype(BF16), w2_ref[...], preferred_element_type=F32) + b_ref[...]
    o_ref[...] = _log_sigmoid(z) / GATE_NORMALIZER


def _gla_gate(x, nw, w1p, w2p, b):
    m = x.shape[0]
    return pl.pallas_call(
        _gla_gate_body,
        grid=(m // TM,),
        in_specs=[
            pl.BlockSpec((TM, D_MODEL), lambda i: (i, 0)),
            pl.BlockSpec((1, D_MODEL), lambda i: (0, 0)),
            pl.BlockSpec((D_MODEL, LANES), lambda i: (0, 0)),
            pl.BlockSpec((LANES, GLA_KEY_DIM), lambda i: (0, 0)),
            pl.BlockSpec((1, GLA_KEY_DIM), lambda i: (0, 0)),
        ],
        out_specs=pl.BlockSpec((TM, GLA_KEY_DIM), lambda i: (i, 0)),
        out_shape=jax.ShapeDtypeStruct((m, GLA_KEY_DIM), F32),
        compiler_params=_cparams(("parallel",)),
        name="gla_gate",
    )(x, nw.reshape(1, D_MODEL), w1p, w2p, b.reshape(1, GLA_KEY_DIM))


def _gla_out(o, gate, norm_w):
    return _rms(o, norm_w) * _silu(gate)


def _cumsum_rows(g):
    c = g.shape[0]
    tri = (lax.broadcasted_iota(jnp.int32, (c, c), 0) >= lax.broadcasted_iota(jnp.int32, (c, c), 1)).astype(F32)
    return jnp.dot(tri, g, precision=lax.Precision.HIGHEST, preferred_element_type=F32)


def _causal(a):
    c = a.shape[0]
    keep = lax.broadcasted_iota(jnp.int32, (c, c), 0) >= lax.broadcasted_iota(jnp.int32, (c, c), 1)
    return jnp.where(keep, a, 0.0)


_NT = (((1,), (1,)), ((), ()))
_TN = (((0,), (0,)), ((), ()))


def _gla_prompt_body(q_ref, k_ref, v_ref, gate_ref, la_ref, nw_ref, o_ref, s_ref, st_ref):
    c = pl.program_id(2)

    @pl.when(c == 0)
    def _():
        st_ref[...] = jnp.zeros_like(st_ref)

    b = _cumsum_rows(la_ref[...])
    k = k_ref[...].astype(F32)
    v = v_ref[...]
    qe = (q_ref[...].astype(F32) * GLA_DK ** -0.5 * jnp.exp(b)).astype(BF16)
    ke = (k * jnp.exp(-b)).astype(BF16)
    a = _causal(lax.dot_general(qe, ke, _NT, preferred_element_type=F32))
    st = st_ref[...]
    o = (jnp.dot(a.astype(BF16), v, preferred_element_type=F32)
         + lax.dot_general(qe, st.astype(BF16), _NT, preferred_element_type=F32))
    b_last = b[GLA_C - 1:GLA_C, :]
    kd = (k * jnp.exp(b_last - b)).astype(BF16)
    st_new = st * jnp.exp(b_last) + lax.dot_general(v, kd, _TN, preferred_element_type=F32)
    st_ref[...] = st_new
    o_ref[...] = _gla_out(o, gate_ref[...].astype(F32), nw_ref[...]).astype(o_ref.dtype)

    @pl.when(c == pl.num_programs(2) - 1)
    def _():
        s_ref[0, 0] = st_new.T


def _gla_prompt(proj, log_a, norm_w):
    nc = SEQ // GLA_C
    kb = GLA_KEY_DIM // GLA_DK
    vb = 2 * GLA_KEY_DIM // GLA_DV
    gb = vb + GLA_VAL_DIM // GLA_DV
    row = lambda b, h, c: b * nc + c
    return pl.pallas_call(
        _gla_prompt_body,
        grid=(BATCH, GLA_HEADS, nc),
        in_specs=[
            pl.BlockSpec((GLA_C, GLA_DK), lambda b, h, c: (row(b, h, c), h)),
            pl.BlockSpec((GLA_C, GLA_DK), lambda b, h, c: (row(b, h, c), kb + h)),
            pl.BlockSpec((GLA_C, GLA_DV), lambda b, h, c: (row(b, h, c), vb + h)),
            pl.BlockSpec((GLA_C, GLA_DV), lambda b, h, c: (row(b, h, c), gb + h)),
            pl.BlockSpec((GLA_C, GLA_DK), lambda b, h, c: (row(b, h, c), h)),
            pl.BlockSpec((1, GLA_DV), lambda b, h, c: (0, 0)),
        ],
        out_specs=[
            pl.BlockSpec((GLA_C, GLA_DV), lambda b, h, c: (row(b, h, c), h)),
            pl.BlockSpec((1, 1, GLA_DK, GLA_DV), lambda b, h, c: (b, h, 0, 0)),
        ],
        out_shape=[
            jax.ShapeDtypeStruct((N_PROMPT, GLA_VAL_DIM), BF16),
            jax.ShapeDtypeStruct((BATCH, GLA_HEADS, GLA_DK, GLA_DV), F32),
        ],
        scratch_shapes=[pltpu.VMEM((GLA_DV, GLA_DK), F32)],
        compiler_params=_cparams(("parallel", "parallel", "arbitrary")),
        name="gla_prompt",
    )(proj, proj, proj, proj, log_a, norm_w.reshape(1, GLA_DV))


def _gla_sample_body(proj_ref, la_ref, s0_ref, nw_ref, o_ref, s_ref):
    ones = jnp.ones((DEC_SEQ, LANES), F32)
    for h in range(GLA_HEADS):
        q = proj_ref[0, :, h * GLA_DK:(h + 1) * GLA_DK]
        k = proj_ref[0, :, GLA_KEY_DIM + h * GLA_DK:GLA_KEY_DIM + (h + 1) * GLA_DK]
        v0 = 2 * GLA_KEY_DIM + h * GLA_DV
        v = proj_ref[0, :, v0:v0 + GLA_DV].astype(BF16)
        gate = proj_ref[0, :, v0 + GLA_VAL_DIM:v0 + GLA_VAL_DIM + GLA_DV]
        g = la_ref[0, :, h * GLA_DK:(h + 1) * GLA_DK]
        b = _cumsum_rows(g)
        qe = (q * GLA_DK ** -0.5 * jnp.exp(b)).astype(BF16)
        ke = (k * jnp.exp(-b)).astype(BF16)
        a = _causal(lax.dot_general(qe, ke, _NT, preferred_element_type=F32))
        s0 = s0_ref[0, h]
        o = (jnp.dot(a.astype(BF16), v, preferred_element_type=F32)
             + jnp.dot(qe, s0.astype(BF16), preferred_element_type=F32))
        b_last = b[DEC_SEQ - 1:DEC_SEQ, :]
        kd = (k * jnp.exp(b_last - b)).astype(BF16)
        dcol = jnp.exp(lax.dot_general(g, ones, _TN, precision=lax.Precision.HIGHEST,
                                       preferred_element_type=F32))
        decay = jnp.concatenate([dcol] * (GLA_DV // LANES), axis=1)
        s_ref[0, h] = s0 * decay + lax.dot_general(kd, v, _TN, preferred_element_type=F32)
        o_ref[0, :, h * GLA_DV:(h + 1) * GLA_DV] = _gla_out(o, gate, nw_ref[...])


def _gla_sample(proj3, log_a3, state, norm_w):
    seq3 = lambda s: (s, 0, 0)
    seq4 = lambda s: (s, 0, 0, 0)
    return pl.pallas_call(
        _gla_sample_body,
        grid=(DEC_BATCH,),
        in_specs=[
            pl.BlockSpec((1, DEC_SEQ, GLA_MAIN_DIM), seq3),
            pl.BlockSpec((1, DEC_SEQ, GLA_KEY_DIM), seq3),
            pl.BlockSpec((1, GLA_HEADS, GLA_DK, GLA_DV), seq4),
            pl.BlockSpec((1, GLA_DV), lambda s: (0, 0)),
        ],
        out_specs=[
            pl.BlockSpec((1, DEC_SEQ, GLA_VAL_DIM), seq3),
            pl.BlockSpec((1, GLA_HEADS, GLA_DK, GLA_DV), seq4),
        ],
        out_shape=[
            jax.ShapeDtypeStruct((DEC_BATCH, DEC_SEQ, GLA_VAL_DIM), F32),
            jax.ShapeDtypeStruct((DEC_BATCH, GLA_HEADS, GLA_DK, GLA_DV), F32),
        ],
        compiler_params=_cparams(("parallel",)),
        name="gla_sample",
    )(proj3, log_a3, state, norm_w.reshape(1, GLA_DV))


def _swa_layer(x, cache_k, cache_v, norm_w, w_qkv, b_qkv, w_o, b_o, sinks, rel_bias):
    q, kv = _norm_proj(x, norm_w, w_qkv.astype(BF16), b_qkv, ((Q_DIM, BF16), (2 * KV_COLS, F32)), "swa_qkv")
    bias_tbl = _bias_table(rel_bias)
    o_p = _swa_prompt(q, kv, bias_tbl, sinks)

    q_rows = (q[N_PROMPT:].astype(F32).reshape(DEC_BATCH, DEC_SEQ, N_KV_HEADS, GROUP, HEAD_DIM)
              .transpose(0, 2, 3, 1, 4).reshape(DEC_BATCH, S_ROWS, HEAD_DIM))
    kv_s = kv[N_PROMPT:].reshape(DEC_BATCH, DEC_SEQ, 2 * KV_COLS)
    pad8 = ((0, 0), (0, SUBLANES - DEC_SEQ), (0, 0))
    k_new8 = jnp.pad(kv_s[..., :KV_COLS], pad8)
    v_new8 = jnp.pad(kv_s[..., KV_COLS:], pad8)
    bias_s = bias_tbl[:, :DEC_SEQ, :].reshape(S_ROWS, 2 * WINDOW)
    sink_col = jnp.repeat(sinks, DEC_SEQ).reshape(S_ROWS, 1)
    o_s, k_s, v_s = _swa_sample(q_rows, k_new8, v_new8,
                                cache_k.reshape(DEC_BATCH, WINDOW, KV_COLS),
                                cache_v.reshape(DEC_BATCH, WINDOW, KV_COLS), bias_s, sink_col)
    o_s = (o_s.reshape(DEC_BATCH, N_KV_HEADS, GROUP, DEC_SEQ, HEAD_DIM)
           .transpose(0, 3, 1, 2, 4).reshape(N_SAMPLE, Q_DIM))
    x = _proj_res(o_p, o_s, w_o.astype(BF16), b_o, x, "swa_out")

    kv_p = kv[:N_PROMPT].reshape(BATCH, SEQ, 2 * KV_COLS)[:, SEQ - WINDOW:]
    k_p = kv_p[..., :KV_COLS].reshape(BATCH, WINDOW, N_KV_HEADS, HEAD_DIM)
    v_p = kv_p[..., KV_COLS:].reshape(BATCH, WINDOW, N_KV_HEADS, HEAD_DIM)
    shape_s = (DEC_BATCH, WINDOW, N_KV_HEADS, HEAD_DIM)
    return x, k_p, v_p, k_s.reshape(shape_s), v_s.reshape(shape_s)


def _gla_layer(x, state, norm_w, w_in, w_gk2, b_gk, gnorm, w_o):
    w_main = w_in[:, :GLA_MAIN_DIM].astype(BF16)
    w1p = jnp.pad(w_in[:, GLA_MAIN_DIM:], ((0, 0), (0, LANES - GATE_RANK))).astype(BF16)
    w2p = jnp.pad(w_gk2, ((0, LANES - GATE_RANK), (0, 0))).astype(BF16)
    (proj,) = _norm_proj(x, norm_w, w_main, jnp.zeros((GLA_MAIN_DIM,), F32), ((GLA_MAIN_DIM, BF16),), "gla_in")
    log_a = _gla_gate(x, norm_w, w1p, w2p, b_gk)
    o_p, s_p = _gla_prompt(proj, log_a, gnorm)
    proj_s = proj[N_PROMPT:].astype(F32).reshape(DEC_BATCH, DEC_SEQ, GLA_MAIN_DIM)
    log_a_s = log_a[N_PROMPT:].reshape(DEC_BATCH, DEC_SEQ, GLA_KEY_DIM)
    o_s, s_s = _gla_sample(proj_s, log_a_s, state, gnorm)
    x = _proj_res(o_p, o_s.reshape(N_SAMPLE, GLA_VAL_DIM), w_o.astype(BF16), jnp.zeros((D_MODEL,), F32), x,
                  "gla_out")
    return x, s_p, s_s


def kernel(x_prompt, x_sample, cache_swa_k, cache_swa_v, state_gla, norm_ffn1, ffn1_w_gate, ffn1_w_up,
           ffn1_w_down, norm_mix, norm_ffn2, ffn2_w_gate, ffn2_w_up, ffn2_w_down, norm_final, rel_bias,
           swa_w_qkv, swa_b_qkv, swa_w_o, swa_b_o, swa_sinks, gla_w_in, gla_w_gk2, gla_b_gk, gla_norm,
           gla_w_o):
    xs = [x_prompt.reshape(N_PROMPT, D_MODEL), x_sample.reshape(N_SAMPLE, D_MODEL)]
    swa_kp, swa_vp, swa_ks, swa_vs, gla_sp, gla_ss = [], [], [], [], [], []
    for i in range(DEPTH):
        (x,) = _ffn(xs, norm_ffn1[i], ffn1_w_gate[i].astype(BF16), ffn1_w_up[i].astype(BF16),
                    ffn1_w_down[i].astype(BF16))
        j = i // 2
        if i % 2 == 0:
            x, kp, vp, ks, vs = _swa_layer(x, cache_swa_k[j], cache_swa_v[j], norm_mix[i], swa_w_qkv[j],
                                           swa_b_qkv[j], swa_w_o[j], swa_b_o[j], swa_sinks[j], rel_bias)
            swa_kp.append(kp)
            swa_vp.append(vp)
            swa_ks.append(ks)
            swa_vs.append(vs)
        else:
            x, sp, ss = _gla_layer(x, state_gla[j], norm_mix[i], gla_w_in[j], gla_w_gk2[j], gla_b_gk[j],
                                   gla_norm[j], gla_w_o[j])
            gla_sp.append(sp)
            gla_ss.append(ss)
        last = i == DEPTH - 1
        xs = _ffn([x], norm_ffn2[i], ffn2_w_gate[i].astype(BF16), ffn2_w_up[i].astype(BF16),
                  ffn2_w_down[i].astype(BF16), norm_final if last else None, split_out=last)
    y_prompt = xs[0].reshape(BATCH, SEQ, D_MODEL)
    y_sample = xs[1].reshape(DEC_BATCH, DEC_SEQ, D_MODEL)
    return (y_prompt, y_sample, jnp.stack(swa_kp), jnp.stack(swa_vp), jnp.stack(swa_ks), jnp.stack(swa_vs),
            jnp.stack(gla_sp), jnp.stack(gla_ss))
```

```python
import functools
import math

import numpy as np
import jax
import jax.numpy as jnp
from jax import lax
from jax.experimental import pallas as pl
from jax.experimental.pallas import tpu as pltpu

F32 = jnp.float32
BF16 = jnp.bfloat16

D_MODEL = 2048
BATCH = 2
SEQ = 4096
DEPTH = 2
DEC_BATCH = 128
DEC_SEQ = 4
RMS_EPS = 1e-6
D_FF = 5632
N_HEADS = 32
N_KV_HEADS = 8
HEAD_DIM = 64
GROUP = N_HEADS // N_KV_HEADS
WINDOW = 128
NUM_BUCKETS = 32
MAX_DISTANCE = 128
NEG_INF = -1e30
GLA_HEADS = 4
GLA_DK = 256
GLA_DV = 512
GLA_KEY_DIM = GLA_HEADS * GLA_DK
GLA_VAL_DIM = GLA_HEADS * GLA_DV
GATE_RANK = 16
GATE_NORMALIZER = 16.0
GLA_MAIN_DIM = 2 * GLA_KEY_DIM + 2 * GLA_VAL_DIM
Q_DIM = N_HEADS * HEAD_DIM
KV_COLS = N_KV_HEADS * HEAD_DIM

N_PROMPT = BATCH * SEQ
N_SAMPLE = DEC_BATCH * DEC_SEQ
N_TOK = N_PROMPT + N_SAMPLE

LANES = 128
SUBLANES = 8
VMEM_LIMIT = 56 * 1024 * 1024

TM = 512
TF = 512
TN = 512
GLA_C = 64
NB = SEQ // WINDOW
N_PROMPT_TILES = N_PROMPT // TM


def _rms(x, w):
    return x * lax.rsqrt(jnp.mean(x * x, axis=-1, keepdims=True) + RMS_EPS) * w


def _silu(x):
    return x * jax.nn.sigmoid(x)


def _cparams(sem):
    return pltpu.CompilerParams(dimension_semantics=sem, vmem_limit_bytes=VMEM_LIMIT)


def _row_specs(split, width):
    if not split:
        return [pl.BlockSpec((TM, width), lambda i, *_: (i, 0))]
    return [pl.BlockSpec((TM, width), lambda i, *_: (jnp.minimum(i, N_PROMPT_TILES - 1), 0)),
            pl.BlockSpec((TM, width), lambda i, *_: (jnp.maximum(i - N_PROMPT_TILES, 0), 0))]


def _on_row_source(fn, *ref_groups):
    if all(len(g) == 1 for g in ref_groups):
        fn(*[g[0] for g in ref_groups])
        return
    i = pl.program_id(0)
    pl.when(i < N_PROMPT_TILES)(lambda: fn(*[g[0] for g in ref_groups]))
    pl.when(i >= N_PROMPT_TILES)(lambda: fn(*[g[-1] for g in ref_groups]))


def _ffn_body(n_in, n_out, final_norm, *refs):
    x_refs = refs[:n_in]
    nw_ref, wg_ref, wu_ref, wd_ref = refs[n_in:n_in + 4]
    rest = refs[n_in + 4:]
    if final_norm:
        fw_ref, rest = rest[0], rest[1:]
    o_refs = rest[:n_out]
    h_ref, acc_ref = rest[n_out:]
    j = pl.program_id(1)

    @pl.when(j == 0)
    def _():
        def start(x_ref):
            h_ref[...] = _rms(x_ref[...], nw_ref[...]).astype(BF16)
        _on_row_source(start, x_refs)
        acc_ref[...] = jnp.zeros_like(acc_ref)

    h = h_ref[...]
    g = jnp.dot(h, wg_ref[...], preferred_element_type=F32)
    u = jnp.dot(h, wu_ref[...], preferred_element_type=F32)
    a = (_silu(g) * u).astype(BF16)
    acc_ref[...] += jnp.dot(a, wd_ref[...], preferred_element_type=F32)

    @pl.when(j == pl.num_programs(1) - 1)
    def _():
        def finish(x_ref, o_ref):
            y = x_ref[...] + 0.5 * acc_ref[...]
            if final_norm:
                y = _rms(y, fw_ref[...])
            o_ref[...] = y
        _on_row_source(finish, x_refs, o_refs)


def _ffn(xs, nw, wg, wu, wd, final_w=None, split_out=False):
    final_norm = final_w is not None
    vec = pl.BlockSpec((1, D_MODEL), lambda i, j: (0, 0))
    in_specs = _row_specs(len(xs) == 2, D_MODEL) + [
        vec,
        pl.BlockSpec((D_MODEL, TF), lambda i, j: (0, j)),
        pl.BlockSpec((D_MODEL, TF), lambda i, j: (0, j)),
        pl.BlockSpec((TF, D_MODEL), lambda i, j: (j, 0)),
    ]
    args = list(xs) + [nw.reshape(1, D_MODEL), wg, wu, wd]
    if final_norm:
        in_specs.append(vec)
        args.append(final_w.reshape(1, D_MODEL))
    if split_out:
        out_shape = [jax.ShapeDtypeStruct((N_PROMPT, D_MODEL), F32), jax.ShapeDtypeStruct((N_SAMPLE, D_MODEL), F32)]
    else:
        out_shape = [jax.ShapeDtypeStruct((N_TOK, D_MODEL), F32)]
    return pl.pallas_call(
        functools.partial(_ffn_body, len(xs), len(out_shape), final_norm),
        grid=(N_TOK // TM, D_FF // TF),
        in_specs=in_specs,
        out_specs=_row_specs(split_out, D_MODEL),
        out_shape=out_shape,
        scratch_shapes=[pltpu.VMEM((TM, D_MODEL), BF16), pltpu.VMEM((TM, D_MODEL), F32)],
        compiler_params=_cparams(("arbitrary", "arbitrary")),
        name="ffn",
    )(*args)


def _norm_proj_body(segments, x_ref, nw_ref, w_ref, b_ref, *o_refs):
    h = _rms(x_ref[...], nw_ref[...]).astype(BF16)
    col = 0
    for (width, _), o_ref in zip(segments, o_refs):
        for c in range(0, width, TN):
            acc = jnp.dot(h, w_ref[:, col + c:col + c + TN], preferred_element_type=F32)
            o_ref[:, c:c + TN] = (acc + b_ref[:, col + c:col + c + TN]).astype(o_ref.dtype)
        col += width


def _norm_proj(x, nw, w, b, segments, name):
    n = w.shape[1]
    return pl.pallas_call(
        functools.partial(_norm_proj_body, segments),
        grid=(N_TOK // TM,),
        in_specs=[
            pl.BlockSpec((TM, D_MODEL), lambda i: (i, 0)),
            pl.BlockSpec((1, D_MODEL), lambda i: (0, 0)),
            pl.BlockSpec((D_MODEL, n), lambda i: (0, 0), pipeline_mode=pl.Buffered(1)),
            pl.BlockSpec((1, n), lambda i: (0, 0)),
        ],
        out_specs=[pl.BlockSpec((TM, width), lambda i: (i, 0)) for width, _ in segments],
        out_shape=[jax.ShapeDtypeStruct((N_TOK, width), dtype) for width, dtype in segments],
        compiler_params=_cparams(("arbitrary",)),
        name=name,
    )(x, nw.reshape(1, D_MODEL), w, b.reshape(1, n))


def _proj_res_body(ap_ref, as_ref, w_ref, b_ref, r_ref, o_ref):
    def run(a_ref):
        a = a_ref[...].astype(BF16)
        for c in range(0, D_MODEL, TN):
            acc = jnp.dot(a, w_ref[:, c:c + TN], preferred_element_type=F32)
            o_ref[:, c:c + TN] = r_ref[:, c:c + TN] + acc + b_ref[:, c:c + TN]
    _on_row_source(run, (ap_ref, as_ref))


def _proj_res(a_prompt, a_sample, w, b, res, name):
    k = w.shape[0]
    return pl.pallas_call(
        _proj_res_body,
        grid=(N_TOK // TM,),
        in_specs=_row_specs(True, k) + [
            pl.BlockSpec((k, D_MODEL), lambda i: (0, 0), pipeline_mode=pl.Buffered(1)),
            pl.BlockSpec((1, D_MODEL), lambda i: (0, 0)),
            pl.BlockSpec((TM, D_MODEL), lambda i: (i, 0)),
        ],
        out_specs=pl.BlockSpec((TM, D_MODEL), lambda i: (i, 0)),
        out_shape=jax.ShapeDtypeStruct((N_TOK, D_MODEL), F32),
        compiler_params=_cparams(("arbitrary",)),
        name=name,
    )(a_prompt, a_sample, w, b.reshape(1, D_MODEL), res)


def _t5_bucket_table():
    i = np.arange(WINDOW)[None, :]
    j = np.arange(2 * WINDOW)[:, None]
    n = np.maximum(WINDOW + i - j, 0)
    max_exact = NUM_BUCKETS // 2
    nf = np.maximum(n, 1).astype(np.float32)
    large = max_exact + (np.log(nf / np.float32(max_exact)) / np.float32(math.log(MAX_DISTANCE / max_exact))
                         * np.float32(NUM_BUCKETS - max_exact)).astype(np.int32)
    large = np.minimum(large, NUM_BUCKETS - 1)
    return np.where(n < max_exact, n, large).astype(np.int32)


KV_PER_STEP = LANES // HEAD_DIM
N_PAIRS = N_KV_HEADS // KV_PER_STEP
Q_COLS_PER_STEP = KV_PER_STEP * GROUP * HEAD_DIM


def _slot_head(slot):
    pair = slot // (GROUP * KV_PER_STEP)
    g = (slot // KV_PER_STEP) % GROUP
    hh = slot % KV_PER_STEP
    return (pair * KV_PER_STEP + hh) * GROUP + g


def _bias_table_body(bucket_ref, rb_ref, o_ref):
    h = _slot_head(pl.program_id(0))
    bucket = bucket_ref[...]
    acc = jnp.zeros((2 * WINDOW, WINDOW), F32)
    for b in range(NUM_BUCKETS):
        acc = jnp.where(bucket == b, rb_ref[b, h], acc)
    j = lax.broadcasted_iota(jnp.int32, (2 * WINDOW, WINDOW), 0)
    i = lax.broadcasted_iota(jnp.int32, (2 * WINDOW, WINDOW), 1)
    dist = WINDOW + i - j
    o_ref[0] = jnp.where((dist >= 0) & (dist < WINDOW), acc, NEG_INF)


def _bias_table(rel_bias):
    return pl.pallas_call(
        _bias_table_body,
        grid=(N_HEADS,),
        in_specs=[
            pl.BlockSpec((2 * WINDOW, WINDOW), lambda h: (0, 0)),
            pl.BlockSpec(memory_space=pltpu.SMEM),
        ],
        out_specs=pl.BlockSpec((1, 2 * WINDOW, WINDOW), lambda h: (h, 0, 0)),
        out_shape=jax.ShapeDtypeStruct((N_HEADS, 2 * WINDOW, WINDOW), F32),
        name="bias_table",
    )(jnp.asarray(_t5_bucket_table()), rel_bias)


def _softmax_with_sink(s, sink_col):
    m = jnp.maximum(jnp.max(s, axis=-1, keepdims=True), sink_col)
    p = jnp.exp(s - m)
    denom = jnp.sum(p, axis=-1, keepdims=True) + jnp.exp(sink_col - m)
    return p, 1.0 / denom


PAIR_SLOTS = GROUP * KV_PER_STEP
PAIR_COLS = PAIR_SLOTS * WINDOW
ONES_ROWS = 16


def _swa_prompt_body(sink_ref, q_ref, kp_ref, ko_ref, vp_ref, vo_ref, bias_ref, o_ref):
    pair = pl.program_id(0)
    blk = pl.program_id(2)
    head_a = lax.broadcasted_iota(jnp.int32, (WINDOW, LANES), 1) < HEAD_DIM
    k = jnp.concatenate([kp_ref[...], ko_ref[...]], axis=0).astype(BF16)
    v = jnp.concatenate([vp_ref[...], vo_ref[...]], axis=0)
    vt = jnp.concatenate([v.T, jnp.ones((ONES_ROWS, 2 * WINDOW), F32)], axis=0).astype(BF16)
    parts = []
    for g in range(GROUP):
        qg = q_ref[:, g * LANES:(g + 1) * LANES] * HEAD_DIM ** -0.5
        zero = jnp.zeros_like(qg)
        parts += [jnp.where(head_a, qg, zero), jnp.where(head_a, zero, qg)]
    qbd = jnp.concatenate(parts, axis=0)
    st = lax.dot_general(k, qbd, (((1,), (1,)), ((), ())), preferred_element_type=F32)
    st = st + jnp.concatenate([bias_ref[t] for t in range(PAIR_SLOTS)], axis=1)
    no_prev = jnp.where(blk > 0, 0.0, NEG_INF)
    st = jnp.concatenate([st[:WINDOW] + no_prev, st[WINDOW:]], axis=0)
    sink_row = jnp.concatenate(
        [jnp.full((1, WINDOW), sink_ref[_slot_head(pair * PAIR_SLOTS + slot)], F32)
         for slot in range(PAIR_SLOTS)], axis=1)
    m = jnp.maximum(jnp.max(st, axis=0, keepdims=True), sink_row)
    pt = jnp.exp(st - m).astype(BF16)
    oa = jnp.dot(vt, pt, preferred_element_type=F32)
    inv = 1.0 / (oa[LANES:LANES + 1] + jnp.exp(sink_row - m))
    o = oa[:LANES] * inv
    for g in range(GROUP):
        c = g * KV_PER_STEP * WINDOW
        ot = jnp.concatenate([o[:HEAD_DIM, c:c + WINDOW], o[HEAD_DIM:, c + WINDOW:c + 2 * WINDOW]], axis=0)
        o_ref[:, g * LANES:(g + 1) * LANES] = ot.T.astype(o_ref.dtype)


def _swa_prompt(q, kv, bias_tbl, sinks):
    v_col0 = KV_COLS // LANES

    def prev(p, b, i):
        return b * NB + jnp.maximum(i - 1, 0)

    return pl.pallas_call(
        _swa_prompt_body,
        grid=(N_PAIRS, BATCH, NB),
        in_specs=[
            pl.BlockSpec(memory_space=pltpu.SMEM),
            pl.BlockSpec((WINDOW, Q_COLS_PER_STEP), lambda p, b, i: (b * NB + i, p)),
            pl.BlockSpec((WINDOW, LANES), lambda p, b, i: (prev(p, b, i), p)),
            pl.BlockSpec((WINDOW, LANES), lambda p, b, i: (b * NB + i, p)),
            pl.BlockSpec((WINDOW, LANES), lambda p, b, i: (prev(p, b, i), v_col0 + p)),
            pl.BlockSpec((WINDOW, LANES), lambda p, b, i: (b * NB + i, v_col0 + p)),
            pl.BlockSpec((PAIR_SLOTS, 2 * WINDOW, WINDOW), lambda p, b, i: (p, 0, 0)),
        ],
        out_specs=pl.BlockSpec((WINDOW, Q_COLS_PER_STEP), lambda p, b, i: (b * NB + i, p)),
        out_shape=jax.ShapeDtypeStruct((N_PROMPT, Q_DIM), BF16),
        compiler_params=_cparams(("arbitrary", "arbitrary", "arbitrary")),
        name="swa_prompt",
    )(sinks, q, kv, kv, kv, kv, bias_tbl)


S_ROWS = N_KV_HEADS * GROUP * DEC_SEQ


def _swa_sample_body(q_ref, kn_ref, vn_ref, ck_ref, cv_ref, bias_ref, sink_ref,
                     o_ref, ko_ref, vo_ref):
    x = (q_ref[0] * HEAD_DIM ** -0.5).astype(BF16)
    xt = jnp.concatenate([x] * N_KV_HEADS, axis=1)
    row_kv = lax.broadcasted_iota(jnp.int32, (S_ROWS, KV_COLS), 0) // (GROUP * DEC_SEQ)
    col_kv = lax.broadcasted_iota(jnp.int32, (S_ROWS, KV_COLS), 1) // HEAD_DIM
    own = row_kv == col_kv
    qbd = jnp.where(own, xt, jnp.zeros_like(xt))
    pad = jnp.zeros((WINDOW - SUBLANES, KV_COLS), F32)
    kk = jnp.concatenate([ck_ref[0], kn_ref[0], pad], axis=0).astype(BF16)
    vv = jnp.concatenate([cv_ref[0], vn_ref[0], pad], axis=0).astype(BF16)
    s = lax.dot_general(qbd, kk, (((1,), (1,)), ((), ())), preferred_element_type=F32)
    s = s + bias_ref[...]
    p, inv = _softmax_with_sink(s, sink_ref[...])
    of = jnp.dot(p.astype(BF16), vv, preferred_element_type=F32)
    of = jnp.where(own, of, 0.0)
    o = of[:, 0:HEAD_DIM]
    for c in range(1, N_KV_HEADS):
        o = o + of[:, c * HEAD_DIM:(c + 1) * HEAD_DIM]
    o_ref[0] = o * inv
    keep = WINDOW - DEC_SEQ
    ko_ref[0, 0:keep, :] = ck_ref[0, DEC_SEQ:WINDOW, :]
    ko_ref[0, keep:WINDOW, :] = kn_ref[0, 0:DEC_SEQ, :]
    vo_ref[0, 0:keep, :] = cv_ref[0, DEC_SEQ:WINDOW, :]
    vo_ref[0, keep:WINDOW, :] = vn_ref[0, 0:DEC_SEQ, :]


def _swa_sample(q_rows, k_new8, v_new8, cache_k, cache_v, bias_s, sink_col):
    seq3 = lambda s: (s, 0, 0)
    full2 = lambda s: (0, 0)
    return pl.pallas_call(
        _swa_sample_body,
        grid=(DEC_BATCH,),
        in_specs=[
            pl.BlockSpec((1, S_ROWS, HEAD_DIM), seq3),
            pl.BlockSpec((1, SUBLANES, KV_COLS), seq3),
            pl.BlockSpec((1, SUBLANES, KV_COLS), seq3),
            pl.BlockSpec((1, WINDOW, KV_COLS), seq3),
            pl.BlockSpec((1, WINDOW, KV_COLS), seq3),
            pl.BlockSpec((S_ROWS, 2 * WINDOW), full2),
            pl.BlockSpec((S_ROWS, 1), full2),
        ],
        out_specs=[
            pl.BlockSpec((1, S_ROWS, HEAD_DIM), seq3),
            pl.BlockSpec((1, WINDOW, KV_COLS), seq3),
            pl.BlockSpec((1, WINDOW, KV_COLS), seq3),
        ],
        out_shape=[
            jax.ShapeDtypeStruct((DEC_BATCH, S_ROWS, HEAD_DIM), F32),
            jax.ShapeDtypeStruct((DEC_BATCH, WINDOW, KV_COLS), F32),
            jax.ShapeDtypeStruct((DEC_BATCH, WINDOW, KV_COLS), F32),
        ],
        compiler_params=_cparams(("parallel",)),
        name="swa_sample",
    )(q_rows, k_new8, v_new8, cache_k, cache_v, bias_s, sink_col)


def _log_sigmoid(x):
    return jnp.minimum(x, 0.0) - jnp.log1p(jnp.exp(-jnp.abs(x)))


def _gla_gate_body(x_ref, nw_ref, w1_ref, w2_ref, b_ref, o_ref):
    h = _rms(x_ref[...], nw_ref[...]).astype(BF16)
    gk = jnp.dot(h, w1_ref[...], preferred_element_type=F32)
    z = jnp.dot(gk.astype(BF16), w2_ref[...], preferred_element_type=F32) + b_ref[...]
    o_ref[...] = _log_sigmoid(z) / GATE_NORMALIZER


def _gla_gate(x, nw, w1p, w2p, b):
    m = x.shape[0]
    return pl.pallas_call(
        _gla_gate_body,
        grid=(m // TM,),
        in_specs=[
            pl.BlockSpec((TM, D_MODEL), lambda i: (i, 0)),
            pl.BlockSpec((1, D_MODEL), lambda i: (0, 0)),
            pl.BlockSpec((D_MODEL, LANES), lambda i: (0, 0)),
            pl.BlockSpec((LANES, GLA_KEY_DIM), lambda i: (0, 0)),
            pl.BlockSpec((1, GLA_KEY_DIM), lambda i: (0, 0)),
        ],
        out_specs=pl.BlockSpec((TM, GLA_KEY_DIM), lambda i: (i, 0)),
        out_shape=jax.ShapeDtypeStruct((m, GLA_KEY_DIM), F32),
        compiler_params=_cparams(("parallel",)),
        name="gla_gate",
    )(x, nw.reshape(1, D_MODEL), w1p, w2p, b.reshape(1, GLA_KEY_DIM))


def _gla_out(o, gate, norm_w):
    return _rms(o, norm_w) * _silu(gate)


def _cumsum_rows(g):
    c = g.shape[0]
    tri = (lax.broadcasted_iota(jnp.int32, (c, c), 0) >= lax.broadcasted_iota(jnp.int32, (c, c), 1)).astype(F32)
    return jnp.dot(tri, g, precision=lax.Precision.HIGHEST, preferred_element_type=F32)


def _causal(a):
    c = a.shape[0]
    keep = lax.broadcasted_iota(jnp.int32, (c, c), 0) >= lax.broadcasted_iota(jnp.int32, (c, c), 1)
    return jnp.where(keep, a, 0.0)


_NT = (((1,), (1,)), ((), ()))
_TN = (((0,), (0,)), ((), ()))


def _gla_prompt_body(q_ref, k_ref, v_ref, gate_ref, la_ref, nw_ref, o_ref, s_ref, st_ref):
    c = pl.program_id(2)

    @pl.when(c == 0)
    def _():
        st_ref[...] = jnp.zeros_like(st_ref)

    b = _cumsum_rows(la_ref[...])
    k = k_ref[...].astype(F32)
    v = v_ref[...]
    qe = (q_ref[...].astype(F32) * GLA_DK ** -0.5 * jnp.exp(b)).astype(BF16)
    ke = (k * jnp.exp(-b)).astype(BF16)
    a = _causal(lax.dot_general(qe, ke, _NT, preferred_element_type=F32))
    st = st_ref[...]
    o = (jnp.dot(a.astype(BF16), v, preferred_element_type=F32)
         + lax.dot_general(qe, st.astype(BF16), _NT, preferred_element_type=F32))
    b_last = b[GLA_C - 1:GLA_C, :]
    kd = (k * jnp.exp(b_last - b)).astype(BF16)
    st_new = st * jnp.exp(b_last) + lax.dot_general(v, kd, _TN, preferred_element_type=F32)
    st_ref[...] = st_new
    o_ref[...] = _gla_out(o, gate_ref[...].astype(F32), nw_ref[...]).astype(o_ref.dtype)

    @pl.when(c == pl.num_programs(2) - 1)
    def _():
        s_ref[0, 0] = st_new.T


def _gla_prompt(proj, log_a, norm_w):
    nc = SEQ // GLA_C
    kb = GLA_KEY_DIM // GLA_DK
    vb = 2 * GLA_KEY_DIM // GLA_DV
    gb = vb + GLA_VAL_DIM // GLA_DV
    row = lambda b, h, c: b * nc + c
    return pl.pallas_call(
        _gla_prompt_body,
        grid=(BATCH, GLA_HEADS, nc),
        in_specs=[
            pl.BlockSpec((GLA_C, GLA_DK), lambda b, h, c: (row(b, h, c), h)),
            pl.BlockSpec((GLA_C, GLA_DK), lambda b, h, c: (row(b, h, c), kb + h)),
            pl.BlockSpec((GLA_C, GLA_DV), lambda b, h, c: (row(b, h, c), vb + h)),
            pl.BlockSpec((GLA_C, GLA_DV), lambda b, h, c: (row(b, h, c), gb + h)),
            pl.BlockSpec((GLA_C, GLA_DK), lambda b, h, c: (row(b, h, c), h)),
            pl.BlockSpec((1, GLA_DV), lambda b, h, c: (0, 0)),
        ],
        out_specs=[
            pl.BlockSpec((GLA_C, GLA_DV), lambda b, h, c: (row(b, h, c), h)),
            pl.BlockSpec((1, 1, GLA_DK, GLA_DV), lambda b, h, c: (b, h, 0, 0)),
        ],
        out_shape=[
            jax.ShapeDtypeStruct((N_PROMPT, GLA_VAL_DIM), BF16),
            jax.ShapeDtypeStruct((BATCH, GLA_HEADS, GLA_DK, GLA_DV), F32),
        ],
        scratch_shapes=[pltpu.VMEM((GLA_DV, GLA_DK), F32)],
        compiler_params=_cparams(("parallel", "parallel", "arbitrary")),
        name="gla_prompt",
    )(proj, proj, proj, proj, log_a, norm_w.reshape(1, GLA_DV))


def _gla_sample_body(proj_ref, la_ref, s0_ref, nw_ref, o_ref, s_ref):
    ones = jnp.ones((DEC_SEQ, LANES), F32)
    for h in range(GLA_HEADS):
        q = proj_ref[0, :, h * GLA_DK:(h + 1) * GLA_DK]
        k = proj_ref[0, :, GLA_KEY_DIM + h * GLA_DK:GLA_KEY_DIM + (h + 1) * GLA_DK]
        v0 = 2 * GLA_KEY_DIM + h * GLA_DV
        v = proj_ref[0, :, v0:v0 + GLA_DV].astype(BF16)
        gate = proj_ref[0, :, v0 + GLA_VAL_DIM:v0 + GLA_VAL_DIM + GLA_DV]
        g = la_ref[0, :, h * GLA_DK:(h + 1) * GLA_DK]
        b = _cumsum_rows(g)
        qe = (q * GLA_DK ** -0.5 * jnp.exp(b)).astype(BF16)
        ke = (k * jnp.exp(-b)).astype(BF16)
        a = _causal(lax.dot_general(qe, ke, _NT, preferred_element_type=F32))
        s0 = s0_ref[0, h]
        o = (jnp.dot(a.astype(BF16), v, preferred_element_type=F32)
             + jnp.dot(qe, s0.astype(BF16), preferred_element_type=F32))
        b_last = b[DEC_SEQ - 1:DEC_SEQ, :]
        kd = (k * jnp.exp(b_last - b)).astype(BF16)
        dcol = jnp.exp(lax.dot_general(g, ones, _TN, precision=lax.Precision.HIGHEST,
                                       preferred_element_type=F32))
        decay = jnp.concatenate([dcol] * (GLA_DV // LANES), axis=1)
        s_ref[0, h] = s0 * decay + lax.dot_general(kd, v, _TN, preferred_element_type=F32)
        o_ref[0, :, h * GLA_DV:(h + 1) * GLA_DV] = _gla_out(o, gate, nw_ref[...])


def _gla_sample(proj3, log_a3, state, norm_w):
    seq3 = lambda s: (s, 0, 0)
    seq4 = lambda s: (s, 0, 0, 0)
    return pl.pallas_call(
        _gla_sample_body,
        grid=(DEC_BATCH,),
        in_specs=[
            pl.BlockSpec((1, DEC_SEQ, GLA_MAIN_DIM), seq3),
            pl.BlockSpec((1, DEC_SEQ, GLA_KEY_DIM), seq3),
            pl.BlockSpec((1, GLA_HEADS, GLA_DK, GLA_DV), seq4),
            pl.BlockSpec((1, GLA_DV), lambda s: (0, 0)),
        ],
        out_specs=[
            pl.BlockSpec((1, DEC_SEQ, GLA_VAL_DIM), seq3),
            pl.BlockSpec((1, GLA_HEADS, GLA_DK, GLA_DV), seq4),
        ],
        out_shape=[
            jax.ShapeDtypeStruct((DEC_BATCH, DEC_SEQ, GLA_VAL_DIM), F32),
            jax.ShapeDtypeStruct((DEC_BATCH, GLA_HEADS, GLA_DK, GLA_DV), F32),
        ],
        compiler_params=_cparams(("parallel",)),
        name="gla_sample",
    )(proj3, log_a3, state, norm_w.reshape(1, GLA_DV))


def _swa_layer(x, cache_k, cache_v, norm_w, w_qkv, b_qkv, w_o, b_o, sinks, rel_bias):
    slots = (N_PAIRS, KV_PER_STEP, GROUP, HEAD_DIM)
    w_q = w_qkv[:, :Q_DIM].reshape((D_MODEL,) + slots).transpose(0, 1, 3, 2, 4).reshape(D_MODEL, Q_DIM)
    b_q = b_qkv[:Q_DIM].reshape(slots).transpose(0, 2, 1, 3).reshape(Q_DIM)
    w_qkv_s = jnp.concatenate([w_q, w_qkv[:, Q_DIM:]], axis=1).astype(BF16)
    b_qkv_s = jnp.concatenate([b_q, b_qkv[Q_DIM:]])
    w_o_s = w_o.reshape(slots + (D_MODEL,)).transpose(0, 2, 1, 3, 4).reshape(Q_DIM, D_MODEL).astype(BF16)

    q, kv = _norm_proj(x, norm_w, w_qkv_s, b_qkv_s, ((Q_DIM, BF16), (2 * KV_COLS, F32)), "swa_qkv")
    bias_tbl = _bias_table(rel_bias)
    o_p = _swa_prompt(q, kv, bias_tbl, sinks)

    q_rows = (q[N_PROMPT:].astype(F32).reshape(DEC_BATCH, DEC_SEQ, N_PAIRS, GROUP, KV_PER_STEP, HEAD_DIM)
              .transpose(0, 2, 4, 3, 1, 5).reshape(DEC_BATCH, S_ROWS, HEAD_DIM))
    kv_s = kv[N_PROMPT:].reshape(DEC_BATCH, DEC_SEQ, 2 * KV_COLS)
    pad8 = ((0, 0), (0, SUBLANES - DEC_SEQ), (0, 0))
    k_new8 = jnp.pad(kv_s[..., :KV_COLS], pad8)
    v_new8 = jnp.pad(kv_s[..., KV_COLS:], pad8)
    bias_s = (bias_tbl[:, :, :DEC_SEQ].reshape(N_PAIRS, GROUP, KV_PER_STEP, 2 * WINDOW, DEC_SEQ)
              .transpose(0, 2, 1, 4, 3).reshape(S_ROWS, 2 * WINDOW))
    sink_col = jnp.repeat(sinks, DEC_SEQ).reshape(S_ROWS, 1)
    o_s, k_s, v_s = _swa_sample(q_rows, k_new8, v_new8,
                                cache_k.reshape(DEC_BATCH, WINDOW, KV_COLS),
                                cache_v.reshape(DEC_BATCH, WINDOW, KV_COLS), bias_s, sink_col)
    o_s = (o_s.reshape(DEC_BATCH, N_PAIRS, KV_PER_STEP, GROUP, DEC_SEQ, HEAD_DIM)
           .transpose(0, 4, 1, 3, 2, 5).reshape(N_SAMPLE, Q_DIM))
    x = _proj_res(o_p, o_s, w_o_s, b_o, x, "swa_out")

    kv_p = kv[:N_PROMPT].reshape(BATCH, SEQ, 2 * KV_COLS)[:, SEQ - WINDOW:]
    k_p = kv_p[..., :KV_COLS].reshape(BATCH, WINDOW, N_KV_HEADS, HEAD_DIM)
    v_p = kv_p[..., KV_COLS:].reshape(BATCH, WINDOW, N_KV_HEADS, HEAD_DIM)
    shape_s = (DEC_BATCH, WINDOW, N_KV_HEADS, HEAD_DIM)
    return x, k_p, v_p, k_s.reshape(shape_s), v_s.reshape(shape_s)


def _gla_layer(x, state, norm_w, w_in, w_gk2, b_gk, gnorm, w_o):
    w_main = w_in[:, :GLA_MAIN_DIM].astype(BF16)
    w1p = jnp.pad(w_in[:, GLA_MAIN_DIM:], ((0, 0), (0, LANES - GATE_RANK))).astype(BF16)
    w2p = jnp.pad(w_gk2, ((0, LANES - GATE_RANK), (0, 0))).astype(BF16)
    (proj,) = _norm_proj(x, norm_w, w_main, jnp.zeros((GLA_MAIN_DIM,), F32), ((GLA_MAIN_DIM, BF16),), "gla_in")
    log_a = _gla_gate(x, norm_w, w1p, w2p, b_gk)
    o_p, s_p = _gla_prompt(proj, log_a, gnorm)
    proj_s = proj[N_PROMPT:].astype(F32).reshape(DEC_BATCH, DEC_SEQ, GLA_MAIN_DIM)
    log_a_s = log_a[N_PROMPT:].reshape(DEC_BATCH, DEC_SEQ, GLA_KEY_DIM)
    o_s, s_s = _gla_sample(proj_s, log_a_s, state, gnorm)
    x = _proj_res(o_p, o_s.reshape(N_SAMPLE, GLA_VAL_DIM), w_o.astype(BF16), jnp.zeros((D_MODEL,), F32), x,
                  "gla_out")
    return x, s_p, s_s


def kernel(x_prompt, x_sample, cache_swa_k, cache_swa_v, state_gla, norm_ffn1, ffn1_w_gate, ffn1_w_up,
           ffn1_w_down, norm_mix, norm_ffn2, ffn2_w_gate, ffn2_w_up, ffn2_w_down, norm_final, rel_bias,
           swa_w_qkv, swa_b_qkv, swa_w_o, swa_b_o, swa_sinks, gla_w_in, gla_w_gk2, gla_b_gk, gla_norm,
           gla_w_o):
    xs = [x_prompt.reshape(N_PROMPT, D_MODEL), x_sample.reshape(N_SAMPLE, D_MODEL)]
    swa_kp, swa_vp, swa_ks, swa_vs, gla_sp, gla_ss = [], [], [], [], [], []
    for i in range(DEPTH):
        (x,) = _ffn(xs, norm_ffn1[i], ffn1_w_gate[i].astype(BF16), ffn1_w_up[i].astype(BF16),
                    ffn1_w_down[i].astype(BF16))
        j = i // 2
        if i % 2 == 0:
            x, kp, vp, ks, vs = _swa_layer(x, cache_swa_k[j], cache_swa_v[j], norm_mix[i], swa_w_qkv[j],
                                           swa_b_qkv[j], swa_w_o[j], swa_b_o[j], swa_sinks[j], rel_bias)
            swa_kp.append(kp)
            swa_vp.append(vp)
            swa_ks.append(ks)
            swa_vs.append(vs)
        else:
            x, sp, ss = _gla_layer(x, state_gla[j], norm_mix[i], gla_w_in[j], gla_w_gk2[j], gla_b_gk[j],
                                   gla_norm[j], gla_w_o[j])
            gla_sp.append(sp)
            gla_ss.append(ss)
        last = i == DEPTH - 1
        xs = _ffn([x], norm_ffn2[i], ffn2_w_gate[i].astype(BF16), ffn2_w_up[i].astype(BF16),
                  ffn2_w_down[i].astype(BF16), norm_final if last else None, split_out=last)
    y_prompt = xs[0].reshape(BATCH, SEQ, D_MODEL)
    y_sample = xs[1].reshape(DEC_BATCH, DEC_SEQ, D_MODEL)
    return (y_prompt, y_sample, jnp.stack(swa_kp), jnp.stack(swa_vp), jnp.stack(swa_ks), jnp.stack(swa_vs),
            jnp.stack(gla_sp), jnp.stack(gla_ss))
```

```python
import functools
import math

import numpy as np
import jax
import jax.numpy as jnp
from jax import lax
from jax.experimental import pallas as pl
from jax.experimental.pallas import tpu as pltpu

F32 = jnp.float32
BF16 = jnp.bfloat16

D_MODEL = 2048
BATCH = 2
SEQ = 4096
DEPTH = 2
DEC_BATCH = 128
DEC_SEQ = 4
RMS_EPS = 1e-6
D_FF = 5632
N_HEADS = 32
N_KV_HEADS = 8
HEAD_DIM = 64
GROUP = N_HEADS // N_KV_HEADS
WINDOW = 128
NUM_BUCKETS = 32
MAX_DISTANCE = 128
NEG_INF = -1e30
GLA_HEADS = 4
GLA_DK = 256
GLA_DV = 512
GLA_KEY_DIM = GLA_HEADS * GLA_DK
GLA_VAL_DIM = GLA_HEADS * GLA_DV
GATE_RANK = 16
GATE_NORMALIZER = 16.0
GLA_MAIN_DIM = 2 * GLA_KEY_DIM + 2 * GLA_VAL_DIM
Q_DIM = N_HEADS * HEAD_DIM
KV_COLS = N_KV_HEADS * HEAD_DIM

N_PROMPT = BATCH * SEQ
N_SAMPLE = DEC_BATCH * DEC_SEQ
N_TOK = N_PROMPT + N_SAMPLE

LANES = 128
SUBLANES = 8
VMEM_LIMIT = 56 * 1024 * 1024

TM = 512
TF = 512
TN = 512
GLA_C = 64
NB = SEQ // WINDOW
N_PROMPT_TILES = N_PROMPT // TM


def _rms(x, w):
    return x * lax.rsqrt(jnp.mean(x * x, axis=-1, keepdims=True) + RMS_EPS) * w


def _silu(x):
    return x * jax.nn.sigmoid(x)


def _cparams(sem):
    return pltpu.CompilerParams(dimension_semantics=sem, vmem_limit_bytes=VMEM_LIMIT)


def _row_specs(split, width):
    if not split:
        return [pl.BlockSpec((TM, width), lambda i, *_: (i, 0))]
    return [pl.BlockSpec((TM, width), lambda i, *_: (jnp.minimum(i, N_PROMPT_TILES - 1), 0)),
            pl.BlockSpec((TM, width), lambda i, *_: (jnp.maximum(i - N_PROMPT_TILES, 0), 0))]


def _on_row_source(fn, *ref_groups):
    if all(len(g) == 1 for g in ref_groups):
        fn(*[g[0] for g in ref_groups])
        return
    i = pl.program_id(0)
    pl.when(i < N_PROMPT_TILES)(lambda: fn(*[g[0] for g in ref_groups]))
    pl.when(i >= N_PROMPT_TILES)(lambda: fn(*[g[-1] for g in ref_groups]))


CAST_BLOCK_BYTES = 6 * 1024 * 1024


def _cast_body(w_ref, o_ref):
    o_ref[...] = w_ref[0].astype(BF16)


def _layer_bf16(w, layer):
    _, rows, cols = w.shape
    tr = max(t for t in range(LANES, rows + 1, LANES)
             if rows % t == 0 and t * cols * 4 <= CAST_BLOCK_BYTES)
    return pl.pallas_call(
        _cast_body,
        grid=(rows // tr,),
        in_specs=[pl.BlockSpec((1, tr, cols), lambda i: (layer, i, 0))],
        out_specs=pl.BlockSpec((tr, cols), lambda i: (i, 0)),
        out_shape=jax.ShapeDtypeStruct((rows, cols), BF16),
        compiler_params=_cparams(("arbitrary",)),
        name="cast_bf16",
    )(w)


def _ffn_body(n_in, n_out, final_norm, *refs):
    x_refs = refs[:n_in]
    nw_ref, wg_ref, wu_ref, wd_ref = refs[n_in:n_in + 4]
    rest = refs[n_in + 4:]
    if final_norm:
        fw_ref, rest = rest[0], rest[1:]
    o_refs = rest[:n_out]
    h_ref, acc_ref = rest[n_out:]
    j = pl.program_id(1)

    @pl.when(j == 0)
    def _():
        def start(x_ref):
            h_ref[...] = _rms(x_ref[...], nw_ref[...]).astype(BF16)
        _on_row_source(start, x_refs)
        acc_ref[...] = jnp.zeros_like(acc_ref)

    h = h_ref[...]
    g = jnp.dot(h, wg_ref[...], preferred_element_type=F32)
    u = jnp.dot(h, wu_ref[...], preferred_element_type=F32)
    a = (_silu(g) * u).astype(BF16)
    acc_ref[...] += jnp.dot(a, wd_ref[...], preferred_element_type=F32)

    @pl.when(j == pl.num_programs(1) - 1)
    def _():
        def finish(x_ref, o_ref):
            y = x_ref[...] + 0.5 * acc_ref[...]
            if final_norm:
                y = _rms(y, fw_ref[...])
            o_ref[...] = y
        _on_row_source(finish, x_refs, o_refs)


def _ffn(xs, nw, wg, wu, wd, final_w=None, split_out=False):
    final_norm = final_w is not None
    vec = pl.BlockSpec((1, D_MODEL), lambda i, j: (0, 0))
    in_specs = _row_specs(len(xs) == 2, D_MODEL) + [
        vec,
        pl.BlockSpec((D_MODEL, TF), lambda i, j: (0, j)),
        pl.BlockSpec((D_MODEL, TF), lambda i, j: (0, j)),
        pl.BlockSpec((TF, D_MODEL), lambda i, j: (j, 0)),
    ]
    args = list(xs) + [nw.reshape(1, D_MODEL), wg, wu, wd]
    if final_norm:
        in_specs.append(vec)
        args.append(final_w.reshape(1, D_MODEL))
    if split_out:
        out_shape = [jax.ShapeDtypeStruct((N_PROMPT, D_MODEL), F32), jax.ShapeDtypeStruct((N_SAMPLE, D_MODEL), F32)]
    else:
        out_shape = [jax.ShapeDtypeStruct((N_TOK, D_MODEL), F32)]
    return pl.pallas_call(
        functools.partial(_ffn_body, len(xs), len(out_shape), final_norm),
        grid=(N_TOK // TM, D_FF // TF),
        in_specs=in_specs,
        out_specs=_row_specs(split_out, D_MODEL),
        out_shape=out_shape,
        scratch_shapes=[pltpu.VMEM((TM, D_MODEL), BF16), pltpu.VMEM((TM, D_MODEL), F32)],
        compiler_params=_cparams(("arbitrary", "arbitrary")),
        name="ffn",
    )(*args)


def _norm_proj_body(segments, x_ref, nw_ref, w_ref, b_ref, *o_refs):
    h = _rms(x_ref[...], nw_ref[...]).astype(BF16)
    col = 0
    for (width, _), o_ref in zip(segments, o_refs):
        for c in range(0, width, TN):
            acc = jnp.dot(h, w_ref[:, col + c:col + c + TN], preferred_element_type=F32)
            o_ref[:, c:c + TN] = (acc + b_ref[:, col + c:col + c + TN]).astype(o_ref.dtype)
        col += width


def _norm_proj(x, nw, w, b, segments, name):
    n = w.shape[1]
    return pl.pallas_call(
        functools.partial(_norm_proj_body, segments),
        grid=(N_TOK // TM,),
        in_specs=[
            pl.BlockSpec((TM, D_MODEL), lambda i: (i, 0)),
            pl.BlockSpec((1, D_MODEL), lambda i: (0, 0)),
            pl.BlockSpec((D_MODEL, n), lambda i: (0, 0), pipeline_mode=pl.Buffered(1)),
            pl.BlockSpec((1, n), lambda i: (0, 0)),
        ],
        out_specs=[pl.BlockSpec((TM, width), lambda i: (i, 0)) for width, _ in segments],
        out_shape=[jax.ShapeDtypeStruct((N_TOK, width), dtype) for width, dtype in segments],
        compiler_params=_cparams(("arbitrary",)),
        name=name,
    )(x, nw.reshape(1, D_MODEL), w, b.reshape(1, n))


def _proj_res_body(ap_ref, as_ref, w_ref, b_ref, r_ref, o_ref):
    def run(a_ref):
        a = a_ref[...].astype(BF16)
        for c in range(0, D_MODEL, TN):
            acc = jnp.dot(a, w_ref[:, c:c + TN], preferred_element_type=F32)
            o_ref[:, c:c + TN] = r_ref[:, c:c + TN] + acc + b_ref[:, c:c + TN]
    _on_row_source(run, (ap_ref, as_ref))


def _proj_res(a_prompt, a_sample, w, b, res, name):
    k = w.shape[0]
    return pl.pallas_call(
        _proj_res_body,
        grid=(N_TOK // TM,),
        in_specs=_row_specs(True, k) + [
            pl.BlockSpec((k, D_MODEL), lambda i: (0, 0), pipeline_mode=pl.Buffered(1)),
            pl.BlockSpec((1, D_MODEL), lambda i: (0, 0)),
            pl.BlockSpec((TM, D_MODEL), lambda i: (i, 0)),
        ],
        out_specs=pl.BlockSpec((TM, D_MODEL), lambda i: (i, 0)),
        out_shape=jax.ShapeDtypeStruct((N_TOK, D_MODEL), F32),
        compiler_params=_cparams(("arbitrary",)),
        name=name,
    )(a_prompt, a_sample, w, b.reshape(1, D_MODEL), res)


def _t5_bucket_table():
    i = np.arange(WINDOW)[None, :]
    j = np.arange(2 * WINDOW)[:, None]
    n = np.maximum(WINDOW + i - j, 0)
    max_exact = NUM_BUCKETS // 2
    nf = np.maximum(n, 1).astype(np.float32)
    large = max_exact + (np.log(nf / np.float32(max_exact)) / np.float32(math.log(MAX_DISTANCE / max_exact))
                         * np.float32(NUM_BUCKETS - max_exact)).astype(np.int32)
    large = np.minimum(large, NUM_BUCKETS - 1)
    return np.where(n < max_exact, n, large).astype(np.int32)


KV_PER_STEP = LANES // HEAD_DIM
N_PAIRS = N_KV_HEADS // KV_PER_STEP
Q_COLS_PER_STEP = KV_PER_STEP * GROUP * HEAD_DIM


def _slot_head(slot):
    pair = slot // (GROUP * KV_PER_STEP)
    g = (slot // KV_PER_STEP) % GROUP
    hh = slot % KV_PER_STEP
    return (pair * KV_PER_STEP + hh) * GROUP + g


def _bias_table_body(bucket_ref, rb_ref, o_ref):
    h = _slot_head(pl.program_id(0))
    bucket = bucket_ref[...]
    acc = jnp.zeros((2 * WINDOW, WINDOW), F32)
    for b in range(NUM_BUCKETS):
        acc = jnp.where(bucket == b, rb_ref[b, h], acc)
    j = lax.broadcasted_iota(jnp.int32, (2 * WINDOW, WINDOW), 0)
    i = lax.broadcasted_iota(jnp.int32, (2 * WINDOW, WINDOW), 1)
    dist = WINDOW + i - j
    o_ref[0] = jnp.where((dist >= 0) & (dist < WINDOW), acc, NEG_INF)


def _bias_table(rel_bias):
    return pl.pallas_call(
        _bias_table_body,
        grid=(N_HEADS,),
        in_specs=[
            pl.BlockSpec((2 * WINDOW, WINDOW), lambda h: (0, 0)),
            pl.BlockSpec(memory_space=pltpu.SMEM),
        ],
        out_specs=pl.BlockSpec((1, 2 * WINDOW, WINDOW), lambda h: (h, 0, 0)),
        out_shape=jax.ShapeDtypeStruct((N_HEADS, 2 * WINDOW, WINDOW), F32),
        name="bias_table",
    )(jnp.asarray(_t5_bucket_table()), rel_bias)


def _softmax_with_sink(s, sink_col):
    m = jnp.maximum(jnp.max(s, axis=-1, keepdims=True), sink_col)
    p = jnp.exp(s - m)
    denom = jnp.sum(p, axis=-1, keepdims=True) + jnp.exp(sink_col - m)
    return p, 1.0 / denom


PAIR_SLOTS = GROUP * KV_PER_STEP
PAIR_COLS = PAIR_SLOTS * WINDOW
ONES_ROWS = 16


def _swa_prompt_body(sink_ref, q_ref, kp_ref, ko_ref, vp_ref, vo_ref, bias_ref, o_ref):
    pair = pl.program_id(0)
    blk = pl.program_id(2)
    head_a = lax.broadcasted_iota(jnp.int32, (WINDOW, LANES), 1) < HEAD_DIM
    k = jnp.concatenate([kp_ref[...], ko_ref[...]], axis=0).astype(BF16)
    v = jnp.concatenate([vp_ref[...], vo_ref[...]], axis=0)
    vt = jnp.concatenate([v.T, jnp.ones((ONES_ROWS, 2 * WINDOW), F32)], axis=0).astype(BF16)
    parts = []
    for g in range(GROUP):
        qg = q_ref[:, g * LANES:(g + 1) * LANES] * HEAD_DIM ** -0.5
        zero = jnp.zeros_like(qg)
        parts += [jnp.where(head_a, qg, zero), jnp.where(head_a, zero, qg)]
    qbd = jnp.concatenate(parts, axis=0)
    st = lax.dot_general(k, qbd, (((1,), (1,)), ((), ())), preferred_element_type=F32)
    st = st + jnp.concatenate([bias_ref[t] for t in range(PAIR_SLOTS)], axis=1)
    no_prev = jnp.where(blk > 0, 0.0, NEG_INF)
    st = jnp.concatenate([st[:WINDOW] + no_prev, st[WINDOW:]], axis=0)
    sink_row = jnp.concatenate(
        [jnp.full((1, WINDOW), sink_ref[_slot_head(pair * PAIR_SLOTS + slot)], F32)
         for slot in range(PAIR_SLOTS)], axis=1)
    m = jnp.maximum(jnp.max(st, axis=0, keepdims=True), sink_row)
    pt = jnp.exp(st - m).astype(BF16)
    oa = jnp.dot(vt, pt, preferred_element_type=F32)
    inv = 1.0 / (oa[LANES:LANES + 1] + jnp.exp(sink_row - m))
    o = oa[:LANES] * inv
    for g in range(GROUP):
        c = g * KV_PER_STEP * WINDOW
        ot = jnp.concatenate([o[:HEAD_DIM, c:c + WINDOW], o[HEAD_DIM:, c + WINDOW:c + 2 * WINDOW]], axis=0)
        o_ref[:, g * LANES:(g + 1) * LANES] = ot.T.astype(o_ref.dtype)


def _swa_prompt(q, kv, bias_tbl, sinks):
    v_col0 = KV_COLS // LANES

    def prev(p, b, i):
        return b * NB + jnp.maximum(i - 1, 0)

    return pl.pallas_call(
        _swa_prompt_body,
        grid=(N_PAIRS, BATCH, NB),
        in_specs=[
            pl.BlockSpec(memory_space=pltpu.SMEM),
            pl.BlockSpec((WINDOW, Q_COLS_PER_STEP), lambda p, b, i: (b * NB + i, p)),
            pl.BlockSpec((WINDOW, LANES), lambda p, b, i: (prev(p, b, i), p)),
            pl.BlockSpec((WINDOW, LANES), lambda p, b, i: (b * NB + i, p)),
            pl.BlockSpec((WINDOW, LANES), lambda p, b, i: (prev(p, b, i), v_col0 + p)),
            pl.BlockSpec((WINDOW, LANES), lambda p, b, i: (b * NB + i, v_col0 + p)),
            pl.BlockSpec((PAIR_SLOTS, 2 * WINDOW, WINDOW), lambda p, b, i: (p, 0, 0)),
        ],
        out_specs=pl.BlockSpec((WINDOW, Q_COLS_PER_STEP), lambda p, b, i: (b * NB + i, p)),
        out_shape=jax.ShapeDtypeStruct((N_PROMPT, Q_DIM), BF16),
        compiler_params=_cparams(("arbitrary", "arbitrary", "arbitrary")),
        name="swa_prompt",
    )(sinks, q, kv, kv, kv, kv, bias_tbl)


S_ROWS = N_KV_HEADS * GROUP * DEC_SEQ


def _swa_sample_body(q_ref, kn_ref, vn_ref, ck_ref, cv_ref, bias_ref, sink_ref,
                     o_ref, ko_ref, vo_ref):
    x = (q_ref[0] * HEAD_DIM ** -0.5).astype(BF16)
    xt = jnp.concatenate([x] * N_KV_HEADS, axis=1)
    row_kv = lax.broadcasted_iota(jnp.int32, (S_ROWS, KV_COLS), 0) // (GROUP * DEC_SEQ)
    col_kv = lax.broadcasted_iota(jnp.int32, (S_ROWS, KV_COLS), 1) // HEAD_DIM
    own = row_kv == col_kv
    qbd = jnp.where(own, xt, jnp.zeros_like(xt))
    pad = jnp.zeros((WINDOW - SUBLANES, KV_COLS), F32)
    kk = jnp.concatenate([ck_ref[0], kn_ref[0], pad], axis=0).astype(BF16)
    vv = jnp.concatenate([cv_ref[0], vn_ref[0], pad], axis=0).astype(BF16)
    s = lax.dot_general(qbd, kk, (((1,), (1,)), ((), ())), preferred_element_type=F32)
    s = s + bias_ref[...]
    p, inv = _softmax_with_sink(s, sink_ref[...])
    of = jnp.dot(p.astype(BF16), vv, preferred_element_type=F32)
    of = jnp.where(own, of, 0.0)
    o = of[:, 0:HEAD_DIM]
    for c in range(1, N_KV_HEADS):
        o = o + of[:, c * HEAD_DIM:(c + 1) * HEAD_DIM]
    o_ref[0] = o * inv
    keep = WINDOW - DEC_SEQ
    ko_ref[0, 0:keep, :] = ck_ref[0, DEC_SEQ:WINDOW, :]
    ko_ref[0, keep:WINDOW, :] = kn_ref[0, 0:DEC_SEQ, :]
    vo_ref[0, 0:keep, :] = cv_ref[0, DEC_SEQ:WINDOW, :]
    vo_ref[0, keep:WINDOW, :] = vn_ref[0, 0:DEC_SEQ, :]


def _swa_sample(q_rows, k_new8, v_new8, cache_k, cache_v, bias_s, sink_col):
    seq3 = lambda s: (s, 0, 0)
    full2 = lambda s: (0, 0)
    return pl.pallas_call(
        _swa_sample_body,
        grid=(DEC_BATCH,),
        in_specs=[
            pl.BlockSpec((1, S_ROWS, HEAD_DIM), seq3),
            pl.BlockSpec((1, SUBLANES, KV_COLS), seq3),
            pl.BlockSpec((1, SUBLANES, KV_COLS), seq3),
            pl.BlockSpec((1, WINDOW, KV_COLS), seq3),
            pl.BlockSpec((1, WINDOW, KV_COLS), seq3),
            pl.BlockSpec((S_ROWS, 2 * WINDOW), full2),
            pl.BlockSpec((S_ROWS, 1), full2),
        ],
        out_specs=[
            pl.BlockSpec((1, S_ROWS, HEAD_DIM), seq3),
            pl.BlockSpec((1, WINDOW, KV_COLS), seq3),
            pl.BlockSpec((1, WINDOW, KV_COLS), seq3),
        ],
        out_shape=[
            jax.ShapeDtypeStruct((DEC_BATCH, S_ROWS, HEAD_DIM), F32),
            jax.ShapeDtypeStruct((DEC_BATCH, WINDOW, KV_COLS), F32),
            jax.ShapeDtypeStruct((DEC_BATCH, WINDOW, KV_COLS), F32),
        ],
        compiler_params=_cparams(("parallel",)),
        name="swa_sample",
    )(q_rows, k_new8, v_new8, cache_k, cache_v, bias_s, sink_col)


def _log_sigmoid(x):
    return jnp.minimum(x, 0.0) - jnp.log1p(jnp.exp(-jnp.abs(x)))


def _gla_gate_body(x_ref, nw_ref, w1_ref, w2_ref, b_ref, o_ref):
    h = _rms(x_ref[...], nw_ref[...]).astype(BF16)
    gk = jnp.dot(h, w1_ref[...], preferred_element_type=F32)
    z = jnp.dot(gk.astype(BF16), w2_ref[...], preferred_element_type=F32) + b_ref[...]
    o_ref[...] = _log_sigmoid(z) / GATE_NORMALIZER


def _gla_gate(x, nw, w1p, w2p, b):
    m = x.shape[0]
    return pl.pallas_call(
        _gla_gate_body,
        grid=(m // TM,),
        in_specs=[
            pl.BlockSpec((TM, D_MODEL), lambda i: (i, 0)),
            pl.BlockSpec((1, D_MODEL), lambda i: (0, 0)),
            pl.BlockSpec((D_MODEL, LANES), lambda i: (0, 0)),
            pl.BlockSpec((LANES, GLA_KEY_DIM), lambda i: (0, 0)),
            pl.BlockSpec((1, GLA_KEY_DIM), lambda i: (0, 0)),
        ],
        out_specs=pl.BlockSpec((TM, GLA_KEY_DIM), lambda i: (i, 0)),
        out_shape=jax.ShapeDtypeStruct((m, GLA_KEY_DIM), F32),
        compiler_params=_cparams(("parallel",)),
        name="gla_gate",
    )(x, nw.reshape(1, D_MODEL), w1p, w2p, b.reshape(1, GLA_KEY_DIM))


def _gla_out(o, gate, norm_w):
    return _rms(o, norm_w) * _silu(gate)


def _cumsum_rows(g):
    c = g.shape[0]
    tri = (lax.broadcasted_iota(jnp.int32, (c, c), 0) >= lax.broadcasted_iota(jnp.int32, (c, c), 1)).astype(F32)
    return jnp.dot(tri, g, precision=lax.Precision.HIGHEST, preferred_element_type=F32)


def _causal(a):
    c = a.shape[0]
    keep = lax.broadcasted_iota(jnp.int32, (c, c), 0) >= lax.broadcasted_iota(jnp.int32, (c, c), 1)
    return jnp.where(keep, a, 0.0)


_NT = (((1,), (1,)), ((), ()))
_TN = (((0,), (0,)), ((), ()))


def _gla_prompt_body(*refs):
    proj_refs = refs[:BATCH]
    la_refs = refs[BATCH:2 * BATCH]
    nw_ref, o_ref, s_ref, st_ref = refs[2 * BATCH:]
    c = pl.program_id(0)

    @pl.when(c == 0)
    def _():
        st_ref[...] = jnp.zeros_like(st_ref)

    for bi in range(BATCH):
        p_ref = proj_refs[bi]
        b_all = _cumsum_rows(la_refs[bi][...])
        for h in range(GLA_HEADS):
            kc = slice(h * GLA_DK, (h + 1) * GLA_DK)
            v0 = 2 * GLA_KEY_DIM + h * GLA_DV
            b = b_all[:, kc]
            q = p_ref[:, kc].astype(F32)
            k = p_ref[:, GLA_KEY_DIM + h * GLA_DK:GLA_KEY_DIM + (h + 1) * GLA_DK].astype(F32)
            v = p_ref[:, v0:v0 + GLA_DV]
            gate = p_ref[:, v0 + GLA_VAL_DIM:v0 + GLA_VAL_DIM + GLA_DV].astype(F32)
            qe = (q * GLA_DK ** -0.5 * jnp.exp(b)).astype(BF16)
            ke = (k * jnp.exp(-b)).astype(BF16)
            a = _causal(lax.dot_general(qe, ke, _NT, preferred_element_type=F32))
            st = st_ref[bi, h]
            o = (jnp.dot(a.astype(BF16), v, preferred_element_type=F32)
                 + lax.dot_general(qe, st.astype(BF16), _NT, preferred_element_type=F32))
            b_last = b[GLA_C - 1:GLA_C, :]
            kd = (k * jnp.exp(b_last - b)).astype(BF16)
            st_new = st * jnp.exp(b_last) + lax.dot_general(v, kd, _TN, preferred_element_type=F32)
            st_ref[bi, h] = st_new
            o_ref[bi, :, h * GLA_DV:(h + 1) * GLA_DV] = _gla_out(o, gate, nw_ref[...]).astype(o_ref.dtype)

    @pl.when(c == pl.num_programs(0) - 1)
    def _():
        for bi in range(BATCH):
            for h in range(GLA_HEADS):
                s_ref[bi, h] = st_ref[bi, h].T


def _gla_prompt(proj, log_a, norm_w):
    nc = SEQ // GLA_C
    rows = [functools.partial(lambda bi, c: (bi * nc + c, 0), bi) for bi in range(BATCH)]
    o, s = pl.pallas_call(
        _gla_prompt_body,
        grid=(nc,),
        in_specs=([pl.BlockSpec((GLA_C, GLA_MAIN_DIM), r) for r in rows]
                  + [pl.BlockSpec((GLA_C, GLA_KEY_DIM), r) for r in rows]
                  + [pl.BlockSpec((1, GLA_DV), lambda c: (0, 0))]),
        out_specs=[
            pl.BlockSpec((BATCH, GLA_C, GLA_VAL_DIM), lambda c: (0, c, 0)),
            pl.BlockSpec((BATCH, GLA_HEADS, GLA_DK, GLA_DV), lambda c: (0, 0, 0, 0)),
        ],
        out_shape=[
            jax.ShapeDtypeStruct((BATCH, SEQ, GLA_VAL_DIM), BF16),
            jax.ShapeDtypeStruct((BATCH, GLA_HEADS, GLA_DK, GLA_DV), F32),
        ],
        scratch_shapes=[pltpu.VMEM((BATCH, GLA_HEADS, GLA_DV, GLA_DK), F32)],
        compiler_params=_cparams(("arbitrary",)),
        name="gla_prompt",
    )(*([proj] * BATCH + [log_a] * BATCH + [norm_w.reshape(1, GLA_DV)]))
    return o.reshape(N_PROMPT, GLA_VAL_DIM), s


def _gla_sample_body(proj_ref, la_ref, s0_ref, nw_ref, o_ref, s_ref):
    ones = jnp.ones((DEC_SEQ, LANES), F32)
    for h in range(GLA_HEADS):
        q = proj_ref[0, :, h * GLA_DK:(h + 1) * GLA_DK]
        k = proj_ref[0, :, GLA_KEY_DIM + h * GLA_DK:GLA_KEY_DIM + (h + 1) * GLA_DK]
        v0 = 2 * GLA_KEY_DIM + h * GLA_DV
        v = proj_ref[0, :, v0:v0 + GLA_DV].astype(BF16)
        gate = proj_ref[0, :, v0 + GLA_VAL_DIM:v0 + GLA_VAL_DIM + GLA_DV]
        g = la_ref[0, :, h * GLA_DK:(h + 1) * GLA_DK]
        b = _cumsum_rows(g)
        qe = (q * GLA_DK ** -0.5 * jnp.exp(b)).astype(BF16)
        ke = (k * jnp.exp(-b)).astype(BF16)
        a = _causal(lax.dot_general(qe, ke, _NT, preferred_element_type=F32))
        s0 = s0_ref[0, h]
        o = (jnp.dot(a.astype(BF16), v, preferred_element_type=F32)
             + jnp.dot(qe, s0.astype(BF16), preferred_element_type=F32))
        b_last = b[DEC_SEQ - 1:DEC_SEQ, :]
        kd = (k * jnp.exp(b_last - b)).astype(BF16)
        dcol = jnp.exp(lax.dot_general(g, ones, _TN, precision=lax.Precision.HIGHEST,
                                       preferred_element_type=F32))
        decay = jnp.concatenate([dcol] * (GLA_DV // LANES), axis=1)
        s_ref[0, h] = s0 * decay + lax.dot_general(kd, v, _TN, preferred_element_type=F32)
        o_ref[0, :, h * GLA_DV:(h + 1) * GLA_DV] = _gla_out(o, gate, nw_ref[...])


def _gla_sample(proj3, log_a3, state, norm_w):
    seq3 = lambda s: (s, 0, 0)
    seq4 = lambda s: (s, 0, 0, 0)
    return pl.pallas_call(
        _gla_sample_body,
        grid=(DEC_BATCH,),
        in_specs=[
            pl.BlockSpec((1, DEC_SEQ, GLA_MAIN_DIM), seq3),
            pl.BlockSpec((1, DEC_SEQ, GLA_KEY_DIM), seq3),
            pl.BlockSpec((1, GLA_HEADS, GLA_DK, GLA_DV), seq4),
            pl.BlockSpec((1, GLA_DV), lambda s: (0, 0)),
        ],
        out_specs=[
            pl.BlockSpec((1, DEC_SEQ, GLA_VAL_DIM), seq3),
            pl.BlockSpec((1, GLA_HEADS, GLA_DK, GLA_DV), seq4),
        ],
        out_shape=[
            jax.ShapeDtypeStruct((DEC_BATCH, DEC_SEQ, GLA_VAL_DIM), F32),
            jax.ShapeDtypeStruct((DEC_BATCH, GLA_HEADS, GLA_DK, GLA_DV), F32),
        ],
        compiler_params=_cparams(("parallel",)),
        name="gla_sample",
    )(proj3, log_a3, state, norm_w.reshape(1, GLA_DV))


def _swa_layer(x, cache_k, cache_v, norm_w, w_qkv, b_qkv, w_o, b_o, sinks, rel_bias):
    slots = (N_PAIRS, KV_PER_STEP, GROUP, HEAD_DIM)
    w_q = w_qkv[:, :Q_DIM].reshape((D_MODEL,) + slots).transpose(0, 1, 3, 2, 4).reshape(D_MODEL, Q_DIM)
    b_q = b_qkv[:Q_DIM].reshape(slots).transpose(0, 2, 1, 3).reshape(Q_DIM)
    w_qkv_s = jnp.concatenate([w_q, w_qkv[:, Q_DIM:]], axis=1).astype(BF16)
    b_qkv_s = jnp.concatenate([b_q, b_qkv[Q_DIM:]])
    w_o_s = w_o.reshape(slots + (D_MODEL,)).transpose(0, 2, 1, 3, 4).reshape(Q_DIM, D_MODEL).astype(BF16)

    q, kv = _norm_proj(x, norm_w, w_qkv_s, b_qkv_s, ((Q_DIM, BF16), (2 * KV_COLS, F32)), "swa_qkv")
    bias_tbl = _bias_table(rel_bias)
    o_p = _swa_prompt(q, kv, bias_tbl, sinks)

    q_rows = (q[N_PROMPT:].astype(F32).reshape(DEC_BATCH, DEC_SEQ, N_PAIRS, GROUP, KV_PER_STEP, HEAD_DIM)
              .transpose(0, 2, 4, 3, 1, 5).reshape(DEC_BATCH, S_ROWS, HEAD_DIM))
    kv_s = kv[N_PROMPT:].reshape(DEC_BATCH, DEC_SEQ, 2 * KV_COLS)
    pad8 = ((0, 0), (0, SUBLANES - DEC_SEQ), (0, 0))
    k_new8 = jnp.pad(kv_s[..., :KV_COLS], pad8)
    v_new8 = jnp.pad(kv_s[..., KV_COLS:], pad8)
    bias_s = (bias_tbl[:, :, :DEC_SEQ].reshape(N_PAIRS, GROUP, KV_PER_STEP, 2 * WINDOW, DEC_SEQ)
              .transpose(0, 2, 1, 4, 3).reshape(S_ROWS, 2 * WINDOW))
    sink_col = jnp.repeat(sinks, DEC_SEQ).reshape(S_ROWS, 1)
    o_s, k_s, v_s = _swa_sample(q_rows, k_new8, v_new8,
                                cache_k.reshape(DEC_BATCH, WINDOW, KV_COLS),
                                cache_v.reshape(DEC_BATCH, WINDOW, KV_COLS), bias_s, sink_col)
    o_s = (o_s.reshape(DEC_BATCH, N_PAIRS, KV_PER_STEP, GROUP, DEC_SEQ, HEAD_DIM)
           .transpose(0, 4, 1, 3, 2, 5).reshape(N_SAMPLE, Q_DIM))
    x = _proj_res(o_p, o_s, w_o_s, b_o, x, "swa_out")

    kv_p = kv[:N_PROMPT].reshape(BATCH, SEQ, 2 * KV_COLS)[:, SEQ - WINDOW:]
    k_p = kv_p[..., :KV_COLS].reshape(BATCH, WINDOW, N_KV_HEADS, HEAD_DIM)
    v_p = kv_p[..., KV_COLS:].reshape(BATCH, WINDOW, N_KV_HEADS, HEAD_DIM)
    shape_s = (DEC_BATCH, WINDOW, N_KV_HEADS, HEAD_DIM)
    return x, k_p, v_p, k_s.reshape(shape_s), v_s.reshape(shape_s)


def _gla_layer(x, state, norm_w, w_in, w_gk2, b_gk, gnorm, w_o):
    w_main = w_in[:, :GLA_MAIN_DIM].astype(BF16)
    w1p = jnp.pad(w_in[:, GLA_MAIN_DIM:], ((0, 0), (0, LANES - GATE_RANK))).astype(BF16)
    w2p = jnp.pad(w_gk2, ((0, LANES - GATE_RANK), (0, 0))).astype(BF16)
    (proj,) = _norm_proj(x, norm_w, w_main, jnp.zeros((GLA_MAIN_DIM,), F32), ((GLA_MAIN_DIM, BF16),), "gla_in")
    log_a = _gla_gate(x, norm_w, w1p, w2p, b_gk)
    o_p, s_p = _gla_prompt(proj, log_a, gnorm)
    proj_s = proj[N_PROMPT:].astype(F32).reshape(DEC_BATCH, DEC_SEQ, GLA_MAIN_DIM)
    log_a_s = log_a[N_PROMPT:].reshape(DEC_BATCH, DEC_SEQ, GLA_KEY_DIM)
    o_s, s_s = _gla_sample(proj_s, log_a_s, state, gnorm)
    x = _proj_res(o_p, o_s.reshape(N_SAMPLE, GLA_VAL_DIM), w_o.astype(BF16), jnp.zeros((D_MODEL,), F32), x,
                  "gla_out")
    return x, s_p, s_s


def kernel(x_prompt, x_sample, cache_swa_k, cache_swa_v, state_gla, norm_ffn1, ffn1_w_gate, ffn1_w_up,
           ffn1_w_down, norm_mix, norm_ffn2, ffn2_w_gate, ffn2_w_up, ffn2_w_down, norm_final, rel_bias,
           swa_w_qkv, swa_b_qkv, swa_w_o, swa_b_o, swa_sinks, gla_w_in, gla_w_gk2, gla_b_gk, gla_norm,
           gla_w_o):
    xs = [x_prompt.reshape(N_PROMPT, D_MODEL), x_sample.reshape(N_SAMPLE, D_MODEL)]
    swa_kp, swa_vp, swa_ks, swa_vs, gla_sp, gla_ss = [], [], [], [], [], []
    for i in range(DEPTH):
        (x,) = _ffn(xs, norm_ffn1[i], _layer_bf16(ffn1_w_gate, i), _layer_bf16(ffn1_w_up, i),
                    _layer_bf16(ffn1_w_down, i))
        j = i // 2
        if i % 2 == 0:
            x, kp, vp, ks, vs = _swa_layer(x, cache_swa_k[j], cache_swa_v[j], norm_mix[i], swa_w_qkv[j],
                                           swa_b_qkv[j], swa_w_o[j], swa_b_o[j], swa_sinks[j], rel_bias)
            swa_kp.append(kp)
            swa_vp.append(vp)
            swa_ks.append(ks)
            swa_vs.append(vs)
        else:
            x, sp, ss = _gla_layer(x, state_gla[j], norm_mix[i], gla_w_in[j], gla_w_gk2[j], gla_b_gk[j],
                                   gla_norm[j], gla_w_o[j])
            gla_sp.append(sp)
            gla_ss.append(ss)
        last = i == DEPTH - 1
        xs = _ffn([x], norm_ffn2[i], _layer_bf16(ffn2_w_gate, i), _layer_bf16(ffn2_w_up, i),
                  _layer_bf16(ffn2_w_down, i), norm_final if last else None, split_out=last)
    y_prompt = xs[0].reshape(BATCH, SEQ, D_MODEL)
    y_sample = xs[1].reshape(DEC_BATCH, DEC_SEQ, D_MODEL)
    return (y_prompt, y_sample, jnp.stack(swa_kp), jnp.stack(swa_vp), jnp.stack(swa_ks), jnp.stack(swa_vs),
            jnp.stack(gla_sp), jnp.stack(gla_ss))
```

```python
import functools
import math

import numpy as np
import jax
import jax.numpy as jnp
from jax import lax
from jax.experimental import pallas as pl
from jax.experimental.pallas import tpu as pltpu

F32 = jnp.float32
BF16 = jnp.bfloat16

D_MODEL = 2048
BATCH = 2
SEQ = 4096
DEPTH = 2
DEC_BATCH = 128
DEC_SEQ = 4
RMS_EPS = 1e-6
D_FF = 5632
N_HEADS = 32
N_KV_HEADS = 8
HEAD_DIM = 64
GROUP = N_HEADS // N_KV_HEADS
WINDOW = 128
NUM_BUCKETS = 32
MAX_DISTANCE = 128
NEG_INF = -1e30
GLA_HEADS = 4
GLA_DK = 256
GLA_DV = 512
GLA_KEY_DIM = GLA_HEADS * GLA_DK
GLA_VAL_DIM = GLA_HEADS * GLA_DV
GATE_RANK = 16
GATE_NORMALIZER = 16.0
GLA_MAIN_DIM = 2 * GLA_KEY_DIM + 2 * GLA_VAL_DIM
Q_DIM = N_HEADS * HEAD_DIM
KV_COLS = N_KV_HEADS * HEAD_DIM

N_PROMPT = BATCH * SEQ
N_SAMPLE = DEC_BATCH * DEC_SEQ
N_TOK = N_PROMPT + N_SAMPLE

LANES = 128
SUBLANES = 8
VMEM_LIMIT = 56 * 1024 * 1024

TM = 512
TF = 512
TN = 512
GLA_C = 64
NB = SEQ // WINDOW
N_PROMPT_TILES = N_PROMPT // TM


def _rms(x, w):
    return x * lax.rsqrt(jnp.mean(x * x, axis=-1, keepdims=True) + RMS_EPS) * w


def _silu(x):
    return x * jax.nn.sigmoid(x)


def _cparams(sem):
    return pltpu.CompilerParams(dimension_semantics=sem, vmem_limit_bytes=VMEM_LIMIT)


def _row_specs(split, width):
    if not split:
        return [pl.BlockSpec((TM, width), lambda i, *_: (i, 0))]
    return [pl.BlockSpec((TM, width), lambda i, *_: (jnp.minimum(i, N_PROMPT_TILES - 1), 0)),
            pl.BlockSpec((TM, width), lambda i, *_: (jnp.maximum(i - N_PROMPT_TILES, 0), 0))]


def _on_row_source(fn, *ref_groups):
    if all(len(g) == 1 for g in ref_groups):
        fn(*[g[0] for g in ref_groups])
        return
    i = pl.program_id(0)
    pl.when(i < N_PROMPT_TILES)(lambda: fn(*[g[0] for g in ref_groups]))
    pl.when(i >= N_PROMPT_TILES)(lambda: fn(*[g[-1] for g in ref_groups]))


FFN_TM = 1024
FFN_TF = 256


def _ffn_body(final_norm, x_ref, nw_ref, wg_ref, wu_ref, wd_ref, *rest):
    if final_norm:
        fw_ref, o_ref, h_ref = rest
    else:
        o_ref, h_ref = rest
    j = pl.program_id(1)

    @pl.when(j == 0)
    def _():
        x = x_ref[...]
        h_ref[...] = _rms(x, nw_ref[...]).astype(BF16)
        o_ref[...] = x + x

    h = h_ref[...]
    g = jnp.dot(h, wg_ref[0].astype(BF16), preferred_element_type=F32)
    u = jnp.dot(h, wu_ref[0].astype(BF16), preferred_element_type=F32)
    a = (_silu(g) * u).astype(BF16)
    o_ref[...] += jnp.dot(a, wd_ref[0].astype(BF16), preferred_element_type=F32)

    @pl.when(j == pl.num_programs(1) - 1)
    def _():
        y = 0.5 * o_ref[...]
        if final_norm:
            y = _rms(y, fw_ref[...])
        o_ref[...] = y


def _ffn(x, layer, nw, wg, wu, wd, final_w=None):
    m = x.shape[0]
    tm = min(FFN_TM, m)
    final_norm = final_w is not None
    vec = pl.BlockSpec((1, D_MODEL), lambda i, j: (0, 0))
    in_specs = [
        pl.BlockSpec((tm, D_MODEL), lambda i, j: (i, 0), pipeline_mode=pl.Buffered(1)),
        vec,
        pl.BlockSpec((1, D_MODEL, FFN_TF), lambda i, j: (layer, 0, j)),
        pl.BlockSpec((1, D_MODEL, FFN_TF), lambda i, j: (layer, 0, j)),
        pl.BlockSpec((1, FFN_TF, D_MODEL), lambda i, j: (layer, j, 0)),
    ]
    args = [x, nw.reshape(1, D_MODEL), wg, wu, wd]
    if final_norm:
        in_specs.append(vec)
        args.append(final_w.reshape(1, D_MODEL))
    return pl.pallas_call(
        functools.partial(_ffn_body, final_norm),
        grid=(m // tm, D_FF // FFN_TF),
        in_specs=in_specs,
        out_specs=pl.BlockSpec((tm, D_MODEL), lambda i, j: (i, 0)),
        out_shape=jax.ShapeDtypeStruct((m, D_MODEL), F32),
        scratch_shapes=[pltpu.VMEM((tm, D_MODEL), BF16)],
        compiler_params=_cparams(("arbitrary", "arbitrary")),
        name="ffn",
    )(*args)


def _norm_proj_body(segments, xp_ref, xs_ref, nw_ref, w_ref, b_ref, *o_refs):
    def run(x_ref):
        h = _rms(x_ref[...], nw_ref[...]).astype(BF16)
        col = 0
        for (width, _), o_ref in zip(segments, o_refs):
            for c in range(0, width, TN):
                acc = jnp.dot(h, w_ref[:, col + c:col + c + TN], preferred_element_type=F32)
                o_ref[:, c:c + TN] = (acc + b_ref[:, col + c:col + c + TN]).astype(o_ref.dtype)
            col += width
    _on_row_source(run, (xp_ref, xs_ref))


def _norm_proj(xs, nw, w, b, segments, name):
    n = w.shape[1]
    return pl.pallas_call(
        functools.partial(_norm_proj_body, segments),
        grid=(N_TOK // TM,),
        in_specs=_row_specs(True, D_MODEL) + [
            pl.BlockSpec((1, D_MODEL), lambda i: (0, 0)),
            pl.BlockSpec((D_MODEL, n), lambda i: (0, 0), pipeline_mode=pl.Buffered(1)),
            pl.BlockSpec((1, n), lambda i: (0, 0)),
        ],
        out_specs=[pl.BlockSpec((TM, width), lambda i: (i, 0)) for width, _ in segments],
        out_shape=[jax.ShapeDtypeStruct((N_TOK, width), dtype) for width, dtype in segments],
        compiler_params=_cparams(("arbitrary",)),
        name=name,
    )(*xs, nw.reshape(1, D_MODEL), w, b.reshape(1, n))


def _proj_res_body(ap_ref, as_ref, w_ref, b_ref, rp_ref, rs_ref, op_ref, os_ref):
    def run(a_ref, r_ref, o_ref):
        a = a_ref[...].astype(BF16)
        for c in range(0, D_MODEL, TN):
            acc = jnp.dot(a, w_ref[:, c:c + TN], preferred_element_type=F32)
            o_ref[:, c:c + TN] = r_ref[:, c:c + TN] + acc + b_ref[:, c:c + TN]
    _on_row_source(run, (ap_ref, as_ref), (rp_ref, rs_ref), (op_ref, os_ref))


def _proj_res(a_pair, w, b, res_pair, name):
    k = w.shape[0]
    return pl.pallas_call(
        _proj_res_body,
        grid=(N_TOK // TM,),
        in_specs=_row_specs(True, k) + [
            pl.BlockSpec((k, D_MODEL), lambda i: (0, 0), pipeline_mode=pl.Buffered(1)),
            pl.BlockSpec((1, D_MODEL), lambda i: (0, 0)),
        ] + _row_specs(True, D_MODEL),
        out_specs=_row_specs(True, D_MODEL),
        out_shape=[jax.ShapeDtypeStruct((N_PROMPT, D_MODEL), F32), jax.ShapeDtypeStruct((N_SAMPLE, D_MODEL), F32)],
        compiler_params=_cparams(("arbitrary",)),
        name=name,
    )(*a_pair, w, b.reshape(1, D_MODEL), *res_pair)


def _t5_bucket_table():
    i = np.arange(WINDOW)[None, :]
    j = np.arange(2 * WINDOW)[:, None]
    n = np.maximum(WINDOW + i - j, 0)
    max_exact = NUM_BUCKETS // 2
    nf = np.maximum(n, 1).astype(np.float32)
    large = max_exact + (np.log(nf / np.float32(max_exact)) / np.float32(math.log(MAX_DISTANCE / max_exact))
                         * np.float32(NUM_BUCKETS - max_exact)).astype(np.int32)
    large = np.minimum(large, NUM_BUCKETS - 1)
    return np.where(n < max_exact, n, large).astype(np.int32)


KV_PER_STEP = LANES // HEAD_DIM
N_PAIRS = N_KV_HEADS // KV_PER_STEP
Q_COLS_PER_STEP = KV_PER_STEP * GROUP * HEAD_DIM


def _slot_head(slot):
    pair = slot // (GROUP * KV_PER_STEP)
    g = (slot // KV_PER_STEP) % GROUP
    hh = slot % KV_PER_STEP
    return (pair * KV_PER_STEP + hh) * GROUP + g


def _bias_table_body(bucket_ref, rb_ref, o_ref):
    h = _slot_head(pl.program_id(0))
    bucket = bucket_ref[...]
    acc = jnp.zeros((2 * WINDOW, WINDOW), F32)
    for b in range(NUM_BUCKETS):
        acc = jnp.where(bucket == b, rb_ref[b, h], acc)
    j = lax.broadcasted_iota(jnp.int32, (2 * WINDOW, WINDOW), 0)
    i = lax.broadcasted_iota(jnp.int32, (2 * WINDOW, WINDOW), 1)
    dist = WINDOW + i - j
    o_ref[0] = jnp.where((dist >= 0) & (dist < WINDOW), acc, NEG_INF)


def _bias_table(rel_bias):
    return pl.pallas_call(
        _bias_table_body,
        grid=(N_HEADS,),
        in_specs=[
            pl.BlockSpec((2 * WINDOW, WINDOW), lambda h: (0, 0)),
            pl.BlockSpec(memory_space=pltpu.SMEM),
        ],
        out_specs=pl.BlockSpec((1, 2 * WINDOW, WINDOW), lambda h: (h, 0, 0)),
        out_shape=jax.ShapeDtypeStruct((N_HEADS, 2 * WINDOW, WINDOW), F32),
        name="bias_table",
    )(jnp.asarray(_t5_bucket_table()), rel_bias)


def _softmax_with_sink(s, sink_col):
    m = jnp.maximum(jnp.max(s, axis=-1, keepdims=True), sink_col)
    p = jnp.exp(s - m)
    denom = jnp.sum(p, axis=-1, keepdims=True) + jnp.exp(sink_col - m)
    return p, 1.0 / denom


PAIR_SLOTS = GROUP * KV_PER_STEP
PAIR_COLS = PAIR_SLOTS * WINDOW
ONES_ROWS = 16


def _swa_prompt_body(sink_ref, q_ref, kp_ref, ko_ref, vp_ref, vo_ref, bias_ref, o_ref):
    pair = pl.program_id(0)
    blk = pl.program_id(2)
    head_a = lax.broadcasted_iota(jnp.int32, (WINDOW, LANES), 1) < HEAD_DIM
    k = jnp.concatenate([kp_ref[...], ko_ref[...]], axis=0).astype(BF16)
    v = jnp.concatenate([vp_ref[...], vo_ref[...]], axis=0)
    vt = jnp.concatenate([v.T, jnp.ones((ONES_ROWS, 2 * WINDOW), F32)], axis=0).astype(BF16)
    parts = []
    for g in range(GROUP):
        qg = q_ref[:, g * LANES:(g + 1) * LANES] * HEAD_DIM ** -0.5
        zero = jnp.zeros_like(qg)
        parts += [jnp.where(head_a, qg, zero), jnp.where(head_a, zero, qg)]
    qbd = jnp.concatenate(parts, axis=0)
    st = lax.dot_general(k, qbd, (((1,), (1,)), ((), ())), preferred_element_type=F32)
    st = st + jnp.concatenate([bias_ref[t] for t in range(PAIR_SLOTS)], axis=1)
    no_prev = jnp.where(blk > 0, 0.0, NEG_INF)
    st = jnp.concatenate([st[:WINDOW] + no_prev, st[WINDOW:]], axis=0)
    sink_row = jnp.concatenate(
        [jnp.full((1, WINDOW), sink_ref[_slot_head(pair * PAIR_SLOTS + slot)], F32)
         for slot in range(PAIR_SLOTS)], axis=1)
    m = jnp.maximum(jnp.max(st, axis=0, keepdims=True), sink_row)
    pt = jnp.exp(st - m).astype(BF16)
    oa = jnp.dot(vt, pt, preferred_element_type=F32)
    inv = 1.0 / (oa[LANES:LANES + 1] + jnp.exp(sink_row - m))
    o = oa[:LANES] * inv
    for g in range(GROUP):
        c = g * KV_PER_STEP * WINDOW
        ot = jnp.concatenate([o[:HEAD_DIM, c:c + WINDOW], o[HEAD_DIM:, c + WINDOW:c + 2 * WINDOW]], axis=0)
        o_ref[:, g * LANES:(g + 1) * LANES] = ot.T.astype(o_ref.dtype)


def _swa_prompt(q, kv, bias_tbl, sinks):
    v_col0 = KV_COLS // LANES

    def prev(p, b, i):
        return b * NB + jnp.maximum(i - 1, 0)

    return pl.pallas_call(
        _swa_prompt_body,
        grid=(N_PAIRS, BATCH, NB),
        in_specs=[
            pl.BlockSpec(memory_space=pltpu.SMEM),
            pl.BlockSpec((WINDOW, Q_COLS_PER_STEP), lambda p, b, i: (b * NB + i, p)),
            pl.BlockSpec((WINDOW, LANES), lambda p, b, i: (prev(p, b, i), p)),
            pl.BlockSpec((WINDOW, LANES), lambda p, b, i: (b * NB + i, p)),
            pl.BlockSpec((WINDOW, LANES), lambda p, b, i: (prev(p, b, i), v_col0 + p)),
            pl.BlockSpec((WINDOW, LANES), lambda p, b, i: (b * NB + i, v_col0 + p)),
            pl.BlockSpec((PAIR_SLOTS, 2 * WINDOW, WINDOW), lambda p, b, i: (p, 0, 0)),
        ],
        out_specs=pl.BlockSpec((WINDOW, Q_COLS_PER_STEP), lambda p, b, i: (b * NB + i, p)),
        out_shape=jax.ShapeDtypeStruct((N_PROMPT, Q_DIM), BF16),
        compiler_params=_cparams(("arbitrary", "arbitrary", "arbitrary")),
        name="swa_prompt",
    )(sinks, q, kv, kv, kv, kv, bias_tbl)


S_ROWS = N_KV_HEADS * GROUP * DEC_SEQ


def _swa_sample_body(q_ref, kn_ref, vn_ref, ck_ref, cv_ref, bias_ref, sink_ref,
                     o_ref, ko_ref, vo_ref):
    x = (q_ref[0] * HEAD_DIM ** -0.5).astype(BF16)
    xt = jnp.concatenate([x] * N_KV_HEADS, axis=1)
    row_kv = lax.broadcasted_iota(jnp.int32, (S_ROWS, KV_COLS), 0) // (GROUP * DEC_SEQ)
    col_kv = lax.broadcasted_iota(jnp.int32, (S_ROWS, KV_COLS), 1) // HEAD_DIM
    own = row_kv == col_kv
    qbd = jnp.where(own, xt, jnp.zeros_like(xt))
    pad = jnp.zeros((WINDOW - SUBLANES, KV_COLS), F32)
    kk = jnp.concatenate([ck_ref[0], kn_ref[0], pad], axis=0).astype(BF16)
    vv = jnp.concatenate([cv_ref[0], vn_ref[0], pad], axis=0).astype(BF16)
    s = lax.dot_general(qbd, kk, (((1,), (1,)), ((), ())), preferred_element_type=F32)
    s = s + bias_ref[...]
    p, inv = _softmax_with_sink(s, sink_ref[...])
    of = jnp.dot(p.astype(BF16), vv, preferred_element_type=F32)
    of = jnp.where(own, of, 0.0)
    o = of[:, 0:HEAD_DIM]
    for c in range(1, N_KV_HEADS):
        o = o + of[:, c * HEAD_DIM:(c + 1) * HEAD_DIM]
    o_ref[0] = o * inv
    keep = WINDOW - DEC_SEQ
    ko_ref[0, 0:keep, :] = ck_ref[0, DEC_SEQ:WINDOW, :]
    ko_ref[0, keep:WINDOW, :] = kn_ref[0, 0:DEC_SEQ, :]
    vo_ref[0, 0:keep, :] = cv_ref[0, DEC_SEQ:WINDOW, :]
    vo_ref[0, keep:WINDOW, :] = vn_ref[0, 0:DEC_SEQ, :]


def _swa_sample(q_rows, k_new8, v_new8, cache_k, cache_v, bias_s, sink_col):
    seq3 = lambda s: (s, 0, 0)
    full2 = lambda s: (0, 0)
    return pl.pallas_call(
        _swa_sample_body,
        grid=(DEC_BATCH,),
        in_specs=[
            pl.BlockSpec((1, S_ROWS, HEAD_DIM), seq3),
            pl.BlockSpec((1, SUBLANES, KV_COLS), seq3),
            pl.BlockSpec((1, SUBLANES, KV_COLS), seq3),
            pl.BlockSpec((1, WINDOW, KV_COLS), seq3),
            pl.BlockSpec((1, WINDOW, KV_COLS), seq3),
            pl.BlockSpec((S_ROWS, 2 * WINDOW), full2),
            pl.BlockSpec((S_ROWS, 1), full2),
        ],
        out_specs=[
            pl.BlockSpec((1, S_ROWS, HEAD_DIM), seq3),
            pl.BlockSpec((1, WINDOW, KV_COLS), seq3),
            pl.BlockSpec((1, WINDOW, KV_COLS), seq3),
        ],
        out_shape=[
            jax.ShapeDtypeStruct((DEC_BATCH, S_ROWS, HEAD_DIM), F32),
            jax.ShapeDtypeStruct((DEC_BATCH, WINDOW, KV_COLS), F32),
            jax.ShapeDtypeStruct((DEC_BATCH, WINDOW, KV_COLS), F32),
        ],
        compiler_params=_cparams(("parallel",)),
        name="swa_sample",
    )(q_rows, k_new8, v_new8, cache_k, cache_v, bias_s, sink_col)


def _log_sigmoid(x):
    return jnp.minimum(x, 0.0) - jnp.log1p(jnp.exp(-jnp.abs(x)))


def _gla_gate_body(xp_ref, xs_ref, nw_ref, w1_ref, w2_ref, b_ref, o_ref):
    def run(x_ref):
        h = _rms(x_ref[...], nw_ref[...]).astype(BF16)
        gk = jnp.dot(h, w1_ref[...], preferred_element_type=F32)
        z = jnp.dot(gk.astype(BF16), w2_ref[...], preferred_element_type=F32) + b_ref[...]
        o_ref[...] = _log_sigmoid(z) / GATE_NORMALIZER
    _on_row_source(run, (xp_ref, xs_ref))


def _gla_gate(xs, nw, w1p, w2p, b):
    m = N_TOK
    return pl.pallas_call(
        _gla_gate_body,
        grid=(m // TM,),
        in_specs=_row_specs(True, D_MODEL) + [
            pl.BlockSpec((1, D_MODEL), lambda i: (0, 0)),
            pl.BlockSpec((D_MODEL, LANES), lambda i: (0, 0)),
            pl.BlockSpec((LANES, GLA_KEY_DIM), lambda i: (0, 0)),
            pl.BlockSpec((1, GLA_KEY_DIM), lambda i: (0, 0)),
        ],
        out_specs=pl.BlockSpec((TM, GLA_KEY_DIM), lambda i: (i, 0)),
        out_shape=jax.ShapeDtypeStruct((m, GLA_KEY_DIM), F32),
        compiler_params=_cparams(("parallel",)),
        name="gla_gate",
    )(*xs, nw.reshape(1, D_MODEL), w1p, w2p, b.reshape(1, GLA_KEY_DIM))


def _gla_out(o, gate, norm_w):
    return _rms(o, norm_w) * _silu(gate)


def _cumsum_rows(g):
    c = g.shape[0]
    tri = (lax.broadcasted_iota(jnp.int32, (c, c), 0) >= lax.broadcasted_iota(jnp.int32, (c, c), 1)).astype(F32)
    return jnp.dot(tri, g, precision=lax.Precision.HIGHEST, preferred_element_type=F32)


def _causal(a):
    c = a.shape[0]
    keep = lax.broadcasted_iota(jnp.int32, (c, c), 0) >= lax.broadcasted_iota(jnp.int32, (c, c), 1)
    return jnp.where(keep, a, 0.0)


_NT = (((1,), (1,)), ((), ()))
_TN = (((0,), (0,)), ((), ()))


def _gla_prompt_body(*refs):
    proj_refs = refs[:BATCH]
    la_refs = refs[BATCH:2 * BATCH]
    nw_ref, o_ref, s_ref, st_ref = refs[2 * BATCH:]
    c = pl.program_id(0)

    @pl.when(c == 0)
    def _():
        st_ref[...] = jnp.zeros_like(st_ref)

    for bi in range(BATCH):
        p_ref = proj_refs[bi]
        b_all = _cumsum_rows(la_refs[bi][...])
        for h in range(GLA_HEADS):
            kc = slice(h * GLA_DK, (h + 1) * GLA_DK)
            v0 = 2 * GLA_KEY_DIM + h * GLA_DV
            b = b_all[:, kc]
            q = p_ref[:, kc].astype(F32)
            k = p_ref[:, GLA_KEY_DIM + h * GLA_DK:GLA_KEY_DIM + (h + 1) * GLA_DK].astype(F32)
            v = p_ref[:, v0:v0 + GLA_DV]
            gate = p_ref[:, v0 + GLA_VAL_DIM:v0 + GLA_VAL_DIM + GLA_DV].astype(F32)
            qe = (q * GLA_DK ** -0.5 * jnp.exp(b)).astype(BF16)
            ke = (k * jnp.exp(-b)).astype(BF16)
            a = _causal(lax.dot_general(qe, ke, _NT, preferred_element_type=F32))
            st = st_ref[bi, h]
            o = (jnp.dot(a.astype(BF16), v, preferred_element_type=F32)
                 + lax.dot_general(qe, st.astype(BF16), _NT, preferred_element_type=F32))
            b_last = b[GLA_C - 1:GLA_C, :]
            kd = (k * jnp.exp(b_last - b)).astype(BF16)
            st_new = st * jnp.exp(b_last) + lax.dot_general(v, kd, _TN, preferred_element_type=F32)
            st_ref[bi, h] = st_new
            o_ref[bi, :, h * GLA_DV:(h + 1) * GLA_DV] = _gla_out(o, gate, nw_ref[...]).astype(o_ref.dtype)

    @pl.when(c == pl.num_programs(0) - 1)
    def _():
        for bi in range(BATCH):
            for h in range(GLA_HEADS):
                s_ref[bi, h] = st_ref[bi, h].T


def _gla_prompt(proj, log_a, norm_w):
    nc = SEQ // GLA_C
    rows = [functools.partial(lambda bi, c: (bi * nc + c, 0), bi) for bi in range(BATCH)]
    o, s = pl.pallas_call(
        _gla_prompt_body,
        grid=(nc,),
        in_specs=([pl.BlockSpec((GLA_C, GLA_MAIN_DIM), r) for r in rows]
                  + [pl.BlockSpec((GLA_C, GLA_KEY_DIM), r) for r in rows]
                  + [pl.BlockSpec((1, GLA_DV), lambda c: (0, 0))]),
        out_specs=[
            pl.BlockSpec((BATCH, GLA_C, GLA_VAL_DIM), lambda c: (0, c, 0)),
            pl.BlockSpec((BATCH, GLA_HEADS, GLA_DK, GLA_DV), lambda c: (0, 0, 0, 0)),
        ],
        out_shape=[
            jax.ShapeDtypeStruct((BATCH, SEQ, GLA_VAL_DIM), BF16),
            jax.ShapeDtypeStruct((BATCH, GLA_HEADS, GLA_DK, GLA_DV), F32),
        ],
        scratch_shapes=[pltpu.VMEM((BATCH, GLA_HEADS, GLA_DV, GLA_DK), F32)],
        compiler_params=_cparams(("arbitrary",)),
        name="gla_prompt",
    )(*([proj] * BATCH + [log_a] * BATCH + [norm_w.reshape(1, GLA_DV)]))
    return o.reshape(N_PROMPT, GLA_VAL_DIM), s


def _gla_sample_body(proj_ref, la_ref, s0_ref, nw_ref, o_ref, s_ref):
    ones = jnp.ones((DEC_SEQ, LANES), F32)
    for h in range(GLA_HEADS):
        q = proj_ref[0, :, h * GLA_DK:(h + 1) * GLA_DK]
        k = proj_ref[0, :, GLA_KEY_DIM + h * GLA_DK:GLA_KEY_DIM + (h + 1) * GLA_DK]
        v0 = 2 * GLA_KEY_DIM + h * GLA_DV
        v = proj_ref[0, :, v0:v0 + GLA_DV].astype(BF16)
        gate = proj_ref[0, :, v0 + GLA_VAL_DIM:v0 + GLA_VAL_DIM + GLA_DV]
        g = la_ref[0, :, h * GLA_DK:(h + 1) * GLA_DK]
        b = _cumsum_rows(g)
        qe = (q * GLA_DK ** -0.5 * jnp.exp(b)).astype(BF16)
        ke = (k * jnp.exp(-b)).astype(BF16)
        a = _causal(lax.dot_general(qe, ke, _NT, preferred_element_type=F32))
        s0 = s0_ref[0, h]
        o = (jnp.dot(a.astype(BF16), v, preferred_element_type=F32)
             + jnp.dot(qe, s0.astype(BF16), preferred_element_type=F32))
        b_last = b[DEC_SEQ - 1:DEC_SEQ, :]
        kd = (k * jnp.exp(b_last - b)).astype(BF16)
        dcol = jnp.exp(lax.dot_general(g, ones, _TN, precision=lax.Precision.HIGHEST,
                                       preferred_element_type=F32))
        decay = jnp.concatenate([dcol] * (GLA_DV // LANES), axis=1)
        s_ref[0, h] = s0 * decay + lax.dot_general(kd, v, _TN, preferred_element_type=F32)
        o_ref[0, :, h * GLA_DV:(h + 1) * GLA_DV] = _gla_out(o, gate, nw_ref[...])


def _gla_sample(proj3, log_a3, state, norm_w):
    seq3 = lambda s: (s, 0, 0)
    seq4 = lambda s: (s, 0, 0, 0)
    return pl.pallas_call(
        _gla_sample_body,
        grid=(DEC_BATCH,),
        in_specs=[
            pl.BlockSpec((1, DEC_SEQ, GLA_MAIN_DIM), seq3),
            pl.BlockSpec((1, DEC_SEQ, GLA_KEY_DIM), seq3),
            pl.BlockSpec((1, GLA_HEADS, GLA_DK, GLA_DV), seq4),
            pl.BlockSpec((1, GLA_DV), lambda s: (0, 0)),
        ],
        out_specs=[
            pl.BlockSpec((1, DEC_SEQ, GLA_VAL_DIM), seq3),
            pl.BlockSpec((1, GLA_HEADS, GLA_DK, GLA_DV), seq4),
        ],
        out_shape=[
            jax.ShapeDtypeStruct((DEC_BATCH, DEC_SEQ, GLA_VAL_DIM), F32),
            jax.ShapeDtypeStruct((DEC_BATCH, GLA_HEADS, GLA_DK, GLA_DV), F32),
        ],
        compiler_params=_cparams(("parallel",)),
        name="gla_sample",
    )(proj3, log_a3, state, norm_w.reshape(1, GLA_DV))


def _swa_layer(x, cache_k, cache_v, norm_w, w_qkv, b_qkv, w_o, b_o, sinks, rel_bias):
    slots = (N_PAIRS, KV_PER_STEP, GROUP, HEAD_DIM)
    w_q = w_qkv[:, :Q_DIM].reshape((D_MODEL,) + slots).transpose(0, 1, 3, 2, 4).reshape(D_MODEL, Q_DIM)
    b_q = b_qkv[:Q_DIM].reshape(slots).transpose(0, 2, 1, 3).reshape(Q_DIM)
    w_qkv_s = jnp.concatenate([w_q, w_qkv[:, Q_DIM:]], axis=1).astype(BF16)
    b_qkv_s = jnp.concatenate([b_q, b_qkv[Q_DIM:]])
    w_o_s = w_o.reshape(slots + (D_MODEL,)).transpose(0, 2, 1, 3, 4).reshape(Q_DIM, D_MODEL).astype(BF16)

    q, kv = _norm_proj(x, norm_w, w_qkv_s, b_qkv_s, ((Q_DIM, BF16), (2 * KV_COLS, F32)), "swa_qkv")
    bias_tbl = _bias_table(rel_bias)
    o_p = _swa_prompt(q, kv, bias_tbl, sinks)

    q_rows = (q[N_PROMPT:].astype(F32).reshape(DEC_BATCH, DEC_SEQ, N_PAIRS, GROUP, KV_PER_STEP, HEAD_DIM)
              .transpose(0, 2, 4, 3, 1, 5).reshape(DEC_BATCH, S_ROWS, HEAD_DIM))
    kv_s = kv[N_PROMPT:].reshape(DEC_BATCH, DEC_SEQ, 2 * KV_COLS)
    pad8 = ((0, 0), (0, SUBLANES - DEC_SEQ), (0, 0))
    k_new8 = jnp.pad(kv_s[..., :KV_COLS], pad8)
    v_new8 = jnp.pad(kv_s[..., KV_COLS:], pad8)
    bias_s = (bias_tbl[:, :, :DEC_SEQ].reshape(N_PAIRS, GROUP, KV_PER_STEP, 2 * WINDOW, DEC_SEQ)
              .transpose(0, 2, 1, 4, 3).reshape(S_ROWS, 2 * WINDOW))
    sink_col = jnp.repeat(sinks, DEC_SEQ).reshape(S_ROWS, 1)
    o_s, k_s, v_s = _swa_sample(q_rows, k_new8, v_new8,
                                cache_k.reshape(DEC_BATCH, WINDOW, KV_COLS),
                                cache_v.reshape(DEC_BATCH, WINDOW, KV_COLS), bias_s, sink_col)
    o_s = (o_s.reshape(DEC_BATCH, N_PAIRS, KV_PER_STEP, GROUP, DEC_SEQ, HEAD_DIM)
           .transpose(0, 4, 1, 3, 2, 5).reshape(N_SAMPLE, Q_DIM))
    x = _proj_res((o_p, o_s), w_o_s, b_o, x, "swa_out")

    kv_p = kv[:N_PROMPT].reshape(BATCH, SEQ, 2 * KV_COLS)[:, SEQ - WINDOW:]
    k_p = kv_p[..., :KV_COLS].reshape(BATCH, WINDOW, N_KV_HEADS, HEAD_DIM)
    v_p = kv_p[..., KV_COLS:].reshape(BATCH, WINDOW, N_KV_HEADS, HEAD_DIM)
    shape_s = (DEC_BATCH, WINDOW, N_KV_HEADS, HEAD_DIM)
    return x, k_p, v_p, k_s.reshape(shape_s), v_s.reshape(shape_s)


def _gla_layer(x, state, norm_w, w_in, w_gk2, b_gk, gnorm, w_o):
    w_main = w_in[:, :GLA_MAIN_DIM].astype(BF16)
    w1p = jnp.pad(w_in[:, GLA_MAIN_DIM:], ((0, 0), (0, LANES - GATE_RANK))).astype(BF16)
    w2p = jnp.pad(w_gk2, ((0, LANES - GATE_RANK), (0, 0))).astype(BF16)
    (proj,) = _norm_proj(x, norm_w, w_main, jnp.zeros((GLA_MAIN_DIM,), F32), ((GLA_MAIN_DIM, BF16),), "gla_in")
    log_a = _gla_gate(x, norm_w, w1p, w2p, b_gk)
    o_p, s_p = _gla_prompt(proj, log_a, gnorm)
    proj_s = proj[N_PROMPT:].astype(F32).reshape(DEC_BATCH, DEC_SEQ, GLA_MAIN_DIM)
    log_a_s = log_a[N_PROMPT:].reshape(DEC_BATCH, DEC_SEQ, GLA_KEY_DIM)
    o_s, s_s = _gla_sample(proj_s, log_a_s, state, gnorm)
    x = _proj_res((o_p, o_s.reshape(N_SAMPLE, GLA_VAL_DIM)), w_o.astype(BF16), jnp.zeros((D_MODEL,), F32), x,
                  "gla_out")
    return x, s_p, s_s


def kernel(x_prompt, x_sample, cache_swa_k, cache_swa_v, state_gla, norm_ffn1, ffn1_w_gate, ffn1_w_up,
           ffn1_w_down, norm_mix, norm_ffn2, ffn2_w_gate, ffn2_w_up, ffn2_w_down, norm_final, rel_bias,
           swa_w_qkv, swa_b_qkv, swa_w_o, swa_b_o, swa_sinks, gla_w_in, gla_w_gk2, gla_b_gk, gla_norm,
           gla_w_o):
    x = (x_prompt.reshape(N_PROMPT, D_MODEL), x_sample.reshape(N_SAMPLE, D_MODEL))
    swa_kp, swa_vp, swa_ks, swa_vs, gla_sp, gla_ss = [], [], [], [], [], []
    for i in range(DEPTH):
        x = [_ffn(part, i, norm_ffn1[i], ffn1_w_gate, ffn1_w_up, ffn1_w_down) for part in x]
        j = i // 2
        if i % 2 == 0:
            x, kp, vp, ks, vs = _swa_layer(x, cache_swa_k[j], cache_swa_v[j], norm_mix[i], swa_w_qkv[j],
                                           swa_b_qkv[j], swa_w_o[j], swa_b_o[j], swa_sinks[j], rel_bias)
            swa_kp.append(kp)
            swa_vp.append(vp)
            swa_ks.append(ks)
            swa_vs.append(vs)
        else:
            x, sp, ss = _gla_layer(x, state_gla[j], norm_mix[i], gla_w_in[j], gla_w_gk2[j], gla_b_gk[j],
                                   gla_norm[j], gla_w_o[j])
            gla_sp.append(sp)
            gla_ss.append(ss)
        final_w = norm_final if i == DEPTH - 1 else None
        x = [_ffn(part, i, norm_ffn2[i], ffn2_w_gate, ffn2_w_up, ffn2_w_down, final_w) for part in x]
    y_prompt = x[0].reshape(BATCH, SEQ, D_MODEL)
    y_sample = x[1].reshape(DEC_BATCH, DEC_SEQ, D_MODEL)
    return (y_prompt, y_sample, jnp.stack(swa_kp), jnp.stack(swa_vp), jnp.stack(swa_ks), jnp.stack(swa_vs),
            jnp.stack(gla_sp), jnp.stack(gla_ss))
```

```python
import functools
import math

import numpy as np
import jax
import jax.numpy as jnp
from jax import lax
from jax.experimental import pallas as pl
from jax.experimental.pallas import tpu as pltpu

F32 = jnp.float32
BF16 = jnp.bfloat16

D_MODEL = 2048
BATCH = 2
SEQ = 4096
DEPTH = 2
DEC_BATCH = 128
DEC_SEQ = 4
RMS_EPS = 1e-6
D_FF = 5632
N_HEADS = 32
N_KV_HEADS = 8
HEAD_DIM = 64
GROUP = N_HEADS // N_KV_HEADS
WINDOW = 128
NUM_BUCKETS = 32
MAX_DISTANCE = 128
NEG_INF = -1e30
GLA_HEADS = 4
GLA_DK = 256
GLA_DV = 512
GLA_KEY_DIM = GLA_HEADS * GLA_DK
GLA_VAL_DIM = GLA_HEADS * GLA_DV
GATE_RANK = 16
GATE_NORMALIZER = 16.0
GLA_MAIN_DIM = 2 * GLA_KEY_DIM + 2 * GLA_VAL_DIM
Q_DIM = N_HEADS * HEAD_DIM
KV_COLS = N_KV_HEADS * HEAD_DIM

N_PROMPT = BATCH * SEQ
N_SAMPLE = DEC_BATCH * DEC_SEQ
N_TOK = N_PROMPT + N_SAMPLE

LANES = 128
SUBLANES = 8
VMEM_LIMIT = 56 * 1024 * 1024

TM = 512
TF = 512
TN = 512
GLA_C = 64
NB = SEQ // WINDOW
N_PROMPT_TILES = N_PROMPT // TM


def _rms(x, w):
    return x * lax.rsqrt(jnp.mean(x * x, axis=-1, keepdims=True) + RMS_EPS) * w


def _silu(x):
    return x * jax.nn.sigmoid(x)


def _cparams(sem):
    return pltpu.CompilerParams(dimension_semantics=sem, vmem_limit_bytes=VMEM_LIMIT)


def _row_specs(split, width):
    if not split:
        return [pl.BlockSpec((TM, width), lambda i, *_: (i, 0))]
    return [pl.BlockSpec((TM, width), lambda i, *_: (jnp.minimum(i, N_PROMPT_TILES - 1), 0)),
            pl.BlockSpec((TM, width), lambda i, *_: (jnp.maximum(i - N_PROMPT_TILES, 0), 0))]


def _on_row_source(fn, *ref_groups):
    if all(len(g) == 1 for g in ref_groups):
        fn(*[g[0] for g in ref_groups])
        return
    i = pl.program_id(0)
    pl.when(i < N_PROMPT_TILES)(lambda: fn(*[g[0] for g in ref_groups]))
    pl.when(i >= N_PROMPT_TILES)(lambda: fn(*[g[-1] for g in ref_groups]))


FFN_TM = 1024
FFN_TF = 256
RELAYOUT_ROWS = 256


def _tile_major_body(w_ref, o_ref):
    for t in range(D_FF // FFN_TF):
        o_ref[t] = w_ref[0, :, t * FFN_TF:(t + 1) * FFN_TF].astype(BF16)


def _tile_major_bf16(w, layer):
    nt = D_FF // FFN_TF
    return pl.pallas_call(
        _tile_major_body,
        grid=(D_MODEL // RELAYOUT_ROWS,),
        in_specs=[pl.BlockSpec((1, RELAYOUT_ROWS, D_FF), lambda i: (layer, i, 0))],
        out_specs=pl.BlockSpec((nt, RELAYOUT_ROWS, FFN_TF), lambda i: (0, i, 0)),
        out_shape=jax.ShapeDtypeStruct((nt, D_MODEL, FFN_TF), BF16),
        compiler_params=_cparams(("arbitrary",)),
        name="tile_major_bf16",
    )(w)


def _ffn_body(final_norm, x_ref, nw_ref, wg_ref, wu_ref, wd_ref, *rest):
    if final_norm:
        fw_ref, o_ref, h_ref = rest
    else:
        o_ref, h_ref = rest
    j = pl.program_id(1)

    @pl.when(j == 0)
    def _():
        x = x_ref[...]
        h_ref[...] = _rms(x, nw_ref[...]).astype(BF16)
        o_ref[...] = x + x

    h = h_ref[...]
    g = jnp.dot(h, wg_ref[0], preferred_element_type=F32)
    u = jnp.dot(h, wu_ref[0], preferred_element_type=F32)
    a = (_silu(g) * u).astype(BF16)
    o_ref[...] += jnp.dot(a, wd_ref[0].astype(BF16), preferred_element_type=F32)

    @pl.when(j == pl.num_programs(1) - 1)
    def _():
        y = 0.5 * o_ref[...]
        if final_norm:
            y = _rms(y, fw_ref[...])
        o_ref[...] = y


def _ffn(x, layer, nw, wg, wu, wd, final_w=None):
    m = x.shape[0]
    tm = min(FFN_TM, m)
    final_norm = final_w is not None
    vec = pl.BlockSpec((1, D_MODEL), lambda i, j: (0, 0))
    in_specs = [
        pl.BlockSpec((tm, D_MODEL), lambda i, j: (i, 0), pipeline_mode=pl.Buffered(1)),
        vec,
        pl.BlockSpec((1, D_MODEL, FFN_TF), lambda i, j: (j, 0, 0)),
        pl.BlockSpec((1, D_MODEL, FFN_TF), lambda i, j: (j, 0, 0)),
        pl.BlockSpec((1, FFN_TF, D_MODEL), lambda i, j: (layer, j, 0)),
    ]
    args = [x, nw.reshape(1, D_MODEL), wg, wu, wd]
    if final_norm:
        in_specs.append(vec)
        args.append(final_w.reshape(1, D_MODEL))
    return pl.pallas_call(
        functools.partial(_ffn_body, final_norm),
        grid=(m // tm, D_FF // FFN_TF),
        in_specs=in_specs,
        out_specs=pl.BlockSpec((tm, D_MODEL), lambda i, j: (i, 0)),
        out_shape=jax.ShapeDtypeStruct((m, D_MODEL), F32),
        scratch_shapes=[pltpu.VMEM((tm, D_MODEL), BF16)],
        compiler_params=_cparams(("arbitrary", "arbitrary")),
        name="ffn",
    )(*args)


def _norm_proj_body(segments, xp_ref, xs_ref, nw_ref, w_ref, b_ref, *o_refs):
    def run(x_ref):
        h = _rms(x_ref[...], nw_ref[...]).astype(BF16)
        col = 0
        for (width, _), o_ref in zip(segments, o_refs):
            for c in range(0, width, TN):
                acc = jnp.dot(h, w_ref[:, col + c:col + c + TN], preferred_element_type=F32)
                o_ref[:, c:c + TN] = (acc + b_ref[:, col + c:col + c + TN]).astype(o_ref.dtype)
            col += width
    _on_row_source(run, (xp_ref, xs_ref))


def _norm_proj(xs, nw, w, b, segments, name):
    n = w.shape[1]
    return pl.pallas_call(
        functools.partial(_norm_proj_body, segments),
        grid=(N_TOK // TM,),
        in_specs=_row_specs(True, D_MODEL) + [
            pl.BlockSpec((1, D_MODEL), lambda i: (0, 0)),
            pl.BlockSpec((D_MODEL, n), lambda i: (0, 0), pipeline_mode=pl.Buffered(1)),
            pl.BlockSpec((1, n), lambda i: (0, 0)),
        ],
        out_specs=[pl.BlockSpec((TM, width), lambda i: (i, 0)) for width, _ in segments],
        out_shape=[jax.ShapeDtypeStruct((N_TOK, width), dtype) for width, dtype in segments],
        compiler_params=_cparams(("arbitrary",)),
        name=name,
    )(*xs, nw.reshape(1, D_MODEL), w, b.reshape(1, n))


def _proj_res_body(ap_ref, as_ref, w_ref, b_ref, rp_ref, rs_ref, op_ref, os_ref):
    def run(a_ref, r_ref, o_ref):
        a = a_ref[...].astype(BF16)
        for c in range(0, D_MODEL, TN):
            acc = jnp.dot(a, w_ref[:, c:c + TN], preferred_element_type=F32)
            o_ref[:, c:c + TN] = r_ref[:, c:c + TN] + acc + b_ref[:, c:c + TN]
    _on_row_source(run, (ap_ref, as_ref), (rp_ref, rs_ref), (op_ref, os_ref))


def _proj_res(a_pair, w, b, res_pair, name):
    k = w.shape[0]
    return pl.pallas_call(
        _proj_res_body,
        grid=(N_TOK // TM,),
        in_specs=_row_specs(True, k) + [
            pl.BlockSpec((k, D_MODEL), lambda i: (0, 0), pipeline_mode=pl.Buffered(1)),
            pl.BlockSpec((1, D_MODEL), lambda i: (0, 0)),
        ] + _row_specs(True, D_MODEL),
        out_specs=_row_specs(True, D_MODEL),
        out_shape=[jax.ShapeDtypeStruct((N_PROMPT, D_MODEL), F32), jax.ShapeDtypeStruct((N_SAMPLE, D_MODEL), F32)],
        compiler_params=_cparams(("arbitrary",)),
        name=name,
    )(*a_pair, w, b.reshape(1, D_MODEL), *res_pair)


def _t5_bucket_table():
    i = np.arange(WINDOW)[None, :]
    j = np.arange(2 * WINDOW)[:, None]
    n = np.maximum(WINDOW + i - j, 0)
    max_exact = NUM_BUCKETS // 2
    nf = np.maximum(n, 1).astype(np.float32)
    large = max_exact + (np.log(nf / np.float32(max_exact)) / np.float32(math.log(MAX_DISTANCE / max_exact))
                         * np.float32(NUM_BUCKETS - max_exact)).astype(np.int32)
    large = np.minimum(large, NUM_BUCKETS - 1)
    return np.where(n < max_exact, n, large).astype(np.int32)


KV_PER_STEP = LANES // HEAD_DIM
N_PAIRS = N_KV_HEADS // KV_PER_STEP
Q_COLS_PER_STEP = KV_PER_STEP * GROUP * HEAD_DIM


def _slot_head(slot):
    pair = slot // (GROUP * KV_PER_STEP)
    g = (slot // KV_PER_STEP) % GROUP
    hh = slot % KV_PER_STEP
    return (pair * KV_PER_STEP + hh) * GROUP + g


def _bias_table_body(bucket_ref, rb_ref, o_ref):
    h = _slot_head(pl.program_id(0))
    bucket = bucket_ref[...]
    acc = jnp.zeros((2 * WINDOW, WINDOW), F32)
    for b in range(NUM_BUCKETS):
        acc = jnp.where(bucket == b, rb_ref[b, h], acc)
    j = lax.broadcasted_iota(jnp.int32, (2 * WINDOW, WINDOW), 0)
    i = lax.broadcasted_iota(jnp.int32, (2 * WINDOW, WINDOW), 1)
    dist = WINDOW + i - j
    o_ref[0] = jnp.where((dist >= 0) & (dist < WINDOW), acc, NEG_INF)


def _bias_table(rel_bias):
    return pl.pallas_call(
        _bias_table_body,
        grid=(N_HEADS,),
        in_specs=[
            pl.BlockSpec((2 * WINDOW, WINDOW), lambda h: (0, 0)),
            pl.BlockSpec(memory_space=pltpu.SMEM),
        ],
        out_specs=pl.BlockSpec((1, 2 * WINDOW, WINDOW), lambda h: (h, 0, 0)),
        out_shape=jax.ShapeDtypeStruct((N_HEADS, 2 * WINDOW, WINDOW), F32),
        name="bias_table",
    )(jnp.asarray(_t5_bucket_table()), rel_bias)


def _softmax_with_sink(s, sink_col):
    m = jnp.maximum(jnp.max(s, axis=-1, keepdims=True), sink_col)
    p = jnp.exp(s - m)
    denom = jnp.sum(p, axis=-1, keepdims=True) + jnp.exp(sink_col - m)
    return p, 1.0 / denom


PAIR_SLOTS = GROUP * KV_PER_STEP
PAIR_COLS = PAIR_SLOTS * WINDOW
ONES_ROWS = 16


def _swa_prompt_body(sink_ref, q_ref, kp_ref, ko_ref, vp_ref, vo_ref, bias_ref, o_ref):
    pair = pl.program_id(0)
    blk = pl.program_id(2)
    head_a = lax.broadcasted_iota(jnp.int32, (WINDOW, LANES), 1) < HEAD_DIM
    k = jnp.concatenate([kp_ref[...], ko_ref[...]], axis=0).astype(BF16)
    v = jnp.concatenate([vp_ref[...], vo_ref[...]], axis=0)
    vt = jnp.concatenate([v.T, jnp.ones((ONES_ROWS, 2 * WINDOW), F32)], axis=0).astype(BF16)
    parts = []
    for g in range(GROUP):
        qg = q_ref[:, g * LANES:(g + 1) * LANES] * HEAD_DIM ** -0.5
        zero = jnp.zeros_like(qg)
        parts += [jnp.where(head_a, qg, zero), jnp.where(head_a, zero, qg)]
    qbd = jnp.concatenate(parts, axis=0)
    st = lax.dot_general(k, qbd, (((1,), (1,)), ((), ())), preferred_element_type=F32)
    st = st + jnp.concatenate([bias_ref[t] for t in range(PAIR_SLOTS)], axis=1)
    no_prev = jnp.where(blk > 0, 0.0, NEG_INF)
    st = jnp.concatenate([st[:WINDOW] + no_prev, st[WINDOW:]], axis=0)
    sink_row = jnp.concatenate(
        [jnp.full((1, WINDOW), sink_ref[_slot_head(pair * PAIR_SLOTS + slot)], F32)
         for slot in range(PAIR_SLOTS)], axis=1)
    m = jnp.maximum(jnp.max(st, axis=0, keepdims=True), sink_row)
    pt = jnp.exp(st - m).astype(BF16)
    oa = jnp.dot(vt, pt, preferred_element_type=F32)
    inv = 1.0 / (oa[LANES:LANES + 1] + jnp.exp(sink_row - m))
    o = oa[:LANES] * inv
    for g in range(GROUP):
        c = g * KV_PER_STEP * WINDOW
        ot = jnp.concatenate([o[:HEAD_DIM, c:c + WINDOW], o[HEAD_DIM:, c + WINDOW:c + 2 * WINDOW]], axis=0)
        o_ref[:, g * LANES:(g + 1) * LANES] = ot.T.astype(o_ref.dtype)


def _swa_prompt(q, kv, bias_tbl, sinks):
    v_col0 = KV_COLS // LANES

    def prev(p, b, i):
        return b * NB + jnp.maximum(i - 1, 0)

    return pl.pallas_call(
        _swa_prompt_body,
        grid=(N_PAIRS, BATCH, NB),
        in_specs=[
            pl.BlockSpec(memory_space=pltpu.SMEM),
            pl.BlockSpec((WINDOW, Q_COLS_PER_STEP), lambda p, b, i: (b * NB + i, p)),
            pl.BlockSpec((WINDOW, LANES), lambda p, b, i: (prev(p, b, i), p)),
            pl.BlockSpec((WINDOW, LANES), lambda p, b, i: (b * NB + i, p)),
            pl.BlockSpec((WINDOW, LANES), lambda p, b, i: (prev(p, b, i), v_col0 + p)),
            pl.BlockSpec((WINDOW, LANES), lambda p, b, i: (b * NB + i, v_col0 + p)),
            pl.BlockSpec((PAIR_SLOTS, 2 * WINDOW, WINDOW), lambda p, b, i: (p, 0, 0)),
        ],
        out_specs=pl.BlockSpec((WINDOW, Q_COLS_PER_STEP), lambda p, b, i: (b * NB + i, p)),
        out_shape=jax.ShapeDtypeStruct((N_PROMPT, Q_DIM), BF16),
        compiler_params=_cparams(("arbitrary", "arbitrary", "arbitrary")),
        name="swa_prompt",
    )(sinks, q, kv, kv, kv, kv, bias_tbl)


S_ROWS = N_KV_HEADS * GROUP * DEC_SEQ


def _swa_sample_body(q_ref, kn_ref, vn_ref, ck_ref, cv_ref, bias_ref, sink_ref,
                     o_ref, ko_ref, vo_ref):
    x = (q_ref[0] * HEAD_DIM ** -0.5).astype(BF16)
    xt = jnp.concatenate([x] * N_KV_HEADS, axis=1)
    row_kv = lax.broadcasted_iota(jnp.int32, (S_ROWS, KV_COLS), 0) // (GROUP * DEC_SEQ)
    col_kv = lax.broadcasted_iota(jnp.int32, (S_ROWS, KV_COLS), 1) // HEAD_DIM
    own = row_kv == col_kv
    qbd = jnp.where(own, xt, jnp.zeros_like(xt))
    pad = jnp.zeros((WINDOW - SUBLANES, KV_COLS), F32)
    kk = jnp.concatenate([ck_ref[0], kn_ref[0], pad], axis=0).astype(BF16)
    vv = jnp.concatenate([cv_ref[0], vn_ref[0], pad], axis=0).astype(BF16)
    s = lax.dot_general(qbd, kk, (((1,), (1,)), ((), ())), preferred_element_type=F32)
    s = s + bias_ref[...]
    p, inv = _softmax_with_sink(s, sink_ref[...])
    of = jnp.dot(p.astype(BF16), vv, preferred_element_type=F32)
    of = jnp.where(own, of, 0.0)
    o = of[:, 0:HEAD_DIM]
    for c in range(1, N_KV_HEADS):
        o = o + of[:, c * HEAD_DIM:(c + 1) * HEAD_DIM]
    o_ref[0] = o * inv
    keep = WINDOW - DEC_SEQ
    ko_ref[0, 0:keep, :] = ck_ref[0, DEC_SEQ:WINDOW, :]
    ko_ref[0, keep:WINDOW, :] = kn_ref[0, 0:DEC_SEQ, :]
    vo_ref[0, 0:keep, :] = cv_ref[0, DEC_SEQ:WINDOW, :]
    vo_ref[0, keep:WINDOW, :] = vn_ref[0, 0:DEC_SEQ, :]


def _swa_sample(q_rows, k_new8, v_new8, cache_k, cache_v, bias_s, sink_col):
    seq3 = lambda s: (s, 0, 0)
    full2 = lambda s: (0, 0)
    return pl.pallas_call(
        _swa_sample_body,
        grid=(DEC_BATCH,),
        in_specs=[
            pl.BlockSpec((1, S_ROWS, HEAD_DIM), seq3),
            pl.BlockSpec((1, SUBLANES, KV_COLS), seq3),
            pl.BlockSpec((1, SUBLANES, KV_COLS), seq3),
            pl.BlockSpec((1, WINDOW, KV_COLS), seq3),
            pl.BlockSpec((1, WINDOW, KV_COLS), seq3),
            pl.BlockSpec((S_ROWS, 2 * WINDOW), full2),
            pl.BlockSpec((S_ROWS, 1), full2),
        ],
        out_specs=[
            pl.BlockSpec((1, S_ROWS, HEAD_DIM), seq3),
            pl.BlockSpec((1, WINDOW, KV_COLS), seq3),
            pl.BlockSpec((1, WINDOW, KV_COLS), seq3),
        ],
        out_shape=[
            jax.ShapeDtypeStruct((DEC_BATCH, S_ROWS, HEAD_DIM), F32),
            jax.ShapeDtypeStruct((DEC_BATCH, WINDOW, KV_COLS), F32),
            jax.ShapeDtypeStruct((DEC_BATCH, WINDOW, KV_COLS), F32),
        ],
        compiler_params=_cparams(("parallel",)),
        name="swa_sample",
    )(q_rows, k_new8, v_new8, cache_k, cache_v, bias_s, sink_col)


def _log_sigmoid(x):
    return jnp.minimum(x, 0.0) - jnp.log1p(jnp.exp(-jnp.abs(x)))


def _gla_gate_body(xp_ref, xs_ref, nw_ref, w1_ref, w2_ref, b_ref, o_ref):
    def run(x_ref):
        h = _rms(x_ref[...], nw_ref[...]).astype(BF16)
        gk = jnp.dot(h, w1_ref[...], preferred_element_type=F32)
        z = jnp.dot(gk.astype(BF16), w2_ref[...], preferred_element_type=F32) + b_ref[...]
        o_ref[...] = _log_sigmoid(z) / GATE_NORMALIZER
    _on_row_source(run, (xp_ref, xs_ref))


def _gla_gate(xs, nw, w1p, w2p, b):
    m = N_TOK
    return pl.pallas_call(
        _gla_gate_body,
        grid=(m // TM,),
        in_specs=_row_specs(True, D_MODEL) + [
            pl.BlockSpec((1, D_MODEL), lambda i: (0, 0)),
            pl.BlockSpec((D_MODEL, LANES), lambda i: (0, 0)),
            pl.BlockSpec((LANES, GLA_KEY_DIM), lambda i: (0, 0)),
            pl.BlockSpec((1, GLA_KEY_DIM), lambda i: (0, 0)),
        ],
        out_specs=pl.BlockSpec((TM, GLA_KEY_DIM), lambda i: (i, 0)),
        out_shape=jax.ShapeDtypeStruct((m, GLA_KEY_DIM), F32),
        compiler_params=_cparams(("parallel",)),
        name="gla_gate",
    )(*xs, nw.reshape(1, D_MODEL), w1p, w2p, b.reshape(1, GLA_KEY_DIM))


def _gla_out(o, gate, norm_w):
    return _rms(o, norm_w) * _silu(gate)


def _cumsum_rows(g):
    c = g.shape[0]
    tri = (lax.broadcasted_iota(jnp.int32, (c, c), 0) >= lax.broadcasted_iota(jnp.int32, (c, c), 1)).astype(F32)
    return jnp.dot(tri, g, precision=lax.Precision.HIGHEST, preferred_element_type=F32)


def _causal(a):
    c = a.shape[0]
    keep = lax.broadcasted_iota(jnp.int32, (c, c), 0) >= lax.broadcasted_iota(jnp.int32, (c, c), 1)
    return jnp.where(keep, a, 0.0)


_NT = (((1,), (1,)), ((), ()))
_TN = (((0,), (0,)), ((), ()))


def _gla_prompt_body(*refs):
    proj_refs = refs[:BATCH]
    la_refs = refs[BATCH:2 * BATCH]
    nw_ref, o_ref, s_ref, st_ref = refs[2 * BATCH:]
    c = pl.program_id(0)

    @pl.when(c == 0)
    def _():
        st_ref[...] = jnp.zeros_like(st_ref)

    for bi in range(BATCH):
        p_ref = proj_refs[bi]
        b_all = _cumsum_rows(la_refs[bi][...])
        for h in range(GLA_HEADS):
            kc = slice(h * GLA_DK, (h + 1) * GLA_DK)
            v0 = 2 * GLA_KEY_DIM + h * GLA_DV
            b = b_all[:, kc]
            q = p_ref[:, kc].astype(F32)
            k = p_ref[:, GLA_KEY_DIM + h * GLA_DK:GLA_KEY_DIM + (h + 1) * GLA_DK].astype(F32)
            v = p_ref[:, v0:v0 + GLA_DV]
            gate = p_ref[:, v0 + GLA_VAL_DIM:v0 + GLA_VAL_DIM + GLA_DV].astype(F32)
            qe = (q * GLA_DK ** -0.5 * jnp.exp(b)).astype(BF16)
            ke = (k * jnp.exp(-b)).astype(BF16)
            a = _causal(lax.dot_general(qe, ke, _NT, preferred_element_type=F32))
            st = st_ref[bi, h]
            o = (jnp.dot(a.astype(BF16), v, preferred_element_type=F32)
                 + lax.dot_general(qe, st.astype(BF16), _NT, preferred_element_type=F32))
            b_last = b[GLA_C - 1:GLA_C, :]
            kd = (k * jnp.exp(b_last - b)).astype(BF16)
            st_new = st * jnp.exp(b_last) + lax.dot_general(v, kd, _TN, preferred_element_type=F32)
            st_ref[bi, h] = st_new
            o_ref[bi, :, h * GLA_DV:(h + 1) * GLA_DV] = _gla_out(o, gate, nw_ref[...]).astype(o_ref.dtype)

    @pl.when(c == pl.num_programs(0) - 1)
    def _():
        for bi in range(BATCH):
            for h in range(GLA_HEADS):
                s_ref[bi, h] = st_ref[bi, h].T


def _gla_prompt(proj, log_a, norm_w):
    nc = SEQ // GLA_C
    rows = [functools.partial(lambda bi, c: (bi * nc + c, 0), bi) for bi in range(BATCH)]
    o, s = pl.pallas_call(
        _gla_prompt_body,
        grid=(nc,),
        in_specs=([pl.BlockSpec((GLA_C, GLA_MAIN_DIM), r) for r in rows]
                  + [pl.BlockSpec((GLA_C, GLA_KEY_DIM), r) for r in rows]
                  + [pl.BlockSpec((1, GLA_DV), lambda c: (0, 0))]),
        out_specs=[
            pl.BlockSpec((BATCH, GLA_C, GLA_VAL_DIM), lambda c: (0, c, 0)),
            pl.BlockSpec((BATCH, GLA_HEADS, GLA_DK, GLA_DV), lambda c: (0, 0, 0, 0)),
        ],
        out_shape=[
            jax.ShapeDtypeStruct((BATCH, SEQ, GLA_VAL_DIM), BF16),
            jax.ShapeDtypeStruct((BATCH, GLA_HEADS, GLA_DK, GLA_DV), F32),
        ],
        scratch_shapes=[pltpu.VMEM((BATCH, GLA_HEADS, GLA_DV, GLA_DK), F32)],
        compiler_params=_cparams(("arbitrary",)),
        name="gla_prompt",
    )(*([proj] * BATCH + [log_a] * BATCH + [norm_w.reshape(1, GLA_DV)]))
    return o.reshape(N_PROMPT, GLA_VAL_DIM), s


def _gla_sample_body(proj_ref, la_ref, s0_ref, nw_ref, o_ref, s_ref):
    ones = jnp.ones((DEC_SEQ, LANES), F32)
    for h in range(GLA_HEADS):
        q = proj_ref[0, :, h * GLA_DK:(h + 1) * GLA_DK]
        k = proj_ref[0, :, GLA_KEY_DIM + h * GLA_DK:GLA_KEY_DIM + (h + 1) * GLA_DK]
        v0 = 2 * GLA_KEY_DIM + h * GLA_DV
        v = proj_ref[0, :, v0:v0 + GLA_DV].astype(BF16)
        gate = proj_ref[0, :, v0 + GLA_VAL_DIM:v0 + GLA_VAL_DIM + GLA_DV]
        g = la_ref[0, :, h * GLA_DK:(h + 1) * GLA_DK]
        b = _cumsum_rows(g)
        qe = (q * GLA_DK ** -0.5 * jnp.exp(b)).astype(BF16)
        ke = (k * jnp.exp(-b)).astype(BF16)
        a = _causal(lax.dot_general(qe, ke, _NT, preferred_element_type=F32))
        s0 = s0_ref[0, h]
        o = (jnp.dot(a.astype(BF16), v, preferred_element_type=F32)
             + jnp.dot(qe, s0.astype(BF16), preferred_element_type=F32))
        b_last = b[DEC_SEQ - 1:DEC_SEQ, :]
        kd = (k * jnp.exp(b_last - b)).astype(BF16)
        dcol = jnp.exp(lax.dot_general(g, ones, _TN, precision=lax.Precision.HIGHEST,
                                       preferred_element_type=F32))
        decay = jnp.concatenate([dcol] * (GLA_DV // LANES), axis=1)
        s_ref[0, h] = s0 * decay + lax.dot_general(kd, v, _TN, preferred_element_type=F32)
        o_ref[0, :, h * GLA_DV:(h + 1) * GLA_DV] = _gla_out(o, gate, nw_ref[...])


def _gla_sample(proj3, log_a3, state, norm_w):
    seq3 = lambda s: (s, 0, 0)
    seq4 = lambda s: (s, 0, 0, 0)
    return pl.pallas_call(
        _gla_sample_body,
        grid=(DEC_BATCH,),
        in_specs=[
            pl.BlockSpec((1, DEC_SEQ, GLA_MAIN_DIM), seq3),
            pl.BlockSpec((1, DEC_SEQ, GLA_KEY_DIM), seq3),
            pl.BlockSpec((1, GLA_HEADS, GLA_DK, GLA_DV), seq4),
            pl.BlockSpec((1, GLA_DV), lambda s: (0, 0)),
        ],
        out_specs=[
            pl.BlockSpec((1, DEC_SEQ, GLA_VAL_DIM), seq3),
            pl.BlockSpec((1, GLA_HEADS, GLA_DK, GLA_DV), seq4),
        ],
        out_shape=[
            jax.ShapeDtypeStruct((DEC_BATCH, DEC_SEQ, GLA_VAL_DIM), F32),
            jax.ShapeDtypeStruct((DEC_BATCH, GLA_HEADS, GLA_DK, GLA_DV), F32),
        ],
        compiler_params=_cparams(("parallel",)),
        name="gla_sample",
    )(proj3, log_a3, state, norm_w.reshape(1, GLA_DV))


def _swa_layer(x, cache_k, cache_v, norm_w, w_qkv, b_qkv, w_o, b_o, sinks, rel_bias):
    slots = (N_PAIRS, KV_PER_STEP, GROUP, HEAD_DIM)
    w_q = w_qkv[:, :Q_DIM].reshape((D_MODEL,) + slots).transpose(0, 1, 3, 2, 4).reshape(D_MODEL, Q_DIM)
    b_q = b_qkv[:Q_DIM].reshape(slots).transpose(0, 2, 1, 3).reshape(Q_DIM)
    w_qkv_s = jnp.concatenate([w_q, w_qkv[:, Q_DIM:]], axis=1).astype(BF16)
    b_qkv_s = jnp.concatenate([b_q, b_qkv[Q_DIM:]])
    w_o_s = w_o.reshape(slots + (D_MODEL,)).transpose(0, 2, 1, 3, 4).reshape(Q_DIM, D_MODEL).astype(BF16)

    q, kv = _norm_proj(x, norm_w, w_qkv_s, b_qkv_s, ((Q_DIM, BF16), (2 * KV_COLS, F32)), "swa_qkv")
    bias_tbl = _bias_table(rel_bias)
    o_p = _swa_prompt(q, kv, bias_tbl, sinks)

    q_rows = (q[N_PROMPT:].astype(F32).reshape(DEC_BATCH, DEC_SEQ, N_PAIRS, GROUP, KV_PER_STEP, HEAD_DIM)
              .transpose(0, 2, 4, 3, 1, 5).reshape(DEC_BATCH, S_ROWS, HEAD_DIM))
    kv_s = kv[N_PROMPT:].reshape(DEC_BATCH, DEC_SEQ, 2 * KV_COLS)
    pad8 = ((0, 0), (0, SUBLANES - DEC_SEQ), (0, 0))
    k_new8 = jnp.pad(kv_s[..., :KV_COLS], pad8)
    v_new8 = jnp.pad(kv_s[..., KV_COLS:], pad8)
    bias_s = (bias_tbl[:, :, :DEC_SEQ].reshape(N_PAIRS, GROUP, KV_PER_STEP, 2 * WINDOW, DEC_SEQ)
              .transpose(0, 2, 1, 4, 3).reshape(S_ROWS, 2 * WINDOW))
    sink_col = jnp.repeat(sinks, DEC_SEQ).reshape(S_ROWS, 1)
    o_s, k_s, v_s = _swa_sample(q_rows, k_new8, v_new8,
                                cache_k.reshape(DEC_BATCH, WINDOW, KV_COLS),
                                cache_v.reshape(DEC_BATCH, WINDOW, KV_COLS), bias_s, sink_col)
    o_s = (o_s.reshape(DEC_BATCH, N_PAIRS, KV_PER_STEP, GROUP, DEC_SEQ, HEAD_DIM)
           .transpose(0, 4, 1, 3, 2, 5).reshape(N_SAMPLE, Q_DIM))
    x = _proj_res((o_p, o_s), w_o_s, b_o, x, "swa_out")

    kv_p = kv[:N_PROMPT].reshape(BATCH, SEQ, 2 * KV_COLS)[:, SEQ - WINDOW:]
    k_p = kv_p[..., :KV_COLS].reshape(BATCH, WINDOW, N_KV_HEADS, HEAD_DIM)
    v_p = kv_p[..., KV_COLS:].reshape(BATCH, WINDOW, N_KV_HEADS, HEAD_DIM)
    shape_s = (DEC_BATCH, WINDOW, N_KV_HEADS, HEAD_DIM)
    return x, k_p, v_p, k_s.reshape(shape_s), v_s.reshape(shape_s)


def _gla_layer(x, state, norm_w, w_in, w_gk2, b_gk, gnorm, w_o):
    w_main = w_in[:, :GLA_MAIN_DIM].astype(BF16)
    w1p = jnp.pad(w_in[:, GLA_MAIN_DIM:], ((0, 0), (0, LANES - GATE_RANK))).astype(BF16)
    w2p = jnp.pad(w_gk2, ((0, LANES - GATE_RANK), (0, 0))).astype(BF16)
    (proj,) = _norm_proj(x, norm_w, w_main, jnp.zeros((GLA_MAIN_DIM,), F32), ((GLA_MAIN_DIM, BF16),), "gla_in")
    log_a = _gla_gate(x, norm_w, w1p, w2p, b_gk)
    o_p, s_p = _gla_prompt(proj, log_a, gnorm)
    proj_s = proj[N_PROMPT:].astype(F32).reshape(DEC_BATCH, DEC_SEQ, GLA_MAIN_DIM)
    log_a_s = log_a[N_PROMPT:].reshape(DEC_BATCH, DEC_SEQ, GLA_KEY_DIM)
    o_s, s_s = _gla_sample(proj_s, log_a_s, state, gnorm)
    x = _proj_res((o_p, o_s.reshape(N_SAMPLE, GLA_VAL_DIM)), w_o.astype(BF16), jnp.zeros((D_MODEL,), F32), x,
                  "gla_out")
    return x, s_p, s_s


def kernel(x_prompt, x_sample, cache_swa_k, cache_swa_v, state_gla, norm_ffn1, ffn1_w_gate, ffn1_w_up,
           ffn1_w_down, norm_mix, norm_ffn2, ffn2_w_gate, ffn2_w_up, ffn2_w_down, norm_final, rel_bias,
           swa_w_qkv, swa_b_qkv, swa_w_o, swa_b_o, swa_sinks, gla_w_in, gla_w_gk2, gla_b_gk, gla_norm,
           gla_w_o):
    x = (x_prompt.reshape(N_PROMPT, D_MODEL), x_sample.reshape(N_SAMPLE, D_MODEL))
    swa_kp, swa_vp, swa_ks, swa_vs, gla_sp, gla_ss = [], [], [], [], [], []
    for i in range(DEPTH):
        wg, wu = _tile_major_bf16(ffn1_w_gate, i), _tile_major_bf16(ffn1_w_up, i)
        x = [_ffn(part, i, norm_ffn1[i], wg, wu, ffn1_w_down) for part in x]
        j = i // 2
        if i % 2 == 0:
            x, kp, vp, ks, vs = _swa_layer(x, cache_swa_k[j], cache_swa_v[j], norm_mix[i], swa_w_qkv[j],
                                           swa_b_qkv[j], swa_w_o[j], swa_b_o[j], swa_sinks[j], rel_bias)
            swa_kp.append(kp)
            swa_vp.append(vp)
            swa_ks.append(ks)
            swa_vs.append(vs)
        else:
            x, sp, ss = _gla_layer(x, state_gla[j], norm_mix[i], gla_w_in[j], gla_w_gk2[j], gla_b_gk[j],
                                   gla_norm[j], gla_w_o[j])
            gla_sp.append(sp)
            gla_ss.append(ss)
        final_w = norm_final if i == DEPTH - 1 else None
        wg, wu = _tile_major_bf16(ffn2_w_gate, i), _tile_major_bf16(ffn2_w_up, i)
        x = [_ffn(part, i, norm_ffn2[i], wg, wu, ffn2_w_down, final_w) for part in x]
    y_prompt = x[0].reshape(BATCH, SEQ, D_MODEL)
    y_sample = x[1].reshape(DEC_BATCH, DEC_SEQ, D_MODEL)
    return (y_prompt, y_sample, jnp.stack(swa_kp), jnp.stack(swa_vp), jnp.stack(swa_ks), jnp.stack(swa_vs),
            jnp.stack(gla_sp), jnp.stack(gla_ss))
```

```python
import functools
import math

import numpy as np
import jax
import jax.numpy as jnp
from jax import lax
from jax.experimental import pallas as pl
from jax.experimental.pallas import tpu as pltpu

F32 = jnp.float32
BF16 = jnp.bfloat16

D_MODEL = 2048
BATCH = 2
SEQ = 4096
DEPTH = 2
DEC_BATCH = 128
DEC_SEQ = 4
RMS_EPS = 1e-6
D_FF = 5632
N_HEADS = 32
N_KV_HEADS = 8
HEAD_DIM = 64
GROUP = N_HEADS // N_KV_HEADS
WINDOW = 128
NUM_BUCKETS = 32
MAX_DISTANCE = 128
NEG_INF = -1e30
GLA_HEADS = 4
GLA_DK = 256
GLA_DV = 512
GLA_KEY_DIM = GLA_HEADS * GLA_DK
GLA_VAL_DIM = GLA_HEADS * GLA_DV
GATE_RANK = 16
GATE_NORMALIZER = 16.0
GLA_MAIN_DIM = 2 * GLA_KEY_DIM + 2 * GLA_VAL_DIM
Q_DIM = N_HEADS * HEAD_DIM
KV_COLS = N_KV_HEADS * HEAD_DIM

N_PROMPT = BATCH * SEQ
N_SAMPLE = DEC_BATCH * DEC_SEQ
N_TOK = N_PROMPT + N_SAMPLE

LANES = 128
SUBLANES = 8
VMEM_LIMIT = 56 * 1024 * 1024

TM = 512
TF = 512
TN = 512
GLA_C = 64
NB = SEQ // WINDOW
N_PROMPT_TILES = N_PROMPT // TM


def _rms(x, w):
    return x * lax.rsqrt(jnp.mean(x * x, axis=-1, keepdims=True) + RMS_EPS) * w


def _silu(x):
    return x * jax.nn.sigmoid(x)


def _cparams(sem):
    return pltpu.CompilerParams(dimension_semantics=sem, vmem_limit_bytes=VMEM_LIMIT)


def _row_specs(split, width):
    if not split:
        return [pl.BlockSpec((TM, width), lambda i, *_: (i, 0))]
    return [pl.BlockSpec((TM, width), lambda i, *_: (jnp.minimum(i, N_PROMPT_TILES - 1), 0)),
            pl.BlockSpec((TM, width), lambda i, *_: (jnp.maximum(i - N_PROMPT_TILES, 0), 0))]


def _on_row_source(fn, *ref_groups):
    if all(len(g) == 1 for g in ref_groups):
        fn(*[g[0] for g in ref_groups])
        return
    i = pl.program_id(0)
    pl.when(i < N_PROMPT_TILES)(lambda: fn(*[g[0] for g in ref_groups]))
    pl.when(i >= N_PROMPT_TILES)(lambda: fn(*[g[-1] for g in ref_groups]))


FFN_TM = 1024
FFN_TF = 256


def _ffn_body(final_norm, x_ref, nw_ref, wg_ref, wu_ref, wd_ref, *rest):
    if final_norm:
        fw_ref, o_ref, h_ref = rest
    else:
        o_ref, h_ref = rest
    j = pl.program_id(1)

    @pl.when(j == 0)
    def _():
        x = x_ref[...]
        h_ref[...] = _rms(x, nw_ref[...]).astype(BF16)
        o_ref[...] = x + x

    h = h_ref[...]
    g = jnp.dot(h, wg_ref[0].astype(BF16), preferred_element_type=F32)
    u = jnp.dot(h, wu_ref[0].astype(BF16), preferred_element_type=F32)
    a = (_silu(g) * u).astype(BF16)
    o_ref[...] += jnp.dot(a, wd_ref[0].astype(BF16), preferred_element_type=F32)

    @pl.when(j == pl.num_programs(1) - 1)
    def _():
        y = 0.5 * o_ref[...]
        if final_norm:
            y = _rms(y, fw_ref[...])
        o_ref[...] = y


def _ffn(x, layer, nw, wg, wu, wd, final_w=None):
    m = x.shape[0]
    tm = min(FFN_TM, m)
    final_norm = final_w is not None
    vec = pl.BlockSpec((1, D_MODEL), lambda i, j: (0, 0))
    in_specs = [
        pl.BlockSpec((tm, D_MODEL), lambda i, j: (i, 0), pipeline_mode=pl.Buffered(1)),
        vec,
        pl.BlockSpec((1, D_MODEL, FFN_TF), lambda i, j: (layer, 0, j)),
        pl.BlockSpec((1, D_MODEL, FFN_TF), lambda i, j: (layer, 0, j)),
        pl.BlockSpec((1, FFN_TF, D_MODEL), lambda i, j: (layer, j, 0)),
    ]
    args = [x, nw.reshape(1, D_MODEL), wg, wu, wd]
    if final_norm:
        in_specs.append(vec)
        args.append(final_w.reshape(1, D_MODEL))
    return pl.pallas_call(
        functools.partial(_ffn_body, final_norm),
        grid=(m // tm, D_FF // FFN_TF),
        in_specs=in_specs,
        out_specs=pl.BlockSpec((tm, D_MODEL), lambda i, j: (i, 0)),
        out_shape=jax.ShapeDtypeStruct((m, D_MODEL), F32),
        scratch_shapes=[pltpu.VMEM((tm, D_MODEL), BF16)],
        compiler_params=_cparams(("arbitrary", "arbitrary")),
        name="ffn",
    )(*args)


def _norm_proj_body(segments, xp_ref, xs_ref, nw_ref, w_ref, b_ref, *o_refs):
    def run(x_ref):
        h = _rms(x_ref[...], nw_ref[...]).astype(BF16)
        col = 0
        for (width, _), o_ref in zip(segments, o_refs):
            for c in range(0, width, TN):
                acc = jnp.dot(h, w_ref[:, col + c:col + c + TN], preferred_element_type=F32)
                o_ref[:, c:c + TN] = (acc + b_ref[:, col + c:col + c + TN]).astype(o_ref.dtype)
            col += width
    _on_row_source(run, (xp_ref, xs_ref))


def _norm_proj(xs, nw, w, b, segments, name):
    n = w.shape[1]
    return pl.pallas_call(
        functools.partial(_norm_proj_body, segments),
        grid=(N_TOK // TM,),
        in_specs=_row_specs(True, D_MODEL) + [
            pl.BlockSpec((1, D_MODEL), lambda i: (0, 0)),
            pl.BlockSpec((D_MODEL, n), lambda i: (0, 0), pipeline_mode=pl.Buffered(1)),
            pl.BlockSpec((1, n), lambda i: (0, 0)),
        ],
        out_specs=[pl.BlockSpec((TM, width), lambda i: (i, 0)) for width, _ in segments],
        out_shape=[jax.ShapeDtypeStruct((N_TOK, width), dtype) for width, dtype in segments],
        compiler_params=_cparams(("arbitrary",)),
        name=name,
    )(*xs, nw.reshape(1, D_MODEL), w, b.reshape(1, n))


def _proj_res_body(ap_ref, as_ref, w_ref, b_ref, rp_ref, rs_ref, op_ref, os_ref):
    def run(a_ref, r_ref, o_ref):
        a = a_ref[...].astype(BF16)
        for c in range(0, D_MODEL, TN):
            acc = jnp.dot(a, w_ref[:, c:c + TN], preferred_element_type=F32)
            o_ref[:, c:c + TN] = r_ref[:, c:c + TN] + acc + b_ref[:, c:c + TN]
    _on_row_source(run, (ap_ref, as_ref), (rp_ref, rs_ref), (op_ref, os_ref))


def _proj_res(a_pair, w, b, res_pair, name):
    k = w.shape[0]
    return pl.pallas_call(
        _proj_res_body,
        grid=(N_TOK // TM,),
        in_specs=_row_specs(True, k) + [
            pl.BlockSpec((k, D_MODEL), lambda i: (0, 0), pipeline_mode=pl.Buffered(1)),
            pl.BlockSpec((1, D_MODEL), lambda i: (0, 0)),
        ] + _row_specs(True, D_MODEL),
        out_specs=_row_specs(True, D_MODEL),
        out_shape=[jax.ShapeDtypeStruct((N_PROMPT, D_MODEL), F32), jax.ShapeDtypeStruct((N_SAMPLE, D_MODEL), F32)],
        compiler_params=_cparams(("arbitrary",)),
        name=name,
    )(*a_pair, w, b.reshape(1, D_MODEL), *res_pair)


def _t5_bucket_table():
    i = np.arange(WINDOW)[None, :]
    j = np.arange(2 * WINDOW)[:, None]
    n = np.maximum(WINDOW + i - j, 0)
    max_exact = NUM_BUCKETS // 2
    nf = np.maximum(n, 1).astype(np.float32)
    large = max_exact + (np.log(nf / np.float32(max_exact)) / np.float32(math.log(MAX_DISTANCE / max_exact))
                         * np.float32(NUM_BUCKETS - max_exact)).astype(np.int32)
    large = np.minimum(large, NUM_BUCKETS - 1)
    return np.where(n < max_exact, n, large).astype(np.int32)


KV_PER_STEP = LANES // HEAD_DIM
N_PAIRS = N_KV_HEADS // KV_PER_STEP
Q_COLS_PER_STEP = KV_PER_STEP * GROUP * HEAD_DIM


def _slot_head(slot):
    pair = slot // (GROUP * KV_PER_STEP)
    g = (slot // KV_PER_STEP) % GROUP
    hh = slot % KV_PER_STEP
    return (pair * KV_PER_STEP + hh) * GROUP + g


def _bias_table_body(bucket_ref, rb_ref, o_ref):
    h = _slot_head(pl.program_id(0))
    bucket = bucket_ref[...]
    acc = jnp.zeros((2 * WINDOW, WINDOW), F32)
    for b in range(NUM_BUCKETS):
        acc = jnp.where(bucket == b, rb_ref[b, h], acc)
    j = lax.broadcasted_iota(jnp.int32, (2 * WINDOW, WINDOW), 0)
    i = lax.broadcasted_iota(jnp.int32, (2 * WINDOW, WINDOW), 1)
    dist = WINDOW + i - j
    o_ref[0] = jnp.where((dist >= 0) & (dist < WINDOW), acc, NEG_INF)


def _bias_table(rel_bias):
    return pl.pallas_call(
        _bias_table_body,
        grid=(N_HEADS,),
        in_specs=[
            pl.BlockSpec((2 * WINDOW, WINDOW), lambda h: (0, 0)),
            pl.BlockSpec(memory_space=pltpu.SMEM),
        ],
        out_specs=pl.BlockSpec((1, 2 * WINDOW, WINDOW), lambda h: (h, 0, 0)),
        out_shape=jax.ShapeDtypeStruct((N_HEADS, 2 * WINDOW, WINDOW), F32),
        name="bias_table",
    )(jnp.asarray(_t5_bucket_table()), rel_bias)


def _softmax_with_sink(s, sink_col):
    m = jnp.maximum(jnp.max(s, axis=-1, keepdims=True), sink_col)
    p = jnp.exp(s - m)
    denom = jnp.sum(p, axis=-1, keepdims=True) + jnp.exp(sink_col - m)
    return p, 1.0 / denom


PAIR_SLOTS = GROUP * KV_PER_STEP
PAIR_COLS = PAIR_SLOTS * WINDOW
ONES_ROWS = 16


def _swa_prompt_body(sink_ref, q_ref, kp_ref, ko_ref, vp_ref, vo_ref, bias_ref, o_ref):
    pair = pl.program_id(0)
    blk = pl.program_id(2)
    head_a = lax.broadcasted_iota(jnp.int32, (WINDOW, LANES), 1) < HEAD_DIM
    k = jnp.concatenate([kp_ref[...], ko_ref[...]], axis=0).astype(BF16)
    v = jnp.concatenate([vp_ref[...], vo_ref[...]], axis=0)
    vt = jnp.concatenate([v.T, jnp.ones((ONES_ROWS, 2 * WINDOW), F32)], axis=0).astype(BF16)
    parts = []
    for g in range(GROUP):
        qg = q_ref[:, g * LANES:(g + 1) * LANES] * HEAD_DIM ** -0.5
        zero = jnp.zeros_like(qg)
        parts += [jnp.where(head_a, qg, zero), jnp.where(head_a, zero, qg)]
    qbd = jnp.concatenate(parts, axis=0)
    st = lax.dot_general(k, qbd, (((1,), (1,)), ((), ())), preferred_element_type=F32)
    st = st + jnp.concatenate([bias_ref[t] for t in range(PAIR_SLOTS)], axis=1)
    no_prev = jnp.where(blk > 0, 0.0, NEG_INF)
    st = jnp.concatenate([st[:WINDOW] + no_prev, st[WINDOW:]], axis=0)
    sink_row = jnp.concatenate(
        [jnp.full((1, WINDOW), sink_ref[_slot_head(pair * PAIR_SLOTS + slot)], F32)
         for slot in range(PAIR_SLOTS)], axis=1)
    m = jnp.maximum(jnp.max(st, axis=0, keepdims=True), sink_row)
    pt = jnp.exp(st - m).astype(BF16)
    oa = jnp.dot(vt, pt, preferred_element_type=F32)
    inv = 1.0 / (oa[LANES:LANES + 1] + jnp.exp(sink_row - m))
    o = oa[:LANES] * inv
    for g in range(GROUP):
        c = g * KV_PER_STEP * WINDOW
        ot = jnp.concatenate([o[:HEAD_DIM, c:c + WINDOW], o[HEAD_DIM:, c + WINDOW:c + 2 * WINDOW]], axis=0)
        o_ref[:, g * LANES:(g + 1) * LANES] = ot.T.astype(o_ref.dtype)


def _swa_prompt(q, kv, bias_tbl, sinks):
    v_col0 = KV_COLS // LANES

    def prev(p, b, i):
        return b * NB + jnp.maximum(i - 1, 0)

    return pl.pallas_call(
        _swa_prompt_body,
        grid=(N_PAIRS, BATCH, NB),
        in_specs=[
            pl.BlockSpec(memory_space=pltpu.SMEM),
            pl.BlockSpec((WINDOW, Q_COLS_PER_STEP), lambda p, b, i: (b * NB + i, p)),
            pl.BlockSpec((WINDOW, LANES), lambda p, b, i: (prev(p, b, i), p)),
            pl.BlockSpec((WINDOW, LANES), lambda p, b, i: (b * NB + i, p)),
            pl.BlockSpec((WINDOW, LANES), lambda p, b, i: (prev(p, b, i), v_col0 + p)),
            pl.BlockSpec((WINDOW, LANES), lambda p, b, i: (b * NB + i, v_col0 + p)),
            pl.BlockSpec((PAIR_SLOTS, 2 * WINDOW, WINDOW), lambda p, b, i: (p, 0, 0)),
        ],
        out_specs=pl.BlockSpec((WINDOW, Q_COLS_PER_STEP), lambda p, b, i: (b * NB + i, p)),
        out_shape=jax.ShapeDtypeStruct((N_PROMPT, Q_DIM), BF16),
        compiler_params=_cparams(("arbitrary", "arbitrary", "arbitrary")),
        name="swa_prompt",
    )(sinks, q, kv, kv, kv, kv, bias_tbl)


S_ROWS = N_KV_HEADS * GROUP * DEC_SEQ


def _swa_sample_body(q_ref, kn_ref, vn_ref, ck_ref, cv_ref, bias_ref, sink_ref,
                     o_ref, ko_ref, vo_ref):
    x = (q_ref[0] * HEAD_DIM ** -0.5).astype(BF16)
    xt = jnp.concatenate([x] * N_KV_HEADS, axis=1)
    row_kv = lax.broadcasted_iota(jnp.int32, (S_ROWS, KV_COLS), 0) // (GROUP * DEC_SEQ)
    col_kv = lax.broadcasted_iota(jnp.int32, (S_ROWS, KV_COLS), 1) // HEAD_DIM
    own = row_kv == col_kv
    qbd = jnp.where(own, xt, jnp.zeros_like(xt))
    pad = jnp.zeros((WINDOW - SUBLANES, KV_COLS), F32)
    kk = jnp.concatenate([ck_ref[0], kn_ref[0], pad], axis=0).astype(BF16)
    vv = jnp.concatenate([cv_ref[0], vn_ref[0], pad], axis=0).astype(BF16)
    s = lax.dot_general(qbd, kk, (((1,), (1,)), ((), ())), preferred_element_type=F32)
    s = s + bias_ref[...]
    p, inv = _softmax_with_sink(s, sink_ref[...])
    of = jnp.dot(p.astype(BF16), vv, preferred_element_type=F32)
    of = jnp.where(own, of, 0.0)
    o = of[:, 0:HEAD_DIM]
    for c in range(1, N_KV_HEADS):
        o = o + of[:, c * HEAD_DIM:(c + 1) * HEAD_DIM]
    o_ref[0] = o * inv
    keep = WINDOW - DEC_SEQ
    ko_ref[0, 0:keep, :] = ck_ref[0, DEC_SEQ:WINDOW, :]
    ko_ref[0, keep:WINDOW, :] = kn_ref[0, 0:DEC_SEQ, :]
    vo_ref[0, 0:keep, :] = cv_ref[0, DEC_SEQ:WINDOW, :]
    vo_ref[0, keep:WINDOW, :] = vn_ref[0, 0:DEC_SEQ, :]


def _swa_sample(q_rows, k_new8, v_new8, cache_k, cache_v, bias_s, sink_col):
    seq3 = lambda s: (s, 0, 0)
    full2 = lambda s: (0, 0)
    return pl.pallas_call(
        _swa_sample_body,
        grid=(DEC_BATCH,),
        in_specs=[
            pl.BlockSpec((1, S_ROWS, HEAD_DIM), seq3),
            pl.BlockSpec((1, SUBLANES, KV_COLS), seq3),
            pl.BlockSpec((1, SUBLANES, KV_COLS), seq3),
            pl.BlockSpec((1, WINDOW, KV_COLS), seq3),
            pl.BlockSpec((1, WINDOW, KV_COLS), seq3),
            pl.BlockSpec((S_ROWS, 2 * WINDOW), full2),
            pl.BlockSpec((S_ROWS, 1), full2),
        ],
        out_specs=[
            pl.BlockSpec((1, S_ROWS, HEAD_DIM), seq3),
            pl.BlockSpec((1, WINDOW, KV_COLS), seq3),
            pl.BlockSpec((1, WINDOW, KV_COLS), seq3),
        ],
        out_shape=[
            jax.ShapeDtypeStruct((DEC_BATCH, S_ROWS, HEAD_DIM), F32),
            jax.ShapeDtypeStruct((DEC_BATCH, WINDOW, KV_COLS), F32),
            jax.ShapeDtypeStruct((DEC_BATCH, WINDOW, KV_COLS), F32),
        ],
        compiler_params=_cparams(("parallel",)),
        name="swa_sample",
    )(q_rows, k_new8, v_new8, cache_k, cache_v, bias_s, sink_col)


def _log_sigmoid(x):
    return jnp.minimum(x, 0.0) - jnp.log1p(jnp.exp(-jnp.abs(x)))


def _gla_gate_body(xp_ref, xs_ref, nw_ref, w1_ref, w2_ref, b_ref, o_ref):
    def run(x_ref):
        h = _rms(x_ref[...], nw_ref[...]).astype(BF16)
        gk = jnp.dot(h, w1_ref[...], preferred_element_type=F32)
        z = jnp.dot(gk.astype(BF16), w2_ref[...], preferred_element_type=F32) + b_ref[...]
        o_ref[...] = _log_sigmoid(z) / GATE_NORMALIZER
    _on_row_source(run, (xp_ref, xs_ref))


def _gla_gate(xs, nw, w1p, w2p, b):
    m = N_TOK
    return pl.pallas_call(
        _gla_gate_body,
        grid=(m // TM,),
        in_specs=_row_specs(True, D_MODEL) + [
            pl.BlockSpec((1, D_MODEL), lambda i: (0, 0)),
            pl.BlockSpec((D_MODEL, LANES), lambda i: (0, 0)),
            pl.BlockSpec((LANES, GLA_KEY_DIM), lambda i: (0, 0)),
            pl.BlockSpec((1, GLA_KEY_DIM), lambda i: (0, 0)),
        ],
        out_specs=pl.BlockSpec((TM, GLA_KEY_DIM), lambda i: (i, 0)),
        out_shape=jax.ShapeDtypeStruct((m, GLA_KEY_DIM), F32),
        compiler_params=_cparams(("parallel",)),
        name="gla_gate",
    )(*xs, nw.reshape(1, D_MODEL), w1p, w2p, b.reshape(1, GLA_KEY_DIM))


def _gla_out(o, gate, norm_w):
    return _rms(o, norm_w) * _silu(gate)


def _split3(x):
    hi = x.astype(BF16)
    r = x - hi.astype(F32)
    mid = r.astype(BF16)
    lo = (r - mid.astype(F32)).astype(BF16)
    return hi, mid, lo


def _cumsum_rows(g):
    c = g.shape[0]
    tri = (lax.broadcasted_iota(jnp.int32, (c, c), 0) >= lax.broadcasted_iota(jnp.int32, (c, c), 1)).astype(BF16)
    return jnp.dot(jnp.concatenate([tri] * 3, axis=1), jnp.concatenate(_split3(g), axis=0),
                   preferred_element_type=F32)


def _causal(a):
    c = a.shape[0]
    keep = lax.broadcasted_iota(jnp.int32, (c, c), 0) >= lax.broadcasted_iota(jnp.int32, (c, c), 1)
    return jnp.where(keep, a, 0.0)


_NT = (((1,), (1,)), ((), ()))
_TN = (((0,), (0,)), ((), ()))


def _gla_prompt_body(*refs):
    proj_refs = refs[:BATCH]
    la_refs = refs[BATCH:2 * BATCH]
    nw_ref, o_ref, s_ref, st_ref = refs[2 * BATCH:]
    c = pl.program_id(0)

    @pl.when(c == 0)
    def _():
        st_ref[...] = jnp.zeros_like(st_ref)

    for bi in range(BATCH):
        p_ref = proj_refs[bi]
        b_all = _cumsum_rows(la_refs[bi][...])
        for h in range(GLA_HEADS):
            kc = slice(h * GLA_DK, (h + 1) * GLA_DK)
            v0 = 2 * GLA_KEY_DIM + h * GLA_DV
            b = b_all[:, kc]
            q = p_ref[:, kc].astype(F32)
            k = p_ref[:, GLA_KEY_DIM + h * GLA_DK:GLA_KEY_DIM + (h + 1) * GLA_DK].astype(F32)
            v = p_ref[:, v0:v0 + GLA_DV]
            gate = p_ref[:, v0 + GLA_VAL_DIM:v0 + GLA_VAL_DIM + GLA_DV].astype(F32)
            qe = (q * GLA_DK ** -0.5 * jnp.exp(b)).astype(BF16)
            ke = (k * jnp.exp(-b)).astype(BF16)
            a = _causal(lax.dot_general(qe, ke, _NT, preferred_element_type=F32))
            st = st_ref[bi, h]
            o = (jnp.dot(a.astype(BF16), v, preferred_element_type=F32)
                 + lax.dot_general(qe, st.astype(BF16), _NT, preferred_element_type=F32))
            b_last = b[GLA_C - 1:GLA_C, :]
            kd = (k * jnp.exp(b_last - b)).astype(BF16)
            st_new = st * jnp.exp(b_last) + lax.dot_general(v, kd, _TN, preferred_element_type=F32)
            st_ref[bi, h] = st_new
            o_ref[bi, :, h * GLA_DV:(h + 1) * GLA_DV] = _gla_out(o, gate, nw_ref[...]).astype(o_ref.dtype)

    @pl.when(c == pl.num_programs(0) - 1)
    def _():
        for bi in range(BATCH):
            for h in range(GLA_HEADS):
                s_ref[bi, h] = st_ref[bi, h].T


def _gla_prompt(proj, log_a, norm_w):
    nc = SEQ // GLA_C
    rows = [functools.partial(lambda bi, c: (bi * nc + c, 0), bi) for bi in range(BATCH)]
    o, s = pl.pallas_call(
        _gla_prompt_body,
        grid=(nc,),
        in_specs=([pl.BlockSpec((GLA_C, GLA_MAIN_DIM), r) for r in rows]
                  + [pl.BlockSpec((GLA_C, GLA_KEY_DIM), r) for r in rows]
                  + [pl.BlockSpec((1, GLA_DV), lambda c: (0, 0))]),
        out_specs=[
            pl.BlockSpec((BATCH, GLA_C, GLA_VAL_DIM), lambda c: (0, c, 0)),
            pl.BlockSpec((BATCH, GLA_HEADS, GLA_DK, GLA_DV), lambda c: (0, 0, 0, 0)),
        ],
        out_shape=[
            jax.ShapeDtypeStruct((BATCH, SEQ, GLA_VAL_DIM), BF16),
            jax.ShapeDtypeStruct((BATCH, GLA_HEADS, GLA_DK, GLA_DV), F32),
        ],
        scratch_shapes=[pltpu.VMEM((BATCH, GLA_HEADS, GLA_DV, GLA_DK), F32)],
        compiler_params=_cparams(("arbitrary",)),
        name="gla_prompt",
    )(*([proj] * BATCH + [log_a] * BATCH + [norm_w.reshape(1, GLA_DV)]))
    return o.reshape(N_PROMPT, GLA_VAL_DIM), s


GLA_SEQS_PER_STEP = 2


def _gla_sample_body(proj_ref, la_ref, s0_ref, nw_ref, o_ref, s_ref):
    ones = jnp.ones((DEC_SEQ, LANES), BF16)
    for i in range(GLA_SEQS_PER_STEP):
        for h in range(GLA_HEADS):
            q = proj_ref[i, :, h * GLA_DK:(h + 1) * GLA_DK]
            k = proj_ref[i, :, GLA_KEY_DIM + h * GLA_DK:GLA_KEY_DIM + (h + 1) * GLA_DK]
            v0 = 2 * GLA_KEY_DIM + h * GLA_DV
            v = proj_ref[i, :, v0:v0 + GLA_DV].astype(BF16)
            gate = proj_ref[i, :, v0 + GLA_VAL_DIM:v0 + GLA_VAL_DIM + GLA_DV]
            g = la_ref[i, :, h * GLA_DK:(h + 1) * GLA_DK]
            rows = [g[0:1]]
            for t in range(1, DEC_SEQ):
                rows.append(rows[-1] + g[t:t + 1])
            b = jnp.concatenate(rows, axis=0)
            b_last = rows[-1]
            qe = (q * GLA_DK ** -0.5 * jnp.exp(b)).astype(BF16)
            ke = (k * jnp.exp(-b)).astype(BF16)
            a = _causal(lax.dot_general(qe, ke, _NT, preferred_element_type=F32))
            s0 = s0_ref[i, h]
            o = (jnp.dot(a.astype(BF16), v, preferred_element_type=F32)
                 + jnp.dot(qe, s0.astype(BF16), preferred_element_type=F32))
            kd = (k * jnp.exp(b_last - b)).astype(BF16)
            dsum = sum(lax.dot_general(piece, ones, _TN, preferred_element_type=F32) for piece in _split3(g))
            decay = jnp.concatenate([jnp.exp(dsum)] * (GLA_DV // LANES), axis=1)
            s_ref[i, h] = s0 * decay + lax.dot_general(kd, v, _TN, preferred_element_type=F32)
            o_ref[i, :, h * GLA_DV:(h + 1) * GLA_DV] = _gla_out(o, gate, nw_ref[...])


def _gla_sample(proj3, log_a3, state, norm_w):
    n = GLA_SEQS_PER_STEP
    seq3 = lambda s: (s, 0, 0)
    seq4 = lambda s: (s, 0, 0, 0)
    return pl.pallas_call(
        _gla_sample_body,
        grid=(DEC_BATCH // n,),
        in_specs=[
            pl.BlockSpec((n, DEC_SEQ, GLA_MAIN_DIM), seq3),
            pl.BlockSpec((n, DEC_SEQ, GLA_KEY_DIM), seq3),
            pl.BlockSpec((n, GLA_HEADS, GLA_DK, GLA_DV), seq4),
            pl.BlockSpec((1, GLA_DV), lambda s: (0, 0)),
        ],
        out_specs=[
            pl.BlockSpec((n, DEC_SEQ, GLA_VAL_DIM), seq3),
            pl.BlockSpec((n, GLA_HEADS, GLA_DK, GLA_DV), seq4),
        ],
        out_shape=[
            jax.ShapeDtypeStruct((DEC_BATCH, DEC_SEQ, GLA_VAL_DIM), F32),
            jax.ShapeDtypeStruct((DEC_BATCH, GLA_HEADS, GLA_DK, GLA_DV), F32),
        ],
        compiler_params=_cparams(("parallel",)),
        name="gla_sample",
    )(proj3, log_a3, state, norm_w.reshape(1, GLA_DV))


def _swa_layer(x, cache_k, cache_v, norm_w, w_qkv, b_qkv, w_o, b_o, sinks, rel_bias):
    slots = (N_PAIRS, KV_PER_STEP, GROUP, HEAD_DIM)
    w_q = w_qkv[:, :Q_DIM].reshape((D_MODEL,) + slots).transpose(0, 1, 3, 2, 4).reshape(D_MODEL, Q_DIM)
    b_q = b_qkv[:Q_DIM].reshape(slots).transpose(0, 2, 1, 3).reshape(Q_DIM)
    w_qkv_s = jnp.concatenate([w_q, w_qkv[:, Q_DIM:]], axis=1).astype(BF16)
    b_qkv_s = jnp.concatenate([b_q, b_qkv[Q_DIM:]])
    w_o_s = w_o.reshape(slots + (D_MODEL,)).transpose(0, 2, 1, 3, 4).reshape(Q_DIM, D_MODEL).astype(BF16)

    q, kv = _norm_proj(x, norm_w, w_qkv_s, b_qkv_s, ((Q_DIM, BF16), (2 * KV_COLS, F32)), "swa_qkv")
    bias_tbl = _bias_table(rel_bias)
    o_p = _swa_prompt(q, kv, bias_tbl, sinks)

    q_rows = (q[N_PROMPT:].astype(F32).reshape(DEC_BATCH, DEC_SEQ, N_PAIRS, GROUP, KV_PER_STEP, HEAD_DIM)
              .transpose(0, 2, 4, 3, 1, 5).reshape(DEC_BATCH, S_ROWS, HEAD_DIM))
    kv_s = kv[N_PROMPT:].reshape(DEC_BATCH, DEC_SEQ, 2 * KV_COLS)
    pad8 = ((0, 0), (0, SUBLANES - DEC_SEQ), (0, 0))
    k_new8 = jnp.pad(kv_s[..., :KV_COLS], pad8)
    v_new8 = jnp.pad(kv_s[..., KV_COLS:], pad8)
    bias_s = (bias_tbl[:, :, :DEC_SEQ].reshape(N_PAIRS, GROUP, KV_PER_STEP, 2 * WINDOW, DEC_SEQ)
              .transpose(0, 2, 1, 4, 3).reshape(S_ROWS, 2 * WINDOW))
    sink_col = jnp.repeat(sinks, DEC_SEQ).reshape(S_ROWS, 1)
    o_s, k_s, v_s = _swa_sample(q_rows, k_new8, v_new8,
                                cache_k.reshape(DEC_BATCH, WINDOW, KV_COLS),
                                cache_v.reshape(DEC_BATCH, WINDOW, KV_COLS), bias_s, sink_col)
    o_s = (o_s.reshape(DEC_BATCH, N_PAIRS, KV_PER_STEP, GROUP, DEC_SEQ, HEAD_DIM)
           .transpose(0, 4, 1, 3, 2, 5).reshape(N_SAMPLE, Q_DIM))
    x = _proj_res((o_p, o_s), w_o_s, b_o, x, "swa_out")

    kv_p = kv[:N_PROMPT].reshape(BATCH, SEQ, 2 * KV_COLS)[:, SEQ - WINDOW:]
    k_p = kv_p[..., :KV_COLS].reshape(BATCH, WINDOW, N_KV_HEADS, HEAD_DIM)
    v_p = kv_p[..., KV_COLS:].reshape(BATCH, WINDOW, N_KV_HEADS, HEAD_DIM)
    shape_s = (DEC_BATCH, WINDOW, N_KV_HEADS, HEAD_DIM)
    return x, k_p, v_p, k_s.reshape(shape_s), v_s.reshape(shape_s)


def _gla_layer(x, state, norm_w, w_in, w_gk2, b_gk, gnorm, w_o):
    w_main = w_in[:, :GLA_MAIN_DIM].astype(BF16)
    w1p = jnp.pad(w_in[:, GLA_MAIN_DIM:], ((0, 0), (0, LANES - GATE_RANK))).astype(BF16)
    w2p = jnp.pad(w_gk2, ((0, LANES - GATE_RANK), (0, 0))).astype(BF16)
    (proj,) = _norm_proj(x, norm_w, w_main, jnp.zeros((GLA_MAIN_DIM,), F32), ((GLA_MAIN_DIM, BF16),), "gla_in")
    log_a = _gla_gate(x, norm_w, w1p, w2p, b_gk)
    o_p, s_p = _gla_prompt(proj, log_a, gnorm)
    proj_s = proj[N_PROMPT:].astype(F32).reshape(DEC_BATCH, DEC_SEQ, GLA_MAIN_DIM)
    log_a_s = log_a[N_PROMPT:].reshape(DEC_BATCH, DEC_SEQ, GLA_KEY_DIM)
    o_s, s_s = _gla_sample(proj_s, log_a_s, state, gnorm)
    x = _proj_res((o_p, o_s.reshape(N_SAMPLE, GLA_VAL_DIM)), w_o.astype(BF16), jnp.zeros((D_MODEL,), F32), x,
                  "gla_out")
    return x, s_p, s_s


def kernel(x_prompt, x_sample, cache_swa_k, cache_swa_v, state_gla, norm_ffn1, ffn1_w_gate, ffn1_w_up,
           ffn1_w_down, norm_mix, norm_ffn2, ffn2_w_gate, ffn2_w_up, ffn2_w_down, norm_final, rel_bias,
           swa_w_qkv, swa_b_qkv, swa_w_o, swa_b_o, swa_sinks, gla_w_in, gla_w_gk2, gla_b_gk, gla_norm,
           gla_w_o):
    x = (x_prompt.reshape(N_PROMPT, D_MODEL), x_sample.reshape(N_SAMPLE, D_MODEL))
    swa_kp, swa_vp, swa_ks, swa_vs, gla_sp, gla_ss = [], [], [], [], [], []
    for i in range(DEPTH):
        x = [_ffn(part, i, norm_ffn1[i], ffn1_w_gate, ffn1_w_up, ffn1_w_down) for part in x]
        j = i // 2
        if i % 2 == 0:
            x, kp, vp, ks, vs = _swa_layer(x, cache_swa_k[j], cache_swa_v[j], norm_mix[i], swa_w_qkv[j],
                                           swa_b_qkv[j], swa_w_o[j], swa_b_o[j], swa_sinks[j], rel_bias)
            swa_kp.append(kp)
            swa_vp.append(vp)
            swa_ks.append(ks)
            swa_vs.append(vs)
        else:
            x, sp, ss = _gla_layer(x, state_gla[j], norm_mix[i], gla_w_in[j], gla_w_gk2[j], gla_b_gk[j],
                                   gla_norm[j], gla_w_o[j])
            gla_sp.append(sp)
            gla_ss.append(ss)
        final_w = norm_final if i == DEPTH - 1 else None
        x = [_ffn(part, i, norm_ffn2[i], ffn2_w_gate, ffn2_w_up, ffn2_w_down, final_w) for part in x]
    y_prompt = x[0].reshape(BATCH, SEQ, D_MODEL)
    y_sample = x[1].reshape(DEC_BATCH, DEC_SEQ, D_MODEL)
    return (y_prompt, y_sample, jnp.stack(swa_kp), jnp.stack(swa_vp), jnp.stack(swa_ks), jnp.stack(swa_vs),
            jnp.stack(gla_sp), jnp.stack(gla_ss))
```

```python
import functools
import math

import numpy as np
import jax
import jax.numpy as jnp
from jax import lax
from jax.experimental import pallas as pl
from jax.experimental.pallas import tpu as pltpu

F32 = jnp.float32
BF16 = jnp.bfloat16

D_MODEL = 2048
BATCH = 2
SEQ = 4096
DEPTH = 2
DEC_BATCH = 128
DEC_SEQ = 4
RMS_EPS = 1e-6
D_FF = 5632
N_HEADS = 32
N_KV_HEADS = 8
HEAD_DIM = 64
GROUP = N_HEADS // N_KV_HEADS
WINDOW = 128
NUM_BUCKETS = 32
MAX_DISTANCE = 128
NEG_INF = -1e30
GLA_HEADS = 4
GLA_DK = 256
GLA_DV = 512
GLA_KEY_DIM = GLA_HEADS * GLA_DK
GLA_VAL_DIM = GLA_HEADS * GLA_DV
GATE_RANK = 16
GATE_NORMALIZER = 16.0
GLA_MAIN_DIM = 2 * GLA_KEY_DIM + 2 * GLA_VAL_DIM
Q_DIM = N_HEADS * HEAD_DIM
KV_COLS = N_KV_HEADS * HEAD_DIM

N_PROMPT = BATCH * SEQ
N_SAMPLE = DEC_BATCH * DEC_SEQ
N_TOK = N_PROMPT + N_SAMPLE

LANES = 128
SUBLANES = 8
VMEM_LIMIT = 56 * 1024 * 1024

TM = 512
TF = 512
TN = 512
GLA_C = 64
NB = SEQ // WINDOW
N_PROMPT_TILES = N_PROMPT // TM


def _rms(x, w):
    return x * lax.rsqrt(jnp.mean(x * x, axis=-1, keepdims=True) + RMS_EPS) * w


def _silu(x):
    return x * jax.nn.sigmoid(x)


def _cparams(sem):
    return pltpu.CompilerParams(dimension_semantics=sem, vmem_limit_bytes=VMEM_LIMIT)


def _row_specs(split, width):
    if not split:
        return [pl.BlockSpec((TM, width), lambda i, *_: (i, 0))]
    return [pl.BlockSpec((TM, width), lambda i, *_: (jnp.minimum(i, N_PROMPT_TILES - 1), 0)),
            pl.BlockSpec((TM, width), lambda i, *_: (jnp.maximum(i - N_PROMPT_TILES, 0), 0))]


def _on_row_source(fn, *ref_groups):
    if all(len(g) == 1 for g in ref_groups):
        fn(*[g[0] for g in ref_groups])
        return
    i = pl.program_id(0)
    pl.when(i < N_PROMPT_TILES)(lambda: fn(*[g[0] for g in ref_groups]))
    pl.when(i >= N_PROMPT_TILES)(lambda: fn(*[g[-1] for g in ref_groups]))


FFN_TM = 1024
FFN_TF = 256
FFN_TF_SAMPLE = 512
FFN_VMEM_LIMIT = 60 * 1024 * 1024


def _ffn_body(final_norm, emit_w, x_ref, nw_ref, wg_ref, wu_ref, wd_ref, *rest):
    rest = list(rest)
    fw_ref = rest.pop(0) if final_norm else None
    o_ref = rest.pop(0)
    h_ref = rest.pop()
    j = pl.program_id(1)

    @pl.when(j == 0)
    def _():
        x = x_ref[...]
        h_ref[...] = _rms(x, nw_ref[...]).astype(BF16)
        o_ref[...] = x

    if emit_w:
        wgo_ref, wuo_ref, wdo_ref = rest
        wgo_ref[...] = wg_ref[0].astype(BF16)
        wuo_ref[...] = wu_ref[0].astype(BF16)
        wdo_ref[...] = wd_ref[0].astype(BF16)
        wg, wu, wd = wgo_ref[...], wuo_ref[...], wdo_ref[...]
    else:
        wg, wu, wd = wg_ref[...], wu_ref[...], wd_ref[...]
    h = h_ref[...]
    g = jnp.dot(h, wg, preferred_element_type=F32)
    u = jnp.dot(h, wu, preferred_element_type=F32)
    a = (_silu(g) * (0.5 * u)).astype(BF16)
    o_ref[...] += jnp.dot(a, wd, preferred_element_type=F32)

    if final_norm:
        @pl.when(j == pl.num_programs(1) - 1)
        def _():
            o_ref[...] = _rms(o_ref[...], fw_ref[...])


def _ffn(x, layer, nw, wg, wu, wd, final_w=None):
    m = x.shape[0]
    emit_w = wg.dtype == F32
    tm = min(FFN_TM, m)
    tf = FFN_TF if emit_w else FFN_TF_SAMPLE
    nj = D_FF // tf
    final_norm = final_w is not None
    vec = pl.BlockSpec((1, D_MODEL), lambda i, j: (0, 0))
    if emit_w:
        w_specs = [
            pl.BlockSpec((1, D_MODEL, tf), lambda i, j: (layer, 0, j)),
            pl.BlockSpec((1, D_MODEL, tf), lambda i, j: (layer, 0, j)),
            pl.BlockSpec((1, tf, D_MODEL), lambda i, j: (layer, j, 0)),
        ]
    else:
        w_specs = [
            pl.BlockSpec((D_MODEL, tf), lambda i, j: (0, j)),
            pl.BlockSpec((D_MODEL, tf), lambda i, j: (0, j)),
            pl.BlockSpec((tf, D_MODEL), lambda i, j: (j, 0)),
        ]
    in_specs = [pl.BlockSpec((tm, D_MODEL), lambda i, j: (i, 0)), vec] + w_specs
    args = [x, nw.reshape(1, D_MODEL), wg, wu, wd]
    if final_norm:
        in_specs.append(vec)
        args.append(final_w.reshape(1, D_MODEL))
    out_specs = [pl.BlockSpec((tm, D_MODEL), lambda i, j: (i, 0))]
    out_shape = [jax.ShapeDtypeStruct((m, D_MODEL), F32)]
    if emit_w:
        once = lambda i, j: jnp.where(i == 0, j, nj - 1)
        out_specs += [
            pl.BlockSpec((D_MODEL, tf), lambda i, j: (0, once(i, j))),
            pl.BlockSpec((D_MODEL, tf), lambda i, j: (0, once(i, j))),
            pl.BlockSpec((tf, D_MODEL), lambda i, j: (once(i, j), 0)),
        ]
        out_shape += [
            jax.ShapeDtypeStruct((D_MODEL, D_FF), BF16),
            jax.ShapeDtypeStruct((D_MODEL, D_FF), BF16),
            jax.ShapeDtypeStruct((D_FF, D_MODEL), BF16),
        ]
    outs = pl.pallas_call(
        functools.partial(_ffn_body, final_norm, emit_w),
        grid=(m // tm, nj),
        in_specs=in_specs,
        out_specs=out_specs,
        out_shape=out_shape,
        scratch_shapes=[pltpu.VMEM((tm, D_MODEL), BF16)],
        compiler_params=pltpu.CompilerParams(dimension_semantics=("arbitrary", "arbitrary"),
                                             vmem_limit_bytes=FFN_VMEM_LIMIT),
        name="ffn",
    )(*args)
    return outs[0], tuple(outs[1:])


def _ffn_pair(x_pair, layer, nw, wg, wu, wd, final_w=None):
    y_prompt, w_bf16 = _ffn(x_pair[0], layer, nw, wg, wu, wd, final_w)
    y_sample, _ = _ffn(x_pair[1], layer, nw, *w_bf16, final_w)
    return y_prompt, y_sample


def _norm_proj_body(segments, xp_ref, xs_ref, nw_ref, w_ref, b_ref, *o_refs):
    def run(x_ref):
        h = _rms(x_ref[...], nw_ref[...]).astype(BF16)
        col = 0
        for (width, _), o_ref in zip(segments, o_refs):
            for c in range(0, width, TN):
                acc = jnp.dot(h, w_ref[:, col + c:col + c + TN], preferred_element_type=F32)
                o_ref[:, c:c + TN] = (acc + b_ref[:, col + c:col + c + TN]).astype(o_ref.dtype)
            col += width
    _on_row_source(run, (xp_ref, xs_ref))


def _norm_proj(xs, nw, w, b, segments, name):
    n = w.shape[1]
    return pl.pallas_call(
        functools.partial(_norm_proj_body, segments),
        grid=(N_TOK // TM,),
        in_specs=_row_specs(True, D_MODEL) + [
            pl.BlockSpec((1, D_MODEL), lambda i: (0, 0)),
            pl.BlockSpec((D_MODEL, n), lambda i: (0, 0), pipeline_mode=pl.Buffered(1)),
            pl.BlockSpec((1, n), lambda i: (0, 0)),
        ],
        out_specs=[pl.BlockSpec((TM, width), lambda i: (i, 0)) for width, _ in segments],
        out_shape=[jax.ShapeDtypeStruct((N_TOK, width), dtype) for width, dtype in segments],
        compiler_params=_cparams(("arbitrary",)),
        name=name,
    )(*xs, nw.reshape(1, D_MODEL), w, b.reshape(1, n))


def _proj_res_body(ap_ref, as_ref, w_ref, b_ref, rp_ref, rs_ref, op_ref, os_ref):
    def run(a_ref, r_ref, o_ref):
        a = a_ref[...].astype(BF16)
        for c in range(0, D_MODEL, TN):
            acc = jnp.dot(a, w_ref[:, c:c + TN], preferred_element_type=F32)
            o_ref[:, c:c + TN] = r_ref[:, c:c + TN] + acc + b_ref[:, c:c + TN]
    _on_row_source(run, (ap_ref, as_ref), (rp_ref, rs_ref), (op_ref, os_ref))


def _proj_res(a_pair, w, b, res_pair, name):
    k = w.shape[0]
    return pl.pallas_call(
        _proj_res_body,
        grid=(N_TOK // TM,),
        in_specs=_row_specs(True, k) + [
            pl.BlockSpec((k, D_MODEL), lambda i: (0, 0), pipeline_mode=pl.Buffered(1)),
            pl.BlockSpec((1, D_MODEL), lambda i: (0, 0)),
        ] + _row_specs(True, D_MODEL),
        out_specs=_row_specs(True, D_MODEL),
        out_shape=[jax.ShapeDtypeStruct((N_PROMPT, D_MODEL), F32), jax.ShapeDtypeStruct((N_SAMPLE, D_MODEL), F32)],
        compiler_params=_cparams(("arbitrary",)),
        name=name,
    )(*a_pair, w, b.reshape(1, D_MODEL), *res_pair)


def _t5_bucket_table():
    i = np.arange(WINDOW)[None, :]
    j = np.arange(2 * WINDOW)[:, None]
    n = np.maximum(WINDOW + i - j, 0)
    max_exact = NUM_BUCKETS // 2
    nf = np.maximum(n, 1).astype(np.float32)
    large = max_exact + (np.log(nf / np.float32(max_exact)) / np.float32(math.log(MAX_DISTANCE / max_exact))
                         * np.float32(NUM_BUCKETS - max_exact)).astype(np.int32)
    large = np.minimum(large, NUM_BUCKETS - 1)
    return np.where(n < max_exact, n, large).astype(np.int32)


KV_PER_STEP = LANES // HEAD_DIM
N_PAIRS = N_KV_HEADS // KV_PER_STEP
Q_COLS_PER_STEP = KV_PER_STEP * GROUP * HEAD_DIM


def _slot_head(slot):
    pair = slot // (GROUP * KV_PER_STEP)
    g = (slot // KV_PER_STEP) % GROUP
    hh = slot % KV_PER_STEP
    return (pair * KV_PER_STEP + hh) * GROUP + g


def _bias_table_body(bucket_ref, rb_ref, o_ref):
    h = _slot_head(pl.program_id(0))
    bucket = bucket_ref[...]
    acc = jnp.zeros((2 * WINDOW, WINDOW), F32)
    for b in range(NUM_BUCKETS):
        acc = jnp.where(bucket == b, rb_ref[b, h], acc)
    j = lax.broadcasted_iota(jnp.int32, (2 * WINDOW, WINDOW), 0)
    i = lax.broadcasted_iota(jnp.int32, (2 * WINDOW, WINDOW), 1)
    dist = WINDOW + i - j
    o_ref[0] = jnp.where((dist >= 0) & (dist < WINDOW), acc, NEG_INF)


def _bias_table(rel_bias):
    return pl.pallas_call(
        _bias_table_body,
        grid=(N_HEADS,),
        in_specs=[
            pl.BlockSpec((2 * WINDOW, WINDOW), lambda h: (0, 0)),
            pl.BlockSpec(memory_space=pltpu.SMEM),
        ],
        out_specs=pl.BlockSpec((1, 2 * WINDOW, WINDOW), lambda h: (h, 0, 0)),
        out_shape=jax.ShapeDtypeStruct((N_HEADS, 2 * WINDOW, WINDOW), F32),
        name="bias_table",
    )(jnp.asarray(_t5_bucket_table()), rel_bias)


def _softmax_with_sink(s, sink_col):
    m = jnp.maximum(jnp.max(s, axis=-1, keepdims=True), sink_col)
    p = jnp.exp(s - m)
    denom = jnp.sum(p, axis=-1, keepdims=True) + jnp.exp(sink_col - m)
    return p, 1.0 / denom


PAIR_SLOTS = GROUP * KV_PER_STEP
PAIR_COLS = PAIR_SLOTS * WINDOW
ONES_ROWS = 16


def _swa_prompt_body(sink_ref, q_ref, kp_ref, ko_ref, vp_ref, vo_ref, bias_ref, o_ref):
    pair = pl.program_id(0)
    blk = pl.program_id(2)
    head_a = lax.broadcasted_iota(jnp.int32, (WINDOW, LANES), 1) < HEAD_DIM
    k = jnp.concatenate([kp_ref[...], ko_ref[...]], axis=0).astype(BF16)
    v = jnp.concatenate([vp_ref[...], vo_ref[...]], axis=0)
    vt = jnp.concatenate([v.T, jnp.ones((ONES_ROWS, 2 * WINDOW), F32)], axis=0).astype(BF16)
    parts = []
    for g in range(GROUP):
        qg = q_ref[:, g * LANES:(g + 1) * LANES] * HEAD_DIM ** -0.5
        zero = jnp.zeros_like(qg)
        parts += [jnp.where(head_a, qg, zero), jnp.where(head_a, zero, qg)]
    qbd = jnp.concatenate(parts, axis=0)
    st = lax.dot_general(k, qbd, (((1,), (1,)), ((), ())), preferred_element_type=F32)
    st = st + jnp.concatenate([bias_ref[t] for t in range(PAIR_SLOTS)], axis=1)
    no_prev = jnp.where(blk > 0, 0.0, NEG_INF)
    st = jnp.concatenate([st[:WINDOW] + no_prev, st[WINDOW:]], axis=0)
    sink_row = jnp.concatenate(
        [jnp.full((1, WINDOW), sink_ref[_slot_head(pair * PAIR_SLOTS + slot)], F32)
         for slot in range(PAIR_SLOTS)], axis=1)
    m = jnp.maximum(jnp.max(st, axis=0, keepdims=True), sink_row)
    pt = jnp.exp(st - m).astype(BF16)
    oa = jnp.dot(vt, pt, preferred_element_type=F32)
    inv = 1.0 / (oa[LANES:LANES + 1] + jnp.exp(sink_row - m))
    o = oa[:LANES] * inv
    for g in range(GROUP):
        c = g * KV_PER_STEP * WINDOW
        ot = jnp.concatenate([o[:HEAD_DIM, c:c + WINDOW], o[HEAD_DIM:, c + WINDOW:c + 2 * WINDOW]], axis=0)
        o_ref[:, g * LANES:(g + 1) * LANES] = ot.T.astype(o_ref.dtype)


def _swa_prompt(q, kv, bias_tbl, sinks):
    v_col0 = KV_COLS // LANES

    def prev(p, b, i):
        return b * NB + jnp.maximum(i - 1, 0)

    return pl.pallas_call(
        _swa_prompt_body,
        grid=(N_PAIRS, BATCH, NB),
        in_specs=[
            pl.BlockSpec(memory_space=pltpu.SMEM),
            pl.BlockSpec((WINDOW, Q_COLS_PER_STEP), lambda p, b, i: (b * NB + i, p)),
            pl.BlockSpec((WINDOW, LANES), lambda p, b, i: (prev(p, b, i), p)),
            pl.BlockSpec((WINDOW, LANES), lambda p, b, i: (b * NB + i, p)),
            pl.BlockSpec((WINDOW, LANES), lambda p, b, i: (prev(p, b, i), v_col0 + p)),
            pl.BlockSpec((WINDOW, LANES), lambda p, b, i: (b * NB + i, v_col0 + p)),
            pl.BlockSpec((PAIR_SLOTS, 2 * WINDOW, WINDOW), lambda p, b, i: (p, 0, 0)),
        ],
        out_specs=pl.BlockSpec((WINDOW, Q_COLS_PER_STEP), lambda p, b, i: (b * NB + i, p)),
        out_shape=jax.ShapeDtypeStruct((N_PROMPT, Q_DIM), BF16),
        compiler_params=_cparams(("arbitrary", "arbitrary", "arbitrary")),
        name="swa_prompt",
    )(sinks, q, kv, kv, kv, kv, bias_tbl)


S_ROWS = N_KV_HEADS * GROUP * DEC_SEQ


def _swa_sample_body(q_ref, kn_ref, vn_ref, ck_ref, cv_ref, bias_ref, sink_ref,
                     o_ref, ko_ref, vo_ref):
    x = (q_ref[0] * HEAD_DIM ** -0.5).astype(BF16)
    xt = jnp.concatenate([x] * N_KV_HEADS, axis=1)
    row_kv = lax.broadcasted_iota(jnp.int32, (S_ROWS, KV_COLS), 0) // (GROUP * DEC_SEQ)
    col_kv = lax.broadcasted_iota(jnp.int32, (S_ROWS, KV_COLS), 1) // HEAD_DIM
    own = row_kv == col_kv
    qbd = jnp.where(own, xt, jnp.zeros_like(xt))
    pad = jnp.zeros((WINDOW - SUBLANES, KV_COLS), F32)
    kk = jnp.concatenate([ck_ref[0], kn_ref[0], pad], axis=0).astype(BF16)
    vv = jnp.concatenate([cv_ref[0], vn_ref[0], pad], axis=0).astype(BF16)
    s = lax.dot_general(qbd, kk, (((1,), (1,)), ((), ())), preferred_element_type=F32)
    s = s + bias_ref[...]
    p, inv = _softmax_with_sink(s, sink_ref[...])
    of = jnp.dot(p.astype(BF16), vv, preferred_element_type=F32)
    of = jnp.where(own, of, 0.0)
    o = of[:, 0:HEAD_DIM]
    for c in range(1, N_KV_HEADS):
        o = o + of[:, c * HEAD_DIM:(c + 1) * HEAD_DIM]
    o_ref[0] = o * inv
    keep = WINDOW - DEC_SEQ
    ko_ref[0, 0:keep, :] = ck_ref[0, DEC_SEQ:WINDOW, :]
    ko_ref[0, keep:WINDOW, :] = kn_ref[0, 0:DEC_SEQ, :]
    vo_ref[0, 0:keep, :] = cv_ref[0, DEC_SEQ:WINDOW, :]
    vo_ref[0, keep:WINDOW, :] = vn_ref[0, 0:DEC_SEQ, :]


def _swa_sample(q_rows, k_new8, v_new8, cache_k, cache_v, bias_s, sink_col):
    seq3 = lambda s: (s, 0, 0)
    full2 = lambda s: (0, 0)
    return pl.pallas_call(
        _swa_sample_body,
        grid=(DEC_BATCH,),
        in_specs=[
            pl.BlockSpec((1, S_ROWS, HEAD_DIM), seq3),
            pl.BlockSpec((1, SUBLANES, KV_COLS), seq3),
            pl.BlockSpec((1, SUBLANES, KV_COLS), seq3),
            pl.BlockSpec((1, WINDOW, KV_COLS), seq3),
            pl.BlockSpec((1, WINDOW, KV_COLS), seq3),
            pl.BlockSpec((S_ROWS, 2 * WINDOW), full2),
            pl.BlockSpec((S_ROWS, 1), full2),
        ],
        out_specs=[
            pl.BlockSpec((1, S_ROWS, HEAD_DIM), seq3),
            pl.BlockSpec((1, WINDOW, KV_COLS), seq3),
            pl.BlockSpec((1, WINDOW, KV_COLS), seq3),
        ],
        out_shape=[
            jax.ShapeDtypeStruct((DEC_BATCH, S_ROWS, HEAD_DIM), F32),
            jax.ShapeDtypeStruct((DEC_BATCH, WINDOW, KV_COLS), F32),
            jax.ShapeDtypeStruct((DEC_BATCH, WINDOW, KV_COLS), F32),
        ],
        compiler_params=_cparams(("parallel",)),
        name="swa_sample",
    )(q_rows, k_new8, v_new8, cache_k, cache_v, bias_s, sink_col)


def _log_sigmoid(x):
    return jnp.minimum(x, 0.0) - jnp.log1p(jnp.exp(-jnp.abs(x)))


def _gla_gate_body(xp_ref, xs_ref, nw_ref, w1_ref, w2_ref, b_ref, o_ref):
    def run(x_ref):
        h = _rms(x_ref[...], nw_ref[...]).astype(BF16)
        gk = jnp.dot(h, w1_ref[...], preferred_element_type=F32)
        z = jnp.dot(gk.astype(BF16), w2_ref[...], preferred_element_type=F32) + b_ref[...]
        o_ref[...] = _log_sigmoid(z) / GATE_NORMALIZER
    _on_row_source(run, (xp_ref, xs_ref))


def _gla_gate(xs, nw, w1p, w2p, b):
    m = N_TOK
    return pl.pallas_call(
        _gla_gate_body,
        grid=(m // TM,),
        in_specs=_row_specs(True, D_MODEL) + [
            pl.BlockSpec((1, D_MODEL), lambda i: (0, 0)),
            pl.BlockSpec((D_MODEL, LANES), lambda i: (0, 0)),
            pl.BlockSpec((LANES, GLA_KEY_DIM), lambda i: (0, 0)),
            pl.BlockSpec((1, GLA_KEY_DIM), lambda i: (0, 0)),
        ],
        out_specs=pl.BlockSpec((TM, GLA_KEY_DIM), lambda i: (i, 0)),
        out_shape=jax.ShapeDtypeStruct((m, GLA_KEY_DIM), F32),
        compiler_params=_cparams(("parallel",)),
        name="gla_gate",
    )(*xs, nw.reshape(1, D_MODEL), w1p, w2p, b.reshape(1, GLA_KEY_DIM))


def _gla_out(o, gate, norm_w):
    return _rms(o, norm_w) * _silu(gate)


def _split3(x):
    hi = x.astype(BF16)
    r = x - hi.astype(F32)
    mid = r.astype(BF16)
    lo = (r - mid.astype(F32)).astype(BF16)
    return hi, mid, lo


def _cumsum_rows(g):
    c = g.shape[0]
    tri = (lax.broadcasted_iota(jnp.int32, (c, c), 0) >= lax.broadcasted_iota(jnp.int32, (c, c), 1)).astype(BF16)
    return jnp.dot(jnp.concatenate([tri] * 3, axis=1), jnp.concatenate(_split3(g), axis=0),
                   preferred_element_type=F32)


def _causal(a):
    c = a.shape[0]
    keep = lax.broadcasted_iota(jnp.int32, (c, c), 0) >= lax.broadcasted_iota(jnp.int32, (c, c), 1)
    return jnp.where(keep, a, 0.0)


_NT = (((1,), (1,)), ((), ()))
_TN = (((0,), (0,)), ((), ()))


def _gla_prompt_body(*refs):
    proj_refs = refs[:BATCH]
    la_refs = refs[BATCH:2 * BATCH]
    nw_ref, o_ref, s_ref, st_ref = refs[2 * BATCH:]
    c = pl.program_id(0)

    @pl.when(c == 0)
    def _():
        st_ref[...] = jnp.zeros_like(st_ref)

    for bi in range(BATCH):
        p_ref = proj_refs[bi]
        b_all = _cumsum_rows(la_refs[bi][...])
        for h in range(GLA_HEADS):
            kc = slice(h * GLA_DK, (h + 1) * GLA_DK)
            v0 = 2 * GLA_KEY_DIM + h * GLA_DV
            b = b_all[:, kc]
            q = p_ref[:, kc].astype(F32)
            k = p_ref[:, GLA_KEY_DIM + h * GLA_DK:GLA_KEY_DIM + (h + 1) * GLA_DK].astype(F32)
            v = p_ref[:, v0:v0 + GLA_DV]
            gate = p_ref[:, v0 + GLA_VAL_DIM:v0 + GLA_VAL_DIM + GLA_DV].astype(F32)
            qe = (q * GLA_DK ** -0.5 * jnp.exp(b)).astype(BF16)
            ke = (k * jnp.exp(-b)).astype(BF16)
            a = _causal(lax.dot_general(qe, ke, _NT, preferred_element_type=F32))
            st = st_ref[bi, h]
            o = (jnp.dot(a.astype(BF16), v, preferred_element_type=F32)
                 + lax.dot_general(qe, st.astype(BF16), _NT, preferred_element_type=F32))
            b_last = b[GLA_C - 1:GLA_C, :]
            kd = (k * jnp.exp(b_last - b)).astype(BF16)
            st_new = st * jnp.exp(b_last) + lax.dot_general(v, kd, _TN, preferred_element_type=F32)
            st_ref[bi, h] = st_new
            o_ref[bi, :, h * GLA_DV:(h + 1) * GLA_DV] = _gla_out(o, gate, nw_ref[...]).astype(o_ref.dtype)

    @pl.when(c == pl.num_programs(0) - 1)
    def _():
        for bi in range(BATCH):
            for h in range(GLA_HEADS):
                s_ref[bi, h] = st_ref[bi, h].T


def _gla_prompt(proj, log_a, norm_w):
    nc = SEQ // GLA_C
    rows = [functools.partial(lambda bi, c: (bi * nc + c, 0), bi) for bi in range(BATCH)]
    o, s = pl.pallas_call(
        _gla_prompt_body,
        grid=(nc,),
        in_specs=([pl.BlockSpec((GLA_C, GLA_MAIN_DIM), r) for r in rows]
                  + [pl.BlockSpec((GLA_C, GLA_KEY_DIM), r) for r in rows]
                  + [pl.BlockSpec((1, GLA_DV), lambda c: (0, 0))]),
        out_specs=[
            pl.BlockSpec((BATCH, GLA_C, GLA_VAL_DIM), lambda c: (0, c, 0)),
            pl.BlockSpec((BATCH, GLA_HEADS, GLA_DK, GLA_DV), lambda c: (0, 0, 0, 0)),
        ],
        out_shape=[
            jax.ShapeDtypeStruct((BATCH, SEQ, GLA_VAL_DIM), BF16),
            jax.ShapeDtypeStruct((BATCH, GLA_HEADS, GLA_DK, GLA_DV), F32),
        ],
        scratch_shapes=[pltpu.VMEM((BATCH, GLA_HEADS, GLA_DV, GLA_DK), F32)],
        compiler_params=_cparams(("arbitrary",)),
        name="gla_prompt",
    )(*([proj] * BATCH + [log_a] * BATCH + [norm_w.reshape(1, GLA_DV)]))
    return o.reshape(N_PROMPT, GLA_VAL_DIM), s


GLA_SEQS_PER_STEP = 2


def _gla_sample_body(proj_ref, la_ref, s0_ref, nw_ref, o_ref, s_ref):
    ones = jnp.ones((DEC_SEQ, LANES), BF16)
    for i in range(GLA_SEQS_PER_STEP):
        for h in range(GLA_HEADS):
            q = proj_ref[i, :, h * GLA_DK:(h + 1) * GLA_DK]
            k = proj_ref[i, :, GLA_KEY_DIM + h * GLA_DK:GLA_KEY_DIM + (h + 1) * GLA_DK]
            v0 = 2 * GLA_KEY_DIM + h * GLA_DV
            v = proj_ref[i, :, v0:v0 + GLA_DV].astype(BF16)
            gate = proj_ref[i, :, v0 + GLA_VAL_DIM:v0 + GLA_VAL_DIM + GLA_DV]
            g = la_ref[i, :, h * GLA_DK:(h + 1) * GLA_DK]
            rows = [g[0:1]]
            for t in range(1, DEC_SEQ):
                rows.append(rows[-1] + g[t:t + 1])
            b = jnp.concatenate(rows, axis=0)
            b_last = rows[-1]
            qe = (q * GLA_DK ** -0.5 * jnp.exp(b)).astype(BF16)
            ke = (k * jnp.exp(-b)).astype(BF16)
            a = _causal(lax.dot_general(qe, ke, _NT, preferred_element_type=F32))
            s0 = s0_ref[i, h]
            o = (jnp.dot(a.astype(BF16), v, preferred_element_type=F32)
                 + jnp.dot(qe, s0.astype(BF16), preferred_element_type=F32))
            kd = (k * jnp.exp(b_last - b)).astype(BF16)
            dsum = sum(lax.dot_general(piece, ones, _TN, preferred_element_type=F32) for piece in _split3(g))
            decay = jnp.concatenate([jnp.exp(dsum)] * (GLA_DV // LANES), axis=1)
            s_ref[i, h] = s0 * decay + lax.dot_general(kd, v, _TN, preferred_element_type=F32)
            o_ref[i, :, h * GLA_DV:(h + 1) * GLA_DV] = _gla_out(o, gate, nw_ref[...])


def _gla_sample(proj3, log_a3, state, norm_w):
    n = GLA_SEQS_PER_STEP
    seq3 = lambda s: (s, 0, 0)
    seq4 = lambda s: (s, 0, 0, 0)
    return pl.pallas_call(
        _gla_sample_body,
        grid=(DEC_BATCH // n,),
        in_specs=[
            pl.BlockSpec((n, DEC_SEQ, GLA_MAIN_DIM), seq3),
            pl.BlockSpec((n, DEC_SEQ, GLA_KEY_DIM), seq3),
            pl.BlockSpec((n, GLA_HEADS, GLA_DK, GLA_DV), seq4),
            pl.BlockSpec((1, GLA_DV), lambda s: (0, 0)),
        ],
        out_specs=[
            pl.BlockSpec((n, DEC_SEQ, GLA_VAL_DIM), seq3),
            pl.BlockSpec((n, GLA_HEADS, GLA_DK, GLA_DV), seq4),
        ],
        out_shape=[
            jax.ShapeDtypeStruct((DEC_BATCH, DEC_SEQ, GLA_VAL_DIM), F32),
            jax.ShapeDtypeStruct((DEC_BATCH, GLA_HEADS, GLA_DK, GLA_DV), F32),
        ],
        compiler_params=_cparams(("parallel",)),
        name="gla_sample",
    )(proj3, log_a3, state, norm_w.reshape(1, GLA_DV))


def _swa_layer(x, cache_k, cache_v, norm_w, w_qkv, b_qkv, w_o, b_o, sinks, rel_bias):
    slots = (N_PAIRS, KV_PER_STEP, GROUP, HEAD_DIM)
    w_q = w_qkv[:, :Q_DIM].reshape((D_MODEL,) + slots).transpose(0, 1, 3, 2, 4).reshape(D_MODEL, Q_DIM)
    b_q = b_qkv[:Q_DIM].reshape(slots).transpose(0, 2, 1, 3).reshape(Q_DIM)
    w_qkv_s = jnp.concatenate([w_q, w_qkv[:, Q_DIM:]], axis=1).astype(BF16)
    b_qkv_s = jnp.concatenate([b_q, b_qkv[Q_DIM:]])
    w_o_s = w_o.reshape(slots + (D_MODEL,)).transpose(0, 2, 1, 3, 4).reshape(Q_DIM, D_MODEL).astype(BF16)

    q, kv = _norm_proj(x, norm_w, w_qkv_s, b_qkv_s, ((Q_DIM, BF16), (2 * KV_COLS, F32)), "swa_qkv")
    bias_tbl = _bias_table(rel_bias)
    o_p = _swa_prompt(q, kv, bias_tbl, sinks)

    q_rows = (q[N_PROMPT:].astype(F32).reshape(DEC_BATCH, DEC_SEQ, N_PAIRS, GROUP, KV_PER_STEP, HEAD_DIM)
              .transpose(0, 2, 4, 3, 1, 5).reshape(DEC_BATCH, S_ROWS, HEAD_DIM))
    kv_s = kv[N_PROMPT:].reshape(DEC_BATCH, DEC_SEQ, 2 * KV_COLS)
    pad8 = ((0, 0), (0, SUBLANES - DEC_SEQ), (0, 0))
    k_new8 = jnp.pad(kv_s[..., :KV_COLS], pad8)
    v_new8 = jnp.pad(kv_s[..., KV_COLS:], pad8)
    bias_s = (bias_tbl[:, :, :DEC_SEQ].reshape(N_PAIRS, GROUP, KV_PER_STEP, 2 * WINDOW, DEC_SEQ)
              .transpose(0, 2, 1, 4, 3).reshape(S_ROWS, 2 * WINDOW))
    sink_col = jnp.repeat(sinks, DEC_SEQ).reshape(S_ROWS, 1)
    o_s, k_s, v_s = _swa_sample(q_rows, k_new8, v_new8,
                                cache_k.reshape(DEC_BATCH, WINDOW, KV_COLS),
                                cache_v.reshape(DEC_BATCH, WINDOW, KV_COLS), bias_s, sink_col)
    o_s = (o_s.reshape(DEC_BATCH, N_PAIRS, KV_PER_STEP, GROUP, DEC_SEQ, HEAD_DIM)
           .transpose(0, 4, 1, 3, 2, 5).reshape(N_SAMPLE, Q_DIM))
    x = _proj_res((o_p, o_s), w_o_s, b_o, x, "swa_out")

    kv_p = kv[:N_PROMPT].reshape(BATCH, SEQ, 2 * KV_COLS)[:, SEQ - WINDOW:]
    k_p = kv_p[..., :KV_COLS].reshape(BATCH, WINDOW, N_KV_HEADS, HEAD_DIM)
    v_p = kv_p[..., KV_COLS:].reshape(BATCH, WINDOW, N_KV_HEADS, HEAD_DIM)
    shape_s = (DEC_BATCH, WINDOW, N_KV_HEADS, HEAD_DIM)
    return x, k_p, v_p, k_s.reshape(shape_s), v_s.reshape(shape_s)


def _gla_layer(x, state, norm_w, w_in, w_gk2, b_gk, gnorm, w_o):
    w_main = w_in[:, :GLA_MAIN_DIM].astype(BF16)
    w1p = jnp.pad(w_in[:, GLA_MAIN_DIM:], ((0, 0), (0, LANES - GATE_RANK))).astype(BF16)
    w2p = jnp.pad(w_gk2, ((0, LANES - GATE_RANK), (0, 0))).astype(BF16)
    (proj,) = _norm_proj(x, norm_w, w_main, jnp.zeros((GLA_MAIN_DIM,), F32), ((GLA_MAIN_DIM, BF16),), "gla_in")
    log_a = _gla_gate(x, norm_w, w1p, w2p, b_gk)
    o_p, s_p = _gla_prompt(proj, log_a, gnorm)
    proj_s = proj[N_PROMPT:].astype(F32).reshape(DEC_BATCH, DEC_SEQ, GLA_MAIN_DIM)
    log_a_s = log_a[N_PROMPT:].reshape(DEC_BATCH, DEC_SEQ, GLA_KEY_DIM)
    o_s, s_s = _gla_sample(proj_s, log_a_s, state, gnorm)
    x = _proj_res((o_p, o_s.reshape(N_SAMPLE, GLA_VAL_DIM)), w_o.astype(BF16), jnp.zeros((D_MODEL,), F32), x,
                  "gla_out")
    return x, s_p, s_s


def kernel(x_prompt, x_sample, cache_swa_k, cache_swa_v, state_gla, norm_ffn1, ffn1_w_gate, ffn1_w_up,
           ffn1_w_down, norm_mix, norm_ffn2, ffn2_w_gate, ffn2_w_up, ffn2_w_down, norm_final, rel_bias,
           swa_w_qkv, swa_b_qkv, swa_w_o, swa_b_o, swa_sinks, gla_w_in, gla_w_gk2, gla_b_gk, gla_norm,
           gla_w_o):
    x = (x_prompt.reshape(N_PROMPT, D_MODEL), x_sample.reshape(N_SAMPLE, D_MODEL))
    swa_kp, swa_vp, swa_ks, swa_vs, gla_sp, gla_ss = [], [], [], [], [], []
    for i in range(DEPTH):
        x = _ffn_pair(x, i, norm_ffn1[i], ffn1_w_gate, ffn1_w_up, ffn1_w_down)
        j = i // 2
        if i % 2 == 0:
            x, kp, vp, ks, vs = _swa_layer(x, cache_swa_k[j], cache_swa_v[j], norm_mix[i], swa_w_qkv[j],
                                           swa_b_qkv[j], swa_w_o[j], swa_b_o[j], swa_sinks[j], rel_bias)
            swa_kp.append(kp)
            swa_vp.append(vp)
            swa_ks.append(ks)
            swa_vs.append(vs)
        else:
            x, sp, ss = _gla_layer(x, state_gla[j], norm_mix[i], gla_w_in[j], gla_w_gk2[j], gla_b_gk[j],
                                   gla_norm[j], gla_w_o[j])
            gla_sp.append(sp)
            gla_ss.append(ss)
        final_w = norm_final if i == DEPTH - 1 else None
        x = _ffn_pair(x, i, norm_ffn2[i], ffn2_w_gate, ffn2_w_up, ffn2_w_down, final_w)
    y_prompt = x[0].reshape(BATCH, SEQ, D_MODEL)
    y_sample = x[1].reshape(DEC_BATCH, DEC_SEQ, D_MODEL)
    return (y_prompt, y_sample, jnp.stack(swa_kp), jnp.stack(swa_vp), jnp.stack(swa_ks), jnp.stack(swa_vs),
            jnp.stack(gla_sp), jnp.stack(gla_ss))
```

```python
import functools
import math

import numpy as np
import jax
import jax.numpy as jnp
from jax import lax
from jax.experimental import pallas as pl
from jax.experimental.pallas import tpu as pltpu

F32 = jnp.float32
BF16 = jnp.bfloat16

D_MODEL = 2048
BATCH = 2
SEQ = 4096
DEPTH = 2
DEC_BATCH = 128
DEC_SEQ = 4
RMS_EPS = 1e-6
D_FF = 5632
N_HEADS = 32
N_KV_HEADS = 8
HEAD_DIM = 64
GROUP = N_HEADS // N_KV_HEADS
WINDOW = 128
NUM_BUCKETS = 32
MAX_DISTANCE = 128
NEG_INF = -1e30
GLA_HEADS = 4
GLA_DK = 256
GLA_DV = 512
GLA_KEY_DIM = GLA_HEADS * GLA_DK
GLA_VAL_DIM = GLA_HEADS * GLA_DV
GATE_RANK = 16
GATE_NORMALIZER = 16.0
GLA_MAIN_DIM = 2 * GLA_KEY_DIM + 2 * GLA_VAL_DIM
Q_DIM = N_HEADS * HEAD_DIM
KV_COLS = N_KV_HEADS * HEAD_DIM

N_PROMPT = BATCH * SEQ
N_SAMPLE = DEC_BATCH * DEC_SEQ
N_TOK = N_PROMPT + N_SAMPLE

LANES = 128
SUBLANES = 8
VMEM_LIMIT = 56 * 1024 * 1024

TM = 512
TF = 512
TN = 512
GLA_C = 64
NB = SEQ // WINDOW
N_PROMPT_TILES = N_PROMPT // TM


def _rms(x, w):
    return x * lax.rsqrt(jnp.mean(x * x, axis=-1, keepdims=True) + RMS_EPS) * w


def _silu(x):
    return x * jax.nn.sigmoid(x)


def _cparams(sem):
    return pltpu.CompilerParams(dimension_semantics=sem, vmem_limit_bytes=VMEM_LIMIT)


def _row_specs(split, width):
    if not split:
        return [pl.BlockSpec((TM, width), lambda i, *_: (i, 0))]
    return [pl.BlockSpec((TM, width), lambda i, *_: (jnp.minimum(i, N_PROMPT_TILES - 1), 0)),
            pl.BlockSpec((TM, width), lambda i, *_: (jnp.maximum(i - N_PROMPT_TILES, 0), 0))]


def _on_row_source(fn, *ref_groups):
    if all(len(g) == 1 for g in ref_groups):
        fn(*[g[0] for g in ref_groups])
        return
    i = pl.program_id(0)
    pl.when(i < N_PROMPT_TILES)(lambda: fn(*[g[0] for g in ref_groups]))
    pl.when(i >= N_PROMPT_TILES)(lambda: fn(*[g[-1] for g in ref_groups]))


FFN_TM = 1024
FFN_TF = 256
FFN_TF_SAMPLE = 512
FFN_VMEM_LIMIT = 60 * 1024 * 1024


def _ffn_body(final_norm, emit_w, x_ref, nw_ref, wg_ref, wu_ref, wd_ref, *rest):
    rest = list(rest)
    fw_ref = rest.pop(0) if final_norm else None
    o_ref = rest.pop(0)
    h_ref = rest.pop()
    j = pl.program_id(1)

    @pl.when(j == 0)
    def _():
        x = x_ref[...]
        h_ref[...] = _rms(x, nw_ref[...]).astype(BF16)
        o_ref[...] = x

    if emit_w:
        wgo_ref, wuo_ref, wdo_ref = rest
        wgo_ref[...] = wg_ref[0].astype(BF16)
        wuo_ref[...] = wu_ref[0].astype(BF16)
        wdo_ref[...] = wd_ref[0].astype(BF16)
        wg, wu, wd = wgo_ref[...], wuo_ref[...], wdo_ref[...]
    else:
        wg, wu, wd = wg_ref[...], wu_ref[...], wd_ref[...]
    h = h_ref[...]
    g = jnp.dot(h, wg, preferred_element_type=F32)
    u = jnp.dot(h, wu, preferred_element_type=F32)
    a = (_silu(g) * (0.5 * u)).astype(BF16)
    o_ref[...] += jnp.dot(a, wd, preferred_element_type=F32)

    if final_norm:
        @pl.when(j == pl.num_programs(1) - 1)
        def _():
            o_ref[...] = _rms(o_ref[...], fw_ref[...])


def _ffn(x, layer, nw, wg, wu, wd, final_w=None):
    m = x.shape[0]
    emit_w = wg.dtype == F32
    tm = min(FFN_TM, m)
    tf = FFN_TF if emit_w else FFN_TF_SAMPLE
    nj = D_FF // tf
    final_norm = final_w is not None
    vec = pl.BlockSpec((1, D_MODEL), lambda i, j: (0, 0))
    if emit_w:
        w_specs = [
            pl.BlockSpec((1, D_MODEL, tf), lambda i, j: (layer, 0, j)),
            pl.BlockSpec((1, D_MODEL, tf), lambda i, j: (layer, 0, j)),
            pl.BlockSpec((1, tf, D_MODEL), lambda i, j: (layer, j, 0)),
        ]
    else:
        w_specs = [
            pl.BlockSpec((D_MODEL, tf), lambda i, j: (0, j)),
            pl.BlockSpec((D_MODEL, tf), lambda i, j: (0, j)),
            pl.BlockSpec((tf, D_MODEL), lambda i, j: (j, 0)),
        ]
    in_specs = [pl.BlockSpec((tm, D_MODEL), lambda i, j: (i, 0)), vec] + w_specs
    args = [x, nw.reshape(1, D_MODEL), wg, wu, wd]
    if final_norm:
        in_specs.append(vec)
        args.append(final_w.reshape(1, D_MODEL))
    out_specs = [pl.BlockSpec((tm, D_MODEL), lambda i, j: (i, 0))]
    out_shape = [jax.ShapeDtypeStruct((m, D_MODEL), F32)]
    if emit_w:
        once = lambda i, j: jnp.where(i == 0, j, nj - 1)
        out_specs += [
            pl.BlockSpec((D_MODEL, tf), lambda i, j: (0, once(i, j))),
            pl.BlockSpec((D_MODEL, tf), lambda i, j: (0, once(i, j))),
            pl.BlockSpec((tf, D_MODEL), lambda i, j: (once(i, j), 0)),
        ]
        out_shape += [
            jax.ShapeDtypeStruct((D_MODEL, D_FF), BF16),
            jax.ShapeDtypeStruct((D_MODEL, D_FF), BF16),
            jax.ShapeDtypeStruct((D_FF, D_MODEL), BF16),
        ]
    outs = pl.pallas_call(
        functools.partial(_ffn_body, final_norm, emit_w),
        grid=(m // tm, nj),
        in_specs=in_specs,
        out_specs=out_specs,
        out_shape=out_shape,
        scratch_shapes=[pltpu.VMEM((tm, D_MODEL), BF16)],
        compiler_params=pltpu.CompilerParams(dimension_semantics=("arbitrary", "arbitrary"),
                                             vmem_limit_bytes=FFN_VMEM_LIMIT),
        name="ffn",
    )(*args)
    return outs[0], tuple(outs[1:])


def _ffn_pair(x_pair, layer, nw, wg, wu, wd, final_w=None):
    y_prompt, w_bf16 = _ffn(x_pair[0], layer, nw, wg, wu, wd, final_w)
    y_sample, _ = _ffn(x_pair[1], layer, nw, *w_bf16, final_w)
    return y_prompt, y_sample


def _norm_proj_body(segments, xp_ref, xs_ref, nw_ref, w_ref, b_ref, *o_refs):
    def run(x_ref):
        h = _rms(x_ref[...], nw_ref[...]).astype(BF16)
        col = 0
        for (width, _), o_ref in zip(segments, o_refs):
            for c in range(0, width, TN):
                acc = jnp.dot(h, w_ref[:, col + c:col + c + TN], preferred_element_type=F32)
                o_ref[:, c:c + TN] = (acc + b_ref[:, col + c:col + c + TN]).astype(o_ref.dtype)
            col += width
    _on_row_source(run, (xp_ref, xs_ref))


def _norm_proj(xs, nw, w, b, segments, name):
    n = w.shape[1]
    return pl.pallas_call(
        functools.partial(_norm_proj_body, segments),
        grid=(N_TOK // TM,),
        in_specs=_row_specs(True, D_MODEL) + [
            pl.BlockSpec((1, D_MODEL), lambda i: (0, 0)),
            pl.BlockSpec((D_MODEL, n), lambda i: (0, 0), pipeline_mode=pl.Buffered(1)),
            pl.BlockSpec((1, n), lambda i: (0, 0)),
        ],
        out_specs=[pl.BlockSpec((TM, width), lambda i: (i, 0)) for width, _ in segments],
        out_shape=[jax.ShapeDtypeStruct((N_TOK, width), dtype) for width, dtype in segments],
        compiler_params=_cparams(("arbitrary",)),
        name=name,
    )(*xs, nw.reshape(1, D_MODEL), w, b.reshape(1, n))


def _proj_res_body(ap_ref, as_ref, w_ref, b_ref, rp_ref, rs_ref, op_ref, os_ref):
    def run(a_ref, r_ref, o_ref):
        a = a_ref[...].astype(BF16)
        for c in range(0, D_MODEL, TN):
            acc = jnp.dot(a, w_ref[:, c:c + TN], preferred_element_type=F32)
            o_ref[:, c:c + TN] = r_ref[:, c:c + TN] + acc + b_ref[:, c:c + TN]
    _on_row_source(run, (ap_ref, as_ref), (rp_ref, rs_ref), (op_ref, os_ref))


def _proj_res(a_pair, w, b, res_pair, name):
    k = w.shape[0]
    return pl.pallas_call(
        _proj_res_body,
        grid=(N_TOK // TM,),
        in_specs=_row_specs(True, k) + [
            pl.BlockSpec((k, D_MODEL), lambda i: (0, 0), pipeline_mode=pl.Buffered(1)),
            pl.BlockSpec((1, D_MODEL), lambda i: (0, 0)),
        ] + _row_specs(True, D_MODEL),
        out_specs=_row_specs(True, D_MODEL),
        out_shape=[jax.ShapeDtypeStruct((N_PROMPT, D_MODEL), F32), jax.ShapeDtypeStruct((N_SAMPLE, D_MODEL), F32)],
        compiler_params=_cparams(("arbitrary",)),
        name=name,
    )(*a_pair, w, b.reshape(1, D_MODEL), *res_pair)


def _t5_bucket_table():
    i = np.arange(WINDOW)[None, :]
    j = np.arange(2 * WINDOW)[:, None]
    n = np.maximum(WINDOW + i - j, 0)
    max_exact = NUM_BUCKETS // 2
    nf = np.maximum(n, 1).astype(np.float32)
    large = max_exact + (np.log(nf / np.float32(max_exact)) / np.float32(math.log(MAX_DISTANCE / max_exact))
                         * np.float32(NUM_BUCKETS - max_exact)).astype(np.int32)
    large = np.minimum(large, NUM_BUCKETS - 1)
    return np.where(n < max_exact, n, large).astype(np.int32)


KV_PER_STEP = LANES // HEAD_DIM
N_PAIRS = N_KV_HEADS // KV_PER_STEP
Q_COLS_PER_STEP = KV_PER_STEP * GROUP * HEAD_DIM


def _slot_head(slot):
    pair = slot // (GROUP * KV_PER_STEP)
    g = (slot // KV_PER_STEP) % GROUP
    hh = slot % KV_PER_STEP
    return (pair * KV_PER_STEP + hh) * GROUP + g


def _bias_table_body(bucket_ref, rb_ref, o_ref):
    h = _slot_head(pl.program_id(0))
    bucket = bucket_ref[...]
    acc = jnp.zeros((2 * WINDOW, WINDOW), F32)
    for b in range(NUM_BUCKETS):
        acc = jnp.where(bucket == b, rb_ref[b, h], acc)
    j = lax.broadcasted_iota(jnp.int32, (2 * WINDOW, WINDOW), 0)
    i = lax.broadcasted_iota(jnp.int32, (2 * WINDOW, WINDOW), 1)
    dist = WINDOW + i - j
    o_ref[0] = jnp.where((dist >= 0) & (dist < WINDOW), acc, NEG_INF)


def _bias_table(rel_bias):
    return pl.pallas_call(
        _bias_table_body,
        grid=(N_HEADS,),
        in_specs=[
            pl.BlockSpec((2 * WINDOW, WINDOW), lambda h: (0, 0)),
            pl.BlockSpec(memory_space=pltpu.SMEM),
        ],
        out_specs=pl.BlockSpec((1, 2 * WINDOW, WINDOW), lambda h: (h, 0, 0)),
        out_shape=jax.ShapeDtypeStruct((N_HEADS, 2 * WINDOW, WINDOW), F32),
        name="bias_table",
    )(jnp.asarray(_t5_bucket_table()), rel_bias)


def _softmax_with_sink(s, sink_col):
    m = jnp.maximum(jnp.max(s, axis=-1, keepdims=True), sink_col)
    p = jnp.exp(s - m)
    denom = jnp.sum(p, axis=-1, keepdims=True) + jnp.exp(sink_col - m)
    return p, 1.0 / denom


PAIR_SLOTS = GROUP * KV_PER_STEP
PAIR_COLS = PAIR_SLOTS * WINDOW
ONES_ROWS = 16


def _swa_prompt_body(sink_ref, q_ref, kp_ref, ko_ref, vp_ref, vo_ref, bias_ref, o_ref):
    pair = pl.program_id(0)
    blk = pl.program_id(2)
    head_a = lax.broadcasted_iota(jnp.int32, (WINDOW, LANES), 1) < HEAD_DIM
    k = jnp.concatenate([kp_ref[...], ko_ref[...]], axis=0).astype(BF16)
    v = jnp.concatenate([vp_ref[...], vo_ref[...]], axis=0)
    vt = jnp.concatenate([v.T, jnp.ones((ONES_ROWS, 2 * WINDOW), F32)], axis=0).astype(BF16)
    parts = []
    for g in range(GROUP):
        qg = q_ref[:, g * LANES:(g + 1) * LANES] * HEAD_DIM ** -0.5
        zero = jnp.zeros_like(qg)
        parts += [jnp.where(head_a, qg, zero), jnp.where(head_a, zero, qg)]
    qbd = jnp.concatenate(parts, axis=0)
    st = lax.dot_general(k, qbd, (((1,), (1,)), ((), ())), preferred_element_type=F32)
    st = st + jnp.concatenate([bias_ref[t] for t in range(PAIR_SLOTS)], axis=1)
    no_prev = jnp.where(blk > 0, 0.0, NEG_INF)
    st = jnp.concatenate([st[:WINDOW] + no_prev, st[WINDOW:]], axis=0)
    sink_row = jnp.concatenate(
        [jnp.full((1, WINDOW), sink_ref[_slot_head(pair * PAIR_SLOTS + slot)], F32)
         for slot in range(PAIR_SLOTS)], axis=1)
    m = jnp.maximum(jnp.max(st, axis=0, keepdims=True), sink_row)
    pt = jnp.exp(st - m).astype(BF16)
    oa = jnp.dot(vt, pt, preferred_element_type=F32)
    inv = 1.0 / (oa[LANES:LANES + 1] + jnp.exp(sink_row - m))
    o = oa[:LANES] * inv
    for g in range(GROUP):
        c = g * KV_PER_STEP * WINDOW
        ot = jnp.concatenate([o[:HEAD_DIM, c:c + WINDOW], o[HEAD_DIM:, c + WINDOW:c + 2 * WINDOW]], axis=0)
        o_ref[:, g * LANES:(g + 1) * LANES] = ot.T.astype(o_ref.dtype)


def _swa_prompt(q, kv, bias_tbl, sinks):
    v_col0 = KV_COLS // LANES

    def prev(p, b, i):
        return b * NB + jnp.maximum(i - 1, 0)

    return pl.pallas_call(
        _swa_prompt_body,
        grid=(N_PAIRS, BATCH, NB),
        in_specs=[
            pl.BlockSpec(memory_space=pltpu.SMEM),
            pl.BlockSpec((WINDOW, Q_COLS_PER_STEP), lambda p, b, i: (b * NB + i, p)),
            pl.BlockSpec((WINDOW, LANES), lambda p, b, i: (prev(p, b, i), p)),
            pl.BlockSpec((WINDOW, LANES), lambda p, b, i: (b * NB + i, p)),
            pl.BlockSpec((WINDOW, LANES), lambda p, b, i: (prev(p, b, i), v_col0 + p)),
            pl.BlockSpec((WINDOW, LANES), lambda p, b, i: (b * NB + i, v_col0 + p)),
            pl.BlockSpec((PAIR_SLOTS, 2 * WINDOW, WINDOW), lambda p, b, i: (p, 0, 0)),
        ],
        out_specs=pl.BlockSpec((WINDOW, Q_COLS_PER_STEP), lambda p, b, i: (b * NB + i, p)),
        out_shape=jax.ShapeDtypeStruct((N_PROMPT, Q_DIM), BF16),
        compiler_params=_cparams(("arbitrary", "arbitrary", "arbitrary")),
        name="swa_prompt",
    )(sinks, q, kv, kv, kv, kv, bias_tbl)


S_ROWS = N_KV_HEADS * GROUP * DEC_SEQ


SWA_SEQS_PER_STEP = 2


def _swa_sample_body(q_ref, kn_ref, vn_ref, ck_ref, cv_ref, bias_ref, sink_ref,
                     knn_ref, vnn_ref, ckn_ref, cvn_ref, o_ref, ko_ref, vo_ref):
    row_kv = lax.broadcasted_iota(jnp.int32, (S_ROWS, KV_COLS), 0) // (GROUP * DEC_SEQ)
    col_kv = lax.broadcasted_iota(jnp.int32, (S_ROWS, KV_COLS), 1) // HEAD_DIM
    own = row_kv == col_kv
    pad = jnp.zeros((WINDOW - SUBLANES, KV_COLS), F32)
    keep = WINDOW - DEC_SEQ
    for i in range(SWA_SEQS_PER_STEP):
        x = (q_ref[i] * HEAD_DIM ** -0.5).astype(BF16)
        xt = jnp.concatenate([x] * N_KV_HEADS, axis=1)
        qbd = jnp.where(own, xt, jnp.zeros_like(xt))
        kk = jnp.concatenate([ck_ref[i], kn_ref[i], pad], axis=0).astype(BF16)
        vv = jnp.concatenate([cv_ref[i], vn_ref[i], pad], axis=0).astype(BF16)
        s = lax.dot_general(qbd, kk, (((1,), (1,)), ((), ())), preferred_element_type=F32)
        s = s + bias_ref[...]
        p, inv = _softmax_with_sink(s, sink_ref[...])
        of = jnp.dot(p.astype(BF16), vv, preferred_element_type=F32)
        of = jnp.where(own, of, 0.0)
        o = of[:, 0:HEAD_DIM]
        for c in range(1, N_KV_HEADS):
            o = o + of[:, c * HEAD_DIM:(c + 1) * HEAD_DIM]
        o_ref[i] = o * inv
        ko_ref[i, 0:keep] = ckn_ref[i, DEC_SEQ:WINDOW]
        ko_ref[i, keep:WINDOW] = knn_ref[i]
        vo_ref[i, 0:keep] = cvn_ref[i, DEC_SEQ:WINDOW]
        vo_ref[i, keep:WINDOW] = vnn_ref[i]


def _swa_sample(q_rows, k_new8, v_new8, cache_k_flat, cache_v_flat, bias_s, sink_col,
                k_new, v_new, cache_k, cache_v):
    n = SWA_SEQS_PER_STEP
    seq3 = lambda s: (s, 0, 0)
    seq4 = lambda s: (s, 0, 0, 0)
    full2 = lambda s: (0, 0)
    cache_shape = (DEC_BATCH, WINDOW, N_KV_HEADS, HEAD_DIM)
    return pl.pallas_call(
        _swa_sample_body,
        grid=(DEC_BATCH // n,),
        in_specs=[
            pl.BlockSpec((n, S_ROWS, HEAD_DIM), seq3),
            pl.BlockSpec((n, SUBLANES, KV_COLS), seq3),
            pl.BlockSpec((n, SUBLANES, KV_COLS), seq3),
            pl.BlockSpec((n, WINDOW, KV_COLS), seq3),
            pl.BlockSpec((n, WINDOW, KV_COLS), seq3),
            pl.BlockSpec((S_ROWS, 2 * WINDOW), full2),
            pl.BlockSpec((S_ROWS, 1), full2),
            pl.BlockSpec((n, DEC_SEQ, N_KV_HEADS, HEAD_DIM), seq4),
            pl.BlockSpec((n, DEC_SEQ, N_KV_HEADS, HEAD_DIM), seq4),
            pl.BlockSpec((n, WINDOW, N_KV_HEADS, HEAD_DIM), seq4),
            pl.BlockSpec((n, WINDOW, N_KV_HEADS, HEAD_DIM), seq4),
        ],
        out_specs=[
            pl.BlockSpec((n, S_ROWS, HEAD_DIM), seq3),
            pl.BlockSpec((n, WINDOW, N_KV_HEADS, HEAD_DIM), seq4),
            pl.BlockSpec((n, WINDOW, N_KV_HEADS, HEAD_DIM), seq4),
        ],
        out_shape=[
            jax.ShapeDtypeStruct((DEC_BATCH, S_ROWS, HEAD_DIM), F32),
            jax.ShapeDtypeStruct(cache_shape, F32),
            jax.ShapeDtypeStruct(cache_shape, F32),
        ],
        compiler_params=_cparams(("arbitrary",)),
        name="swa_sample",
    )(q_rows, k_new8, v_new8, cache_k_flat, cache_v_flat, bias_s, sink_col, k_new, v_new, cache_k, cache_v)


def _log_sigmoid(x):
    return jnp.minimum(x, 0.0) - jnp.log1p(jnp.exp(-jnp.abs(x)))


def _gla_gate_body(xp_ref, xs_ref, nw_ref, w1_ref, w2_ref, b_ref, o_ref):
    def run(x_ref):
        h = _rms(x_ref[...], nw_ref[...]).astype(BF16)
        gk = jnp.dot(h, w1_ref[...], preferred_element_type=F32)
        z = jnp.dot(gk.astype(BF16), w2_ref[...], preferred_element_type=F32) + b_ref[...]
        o_ref[...] = _log_sigmoid(z) / GATE_NORMALIZER
    _on_row_source(run, (xp_ref, xs_ref))


def _gla_gate(xs, nw, w1p, w2p, b):
    m = N_TOK
    return pl.pallas_call(
        _gla_gate_body,
        grid=(m // TM,),
        in_specs=_row_specs(True, D_MODEL) + [
            pl.BlockSpec((1, D_MODEL), lambda i: (0, 0)),
            pl.BlockSpec((D_MODEL, LANES), lambda i: (0, 0)),
            pl.BlockSpec((LANES, GLA_KEY_DIM), lambda i: (0, 0)),
            pl.BlockSpec((1, GLA_KEY_DIM), lambda i: (0, 0)),
        ],
        out_specs=pl.BlockSpec((TM, GLA_KEY_DIM), lambda i: (i, 0)),
        out_shape=jax.ShapeDtypeStruct((m, GLA_KEY_DIM), F32),
        compiler_params=_cparams(("parallel",)),
        name="gla_gate",
    )(*xs, nw.reshape(1, D_MODEL), w1p, w2p, b.reshape(1, GLA_KEY_DIM))


def _gla_out(o, gate, norm_w):
    return _rms(o, norm_w) * _silu(gate)


def _split3(x):
    hi = x.astype(BF16)
    r = x - hi.astype(F32)
    mid = r.astype(BF16)
    lo = (r - mid.astype(F32)).astype(BF16)
    return hi, mid, lo


def _cumsum_rows(g):
    c = g.shape[0]
    tri = (lax.broadcasted_iota(jnp.int32, (c, c), 0) >= lax.broadcasted_iota(jnp.int32, (c, c), 1)).astype(BF16)
    return jnp.dot(jnp.concatenate([tri] * 3, axis=1), jnp.concatenate(_split3(g), axis=0),
                   preferred_element_type=F32)


def _causal(a):
    c = a.shape[0]
    keep = lax.broadcasted_iota(jnp.int32, (c, c), 0) >= lax.broadcasted_iota(jnp.int32, (c, c), 1)
    return jnp.where(keep, a, 0.0)


_NT = (((1,), (1,)), ((), ()))
_TN = (((0,), (0,)), ((), ()))


def _gla_prompt_body(*refs):
    proj_refs = refs[:BATCH]
    la_refs = refs[BATCH:2 * BATCH]
    nw_ref, o_ref, s_ref, st_ref = refs[2 * BATCH:]
    c = pl.program_id(0)

    @pl.when(c == 0)
    def _():
        st_ref[...] = jnp.zeros_like(st_ref)

    for bi in range(BATCH):
        p_ref = proj_refs[bi]
        b_all = _cumsum_rows(la_refs[bi][...])
        for h in range(GLA_HEADS):
            kc = slice(h * GLA_DK, (h + 1) * GLA_DK)
            v0 = 2 * GLA_KEY_DIM + h * GLA_DV
            b = b_all[:, kc]
            q = p_ref[:, kc].astype(F32)
            k = p_ref[:, GLA_KEY_DIM + h * GLA_DK:GLA_KEY_DIM + (h + 1) * GLA_DK].astype(F32)
            v = p_ref[:, v0:v0 + GLA_DV]
            gate = p_ref[:, v0 + GLA_VAL_DIM:v0 + GLA_VAL_DIM + GLA_DV].astype(F32)
            qe = (q * GLA_DK ** -0.5 * jnp.exp(b)).astype(BF16)
            ke = (k * jnp.exp(-b)).astype(BF16)
            a = _causal(lax.dot_general(qe, ke, _NT, preferred_element_type=F32))
            st = st_ref[bi, h]
            o = (jnp.dot(a.astype(BF16), v, preferred_element_type=F32)
                 + lax.dot_general(qe, st.astype(BF16), _NT, preferred_element_type=F32))
            b_last = b[GLA_C - 1:GLA_C, :]
            kd = (k * jnp.exp(b_last - b)).astype(BF16)
            st_new = st * jnp.exp(b_last) + lax.dot_general(v, kd, _TN, preferred_element_type=F32)
            st_ref[bi, h] = st_new
            o_ref[bi, :, h * GLA_DV:(h + 1) * GLA_DV] = _gla_out(o, gate, nw_ref[...]).astype(o_ref.dtype)

    @pl.when(c == pl.num_programs(0) - 1)
    def _():
        for bi in range(BATCH):
            for h in range(GLA_HEADS):
                s_ref[bi, h] = st_ref[bi, h].T


def _gla_prompt(proj, log_a, norm_w):
    nc = SEQ // GLA_C
    rows = [functools.partial(lambda bi, c: (bi * nc + c, 0), bi) for bi in range(BATCH)]
    o, s = pl.pallas_call(
        _gla_prompt_body,
        grid=(nc,),
        in_specs=([pl.BlockSpec((GLA_C, GLA_MAIN_DIM), r) for r in rows]
                  + [pl.BlockSpec((GLA_C, GLA_KEY_DIM), r) for r in rows]
                  + [pl.BlockSpec((1, GLA_DV), lambda c: (0, 0))]),
        out_specs=[
            pl.BlockSpec((BATCH, GLA_C, GLA_VAL_DIM), lambda c: (0, c, 0)),
            pl.BlockSpec((BATCH, GLA_HEADS, GLA_DK, GLA_DV), lambda c: (0, 0, 0, 0)),
        ],
        out_shape=[
            jax.ShapeDtypeStruct((BATCH, SEQ, GLA_VAL_DIM), BF16),
            jax.ShapeDtypeStruct((BATCH, GLA_HEADS, GLA_DK, GLA_DV), F32),
        ],
        scratch_shapes=[pltpu.VMEM((BATCH, GLA_HEADS, GLA_DV, GLA_DK), F32)],
        compiler_params=_cparams(("arbitrary",)),
        name="gla_prompt",
    )(*([proj] * BATCH + [log_a] * BATCH + [norm_w.reshape(1, GLA_DV)]))
    return o.reshape(N_PROMPT, GLA_VAL_DIM), s


GLA_SEQS_PER_STEP = 2


def _gla_sample_body(proj_ref, la_ref, s0_ref, nw_ref, o_ref, s_ref):
    ones = jnp.ones((DEC_SEQ, LANES), BF16)
    for i in range(GLA_SEQS_PER_STEP):
        for h in range(GLA_HEADS):
            q = proj_ref[i, :, h * GLA_DK:(h + 1) * GLA_DK]
            k = proj_ref[i, :, GLA_KEY_DIM + h * GLA_DK:GLA_KEY_DIM + (h + 1) * GLA_DK]
            v0 = 2 * GLA_KEY_DIM + h * GLA_DV
            v = proj_ref[i, :, v0:v0 + GLA_DV].astype(BF16)
            gate = proj_ref[i, :, v0 + GLA_VAL_DIM:v0 + GLA_VAL_DIM + GLA_DV]
            g = la_ref[i, :, h * GLA_DK:(h + 1) * GLA_DK]
            rows = [g[0:1]]
            for t in range(1, DEC_SEQ):
                rows.append(rows[-1] + g[t:t + 1])
            b = jnp.concatenate(rows, axis=0)
            b_last = rows[-1]
            qe = (q * GLA_DK ** -0.5 * jnp.exp(b)).astype(BF16)
            ke = (k * jnp.exp(-b)).astype(BF16)
            a = _causal(lax.dot_general(qe, ke, _NT, preferred_element_type=F32))
            s0 = s0_ref[i, h]
            o = (jnp.dot(a.astype(BF16), v, preferred_element_type=F32)
                 + jnp.dot(qe, s0.astype(BF16), preferred_element_type=F32))
            kd = (k * jnp.exp(b_last - b)).astype(BF16)
            dsum = sum(lax.dot_general(piece, ones, _TN, preferred_element_type=F32) for piece in _split3(g))
            decay = jnp.concatenate([jnp.exp(dsum)] * (GLA_DV // LANES), axis=1)
            s_ref[i, h] = s0 * decay + lax.dot_general(kd, v, _TN, preferred_element_type=F32)
            o_ref[i, :, h * GLA_DV:(h + 1) * GLA_DV] = _gla_out(o, gate, nw_ref[...])


def _gla_sample(proj3, log_a3, state, norm_w):
    n = GLA_SEQS_PER_STEP
    seq3 = lambda s: (s, 0, 0)
    seq4 = lambda s: (s, 0, 0, 0)
    return pl.pallas_call(
        _gla_sample_body,
        grid=(DEC_BATCH // n,),
        in_specs=[
            pl.BlockSpec((n, DEC_SEQ, GLA_MAIN_DIM), seq3),
            pl.BlockSpec((n, DEC_SEQ, GLA_KEY_DIM), seq3),
            pl.BlockSpec((n, GLA_HEADS, GLA_DK, GLA_DV), seq4),
            pl.BlockSpec((1, GLA_DV), lambda s: (0, 0)),
        ],
        out_specs=[
            pl.BlockSpec((n, DEC_SEQ, GLA_VAL_DIM), seq3),
            pl.BlockSpec((n, GLA_HEADS, GLA_DK, GLA_DV), seq4),
        ],
        out_shape=[
            jax.ShapeDtypeStruct((DEC_BATCH, DEC_SEQ, GLA_VAL_DIM), F32),
            jax.ShapeDtypeStruct((DEC_BATCH, GLA_HEADS, GLA_DK, GLA_DV), F32),
        ],
        compiler_params=_cparams(("parallel",)),
        name="gla_sample",
    )(proj3, log_a3, state, norm_w.reshape(1, GLA_DV))


def _swa_layer(x, cache_k, cache_v, norm_w, w_qkv, b_qkv, w_o, b_o, sinks, rel_bias):
    slots = (N_PAIRS, KV_PER_STEP, GROUP, HEAD_DIM)
    w_q = w_qkv[:, :Q_DIM].reshape((D_MODEL,) + slots).transpose(0, 1, 3, 2, 4).reshape(D_MODEL, Q_DIM)
    b_q = b_qkv[:Q_DIM].reshape(slots).transpose(0, 2, 1, 3).reshape(Q_DIM)
    w_qkv_s = jnp.concatenate([w_q, w_qkv[:, Q_DIM:]], axis=1).astype(BF16)
    b_qkv_s = jnp.concatenate([b_q, b_qkv[Q_DIM:]])
    w_o_s = w_o.reshape(slots + (D_MODEL,)).transpose(0, 2, 1, 3, 4).reshape(Q_DIM, D_MODEL).astype(BF16)

    q, kv = _norm_proj(x, norm_w, w_qkv_s, b_qkv_s, ((Q_DIM, BF16), (2 * KV_COLS, F32)), "swa_qkv")
    bias_tbl = _bias_table(rel_bias)
    o_p = _swa_prompt(q, kv, bias_tbl, sinks)

    q_rows = (q[N_PROMPT:].astype(F32).reshape(DEC_BATCH, DEC_SEQ, N_PAIRS, GROUP, KV_PER_STEP, HEAD_DIM)
              .transpose(0, 2, 4, 3, 1, 5).reshape(DEC_BATCH, S_ROWS, HEAD_DIM))
    kv_s = kv[N_PROMPT:].reshape(DEC_BATCH, DEC_SEQ, 2 * KV_COLS)
    pad8 = ((0, 0), (0, SUBLANES - DEC_SEQ), (0, 0))
    k_new8 = jnp.pad(kv_s[..., :KV_COLS], pad8)
    v_new8 = jnp.pad(kv_s[..., KV_COLS:], pad8)
    bias_s = (bias_tbl[:, :, :DEC_SEQ].reshape(N_PAIRS, GROUP, KV_PER_STEP, 2 * WINDOW, DEC_SEQ)
              .transpose(0, 2, 1, 4, 3).reshape(S_ROWS, 2 * WINDOW))
    sink_col = jnp.repeat(sinks, DEC_SEQ).reshape(S_ROWS, 1)
    new_shape = (DEC_BATCH, DEC_SEQ, N_KV_HEADS, HEAD_DIM)
    o_s, k_s, v_s = _swa_sample(q_rows, k_new8, v_new8,
                                cache_k.reshape(DEC_BATCH, WINDOW, KV_COLS),
                                cache_v.reshape(DEC_BATCH, WINDOW, KV_COLS), bias_s, sink_col,
                                kv_s[..., :KV_COLS].reshape(new_shape), kv_s[..., KV_COLS:].reshape(new_shape),
                                cache_k, cache_v)
    o_s = (o_s.reshape(DEC_BATCH, N_PAIRS, KV_PER_STEP, GROUP, DEC_SEQ, HEAD_DIM)
           .transpose(0, 4, 1, 3, 2, 5).reshape(N_SAMPLE, Q_DIM))
    x = _proj_res((o_p, o_s), w_o_s, b_o, x, "swa_out")

    kv_p = kv[:N_PROMPT].reshape(BATCH, SEQ, 2 * KV_COLS)[:, SEQ - WINDOW:]
    k_p = kv_p[..., :KV_COLS].reshape(BATCH, WINDOW, N_KV_HEADS, HEAD_DIM)
    v_p = kv_p[..., KV_COLS:].reshape(BATCH, WINDOW, N_KV_HEADS, HEAD_DIM)
    return x, k_p, v_p, k_s, v_s


def _gla_layer(x, state, norm_w, w_in, w_gk2, b_gk, gnorm, w_o):
    w_main = w_in[:, :GLA_MAIN_DIM].astype(BF16)
    w1p = jnp.pad(w_in[:, GLA_MAIN_DIM:], ((0, 0), (0, LANES - GATE_RANK))).astype(BF16)
    w2p = jnp.pad(w_gk2, ((0, LANES - GATE_RANK), (0, 0))).astype(BF16)
    (proj,) = _norm_proj(x, norm_w, w_main, jnp.zeros((GLA_MAIN_DIM,), F32), ((GLA_MAIN_DIM, BF16),), "gla_in")
    log_a = _gla_gate(x, norm_w, w1p, w2p, b_gk)
    o_p, s_p = _gla_prompt(proj, log_a, gnorm)
    proj_s = proj[N_PROMPT:].astype(F32).reshape(DEC_BATCH, DEC_SEQ, GLA_MAIN_DIM)
    log_a_s = log_a[N_PROMPT:].reshape(DEC_BATCH, DEC_SEQ, GLA_KEY_DIM)
    o_s, s_s = _gla_sample(proj_s, log_a_s, state, gnorm)
    x = _proj_res((o_p, o_s.reshape(N_SAMPLE, GLA_VAL_DIM)), w_o.astype(BF16), jnp.zeros((D_MODEL,), F32), x,
                  "gla_out")
    return x, s_p, s_s


def kernel(x_prompt, x_sample, cache_swa_k, cache_swa_v, state_gla, norm_ffn1, ffn1_w_gate, ffn1_w_up,
           ffn1_w_down, norm_mix, norm_ffn2, ffn2_w_gate, ffn2_w_up, ffn2_w_down, norm_final, rel_bias,
           swa_w_qkv, swa_b_qkv, swa_w_o, swa_b_o, swa_sinks, gla_w_in, gla_w_gk2, gla_b_gk, gla_norm,
           gla_w_o):
    x = (x_prompt.reshape(N_PROMPT, D_MODEL), x_sample.reshape(N_SAMPLE, D_MODEL))
    swa_kp, swa_vp, swa_ks, swa_vs, gla_sp, gla_ss = [], [], [], [], [], []
    for i in range(DEPTH):
        x = _ffn_pair(x, i, norm_ffn1[i], ffn1_w_gate, ffn1_w_up, ffn1_w_down)
        j = i // 2
        if i % 2 == 0:
            x, kp, vp, ks, vs = _swa_layer(x, cache_swa_k[j], cache_swa_v[j], norm_mix[i], swa_w_qkv[j],
                                           swa_b_qkv[j], swa_w_o[j], swa_b_o[j], swa_sinks[j], rel_bias)
            swa_kp.append(kp)
            swa_vp.append(vp)
            swa_ks.append(ks)
            swa_vs.append(vs)
        else:
            x, sp, ss = _gla_layer(x, state_gla[j], norm_mix[i], gla_w_in[j], gla_w_gk2[j], gla_b_gk[j],
                                   gla_norm[j], gla_w_o[j])
            gla_sp.append(sp)
            gla_ss.append(ss)
        final_w = norm_final if i == DEPTH - 1 else None
        x = _ffn_pair(x, i, norm_ffn2[i], ffn2_w_gate, ffn2_w_up, ffn2_w_down, final_w)
    y_prompt = x[0].reshape(BATCH, SEQ, D_MODEL)
    y_sample = x[1].reshape(DEC_BATCH, DEC_SEQ, D_MODEL)
    return (y_prompt, y_sample, jnp.stack(swa_kp), jnp.stack(swa_vp), jnp.stack(swa_ks), jnp.stack(swa_vs),
            jnp.stack(gla_sp), jnp.stack(gla_ss))
```

```python
import functools
import math

import numpy as np
import jax
import jax.numpy as jnp
from jax import lax
from jax.experimental import pallas as pl
from jax.experimental.pallas import tpu as pltpu

F32 = jnp.float32
BF16 = jnp.bfloat16

D_MODEL = 2048
BATCH = 2
SEQ = 4096
DEPTH = 2
DEC_BATCH = 128
DEC_SEQ = 4
RMS_EPS = 1e-6
D_FF = 5632
N_HEADS = 32
N_KV_HEADS = 8
HEAD_DIM = 64
GROUP = N_HEADS // N_KV_HEADS
WINDOW = 128
NUM_BUCKETS = 32
MAX_DISTANCE = 128
NEG_INF = -1e30
GLA_HEADS = 4
GLA_DK = 256
GLA_DV = 512
GLA_KEY_DIM = GLA_HEADS * GLA_DK
GLA_VAL_DIM = GLA_HEADS * GLA_DV
GATE_RANK = 16
GATE_NORMALIZER = 16.0
GLA_MAIN_DIM = 2 * GLA_KEY_DIM + 2 * GLA_VAL_DIM
Q_DIM = N_HEADS * HEAD_DIM
KV_COLS = N_KV_HEADS * HEAD_DIM

N_PROMPT = BATCH * SEQ
N_SAMPLE = DEC_BATCH * DEC_SEQ
N_TOK = N_PROMPT + N_SAMPLE

LANES = 128
SUBLANES = 8
VMEM_LIMIT = 56 * 1024 * 1024

TM = 512
TF = 512
TN = 512
GLA_C = 64
NB = SEQ // WINDOW
N_PROMPT_TILES = N_PROMPT // TM


def _rms(x, w):
    return x * lax.rsqrt(jnp.mean(x * x, axis=-1, keepdims=True) + RMS_EPS) * w


def _silu(x):
    return x * jax.nn.sigmoid(x)


def _cparams(sem):
    return pltpu.CompilerParams(dimension_semantics=sem, vmem_limit_bytes=VMEM_LIMIT)


def _row_specs(split, width):
    if not split:
        return [pl.BlockSpec((TM, width), lambda i, *_: (i, 0))]
    return [pl.BlockSpec((TM, width), lambda i, *_: (jnp.minimum(i, N_PROMPT_TILES - 1), 0)),
            pl.BlockSpec((TM, width), lambda i, *_: (jnp.maximum(i - N_PROMPT_TILES, 0), 0))]


def _on_row_source(fn, *ref_groups):
    if all(len(g) == 1 for g in ref_groups):
        fn(*[g[0] for g in ref_groups])
        return
    i = pl.program_id(0)
    pl.when(i < N_PROMPT_TILES)(lambda: fn(*[g[0] for g in ref_groups]))
    pl.when(i >= N_PROMPT_TILES)(lambda: fn(*[g[-1] for g in ref_groups]))


FFN_TM = 1024
FFN_TF = 256
FFN_TF_SAMPLE = 512
FFN_VMEM_LIMIT = 60 * 1024 * 1024


def _ffn_body(final_norm, emit_w, x_ref, nw_ref, wg_ref, wu_ref, wd_ref, *rest):
    rest = list(rest)
    fw_ref = rest.pop(0) if final_norm else None
    o_ref = rest.pop(0)
    h_ref = rest.pop()
    j = pl.program_id(1)

    @pl.when(j == 0)
    def _():
        x = x_ref[...]
        h_ref[...] = _rms(x, nw_ref[...]).astype(BF16)
        o_ref[...] = x

    if emit_w:
        wgo_ref, wuo_ref, wdo_ref = rest
        wgo_ref[...] = wg_ref[0].astype(BF16)
        wuo_ref[...] = wu_ref[0].astype(BF16)
        wdo_ref[...] = wd_ref[0].astype(BF16)
        wg, wu, wd = wgo_ref[...], wuo_ref[...], wdo_ref[...]
    else:
        wg, wu, wd = wg_ref[...], wu_ref[...], wd_ref[...]
    h = h_ref[...]
    g = jnp.dot(h, wg, preferred_element_type=F32)
    u = jnp.dot(h, wu, preferred_element_type=F32)
    a = (_silu(g) * (0.5 * u)).astype(BF16)
    o_ref[...] += jnp.dot(a, wd, preferred_element_type=F32)

    if final_norm:
        @pl.when(j == pl.num_programs(1) - 1)
        def _():
            o_ref[...] = _rms(o_ref[...], fw_ref[...])


def _ffn(x, layer, nw, wg, wu, wd, final_w=None):
    m = x.shape[0]
    emit_w = wg.dtype == F32
    tm = min(FFN_TM, m)
    tf = FFN_TF if emit_w else FFN_TF_SAMPLE
    nj = D_FF // tf
    final_norm = final_w is not None
    vec = pl.BlockSpec((1, D_MODEL), lambda i, j: (0, 0))
    if emit_w:
        w_specs = [
            pl.BlockSpec((1, D_MODEL, tf), lambda i, j: (layer, 0, j)),
            pl.BlockSpec((1, D_MODEL, tf), lambda i, j: (layer, 0, j)),
            pl.BlockSpec((1, tf, D_MODEL), lambda i, j: (layer, j, 0)),
        ]
    else:
        w_specs = [
            pl.BlockSpec((D_MODEL, tf), lambda i, j: (0, j)),
            pl.BlockSpec((D_MODEL, tf), lambda i, j: (0, j)),
            pl.BlockSpec((tf, D_MODEL), lambda i, j: (j, 0)),
        ]
    in_specs = [pl.BlockSpec((tm, D_MODEL), lambda i, j: (i, 0)), vec] + w_specs
    args = [x, nw.reshape(1, D_MODEL), wg, wu, wd]
    if final_norm:
        in_specs.append(vec)
        args.append(final_w.reshape(1, D_MODEL))
    out_specs = [pl.BlockSpec((tm, D_MODEL), lambda i, j: (i, 0))]
    out_shape = [jax.ShapeDtypeStruct((m, D_MODEL), F32)]
    if emit_w:
        once = lambda i, j: jnp.where(i == 0, j, nj - 1)
        out_specs += [
            pl.BlockSpec((D_MODEL, tf), lambda i, j: (0, once(i, j))),
            pl.BlockSpec((D_MODEL, tf), lambda i, j: (0, once(i, j))),
            pl.BlockSpec((tf, D_MODEL), lambda i, j: (once(i, j), 0)),
        ]
        out_shape += [
            jax.ShapeDtypeStruct((D_MODEL, D_FF), BF16),
            jax.ShapeDtypeStruct((D_MODEL, D_FF), BF16),
            jax.ShapeDtypeStruct((D_FF, D_MODEL), BF16),
        ]
    outs = pl.pallas_call(
        functools.partial(_ffn_body, final_norm, emit_w),
        grid=(m // tm, nj),
        in_specs=in_specs,
        out_specs=out_specs,
        out_shape=out_shape,
        scratch_shapes=[pltpu.VMEM((tm, D_MODEL), BF16)],
        compiler_params=pltpu.CompilerParams(dimension_semantics=("arbitrary", "arbitrary"),
                                             vmem_limit_bytes=FFN_VMEM_LIMIT),
        name="ffn",
    )(*args)
    return outs[0], tuple(outs[1:])


def _ffn_pair(x_pair, layer, nw, wg, wu, wd, final_w=None):
    y_prompt, w_bf16 = _ffn(x_pair[0], layer, nw, wg, wu, wd, final_w)
    y_sample, _ = _ffn(x_pair[1], layer, nw, *w_bf16, final_w)
    return y_prompt, y_sample


def _log_sigmoid(x):
    return jnp.minimum(x, 0.0) - jnp.log1p(jnp.exp(-jnp.abs(x)))


def _norm_proj_body(segments, with_gate, xp_ref, xs_ref, nw_ref, w_ref, b_ref, *rest):
    if with_gate:
        w1_ref, w2_ref, bg_ref = rest[:3]
        rest = rest[3:]
    o_refs = rest[:len(segments)]

    def run(x_ref):
        h = _rms(x_ref[...], nw_ref[...]).astype(BF16)
        col = 0
        for (width, _), o_ref in zip(segments, o_refs):
            for c in range(0, width, TN):
                acc = jnp.dot(h, w_ref[:, col + c:col + c + TN], preferred_element_type=F32)
                o_ref[:, c:c + TN] = (acc + b_ref[:, col + c:col + c + TN]).astype(o_ref.dtype)
            col += width
        if with_gate:
            gk = jnp.dot(h, w1_ref[...], preferred_element_type=F32)
            z = jnp.dot(gk.astype(BF16), w2_ref[...], preferred_element_type=F32) + bg_ref[...]
            rest[-1][...] = _log_sigmoid(z) / GATE_NORMALIZER
    _on_row_source(run, (xp_ref, xs_ref))


def _norm_proj(xs, nw, w, b, segments, name, gate=None):
    n = w.shape[1]
    in_specs = _row_specs(True, D_MODEL) + [
        pl.BlockSpec((1, D_MODEL), lambda i: (0, 0)),
        pl.BlockSpec((D_MODEL, n), lambda i: (0, 0), pipeline_mode=pl.Buffered(1)),
        pl.BlockSpec((1, n), lambda i: (0, 0)),
    ]
    args = [*xs, nw.reshape(1, D_MODEL), w, b.reshape(1, n)]
    out_specs = [pl.BlockSpec((TM, width), lambda i: (i, 0)) for width, _ in segments]
    out_shape = [jax.ShapeDtypeStruct((N_TOK, width), dtype) for width, dtype in segments]
    if gate is not None:
        w1p, w2p, bg = gate
        in_specs += [
            pl.BlockSpec((D_MODEL, LANES), lambda i: (0, 0)),
            pl.BlockSpec((LANES, GLA_KEY_DIM), lambda i: (0, 0)),
            pl.BlockSpec((1, GLA_KEY_DIM), lambda i: (0, 0)),
        ]
        args += [w1p, w2p, bg.reshape(1, GLA_KEY_DIM)]
        out_specs.append(pl.BlockSpec((TM, GLA_KEY_DIM), lambda i: (i, 0)))
        out_shape.append(jax.ShapeDtypeStruct((N_TOK, GLA_KEY_DIM), F32))
    return pl.pallas_call(
        functools.partial(_norm_proj_body, segments, gate is not None),
        grid=(N_TOK // TM,),
        in_specs=in_specs,
        out_specs=out_specs,
        out_shape=out_shape,
        compiler_params=pltpu.CompilerParams(dimension_semantics=("arbitrary",),
                                             vmem_limit_bytes=FFN_VMEM_LIMIT),
        name=name,
    )(*args)


def _proj_res_body(ap_ref, as_ref, w_ref, b_ref, rp_ref, rs_ref, op_ref, os_ref):
    def run(a_ref, r_ref, o_ref):
        a = a_ref[...].astype(BF16)
        for c in range(0, D_MODEL, TN):
            acc = jnp.dot(a, w_ref[:, c:c + TN], preferred_element_type=F32)
            o_ref[:, c:c + TN] = r_ref[:, c:c + TN] + acc + b_ref[:, c:c + TN]
    _on_row_source(run, (ap_ref, as_ref), (rp_ref, rs_ref), (op_ref, os_ref))


def _proj_res(a_pair, w, b, res_pair, name):
    k = w.shape[0]
    return pl.pallas_call(
        _proj_res_body,
        grid=(N_TOK // TM,),
        in_specs=_row_specs(True, k) + [
            pl.BlockSpec((k, D_MODEL), lambda i: (0, 0), pipeline_mode=pl.Buffered(1)),
            pl.BlockSpec((1, D_MODEL), lambda i: (0, 0)),
        ] + _row_specs(True, D_MODEL),
        out_specs=_row_specs(True, D_MODEL),
        out_shape=[jax.ShapeDtypeStruct((N_PROMPT, D_MODEL), F32), jax.ShapeDtypeStruct((N_SAMPLE, D_MODEL), F32)],
        compiler_params=_cparams(("arbitrary",)),
        name=name,
    )(*a_pair, w, b.reshape(1, D_MODEL), *res_pair)


def _t5_bucket_table():
    i = np.arange(WINDOW)[None, :]
    j = np.arange(2 * WINDOW)[:, None]
    n = np.maximum(WINDOW + i - j, 0)
    max_exact = NUM_BUCKETS // 2
    nf = np.maximum(n, 1).astype(np.float32)
    large = max_exact + (np.log(nf / np.float32(max_exact)) / np.float32(math.log(MAX_DISTANCE / max_exact))
                         * np.float32(NUM_BUCKETS - max_exact)).astype(np.int32)
    large = np.minimum(large, NUM_BUCKETS - 1)
    return np.where(n < max_exact, n, large).astype(np.int32)


KV_PER_STEP = LANES // HEAD_DIM
N_PAIRS = N_KV_HEADS // KV_PER_STEP
Q_COLS_PER_STEP = KV_PER_STEP * GROUP * HEAD_DIM


def _slot_head(slot):
    pair = slot // (GROUP * KV_PER_STEP)
    g = (slot // KV_PER_STEP) % GROUP
    hh = slot % KV_PER_STEP
    return (pair * KV_PER_STEP + hh) * GROUP + g


def _bias_table_body(bucket_ref, rb_ref, o_ref):
    h = _slot_head(pl.program_id(0))
    bucket = bucket_ref[...]
    acc = jnp.zeros((2 * WINDOW, WINDOW), F32)
    for b in range(NUM_BUCKETS):
        acc = jnp.where(bucket == b, rb_ref[b, h], acc)
    j = lax.broadcasted_iota(jnp.int32, (2 * WINDOW, WINDOW), 0)
    i = lax.broadcasted_iota(jnp.int32, (2 * WINDOW, WINDOW), 1)
    dist = WINDOW + i - j
    o_ref[0] = jnp.where((dist >= 0) & (dist < WINDOW), acc, NEG_INF)


def _bias_table(rel_bias):
    return pl.pallas_call(
        _bias_table_body,
        grid=(N_HEADS,),
        in_specs=[
            pl.BlockSpec((2 * WINDOW, WINDOW), lambda h: (0, 0)),
            pl.BlockSpec(memory_space=pltpu.SMEM),
        ],
        out_specs=pl.BlockSpec((1, 2 * WINDOW, WINDOW), lambda h: (h, 0, 0)),
        out_shape=jax.ShapeDtypeStruct((N_HEADS, 2 * WINDOW, WINDOW), F32),
        name="bias_table",
    )(jnp.asarray(_t5_bucket_table()), rel_bias)


def _softmax_with_sink(s, sink_col):
    m = jnp.maximum(jnp.max(s, axis=-1, keepdims=True), sink_col)
    p = jnp.exp(s - m)
    denom = jnp.sum(p, axis=-1, keepdims=True) + jnp.exp(sink_col - m)
    return p, 1.0 / denom


PAIR_SLOTS = GROUP * KV_PER_STEP
PAIR_COLS = PAIR_SLOTS * WINDOW
ONES_ROWS = 16


PAIRS_PER_STEP = 2


def _swa_prompt_body(sink_ref, q_ref, kp_ref, ko_ref, vp_ref, vo_ref, bias_ref, o_ref):
    blk = pl.program_id(2)
    head_a = lax.broadcasted_iota(jnp.int32, (WINDOW, LANES), 1) < HEAD_DIM
    no_prev = jnp.where(blk > 0, 0.0, NEG_INF)
    for t in range(PAIRS_PER_STEP):
        pair = pl.program_id(0) * PAIRS_PER_STEP + t
        lanes = slice(t * LANES, (t + 1) * LANES)
        q0 = t * Q_COLS_PER_STEP
        k = jnp.concatenate([kp_ref[:, lanes], ko_ref[:, lanes]], axis=0).astype(BF16)
        v = jnp.concatenate([vp_ref[:, lanes], vo_ref[:, lanes]], axis=0)
        vt = jnp.concatenate([v.T, jnp.ones((ONES_ROWS, 2 * WINDOW), F32)], axis=0).astype(BF16)
        parts = []
        for g in range(GROUP):
            qg = q_ref[:, q0 + g * LANES:q0 + (g + 1) * LANES] * HEAD_DIM ** -0.5
            zero = jnp.zeros_like(qg)
            parts += [jnp.where(head_a, qg, zero), jnp.where(head_a, zero, qg)]
        qbd = jnp.concatenate(parts, axis=0)
        st = lax.dot_general(k, qbd, (((1,), (1,)), ((), ())), preferred_element_type=F32)
        st = st + jnp.concatenate([bias_ref[t * PAIR_SLOTS + u] for u in range(PAIR_SLOTS)], axis=1)
        st = jnp.concatenate([st[:WINDOW] + no_prev, st[WINDOW:]], axis=0)
        sink_row = jnp.concatenate(
            [jnp.full((1, WINDOW), sink_ref[_slot_head(pair * PAIR_SLOTS + slot)], F32)
             for slot in range(PAIR_SLOTS)], axis=1)
        m = jnp.maximum(jnp.max(st, axis=0, keepdims=True), sink_row)
        pt = jnp.exp(st - m).astype(BF16)
        oa = jnp.dot(vt, pt, preferred_element_type=F32)
        inv = 1.0 / (oa[LANES:LANES + 1] + jnp.exp(sink_row - m))
        o = oa[:LANES] * inv
        for g in range(GROUP):
            c = g * KV_PER_STEP * WINDOW
            ot = jnp.concatenate([o[:HEAD_DIM, c:c + WINDOW], o[HEAD_DIM:, c + WINDOW:c + 2 * WINDOW]], axis=0)
            o_ref[:, q0 + g * LANES:q0 + (g + 1) * LANES] = ot.T.astype(o_ref.dtype)


def _swa_prompt(q, kv, bias_tbl, sinks):
    n = PAIRS_PER_STEP
    v_col0 = KV_COLS // (n * LANES)

    def prev(p, b, i):
        return b * NB + jnp.maximum(i - 1, 0)

    return pl.pallas_call(
        _swa_prompt_body,
        grid=(N_PAIRS // n, BATCH, NB),
        in_specs=[
            pl.BlockSpec(memory_space=pltpu.SMEM),
            pl.BlockSpec((WINDOW, n * Q_COLS_PER_STEP), lambda p, b, i: (b * NB + i, p)),
            pl.BlockSpec((WINDOW, n * LANES), lambda p, b, i: (prev(p, b, i), p)),
            pl.BlockSpec((WINDOW, n * LANES), lambda p, b, i: (b * NB + i, p)),
            pl.BlockSpec((WINDOW, n * LANES), lambda p, b, i: (prev(p, b, i), v_col0 + p)),
            pl.BlockSpec((WINDOW, n * LANES), lambda p, b, i: (b * NB + i, v_col0 + p)),
            pl.BlockSpec((n * PAIR_SLOTS, 2 * WINDOW, WINDOW), lambda p, b, i: (p, 0, 0)),
        ],
        out_specs=pl.BlockSpec((WINDOW, n * Q_COLS_PER_STEP), lambda p, b, i: (b * NB + i, p)),
        out_shape=jax.ShapeDtypeStruct((N_PROMPT, Q_DIM), BF16),
        compiler_params=_cparams(("arbitrary", "arbitrary", "arbitrary")),
        name="swa_prompt",
    )(sinks, q, kv, kv, kv, kv, bias_tbl)


S_ROWS = N_KV_HEADS * GROUP * DEC_SEQ


def _swa_sample_body(q_ref, kn_ref, vn_ref, ck_ref, cv_ref, bias_ref, sink_ref,
                     o_ref, ko_ref, vo_ref):
    x = (q_ref[0] * HEAD_DIM ** -0.5).astype(BF16)
    xt = jnp.concatenate([x] * N_KV_HEADS, axis=1)
    row_kv = lax.broadcasted_iota(jnp.int32, (S_ROWS, KV_COLS), 0) // (GROUP * DEC_SEQ)
    col_kv = lax.broadcasted_iota(jnp.int32, (S_ROWS, KV_COLS), 1) // HEAD_DIM
    own = row_kv == col_kv
    qbd = jnp.where(own, xt, jnp.zeros_like(xt))
    pad = jnp.zeros((WINDOW - SUBLANES, KV_COLS), F32)
    kk = jnp.concatenate([ck_ref[0], kn_ref[0], pad], axis=0).astype(BF16)
    vv = jnp.concatenate([cv_ref[0], vn_ref[0], pad], axis=0).astype(BF16)
    s = lax.dot_general(qbd, kk, (((1,), (1,)), ((), ())), preferred_element_type=F32)
    s = s + bias_ref[...]
    p, inv = _softmax_with_sink(s, sink_ref[...])
    of = jnp.dot(p.astype(BF16), vv, preferred_element_type=F32)
    of = jnp.where(own, of, 0.0)
    o = of[:, 0:HEAD_DIM]
    for c in range(1, N_KV_HEADS):
        o = o + of[:, c * HEAD_DIM:(c + 1) * HEAD_DIM]
    o_ref[0] = o * inv
    keep = WINDOW - DEC_SEQ
    ko_ref[0, 0:keep, :] = ck_ref[0, DEC_SEQ:WINDOW, :]
    ko_ref[0, keep:WINDOW, :] = kn_ref[0, 0:DEC_SEQ, :]
    vo_ref[0, 0:keep, :] = cv_ref[0, DEC_SEQ:WINDOW, :]
    vo_ref[0, keep:WINDOW, :] = vn_ref[0, 0:DEC_SEQ, :]


def _swa_sample(q_rows, k_new8, v_new8, cache_k, cache_v, bias_s, sink_col):
    seq3 = lambda s: (s, 0, 0)
    full2 = lambda s: (0, 0)
    return pl.pallas_call(
        _swa_sample_body,
        grid=(DEC_BATCH,),
        in_specs=[
            pl.BlockSpec((1, S_ROWS, HEAD_DIM), seq3),
            pl.BlockSpec((1, SUBLANES, KV_COLS), seq3),
            pl.BlockSpec((1, SUBLANES, KV_COLS), seq3),
            pl.BlockSpec((1, WINDOW, KV_COLS), seq3),
            pl.BlockSpec((1, WINDOW, KV_COLS), seq3),
            pl.BlockSpec((S_ROWS, 2 * WINDOW), full2),
            pl.BlockSpec((S_ROWS, 1), full2),
        ],
        out_specs=[
            pl.BlockSpec((1, S_ROWS, HEAD_DIM), seq3),
            pl.BlockSpec((1, WINDOW, KV_COLS), seq3),
            pl.BlockSpec((1, WINDOW, KV_COLS), seq3),
        ],
        out_shape=[
            jax.ShapeDtypeStruct((DEC_BATCH, S_ROWS, HEAD_DIM), F32),
            jax.ShapeDtypeStruct((DEC_BATCH, WINDOW, KV_COLS), F32),
            jax.ShapeDtypeStruct((DEC_BATCH, WINDOW, KV_COLS), F32),
        ],
        compiler_params=_cparams(("parallel",)),
        name="swa_sample",
    )(q_rows, k_new8, v_new8, cache_k, cache_v, bias_s, sink_col)


def _gla_out(o, gate, norm_w):
    return _rms(o, norm_w) * _silu(gate)


def _split3(x):
    hi = x.astype(BF16)
    r = x - hi.astype(F32)
    mid = r.astype(BF16)
    lo = (r - mid.astype(F32)).astype(BF16)
    return hi, mid, lo


def _cumsum_rows(g):
    c = g.shape[0]
    tri = (lax.broadcasted_iota(jnp.int32, (c, c), 0) >= lax.broadcasted_iota(jnp.int32, (c, c), 1)).astype(BF16)
    return jnp.dot(jnp.concatenate([tri] * 3, axis=1), jnp.concatenate(_split3(g), axis=0),
                   preferred_element_type=F32)


def _causal(a):
    c = a.shape[0]
    keep = lax.broadcasted_iota(jnp.int32, (c, c), 0) >= lax.broadcasted_iota(jnp.int32, (c, c), 1)
    return jnp.where(keep, a, 0.0)


_NT = (((1,), (1,)), ((), ()))
_TN = (((0,), (0,)), ((), ()))


def _gla_prompt_body(*refs):
    proj_refs = refs[:BATCH]
    la_refs = refs[BATCH:2 * BATCH]
    nw_ref, o_ref, s_ref, st_ref = refs[2 * BATCH:]
    c = pl.program_id(0)

    @pl.when(c == 0)
    def _():
        st_ref[...] = jnp.zeros_like(st_ref)

    for bi in range(BATCH):
        p_ref = proj_refs[bi]
        b_all = _cumsum_rows(la_refs[bi][...])
        for h in range(GLA_HEADS):
            kc = slice(h * GLA_DK, (h + 1) * GLA_DK)
            v0 = 2 * GLA_KEY_DIM + h * GLA_DV
            b = b_all[:, kc]
            q = p_ref[:, kc].astype(F32)
            k = p_ref[:, GLA_KEY_DIM + h * GLA_DK:GLA_KEY_DIM + (h + 1) * GLA_DK].astype(F32)
            v = p_ref[:, v0:v0 + GLA_DV]
            gate = p_ref[:, v0 + GLA_VAL_DIM:v0 + GLA_VAL_DIM + GLA_DV].astype(F32)
            qe = (q * GLA_DK ** -0.5 * jnp.exp(b)).astype(BF16)
            ke = (k * jnp.exp(-b)).astype(BF16)
            a = _causal(lax.dot_general(qe, ke, _NT, preferred_element_type=F32))
            st = st_ref[bi, h]
            o = (jnp.dot(a.astype(BF16), v, preferred_element_type=F32)
                 + lax.dot_general(qe, st.astype(BF16), _NT, preferred_element_type=F32))
            b_last = b[GLA_C - 1:GLA_C, :]
            kd = (k * jnp.exp(b_last - b)).astype(BF16)
            st_new = st * jnp.exp(b_last) + lax.dot_general(v, kd, _TN, preferred_element_type=F32)
            st_ref[bi, h] = st_new
            o_ref[bi, :, h * GLA_DV:(h + 1) * GLA_DV] = _gla_out(o, gate, nw_ref[...]).astype(o_ref.dtype)

    @pl.when(c == pl.num_programs(0) - 1)
    def _():
        for bi in range(BATCH):
            for h in range(GLA_HEADS):
                s_ref[bi, h] = st_ref[bi, h].T


def _gla_prompt(proj, log_a, norm_w):
    nc = SEQ // GLA_C
    rows = [functools.partial(lambda bi, c: (bi * nc + c, 0), bi) for bi in range(BATCH)]
    o, s = pl.pallas_call(
        _gla_prompt_body,
        grid=(nc,),
        in_specs=([pl.BlockSpec((GLA_C, GLA_MAIN_DIM), r) for r in rows]
                  + [pl.BlockSpec((GLA_C, GLA_KEY_DIM), r) for r in rows]
                  + [pl.BlockSpec((1, GLA_DV), lambda c: (0, 0))]),
        out_specs=[
            pl.BlockSpec((BATCH, GLA_C, GLA_VAL_DIM), lambda c: (0, c, 0)),
            pl.BlockSpec((BATCH, GLA_HEADS, GLA_DK, GLA_DV), lambda c: (0, 0, 0, 0)),
        ],
        out_shape=[
            jax.ShapeDtypeStruct((BATCH, SEQ, GLA_VAL_DIM), BF16),
            jax.ShapeDtypeStruct((BATCH, GLA_HEADS, GLA_DK, GLA_DV), F32),
        ],
        scratch_shapes=[pltpu.VMEM((BATCH, GLA_HEADS, GLA_DV, GLA_DK), F32)],
        compiler_params=_cparams(("arbitrary",)),
        name="gla_prompt",
    )(*([proj] * BATCH + [log_a] * BATCH + [norm_w.reshape(1, GLA_DV)]))
    return o.reshape(N_PROMPT, GLA_VAL_DIM), s


GLA_SEQS_PER_STEP = 2


def _gla_sample_body(proj_ref, la_ref, s0_ref, nw_ref, o_ref, s_ref):
    ones = jnp.ones((DEC_SEQ, LANES), BF16)
    for i in range(GLA_SEQS_PER_STEP):
        for h in range(GLA_HEADS):
            q = proj_ref[i, :, h * GLA_DK:(h + 1) * GLA_DK]
            k = proj_ref[i, :, GLA_KEY_DIM + h * GLA_DK:GLA_KEY_DIM + (h + 1) * GLA_DK]
            v0 = 2 * GLA_KEY_DIM + h * GLA_DV
            v = proj_ref[i, :, v0:v0 + GLA_DV].astype(BF16)
            gate = proj_ref[i, :, v0 + GLA_VAL_DIM:v0 + GLA_VAL_DIM + GLA_DV]
            g = la_ref[i, :, h * GLA_DK:(h + 1) * GLA_DK]
            rows = [g[0:1]]
            for t in range(1, DEC_SEQ):
                rows.append(rows[-1] + g[t:t + 1])
            b = jnp.concatenate(rows, axis=0)
            b_last = rows[-1]
            qe = (q * GLA_DK ** -0.5 * jnp.exp(b)).astype(BF16)
            ke = (k * jnp.exp(-b)).astype(BF16)
            a = _causal(lax.dot_general(qe, ke, _NT, preferred_element_type=F32))
            s0 = s0_ref[i, h]
            o = (jnp.dot(a.astype(BF16), v, preferred_element_type=F32)
                 + jnp.dot(qe, s0.astype(BF16), preferred_element_type=F32))
            kd = (k * jnp.exp(b_last - b)).astype(BF16)
            dsum = sum(lax.dot_general(piece, ones, _TN, preferred_element_type=F32) for piece in _split3(g))
            decay = jnp.concatenate([jnp.exp(dsum)] * (GLA_DV // LANES), axis=1)
            s_ref[i, h] = s0 * decay + lax.dot_general(kd, v, _TN, preferred_element_type=F32)
            o_ref[i, :, h * GLA_DV:(h + 1) * GLA_DV] = _gla_out(o, gate, nw_ref[...])


def _gla_sample(proj3, log_a3, state, norm_w):
    n = GLA_SEQS_PER_STEP
    seq3 = lambda s: (s, 0, 0)
    seq4 = lambda s: (s, 0, 0, 0)
    return pl.pallas_call(
        _gla_sample_body,
        grid=(DEC_BATCH // n,),
        in_specs=[
            pl.BlockSpec((n, DEC_SEQ, GLA_MAIN_DIM), seq3),
            pl.BlockSpec((n, DEC_SEQ, GLA_KEY_DIM), seq3),
            pl.BlockSpec((n, GLA_HEADS, GLA_DK, GLA_DV), seq4),
            pl.BlockSpec((1, GLA_DV), lambda s: (0, 0)),
        ],
        out_specs=[
            pl.BlockSpec((n, DEC_SEQ, GLA_VAL_DIM), seq3),
            pl.BlockSpec((n, GLA_HEADS, GLA_DK, GLA_DV), seq4),
        ],
        out_shape=[
            jax.ShapeDtypeStruct((DEC_BATCH, DEC_SEQ, GLA_VAL_DIM), F32),
            jax.ShapeDtypeStruct((DEC_BATCH, GLA_HEADS, GLA_DK, GLA_DV), F32),
        ],
        compiler_params=_cparams(("parallel",)),
        name="gla_sample",
    )(proj3, log_a3, state, norm_w.reshape(1, GLA_DV))


def _swa_layer(x, cache_k, cache_v, norm_w, w_qkv, b_qkv, w_o, b_o, sinks, rel_bias):
    slots = (N_PAIRS, KV_PER_STEP, GROUP, HEAD_DIM)
    w_q = w_qkv[:, :Q_DIM].reshape((D_MODEL,) + slots).transpose(0, 1, 3, 2, 4).reshape(D_MODEL, Q_DIM)
    b_q = b_qkv[:Q_DIM].reshape(slots).transpose(0, 2, 1, 3).reshape(Q_DIM)
    w_qkv_s = jnp.concatenate([w_q, w_qkv[:, Q_DIM:]], axis=1).astype(BF16)
    b_qkv_s = jnp.concatenate([b_q, b_qkv[Q_DIM:]])
    w_o_s = w_o.reshape(slots + (D_MODEL,)).transpose(0, 2, 1, 3, 4).reshape(Q_DIM, D_MODEL).astype(BF16)

    q, kv = _norm_proj(x, norm_w, w_qkv_s, b_qkv_s, ((Q_DIM, BF16), (2 * KV_COLS, F32)), "swa_qkv")
    bias_tbl = _bias_table(rel_bias)
    o_p = _swa_prompt(q, kv, bias_tbl, sinks)

    q_rows = (q[N_PROMPT:].astype(F32).reshape(DEC_BATCH, DEC_SEQ, N_PAIRS, GROUP, KV_PER_STEP, HEAD_DIM)
              .transpose(0, 2, 4, 3, 1, 5).reshape(DEC_BATCH, S_ROWS, HEAD_DIM))
    kv_s = kv[N_PROMPT:].reshape(DEC_BATCH, DEC_SEQ, 2 * KV_COLS)
    pad8 = ((0, 0), (0, SUBLANES - DEC_SEQ), (0, 0))
    k_new8 = jnp.pad(kv_s[..., :KV_COLS], pad8)
    v_new8 = jnp.pad(kv_s[..., KV_COLS:], pad8)
    bias_s = (bias_tbl[:, :, :DEC_SEQ].reshape(N_PAIRS, GROUP, KV_PER_STEP, 2 * WINDOW, DEC_SEQ)
              .transpose(0, 2, 1, 4, 3).reshape(S_ROWS, 2 * WINDOW))
    sink_col = jnp.repeat(sinks, DEC_SEQ).reshape(S_ROWS, 1)
    o_s, k_s, v_s = _swa_sample(q_rows, k_new8, v_new8,
                                cache_k.reshape(DEC_BATCH, WINDOW, KV_COLS),
                                cache_v.reshape(DEC_BATCH, WINDOW, KV_COLS), bias_s, sink_col)
    o_s = (o_s.reshape(DEC_BATCH, N_PAIRS, KV_PER_STEP, GROUP, DEC_SEQ, HEAD_DIM)
           .transpose(0, 4, 1, 3, 2, 5).reshape(N_SAMPLE, Q_DIM))
    x = _proj_res((o_p, o_s), w_o_s, b_o, x, "swa_out")

    kv_p = kv[:N_PROMPT].reshape(BATCH, SEQ, 2 * KV_COLS)[:, SEQ - WINDOW:]
    k_p = kv_p[..., :KV_COLS].reshape(BATCH, WINDOW, N_KV_HEADS, HEAD_DIM)
    v_p = kv_p[..., KV_COLS:].reshape(BATCH, WINDOW, N_KV_HEADS, HEAD_DIM)
    shape_s = (DEC_BATCH, WINDOW, N_KV_HEADS, HEAD_DIM)
    return x, k_p, v_p, k_s.reshape(shape_s), v_s.reshape(shape_s)


def _gla_layer(x, state, norm_w, w_in, w_gk2, b_gk, gnorm, w_o):
    w_main = w_in[:, :GLA_MAIN_DIM].astype(BF16)
    w1p = jnp.pad(w_in[:, GLA_MAIN_DIM:], ((0, 0), (0, LANES - GATE_RANK))).astype(BF16)
    w2p = jnp.pad(w_gk2, ((0, LANES - GATE_RANK), (0, 0))).astype(BF16)
    proj, log_a = _norm_proj(x, norm_w, w_main, jnp.zeros((GLA_MAIN_DIM,), F32), ((GLA_MAIN_DIM, BF16),),
                             "gla_in", gate=(w1p, w2p, b_gk))
    o_p, s_p = _gla_prompt(proj, log_a, gnorm)
    proj_s = proj[N_PROMPT:].astype(F32).reshape(DEC_BATCH, DEC_SEQ, GLA_MAIN_DIM)
    log_a_s = log_a[N_PROMPT:].reshape(DEC_BATCH, DEC_SEQ, GLA_KEY_DIM)
    o_s, s_s = _gla_sample(proj_s, log_a_s, state, gnorm)
    x = _proj_res((o_p, o_s.reshape(N_SAMPLE, GLA_VAL_DIM)), w_o.astype(BF16), jnp.zeros((D_MODEL,), F32), x,
                  "gla_out")
    return x, s_p, s_s


def kernel(x_prompt, x_sample, cache_swa_k, cache_swa_v, state_gla, norm_ffn1, ffn1_w_gate, ffn1_w_up,
           ffn1_w_down, norm_mix, norm_ffn2, ffn2_w_gate, ffn2_w_up, ffn2_w_down, norm_final, rel_bias,
           swa_w_qkv, swa_b_qkv, swa_w_o, swa_b_o, swa_sinks, gla_w_in, gla_w_gk2, gla_b_gk, gla_norm,
           gla_w_o):
    x = (x_prompt.reshape(N_PROMPT, D_MODEL), x_sample.reshape(N_SAMPLE, D_MODEL))
    swa_kp, swa_vp, swa_ks, swa_vs, gla_sp, gla_ss = [], [], [], [], [], []
    for i in range(DEPTH):
        x = _ffn_pair(x, i, norm_ffn1[i], ffn1_w_gate, ffn1_w_up, ffn1_w_down)
        j = i // 2
        if i % 2 == 0:
            x, kp, vp, ks, vs = _swa_layer(x, cache_swa_k[j], cache_swa_v[j], norm_mix[i], swa_w_qkv[j],
                                           swa_b_qkv[j], swa_w_o[j], swa_b_o[j], swa_sinks[j], rel_bias)
            swa_kp.append(kp)
            swa_vp.append(vp)
            swa_ks.append(ks)
            swa_vs.append(vs)
        else:
            x, sp, ss = _gla_layer(x, state_gla[j], norm_mix[i], gla_w_in[j], gla_w_gk2[j], gla_b_gk[j],
                                   gla_norm[j], gla_w_o[j])
            gla_sp.append(sp)
            gla_ss.append(ss)
        final_w = norm_final if i == DEPTH - 1 else None
        x = _ffn_pair(x, i, norm_ffn2[i], ffn2_w_gate, ffn2_w_up, ffn2_w_down, final_w)
    y_prompt = x[0].reshape(BATCH, SEQ, D_MODEL)
    y_sample = x[1].reshape(DEC_BATCH, DEC_SEQ, D_MODEL)
    return (y_prompt, y_sample, jnp.stack(swa_kp), jnp.stack(swa_vp), jnp.stack(swa_ks), jnp.stack(swa_vs),
            jnp.stack(gla_sp), jnp.stack(gla_ss))
```

```python
import functools
import math

import numpy as np
import jax
import jax.numpy as jnp
from jax import lax
from jax.experimental import pallas as pl
from jax.experimental.pallas import tpu as pltpu

F32 = jnp.float32
BF16 = jnp.bfloat16

D_MODEL = 2048
BATCH = 2
SEQ = 4096
DEPTH = 2
DEC_BATCH = 128
DEC_SEQ = 4
RMS_EPS = 1e-6
D_FF = 5632
N_HEADS = 32
N_KV_HEADS = 8
HEAD_DIM = 64
GROUP = N_HEADS // N_KV_HEADS
WINDOW = 128
NUM_BUCKETS = 32
MAX_DISTANCE = 128
NEG_INF = -1e30
GLA_HEADS = 4
GLA_DK = 256
GLA_DV = 512
GLA_KEY_DIM = GLA_HEADS * GLA_DK
GLA_VAL_DIM = GLA_HEADS * GLA_DV
GATE_RANK = 16
GATE_NORMALIZER = 16.0
GLA_MAIN_DIM = 2 * GLA_KEY_DIM + 2 * GLA_VAL_DIM
Q_DIM = N_HEADS * HEAD_DIM
KV_COLS = N_KV_HEADS * HEAD_DIM

N_PROMPT = BATCH * SEQ
N_SAMPLE = DEC_BATCH * DEC_SEQ
N_TOK = N_PROMPT + N_SAMPLE

LANES = 128
SUBLANES = 8
VMEM_LIMIT = 56 * 1024 * 1024

TM = 512
TF = 512
TN = 512
GLA_C = 64
NB = SEQ // WINDOW
N_PROMPT_TILES = N_PROMPT // TM


def _rms(x, w):
    return x * lax.rsqrt(jnp.mean(x * x, axis=-1, keepdims=True) + RMS_EPS) * w


def _silu(x):
    return x * jax.nn.sigmoid(x)


def _cparams(sem):
    return pltpu.CompilerParams(dimension_semantics=sem, vmem_limit_bytes=VMEM_LIMIT)


def _row_specs(split, width):
    if not split:
        return [pl.BlockSpec((TM, width), lambda i, *_: (i, 0))]
    return [pl.BlockSpec((TM, width), lambda i, *_: (jnp.minimum(i, N_PROMPT_TILES - 1), 0)),
            pl.BlockSpec((TM, width), lambda i, *_: (jnp.maximum(i - N_PROMPT_TILES, 0), 0))]


def _on_row_source(fn, *ref_groups):
    if all(len(g) == 1 for g in ref_groups):
        fn(*[g[0] for g in ref_groups])
        return
    i = pl.program_id(0)
    pl.when(i < N_PROMPT_TILES)(lambda: fn(*[g[0] for g in ref_groups]))
    pl.when(i >= N_PROMPT_TILES)(lambda: fn(*[g[-1] for g in ref_groups]))


FFN_TM = 1024
FFN_TF = 256
FFN_TF_SAMPLE = 512
FFN_VMEM_LIMIT = 60 * 1024 * 1024


def _ffn_body(final_norm, emit_w, x_ref, nw_ref, wg_ref, wu_ref, wd_ref, *rest):
    rest = list(rest)
    fw_ref = rest.pop(0) if final_norm else None
    o_ref = rest.pop(0)
    h_ref = rest.pop()
    j = pl.program_id(1)

    @pl.when(j == 0)
    def _():
        x = x_ref[...]
        h_ref[...] = _rms(x, nw_ref[...]).astype(BF16)
        o_ref[...] = x

    if emit_w:
        wgo_ref, wuo_ref, wdo_ref = rest
        wgo_ref[...] = wg_ref[0].astype(BF16)
        wuo_ref[...] = wu_ref[0].astype(BF16)
        wdo_ref[...] = wd_ref[0].astype(BF16)
        wg, wu, wd = wgo_ref[...], wuo_ref[...], wdo_ref[...]
    else:
        wg, wu, wd = wg_ref[...], wu_ref[...], wd_ref[...]
    h = h_ref[...]
    g = jnp.dot(h, wg, preferred_element_type=F32)
    u = jnp.dot(h, wu, preferred_element_type=F32)
    a = (_silu(g) * (0.5 * u)).astype(BF16)
    o_ref[...] += jnp.dot(a, wd, preferred_element_type=F32)

    if final_norm:
        @pl.when(j == pl.num_programs(1) - 1)
        def _():
            o_ref[...] = _rms(o_ref[...], fw_ref[...])


def _ffn(x, layer, nw, wg, wu, wd, final_w=None):
    m = x.shape[0]
    emit_w = wg.dtype == F32
    tm = min(FFN_TM, m)
    tf = FFN_TF if emit_w else FFN_TF_SAMPLE
    nj = D_FF // tf
    final_norm = final_w is not None
    vec = pl.BlockSpec((1, D_MODEL), lambda i, j: (0, 0))
    if emit_w:
        w_specs = [
            pl.BlockSpec((1, D_MODEL, tf), lambda i, j: (layer, 0, j)),
            pl.BlockSpec((1, D_MODEL, tf), lambda i, j: (layer, 0, j)),
            pl.BlockSpec((1, tf, D_MODEL), lambda i, j: (layer, j, 0)),
        ]
    else:
        w_specs = [
            pl.BlockSpec((D_MODEL, tf), lambda i, j: (0, j)),
            pl.BlockSpec((D_MODEL, tf), lambda i, j: (0, j)),
            pl.BlockSpec((tf, D_MODEL), lambda i, j: (j, 0)),
        ]
    in_specs = [pl.BlockSpec((tm, D_MODEL), lambda i, j: (i, 0)), vec] + w_specs
    args = [x, nw.reshape(1, D_MODEL), wg, wu, wd]
    if final_norm:
        in_specs.append(vec)
        args.append(final_w.reshape(1, D_MODEL))
    out_specs = [pl.BlockSpec((tm, D_MODEL), lambda i, j: (i, 0))]
    out_shape = [jax.ShapeDtypeStruct((m, D_MODEL), F32)]
    if emit_w:
        once = lambda i, j: jnp.where(i == 0, j, nj - 1)
        out_specs += [
            pl.BlockSpec((D_MODEL, tf), lambda i, j: (0, once(i, j))),
            pl.BlockSpec((D_MODEL, tf), lambda i, j: (0, once(i, j))),
            pl.BlockSpec((tf, D_MODEL), lambda i, j: (once(i, j), 0)),
        ]
        out_shape += [
            jax.ShapeDtypeStruct((D_MODEL, D_FF), BF16),
            jax.ShapeDtypeStruct((D_MODEL, D_FF), BF16),
            jax.ShapeDtypeStruct((D_FF, D_MODEL), BF16),
        ]
    outs = pl.pallas_call(
        functools.partial(_ffn_body, final_norm, emit_w),
        grid=(m // tm, nj),
        in_specs=in_specs,
        out_specs=out_specs,
        out_shape=out_shape,
        scratch_shapes=[pltpu.VMEM((tm, D_MODEL), BF16)],
        compiler_params=pltpu.CompilerParams(dimension_semantics=("arbitrary", "arbitrary"),
                                             vmem_limit_bytes=FFN_VMEM_LIMIT),
        name="ffn",
    )(*args)
    return outs[0], tuple(outs[1:])


def _ffn_pair(x_pair, layer, nw, wg, wu, wd, final_w=None):
    y_prompt, w_bf16 = _ffn(x_pair[0], layer, nw, wg, wu, wd, final_w)
    y_sample, _ = _ffn(x_pair[1], layer, nw, *w_bf16, final_w)
    return y_prompt, y_sample


def _log_sigmoid(x):
    return jnp.minimum(x, 0.0) - jnp.log1p(jnp.exp(-jnp.abs(x)))


def _norm_proj_body(segments, with_gate, xp_ref, xs_ref, nw_ref, w_ref, b_ref, *rest):
    if with_gate:
        w1_ref, w2_ref, bg_ref = rest[:3]
        rest = rest[3:]
    o_refs = rest[:len(segments)]

    def run(x_ref):
        h = _rms(x_ref[...], nw_ref[...]).astype(BF16)
        col = 0
        for (width, _), o_ref in zip(segments, o_refs):
            for c in range(0, width, TN):
                acc = jnp.dot(h, w_ref[:, col + c:col + c + TN], preferred_element_type=F32)
                o_ref[:, c:c + TN] = (acc + b_ref[:, col + c:col + c + TN]).astype(o_ref.dtype)
            col += width
        if with_gate:
            gk = jnp.dot(h, w1_ref[...], preferred_element_type=F32)
            z = jnp.dot(gk.astype(BF16), w2_ref[...], preferred_element_type=F32) + bg_ref[...]
            rest[-1][...] = _log_sigmoid(z) / GATE_NORMALIZER
    _on_row_source(run, (xp_ref, xs_ref))


def _norm_proj(xs, nw, w, b, segments, name, gate=None):
    n = w.shape[1]
    in_specs = _row_specs(True, D_MODEL) + [
        pl.BlockSpec((1, D_MODEL), lambda i: (0, 0)),
        pl.BlockSpec((D_MODEL, n), lambda i: (0, 0), pipeline_mode=pl.Buffered(1)),
        pl.BlockSpec((1, n), lambda i: (0, 0)),
    ]
    args = [*xs, nw.reshape(1, D_MODEL), w, b.reshape(1, n)]
    out_specs = [pl.BlockSpec((TM, width), lambda i: (i, 0)) for width, _ in segments]
    out_shape = [jax.ShapeDtypeStruct((N_TOK, width), dtype) for width, dtype in segments]
    if gate is not None:
        w1p, w2p, bg = gate
        in_specs += [
            pl.BlockSpec((D_MODEL, LANES), lambda i: (0, 0)),
            pl.BlockSpec((LANES, GLA_KEY_DIM), lambda i: (0, 0)),
            pl.BlockSpec((1, GLA_KEY_DIM), lambda i: (0, 0)),
        ]
        args += [w1p, w2p, bg.reshape(1, GLA_KEY_DIM)]
        out_specs.append(pl.BlockSpec((TM, GLA_KEY_DIM), lambda i: (i, 0)))
        out_shape.append(jax.ShapeDtypeStruct((N_TOK, GLA_KEY_DIM), F32))
    return pl.pallas_call(
        functools.partial(_norm_proj_body, segments, gate is not None),
        grid=(N_TOK // TM,),
        in_specs=in_specs,
        out_specs=out_specs,
        out_shape=out_shape,
        compiler_params=pltpu.CompilerParams(dimension_semantics=("arbitrary",),
                                             vmem_limit_bytes=FFN_VMEM_LIMIT),
        name=name,
    )(*args)


def _proj_res_body(ap_ref, as_ref, w_ref, b_ref, rp_ref, rs_ref, op_ref, os_ref):
    def run(a_ref, r_ref, o_ref):
        a = a_ref[...].astype(BF16)
        for c in range(0, D_MODEL, TN):
            acc = jnp.dot(a, w_ref[:, c:c + TN], preferred_element_type=F32)
            o_ref[:, c:c + TN] = r_ref[:, c:c + TN] + acc + b_ref[:, c:c + TN]
    _on_row_source(run, (ap_ref, as_ref), (rp_ref, rs_ref), (op_ref, os_ref))


def _proj_res(a_pair, w, b, res_pair, name):
    k = w.shape[0]
    return pl.pallas_call(
        _proj_res_body,
        grid=(N_TOK // TM,),
        in_specs=_row_specs(True, k) + [
            pl.BlockSpec((k, D_MODEL), lambda i: (0, 0), pipeline_mode=pl.Buffered(1)),
            pl.BlockSpec((1, D_MODEL), lambda i: (0, 0)),
        ] + _row_specs(True, D_MODEL),
        out_specs=_row_specs(True, D_MODEL),
        out_shape=[jax.ShapeDtypeStruct((N_PROMPT, D_MODEL), F32), jax.ShapeDtypeStruct((N_SAMPLE, D_MODEL), F32)],
        compiler_params=_cparams(("arbitrary",)),
        name=name,
    )(*a_pair, w, b.reshape(1, D_MODEL), *res_pair)


def _t5_bucket_table():
    i = np.arange(WINDOW)[None, :]
    j = np.arange(2 * WINDOW)[:, None]
    n = np.maximum(WINDOW + i - j, 0)
    max_exact = NUM_BUCKETS // 2
    nf = np.maximum(n, 1).astype(np.float32)
    large = max_exact + (np.log(nf / np.float32(max_exact)) / np.float32(math.log(MAX_DISTANCE / max_exact))
                         * np.float32(NUM_BUCKETS - max_exact)).astype(np.int32)
    large = np.minimum(large, NUM_BUCKETS - 1)
    return np.where(n < max_exact, n, large).astype(np.int32)


KV_PER_STEP = LANES // HEAD_DIM
N_PAIRS = N_KV_HEADS // KV_PER_STEP
Q_COLS_PER_STEP = KV_PER_STEP * GROUP * HEAD_DIM


def _slot_head(slot):
    pair = slot // (GROUP * KV_PER_STEP)
    g = (slot // KV_PER_STEP) % GROUP
    hh = slot % KV_PER_STEP
    return (pair * KV_PER_STEP + hh) * GROUP + g


def _bias_table_body(bucket_ref, rb_ref, o_ref):
    h = _slot_head(pl.program_id(0))
    bucket = bucket_ref[...]
    acc = jnp.zeros((2 * WINDOW, WINDOW), F32)
    for b in range(NUM_BUCKETS):
        acc = jnp.where(bucket == b, rb_ref[b, h], acc)
    j = lax.broadcasted_iota(jnp.int32, (2 * WINDOW, WINDOW), 0)
    i = lax.broadcasted_iota(jnp.int32, (2 * WINDOW, WINDOW), 1)
    dist = WINDOW + i - j
    o_ref[0] = jnp.where((dist >= 0) & (dist < WINDOW), acc, NEG_INF)


def _bias_table(rel_bias):
    return pl.pallas_call(
        _bias_table_body,
        grid=(N_HEADS,),
        in_specs=[
            pl.BlockSpec((2 * WINDOW, WINDOW), lambda h: (0, 0)),
            pl.BlockSpec(memory_space=pltpu.SMEM),
        ],
        out_specs=pl.BlockSpec((1, 2 * WINDOW, WINDOW), lambda h: (h, 0, 0)),
        out_shape=jax.ShapeDtypeStruct((N_HEADS, 2 * WINDOW, WINDOW), F32),
        name="bias_table",
    )(jnp.asarray(_t5_bucket_table()), rel_bias)


def _softmax_with_sink(s, sink_col):
    m = jnp.maximum(jnp.max(s, axis=-1, keepdims=True), sink_col)
    p = jnp.exp(s - m)
    denom = jnp.sum(p, axis=-1, keepdims=True) + jnp.exp(sink_col - m)
    return p, 1.0 / denom


PAIR_SLOTS = GROUP * KV_PER_STEP
PAIR_COLS = PAIR_SLOTS * WINDOW
ONES_ROWS = 16


PAIRS_PER_STEP = 4


def _swa_prompt_body(sink_ref, q_ref, kp_ref, ko_ref, vp_ref, vo_ref, bias_ref, o_ref):
    blk = pl.program_id(2)
    head_a = lax.broadcasted_iota(jnp.int32, (WINDOW, LANES), 1) < HEAD_DIM
    no_prev = jnp.where(blk > 0, 0.0, NEG_INF)
    for t in range(PAIRS_PER_STEP):
        pair = pl.program_id(0) * PAIRS_PER_STEP + t
        lanes = slice(t * LANES, (t + 1) * LANES)
        q0 = t * Q_COLS_PER_STEP
        k = jnp.concatenate([kp_ref[:, lanes], ko_ref[:, lanes]], axis=0).astype(BF16)
        v = jnp.concatenate([vp_ref[:, lanes], vo_ref[:, lanes]], axis=0)
        vt = jnp.concatenate([v.T, jnp.ones((ONES_ROWS, 2 * WINDOW), F32)], axis=0).astype(BF16)
        parts = []
        for g in range(GROUP):
            qg = q_ref[:, q0 + g * LANES:q0 + (g + 1) * LANES] * HEAD_DIM ** -0.5
            zero = jnp.zeros_like(qg)
            parts += [jnp.where(head_a, qg, zero), jnp.where(head_a, zero, qg)]
        qbd = jnp.concatenate(parts, axis=0)
        st = lax.dot_general(k, qbd, (((1,), (1,)), ((), ())), preferred_element_type=F32)
        st = st + jnp.concatenate([bias_ref[t * PAIR_SLOTS + u] for u in range(PAIR_SLOTS)], axis=1)
        st = jnp.concatenate([st[:WINDOW] + no_prev, st[WINDOW:]], axis=0)
        sink_row = jnp.concatenate(
            [jnp.full((1, WINDOW), sink_ref[_slot_head(pair * PAIR_SLOTS + slot)], F32)
             for slot in range(PAIR_SLOTS)], axis=1)
        m = jnp.maximum(jnp.max(st, axis=0, keepdims=True), sink_row)
        pt = jnp.exp(st - m).astype(BF16)
        oa = jnp.dot(vt, pt, preferred_element_type=F32)
        inv = 1.0 / (oa[LANES:LANES + 1] + jnp.exp(sink_row - m))
        o = oa[:LANES] * inv
        for g in range(GROUP):
            c = g * KV_PER_STEP * WINDOW
            ot = jnp.concatenate([o[:HEAD_DIM, c:c + WINDOW], o[HEAD_DIM:, c + WINDOW:c + 2 * WINDOW]], axis=0)
            o_ref[:, q0 + g * LANES:q0 + (g + 1) * LANES] = ot.T.astype(o_ref.dtype)


def _swa_prompt(q, kv, bias_tbl, sinks):
    n = PAIRS_PER_STEP
    v_col0 = KV_COLS // (n * LANES)

    def prev(p, b, i):
        return b * NB + jnp.maximum(i - 1, 0)

    return pl.pallas_call(
        _swa_prompt_body,
        grid=(N_PAIRS // n, BATCH, NB),
        in_specs=[
            pl.BlockSpec(memory_space=pltpu.SMEM),
            pl.BlockSpec((WINDOW, n * Q_COLS_PER_STEP), lambda p, b, i: (b * NB + i, p)),
            pl.BlockSpec((WINDOW, n * LANES), lambda p, b, i: (prev(p, b, i), p)),
            pl.BlockSpec((WINDOW, n * LANES), lambda p, b, i: (b * NB + i, p)),
            pl.BlockSpec((WINDOW, n * LANES), lambda p, b, i: (prev(p, b, i), v_col0 + p)),
            pl.BlockSpec((WINDOW, n * LANES), lambda p, b, i: (b * NB + i, v_col0 + p)),
            pl.BlockSpec((n * PAIR_SLOTS, 2 * WINDOW, WINDOW), lambda p, b, i: (p, 0, 0)),
        ],
        out_specs=pl.BlockSpec((WINDOW, n * Q_COLS_PER_STEP), lambda p, b, i: (b * NB + i, p)),
        out_shape=jax.ShapeDtypeStruct((N_PROMPT, Q_DIM), BF16),
        compiler_params=_cparams(("arbitrary", "arbitrary", "arbitrary")),
        name="swa_prompt",
    )(sinks, q, kv, kv, kv, kv, bias_tbl)


S_ROWS = N_KV_HEADS * GROUP * DEC_SEQ


SWA_SEQS_PER_STEP = 4


def _swa_sample_body(q_ref, kn_ref, vn_ref, ck_ref, cv_ref, bias_ref, sink_ref,
                     o_ref, ko_ref, vo_ref):
    row_kv = lax.broadcasted_iota(jnp.int32, (S_ROWS, KV_COLS), 0) // (GROUP * DEC_SEQ)
    col_kv = lax.broadcasted_iota(jnp.int32, (S_ROWS, KV_COLS), 1) // HEAD_DIM
    own = row_kv == col_kv
    pad = jnp.zeros((WINDOW - SUBLANES, KV_COLS), F32)
    keep = WINDOW - DEC_SEQ
    for i in range(SWA_SEQS_PER_STEP):
        x = (q_ref[i] * HEAD_DIM ** -0.5).astype(BF16)
        xt = jnp.concatenate([x] * N_KV_HEADS, axis=1)
        qbd = jnp.where(own, xt, jnp.zeros_like(xt))
        kk = jnp.concatenate([ck_ref[i], kn_ref[i], pad], axis=0).astype(BF16)
        vv = jnp.concatenate([cv_ref[i], vn_ref[i], pad], axis=0).astype(BF16)
        s = lax.dot_general(qbd, kk, (((1,), (1,)), ((), ())), preferred_element_type=F32)
        s = s + bias_ref[...]
        p, inv = _softmax_with_sink(s, sink_ref[...])
        of = jnp.dot(p.astype(BF16), vv, preferred_element_type=F32)
        of = jnp.where(own, of, 0.0)
        o = of[:, 0:HEAD_DIM]
        for c in range(1, N_KV_HEADS):
            o = o + of[:, c * HEAD_DIM:(c + 1) * HEAD_DIM]
        o_ref[i] = o * inv
        ko_ref[i, 0:keep, :] = ck_ref[i, DEC_SEQ:WINDOW, :]
        ko_ref[i, keep:WINDOW, :] = kn_ref[i, 0:DEC_SEQ, :]
        vo_ref[i, 0:keep, :] = cv_ref[i, DEC_SEQ:WINDOW, :]
        vo_ref[i, keep:WINDOW, :] = vn_ref[i, 0:DEC_SEQ, :]


def _swa_sample(q_rows, k_new8, v_new8, cache_k, cache_v, bias_s, sink_col):
    n = SWA_SEQS_PER_STEP
    seq3 = lambda s: (s, 0, 0)
    full2 = lambda s: (0, 0)
    return pl.pallas_call(
        _swa_sample_body,
        grid=(DEC_BATCH // n,),
        in_specs=[
            pl.BlockSpec((n, S_ROWS, HEAD_DIM), seq3),
            pl.BlockSpec((n, SUBLANES, KV_COLS), seq3),
            pl.BlockSpec((n, SUBLANES, KV_COLS), seq3),
            pl.BlockSpec((n, WINDOW, KV_COLS), seq3),
            pl.BlockSpec((n, WINDOW, KV_COLS), seq3),
            pl.BlockSpec((S_ROWS, 2 * WINDOW), full2),
            pl.BlockSpec((S_ROWS, 1), full2),
        ],
        out_specs=[
            pl.BlockSpec((n, S_ROWS, HEAD_DIM), seq3),
            pl.BlockSpec((n, WINDOW, KV_COLS), seq3),
            pl.BlockSpec((n, WINDOW, KV_COLS), seq3),
        ],
        out_shape=[
            jax.ShapeDtypeStruct((DEC_BATCH, S_ROWS, HEAD_DIM), F32),
            jax.ShapeDtypeStruct((DEC_BATCH, WINDOW, KV_COLS), F32),
            jax.ShapeDtypeStruct((DEC_BATCH, WINDOW, KV_COLS), F32),
        ],
        compiler_params=_cparams(("parallel",)),
        name="swa_sample",
    )(q_rows, k_new8, v_new8, cache_k, cache_v, bias_s, sink_col)


def _gla_out(o, gate, norm_w):
    return _rms(o, norm_w) * _silu(gate)


def _split3(x):
    hi = x.astype(BF16)
    r = x - hi.astype(F32)
    mid = r.astype(BF16)
    lo = (r - mid.astype(F32)).astype(BF16)
    return hi, mid, lo


def _cumsum_rows(g):
    c = g.shape[0]
    tri = (lax.broadcasted_iota(jnp.int32, (c, c), 0) >= lax.broadcasted_iota(jnp.int32, (c, c), 1)).astype(BF16)
    return jnp.dot(jnp.concatenate([tri] * 3, axis=1), jnp.concatenate(_split3(g), axis=0),
                   preferred_element_type=F32)


def _causal(a):
    c = a.shape[0]
    keep = lax.broadcasted_iota(jnp.int32, (c, c), 0) >= lax.broadcasted_iota(jnp.int32, (c, c), 1)
    return jnp.where(keep, a, 0.0)


_NT = (((1,), (1,)), ((), ()))
_TN = (((0,), (0,)), ((), ()))


def _gla_prompt_body(*refs):
    proj_refs = refs[:BATCH]
    la_refs = refs[BATCH:2 * BATCH]
    nw_ref, o_ref, s_ref, st_ref = refs[2 * BATCH:]
    c = pl.program_id(0)

    @pl.when(c == 0)
    def _():
        st_ref[...] = jnp.zeros_like(st_ref)

    for bi in range(BATCH):
        p_ref = proj_refs[bi]
        b_all = _cumsum_rows(la_refs[bi][...])
        for h in range(GLA_HEADS):
            kc = slice(h * GLA_DK, (h + 1) * GLA_DK)
            v0 = 2 * GLA_KEY_DIM + h * GLA_DV
            b = b_all[:, kc]
            q = p_ref[:, kc].astype(F32)
            k = p_ref[:, GLA_KEY_DIM + h * GLA_DK:GLA_KEY_DIM + (h + 1) * GLA_DK].astype(F32)
            v = p_ref[:, v0:v0 + GLA_DV]
            gate = p_ref[:, v0 + GLA_VAL_DIM:v0 + GLA_VAL_DIM + GLA_DV].astype(F32)
            qe = (q * GLA_DK ** -0.5 * jnp.exp(b)).astype(BF16)
            ke = (k * jnp.exp(-b)).astype(BF16)
            a = _causal(lax.dot_general(qe, ke, _NT, preferred_element_type=F32))
            st = st_ref[bi, h]
            o = (jnp.dot(a.astype(BF16), v, preferred_element_type=F32)
                 + lax.dot_general(qe, st.astype(BF16), _NT, preferred_element_type=F32))
            b_last = b[GLA_C - 1:GLA_C, :]
            kd = (k * jnp.exp(b_last - b)).astype(BF16)
            st_new = st * jnp.exp(b_last) + lax.dot_general(v, kd, _TN, preferred_element_type=F32)
            st_ref[bi, h] = st_new
            o_ref[bi, :, h * GLA_DV:(h + 1) * GLA_DV] = _gla_out(o, gate, nw_ref[...]).astype(o_ref.dtype)

    @pl.when(c == pl.num_programs(0) - 1)
    def _():
        for bi in range(BATCH):
            for h in range(GLA_HEADS):
                s_ref[bi, h] = st_ref[bi, h].T


def _gla_prompt(proj, log_a, norm_w):
    nc = SEQ // GLA_C
    rows = [functools.partial(lambda bi, c: (bi * nc + c, 0), bi) for bi in range(BATCH)]
    o, s = pl.pallas_call(
        _gla_prompt_body,
        grid=(nc,),
        in_specs=([pl.BlockSpec((GLA_C, GLA_MAIN_DIM), r) for r in rows]
                  + [pl.BlockSpec((GLA_C, GLA_KEY_DIM), r) for r in rows]
                  + [pl.BlockSpec((1, GLA_DV), lambda c: (0, 0))]),
        out_specs=[
            pl.BlockSpec((BATCH, GLA_C, GLA_VAL_DIM), lambda c: (0, c, 0)),
            pl.BlockSpec((BATCH, GLA_HEADS, GLA_DK, GLA_DV), lambda c: (0, 0, 0, 0)),
        ],
        out_shape=[
            jax.ShapeDtypeStruct((BATCH, SEQ, GLA_VAL_DIM), BF16),
            jax.ShapeDtypeStruct((BATCH, GLA_HEADS, GLA_DK, GLA_DV), F32),
        ],
        scratch_shapes=[pltpu.VMEM((BATCH, GLA_HEADS, GLA_DV, GLA_DK), F32)],
        compiler_params=_cparams(("arbitrary",)),
        name="gla_prompt",
    )(*([proj] * BATCH + [log_a] * BATCH + [norm_w.reshape(1, GLA_DV)]))
    return o.reshape(N_PROMPT, GLA_VAL_DIM), s


GLA_SEQS_PER_STEP = 4


def _gla_sample_body(proj_ref, la_ref, s0_ref, nw_ref, o_ref, s_ref):
    ones = jnp.ones((DEC_SEQ, LANES), BF16)
    for i in range(GLA_SEQS_PER_STEP):
        for h in range(GLA_HEADS):
            q = proj_ref[i, :, h * GLA_DK:(h + 1) * GLA_DK]
            k = proj_ref[i, :, GLA_KEY_DIM + h * GLA_DK:GLA_KEY_DIM + (h + 1) * GLA_DK]
            v0 = 2 * GLA_KEY_DIM + h * GLA_DV
            v = proj_ref[i, :, v0:v0 + GLA_DV].astype(BF16)
            gate = proj_ref[i, :, v0 + GLA_VAL_DIM:v0 + GLA_VAL_DIM + GLA_DV]
            g = la_ref[i, :, h * GLA_DK:(h + 1) * GLA_DK]
            rows = [g[0:1]]
            for t in range(1, DEC_SEQ):
                rows.append(rows[-1] + g[t:t + 1])
            b = jnp.concatenate(rows, axis=0)
            b_last = rows[-1]
            qe = (q * GLA_DK ** -0.5 * jnp.exp(b)).astype(BF16)
            ke = (k * jnp.exp(-b)).astype(BF16)
            a = _causal(lax.dot_general(qe, ke, _NT, preferred_element_type=F32))
            s0 = s0_ref[i, h]
            o = (jnp.dot(a.astype(BF16), v, preferred_element_type=F32)
                 + jnp.dot(qe, s0.astype(BF16), preferred_element_type=F32))
            kd = (k * jnp.exp(b_last - b)).astype(BF16)
            dsum = sum(lax.dot_general(piece, ones, _TN, preferred_element_type=F32) for piece in _split3(g))
            decay = jnp.concatenate([jnp.exp(dsum)] * (GLA_DV // LANES), axis=1)
            s_ref[i, h] = s0 * decay + lax.dot_general(kd, v, _TN, preferred_element_type=F32)
            o_ref[i, :, h * GLA_DV:(h + 1) * GLA_DV] = _gla_out(o, gate, nw_ref[...])


def _gla_sample(proj3, log_a3, state, norm_w):
    n = GLA_SEQS_PER_STEP
    seq3 = lambda s: (s, 0, 0)
    seq4 = lambda s: (s, 0, 0, 0)
    return pl.pallas_call(
        _gla_sample_body,
        grid=(DEC_BATCH // n,),
        in_specs=[
            pl.BlockSpec((n, DEC_SEQ, GLA_MAIN_DIM), seq3),
            pl.BlockSpec((n, DEC_SEQ, GLA_KEY_DIM), seq3),
            pl.BlockSpec((n, GLA_HEADS, GLA_DK, GLA_DV), seq4),
            pl.BlockSpec((1, GLA_DV), lambda s: (0, 0)),
        ],
        out_specs=[
            pl.BlockSpec((n, DEC_SEQ, GLA_VAL_DIM), seq3),
            pl.BlockSpec((n, GLA_HEADS, GLA_DK, GLA_DV), seq4),
        ],
        out_shape=[
            jax.ShapeDtypeStruct((DEC_BATCH, DEC_SEQ, GLA_VAL_DIM), F32),
            jax.ShapeDtypeStruct((DEC_BATCH, GLA_HEADS, GLA_DK, GLA_DV), F32),
        ],
        compiler_params=_cparams(("parallel",)),
        name="gla_sample",
    )(proj3, log_a3, state, norm_w.reshape(1, GLA_DV))


def _swa_layer(x, cache_k, cache_v, norm_w, w_qkv, b_qkv, w_o, b_o, sinks, rel_bias):
    slots = (N_PAIRS, KV_PER_STEP, GROUP, HEAD_DIM)
    w_q = w_qkv[:, :Q_DIM].reshape((D_MODEL,) + slots).transpose(0, 1, 3, 2, 4).reshape(D_MODEL, Q_DIM)
    b_q = b_qkv[:Q_DIM].reshape(slots).transpose(0, 2, 1, 3).reshape(Q_DIM)
    w_qkv_s = jnp.concatenate([w_q, w_qkv[:, Q_DIM:]], axis=1).astype(BF16)
    b_qkv_s = jnp.concatenate([b_q, b_qkv[Q_DIM:]])
    w_o_s = w_o.reshape(slots + (D_MODEL,)).transpose(0, 2, 1, 3, 4).reshape(Q_DIM, D_MODEL).astype(BF16)

    q, kv = _norm_proj(x, norm_w, w_qkv_s, b_qkv_s, ((Q_DIM, BF16), (2 * KV_COLS, F32)), "swa_qkv")
    bias_tbl = _bias_table(rel_bias)
    o_p = _swa_prompt(q, kv, bias_tbl, sinks)

    q_rows = (q[N_PROMPT:].astype(F32).reshape(DEC_BATCH, DEC_SEQ, N_PAIRS, GROUP, KV_PER_STEP, HEAD_DIM)
              .transpose(0, 2, 4, 3, 1, 5).reshape(DEC_BATCH, S_ROWS, HEAD_DIM))
    kv_s = kv[N_PROMPT:].reshape(DEC_BATCH, DEC_SEQ, 2 * KV_COLS)
    pad8 = ((0, 0), (0, SUBLANES - DEC_SEQ), (0, 0))
    k_new8 = jnp.pad(kv_s[..., :KV_COLS], pad8)
    v_new8 = jnp.pad(kv_s[..., KV_COLS:], pad8)
    bias_s = (bias_tbl[:, :, :DEC_SEQ].reshape(N_PAIRS, GROUP, KV_PER_STEP, 2 * WINDOW, DEC_SEQ)
              .transpose(0, 2, 1, 4, 3).reshape(S_ROWS, 2 * WINDOW))
    sink_col = jnp.repeat(sinks, DEC_SEQ).reshape(S_ROWS, 1)
    o_s, k_s, v_s = _swa_sample(q_rows, k_new8, v_new8,
                                cache_k.reshape(DEC_BATCH, WINDOW, KV_COLS),
                                cache_v.reshape(DEC_BATCH, WINDOW, KV_COLS), bias_s, sink_col)
    o_s = (o_s.reshape(DEC_BATCH, N_PAIRS, KV_PER_STEP, GROUP, DEC_SEQ, HEAD_DIM)
           .transpose(0, 4, 1, 3, 2, 5).reshape(N_SAMPLE, Q_DIM))
    x = _proj_res((o_p, o_s), w_o_s, b_o, x, "swa_out")

    kv_p = kv[:N_PROMPT].reshape(BATCH, SEQ, 2 * KV_COLS)[:, SEQ - WINDOW:]
    k_p = kv_p[..., :KV_COLS].reshape(BATCH, WINDOW, N_KV_HEADS, HEAD_DIM)
    v_p = kv_p[..., KV_COLS:].reshape(BATCH, WINDOW, N_KV_HEADS, HEAD_DIM)
    shape_s = (DEC_BATCH, WINDOW, N_KV_HEADS, HEAD_DIM)
    return x, k_p, v_p, k_s.reshape(shape_s), v_s.reshape(shape_s)


def _gla_layer(x, state, norm_w, w_in, w_gk2, b_gk, gnorm, w_o):
    w_main = w_in[:, :GLA_MAIN_DIM].astype(BF16)
    w1p = jnp.pad(w_in[:, GLA_MAIN_DIM:], ((0, 0), (0, LANES - GATE_RANK))).astype(BF16)
    w2p = jnp.pad(w_gk2, ((0, LANES - GATE_RANK), (0, 0))).astype(BF16)
    proj, log_a = _norm_proj(x, norm_w, w_main, jnp.zeros((GLA_MAIN_DIM,), F32), ((GLA_MAIN_DIM, BF16),),
                             "gla_in", gate=(w1p, w2p, b_gk))
    o_p, s_p = _gla_prompt(proj, log_a, gnorm)
    proj_s = proj[N_PROMPT:].astype(F32).reshape(DEC_BATCH, DEC_SEQ, GLA_MAIN_DIM)
    log_a_s = log_a[N_PROMPT:].reshape(DEC_BATCH, DEC_SEQ, GLA_KEY_DIM)
    o_s, s_s = _gla_sample(proj_s, log_a_s, state, gnorm)
    x = _proj_res((o_p, o_s.reshape(N_SAMPLE, GLA_VAL_DIM)), w_o.astype(BF16), jnp.zeros((D_MODEL,), F32), x,
                  "gla_out")
    return x, s_p, s_s


def kernel(x_prompt, x_sample, cache_swa_k, cache_swa_v, state_gla, norm_ffn1, ffn1_w_gate, ffn1_w_up,
           ffn1_w_down, norm_mix, norm_ffn2, ffn2_w_gate, ffn2_w_up, ffn2_w_down, norm_final, rel_bias,
           swa_w_qkv, swa_b_qkv, swa_w_o, swa_b_o, swa_sinks, gla_w_in, gla_w_gk2, gla_b_gk, gla_norm,
           gla_w_o):
    x = (x_prompt.reshape(N_PROMPT, D_MODEL), x_sample.reshape(N_SAMPLE, D_MODEL))
    swa_kp, swa_vp, swa_ks, swa_vs, gla_sp, gla_ss = [], [], [], [], [], []
    for i in range(DEPTH):
        x = _ffn_pair(x, i, norm_ffn1[i], ffn1_w_gate, ffn1_w_up, ffn1_w_down)
        j = i // 2
        if i % 2 == 0:
            x, kp, vp, ks, vs = _swa_layer(x, cache_swa_k[j], cache_swa_v[j], norm_mix[i], swa_w_qkv[j],
                                           swa_b_qkv[j], swa_w_o[j], swa_b_o[j], swa_sinks[j], rel_bias)
            swa_kp.append(kp)
            swa_vp.append(vp)
            swa_ks.append(ks)
            swa_vs.append(vs)
        else:
            x, sp, ss = _gla_layer(x, state_gla[j], norm_mix[i], gla_w_in[j], gla_w_gk2[j], gla_b_gk[j],
                                   gla_norm[j], gla_w_o[j])
            gla_sp.append(sp)
            gla_ss.append(ss)
        final_w = norm_final if i == DEPTH - 1 else None
        x = _ffn_pair(x, i, norm_ffn2[i], ffn2_w_gate, ffn2_w_up, ffn2_w_down, final_w)
    y_prompt = x[0].reshape(BATCH, SEQ, D_MODEL)
    y_sample = x[1].reshape(DEC_BATCH, DEC_SEQ, D_MODEL)
    return (y_prompt, y_sample, jnp.stack(swa_kp), jnp.stack(swa_vp), jnp.stack(swa_ks), jnp.stack(swa_vs),
            jnp.stack(gla_sp), jnp.stack(gla_ss))
```

```python
import functools
import math

import numpy as np
import jax
import jax.numpy as jnp
from jax import lax
from jax.experimental import pallas as pl
from jax.experimental.pallas import tpu as pltpu

F32 = jnp.float32
BF16 = jnp.bfloat16

D_MODEL = 2048
BATCH = 2
SEQ = 4096
DEPTH = 2
DEC_BATCH = 128
DEC_SEQ = 4
RMS_EPS = 1e-6
D_FF = 5632
N_HEADS = 32
N_KV_HEADS = 8
HEAD_DIM = 64
GROUP = N_HEADS // N_KV_HEADS
WINDOW = 128
NUM_BUCKETS = 32
MAX_DISTANCE = 128
NEG_INF = -1e30
GLA_HEADS = 4
GLA_DK = 256
GLA_DV = 512
GLA_KEY_DIM = GLA_HEADS * GLA_DK
GLA_VAL_DIM = GLA_HEADS * GLA_DV
GATE_RANK = 16
GATE_NORMALIZER = 16.0
GLA_MAIN_DIM = 2 * GLA_KEY_DIM + 2 * GLA_VAL_DIM
Q_DIM = N_HEADS * HEAD_DIM
KV_COLS = N_KV_HEADS * HEAD_DIM

N_PROMPT = BATCH * SEQ
N_SAMPLE = DEC_BATCH * DEC_SEQ
N_TOK = N_PROMPT + N_SAMPLE

LANES = 128
SUBLANES = 8
VMEM_LIMIT = 56 * 1024 * 1024

TM = 512
TF = 512
TN = 512
GLA_C = 64
NB = SEQ // WINDOW
N_PROMPT_TILES = N_PROMPT // TM


def _rms(x, w):
    return x * lax.rsqrt(jnp.mean(x * x, axis=-1, keepdims=True) + RMS_EPS) * w


def _silu(x):
    return x * jax.nn.sigmoid(x)


def _cparams(sem):
    return pltpu.CompilerParams(dimension_semantics=sem, vmem_limit_bytes=VMEM_LIMIT)


def _row_specs(split, width):
    if not split:
        return [pl.BlockSpec((TM, width), lambda i, *_: (i, 0))]
    return [pl.BlockSpec((TM, width), lambda i, *_: (jnp.minimum(i, N_PROMPT_TILES - 1), 0)),
            pl.BlockSpec((TM, width), lambda i, *_: (jnp.maximum(i - N_PROMPT_TILES, 0), 0))]


def _on_row_source(fn, *ref_groups):
    if all(len(g) == 1 for g in ref_groups):
        fn(*[g[0] for g in ref_groups])
        return
    i = pl.program_id(0)
    pl.when(i < N_PROMPT_TILES)(lambda: fn(*[g[0] for g in ref_groups]))
    pl.when(i >= N_PROMPT_TILES)(lambda: fn(*[g[-1] for g in ref_groups]))


FFN_TM = 1024
FFN_TF = 256
FFN_TF_SAMPLE = 512
FFN_VMEM_LIMIT = 60 * 1024 * 1024


def _ffn_body(final_norm, emit_w, x_ref, nw_ref, wg_ref, wu_ref, wd_ref, *rest):
    rest = list(rest)
    fw_ref = rest.pop(0) if final_norm else None
    o_ref = rest.pop(0)
    h_ref = rest.pop()
    j = pl.program_id(1)

    @pl.when(j == 0)
    def _():
        x = x_ref[...]
        h_ref[...] = _rms(x, nw_ref[...]).astype(BF16)
        o_ref[...] = x

    if emit_w:
        wgo_ref, wuo_ref, wdo_ref = rest
        wgo_ref[...] = wg_ref[0].astype(BF16)
        wuo_ref[...] = wu_ref[0].astype(BF16)
        wdo_ref[...] = wd_ref[0].astype(BF16)
        wg, wu, wd = wgo_ref[...], wuo_ref[...], wdo_ref[...]
    else:
        wg, wu, wd = wg_ref[...], wu_ref[...], wd_ref[...]
    h = h_ref[...]
    g = jnp.dot(h, wg, preferred_element_type=F32)
    u = jnp.dot(h, wu, preferred_element_type=F32)
    a = (_silu(g) * (0.5 * u)).astype(BF16)
    o_ref[...] += jnp.dot(a, wd, preferred_element_type=F32)

    if final_norm:
        @pl.when(j == pl.num_programs(1) - 1)
        def _():
            o_ref[...] = _rms(o_ref[...], fw_ref[...])


def _ffn(x, layer, nw, wg, wu, wd, final_w=None):
    m = x.shape[0]
    emit_w = wg.dtype == F32
    tm = min(FFN_TM, m)
    tf = FFN_TF if emit_w else FFN_TF_SAMPLE
    nj = D_FF // tf
    final_norm = final_w is not None
    vec = pl.BlockSpec((1, D_MODEL), lambda i, j: (0, 0))
    if emit_w:
        w_specs = [
            pl.BlockSpec((1, D_MODEL, tf), lambda i, j: (layer, 0, j)),
            pl.BlockSpec((1, D_MODEL, tf), lambda i, j: (layer, 0, j)),
            pl.BlockSpec((1, tf, D_MODEL), lambda i, j: (layer, j, 0)),
        ]
    else:
        w_specs = [
            pl.BlockSpec((D_MODEL, tf), lambda i, j: (0, j)),
            pl.BlockSpec((D_MODEL, tf), lambda i, j: (0, j)),
            pl.BlockSpec((tf, D_MODEL), lambda i, j: (j, 0)),
        ]
    in_specs = [pl.BlockSpec((tm, D_MODEL), lambda i, j: (i, 0)), vec] + w_specs
    args = [x, nw.reshape(1, D_MODEL), wg, wu, wd]
    if final_norm:
        in_specs.append(vec)
        args.append(final_w.reshape(1, D_MODEL))
    out_specs = [pl.BlockSpec((tm, D_MODEL), lambda i, j: (i, 0))]
    out_shape = [jax.ShapeDtypeStruct((m, D_MODEL), F32)]
    if emit_w:
        once = lambda i, j: jnp.where(i == 0, j, nj - 1)
        out_specs += [
            pl.BlockSpec((D_MODEL, tf), lambda i, j: (0, once(i, j))),
            pl.BlockSpec((D_MODEL, tf), lambda i, j: (0, once(i, j))),
            pl.BlockSpec((tf, D_MODEL), lambda i, j: (once(i, j), 0)),
        ]
        out_shape += [
            jax.ShapeDtypeStruct((D_MODEL, D_FF), BF16),
            jax.ShapeDtypeStruct((D_MODEL, D_FF), BF16),
            jax.ShapeDtypeStruct((D_FF, D_MODEL), BF16),
        ]
    outs = pl.pallas_call(
        functools.partial(_ffn_body, final_norm, emit_w),
        grid=(m // tm, nj),
        in_specs=in_specs,
        out_specs=out_specs,
        out_shape=out_shape,
        scratch_shapes=[pltpu.VMEM((tm, D_MODEL), BF16)],
        compiler_params=pltpu.CompilerParams(dimension_semantics=("arbitrary", "arbitrary"),
                                             vmem_limit_bytes=FFN_VMEM_LIMIT),
        name="ffn",
    )(*args)
    return outs[0], tuple(outs[1:])


def _ffn_pair(x_pair, layer, nw, wg, wu, wd, final_w=None):
    y_prompt, w_bf16 = _ffn(x_pair[0], layer, nw, wg, wu, wd, final_w)
    y_sample, _ = _ffn(x_pair[1], layer, nw, *w_bf16, final_w)
    return y_prompt, y_sample


def _log_sigmoid(x):
    return jnp.minimum(x, 0.0) - jnp.log1p(jnp.exp(-jnp.abs(x)))


def _norm_proj_body(segments, with_gate, xp_ref, xs_ref, nw_ref, w_ref, b_ref, *rest):
    if with_gate:
        w1_ref, w2_ref, bg_ref = rest[:3]
        rest = rest[3:]
    o_refs = rest[:len(segments)]

    def run(x_ref):
        h = _rms(x_ref[...], nw_ref[...]).astype(BF16)
        col = 0
        for (width, _), o_ref in zip(segments, o_refs):
            for c in range(0, width, TN):
                acc = jnp.dot(h, w_ref[:, col + c:col + c + TN], preferred_element_type=F32)
                o_ref[:, c:c + TN] = (acc + b_ref[:, col + c:col + c + TN]).astype(o_ref.dtype)
            col += width
        if with_gate:
            gk = jnp.dot(h, w1_ref[...], preferred_element_type=F32)
            z = jnp.dot(gk.astype(BF16), w2_ref[...], preferred_element_type=F32) + bg_ref[...]
            rest[-1][...] = _log_sigmoid(z) / GATE_NORMALIZER
    _on_row_source(run, (xp_ref, xs_ref))


def _norm_proj(xs, nw, w, b, segments, name, gate=None):
    n = w.shape[1]
    in_specs = _row_specs(True, D_MODEL) + [
        pl.BlockSpec((1, D_MODEL), lambda i: (0, 0)),
        pl.BlockSpec((D_MODEL, n), lambda i: (0, 0), pipeline_mode=pl.Buffered(1)),
        pl.BlockSpec((1, n), lambda i: (0, 0)),
    ]
    args = [*xs, nw.reshape(1, D_MODEL), w, b.reshape(1, n)]
    out_specs = [pl.BlockSpec((TM, width), lambda i: (i, 0)) for width, _ in segments]
    out_shape = [jax.ShapeDtypeStruct((N_TOK, width), dtype) for width, dtype in segments]
    if gate is not None:
        w1p, w2p, bg = gate
        in_specs += [
            pl.BlockSpec((D_MODEL, LANES), lambda i: (0, 0)),
            pl.BlockSpec((LANES, GLA_KEY_DIM), lambda i: (0, 0)),
            pl.BlockSpec((1, GLA_KEY_DIM), lambda i: (0, 0)),
        ]
        args += [w1p, w2p, bg.reshape(1, GLA_KEY_DIM)]
        out_specs.append(pl.BlockSpec((TM, GLA_KEY_DIM), lambda i: (i, 0)))
        out_shape.append(jax.ShapeDtypeStruct((N_TOK, GLA_KEY_DIM), F32))
    return pl.pallas_call(
        functools.partial(_norm_proj_body, segments, gate is not None),
        grid=(N_TOK // TM,),
        in_specs=in_specs,
        out_specs=out_specs,
        out_shape=out_shape,
        compiler_params=pltpu.CompilerParams(dimension_semantics=("arbitrary",),
                                             vmem_limit_bytes=FFN_VMEM_LIMIT),
        name=name,
    )(*args)


def _proj_res_body(ap_ref, as_ref, w_ref, b_ref, rp_ref, rs_ref, op_ref, os_ref):
    def run(a_ref, r_ref, o_ref):
        a = a_ref[...].astype(BF16)
        for c in range(0, D_MODEL, TN):
            acc = jnp.dot(a, w_ref[:, c:c + TN], preferred_element_type=F32)
            o_ref[:, c:c + TN] = r_ref[:, c:c + TN] + acc + b_ref[:, c:c + TN]
    _on_row_source(run, (ap_ref, as_ref), (rp_ref, rs_ref), (op_ref, os_ref))


def _proj_res(a_pair, w, b, res_pair, name):
    k = w.shape[0]
    return pl.pallas_call(
        _proj_res_body,
        grid=(N_TOK // TM,),
        in_specs=_row_specs(True, k) + [
            pl.BlockSpec((k, D_MODEL), lambda i: (0, 0), pipeline_mode=pl.Buffered(1)),
            pl.BlockSpec((1, D_MODEL), lambda i: (0, 0)),
        ] + _row_specs(True, D_MODEL),
        out_specs=_row_specs(True, D_MODEL),
        out_shape=[jax.ShapeDtypeStruct((N_PROMPT, D_MODEL), F32), jax.ShapeDtypeStruct((N_SAMPLE, D_MODEL), F32)],
        compiler_params=_cparams(("arbitrary",)),
        name=name,
    )(*a_pair, w, b.reshape(1, D_MODEL), *res_pair)


def _t5_bucket_table():
    i = np.arange(WINDOW)[None, :]
    j = np.arange(2 * WINDOW)[:, None]
    n = np.maximum(WINDOW + i - j, 0)
    max_exact = NUM_BUCKETS // 2
    nf = np.maximum(n, 1).astype(np.float32)
    large = max_exact + (np.log(nf / np.float32(max_exact)) / np.float32(math.log(MAX_DISTANCE / max_exact))
                         * np.float32(NUM_BUCKETS - max_exact)).astype(np.int32)
    large = np.minimum(large, NUM_BUCKETS - 1)
    return np.where(n < max_exact, n, large).astype(np.int32)


KV_PER_STEP = LANES // HEAD_DIM
N_PAIRS = N_KV_HEADS // KV_PER_STEP
Q_COLS_PER_STEP = KV_PER_STEP * GROUP * HEAD_DIM


def _slot_head(slot):
    pair = slot // (GROUP * KV_PER_STEP)
    g = (slot // KV_PER_STEP) % GROUP
    hh = slot % KV_PER_STEP
    return (pair * KV_PER_STEP + hh) * GROUP + g


def _bias_table_body(bucket_ref, rb_ref, o_ref):
    h = _slot_head(pl.program_id(0))
    bucket = bucket_ref[...]
    acc = jnp.zeros((2 * WINDOW, WINDOW), F32)
    for b in range(NUM_BUCKETS):
        acc = jnp.where(bucket == b, rb_ref[b, h], acc)
    j = lax.broadcasted_iota(jnp.int32, (2 * WINDOW, WINDOW), 0)
    i = lax.broadcasted_iota(jnp.int32, (2 * WINDOW, WINDOW), 1)
    dist = WINDOW + i - j
    o_ref[0] = jnp.where((dist >= 0) & (dist < WINDOW), acc, NEG_INF)


def _bias_table(rel_bias):
    return pl.pallas_call(
        _bias_table_body,
        grid=(N_HEADS,),
        in_specs=[
            pl.BlockSpec((2 * WINDOW, WINDOW), lambda h: (0, 0)),
            pl.BlockSpec(memory_space=pltpu.SMEM),
        ],
        out_specs=pl.BlockSpec((1, 2 * WINDOW, WINDOW), lambda h: (h, 0, 0)),
        out_shape=jax.ShapeDtypeStruct((N_HEADS, 2 * WINDOW, WINDOW), F32),
        name="bias_table",
    )(jnp.asarray(_t5_bucket_table()), rel_bias)


def _softmax_with_sink(s, sink_col):
    m = jnp.maximum(jnp.max(s, axis=-1, keepdims=True), sink_col)
    p = jnp.exp(s - m)
    denom = jnp.sum(p, axis=-1, keepdims=True) + jnp.exp(sink_col - m)
    return p, 1.0 / denom


PAIR_SLOTS = GROUP * KV_PER_STEP
PAIR_COLS = PAIR_SLOTS * WINDOW
ONES_ROWS = 16


PAIRS_PER_STEP = 4


def _swa_prompt_body(sink_ref, q_ref, kp_ref, ko_ref, vp_ref, vo_ref, bias_ref, o_ref):
    blk = pl.program_id(2)
    head_a = lax.broadcasted_iota(jnp.int32, (WINDOW, LANES), 1) < HEAD_DIM
    no_prev = jnp.where(blk > 0, 0.0, NEG_INF)
    for t in range(PAIRS_PER_STEP):
        pair = pl.program_id(0) * PAIRS_PER_STEP + t
        lanes = slice(t * LANES, (t + 1) * LANES)
        q0 = t * Q_COLS_PER_STEP
        k = jnp.concatenate([kp_ref[:, lanes], ko_ref[:, lanes]], axis=0).astype(BF16)
        v = jnp.concatenate([vp_ref[:, lanes], vo_ref[:, lanes]], axis=0)
        vt = jnp.concatenate([v.T, jnp.ones((ONES_ROWS, 2 * WINDOW), F32)], axis=0).astype(BF16)
        parts = []
        for g in range(GROUP):
            qg = q_ref[:, q0 + g * LANES:q0 + (g + 1) * LANES] * HEAD_DIM ** -0.5
            zero = jnp.zeros_like(qg)
            parts += [jnp.where(head_a, qg, zero), jnp.where(head_a, zero, qg)]
        qbd = jnp.concatenate(parts, axis=0)
        st = lax.dot_general(k, qbd, (((1,), (1,)), ((), ())), preferred_element_type=F32)
        st = st + jnp.concatenate([bias_ref[t * PAIR_SLOTS + u] for u in range(PAIR_SLOTS)], axis=1)
        st = jnp.concatenate([st[:WINDOW] + no_prev, st[WINDOW:]], axis=0)
        sink_row = jnp.concatenate(
            [jnp.full((1, WINDOW), sink_ref[_slot_head(pair * PAIR_SLOTS + slot)], F32)
             for slot in range(PAIR_SLOTS)], axis=1)
        m = jnp.maximum(jnp.max(st, axis=0, keepdims=True), sink_row)
        pt = jnp.exp(st - m).astype(BF16)
        oa = jnp.dot(vt, pt, preferred_element_type=F32)
        inv = 1.0 / (oa[LANES:LANES + 1] + jnp.exp(sink_row - m))
        o = oa[:LANES] * inv
        for g in range(GROUP):
            c = g * KV_PER_STEP * WINDOW
            ot = jnp.concatenate([o[:HEAD_DIM, c:c + WINDOW], o[HEAD_DIM:, c + WINDOW:c + 2 * WINDOW]], axis=0)
            o_ref[:, q0 + g * LANES:q0 + (g + 1) * LANES] = ot.T.astype(o_ref.dtype)


def _swa_prompt(q, kv, bias_tbl, sinks):
    n = PAIRS_PER_STEP
    v_col0 = KV_COLS // (n * LANES)

    def prev(p, b, i):
        return b * NB + jnp.maximum(i - 1, 0)

    return pl.pallas_call(
        _swa_prompt_body,
        grid=(N_PAIRS // n, BATCH, NB),
        in_specs=[
            pl.BlockSpec(memory_space=pltpu.SMEM),
            pl.BlockSpec((WINDOW, n * Q_COLS_PER_STEP), lambda p, b, i: (b * NB + i, p)),
            pl.BlockSpec((WINDOW, n * LANES), lambda p, b, i: (prev(p, b, i), p)),
            pl.BlockSpec((WINDOW, n * LANES), lambda p, b, i: (b * NB + i, p)),
            pl.BlockSpec((WINDOW, n * LANES), lambda p, b, i: (prev(p, b, i), v_col0 + p)),
            pl.BlockSpec((WINDOW, n * LANES), lambda p, b, i: (b * NB + i, v_col0 + p)),
            pl.BlockSpec((n * PAIR_SLOTS, 2 * WINDOW, WINDOW), lambda p, b, i: (p, 0, 0)),
        ],
        out_specs=pl.BlockSpec((WINDOW, n * Q_COLS_PER_STEP), lambda p, b, i: (b * NB + i, p)),
        out_shape=jax.ShapeDtypeStruct((N_PROMPT, Q_DIM), BF16),
        compiler_params=_cparams(("arbitrary", "arbitrary", "arbitrary")),
        name="swa_prompt",
    )(sinks, q, kv, kv, kv, kv, bias_tbl)


S_ROWS = N_KV_HEADS * GROUP * DEC_SEQ


SWA_SEQS_PER_STEP = 4


def _swa_sample_body(q_ref, kn_ref, vn_ref, ck_ref, cv_ref, bias_ref, sink_ref,
                     o_ref, ko_ref, vo_ref):
    row_kv = lax.broadcasted_iota(jnp.int32, (S_ROWS, KV_COLS), 0) // (GROUP * DEC_SEQ)
    col_kv = lax.broadcasted_iota(jnp.int32, (S_ROWS, KV_COLS), 1) // HEAD_DIM
    own = row_kv == col_kv
    pad = jnp.zeros((WINDOW - SUBLANES, KV_COLS), F32)
    keep = WINDOW - DEC_SEQ
    for i in range(SWA_SEQS_PER_STEP):
        x = (q_ref[i] * HEAD_DIM ** -0.5).astype(BF16)
        xt = jnp.concatenate([x] * N_KV_HEADS, axis=1)
        qbd = jnp.where(own, xt, jnp.zeros_like(xt))
        kk = jnp.concatenate([ck_ref[i], kn_ref[i], pad], axis=0).astype(BF16)
        vv = jnp.concatenate([cv_ref[i], vn_ref[i], pad], axis=0).astype(BF16)
        s = lax.dot_general(qbd, kk, (((1,), (1,)), ((), ())), preferred_element_type=F32)
        s = s + bias_ref[...]
        p, inv = _softmax_with_sink(s, sink_ref[...])
        of = jnp.dot(p.astype(BF16), vv, preferred_element_type=F32)
        of = jnp.where(own, of, 0.0)
        o = of[:, 0:HEAD_DIM]
        for c in range(1, N_KV_HEADS):
            o = o + of[:, c * HEAD_DIM:(c + 1) * HEAD_DIM]
        o_ref[i] = o * inv
        ko_ref[i, 0:keep, :] = ck_ref[i, DEC_SEQ:WINDOW, :]
        ko_ref[i, keep:WINDOW, :] = kn_ref[i, 0:DEC_SEQ, :]
        vo_ref[i, 0:keep, :] = cv_ref[i, DEC_SEQ:WINDOW, :]
        vo_ref[i, keep:WINDOW, :] = vn_ref[i, 0:DEC_SEQ, :]


def _swa_sample(q_rows, k_new8, v_new8, cache_k, cache_v, bias_s, sink_col):
    n = SWA_SEQS_PER_STEP
    seq3 = lambda s: (s, 0, 0)
    full2 = lambda s: (0, 0)
    return pl.pallas_call(
        _swa_sample_body,
        grid=(DEC_BATCH // n,),
        in_specs=[
            pl.BlockSpec((n, S_ROWS, HEAD_DIM), seq3),
            pl.BlockSpec((n, SUBLANES, KV_COLS), seq3),
            pl.BlockSpec((n, SUBLANES, KV_COLS), seq3),
            pl.BlockSpec((n, WINDOW, KV_COLS), seq3),
            pl.BlockSpec((n, WINDOW, KV_COLS), seq3),
            pl.BlockSpec((S_ROWS, 2 * WINDOW), full2),
            pl.BlockSpec((S_ROWS, 1), full2),
        ],
        out_specs=[
            pl.BlockSpec((n, S_ROWS, HEAD_DIM), seq3),
            pl.BlockSpec((n, WINDOW, KV_COLS), seq3),
            pl.BlockSpec((n, WINDOW, KV_COLS), seq3),
        ],
        out_shape=[
            jax.ShapeDtypeStruct((DEC_BATCH, S_ROWS, HEAD_DIM), F32),
            jax.ShapeDtypeStruct((DEC_BATCH, WINDOW, KV_COLS), F32),
            jax.ShapeDtypeStruct((DEC_BATCH, WINDOW, KV_COLS), F32),
        ],
        compiler_params=_cparams(("parallel",)),
        name="swa_sample",
    )(q_rows, k_new8, v_new8, cache_k, cache_v, bias_s, sink_col)


def _gla_out(o, gate, norm_w):
    return _rms(o, norm_w) * _silu(gate)


def _split3(x):
    hi = x.astype(BF16)
    r = x - hi.astype(F32)
    mid = r.astype(BF16)
    lo = (r - mid.astype(F32)).astype(BF16)
    return hi, mid, lo


def _cumsum_rows(g):
    c = g.shape[0]
    tri = (lax.broadcasted_iota(jnp.int32, (c, c), 0) >= lax.broadcasted_iota(jnp.int32, (c, c), 1)).astype(BF16)
    return jnp.dot(jnp.concatenate([tri] * 3, axis=1), jnp.concatenate(_split3(g), axis=0),
                   preferred_element_type=F32)


def _causal(a):
    c = a.shape[0]
    keep = lax.broadcasted_iota(jnp.int32, (c, c), 0) >= lax.broadcasted_iota(jnp.int32, (c, c), 1)
    return jnp.where(keep, a, 0.0)


_NT = (((1,), (1,)), ((), ()))
_TN = (((0,), (0,)), ((), ()))


def _gla_prompt_body(*refs):
    proj_refs = refs[:BATCH]
    la_refs = refs[BATCH:2 * BATCH]
    nw_ref, o_ref, s_ref, st_ref = refs[2 * BATCH:]
    c = pl.program_id(0)

    @pl.when(c == 0)
    def _():
        st_ref[...] = jnp.zeros_like(st_ref)

    for bi in range(BATCH):
        p_ref = proj_refs[bi]
        b_all = _cumsum_rows(la_refs[bi][...])
        for h in range(GLA_HEADS):
            kc = slice(h * GLA_DK, (h + 1) * GLA_DK)
            v0 = 2 * GLA_KEY_DIM + h * GLA_DV
            b = b_all[:, kc]
            q = p_ref[:, kc].astype(F32)
            k = p_ref[:, GLA_KEY_DIM + h * GLA_DK:GLA_KEY_DIM + (h + 1) * GLA_DK].astype(F32)
            v = p_ref[:, v0:v0 + GLA_DV]
            gate = p_ref[:, v0 + GLA_VAL_DIM:v0 + GLA_VAL_DIM + GLA_DV].astype(F32)
            qe = (q * GLA_DK ** -0.5 * jnp.exp(b)).astype(BF16)
            ke = (k * jnp.exp(-b)).astype(BF16)
            a = _causal(lax.dot_general(qe, ke, _NT, preferred_element_type=F32))
            st = st_ref[bi, h]
            o = (jnp.dot(a.astype(BF16), v, preferred_element_type=F32)
                 + lax.dot_general(qe, st.astype(BF16), _NT, preferred_element_type=F32))
            b_last = b[GLA_C - 1:GLA_C, :]
            kd = (k * jnp.exp(b_last - b)).astype(BF16)
            st_new = st * jnp.exp(b_last) + lax.dot_general(v, kd, _TN, preferred_element_type=F32)
            st_ref[bi, h] = st_new
            o_ref[bi, :, h * GLA_DV:(h + 1) * GLA_DV] = _gla_out(o, gate, nw_ref[...]).astype(o_ref.dtype)

    @pl.when(c == pl.num_programs(0) - 1)
    def _():
        for bi in range(BATCH):
            for h in range(GLA_HEADS):
                s_ref[bi, h] = st_ref[bi, h].T


def _gla_prompt(proj, log_a, norm_w):
    nc = SEQ // GLA_C
    rows = [functools.partial(lambda bi, c: (bi * nc + c, 0), bi) for bi in range(BATCH)]
    o, s = pl.pallas_call(
        _gla_prompt_body,
        grid=(nc,),
        in_specs=([pl.BlockSpec((GLA_C, GLA_MAIN_DIM), r) for r in rows]
                  + [pl.BlockSpec((GLA_C, GLA_KEY_DIM), r) for r in rows]
                  + [pl.BlockSpec((1, GLA_DV), lambda c: (0, 0))]),
        out_specs=[
            pl.BlockSpec((BATCH, GLA_C, GLA_VAL_DIM), lambda c: (0, c, 0)),
            pl.BlockSpec((BATCH, GLA_HEADS, GLA_DK, GLA_DV), lambda c: (0, 0, 0, 0)),
        ],
        out_shape=[
            jax.ShapeDtypeStruct((BATCH, SEQ, GLA_VAL_DIM), BF16),
            jax.ShapeDtypeStruct((BATCH, GLA_HEADS, GLA_DK, GLA_DV), F32),
        ],
        scratch_shapes=[pltpu.VMEM((BATCH, GLA_HEADS, GLA_DV, GLA_DK), F32)],
        compiler_params=_cparams(("arbitrary",)),
        name="gla_prompt",
    )(*([proj] * BATCH + [log_a] * BATCH + [norm_w.reshape(1, GLA_DV)]))
    return o.reshape(N_PROMPT, GLA_VAL_DIM), s


GLA_SEQS_PER_STEP = 4


def _gla_sample_body(proj_ref, la_ref, s0_ref, nw_ref, o_ref, s_ref):
    ones = jnp.ones((DEC_SEQ, LANES), BF16)
    proj = proj_ref[...].astype(F32)
    la = la_ref[...]
    for i in range(GLA_SEQS_PER_STEP):
        rows_i = slice(i * DEC_SEQ, (i + 1) * DEC_SEQ)
        for h in range(GLA_HEADS):
            q = proj[rows_i, h * GLA_DK:(h + 1) * GLA_DK]
            k = proj[rows_i, GLA_KEY_DIM + h * GLA_DK:GLA_KEY_DIM + (h + 1) * GLA_DK]
            v0 = 2 * GLA_KEY_DIM + h * GLA_DV
            v = proj[rows_i, v0:v0 + GLA_DV].astype(BF16)
            gate = proj[rows_i, v0 + GLA_VAL_DIM:v0 + GLA_VAL_DIM + GLA_DV]
            g = la[rows_i, h * GLA_DK:(h + 1) * GLA_DK]
            rows = [g[0:1]]
            for t in range(1, DEC_SEQ):
                rows.append(rows[-1] + g[t:t + 1])
            b = jnp.concatenate(rows, axis=0)
            b_last = rows[-1]
            qe = (q * GLA_DK ** -0.5 * jnp.exp(b)).astype(BF16)
            ke = (k * jnp.exp(-b)).astype(BF16)
            a = _causal(lax.dot_general(qe, ke, _NT, preferred_element_type=F32))
            s0 = s0_ref[i, h]
            o = (jnp.dot(a.astype(BF16), v, preferred_element_type=F32)
                 + jnp.dot(qe, s0.astype(BF16), preferred_element_type=F32))
            kd = (k * jnp.exp(b_last - b)).astype(BF16)
            dsum = sum(lax.dot_general(piece, ones, _TN, preferred_element_type=F32) for piece in _split3(g))
            decay = jnp.concatenate([jnp.exp(dsum)] * (GLA_DV // LANES), axis=1)
            s_ref[i, h] = s0 * decay + lax.dot_general(kd, v, _TN, preferred_element_type=F32)
            o_ref[rows_i, h * GLA_DV:(h + 1) * GLA_DV] = _gla_out(o, gate, nw_ref[...])


def _gla_sample(proj, log_a, state, norm_w):
    n = GLA_SEQS_PER_STEP
    rows = n * DEC_SEQ
    first = N_PROMPT // rows
    seq4 = lambda s: (s, 0, 0, 0)
    return pl.pallas_call(
        _gla_sample_body,
        grid=(DEC_BATCH // n,),
        in_specs=[
            pl.BlockSpec((rows, GLA_MAIN_DIM), lambda s: (first + s, 0)),
            pl.BlockSpec((rows, GLA_KEY_DIM), lambda s: (first + s, 0)),
            pl.BlockSpec((n, GLA_HEADS, GLA_DK, GLA_DV), seq4),
            pl.BlockSpec((1, GLA_DV), lambda s: (0, 0)),
        ],
        out_specs=[
            pl.BlockSpec((rows, GLA_VAL_DIM), lambda s: (s, 0)),
            pl.BlockSpec((n, GLA_HEADS, GLA_DK, GLA_DV), seq4),
        ],
        out_shape=[
            jax.ShapeDtypeStruct((N_SAMPLE, GLA_VAL_DIM), F32),
            jax.ShapeDtypeStruct((DEC_BATCH, GLA_HEADS, GLA_DK, GLA_DV), F32),
        ],
        compiler_params=_cparams(("parallel",)),
        name="gla_sample",
    )(proj, log_a, state, norm_w.reshape(1, GLA_DV))


CAST_ROWS = 256


def _cast_body(w_ref, o_ref):
    o_ref[...] = w_ref[0].astype(BF16)


def _layer_bf16(w, layer, cols):
    rows = w.shape[1]
    return pl.pallas_call(
        _cast_body,
        grid=(rows // CAST_ROWS,),
        in_specs=[pl.BlockSpec((1, CAST_ROWS, cols), lambda i: (layer, i, 0))],
        out_specs=pl.BlockSpec((CAST_ROWS, cols), lambda i: (i, 0)),
        out_shape=jax.ShapeDtypeStruct((rows, cols), BF16),
        compiler_params=_cparams(("arbitrary",)),
        name="cast_bf16",
    )(w)


def _swa_layer(x, cache_k, cache_v, norm_w, w_qkv, b_qkv, w_o, b_o, sinks, rel_bias):
    slots = (N_PAIRS, KV_PER_STEP, GROUP, HEAD_DIM)
    w_q = w_qkv[:, :Q_DIM].reshape((D_MODEL,) + slots).transpose(0, 1, 3, 2, 4).reshape(D_MODEL, Q_DIM)
    b_q = b_qkv[:Q_DIM].reshape(slots).transpose(0, 2, 1, 3).reshape(Q_DIM)
    w_qkv_s = jnp.concatenate([w_q, w_qkv[:, Q_DIM:]], axis=1).astype(BF16)
    b_qkv_s = jnp.concatenate([b_q, b_qkv[Q_DIM:]])
    w_o_s = w_o.reshape(slots + (D_MODEL,)).transpose(0, 2, 1, 3, 4).reshape(Q_DIM, D_MODEL).astype(BF16)

    q, kv = _norm_proj(x, norm_w, w_qkv_s, b_qkv_s, ((Q_DIM, BF16), (2 * KV_COLS, F32)), "swa_qkv")
    bias_tbl = _bias_table(rel_bias)
    o_p = _swa_prompt(q, kv, bias_tbl, sinks)

    q_rows = (q[N_PROMPT:].astype(F32).reshape(DEC_BATCH, DEC_SEQ, N_PAIRS, GROUP, KV_PER_STEP, HEAD_DIM)
              .transpose(0, 2, 4, 3, 1, 5).reshape(DEC_BATCH, S_ROWS, HEAD_DIM))
    kv_s = kv[N_PROMPT:].reshape(DEC_BATCH, DEC_SEQ, 2 * KV_COLS)
    pad8 = ((0, 0), (0, SUBLANES - DEC_SEQ), (0, 0))
    k_new8 = jnp.pad(kv_s[..., :KV_COLS], pad8)
    v_new8 = jnp.pad(kv_s[..., KV_COLS:], pad8)
    bias_s = (bias_tbl[:, :, :DEC_SEQ].reshape(N_PAIRS, GROUP, KV_PER_STEP, 2 * WINDOW, DEC_SEQ)
              .transpose(0, 2, 1, 4, 3).reshape(S_ROWS, 2 * WINDOW))
    sink_col = jnp.repeat(sinks, DEC_SEQ).reshape(S_ROWS, 1)
    o_s, k_s, v_s = _swa_sample(q_rows, k_new8, v_new8,
                                cache_k.reshape(DEC_BATCH, WINDOW, KV_COLS),
                                cache_v.reshape(DEC_BATCH, WINDOW, KV_COLS), bias_s, sink_col)
    o_s = (o_s.reshape(DEC_BATCH, N_PAIRS, KV_PER_STEP, GROUP, DEC_SEQ, HEAD_DIM)
           .transpose(0, 4, 1, 3, 2, 5).reshape(N_SAMPLE, Q_DIM))
    x = _proj_res((o_p, o_s), w_o_s, b_o, x, "swa_out")

    kv_p = jnp.stack([kv[(b + 1) * SEQ - WINDOW:(b + 1) * SEQ] for b in range(BATCH)])
    k_p = kv_p[..., :KV_COLS].reshape(BATCH, WINDOW, N_KV_HEADS, HEAD_DIM)
    v_p = kv_p[..., KV_COLS:].reshape(BATCH, WINDOW, N_KV_HEADS, HEAD_DIM)
    shape_s = (DEC_BATCH, WINDOW, N_KV_HEADS, HEAD_DIM)
    return x, k_p, v_p, k_s.reshape(shape_s), v_s.reshape(shape_s)


def _gla_layer(x, state, norm_w, j, w_in_all, w_gk2, b_gk, gnorm, w_o_all):
    w_main = _layer_bf16(w_in_all, j, GLA_MAIN_DIM)
    w1p = jnp.pad(w_in_all[j, :, GLA_MAIN_DIM:], ((0, 0), (0, LANES - GATE_RANK))).astype(BF16)
    w2p = jnp.pad(w_gk2, ((0, LANES - GATE_RANK), (0, 0))).astype(BF16)
    proj, log_a = _norm_proj(x, norm_w, w_main, jnp.zeros((GLA_MAIN_DIM,), F32), ((GLA_MAIN_DIM, BF16),),
                             "gla_in", gate=(w1p, w2p, b_gk))
    o_p, s_p = _gla_prompt(proj, log_a, gnorm)
    o_s, s_s = _gla_sample(proj, log_a, state, gnorm)
    x = _proj_res((o_p, o_s), _layer_bf16(w_o_all, j, D_MODEL), jnp.zeros((D_MODEL,), F32), x, "gla_out")
    return x, s_p, s_s


def kernel(x_prompt, x_sample, cache_swa_k, cache_swa_v, state_gla, norm_ffn1, ffn1_w_gate, ffn1_w_up,
           ffn1_w_down, norm_mix, norm_ffn2, ffn2_w_gate, ffn2_w_up, ffn2_w_down, norm_final, rel_bias,
           swa_w_qkv, swa_b_qkv, swa_w_o, swa_b_o, swa_sinks, gla_w_in, gla_w_gk2, gla_b_gk, gla_norm,
           gla_w_o):
    x = (x_prompt.reshape(N_PROMPT, D_MODEL), x_sample.reshape(N_SAMPLE, D_MODEL))
    swa_kp, swa_vp, swa_ks, swa_vs, gla_sp, gla_ss = [], [], [], [], [], []
    for i in range(DEPTH):
        x = _ffn_pair(x, i, norm_ffn1[i], ffn1_w_gate, ffn1_w_up, ffn1_w_down)
        j = i // 2
        if i % 2 == 0:
            x, kp, vp, ks, vs = _swa_layer(x, cache_swa_k[j], cache_swa_v[j], norm_mix[i], swa_w_qkv[j],
                                           swa_b_qkv[j], swa_w_o[j], swa_b_o[j], swa_sinks[j], rel_bias)
            swa_kp.append(kp)
            swa_vp.append(vp)
            swa_ks.append(ks)
            swa_vs.append(vs)
        else:
            x, sp, ss = _gla_layer(x, state_gla[j], norm_mix[i], j, gla_w_in, gla_w_gk2[j], gla_b_gk[j],
                                   gla_norm[j], gla_w_o)
            gla_sp.append(sp)
            gla_ss.append(ss)
        final_w = norm_final if i == DEPTH - 1 else None
        x = _ffn_pair(x, i, norm_ffn2[i], ffn2_w_gate, ffn2_w_up, ffn2_w_down, final_w)
    y_prompt = x[0].reshape(BATCH, SEQ, D_MODEL)
    y_sample = x[1].reshape(DEC_BATCH, DEC_SEQ, D_MODEL)
    return (y_prompt, y_sample, jnp.stack(swa_kp), jnp.stack(swa_vp), jnp.stack(swa_ks), jnp.stack(swa_vs),
            jnp.stack(gla_sp), jnp.stack(gla_ss))
```

```python
import functools
import math

import numpy as np
import jax
import jax.numpy as jnp
from jax import lax
from jax.experimental import pallas as pl
from jax.experimental.pallas import tpu as pltpu

F32 = jnp.float32
BF16 = jnp.bfloat16

D_MODEL = 2048
BATCH = 2
SEQ = 4096
DEPTH = 2
DEC_BATCH = 128
DEC_SEQ = 4
RMS_EPS = 1e-6
D_FF = 5632
N_HEADS = 32
N_KV_HEADS = 8
HEAD_DIM = 64
GROUP = N_HEADS // N_KV_HEADS
WINDOW = 128
NUM_BUCKETS = 32
MAX_DISTANCE = 128
NEG_INF = -1e30
GLA_HEADS = 4
GLA_DK = 256
GLA_DV = 512
GLA_KEY_DIM = GLA_HEADS * GLA_DK
GLA_VAL_DIM = GLA_HEADS * GLA_DV
GATE_RANK = 16
GATE_NORMALIZER = 16.0
GLA_MAIN_DIM = 2 * GLA_KEY_DIM + 2 * GLA_VAL_DIM
Q_DIM = N_HEADS * HEAD_DIM
KV_COLS = N_KV_HEADS * HEAD_DIM

N_PROMPT = BATCH * SEQ
N_SAMPLE = DEC_BATCH * DEC_SEQ
N_TOK = N_PROMPT + N_SAMPLE

LANES = 128
SUBLANES = 8
VMEM_LIMIT = 56 * 1024 * 1024

TM = 512
TF = 512
TN = 512
GLA_C = 64
NB = SEQ // WINDOW
N_PROMPT_TILES = N_PROMPT // TM


def _rms(x, w):
    return x * lax.rsqrt(jnp.mean(x * x, axis=-1, keepdims=True) + RMS_EPS) * w


def _silu(x):
    return x * jax.nn.sigmoid(x)


def _cparams(sem):
    return pltpu.CompilerParams(dimension_semantics=sem, vmem_limit_bytes=VMEM_LIMIT)


def _row_specs(split, width):
    if not split:
        return [pl.BlockSpec((TM, width), lambda i, *_: (i, 0))]
    return [pl.BlockSpec((TM, width), lambda i, *_: (jnp.minimum(i, N_PROMPT_TILES - 1), 0)),
            pl.BlockSpec((TM, width), lambda i, *_: (jnp.maximum(i - N_PROMPT_TILES, 0), 0))]


def _on_row_source(fn, *ref_groups):
    if all(len(g) == 1 for g in ref_groups):
        fn(*[g[0] for g in ref_groups])
        return
    i = pl.program_id(0)
    pl.when(i < N_PROMPT_TILES)(lambda: fn(*[g[0] for g in ref_groups]))
    pl.when(i >= N_PROMPT_TILES)(lambda: fn(*[g[-1] for g in ref_groups]))


FFN_TM = 1024
FFN_TF = 256
FFN_TF_SAMPLE = 512
FFN_VMEM_LIMIT = 60 * 1024 * 1024


def _ffn_body(final_norm, emit_w, x_ref, nw_ref, wg_ref, wu_ref, wd_ref, *rest):
    rest = list(rest)
    fw_ref = rest.pop(0) if final_norm else None
    o_ref = rest.pop(0)
    h_ref = rest.pop()
    j = pl.program_id(1)

    @pl.when(j == 0)
    def _():
        x = x_ref[...]
        h_ref[...] = _rms(x, nw_ref[...]).astype(BF16)
        o_ref[...] = x

    if emit_w:
        wgo_ref, wuo_ref, wdo_ref = rest
        wgo_ref[...] = wg_ref[0].astype(BF16)
        wuo_ref[...] = wu_ref[0].astype(BF16)
        wdo_ref[...] = wd_ref[0].astype(BF16)
        wg, wu, wd = wgo_ref[...], wuo_ref[...], wdo_ref[...]
    else:
        wg, wu, wd = wg_ref[...], wu_ref[...], wd_ref[...]
    h = h_ref[...]
    g = jnp.dot(h, wg, preferred_element_type=F32)
    u = jnp.dot(h, wu, preferred_element_type=F32)
    a = (_silu(g) * (0.5 * u)).astype(BF16)
    o_ref[...] += jnp.dot(a, wd, preferred_element_type=F32)

    if final_norm:
        @pl.when(j == pl.num_programs(1) - 1)
        def _():
            o_ref[...] = _rms(o_ref[...], fw_ref[...])


def _ffn(x, layer, nw, wg, wu, wd, final_w=None):
    m = x.shape[0]
    emit_w = wg.dtype == F32
    tm = min(FFN_TM, m)
    tf = FFN_TF if emit_w else FFN_TF_SAMPLE
    nj = D_FF // tf
    final_norm = final_w is not None
    vec = pl.BlockSpec((1, D_MODEL), lambda i, j: (0, 0))
    if emit_w:
        w_specs = [
            pl.BlockSpec((1, D_MODEL, tf), lambda i, j: (layer, 0, j)),
            pl.BlockSpec((1, D_MODEL, tf), lambda i, j: (layer, 0, j)),
            pl.BlockSpec((1, tf, D_MODEL), lambda i, j: (layer, j, 0)),
        ]
    else:
        w_specs = [
            pl.BlockSpec((D_MODEL, tf), lambda i, j: (0, j)),
            pl.BlockSpec((D_MODEL, tf), lambda i, j: (0, j)),
            pl.BlockSpec((tf, D_MODEL), lambda i, j: (j, 0)),
        ]
    in_specs = [pl.BlockSpec((tm, D_MODEL), lambda i, j: (i, 0)), vec] + w_specs
    args = [x, nw.reshape(1, D_MODEL), wg, wu, wd]
    if final_norm:
        in_specs.append(vec)
        args.append(final_w.reshape(1, D_MODEL))
    out_specs = [pl.BlockSpec((tm, D_MODEL), lambda i, j: (i, 0))]
    out_shape = [jax.ShapeDtypeStruct((m, D_MODEL), F32)]
    if emit_w:
        once = lambda i, j: jnp.where(i == 0, j, nj - 1)
        out_specs += [
            pl.BlockSpec((D_MODEL, tf), lambda i, j: (0, once(i, j))),
            pl.BlockSpec((D_MODEL, tf), lambda i, j: (0, once(i, j))),
            pl.BlockSpec((tf, D_MODEL), lambda i, j: (once(i, j), 0)),
        ]
        out_shape += [
            jax.ShapeDtypeStruct((D_MODEL, D_FF), BF16),
            jax.ShapeDtypeStruct((D_MODEL, D_FF), BF16),
            jax.ShapeDtypeStruct((D_FF, D_MODEL), BF16),
        ]
    outs = pl.pallas_call(
        functools.partial(_ffn_body, final_norm, emit_w),
        grid=(m // tm, nj),
        in_specs=in_specs,
        out_specs=out_specs,
        out_shape=out_shape,
        scratch_shapes=[pltpu.VMEM((tm, D_MODEL), BF16)],
        compiler_params=pltpu.CompilerParams(dimension_semantics=("arbitrary", "arbitrary"),
                                             vmem_limit_bytes=FFN_VMEM_LIMIT),
        name="ffn",
    )(*args)
    return outs[0], tuple(outs[1:])


def _ffn_pair(x_pair, layer, nw, wg, wu, wd, final_w=None):
    y_prompt, w_bf16 = _ffn(x_pair[0], layer, nw, wg, wu, wd, final_w)
    y_sample, _ = _ffn(x_pair[1], layer, nw, *w_bf16, final_w)
    return y_prompt, y_sample


def _log_sigmoid(x):
    return jnp.minimum(x, 0.0) - jnp.log1p(jnp.exp(-jnp.abs(x)))


def _norm_proj_body(segments, with_gate, xp_ref, xs_ref, nw_ref, w_ref, b_ref, *rest):
    if with_gate:
        w1_ref, w2_ref, bg_ref = rest[:3]
        rest = rest[3:]
    o_refs = rest[:len(segments)]

    def run(x_ref):
        h = _rms(x_ref[...], nw_ref[...]).astype(BF16)
        col = 0
        for (width, _), o_ref in zip(segments, o_refs):
            for c in range(0, width, TN):
                acc = jnp.dot(h, w_ref[:, col + c:col + c + TN], preferred_element_type=F32)
                o_ref[:, c:c + TN] = (acc + b_ref[:, col + c:col + c + TN]).astype(o_ref.dtype)
            col += width
        if with_gate:
            gk = jnp.dot(h, w1_ref[...], preferred_element_type=F32)
            z = jnp.dot(gk.astype(BF16), w2_ref[...], preferred_element_type=F32) + bg_ref[...]
            rest[-1][...] = _log_sigmoid(z) / GATE_NORMALIZER
    _on_row_source(run, (xp_ref, xs_ref))


def _norm_proj(xs, nw, w, b, segments, name, gate=None):
    n = w.shape[1]
    in_specs = _row_specs(True, D_MODEL) + [
        pl.BlockSpec((1, D_MODEL), lambda i: (0, 0)),
        pl.BlockSpec((D_MODEL, n), lambda i: (0, 0), pipeline_mode=pl.Buffered(1)),
        pl.BlockSpec((1, n), lambda i: (0, 0)),
    ]
    args = [*xs, nw.reshape(1, D_MODEL), w, b.reshape(1, n)]
    out_specs = [pl.BlockSpec((TM, width), lambda i: (i, 0)) for width, _ in segments]
    out_shape = [jax.ShapeDtypeStruct((N_TOK, width), dtype) for width, dtype in segments]
    if gate is not None:
        w1p, w2p, bg = gate
        in_specs += [
            pl.BlockSpec((D_MODEL, LANES), lambda i: (0, 0)),
            pl.BlockSpec((LANES, GLA_KEY_DIM), lambda i: (0, 0)),
            pl.BlockSpec((1, GLA_KEY_DIM), lambda i: (0, 0)),
        ]
        args += [w1p, w2p, bg.reshape(1, GLA_KEY_DIM)]
        out_specs.append(pl.BlockSpec((TM, GLA_KEY_DIM), lambda i: (i, 0)))
        out_shape.append(jax.ShapeDtypeStruct((N_TOK, GLA_KEY_DIM), F32))
    return pl.pallas_call(
        functools.partial(_norm_proj_body, segments, gate is not None),
        grid=(N_TOK // TM,),
        in_specs=in_specs,
        out_specs=out_specs,
        out_shape=out_shape,
        compiler_params=pltpu.CompilerParams(dimension_semantics=("arbitrary",),
                                             vmem_limit_bytes=FFN_VMEM_LIMIT),
        name=name,
    )(*args)


def _proj_res_body(ap_ref, as_ref, w_ref, b_ref, rp_ref, rs_ref, op_ref, os_ref):
    def run(a_ref, r_ref, o_ref):
        a = a_ref[...].astype(BF16)
        for c in range(0, D_MODEL, TN):
            acc = jnp.dot(a, w_ref[:, c:c + TN], preferred_element_type=F32)
            o_ref[:, c:c + TN] = r_ref[:, c:c + TN] + acc + b_ref[:, c:c + TN]
    _on_row_source(run, (ap_ref, as_ref), (rp_ref, rs_ref), (op_ref, os_ref))


def _proj_res(a_pair, w, b, res_pair, name):
    k = w.shape[0]
    return pl.pallas_call(
        _proj_res_body,
        grid=(N_TOK // TM,),
        in_specs=_row_specs(True, k) + [
            pl.BlockSpec((k, D_MODEL), lambda i: (0, 0), pipeline_mode=pl.Buffered(1)),
            pl.BlockSpec((1, D_MODEL), lambda i: (0, 0)),
        ] + _row_specs(True, D_MODEL),
        out_specs=_row_specs(True, D_MODEL),
        out_shape=[jax.ShapeDtypeStruct((N_PROMPT, D_MODEL), F32), jax.ShapeDtypeStruct((N_SAMPLE, D_MODEL), F32)],
        compiler_params=_cparams(("arbitrary",)),
        name=name,
    )(*a_pair, w, b.reshape(1, D_MODEL), *res_pair)


def _t5_bucket_table():
    i = np.arange(WINDOW)[None, :]
    j = np.arange(2 * WINDOW)[:, None]
    n = np.maximum(WINDOW + i - j, 0)
    max_exact = NUM_BUCKETS // 2
    nf = np.maximum(n, 1).astype(np.float32)
    large = max_exact + (np.log(nf / np.float32(max_exact)) / np.float32(math.log(MAX_DISTANCE / max_exact))
                         * np.float32(NUM_BUCKETS - max_exact)).astype(np.int32)
    large = np.minimum(large, NUM_BUCKETS - 1)
    return np.where(n < max_exact, n, large).astype(np.int32)


KV_PER_STEP = LANES // HEAD_DIM
N_PAIRS = N_KV_HEADS // KV_PER_STEP
Q_COLS_PER_STEP = KV_PER_STEP * GROUP * HEAD_DIM


def _slot_head(slot):
    pair = slot // (GROUP * KV_PER_STEP)
    g = (slot // KV_PER_STEP) % GROUP
    hh = slot % KV_PER_STEP
    return (pair * KV_PER_STEP + hh) * GROUP + g


BIAS_SLOTS_PER_STEP = 8


def _bias_table_body(bucket_ref, rb_ref, o_ref):
    bucket = bucket_ref[...]
    j = lax.broadcasted_iota(jnp.int32, (2 * WINDOW, WINDOW), 0)
    i = lax.broadcasted_iota(jnp.int32, (2 * WINDOW, WINDOW), 1)
    dist = WINDOW + i - j
    in_window = (dist >= 0) & (dist < WINDOW)
    for t in range(BIAS_SLOTS_PER_STEP):
        h = _slot_head(pl.program_id(0) * BIAS_SLOTS_PER_STEP + t)
        acc = jnp.zeros((2 * WINDOW, WINDOW), F32)
        for b in range(NUM_BUCKETS):
            acc = jnp.where(bucket == b, rb_ref[b, h], acc)
        o_ref[t] = jnp.where(in_window, acc, NEG_INF)


def _bias_table(rel_bias):
    n = BIAS_SLOTS_PER_STEP
    return pl.pallas_call(
        _bias_table_body,
        grid=(N_HEADS // n,),
        in_specs=[
            pl.BlockSpec((2 * WINDOW, WINDOW), lambda h: (0, 0)),
            pl.BlockSpec(memory_space=pltpu.SMEM),
        ],
        out_specs=pl.BlockSpec((n, 2 * WINDOW, WINDOW), lambda h: (h, 0, 0)),
        out_shape=jax.ShapeDtypeStruct((N_HEADS, 2 * WINDOW, WINDOW), F32),
        name="bias_table",
    )(jnp.asarray(_t5_bucket_table()), rel_bias)


def _softmax_with_sink(s, sink_col):
    m = jnp.maximum(jnp.max(s, axis=-1, keepdims=True), sink_col)
    p = jnp.exp(s - m)
    denom = jnp.sum(p, axis=-1, keepdims=True) + jnp.exp(sink_col - m)
    return p, 1.0 / denom


PAIR_SLOTS = GROUP * KV_PER_STEP
PAIR_COLS = PAIR_SLOTS * WINDOW
ONES_ROWS = 16


PAIRS_PER_STEP = 4


def _swa_prompt_body(sink_ref, q_ref, kp_ref, ko_ref, vp_ref, vo_ref, bias_ref, o_ref):
    blk = pl.program_id(2)
    head_a = lax.broadcasted_iota(jnp.int32, (WINDOW, LANES), 1) < HEAD_DIM
    no_prev = jnp.where(blk > 0, 0.0, NEG_INF)
    for t in range(PAIRS_PER_STEP):
        pair = pl.program_id(0) * PAIRS_PER_STEP + t
        lanes = slice(t * LANES, (t + 1) * LANES)
        q0 = t * Q_COLS_PER_STEP
        k = jnp.concatenate([kp_ref[:, lanes], ko_ref[:, lanes]], axis=0).astype(BF16)
        v = jnp.concatenate([vp_ref[:, lanes], vo_ref[:, lanes]], axis=0)
        vt = jnp.concatenate([v.T, jnp.ones((ONES_ROWS, 2 * WINDOW), F32)], axis=0).astype(BF16)
        parts = []
        for g in range(GROUP):
            qg = q_ref[:, q0 + g * LANES:q0 + (g + 1) * LANES] * HEAD_DIM ** -0.5
            zero = jnp.zeros_like(qg)
            parts += [jnp.where(head_a, qg, zero), jnp.where(head_a, zero, qg)]
        qbd = jnp.concatenate(parts, axis=0)
        st = lax.dot_general(k, qbd, (((1,), (1,)), ((), ())), preferred_element_type=F32)
        st = st + jnp.concatenate([bias_ref[t * PAIR_SLOTS + u] for u in range(PAIR_SLOTS)], axis=1)
        st = jnp.concatenate([st[:WINDOW] + no_prev, st[WINDOW:]], axis=0)
        sink_row = jnp.concatenate(
            [jnp.full((1, WINDOW), sink_ref[_slot_head(pair * PAIR_SLOTS + slot)], F32)
             for slot in range(PAIR_SLOTS)], axis=1)
        m = jnp.maximum(jnp.max(st, axis=0, keepdims=True), sink_row)
        pt = jnp.exp(st - m).astype(BF16)
        oa = jnp.dot(vt, pt, preferred_element_type=F32)
        inv = 1.0 / (oa[LANES:LANES + 1] + jnp.exp(sink_row - m))
        o = oa[:LANES] * inv
        for g in range(GROUP):
            c = g * KV_PER_STEP * WINDOW
            ot = jnp.concatenate([o[:HEAD_DIM, c:c + WINDOW], o[HEAD_DIM:, c + WINDOW:c + 2 * WINDOW]], axis=0)
            o_ref[:, q0 + g * LANES:q0 + (g + 1) * LANES] = ot.T.astype(o_ref.dtype)


def _swa_prompt(q, kv, bias_tbl, sinks):
    n = PAIRS_PER_STEP
    v_col0 = KV_COLS // (n * LANES)

    def prev(p, b, i):
        return b * NB + jnp.maximum(i - 1, 0)

    return pl.pallas_call(
        _swa_prompt_body,
        grid=(N_PAIRS // n, BATCH, NB),
        in_specs=[
            pl.BlockSpec(memory_space=pltpu.SMEM),
            pl.BlockSpec((WINDOW, n * Q_COLS_PER_STEP), lambda p, b, i: (b * NB + i, p)),
            pl.BlockSpec((WINDOW, n * LANES), lambda p, b, i: (prev(p, b, i), p)),
            pl.BlockSpec((WINDOW, n * LANES), lambda p, b, i: (b * NB + i, p)),
            pl.BlockSpec((WINDOW, n * LANES), lambda p, b, i: (prev(p, b, i), v_col0 + p)),
            pl.BlockSpec((WINDOW, n * LANES), lambda p, b, i: (b * NB + i, v_col0 + p)),
            pl.BlockSpec((n * PAIR_SLOTS, 2 * WINDOW, WINDOW), lambda p, b, i: (p, 0, 0)),
        ],
        out_specs=pl.BlockSpec((WINDOW, n * Q_COLS_PER_STEP), lambda p, b, i: (b * NB + i, p)),
        out_shape=jax.ShapeDtypeStruct((N_PROMPT, Q_DIM), BF16),
        compiler_params=_cparams(("arbitrary", "arbitrary", "arbitrary")),
        name="swa_prompt",
    )(sinks, q, kv, kv, kv, kv, bias_tbl)


S_ROWS = N_KV_HEADS * GROUP * DEC_SEQ


SWA_SEQS_PER_STEP = 4


def _swa_sample_body(q_ref, kn_ref, vn_ref, ck_ref, cv_ref, bias_ref, sink_ref,
                     o_ref, ko_ref, vo_ref):
    slot = lax.broadcasted_iota(jnp.int32, (S_ROWS, KV_COLS), 0) % N_HEADS
    row_kv = slot // PAIR_SLOTS * KV_PER_STEP + slot % KV_PER_STEP
    col_kv = lax.broadcasted_iota(jnp.int32, (S_ROWS, KV_COLS), 1) // HEAD_DIM
    own = row_kv == col_kv
    pad = jnp.zeros((WINDOW - SUBLANES, KV_COLS), F32)
    keep = WINDOW - DEC_SEQ
    for i in range(SWA_SEQS_PER_STEP):
        x = (q_ref[i] * HEAD_DIM ** -0.5).astype(BF16)
        xt = jnp.concatenate([x] * N_KV_HEADS, axis=1)
        qbd = jnp.where(own, xt, jnp.zeros_like(xt))
        kk = jnp.concatenate([ck_ref[i], kn_ref[i], pad], axis=0).astype(BF16)
        vv = jnp.concatenate([cv_ref[i], vn_ref[i], pad], axis=0).astype(BF16)
        s = lax.dot_general(qbd, kk, (((1,), (1,)), ((), ())), preferred_element_type=F32)
        s = s + bias_ref[...]
        p, inv = _softmax_with_sink(s, sink_ref[...])
        of = jnp.dot(p.astype(BF16), vv, preferred_element_type=F32)
        of = jnp.where(own, of, 0.0)
        o = of[:, 0:HEAD_DIM]
        for c in range(1, N_KV_HEADS):
            o = o + of[:, c * HEAD_DIM:(c + 1) * HEAD_DIM]
        o_ref[i] = o * inv
        ko_ref[i, 0:keep, :] = ck_ref[i, DEC_SEQ:WINDOW, :]
        ko_ref[i, keep:WINDOW, :] = kn_ref[i, 0:DEC_SEQ, :]
        vo_ref[i, 0:keep, :] = cv_ref[i, DEC_SEQ:WINDOW, :]
        vo_ref[i, keep:WINDOW, :] = vn_ref[i, 0:DEC_SEQ, :]


def _swa_sample(q_rows, k_new8, v_new8, cache_k, cache_v, bias_s, sink_col):
    n = SWA_SEQS_PER_STEP
    seq3 = lambda s: (s, 0, 0)
    full2 = lambda s: (0, 0)
    return pl.pallas_call(
        _swa_sample_body,
        grid=(DEC_BATCH // n,),
        in_specs=[
            pl.BlockSpec((n, S_ROWS, HEAD_DIM), seq3),
            pl.BlockSpec((n, SUBLANES, KV_COLS), seq3),
            pl.BlockSpec((n, SUBLANES, KV_COLS), seq3),
            pl.BlockSpec((n, WINDOW, KV_COLS), seq3),
            pl.BlockSpec((n, WINDOW, KV_COLS), seq3),
            pl.BlockSpec((S_ROWS, 2 * WINDOW), full2),
            pl.BlockSpec((S_ROWS, 1), full2),
        ],
        out_specs=[
            pl.BlockSpec((n, S_ROWS, HEAD_DIM), seq3),
            pl.BlockSpec((n, WINDOW, KV_COLS), seq3),
            pl.BlockSpec((n, WINDOW, KV_COLS), seq3),
        ],
        out_shape=[
            jax.ShapeDtypeStruct((DEC_BATCH, S_ROWS, HEAD_DIM), F32),
            jax.ShapeDtypeStruct((DEC_BATCH, WINDOW, KV_COLS), F32),
            jax.ShapeDtypeStruct((DEC_BATCH, WINDOW, KV_COLS), F32),
        ],
        compiler_params=_cparams(("parallel",)),
        name="swa_sample",
    )(q_rows, k_new8, v_new8, cache_k, cache_v, bias_s, sink_col)


def _gla_out(o, gate, norm_w):
    return _rms(o, norm_w) * _silu(gate)


def _split3(x):
    hi = x.astype(BF16)
    r = x - hi.astype(F32)
    mid = r.astype(BF16)
    lo = (r - mid.astype(F32)).astype(BF16)
    return hi, mid, lo


def _cumsum_rows(g):
    c = g.shape[0]
    tri = (lax.broadcasted_iota(jnp.int32, (c, c), 0) >= lax.broadcasted_iota(jnp.int32, (c, c), 1)).astype(BF16)
    return jnp.dot(jnp.concatenate([tri] * 3, axis=1), jnp.concatenate(_split3(g), axis=0),
                   preferred_element_type=F32)


def _causal(a):
    c = a.shape[0]
    keep = lax.broadcasted_iota(jnp.int32, (c, c), 0) >= lax.broadcasted_iota(jnp.int32, (c, c), 1)
    return jnp.where(keep, a, 0.0)


_NT = (((1,), (1,)), ((), ()))
_TN = (((0,), (0,)), ((), ()))


def _gla_prompt_body(*refs):
    proj_refs = refs[:BATCH]
    la_refs = refs[BATCH:2 * BATCH]
    nw_ref, o_ref, s_ref, st_ref = refs[2 * BATCH:]
    c = pl.program_id(0)

    @pl.when(c == 0)
    def _():
        st_ref[...] = jnp.zeros_like(st_ref)

    for bi in range(BATCH):
        p_ref = proj_refs[bi]
        b_all = _cumsum_rows(la_refs[bi][...])
        for h in range(GLA_HEADS):
            kc = slice(h * GLA_DK, (h + 1) * GLA_DK)
            v0 = 2 * GLA_KEY_DIM + h * GLA_DV
            b = b_all[:, kc]
            q = p_ref[:, kc].astype(F32)
            k = p_ref[:, GLA_KEY_DIM + h * GLA_DK:GLA_KEY_DIM + (h + 1) * GLA_DK].astype(F32)
            v = p_ref[:, v0:v0 + GLA_DV]
            gate = p_ref[:, v0 + GLA_VAL_DIM:v0 + GLA_VAL_DIM + GLA_DV].astype(F32)
            qe = (q * GLA_DK ** -0.5 * jnp.exp(b)).astype(BF16)
            ke = (k * jnp.exp(-b)).astype(BF16)
            a = _causal(lax.dot_general(qe, ke, _NT, preferred_element_type=F32))
            st = st_ref[bi, h]
            o = (jnp.dot(a.astype(BF16), v, preferred_element_type=F32)
                 + lax.dot_general(qe, st.astype(BF16), _NT, preferred_element_type=F32))
            b_last = b[GLA_C - 1:GLA_C, :]
            kd = (k * jnp.exp(b_last - b)).astype(BF16)
            st_new = st * jnp.exp(b_last) + lax.dot_general(v, kd, _TN, preferred_element_type=F32)
            st_ref[bi, h] = st_new
            o_ref[bi, :, h * GLA_DV:(h + 1) * GLA_DV] = _gla_out(o, gate, nw_ref[...]).astype(o_ref.dtype)

    @pl.when(c == pl.num_programs(0) - 1)
    def _():
        for bi in range(BATCH):
            for h in range(GLA_HEADS):
                s_ref[bi, h] = st_ref[bi, h].T


def _gla_prompt(proj, log_a, norm_w):
    nc = SEQ // GLA_C
    rows = [functools.partial(lambda bi, c: (bi * nc + c, 0), bi) for bi in range(BATCH)]
    o, s = pl.pallas_call(
        _gla_prompt_body,
        grid=(nc,),
        in_specs=([pl.BlockSpec((GLA_C, GLA_MAIN_DIM), r) for r in rows]
                  + [pl.BlockSpec((GLA_C, GLA_KEY_DIM), r) for r in rows]
                  + [pl.BlockSpec((1, GLA_DV), lambda c: (0, 0))]),
        out_specs=[
            pl.BlockSpec((BATCH, GLA_C, GLA_VAL_DIM), lambda c: (0, c, 0)),
            pl.BlockSpec((BATCH, GLA_HEADS, GLA_DK, GLA_DV), lambda c: (0, 0, 0, 0)),
        ],
        out_shape=[
            jax.ShapeDtypeStruct((BATCH, SEQ, GLA_VAL_DIM), BF16),
            jax.ShapeDtypeStruct((BATCH, GLA_HEADS, GLA_DK, GLA_DV), F32),
        ],
        scratch_shapes=[pltpu.VMEM((BATCH, GLA_HEADS, GLA_DV, GLA_DK), F32)],
        compiler_params=_cparams(("arbitrary",)),
        name="gla_prompt",
    )(*([proj] * BATCH + [log_a] * BATCH + [norm_w.reshape(1, GLA_DV)]))
    return o.reshape(N_PROMPT, GLA_VAL_DIM), s


GLA_SEQS_PER_STEP = 4


def _gla_sample_body(proj_ref, la_ref, s0_ref, nw_ref, o_ref, s_ref):
    ones = jnp.ones((DEC_SEQ, LANES), BF16)
    proj = proj_ref[...].astype(F32)
    la = la_ref[...]
    for i in range(GLA_SEQS_PER_STEP):
        rows_i = slice(i * DEC_SEQ, (i + 1) * DEC_SEQ)
        for h in range(GLA_HEADS):
            q = proj[rows_i, h * GLA_DK:(h + 1) * GLA_DK]
            k = proj[rows_i, GLA_KEY_DIM + h * GLA_DK:GLA_KEY_DIM + (h + 1) * GLA_DK]
            v0 = 2 * GLA_KEY_DIM + h * GLA_DV
            v = proj[rows_i, v0:v0 + GLA_DV].astype(BF16)
            gate = proj[rows_i, v0 + GLA_VAL_DIM:v0 + GLA_VAL_DIM + GLA_DV]
            g = la[rows_i, h * GLA_DK:(h + 1) * GLA_DK]
            rows = [g[0:1]]
            for t in range(1, DEC_SEQ):
                rows.append(rows[-1] + g[t:t + 1])
            b = jnp.concatenate(rows, axis=0)
            b_last = rows[-1]
            qe = (q * GLA_DK ** -0.5 * jnp.exp(b)).astype(BF16)
            ke = (k * jnp.exp(-b)).astype(BF16)
            a = _causal(lax.dot_general(qe, ke, _NT, preferred_element_type=F32))
            s0 = s0_ref[i, h]
            o = (jnp.dot(a.astype(BF16), v, preferred_element_type=F32)
                 + jnp.dot(qe, s0.astype(BF16), preferred_element_type=F32))
            kd = (k * jnp.exp(b_last - b)).astype(BF16)
            dsum = sum(lax.dot_general(piece, ones, _TN, preferred_element_type=F32) for piece in _split3(g))
            decay = jnp.concatenate([jnp.exp(dsum)] * (GLA_DV // LANES), axis=1)
            s_ref[i, h] = s0 * decay + lax.dot_general(kd, v, _TN, preferred_element_type=F32)
            o_ref[rows_i, h * GLA_DV:(h + 1) * GLA_DV] = _gla_out(o, gate, nw_ref[...])


def _gla_sample(proj, log_a, state, norm_w):
    n = GLA_SEQS_PER_STEP
    rows = n * DEC_SEQ
    first = N_PROMPT // rows
    seq4 = lambda s: (s, 0, 0, 0)
    return pl.pallas_call(
        _gla_sample_body,
        grid=(DEC_BATCH // n,),
        in_specs=[
            pl.BlockSpec((rows, GLA_MAIN_DIM), lambda s: (first + s, 0)),
            pl.BlockSpec((rows, GLA_KEY_DIM), lambda s: (first + s, 0)),
            pl.BlockSpec((n, GLA_HEADS, GLA_DK, GLA_DV), seq4),
            pl.BlockSpec((1, GLA_DV), lambda s: (0, 0)),
        ],
        out_specs=[
            pl.BlockSpec((rows, GLA_VAL_DIM), lambda s: (s, 0)),
            pl.BlockSpec((n, GLA_HEADS, GLA_DK, GLA_DV), seq4),
        ],
        out_shape=[
            jax.ShapeDtypeStruct((N_SAMPLE, GLA_VAL_DIM), F32),
            jax.ShapeDtypeStruct((DEC_BATCH, GLA_HEADS, GLA_DK, GLA_DV), F32),
        ],
        compiler_params=_cparams(("parallel",)),
        name="gla_sample",
    )(proj, log_a, state, norm_w.reshape(1, GLA_DV))


CAST_ROWS = 256


def _cast_body(w_ref, o_ref):
    o_ref[...] = w_ref[0].astype(BF16)


def _slot_rows_bf16(w, layer):
    def src(o):
        pair, g, hh = o // PAIR_SLOTS, (o // KV_PER_STEP) % GROUP, o % KV_PER_STEP
        return (layer, (pair * KV_PER_STEP + hh) * GROUP + g, 0)

    cols = w.shape[2]
    return pl.pallas_call(
        _cast_body,
        grid=(N_HEADS,),
        in_specs=[pl.BlockSpec((1, HEAD_DIM, cols), src)],
        out_specs=pl.BlockSpec((HEAD_DIM, cols), lambda o: (o, 0)),
        out_shape=jax.ShapeDtypeStruct((Q_DIM, cols), BF16),
        compiler_params=_cparams(("arbitrary",)),
        name="slot_rows_bf16",
    )(w)


def _layer_bf16(w, layer, cols):
    rows = w.shape[1]
    return pl.pallas_call(
        _cast_body,
        grid=(rows // CAST_ROWS,),
        in_specs=[pl.BlockSpec((1, CAST_ROWS, cols), lambda i: (layer, i, 0))],
        out_specs=pl.BlockSpec((CAST_ROWS, cols), lambda i: (i, 0)),
        out_shape=jax.ShapeDtypeStruct((rows, cols), BF16),
        compiler_params=_cparams(("arbitrary",)),
        name="cast_bf16",
    )(w)


def _swa_layer(x, cache_k, cache_v, norm_w, w_qkv, b_qkv, j, w_o_all, b_o, sinks, rel_bias):
    slots = (N_PAIRS, KV_PER_STEP, GROUP, HEAD_DIM)
    w_q = w_qkv[:, :Q_DIM].reshape((D_MODEL,) + slots).transpose(0, 1, 3, 2, 4).reshape(D_MODEL, Q_DIM)
    b_q = b_qkv[:Q_DIM].reshape(slots).transpose(0, 2, 1, 3).reshape(Q_DIM)
    w_qkv_s = jnp.concatenate([w_q, w_qkv[:, Q_DIM:]], axis=1).astype(BF16)
    b_qkv_s = jnp.concatenate([b_q, b_qkv[Q_DIM:]])
    w_o_s = _slot_rows_bf16(w_o_all, j)

    q, kv = _norm_proj(x, norm_w, w_qkv_s, b_qkv_s, ((Q_DIM, BF16), (2 * KV_COLS, F32)), "swa_qkv")
    bias_tbl = _bias_table(rel_bias)
    o_p = _swa_prompt(q, kv, bias_tbl, sinks)

    q_rows = q[N_PROMPT:].astype(F32).reshape(DEC_BATCH, S_ROWS, HEAD_DIM)
    kv_s = kv[N_PROMPT:].reshape(DEC_BATCH, DEC_SEQ, 2 * KV_COLS)
    pad8 = ((0, 0), (0, SUBLANES - DEC_SEQ), (0, 0))
    k_new8 = jnp.pad(kv_s[..., :KV_COLS], pad8)
    v_new8 = jnp.pad(kv_s[..., KV_COLS:], pad8)
    bias_s = bias_tbl[:, :, :DEC_SEQ].transpose(2, 0, 1).reshape(S_ROWS, 2 * WINDOW)
    slot_heads = np.array([int(_slot_head(s)) for s in range(N_HEADS)])
    sink_col = jnp.tile(sinks[slot_heads], DEC_SEQ).reshape(S_ROWS, 1)
    o_s, k_s, v_s = _swa_sample(q_rows, k_new8, v_new8,
                                cache_k.reshape(DEC_BATCH, WINDOW, KV_COLS),
                                cache_v.reshape(DEC_BATCH, WINDOW, KV_COLS), bias_s, sink_col)
    o_s = o_s.reshape(N_SAMPLE, Q_DIM)
    x = _proj_res((o_p, o_s), w_o_s, b_o, x, "swa_out")

    kv_p = jnp.stack([kv[(b + 1) * SEQ - WINDOW:(b + 1) * SEQ] for b in range(BATCH)])
    k_p = kv_p[..., :KV_COLS].reshape(BATCH, WINDOW, N_KV_HEADS, HEAD_DIM)
    v_p = kv_p[..., KV_COLS:].reshape(BATCH, WINDOW, N_KV_HEADS, HEAD_DIM)
    shape_s = (DEC_BATCH, WINDOW, N_KV_HEADS, HEAD_DIM)
    return x, k_p, v_p, k_s.reshape(shape_s), v_s.reshape(shape_s)


def _gla_layer(x, state, norm_w, j, w_in_all, w_gk2, b_gk, gnorm, w_o_all):
    w_main = _layer_bf16(w_in_all, j, GLA_MAIN_DIM)
    w1p = jnp.pad(w_in_all[j, :, GLA_MAIN_DIM:], ((0, 0), (0, LANES - GATE_RANK))).astype(BF16)
    w2p = jnp.pad(w_gk2, ((0, LANES - GATE_RANK), (0, 0))).astype(BF16)
    proj, log_a = _norm_proj(x, norm_w, w_main, jnp.zeros((GLA_MAIN_DIM,), F32), ((GLA_MAIN_DIM, BF16),),
                             "gla_in", gate=(w1p, w2p, b_gk))
    o_p, s_p = _gla_prompt(proj, log_a, gnorm)
    o_s, s_s = _gla_sample(proj, log_a, state, gnorm)
    x = _proj_res((o_p, o_s), _layer_bf16(w_o_all, j, D_MODEL), jnp.zeros((D_MODEL,), F32), x, "gla_out")
    return x, s_p, s_s


def kernel(x_prompt, x_sample, cache_swa_k, cache_swa_v, state_gla, norm_ffn1, ffn1_w_gate, ffn1_w_up,
           ffn1_w_down, norm_mix, norm_ffn2, ffn2_w_gate, ffn2_w_up, ffn2_w_down, norm_final, rel_bias,
           swa_w_qkv, swa_b_qkv, swa_w_o, swa_b_o, swa_sinks, gla_w_in, gla_w_gk2, gla_b_gk, gla_norm,
           gla_w_o):
    x = (x_prompt.reshape(N_PROMPT, D_MODEL), x_sample.reshape(N_SAMPLE, D_MODEL))
    swa_kp, swa_vp, swa_ks, swa_vs, gla_sp, gla_ss = [], [], [], [], [], []
    for i in range(DEPTH):
        x = _ffn_pair(x, i, norm_ffn1[i], ffn1_w_gate, ffn1_w_up, ffn1_w_down)
        j = i // 2
        if i % 2 == 0:
            x, kp, vp, ks, vs = _swa_layer(x, cache_swa_k[j], cache_swa_v[j], norm_mix[i], swa_w_qkv[j],
                                           swa_b_qkv[j], j, swa_w_o, swa_b_o[j], swa_sinks[j], rel_bias)
            swa_kp.append(kp)
            swa_vp.append(vp)
            swa_ks.append(ks)
            swa_vs.append(vs)
        else:
            x, sp, ss = _gla_layer(x, state_gla[j], norm_mix[i], j, gla_w_in, gla_w_gk2[j], gla_b_gk[j],
                                   gla_norm[j], gla_w_o)
            gla_sp.append(sp)
            gla_ss.append(ss)
        final_w = norm_final if i == DEPTH - 1 else None
        x = _ffn_pair(x, i, norm_ffn2[i], ffn2_w_gate, ffn2_w_up, ffn2_w_down, final_w)
    y_prompt = x[0].reshape(BATCH, SEQ, D_MODEL)
    y_sample = x[1].reshape(DEC_BATCH, DEC_SEQ, D_MODEL)
    return (y_prompt, y_sample, jnp.stack(swa_kp), jnp.stack(swa_vp), jnp.stack(swa_ks), jnp.stack(swa_vs),
            jnp.stack(gla_sp), jnp.stack(gla_ss))
```

```python
import functools
import math

import numpy as np
import jax
import jax.numpy as jnp
from jax import lax
from jax.experimental import pallas as pl
from jax.experimental.pallas import tpu as pltpu

F32 = jnp.float32
BF16 = jnp.bfloat16

D_MODEL = 2048
BATCH = 2
SEQ = 4096
DEPTH = 2
DEC_BATCH = 128
DEC_SEQ = 4
RMS_EPS = 1e-6
D_FF = 5632
N_HEADS = 32
N_KV_HEADS = 8
HEAD_DIM = 64
GROUP = N_HEADS // N_KV_HEADS
WINDOW = 128
NUM_BUCKETS = 32
MAX_DISTANCE = 128
NEG_INF = -1e30
GLA_HEADS = 4
GLA_DK = 256
GLA_DV = 512
GLA_KEY_DIM = GLA_HEADS * GLA_DK
GLA_VAL_DIM = GLA_HEADS * GLA_DV
GATE_RANK = 16
GATE_NORMALIZER = 16.0
GLA_MAIN_DIM = 2 * GLA_KEY_DIM + 2 * GLA_VAL_DIM
Q_DIM = N_HEADS * HEAD_DIM
KV_COLS = N_KV_HEADS * HEAD_DIM

N_PROMPT = BATCH * SEQ
N_SAMPLE = DEC_BATCH * DEC_SEQ
N_TOK = N_PROMPT + N_SAMPLE

LANES = 128
SUBLANES = 8
VMEM_LIMIT = 56 * 1024 * 1024

TM = 512
TF = 512
TN = 512
GLA_C = 64
NB = SEQ // WINDOW
N_PROMPT_TILES = N_PROMPT // TM


def _rms(x, w):
    return x * lax.rsqrt(jnp.mean(x * x, axis=-1, keepdims=True) + RMS_EPS) * w


def _silu(x):
    return x * jax.nn.sigmoid(x)


def _cparams(sem):
    return pltpu.CompilerParams(dimension_semantics=sem, vmem_limit_bytes=VMEM_LIMIT)


def _row_specs(split, width):
    if not split:
        return [pl.BlockSpec((TM, width), lambda i, *_: (i, 0))]
    return [pl.BlockSpec((TM, width), lambda i, *_: (jnp.minimum(i, N_PROMPT_TILES - 1), 0)),
            pl.BlockSpec((TM, width), lambda i, *_: (jnp.maximum(i - N_PROMPT_TILES, 0), 0))]


def _on_row_source(fn, *ref_groups):
    if all(len(g) == 1 for g in ref_groups):
        fn(*[g[0] for g in ref_groups])
        return
    i = pl.program_id(0)
    pl.when(i < N_PROMPT_TILES)(lambda: fn(*[g[0] for g in ref_groups]))
    pl.when(i >= N_PROMPT_TILES)(lambda: fn(*[g[-1] for g in ref_groups]))


FFN_TM = 1024
FFN_TF = 256
FFN_TF_SAMPLE = 512
FFN_VMEM_LIMIT = 60 * 1024 * 1024


def _ffn_body(final_norm, emit_w, x_ref, nw_ref, wg_ref, wu_ref, wd_ref, *rest):
    rest = list(rest)
    fw_ref = rest.pop(0) if final_norm else None
    o_ref = rest.pop(0)
    h_ref = rest.pop()
    j = pl.program_id(1)

    @pl.when(j == 0)
    def _():
        x = x_ref[...]
        h_ref[...] = _rms(x, nw_ref[...]).astype(BF16)
        o_ref[...] = x

    if emit_w:
        wgo_ref, wuo_ref, wdo_ref = rest
        wgo_ref[...] = wg_ref[0].astype(BF16)
        wuo_ref[...] = wu_ref[0].astype(BF16)
        wdo_ref[...] = wd_ref[0].astype(BF16)
        wg, wu, wd = wgo_ref[...], wuo_ref[...], wdo_ref[...]
    else:
        wg, wu, wd = wg_ref[...], wu_ref[...], wd_ref[...]
    h = h_ref[...]
    g = jnp.dot(h, wg, preferred_element_type=F32)
    u = jnp.dot(h, wu, preferred_element_type=F32)
    a = (_silu(g) * (0.5 * u)).astype(BF16)
    o_ref[...] += jnp.dot(a, wd, preferred_element_type=F32)

    if final_norm:
        @pl.when(j == pl.num_programs(1) - 1)
        def _():
            o_ref[...] = _rms(o_ref[...], fw_ref[...])


def _ffn(x, layer, nw, wg, wu, wd, final_w=None):
    m = x.shape[0]
    emit_w = wg.dtype == F32
    tm = min(FFN_TM, m)
    tf = FFN_TF if emit_w else FFN_TF_SAMPLE
    nj = D_FF // tf
    final_norm = final_w is not None
    vec = pl.BlockSpec((1, D_MODEL), lambda i, j: (0, 0))
    if emit_w:
        w_specs = [
            pl.BlockSpec((1, D_MODEL, tf), lambda i, j: (layer, 0, j)),
            pl.BlockSpec((1, D_MODEL, tf), lambda i, j: (layer, 0, j)),
            pl.BlockSpec((1, tf, D_MODEL), lambda i, j: (layer, j, 0)),
        ]
    else:
        w_specs = [
            pl.BlockSpec((D_MODEL, tf), lambda i, j: (0, j)),
            pl.BlockSpec((D_MODEL, tf), lambda i, j: (0, j)),
            pl.BlockSpec((tf, D_MODEL), lambda i, j: (j, 0)),
        ]
    in_specs = [pl.BlockSpec((tm, D_MODEL), lambda i, j: (i, 0)), vec] + w_specs
    args = [x, nw.reshape(1, D_MODEL), wg, wu, wd]
    if final_norm:
        in_specs.append(vec)
        args.append(final_w.reshape(1, D_MODEL))
    out_specs = [pl.BlockSpec((tm, D_MODEL), lambda i, j: (i, 0))]
    out_shape = [jax.ShapeDtypeStruct((m, D_MODEL), F32)]
    if emit_w:
        once = lambda i, j: jnp.where(i == 0, j, nj - 1)
        out_specs += [
            pl.BlockSpec((D_MODEL, tf), lambda i, j: (0, once(i, j))),
            pl.BlockSpec((D_MODEL, tf), lambda i, j: (0, once(i, j))),
            pl.BlockSpec((tf, D_MODEL), lambda i, j: (once(i, j), 0)),
        ]
        out_shape += [
            jax.ShapeDtypeStruct((D_MODEL, D_FF), BF16),
            jax.ShapeDtypeStruct((D_MODEL, D_FF), BF16),
            jax.ShapeDtypeStruct((D_FF, D_MODEL), BF16),
        ]
    outs = pl.pallas_call(
        functools.partial(_ffn_body, final_norm, emit_w),
        grid=(m // tm, nj),
        in_specs=in_specs,
        out_specs=out_specs,
        out_shape=out_shape,
        scratch_shapes=[pltpu.VMEM((tm, D_MODEL), BF16)],
        compiler_params=pltpu.CompilerParams(dimension_semantics=("arbitrary", "arbitrary"),
                                             vmem_limit_bytes=FFN_VMEM_LIMIT),
        name="ffn",
    )(*args)
    return outs[0], tuple(outs[1:])


def _ffn_pair(x_pair, layer, nw, wg, wu, wd, final_w=None):
    y_prompt, w_bf16 = _ffn(x_pair[0], layer, nw, wg, wu, wd, final_w)
    y_sample, _ = _ffn(x_pair[1], layer, nw, *w_bf16, final_w)
    return y_prompt, y_sample


def _log_sigmoid(x):
    return jnp.minimum(x, 0.0) - jnp.log1p(jnp.exp(-jnp.abs(x)))


def _norm_proj_body(segments, with_gate, xp_ref, xs_ref, nw_ref, w_ref, b_ref, *rest):
    if with_gate:
        w2_ref, bg_ref = rest[:2]
        rest = rest[2:]
    o_refs = rest[:len(segments)]

    def run(x_ref):
        h = _rms(x_ref[...], nw_ref[...]).astype(BF16)
        col = 0
        for (width, _), o_ref in zip(segments, o_refs):
            for c in range(0, width, TN):
                acc = jnp.dot(h, w_ref[:, col + c:col + c + TN], preferred_element_type=F32)
                o_ref[:, c:c + TN] = (acc + b_ref[:, col + c:col + c + TN]).astype(o_ref.dtype)
            col += width
        if with_gate:
            gk = jnp.dot(h, w_ref[:, col:col + GATE_RANK], preferred_element_type=F32)
            z = jnp.dot(gk.astype(BF16), w2_ref[...], preferred_element_type=F32) + bg_ref[...]
            rest[-1][...] = _log_sigmoid(z) / GATE_NORMALIZER
    _on_row_source(run, (xp_ref, xs_ref))


def _norm_proj(xs, nw, w, b, segments, name, gate=None):
    n = sum(width for width, _ in segments)
    in_specs = _row_specs(True, D_MODEL) + [
        pl.BlockSpec((1, D_MODEL), lambda i: (0, 0)),
        pl.BlockSpec(w.shape, lambda i: (0, 0), pipeline_mode=pl.Buffered(1)),
        pl.BlockSpec((1, n), lambda i: (0, 0)),
    ]
    args = [*xs, nw.reshape(1, D_MODEL), w, b.reshape(1, n)]
    out_specs = [pl.BlockSpec((TM, width), lambda i: (i, 0)) for width, _ in segments]
    out_shape = [jax.ShapeDtypeStruct((N_TOK, width), dtype) for width, dtype in segments]
    if gate is not None:
        w2, bg = gate
        assert w.shape[1] == n + GATE_RANK, w.shape
        in_specs += [
            pl.BlockSpec((GATE_RANK, GLA_KEY_DIM), lambda i: (0, 0)),
            pl.BlockSpec((1, GLA_KEY_DIM), lambda i: (0, 0)),
        ]
        args += [w2, bg.reshape(1, GLA_KEY_DIM)]
        out_specs.append(pl.BlockSpec((TM, GLA_KEY_DIM), lambda i: (i, 0)))
        out_shape.append(jax.ShapeDtypeStruct((N_TOK, GLA_KEY_DIM), F32))
    return pl.pallas_call(
        functools.partial(_norm_proj_body, segments, gate is not None),
        grid=(N_TOK // TM,),
        in_specs=in_specs,
        out_specs=out_specs,
        out_shape=out_shape,
        compiler_params=pltpu.CompilerParams(dimension_semantics=("arbitrary",),
                                             vmem_limit_bytes=FFN_VMEM_LIMIT),
        name=name,
    )(*args)


def _proj_res_body(ap_ref, as_ref, w_ref, b_ref, rp_ref, rs_ref, op_ref, os_ref):
    def run(a_ref, r_ref, o_ref):
        a = a_ref[...].astype(BF16)
        for c in range(0, D_MODEL, TN):
            acc = jnp.dot(a, w_ref[:, c:c + TN], preferred_element_type=F32)
            o_ref[:, c:c + TN] = r_ref[:, c:c + TN] + acc + b_ref[:, c:c + TN]
    _on_row_source(run, (ap_ref, as_ref), (rp_ref, rs_ref), (op_ref, os_ref))


def _proj_res(a_pair, w, b, res_pair, name):
    k = w.shape[0]
    return pl.pallas_call(
        _proj_res_body,
        grid=(N_TOK // TM,),
        in_specs=_row_specs(True, k) + [
            pl.BlockSpec((k, D_MODEL), lambda i: (0, 0), pipeline_mode=pl.Buffered(1)),
            pl.BlockSpec((1, D_MODEL), lambda i: (0, 0)),
        ] + _row_specs(True, D_MODEL),
        out_specs=_row_specs(True, D_MODEL),
        out_shape=[jax.ShapeDtypeStruct((N_PROMPT, D_MODEL), F32), jax.ShapeDtypeStruct((N_SAMPLE, D_MODEL), F32)],
        compiler_params=_cparams(("arbitrary",)),
        name=name,
    )(*a_pair, w, b.reshape(1, D_MODEL), *res_pair)


def _t5_bucket_table():
    i = np.arange(WINDOW)[None, :]
    j = np.arange(2 * WINDOW)[:, None]
    n = np.maximum(WINDOW + i - j, 0)
    max_exact = NUM_BUCKETS // 2
    nf = np.maximum(n, 1).astype(np.float32)
    large = max_exact + (np.log(nf / np.float32(max_exact)) / np.float32(math.log(MAX_DISTANCE / max_exact))
                         * np.float32(NUM_BUCKETS - max_exact)).astype(np.int32)
    large = np.minimum(large, NUM_BUCKETS - 1)
    return np.where(n < max_exact, n, large).astype(np.int32)


KV_PER_STEP = LANES // HEAD_DIM
N_PAIRS = N_KV_HEADS // KV_PER_STEP
Q_COLS_PER_STEP = KV_PER_STEP * GROUP * HEAD_DIM


def _slot_head(slot):
    pair = slot // (GROUP * KV_PER_STEP)
    g = (slot // KV_PER_STEP) % GROUP
    hh = slot % KV_PER_STEP
    return (pair * KV_PER_STEP + hh) * GROUP + g


BIAS_SLOTS_PER_STEP = 8


def _bias_table_body(bucket_ref, rb_ref, o_ref):
    bucket = bucket_ref[...]
    j = lax.broadcasted_iota(jnp.int32, (2 * WINDOW, WINDOW), 0)
    i = lax.broadcasted_iota(jnp.int32, (2 * WINDOW, WINDOW), 1)
    dist = WINDOW + i - j
    in_window = (dist >= 0) & (dist < WINDOW)
    for t in range(BIAS_SLOTS_PER_STEP):
        h = _slot_head(pl.program_id(0) * BIAS_SLOTS_PER_STEP + t)
        acc = jnp.zeros((2 * WINDOW, WINDOW), F32)
        for b in range(NUM_BUCKETS):
            acc = jnp.where(bucket == b, rb_ref[b, h], acc)
        o_ref[t] = jnp.where(in_window, acc, NEG_INF)


def _bias_table(rel_bias):
    n = BIAS_SLOTS_PER_STEP
    return pl.pallas_call(
        _bias_table_body,
        grid=(N_HEADS // n,),
        in_specs=[
            pl.BlockSpec((2 * WINDOW, WINDOW), lambda h: (0, 0)),
            pl.BlockSpec(memory_space=pltpu.SMEM),
        ],
        out_specs=pl.BlockSpec((n, 2 * WINDOW, WINDOW), lambda h: (h, 0, 0)),
        out_shape=jax.ShapeDtypeStruct((N_HEADS, 2 * WINDOW, WINDOW), F32),
        name="bias_table",
    )(jnp.asarray(_t5_bucket_table()), rel_bias)


def _softmax_with_sink(s, sink_col):
    m = jnp.maximum(jnp.max(s, axis=-1, keepdims=True), sink_col)
    p = jnp.exp(s - m)
    denom = jnp.sum(p, axis=-1, keepdims=True) + jnp.exp(sink_col - m)
    return p, 1.0 / denom


PAIR_SLOTS = GROUP * KV_PER_STEP
PAIR_COLS = PAIR_SLOTS * WINDOW
ONES_ROWS = 16


PAIRS_PER_STEP = 4


def _swa_prompt_body(sink_ref, q_ref, kp_ref, ko_ref, vp_ref, vo_ref, bias_ref, o_ref):
    blk = pl.program_id(2)
    head_a = lax.broadcasted_iota(jnp.int32, (WINDOW, LANES), 1) < HEAD_DIM
    no_prev = jnp.where(blk > 0, 0.0, NEG_INF)
    for t in range(PAIRS_PER_STEP):
        pair = pl.program_id(0) * PAIRS_PER_STEP + t
        lanes = slice(t * LANES, (t + 1) * LANES)
        q0 = t * Q_COLS_PER_STEP
        k = jnp.concatenate([kp_ref[:, lanes], ko_ref[:, lanes]], axis=0).astype(BF16)
        v = jnp.concatenate([vp_ref[:, lanes], vo_ref[:, lanes]], axis=0)
        vt = jnp.concatenate([v.T, jnp.ones((ONES_ROWS, 2 * WINDOW), F32)], axis=0).astype(BF16)
        parts = []
        for g in range(GROUP):
            qg = q_ref[:, q0 + g * LANES:q0 + (g + 1) * LANES] * HEAD_DIM ** -0.5
            zero = jnp.zeros_like(qg)
            parts += [jnp.where(head_a, qg, zero), jnp.where(head_a, zero, qg)]
        qbd = jnp.concatenate(parts, axis=0)
        st = lax.dot_general(k, qbd, (((1,), (1,)), ((), ())), preferred_element_type=F32)
        st = st + jnp.concatenate([bias_ref[t * PAIR_SLOTS + u] for u in range(PAIR_SLOTS)], axis=1)
        st = jnp.concatenate([st[:WINDOW] + no_prev, st[WINDOW:]], axis=0)
        sink_row = jnp.concatenate(
            [jnp.full((1, WINDOW), sink_ref[_slot_head(pair * PAIR_SLOTS + slot)], F32)
             for slot in range(PAIR_SLOTS)], axis=1)
        m = jnp.maximum(jnp.max(st, axis=0, keepdims=True), sink_row)
        pt = jnp.exp(st - m).astype(BF16)
        oa = jnp.dot(vt, pt, preferred_element_type=F32)
        inv = 1.0 / (oa[LANES:LANES + 1] + jnp.exp(sink_row - m))
        o = oa[:LANES] * inv
        for g in range(GROUP):
            c = g * KV_PER_STEP * WINDOW
            ot = jnp.concatenate([o[:HEAD_DIM, c:c + WINDOW], o[HEAD_DIM:, c + WINDOW:c + 2 * WINDOW]], axis=0)
            o_ref[:, q0 + g * LANES:q0 + (g + 1) * LANES] = ot.T.astype(o_ref.dtype)


def _swa_prompt(q, kv, bias_tbl, sinks):
    n = PAIRS_PER_STEP
    v_col0 = KV_COLS // (n * LANES)

    def prev(p, b, i):
        return b * NB + jnp.maximum(i - 1, 0)

    return pl.pallas_call(
        _swa_prompt_body,
        grid=(N_PAIRS // n, BATCH, NB),
        in_specs=[
            pl.BlockSpec(memory_space=pltpu.SMEM),
            pl.BlockSpec((WINDOW, n * Q_COLS_PER_STEP), lambda p, b, i: (b * NB + i, p)),
            pl.BlockSpec((WINDOW, n * LANES), lambda p, b, i: (prev(p, b, i), p)),
            pl.BlockSpec((WINDOW, n * LANES), lambda p, b, i: (b * NB + i, p)),
            pl.BlockSpec((WINDOW, n * LANES), lambda p, b, i: (prev(p, b, i), v_col0 + p)),
            pl.BlockSpec((WINDOW, n * LANES), lambda p, b, i: (b * NB + i, v_col0 + p)),
            pl.BlockSpec((n * PAIR_SLOTS, 2 * WINDOW, WINDOW), lambda p, b, i: (p, 0, 0)),
        ],
        out_specs=pl.BlockSpec((WINDOW, n * Q_COLS_PER_STEP), lambda p, b, i: (b * NB + i, p)),
        out_shape=jax.ShapeDtypeStruct((N_PROMPT, Q_DIM), BF16),
        compiler_params=_cparams(("arbitrary", "arbitrary", "arbitrary")),
        name="swa_prompt",
    )(sinks, q, kv, kv, kv, kv, bias_tbl)


S_ROWS = N_KV_HEADS * GROUP * DEC_SEQ


SWA_SEQS_PER_STEP = 4


def _swa_sample_body(q_ref, kn_ref, vn_ref, ck_ref, cv_ref, bias_ref, sink_ref,
                     o_ref, ko_ref, vo_ref):
    slot = lax.broadcasted_iota(jnp.int32, (S_ROWS, KV_COLS), 0) % N_HEADS
    row_kv = slot // PAIR_SLOTS * KV_PER_STEP + slot % KV_PER_STEP
    col_kv = lax.broadcasted_iota(jnp.int32, (S_ROWS, KV_COLS), 1) // HEAD_DIM
    own = row_kv == col_kv
    pad = jnp.zeros((WINDOW - SUBLANES, KV_COLS), F32)
    keep = WINDOW - DEC_SEQ
    for i in range(SWA_SEQS_PER_STEP):
        x = (q_ref[i] * HEAD_DIM ** -0.5).astype(BF16)
        xt = jnp.concatenate([x] * N_KV_HEADS, axis=1)
        qbd = jnp.where(own, xt, jnp.zeros_like(xt))
        kk = jnp.concatenate([ck_ref[i], kn_ref[i], pad], axis=0).astype(BF16)
        vv = jnp.concatenate([cv_ref[i], vn_ref[i], pad], axis=0).astype(BF16)
        s = lax.dot_general(qbd, kk, (((1,), (1,)), ((), ())), preferred_element_type=F32)
        s = s + bias_ref[...]
        p, inv = _softmax_with_sink(s, sink_ref[...])
        of = jnp.dot(p.astype(BF16), vv, preferred_element_type=F32)
        of = jnp.where(own, of, 0.0)
        o = of[:, 0:HEAD_DIM]
        for c in range(1, N_KV_HEADS):
            o = o + of[:, c * HEAD_DIM:(c + 1) * HEAD_DIM]
        o_ref[i] = o * inv
        ko_ref[i, 0:keep, :] = ck_ref[i, DEC_SEQ:WINDOW, :]
        ko_ref[i, keep:WINDOW, :] = kn_ref[i, 0:DEC_SEQ, :]
        vo_ref[i, 0:keep, :] = cv_ref[i, DEC_SEQ:WINDOW, :]
        vo_ref[i, keep:WINDOW, :] = vn_ref[i, 0:DEC_SEQ, :]


def _swa_sample(q_rows, k_new8, v_new8, cache_k, cache_v, bias_s, sink_col):
    n = SWA_SEQS_PER_STEP
    seq3 = lambda s: (s, 0, 0)
    full2 = lambda s: (0, 0)
    return pl.pallas_call(
        _swa_sample_body,
        grid=(DEC_BATCH // n,),
        in_specs=[
            pl.BlockSpec((n, S_ROWS, HEAD_DIM), seq3),
            pl.BlockSpec((n, SUBLANES, KV_COLS), seq3),
            pl.BlockSpec((n, SUBLANES, KV_COLS), seq3),
            pl.BlockSpec((n, WINDOW, KV_COLS), seq3),
            pl.BlockSpec((n, WINDOW, KV_COLS), seq3),
            pl.BlockSpec((S_ROWS, 2 * WINDOW), full2),
            pl.BlockSpec((S_ROWS, 1), full2),
        ],
        out_specs=[
            pl.BlockSpec((n, S_ROWS, HEAD_DIM), seq3),
            pl.BlockSpec((n, WINDOW, KV_COLS), seq3),
            pl.BlockSpec((n, WINDOW, KV_COLS), seq3),
        ],
        out_shape=[
            jax.ShapeDtypeStruct((DEC_BATCH, S_ROWS, HEAD_DIM), F32),
            jax.ShapeDtypeStruct((DEC_BATCH, WINDOW, KV_COLS), F32),
            jax.ShapeDtypeStruct((DEC_BATCH, WINDOW, KV_COLS), F32),
        ],
        compiler_params=_cparams(("parallel",)),
        name="swa_sample",
    )(q_rows, k_new8, v_new8, cache_k, cache_v, bias_s, sink_col)


def _gla_out(o, gate, norm_w):
    return _rms(o, norm_w) * _silu(gate)


def _split3(x):
    hi = x.astype(BF16)
    r = x - hi.astype(F32)
    mid = r.astype(BF16)
    lo = (r - mid.astype(F32)).astype(BF16)
    return hi, mid, lo


def _cumsum_rows(g):
    c = g.shape[0]
    tri = (lax.broadcasted_iota(jnp.int32, (c, c), 0) >= lax.broadcasted_iota(jnp.int32, (c, c), 1)).astype(BF16)
    return jnp.dot(jnp.concatenate([tri] * 3, axis=1), jnp.concatenate(_split3(g), axis=0),
                   preferred_element_type=F32)


def _causal(a):
    c = a.shape[0]
    keep = lax.broadcasted_iota(jnp.int32, (c, c), 0) >= lax.broadcasted_iota(jnp.int32, (c, c), 1)
    return jnp.where(keep, a, 0.0)


_NT = (((1,), (1,)), ((), ()))
_TN = (((0,), (0,)), ((), ()))


def _gla_prompt_body(*refs):
    proj_refs = refs[:BATCH]
    la_refs = refs[BATCH:2 * BATCH]
    nw_ref, o_ref, s_ref, st_ref = refs[2 * BATCH:]
    c = pl.program_id(0)

    @pl.when(c == 0)
    def _():
        st_ref[...] = jnp.zeros_like(st_ref)

    for bi in range(BATCH):
        p_ref = proj_refs[bi]
        b_all = _cumsum_rows(la_refs[bi][...])
        for h in range(GLA_HEADS):
            kc = slice(h * GLA_DK, (h + 1) * GLA_DK)
            v0 = 2 * GLA_KEY_DIM + h * GLA_DV
            b = b_all[:, kc]
            q = p_ref[:, kc].astype(F32)
            k = p_ref[:, GLA_KEY_DIM + h * GLA_DK:GLA_KEY_DIM + (h + 1) * GLA_DK].astype(F32)
            v = p_ref[:, v0:v0 + GLA_DV]
            gate = p_ref[:, v0 + GLA_VAL_DIM:v0 + GLA_VAL_DIM + GLA_DV].astype(F32)
            qe = (q * GLA_DK ** -0.5 * jnp.exp(b)).astype(BF16)
            ke = (k * jnp.exp(-b)).astype(BF16)
            a = _causal(lax.dot_general(qe, ke, _NT, preferred_element_type=F32))
            st = st_ref[bi, h]
            o = (jnp.dot(a.astype(BF16), v, preferred_element_type=F32)
                 + lax.dot_general(qe, st.astype(BF16), _NT, preferred_element_type=F32))
            b_last = b[GLA_C - 1:GLA_C, :]
            kd = (k * jnp.exp(b_last - b)).astype(BF16)
            st_new = st * jnp.exp(b_last) + lax.dot_general(v, kd, _TN, preferred_element_type=F32)
            st_ref[bi, h] = st_new
            o_ref[bi, :, h * GLA_DV:(h + 1) * GLA_DV] = _gla_out(o, gate, nw_ref[...]).astype(o_ref.dtype)

    @pl.when(c == pl.num_programs(0) - 1)
    def _():
        for bi in range(BATCH):
            for h in range(GLA_HEADS):
                s_ref[bi, h] = st_ref[bi, h].T


def _gla_prompt(proj, log_a, norm_w):
    nc = SEQ // GLA_C
    rows = [functools.partial(lambda bi, c: (bi * nc + c, 0), bi) for bi in range(BATCH)]
    o, s = pl.pallas_call(
        _gla_prompt_body,
        grid=(nc,),
        in_specs=([pl.BlockSpec((GLA_C, GLA_MAIN_DIM), r) for r in rows]
                  + [pl.BlockSpec((GLA_C, GLA_KEY_DIM), r) for r in rows]
                  + [pl.BlockSpec((1, GLA_DV), lambda c: (0, 0))]),
        out_specs=[
            pl.BlockSpec((BATCH, GLA_C, GLA_VAL_DIM), lambda c: (0, c, 0)),
            pl.BlockSpec((BATCH, GLA_HEADS, GLA_DK, GLA_DV), lambda c: (0, 0, 0, 0)),
        ],
        out_shape=[
            jax.ShapeDtypeStruct((BATCH, SEQ, GLA_VAL_DIM), BF16),
            jax.ShapeDtypeStruct((BATCH, GLA_HEADS, GLA_DK, GLA_DV), F32),
        ],
        scratch_shapes=[pltpu.VMEM((BATCH, GLA_HEADS, GLA_DV, GLA_DK), F32)],
        compiler_params=_cparams(("arbitrary",)),
        name="gla_prompt",
    )(*([proj] * BATCH + [log_a] * BATCH + [norm_w.reshape(1, GLA_DV)]))
    return o.reshape(N_PROMPT, GLA_VAL_DIM), s


GLA_SEQS_PER_STEP = 4


def _gla_sample_body(proj_ref, la_ref, s0_ref, nw_ref, o_ref, s_ref):
    ones = jnp.ones((DEC_SEQ, LANES), BF16)
    proj = proj_ref[...].astype(F32)
    la = la_ref[...]
    for i in range(GLA_SEQS_PER_STEP):
        rows_i = slice(i * DEC_SEQ, (i + 1) * DEC_SEQ)
        for h in range(GLA_HEADS):
            q = proj[rows_i, h * GLA_DK:(h + 1) * GLA_DK]
            k = proj[rows_i, GLA_KEY_DIM + h * GLA_DK:GLA_KEY_DIM + (h + 1) * GLA_DK]
            v0 = 2 * GLA_KEY_DIM + h * GLA_DV
            v = proj[rows_i, v0:v0 + GLA_DV].astype(BF16)
            gate = proj[rows_i, v0 + GLA_VAL_DIM:v0 + GLA_VAL_DIM + GLA_DV]
            g = la[rows_i, h * GLA_DK:(h + 1) * GLA_DK]
            rows = [g[0:1]]
            for t in range(1, DEC_SEQ):
                rows.append(rows[-1] + g[t:t + 1])
            b = jnp.concatenate(rows, axis=0)
            b_last = rows[-1]
            qe = (q * GLA_DK ** -0.5 * jnp.exp(b)).astype(BF16)
            ke = (k * jnp.exp(-b)).astype(BF16)
            a = _causal(lax.dot_general(qe, ke, _NT, preferred_element_type=F32))
            s0 = s0_ref[i, h]
            o = (jnp.dot(a.astype(BF16), v, preferred_element_type=F32)
                 + jnp.dot(qe, s0.astype(BF16), preferred_element_type=F32))
            kd = (k * jnp.exp(b_last - b)).astype(BF16)
            dsum = sum(lax.dot_general(piece, ones, _TN, preferred_element_type=F32) for piece in _split3(g))
            decay = jnp.concatenate([jnp.exp(dsum)] * (GLA_DV // LANES), axis=1)
            s_ref[i, h] = s0 * decay + lax.dot_general(kd, v, _TN, preferred_element_type=F32)
            o_ref[rows_i, h * GLA_DV:(h + 1) * GLA_DV] = _gla_out(o, gate, nw_ref[...])


def _gla_sample(proj, log_a, state, norm_w):
    n = GLA_SEQS_PER_STEP
    rows = n * DEC_SEQ
    first = N_PROMPT // rows
    seq4 = lambda s: (s, 0, 0, 0)
    return pl.pallas_call(
        _gla_sample_body,
        grid=(DEC_BATCH // n,),
        in_specs=[
            pl.BlockSpec((rows, GLA_MAIN_DIM), lambda s: (first + s, 0)),
            pl.BlockSpec((rows, GLA_KEY_DIM), lambda s: (first + s, 0)),
            pl.BlockSpec((n, GLA_HEADS, GLA_DK, GLA_DV), seq4),
            pl.BlockSpec((1, GLA_DV), lambda s: (0, 0)),
        ],
        out_specs=[
            pl.BlockSpec((rows, GLA_VAL_DIM), lambda s: (s, 0)),
            pl.BlockSpec((n, GLA_HEADS, GLA_DK, GLA_DV), seq4),
        ],
        out_shape=[
            jax.ShapeDtypeStruct((N_SAMPLE, GLA_VAL_DIM), F32),
            jax.ShapeDtypeStruct((DEC_BATCH, GLA_HEADS, GLA_DK, GLA_DV), F32),
        ],
        compiler_params=_cparams(("parallel",)),
        name="gla_sample",
    )(proj, log_a, state, norm_w.reshape(1, GLA_DV))


CAST_ROWS = 256


def _cast_body(w_ref, o_ref):
    o_ref[...] = w_ref[0].astype(BF16)


def _slot_rows_bf16(w, layer):
    def src(o):
        pair, g, hh = o // PAIR_SLOTS, (o // KV_PER_STEP) % GROUP, o % KV_PER_STEP
        return (layer, (pair * KV_PER_STEP + hh) * GROUP + g, 0)

    cols = w.shape[2]
    return pl.pallas_call(
        _cast_body,
        grid=(N_HEADS,),
        in_specs=[pl.BlockSpec((1, HEAD_DIM, cols), src)],
        out_specs=pl.BlockSpec((HEAD_DIM, cols), lambda o: (o, 0)),
        out_shape=jax.ShapeDtypeStruct((Q_DIM, cols), BF16),
        compiler_params=_cparams(("arbitrary",)),
        name="slot_rows_bf16",
    )(w)


def _layer_bf16(w, layer, cols):
    rows = w.shape[1]
    return pl.pallas_call(
        _cast_body,
        grid=(rows // CAST_ROWS,),
        in_specs=[pl.BlockSpec((1, CAST_ROWS, cols), lambda i: (layer, i, 0))],
        out_specs=pl.BlockSpec((CAST_ROWS, cols), lambda i: (i, 0)),
        out_shape=jax.ShapeDtypeStruct((rows, cols), BF16),
        compiler_params=_cparams(("arbitrary",)),
        name="cast_bf16",
    )(w)


def _swa_layer(x, cache_k, cache_v, norm_w, w_qkv, b_qkv, j, w_o_all, b_o, sinks, rel_bias):
    slots = (N_PAIRS, KV_PER_STEP, GROUP, HEAD_DIM)
    w_q = w_qkv[:, :Q_DIM].reshape((D_MODEL,) + slots).transpose(0, 1, 3, 2, 4).reshape(D_MODEL, Q_DIM)
    b_q = b_qkv[:Q_DIM].reshape(slots).transpose(0, 2, 1, 3).reshape(Q_DIM)
    w_qkv_s = jnp.concatenate([w_q, w_qkv[:, Q_DIM:]], axis=1).astype(BF16)
    b_qkv_s = jnp.concatenate([b_q, b_qkv[Q_DIM:]])
    w_o_s = _slot_rows_bf16(w_o_all, j)

    q, kv = _norm_proj(x, norm_w, w_qkv_s, b_qkv_s, ((Q_DIM, BF16), (2 * KV_COLS, F32)), "swa_qkv")
    bias_tbl = _bias_table(rel_bias)
    o_p = _swa_prompt(q, kv, bias_tbl, sinks)

    q_rows = q[N_PROMPT:].astype(F32).reshape(DEC_BATCH, S_ROWS, HEAD_DIM)
    kv_s = kv[N_PROMPT:].reshape(DEC_BATCH, DEC_SEQ, 2 * KV_COLS)
    pad8 = ((0, 0), (0, SUBLANES - DEC_SEQ), (0, 0))
    k_new8 = jnp.pad(kv_s[..., :KV_COLS], pad8)
    v_new8 = jnp.pad(kv_s[..., KV_COLS:], pad8)
    bias_s = bias_tbl[:, :, :DEC_SEQ].transpose(2, 0, 1).reshape(S_ROWS, 2 * WINDOW)
    slot_heads = np.array([int(_slot_head(s)) for s in range(N_HEADS)])
    sink_col = jnp.tile(sinks[slot_heads], DEC_SEQ).reshape(S_ROWS, 1)
    o_s, k_s, v_s = _swa_sample(q_rows, k_new8, v_new8,
                                cache_k.reshape(DEC_BATCH, WINDOW, KV_COLS),
                                cache_v.reshape(DEC_BATCH, WINDOW, KV_COLS), bias_s, sink_col)
    o_s = o_s.reshape(N_SAMPLE, Q_DIM)
    x = _proj_res((o_p, o_s), w_o_s, b_o, x, "swa_out")

    kv_p = jnp.stack([kv[(b + 1) * SEQ - WINDOW:(b + 1) * SEQ] for b in range(BATCH)])
    k_p = kv_p[..., :KV_COLS].reshape(BATCH, WINDOW, N_KV_HEADS, HEAD_DIM)
    v_p = kv_p[..., KV_COLS:].reshape(BATCH, WINDOW, N_KV_HEADS, HEAD_DIM)
    shape_s = (DEC_BATCH, WINDOW, N_KV_HEADS, HEAD_DIM)
    return x, k_p, v_p, k_s.reshape(shape_s), v_s.reshape(shape_s)


def _gla_layer(x, state, norm_w, j, w_in_all, w_gk2, b_gk, gnorm, w_o_all):
    proj, log_a = _norm_proj(x, norm_w, w_in_all[j].astype(BF16), jnp.zeros((GLA_MAIN_DIM,), F32),
                             ((GLA_MAIN_DIM, BF16),), "gla_in", gate=(w_gk2.astype(BF16), b_gk))
    o_p, s_p = _gla_prompt(proj, log_a, gnorm)
    o_s, s_s = _gla_sample(proj, log_a, state, gnorm)
    x = _proj_res((o_p, o_s), _layer_bf16(w_o_all, j, D_MODEL), jnp.zeros((D_MODEL,), F32), x, "gla_out")
    return x, s_p, s_s


def kernel(x_prompt, x_sample, cache_swa_k, cache_swa_v, state_gla, norm_ffn1, ffn1_w_gate, ffn1_w_up,
           ffn1_w_down, norm_mix, norm_ffn2, ffn2_w_gate, ffn2_w_up, ffn2_w_down, norm_final, rel_bias,
           swa_w_qkv, swa_b_qkv, swa_w_o, swa_b_o, swa_sinks, gla_w_in, gla_w_gk2, gla_b_gk, gla_norm,
           gla_w_o):
    x = (x_prompt.reshape(N_PROMPT, D_MODEL), x_sample.reshape(N_SAMPLE, D_MODEL))
    swa_kp, swa_vp, swa_ks, swa_vs, gla_sp, gla_ss = [], [], [], [], [], []
    for i in range(DEPTH):
        x = _ffn_pair(x, i, norm_ffn1[i], ffn1_w_gate, ffn1_w_up, ffn1_w_down)
        j = i // 2
        if i % 2 == 0:
            x, kp, vp, ks, vs = _swa_layer(x, cache_swa_k[j], cache_swa_v[j], norm_mix[i], swa_w_qkv[j],
                                           swa_b_qkv[j], j, swa_w_o, swa_b_o[j], swa_sinks[j], rel_bias)
            swa_kp.append(kp)
            swa_vp.append(vp)
            swa_ks.append(ks)
            swa_vs.append(vs)
        else:
            x, sp, ss = _gla_layer(x, state_gla[j], norm_mix[i], j, gla_w_in, gla_w_gk2[j], gla_b_gk[j],
                                   gla_norm[j], gla_w_o)
            gla_sp.append(sp)
            gla_ss.append(ss)
        final_w = norm_final if i == DEPTH - 1 else None
        x = _ffn_pair(x, i, norm_ffn2[i], ffn2_w_gate, ffn2_w_up, ffn2_w_down, final_w)
    y_prompt = x[0].reshape(BATCH, SEQ, D_MODEL)
    y_sample = x[1].reshape(DEC_BATCH, DEC_SEQ, D_MODEL)
    return (y_prompt, y_sample, jnp.stack(swa_kp), jnp.stack(swa_vp), jnp.stack(swa_ks), jnp.stack(swa_vs),
            jnp.stack(gla_sp), jnp.stack(gla_ss))
```

```python
import functools
import math

import numpy as np
import jax
import jax.numpy as jnp
from jax import lax
from jax.experimental import pallas as pl
from jax.experimental.pallas import tpu as pltpu

F32 = jnp.float32
BF16 = jnp.bfloat16

D_MODEL = 2048
BATCH = 2
SEQ = 4096
DEPTH = 2
DEC_BATCH = 128
DEC_SEQ = 4
RMS_EPS = 1e-6
D_FF = 5632
N_HEADS = 32
N_KV_HEADS = 8
HEAD_DIM = 64
GROUP = N_HEADS // N_KV_HEADS
WINDOW = 128
NUM_BUCKETS = 32
MAX_DISTANCE = 128
NEG_INF = -1e30
GLA_HEADS = 4
GLA_DK = 256
GLA_DV = 512
GLA_KEY_DIM = GLA_HEADS * GLA_DK
GLA_VAL_DIM = GLA_HEADS * GLA_DV
GATE_RANK = 16
GATE_NORMALIZER = 16.0
GLA_MAIN_DIM = 2 * GLA_KEY_DIM + 2 * GLA_VAL_DIM
Q_DIM = N_HEADS * HEAD_DIM
KV_COLS = N_KV_HEADS * HEAD_DIM

N_PROMPT = BATCH * SEQ
N_SAMPLE = DEC_BATCH * DEC_SEQ
N_TOK = N_PROMPT + N_SAMPLE

LANES = 128
SUBLANES = 8
VMEM_LIMIT = 56 * 1024 * 1024

TM = 512
TF = 512
TN = 512
GLA_C = 64
NB = SEQ // WINDOW
N_PROMPT_TILES = N_PROMPT // TM


def _rms(x, w):
    return x * lax.rsqrt(jnp.mean(x * x, axis=-1, keepdims=True) + RMS_EPS) * w


def _silu(x):
    return x * jax.nn.sigmoid(x)


def _cparams(sem):
    return pltpu.CompilerParams(dimension_semantics=sem, vmem_limit_bytes=VMEM_LIMIT)


def _row_specs(split, width):
    if not split:
        return [pl.BlockSpec((TM, width), lambda i, *_: (i, 0))]
    return [pl.BlockSpec((TM, width), lambda i, *_: (jnp.minimum(i, N_PROMPT_TILES - 1), 0)),
            pl.BlockSpec((TM, width), lambda i, *_: (jnp.maximum(i - N_PROMPT_TILES, 0), 0))]


def _on_row_source(fn, *ref_groups):
    if all(len(g) == 1 for g in ref_groups):
        fn(*[g[0] for g in ref_groups])
        return
    i = pl.program_id(0)
    pl.when(i < N_PROMPT_TILES)(lambda: fn(*[g[0] for g in ref_groups]))
    pl.when(i >= N_PROMPT_TILES)(lambda: fn(*[g[-1] for g in ref_groups]))


FFN_TM = 1024
FFN_TF = 256
FFN_TF_SAMPLE = 512
FFN_VMEM_LIMIT = 60 * 1024 * 1024


def _ffn_body(final_norm, emit_w, x_ref, nw_ref, wg_ref, wu_ref, wd_ref, *rest):
    rest = list(rest)
    fw_ref = rest.pop(0) if final_norm else None
    o_ref = rest.pop(0)
    h_ref = rest.pop()
    j = pl.program_id(1)

    @pl.when(j == 0)
    def _():
        x = x_ref[...]
        h_ref[...] = _rms(x, nw_ref[...]).astype(BF16)
        o_ref[...] = x

    if emit_w:
        wgo_ref, wuo_ref, wdo_ref = rest
        wgo_ref[...] = wg_ref[0].astype(BF16)
        wuo_ref[...] = wu_ref[0].astype(BF16)
        wdo_ref[...] = wd_ref[0].astype(BF16)
        wg, wu, wd = wgo_ref[...], wuo_ref[...], wdo_ref[...]
    else:
        wg, wu, wd = wg_ref[...], wu_ref[...], wd_ref[...]
    h = h_ref[...]
    g = jnp.dot(h, wg, preferred_element_type=F32)
    u = jnp.dot(h, wu, preferred_element_type=F32)
    a = (_silu(g) * (0.5 * u)).astype(BF16)
    o_ref[...] += jnp.dot(a, wd, preferred_element_type=F32)

    if final_norm:
        @pl.when(j == pl.num_programs(1) - 1)
        def _():
            o_ref[...] = _rms(o_ref[...], fw_ref[...])


def _ffn(x, layer, nw, wg, wu, wd, final_w=None):
    m = x.shape[0]
    emit_w = wg.dtype == F32
    tm = min(FFN_TM, m)
    tf = FFN_TF if emit_w else FFN_TF_SAMPLE
    nj = D_FF // tf
    final_norm = final_w is not None
    vec = pl.BlockSpec((1, D_MODEL), lambda i, j: (0, 0))
    if emit_w:
        w_specs = [
            pl.BlockSpec((1, D_MODEL, tf), lambda i, j: (layer, 0, j)),
            pl.BlockSpec((1, D_MODEL, tf), lambda i, j: (layer, 0, j)),
            pl.BlockSpec((1, tf, D_MODEL), lambda i, j: (layer, j, 0)),
        ]
    else:
        w_specs = [
            pl.BlockSpec((D_MODEL, tf), lambda i, j: (0, j)),
            pl.BlockSpec((D_MODEL, tf), lambda i, j: (0, j)),
            pl.BlockSpec((tf, D_MODEL), lambda i, j: (j, 0)),
        ]
    in_specs = [pl.BlockSpec((tm, D_MODEL), lambda i, j: (i, 0)), vec] + w_specs
    args = [x, nw.reshape(1, D_MODEL), wg, wu, wd]
    if final_norm:
        in_specs.append(vec)
        args.append(final_w.reshape(1, D_MODEL))
    out_specs = [pl.BlockSpec((tm, D_MODEL), lambda i, j: (i, 0))]
    out_shape = [jax.ShapeDtypeStruct((m, D_MODEL), F32)]
    if emit_w:
        once = lambda i, j: jnp.where(i == 0, j, nj - 1)
        out_specs += [
            pl.BlockSpec((D_MODEL, tf), lambda i, j: (0, once(i, j))),
            pl.BlockSpec((D_MODEL, tf), lambda i, j: (0, once(i, j))),
            pl.BlockSpec((tf, D_MODEL), lambda i, j: (once(i, j), 0)),
        ]
        out_shape += [
            jax.ShapeDtypeStruct((D_MODEL, D_FF), BF16),
            jax.ShapeDtypeStruct((D_MODEL, D_FF), BF16),
            jax.ShapeDtypeStruct((D_FF, D_MODEL), BF16),
        ]
    outs = pl.pallas_call(
        functools.partial(_ffn_body, final_norm, emit_w),
        grid=(m // tm, nj),
        in_specs=in_specs,
        out_specs=out_specs,
        out_shape=out_shape,
        scratch_shapes=[pltpu.VMEM((tm, D_MODEL), BF16)],
        compiler_params=pltpu.CompilerParams(dimension_semantics=("arbitrary", "arbitrary"),
                                             vmem_limit_bytes=FFN_VMEM_LIMIT),
        name="ffn",
    )(*args)
    return outs[0], tuple(outs[1:])


def _ffn_pair(x_pair, layer, nw, wg, wu, wd, final_w=None):
    y_prompt, w_bf16 = _ffn(x_pair[0], layer, nw, wg, wu, wd, final_w)
    y_sample, _ = _ffn(x_pair[1], layer, nw, *w_bf16, final_w)
    return y_prompt, y_sample


def _log_sigmoid(x):
    return jnp.minimum(x, 0.0) - jnp.log1p(jnp.exp(-jnp.abs(x)))


def _norm_proj_body(segments, with_gate, xp_ref, xs_ref, nw_ref, w_ref, b_ref, *rest):
    if with_gate:
        w2_ref, bg_ref = rest[:2]
        rest = rest[2:]
    o_refs = rest[:len(segments)]

    def run(x_ref):
        half = TM // 2
        for rows in (slice(0, half), slice(half, TM)):
            h = _rms(x_ref[rows], nw_ref[...]).astype(BF16)
            if with_gate:
                n_main = sum(width for width, _ in segments)
                gk = jnp.dot(h, w_ref[:, n_main:n_main + GATE_RANK], preferred_element_type=F32)
                z = jnp.dot(gk.astype(BF16), w2_ref[...], preferred_element_type=F32) + bg_ref[...]
                rest[-1][rows] = _log_sigmoid(z) / GATE_NORMALIZER
            col = 0
            for (width, _), o_ref in zip(segments, o_refs):
                for c in range(0, width, TN):
                    acc = jnp.dot(h, w_ref[:, col + c:col + c + TN], preferred_element_type=F32)
                    o_ref[rows, c:c + TN] = (acc + b_ref[:, col + c:col + c + TN]).astype(o_ref.dtype)
                col += width
    _on_row_source(run, (xp_ref, xs_ref))


def _norm_proj(xs, nw, w, b, segments, name, gate=None):
    n = sum(width for width, _ in segments)
    in_specs = _row_specs(True, D_MODEL) + [
        pl.BlockSpec((1, D_MODEL), lambda i: (0, 0)),
        pl.BlockSpec(w.shape, lambda i: (0, 0), pipeline_mode=pl.Buffered(1)),
        pl.BlockSpec((1, n), lambda i: (0, 0)),
    ]
    args = [*xs, nw.reshape(1, D_MODEL), w, b.reshape(1, n)]
    out_specs = [pl.BlockSpec((TM, width), lambda i: (i, 0)) for width, _ in segments]
    out_shape = [jax.ShapeDtypeStruct((N_TOK, width), dtype) for width, dtype in segments]
    if gate is not None:
        w2, bg = gate
        assert w.shape[1] == n + GATE_RANK, w.shape
        in_specs += [
            pl.BlockSpec((GATE_RANK, GLA_KEY_DIM), lambda i: (0, 0)),
            pl.BlockSpec((1, GLA_KEY_DIM), lambda i: (0, 0)),
        ]
        args += [w2, bg.reshape(1, GLA_KEY_DIM)]
        out_specs.append(pl.BlockSpec((TM, GLA_KEY_DIM), lambda i: (i, 0)))
        out_shape.append(jax.ShapeDtypeStruct((N_TOK, GLA_KEY_DIM), F32))
    return pl.pallas_call(
        functools.partial(_norm_proj_body, segments, gate is not None),
        grid=(N_TOK // TM,),
        in_specs=in_specs,
        out_specs=out_specs,
        out_shape=out_shape,
        compiler_params=pltpu.CompilerParams(dimension_semantics=("arbitrary",),
                                             vmem_limit_bytes=FFN_VMEM_LIMIT),
        name=name,
    )(*args)


def _proj_res_body(ap_ref, as_ref, w_ref, b_ref, rp_ref, rs_ref, op_ref, os_ref):
    def run(a_ref, r_ref, o_ref):
        a = a_ref[...].astype(BF16)
        for c in range(0, D_MODEL, TN):
            acc = jnp.dot(a, w_ref[:, c:c + TN], preferred_element_type=F32)
            o_ref[:, c:c + TN] = r_ref[:, c:c + TN] + acc + b_ref[:, c:c + TN]
    _on_row_source(run, (ap_ref, as_ref), (rp_ref, rs_ref), (op_ref, os_ref))


def _proj_res(a_pair, w, b, res_pair, name):
    k = w.shape[0]
    return pl.pallas_call(
        _proj_res_body,
        grid=(N_TOK // TM,),
        in_specs=_row_specs(True, k) + [
            pl.BlockSpec((k, D_MODEL), lambda i: (0, 0), pipeline_mode=pl.Buffered(1)),
            pl.BlockSpec((1, D_MODEL), lambda i: (0, 0)),
        ] + _row_specs(True, D_MODEL),
        out_specs=_row_specs(True, D_MODEL),
        out_shape=[jax.ShapeDtypeStruct((N_PROMPT, D_MODEL), F32), jax.ShapeDtypeStruct((N_SAMPLE, D_MODEL), F32)],
        compiler_params=_cparams(("arbitrary",)),
        name=name,
    )(*a_pair, w, b.reshape(1, D_MODEL), *res_pair)


def _t5_bucket_table():
    i = np.arange(WINDOW)[None, :]
    j = np.arange(2 * WINDOW)[:, None]
    n = np.maximum(WINDOW + i - j, 0)
    max_exact = NUM_BUCKETS // 2
    nf = np.maximum(n, 1).astype(np.float32)
    large = max_exact + (np.log(nf / np.float32(max_exact)) / np.float32(math.log(MAX_DISTANCE / max_exact))
                         * np.float32(NUM_BUCKETS - max_exact)).astype(np.int32)
    large = np.minimum(large, NUM_BUCKETS - 1)
    return np.where(n < max_exact, n, large).astype(np.int32)


KV_PER_STEP = LANES // HEAD_DIM
N_PAIRS = N_KV_HEADS // KV_PER_STEP
Q_COLS_PER_STEP = KV_PER_STEP * GROUP * HEAD_DIM


def _slot_head(slot):
    pair = slot // (GROUP * KV_PER_STEP)
    g = (slot // KV_PER_STEP) % GROUP
    hh = slot % KV_PER_STEP
    return (pair * KV_PER_STEP + hh) * GROUP + g


BIAS_SLOTS_PER_STEP = 8


def _bias_table_body(bucket_ref, rb_ref, o_ref):
    bucket = bucket_ref[...]
    j = lax.broadcasted_iota(jnp.int32, (2 * WINDOW, WINDOW), 0)
    i = lax.broadcasted_iota(jnp.int32, (2 * WINDOW, WINDOW), 1)
    dist = WINDOW + i - j
    in_window = (dist >= 0) & (dist < WINDOW)
    for t in range(BIAS_SLOTS_PER_STEP):
        h = _slot_head(pl.program_id(0) * BIAS_SLOTS_PER_STEP + t)
        acc = jnp.zeros((2 * WINDOW, WINDOW), F32)
        for b in range(NUM_BUCKETS):
            acc = jnp.where(bucket == b, rb_ref[b, h], acc)
        o_ref[t] = jnp.where(in_window, acc, NEG_INF)


def _bias_table(rel_bias):
    n = BIAS_SLOTS_PER_STEP
    return pl.pallas_call(
        _bias_table_body,
        grid=(N_HEADS // n,),
        in_specs=[
            pl.BlockSpec((2 * WINDOW, WINDOW), lambda h: (0, 0)),
            pl.BlockSpec(memory_space=pltpu.SMEM),
        ],
        out_specs=pl.BlockSpec((n, 2 * WINDOW, WINDOW), lambda h: (h, 0, 0)),
        out_shape=jax.ShapeDtypeStruct((N_HEADS, 2 * WINDOW, WINDOW), F32),
        name="bias_table",
    )(jnp.asarray(_t5_bucket_table()), rel_bias)


def _softmax_with_sink(s, sink_col):
    m = jnp.maximum(jnp.max(s, axis=-1, keepdims=True), sink_col)
    p = jnp.exp(s - m)
    denom = jnp.sum(p, axis=-1, keepdims=True) + jnp.exp(sink_col - m)
    return p, 1.0 / denom


PAIR_SLOTS = GROUP * KV_PER_STEP
PAIR_COLS = PAIR_SLOTS * WINDOW
ONES_ROWS = 16


PAIRS_PER_STEP = 4


def _swa_prompt_body(sink_ref, q_ref, kp_ref, ko_ref, vp_ref, vo_ref, bias_ref, o_ref):
    blk = pl.program_id(2)
    head_a = lax.broadcasted_iota(jnp.int32, (WINDOW, LANES), 1) < HEAD_DIM
    no_prev = jnp.where(blk > 0, 0.0, NEG_INF)
    for t in range(PAIRS_PER_STEP):
        pair = pl.program_id(0) * PAIRS_PER_STEP + t
        lanes = slice(t * LANES, (t + 1) * LANES)
        q0 = t * Q_COLS_PER_STEP
        k = jnp.concatenate([kp_ref[:, lanes], ko_ref[:, lanes]], axis=0).astype(BF16)
        v = jnp.concatenate([vp_ref[:, lanes], vo_ref[:, lanes]], axis=0)
        vt = jnp.concatenate([v.T, jnp.ones((ONES_ROWS, 2 * WINDOW), F32)], axis=0).astype(BF16)
        parts = []
        for g in range(GROUP):
            qg = q_ref[:, q0 + g * LANES:q0 + (g + 1) * LANES] * HEAD_DIM ** -0.5
            zero = jnp.zeros_like(qg)
            parts += [jnp.where(head_a, qg, zero), jnp.where(head_a, zero, qg)]
        qbd = jnp.concatenate(parts, axis=0)
        st = lax.dot_general(k, qbd, (((1,), (1,)), ((), ())), preferred_element_type=F32)
        st = st + jnp.concatenate([bias_ref[t * PAIR_SLOTS + u] for u in range(PAIR_SLOTS)], axis=1)
        st = jnp.concatenate([st[:WINDOW] + no_prev, st[WINDOW:]], axis=0)
        sink_row = jnp.concatenate(
            [jnp.full((1, WINDOW), sink_ref[_slot_head(pair * PAIR_SLOTS + slot)], F32)
             for slot in range(PAIR_SLOTS)], axis=1)
        m = jnp.maximum(jnp.max(st, axis=0, keepdims=True), sink_row)
        pt = jnp.exp(st - m).astype(BF16)
        oa = jnp.dot(vt, pt, preferred_element_type=F32)
        inv = 1.0 / (oa[LANES:LANES + 1] + jnp.exp(sink_row - m))
        o = oa[:LANES] * inv
        for g in range(GROUP):
            c = g * KV_PER_STEP * WINDOW
            ot = jnp.concatenate([o[:HEAD_DIM, c:c + WINDOW], o[HEAD_DIM:, c + WINDOW:c + 2 * WINDOW]], axis=0)
            o_ref[:, q0 + g * LANES:q0 + (g + 1) * LANES] = ot.T.astype(o_ref.dtype)


def _swa_prompt(q, kv, bias_tbl, sinks):
    n = PAIRS_PER_STEP
    v_col0 = KV_COLS // (n * LANES)

    def prev(p, b, i):
        return b * NB + jnp.maximum(i - 1, 0)

    return pl.pallas_call(
        _swa_prompt_body,
        grid=(N_PAIRS // n, BATCH, NB),
        in_specs=[
            pl.BlockSpec(memory_space=pltpu.SMEM),
            pl.BlockSpec((WINDOW, n * Q_COLS_PER_STEP), lambda p, b, i: (b * NB + i, p)),
            pl.BlockSpec((WINDOW, n * LANES), lambda p, b, i: (prev(p, b, i), p)),
            pl.BlockSpec((WINDOW, n * LANES), lambda p, b, i: (b * NB + i, p)),
            pl.BlockSpec((WINDOW, n * LANES), lambda p, b, i: (prev(p, b, i), v_col0 + p)),
            pl.BlockSpec((WINDOW, n * LANES), lambda p, b, i: (b * NB + i, v_col0 + p)),
            pl.BlockSpec((n * PAIR_SLOTS, 2 * WINDOW, WINDOW), lambda p, b, i: (p, 0, 0)),
        ],
        out_specs=pl.BlockSpec((WINDOW, n * Q_COLS_PER_STEP), lambda p, b, i: (b * NB + i, p)),
        out_shape=jax.ShapeDtypeStruct((N_PROMPT, Q_DIM), BF16),
        compiler_params=_cparams(("arbitrary", "arbitrary", "arbitrary")),
        name="swa_prompt",
    )(sinks, q, kv, kv, kv, kv, bias_tbl)


S_ROWS = N_KV_HEADS * GROUP * DEC_SEQ


SWA_SEQS_PER_STEP = 4


def _swa_sample_body(q_ref, kn_ref, vn_ref, ck_ref, cv_ref, bias_ref, sink_ref,
                     o_ref, ko_ref, vo_ref):
    slot = lax.broadcasted_iota(jnp.int32, (S_ROWS, KV_COLS), 0) % N_HEADS
    row_kv = slot // PAIR_SLOTS * KV_PER_STEP + slot % KV_PER_STEP
    col_kv = lax.broadcasted_iota(jnp.int32, (S_ROWS, KV_COLS), 1) // HEAD_DIM
    own = row_kv == col_kv
    pad = jnp.zeros((WINDOW - SUBLANES, KV_COLS), F32)
    keep = WINDOW - DEC_SEQ
    for i in range(SWA_SEQS_PER_STEP):
        x = (q_ref[i] * HEAD_DIM ** -0.5).astype(BF16)
        xt = jnp.concatenate([x] * N_KV_HEADS, axis=1)
        qbd = jnp.where(own, xt, jnp.zeros_like(xt))
        kk = jnp.concatenate([ck_ref[i], kn_ref[i], pad], axis=0).astype(BF16)
        vv = jnp.concatenate([cv_ref[i], vn_ref[i], pad], axis=0).astype(BF16)
        s = lax.dot_general(qbd, kk, (((1,), (1,)), ((), ())), preferred_element_type=F32)
        s = s + bias_ref[...]
        p, inv = _softmax_with_sink(s, sink_ref[...])
        of = jnp.dot(p.astype(BF16), vv, preferred_element_type=F32)
        of = jnp.where(own, of, 0.0)
        o = of[:, 0:HEAD_DIM]
        for c in range(1, N_KV_HEADS):
            o = o + of[:, c * HEAD_DIM:(c + 1) * HEAD_DIM]
        o_ref[i] = o * inv
        ko_ref[i, 0:keep, :] = ck_ref[i, DEC_SEQ:WINDOW, :]
        ko_ref[i, keep:WINDOW, :] = kn_ref[i, 0:DEC_SEQ, :]
        vo_ref[i, 0:keep, :] = cv_ref[i, DEC_SEQ:WINDOW, :]
        vo_ref[i, keep:WINDOW, :] = vn_ref[i, 0:DEC_SEQ, :]


def _swa_sample(q_rows, k_new8, v_new8, cache_k, cache_v, bias_s, sink_col):
    n = SWA_SEQS_PER_STEP
    seq3 = lambda s: (s, 0, 0)
    full2 = lambda s: (0, 0)
    return pl.pallas_call(
        _swa_sample_body,
        grid=(DEC_BATCH // n,),
        in_specs=[
            pl.BlockSpec((n, S_ROWS, HEAD_DIM), seq3),
            pl.BlockSpec((n, SUBLANES, KV_COLS), seq3),
            pl.BlockSpec((n, SUBLANES, KV_COLS), seq3),
            pl.BlockSpec((n, WINDOW, KV_COLS), seq3),
            pl.BlockSpec((n, WINDOW, KV_COLS), seq3),
            pl.BlockSpec((S_ROWS, 2 * WINDOW), full2),
            pl.BlockSpec((S_ROWS, 1), full2),
        ],
        out_specs=[
            pl.BlockSpec((n, S_ROWS, HEAD_DIM), seq3),
            pl.BlockSpec((n, WINDOW, KV_COLS), seq3),
            pl.BlockSpec((n, WINDOW, KV_COLS), seq3),
        ],
        out_shape=[
            jax.ShapeDtypeStruct((DEC_BATCH, S_ROWS, HEAD_DIM), F32),
            jax.ShapeDtypeStruct((DEC_BATCH, WINDOW, KV_COLS), F32),
            jax.ShapeDtypeStruct((DEC_BATCH, WINDOW, KV_COLS), F32),
        ],
        compiler_params=_cparams(("parallel",)),
        name="swa_sample",
    )(q_rows, k_new8, v_new8, cache_k, cache_v, bias_s, sink_col)


def _gla_out(o, gate, norm_w):
    return _rms(o, norm_w) * _silu(gate)


def _split3(x):
    hi = x.astype(BF16)
    r = x - hi.astype(F32)
    mid = r.astype(BF16)
    lo = (r - mid.astype(F32)).astype(BF16)
    return hi, mid, lo


def _cumsum_rows(g):
    c = g.shape[0]
    tri = (lax.broadcasted_iota(jnp.int32, (c, c), 0) >= lax.broadcasted_iota(jnp.int32, (c, c), 1)).astype(BF16)
    return jnp.dot(jnp.concatenate([tri] * 3, axis=1), jnp.concatenate(_split3(g), axis=0),
                   preferred_element_type=F32)


def _causal(a):
    c = a.shape[0]
    keep = lax.broadcasted_iota(jnp.int32, (c, c), 0) >= lax.broadcasted_iota(jnp.int32, (c, c), 1)
    return jnp.where(keep, a, 0.0)


_NT = (((1,), (1,)), ((), ()))
_TN = (((0,), (0,)), ((), ()))


def _gla_prompt_body(*refs):
    proj_refs = refs[:BATCH]
    la_refs = refs[BATCH:2 * BATCH]
    nw_ref, o_ref, s_ref, st_ref = refs[2 * BATCH:]
    c = pl.program_id(0)

    @pl.when(c == 0)
    def _():
        st_ref[...] = jnp.zeros_like(st_ref)

    for bi in range(BATCH):
        p_ref = proj_refs[bi]
        b_all = _cumsum_rows(la_refs[bi][...])
        for h in range(GLA_HEADS):
            kc = slice(h * GLA_DK, (h + 1) * GLA_DK)
            v0 = 2 * GLA_KEY_DIM + h * GLA_DV
            b = b_all[:, kc]
            q = p_ref[:, kc].astype(F32)
            k = p_ref[:, GLA_KEY_DIM + h * GLA_DK:GLA_KEY_DIM + (h + 1) * GLA_DK].astype(F32)
            v = p_ref[:, v0:v0 + GLA_DV]
            gate = p_ref[:, v0 + GLA_VAL_DIM:v0 + GLA_VAL_DIM + GLA_DV].astype(F32)
            qe = (q * GLA_DK ** -0.5 * jnp.exp(b)).astype(BF16)
            ke = (k * jnp.exp(-b)).astype(BF16)
            a = _causal(lax.dot_general(qe, ke, _NT, preferred_element_type=F32))
            st = st_ref[bi, h]
            o = (jnp.dot(a.astype(BF16), v, preferred_element_type=F32)
                 + lax.dot_general(qe, st.astype(BF16), _NT, preferred_element_type=F32))
            b_last = b[GLA_C - 1:GLA_C, :]
            kd = (k * jnp.exp(b_last - b)).astype(BF16)
            st_new = st * jnp.exp(b_last) + lax.dot_general(v, kd, _TN, preferred_element_type=F32)
            st_ref[bi, h] = st_new
            o_ref[bi, :, h * GLA_DV:(h + 1) * GLA_DV] = _gla_out(o, gate, nw_ref[...]).astype(o_ref.dtype)

    @pl.when(c == pl.num_programs(0) - 1)
    def _():
        for bi in range(BATCH):
            for h in range(GLA_HEADS):
                s_ref[bi, h] = st_ref[bi, h].T


def _gla_prompt(proj, log_a, norm_w):
    nc = SEQ // GLA_C
    rows = [functools.partial(lambda bi, c: (bi * nc + c, 0), bi) for bi in range(BATCH)]
    o, s = pl.pallas_call(
        _gla_prompt_body,
        grid=(nc,),
        in_specs=([pl.BlockSpec((GLA_C, GLA_MAIN_DIM), r) for r in rows]
                  + [pl.BlockSpec((GLA_C, GLA_KEY_DIM), r) for r in rows]
                  + [pl.BlockSpec((1, GLA_DV), lambda c: (0, 0))]),
        out_specs=[
            pl.BlockSpec((BATCH, GLA_C, GLA_VAL_DIM), lambda c: (0, c, 0)),
            pl.BlockSpec((BATCH, GLA_HEADS, GLA_DK, GLA_DV), lambda c: (0, 0, 0, 0)),
        ],
        out_shape=[
            jax.ShapeDtypeStruct((BATCH, SEQ, GLA_VAL_DIM), BF16),
            jax.ShapeDtypeStruct((BATCH, GLA_HEADS, GLA_DK, GLA_DV), F32),
        ],
        scratch_shapes=[pltpu.VMEM((BATCH, GLA_HEADS, GLA_DV, GLA_DK), F32)],
        compiler_params=_cparams(("arbitrary",)),
        name="gla_prompt",
    )(*([proj] * BATCH + [log_a] * BATCH + [norm_w.reshape(1, GLA_DV)]))
    return o.reshape(N_PROMPT, GLA_VAL_DIM), s


GLA_SEQS_PER_STEP = 4


def _gla_sample_body(proj_ref, la_ref, s0_ref, nw_ref, o_ref, s_ref):
    ones = jnp.ones((DEC_SEQ, LANES), BF16)
    proj = proj_ref[...].astype(F32)
    la = la_ref[...]
    for i in range(GLA_SEQS_PER_STEP):
        rows_i = slice(i * DEC_SEQ, (i + 1) * DEC_SEQ)
        for h in range(GLA_HEADS):
            q = proj[rows_i, h * GLA_DK:(h + 1) * GLA_DK]
            k = proj[rows_i, GLA_KEY_DIM + h * GLA_DK:GLA_KEY_DIM + (h + 1) * GLA_DK]
            v0 = 2 * GLA_KEY_DIM + h * GLA_DV
            v = proj[rows_i, v0:v0 + GLA_DV].astype(BF16)
            gate = proj[rows_i, v0 + GLA_VAL_DIM:v0 + GLA_VAL_DIM + GLA_DV]
            g = la[rows_i, h * GLA_DK:(h + 1) * GLA_DK]
            rows = [g[0:1]]
            for t in range(1, DEC_SEQ):
                rows.append(rows[-1] + g[t:t + 1])
            b = jnp.concatenate(rows, axis=0)
            b_last = rows[-1]
            qe = (q * GLA_DK ** -0.5 * jnp.exp(b)).astype(BF16)
            ke = (k * jnp.exp(-b)).astype(BF16)
            a = _causal(lax.dot_general(qe, ke, _NT, preferred_element_type=F32))
            s0 = s0_ref[i, h]
            o = (jnp.dot(a.astype(BF16), v, preferred_element_type=F32)
                 + jnp.dot(qe, s0.astype(BF16), preferred_element_type=F32))
            kd = (k * jnp.exp(b_last - b)).astype(BF16)
            dsum = sum(lax.dot_general(piece, ones, _TN, preferred_element_type=F32) for piece in _split3(g))
            decay = jnp.concatenate([jnp.exp(dsum)] * (GLA_DV // LANES), axis=1)
            s_ref[i, h] = s0 * decay + lax.dot_general(kd, v, _TN, preferred_element_type=F32)
            o_ref[rows_i, h * GLA_DV:(h + 1) * GLA_DV] = _gla_out(o, gate, nw_ref[...])


def _gla_sample(proj, log_a, state, norm_w):
    n = GLA_SEQS_PER_STEP
    rows = n * DEC_SEQ
    first = N_PROMPT // rows
    seq4 = lambda s: (s, 0, 0, 0)
    return pl.pallas_call(
        _gla_sample_body,
        grid=(DEC_BATCH // n,),
        in_specs=[
            pl.BlockSpec((rows, GLA_MAIN_DIM), lambda s: (first + s, 0)),
            pl.BlockSpec((rows, GLA_KEY_DIM), lambda s: (first + s, 0)),
            pl.BlockSpec((n, GLA_HEADS, GLA_DK, GLA_DV), seq4),
            pl.BlockSpec((1, GLA_DV), lambda s: (0, 0)),
        ],
        out_specs=[
            pl.BlockSpec((rows, GLA_VAL_DIM), lambda s: (s, 0)),
            pl.BlockSpec((n, GLA_HEADS, GLA_DK, GLA_DV), seq4),
        ],
        out_shape=[
            jax.ShapeDtypeStruct((N_SAMPLE, GLA_VAL_DIM), F32),
            jax.ShapeDtypeStruct((DEC_BATCH, GLA_HEADS, GLA_DK, GLA_DV), F32),
        ],
        compiler_params=_cparams(("parallel",)),
        name="gla_sample",
    )(proj, log_a, state, norm_w.reshape(1, GLA_DV))


CAST_ROWS = 256


def _cast_body(w_ref, o_ref):
    o_ref[...] = w_ref[0].astype(BF16)


def _slot_rows_body(*refs):
    o_ref = refs[-1]
    for u, w_ref in enumerate(refs[:-1]):
        o_ref[u * HEAD_DIM:(u + 1) * HEAD_DIM] = w_ref[0].astype(BF16)


def _slot_rows_bf16(w, layer):
    def src(u):
        g, hh = u // KV_PER_STEP, u % KV_PER_STEP
        return lambda pair: (layer, (pair * KV_PER_STEP + hh) * GROUP + g, 0)

    cols = w.shape[2]
    return pl.pallas_call(
        _slot_rows_body,
        grid=(N_PAIRS,),
        in_specs=[pl.BlockSpec((1, HEAD_DIM, cols), src(u)) for u in range(PAIR_SLOTS)],
        out_specs=pl.BlockSpec((PAIR_SLOTS * HEAD_DIM, cols), lambda pair: (pair, 0)),
        out_shape=jax.ShapeDtypeStruct((Q_DIM, cols), BF16),
        compiler_params=_cparams(("arbitrary",)),
        name="slot_rows_bf16",
    )(*([w] * PAIR_SLOTS))


def _layer_bf16(w, layer, cols):
    rows = w.shape[1]
    return pl.pallas_call(
        _cast_body,
        grid=(rows // CAST_ROWS,),
        in_specs=[pl.BlockSpec((1, CAST_ROWS, cols), lambda i: (layer, i, 0))],
        out_specs=pl.BlockSpec((CAST_ROWS, cols), lambda i: (i, 0)),
        out_shape=jax.ShapeDtypeStruct((rows, cols), BF16),
        compiler_params=_cparams(("arbitrary",)),
        name="cast_bf16",
    )(w)


def _swa_layer(x, cache_k, cache_v, norm_w, w_qkv, b_qkv, j, w_o_all, b_o, sinks, rel_bias):
    slots = (N_PAIRS, KV_PER_STEP, GROUP, HEAD_DIM)
    w_q = w_qkv[:, :Q_DIM].reshape((D_MODEL,) + slots).transpose(0, 1, 3, 2, 4).reshape(D_MODEL, Q_DIM)
    b_q = b_qkv[:Q_DIM].reshape(slots).transpose(0, 2, 1, 3).reshape(Q_DIM)
    w_qkv_s = jnp.concatenate([w_q, w_qkv[:, Q_DIM:]], axis=1).astype(BF16)
    b_qkv_s = jnp.concatenate([b_q, b_qkv[Q_DIM:]])
    w_o_s = _slot_rows_bf16(w_o_all, j)

    q, kv = _norm_proj(x, norm_w, w_qkv_s, b_qkv_s, ((Q_DIM, BF16), (2 * KV_COLS, F32)), "swa_qkv")
    bias_tbl = _bias_table(rel_bias)
    o_p = _swa_prompt(q, kv, bias_tbl, sinks)

    q_rows = q[N_PROMPT:].astype(F32).reshape(DEC_BATCH, S_ROWS, HEAD_DIM)
    kv_s = kv[N_PROMPT:].reshape(DEC_BATCH, DEC_SEQ, 2 * KV_COLS)
    pad8 = ((0, 0), (0, SUBLANES - DEC_SEQ), (0, 0))
    k_new8 = jnp.pad(kv_s[..., :KV_COLS], pad8)
    v_new8 = jnp.pad(kv_s[..., KV_COLS:], pad8)
    bias_s = bias_tbl[:, :, :DEC_SEQ].transpose(2, 0, 1).reshape(S_ROWS, 2 * WINDOW)
    slot_heads = np.array([int(_slot_head(s)) for s in range(N_HEADS)])
    sink_col = jnp.tile(sinks[slot_heads], DEC_SEQ).reshape(S_ROWS, 1)
    o_s, k_s, v_s = _swa_sample(q_rows, k_new8, v_new8,
                                cache_k.reshape(DEC_BATCH, WINDOW, KV_COLS),
                                cache_v.reshape(DEC_BATCH, WINDOW, KV_COLS), bias_s, sink_col)
    o_s = o_s.reshape(N_SAMPLE, Q_DIM)
    x = _proj_res((o_p, o_s), w_o_s, b_o, x, "swa_out")

    kv_p = jnp.stack([kv[(b + 1) * SEQ - WINDOW:(b + 1) * SEQ] for b in range(BATCH)])
    k_p = kv_p[..., :KV_COLS].reshape(BATCH, WINDOW, N_KV_HEADS, HEAD_DIM)
    v_p = kv_p[..., KV_COLS:].reshape(BATCH, WINDOW, N_KV_HEADS, HEAD_DIM)
    shape_s = (DEC_BATCH, WINDOW, N_KV_HEADS, HEAD_DIM)
    return x, k_p, v_p, k_s.reshape(shape_s), v_s.reshape(shape_s)


def _gla_layer(x, state, norm_w, j, w_in_all, w_gk2, b_gk, gnorm, w_o_all):
    proj, log_a = _norm_proj(x, norm_w, w_in_all[j].astype(BF16), jnp.zeros((GLA_MAIN_DIM,), F32),
                             ((GLA_MAIN_DIM, BF16),), "gla_in", gate=(w_gk2.astype(BF16), b_gk))
    o_p, s_p = _gla_prompt(proj, log_a, gnorm)
    o_s, s_s = _gla_sample(proj, log_a, state, gnorm)
    x = _proj_res((o_p, o_s), _layer_bf16(w_o_all, j, D_MODEL), jnp.zeros((D_MODEL,), F32), x, "gla_out")
    return x, s_p, s_s


def kernel(x_prompt, x_sample, cache_swa_k, cache_swa_v, state_gla, norm_ffn1, ffn1_w_gate, ffn1_w_up,
           ffn1_w_down, norm_mix, norm_ffn2, ffn2_w_gate, ffn2_w_up, ffn2_w_down, norm_final, rel_bias,
           swa_w_qkv, swa_b_qkv, swa_w_o, swa_b_o, swa_sinks, gla_w_in, gla_w_gk2, gla_b_gk, gla_norm,
           gla_w_o):
    x = (x_prompt.reshape(N_PROMPT, D_MODEL), x_sample.reshape(N_SAMPLE, D_MODEL))
    swa_kp, swa_vp, swa_ks, swa_vs, gla_sp, gla_ss = [], [], [], [], [], []
    for i in range(DEPTH):
        x = _ffn_pair(x, i, norm_ffn1[i], ffn1_w_gate, ffn1_w_up, ffn1_w_down)
        j = i // 2
        if i % 2 == 0:
            x, kp, vp, ks, vs = _swa_layer(x, cache_swa_k[j], cache_swa_v[j], norm_mix[i], swa_w_qkv[j],
                                           swa_b_qkv[j], j, swa_w_o, swa_b_o[j], swa_sinks[j], rel_bias)
            swa_kp.append(kp)
            swa_vp.append(vp)
            swa_ks.append(ks)
            swa_vs.append(vs)
        else:
            x, sp, ss = _gla_layer(x, state_gla[j], norm_mix[i], j, gla_w_in, gla_w_gk2[j], gla_b_gk[j],
                                   gla_norm[j], gla_w_o)
            gla_sp.append(sp)
            gla_ss.append(ss)
        final_w = norm_final if i == DEPTH - 1 else None
        x = _ffn_pair(x, i, norm_ffn2[i], ffn2_w_gate, ffn2_w_up, ffn2_w_down, final_w)
    y_prompt = x[0].reshape(BATCH, SEQ, D_MODEL)
    y_sample = x[1].reshape(DEC_BATCH, DEC_SEQ, D_MODEL)
    return (y_prompt, y_sample, jnp.stack(swa_kp), jnp.stack(swa_vp), jnp.stack(swa_ks), jnp.stack(swa_vs),
            jnp.stack(gla_sp), jnp.stack(gla_ss))
```

```python
import functools
import math

import numpy as np
import jax
import jax.numpy as jnp
from jax import lax
from jax.experimental import pallas as pl
from jax.experimental.pallas import tpu as pltpu

F32 = jnp.float32
BF16 = jnp.bfloat16

D_MODEL = 2048
BATCH = 2
SEQ = 4096
DEPTH = 2
DEC_BATCH = 128
DEC_SEQ = 4
RMS_EPS = 1e-6
D_FF = 5632
N_HEADS = 32
N_KV_HEADS = 8
HEAD_DIM = 64
GROUP = N_HEADS // N_KV_HEADS
WINDOW = 128
NUM_BUCKETS = 32
MAX_DISTANCE = 128
NEG_INF = -1e30
GLA_HEADS = 4
GLA_DK = 256
GLA_DV = 512
GLA_KEY_DIM = GLA_HEADS * GLA_DK
GLA_VAL_DIM = GLA_HEADS * GLA_DV
GATE_RANK = 16
GATE_NORMALIZER = 16.0
GLA_MAIN_DIM = 2 * GLA_KEY_DIM + 2 * GLA_VAL_DIM
Q_DIM = N_HEADS * HEAD_DIM
KV_COLS = N_KV_HEADS * HEAD_DIM

N_PROMPT = BATCH * SEQ
N_SAMPLE = DEC_BATCH * DEC_SEQ
N_TOK = N_PROMPT + N_SAMPLE

LANES = 128
SUBLANES = 8
VMEM_LIMIT = 56 * 1024 * 1024
BIG_VMEM_LIMIT = 60 * 1024 * 1024

TM = 512
TN = 512
GLA_C = 64
NB = SEQ // WINDOW
N_PROMPT_TILES = N_PROMPT // TM


def _rms(x, w):
    return x * lax.rsqrt(jnp.mean(x * x, axis=-1, keepdims=True) + RMS_EPS) * w


def _silu(x):
    return x * jax.nn.sigmoid(x)


def _cparams(sem, vmem_limit=VMEM_LIMIT):
    return pltpu.CompilerParams(dimension_semantics=sem, vmem_limit_bytes=vmem_limit)


def _row_pair_specs(width):
    return [pl.BlockSpec((TM, width), lambda i: (jnp.minimum(i, N_PROMPT_TILES - 1), 0)),
            pl.BlockSpec((TM, width), lambda i: (jnp.maximum(i - N_PROMPT_TILES, 0), 0))]


def _on_row_source(fn, *ref_pairs):
    i = pl.program_id(0)
    pl.when(i < N_PROMPT_TILES)(lambda: fn(*[p[0] for p in ref_pairs]))
    pl.when(i >= N_PROMPT_TILES)(lambda: fn(*[p[1] for p in ref_pairs]))


FFN_TM = 1024
FFN_TF = 256
FFN_TF_SAMPLE = 512


def _ffn_body(final_norm, emit_w, x_ref, nw_ref, wg_ref, wu_ref, wd_ref, *rest):
    rest = list(rest)
    fw_ref = rest.pop(0) if final_norm else None
    o_ref = rest.pop(0)
    h_ref = rest.pop()
    j = pl.program_id(1)

    @pl.when(j == 0)
    def _():
        x = x_ref[...]
        h_ref[...] = _rms(x, nw_ref[...]).astype(BF16)
        o_ref[...] = x

    if emit_w:
        wgo_ref, wuo_ref, wdo_ref = rest
        wgo_ref[...] = wg_ref[0].astype(BF16)
        wuo_ref[...] = wu_ref[0].astype(BF16)
        wdo_ref[...] = wd_ref[0].astype(BF16)
        wg, wu, wd = wgo_ref[...], wuo_ref[...], wdo_ref[...]
    else:
        wg, wu, wd = wg_ref[...], wu_ref[...], wd_ref[...]
    h = h_ref[...]
    g = jnp.dot(h, wg, preferred_element_type=F32)
    u = jnp.dot(h, wu, preferred_element_type=F32)
    a = (_silu(g) * (0.5 * u)).astype(BF16)
    o_ref[...] += jnp.dot(a, wd, preferred_element_type=F32)

    if final_norm:
        @pl.when(j == pl.num_programs(1) - 1)
        def _():
            o_ref[...] = _rms(o_ref[...], fw_ref[...])


def _ffn(x, layer, nw, wg, wu, wd, final_w=None):
    m = x.shape[0]
    emit_w = wg.dtype == F32
    tm = min(FFN_TM, m)
    tf = FFN_TF if emit_w else FFN_TF_SAMPLE
    nj = D_FF // tf
    final_norm = final_w is not None
    vec = pl.BlockSpec((1, D_MODEL), lambda i, j: (0, 0))
    if emit_w:
        w_specs = [
            pl.BlockSpec((1, D_MODEL, tf), lambda i, j: (layer, 0, j)),
            pl.BlockSpec((1, D_MODEL, tf), lambda i, j: (layer, 0, j)),
            pl.BlockSpec((1, tf, D_MODEL), lambda i, j: (layer, j, 0)),
        ]
    else:
        w_specs = [
            pl.BlockSpec((D_MODEL, tf), lambda i, j: (0, j)),
            pl.BlockSpec((D_MODEL, tf), lambda i, j: (0, j)),
            pl.BlockSpec((tf, D_MODEL), lambda i, j: (j, 0)),
        ]
    in_specs = [pl.BlockSpec((tm, D_MODEL), lambda i, j: (i, 0)), vec] + w_specs
    args = [x, nw.reshape(1, D_MODEL), wg, wu, wd]
    if final_norm:
        in_specs.append(vec)
        args.append(final_w.reshape(1, D_MODEL))
    out_specs = [pl.BlockSpec((tm, D_MODEL), lambda i, j: (i, 0))]
    out_shape = [jax.ShapeDtypeStruct((m, D_MODEL), F32)]
    if emit_w:
        once = lambda i, j: jnp.where(i == 0, j, nj - 1)
        out_specs += [
            pl.BlockSpec((D_MODEL, tf), lambda i, j: (0, once(i, j))),
            pl.BlockSpec((D_MODEL, tf), lambda i, j: (0, once(i, j))),
            pl.BlockSpec((tf, D_MODEL), lambda i, j: (once(i, j), 0)),
        ]
        out_shape += [
            jax.ShapeDtypeStruct((D_MODEL, D_FF), BF16),
            jax.ShapeDtypeStruct((D_MODEL, D_FF), BF16),
            jax.ShapeDtypeStruct((D_FF, D_MODEL), BF16),
        ]
    outs = pl.pallas_call(
        functools.partial(_ffn_body, final_norm, emit_w),
        grid=(m // tm, nj),
        in_specs=in_specs,
        out_specs=out_specs,
        out_shape=out_shape,
        scratch_shapes=[pltpu.VMEM((tm, D_MODEL), BF16)],
        compiler_params=_cparams(("arbitrary", "arbitrary"), BIG_VMEM_LIMIT),
        name="ffn",
    )(*args)
    return outs[0], tuple(outs[1:])


def _ffn_pair(x_pair, layer, nw, wg, wu, wd, final_w=None):
    y_prompt, w_bf16 = _ffn(x_pair[0], layer, nw, wg, wu, wd, final_w)
    y_sample, _ = _ffn(x_pair[1], layer, nw, *w_bf16, final_w)
    return y_prompt, y_sample


def _log_sigmoid(x):
    return jnp.minimum(x, 0.0) - jnp.log1p(jnp.exp(-jnp.abs(x)))


def _norm_proj_body(segments, with_gate, xp_ref, xs_ref, nw_ref, w_ref, b_ref, *rest):
    if with_gate:
        w2_ref, bg_ref = rest[:2]
        rest = rest[2:]
    o_refs = rest[:len(segments)]

    def run(x_ref):
        h = _rms(x_ref[...], nw_ref[...]).astype(BF16)
        col = 0
        for (width, _), o_ref in zip(segments, o_refs):
            for c in range(0, width, TN):
                acc = jnp.dot(h, w_ref[:, col + c:col + c + TN], preferred_element_type=F32)
                o_ref[:, c:c + TN] = (acc + b_ref[:, col + c:col + c + TN]).astype(o_ref.dtype)
            col += width
        if with_gate:
            gk = jnp.dot(h, w_ref[:, col:col + GATE_RANK], preferred_element_type=F32)
            z = jnp.dot(gk.astype(BF16), w2_ref[...], preferred_element_type=F32) + bg_ref[...]
            rest[-1][...] = _log_sigmoid(z) / GATE_NORMALIZER
    _on_row_source(run, (xp_ref, xs_ref))


def _norm_proj(xs, nw, w, b, segments, name, gate=None):
    n = sum(width for width, _ in segments)
    in_specs = _row_pair_specs(D_MODEL) + [
        pl.BlockSpec((1, D_MODEL), lambda i: (0, 0)),
        pl.BlockSpec(w.shape, lambda i: (0, 0), pipeline_mode=pl.Buffered(1)),
        pl.BlockSpec((1, n), lambda i: (0, 0)),
    ]
    args = [*xs, nw.reshape(1, D_MODEL), w, b.reshape(1, n)]
    out_specs = [pl.BlockSpec((TM, width), lambda i: (i, 0)) for width, _ in segments]
    out_shape = [jax.ShapeDtypeStruct((N_TOK, width), dtype) for width, dtype in segments]
    if gate is not None:
        w2, bg = gate
        assert w.shape[1] == n + GATE_RANK, w.shape
        in_specs += [
            pl.BlockSpec((GATE_RANK, GLA_KEY_DIM), lambda i: (0, 0)),
            pl.BlockSpec((1, GLA_KEY_DIM), lambda i: (0, 0)),
        ]
        args += [w2, bg.reshape(1, GLA_KEY_DIM)]
        out_specs.append(pl.BlockSpec((TM, GLA_KEY_DIM), lambda i: (i, 0)))
        out_shape.append(jax.ShapeDtypeStruct((N_TOK, GLA_KEY_DIM), F32))
    return pl.pallas_call(
        functools.partial(_norm_proj_body, segments, gate is not None),
        grid=(N_TOK // TM,),
        in_specs=in_specs,
        out_specs=out_specs,
        out_shape=out_shape,
        compiler_params=_cparams(("arbitrary",), BIG_VMEM_LIMIT),
        name=name,
    )(*args)


def _proj_res_body(ap_ref, as_ref, w_ref, b_ref, rp_ref, rs_ref, op_ref, os_ref):
    def run(a_ref, r_ref, o_ref):
        a = a_ref[...].astype(BF16)
        for c in range(0, D_MODEL, TN):
            acc = jnp.dot(a, w_ref[:, c:c + TN], preferred_element_type=F32)
            o_ref[:, c:c + TN] = r_ref[:, c:c + TN] + acc + b_ref[:, c:c + TN]
    _on_row_source(run, (ap_ref, as_ref), (rp_ref, rs_ref), (op_ref, os_ref))


def _proj_res(a_pair, w, b, res_pair, name):
    k = w.shape[0]
    return pl.pallas_call(
        _proj_res_body,
        grid=(N_TOK // TM,),
        in_specs=_row_pair_specs(k) + [
            pl.BlockSpec((k, D_MODEL), lambda i: (0, 0), pipeline_mode=pl.Buffered(1)),
            pl.BlockSpec((1, D_MODEL), lambda i: (0, 0)),
        ] + _row_pair_specs(D_MODEL),
        out_specs=_row_pair_specs(D_MODEL),
        out_shape=[jax.ShapeDtypeStruct((N_PROMPT, D_MODEL), F32), jax.ShapeDtypeStruct((N_SAMPLE, D_MODEL), F32)],
        compiler_params=_cparams(("arbitrary",)),
        name=name,
    )(*a_pair, w, b.reshape(1, D_MODEL), *res_pair)


def _t5_bucket_table():
    i = np.arange(WINDOW)[None, :]
    j = np.arange(2 * WINDOW)[:, None]
    n = np.maximum(WINDOW + i - j, 0)
    max_exact = NUM_BUCKETS // 2
    nf = np.maximum(n, 1).astype(np.float32)
    large = max_exact + (np.log(nf / np.float32(max_exact)) / np.float32(math.log(MAX_DISTANCE / max_exact))
                         * np.float32(NUM_BUCKETS - max_exact)).astype(np.int32)
    large = np.minimum(large, NUM_BUCKETS - 1)
    return np.where(n < max_exact, n, large).astype(np.int32)


KV_PER_STEP = LANES // HEAD_DIM
N_PAIRS = N_KV_HEADS // KV_PER_STEP
Q_COLS_PER_STEP = KV_PER_STEP * GROUP * HEAD_DIM
PAIR_SLOTS = GROUP * KV_PER_STEP


def _slot_head(slot):
    pair = slot // PAIR_SLOTS
    g = (slot // KV_PER_STEP) % GROUP
    hh = slot % KV_PER_STEP
    return (pair * KV_PER_STEP + hh) * GROUP + g


BIAS_SLOTS_PER_STEP = 8


def _bias_table_body(bucket_ref, rb_ref, o_ref):
    bucket = bucket_ref[...]
    j = lax.broadcasted_iota(jnp.int32, (2 * WINDOW, WINDOW), 0)
    i = lax.broadcasted_iota(jnp.int32, (2 * WINDOW, WINDOW), 1)
    dist = WINDOW + i - j
    in_window = (dist >= 0) & (dist < WINDOW)
    for t in range(BIAS_SLOTS_PER_STEP):
        h = _slot_head(pl.program_id(0) * BIAS_SLOTS_PER_STEP + t)
        acc = jnp.zeros((2 * WINDOW, WINDOW), F32)
        for b in range(NUM_BUCKETS):
            acc = jnp.where(bucket == b, rb_ref[b, h], acc)
        o_ref[t] = jnp.where(in_window, acc, NEG_INF)


def _bias_table(rel_bias):
    n = BIAS_SLOTS_PER_STEP
    return pl.pallas_call(
        _bias_table_body,
        grid=(N_HEADS // n,),
        in_specs=[
            pl.BlockSpec((2 * WINDOW, WINDOW), lambda h: (0, 0)),
            pl.BlockSpec(memory_space=pltpu.SMEM),
        ],
        out_specs=pl.BlockSpec((n, 2 * WINDOW, WINDOW), lambda h: (h, 0, 0)),
        out_shape=jax.ShapeDtypeStruct((N_HEADS, 2 * WINDOW, WINDOW), F32),
        name="bias_table",
    )(jnp.asarray(_t5_bucket_table()), rel_bias)


def _softmax_with_sink(s, sink_col):
    m = jnp.maximum(jnp.max(s, axis=-1, keepdims=True), sink_col)
    p = jnp.exp(s - m)
    denom = jnp.sum(p, axis=-1, keepdims=True) + jnp.exp(sink_col - m)
    return p, 1.0 / denom


PAIR_COLS = PAIR_SLOTS * WINDOW
ONES_ROWS = 16
PAIRS_PER_STEP = 4


def _swa_prompt_body(sink_ref, q_ref, kp_ref, ko_ref, vp_ref, vo_ref, bias_ref, o_ref):
    blk = pl.program_id(2)
    head_a = lax.broadcasted_iota(jnp.int32, (WINDOW, LANES), 1) < HEAD_DIM
    no_prev = jnp.where(blk > 0, 0.0, NEG_INF)
    for t in range(PAIRS_PER_STEP):
        pair = pl.program_id(0) * PAIRS_PER_STEP + t
        lanes = slice(t * LANES, (t + 1) * LANES)
        q0 = t * Q_COLS_PER_STEP
        k = jnp.concatenate([kp_ref[:, lanes], ko_ref[:, lanes]], axis=0).astype(BF16)
        v = jnp.concatenate([vp_ref[:, lanes], vo_ref[:, lanes]], axis=0)
        vt = jnp.concatenate([v.T, jnp.ones((ONES_ROWS, 2 * WINDOW), F32)], axis=0).astype(BF16)
        parts = []
        for g in range(GROUP):
            qg = q_ref[:, q0 + g * LANES:q0 + (g + 1) * LANES] * HEAD_DIM ** -0.5
            zero = jnp.zeros_like(qg)
            parts += [jnp.where(head_a, qg, zero), jnp.where(head_a, zero, qg)]
        qbd = jnp.concatenate(parts, axis=0)
        st = lax.dot_general(k, qbd, (((1,), (1,)), ((), ())), preferred_element_type=F32)
        st = st + jnp.concatenate([bias_ref[t * PAIR_SLOTS + u] for u in range(PAIR_SLOTS)], axis=1)
        st = jnp.concatenate([st[:WINDOW] + no_prev, st[WINDOW:]], axis=0)
        sink_row = jnp.concatenate(
            [jnp.full((1, WINDOW), sink_ref[_slot_head(pair * PAIR_SLOTS + slot)], F32)
             for slot in range(PAIR_SLOTS)], axis=1)
        m = jnp.maximum(jnp.max(st, axis=0, keepdims=True), sink_row)
        pt = jnp.exp(st - m).astype(BF16)
        oa = jnp.dot(vt, pt, preferred_element_type=F32)
        inv = 1.0 / (oa[LANES:LANES + 1] + jnp.exp(sink_row - m))
        o = oa[:LANES] * inv
        for g in range(GROUP):
            c = g * KV_PER_STEP * WINDOW
            ot = jnp.concatenate([o[:HEAD_DIM, c:c + WINDOW], o[HEAD_DIM:, c + WINDOW:c + 2 * WINDOW]], axis=0)
            o_ref[:, q0 + g * LANES:q0 + (g + 1) * LANES] = ot.T.astype(o_ref.dtype)


def _swa_prompt(q, kv, bias_tbl, sinks):
    n = PAIRS_PER_STEP
    v_col0 = KV_COLS // (n * LANES)

    def prev(p, b, i):
        return b * NB + jnp.maximum(i - 1, 0)

    return pl.pallas_call(
        _swa_prompt_body,
        grid=(N_PAIRS // n, BATCH, NB),
        in_specs=[
            pl.BlockSpec(memory_space=pltpu.SMEM),
            pl.BlockSpec((WINDOW, n * Q_COLS_PER_STEP), lambda p, b, i: (b * NB + i, p)),
            pl.BlockSpec((WINDOW, n * LANES), lambda p, b, i: (prev(p, b, i), p)),
            pl.BlockSpec((WINDOW, n * LANES), lambda p, b, i: (b * NB + i, p)),
            pl.BlockSpec((WINDOW, n * LANES), lambda p, b, i: (prev(p, b, i), v_col0 + p)),
            pl.BlockSpec((WINDOW, n * LANES), lambda p, b, i: (b * NB + i, v_col0 + p)),
            pl.BlockSpec((n * PAIR_SLOTS, 2 * WINDOW, WINDOW), lambda p, b, i: (p, 0, 0)),
        ],
        out_specs=pl.BlockSpec((WINDOW, n * Q_COLS_PER_STEP), lambda p, b, i: (b * NB + i, p)),
        out_shape=jax.ShapeDtypeStruct((N_PROMPT, Q_DIM), BF16),
        compiler_params=_cparams(("arbitrary", "arbitrary", "arbitrary")),
        name="swa_prompt",
    )(sinks, q, kv, kv, kv, kv, bias_tbl)


S_ROWS = N_HEADS * DEC_SEQ
SWA_SEQS_PER_STEP = 4


def _swa_sample_body(q_ref, kn_ref, vn_ref, ck_ref, cv_ref, bias_ref, sink_ref,
                     o_ref, ko_ref, vo_ref):
    slot = lax.broadcasted_iota(jnp.int32, (S_ROWS, KV_COLS), 0) % N_HEADS
    row_kv = slot // PAIR_SLOTS * KV_PER_STEP + slot % KV_PER_STEP
    col_kv = lax.broadcasted_iota(jnp.int32, (S_ROWS, KV_COLS), 1) // HEAD_DIM
    own = row_kv == col_kv
    pad = jnp.zeros((WINDOW - SUBLANES, KV_COLS), F32)
    keep = WINDOW - DEC_SEQ
    for i in range(SWA_SEQS_PER_STEP):
        x = (q_ref[i] * HEAD_DIM ** -0.5).astype(BF16)
        xt = jnp.concatenate([x] * N_KV_HEADS, axis=1)
        qbd = jnp.where(own, xt, jnp.zeros_like(xt))
        kk = jnp.concatenate([ck_ref[i], kn_ref[i], pad], axis=0).astype(BF16)
        vv = jnp.concatenate([cv_ref[i], vn_ref[i], pad], axis=0).astype(BF16)
        s = lax.dot_general(qbd, kk, (((1,), (1,)), ((), ())), preferred_element_type=F32)
        s = s + bias_ref[...]
        p, inv = _softmax_with_sink(s, sink_ref[...])
        of = jnp.dot(p.astype(BF16), vv, preferred_element_type=F32)
        of = jnp.where(own, of, 0.0)
        o = of[:, 0:HEAD_DIM]
        for c in range(1, N_KV_HEADS):
            o = o + of[:, c * HEAD_DIM:(c + 1) * HEAD_DIM]
        o_ref[i] = o * inv
        ko_ref[i, 0:keep, :] = ck_ref[i, DEC_SEQ:WINDOW, :]
        ko_ref[i, keep:WINDOW, :] = kn_ref[i, 0:DEC_SEQ, :]
        vo_ref[i, 0:keep, :] = cv_ref[i, DEC_SEQ:WINDOW, :]
        vo_ref[i, keep:WINDOW, :] = vn_ref[i, 0:DEC_SEQ, :]


def _swa_sample(q_rows, k_new8, v_new8, cache_k, cache_v, bias_s, sink_col):
    n = SWA_SEQS_PER_STEP
    seq3 = lambda s: (s, 0, 0)
    full2 = lambda s: (0, 0)
    return pl.pallas_call(
        _swa_sample_body,
        grid=(DEC_BATCH // n,),
        in_specs=[
            pl.BlockSpec((n, S_ROWS, HEAD_DIM), seq3),
            pl.BlockSpec((n, SUBLANES, KV_COLS), seq3),
            pl.BlockSpec((n, SUBLANES, KV_COLS), seq3),
            pl.BlockSpec((n, WINDOW, KV_COLS), seq3),
            pl.BlockSpec((n, WINDOW, KV_COLS), seq3),
            pl.BlockSpec((S_ROWS, 2 * WINDOW), full2),
            pl.BlockSpec((S_ROWS, 1), full2),
        ],
        out_specs=[
            pl.BlockSpec((n, S_ROWS, HEAD_DIM), seq3),
            pl.BlockSpec((n, WINDOW, KV_COLS), seq3),
            pl.BlockSpec((n, WINDOW, KV_COLS), seq3),
        ],
        out_shape=[
            jax.ShapeDtypeStruct((DEC_BATCH, S_ROWS, HEAD_DIM), F32),
            jax.ShapeDtypeStruct((DEC_BATCH, WINDOW, KV_COLS), F32),
            jax.ShapeDtypeStruct((DEC_BATCH, WINDOW, KV_COLS), F32),
        ],
        compiler_params=_cparams(("arbitrary",)),
        name="swa_sample",
    )(q_rows, k_new8, v_new8, cache_k, cache_v, bias_s, sink_col)


def _gla_out(o, gate, norm_w):
    return _rms(o, norm_w) * _silu(gate)


def _split3(x):
    hi = x.astype(BF16)
    r = x - hi.astype(F32)
    mid = r.astype(BF16)
    lo = (r - mid.astype(F32)).astype(BF16)
    return hi, mid, lo


def _cumsum_rows(g):
    c = g.shape[0]
    tri = (lax.broadcasted_iota(jnp.int32, (c, c), 0) >= lax.broadcasted_iota(jnp.int32, (c, c), 1)).astype(BF16)
    return jnp.dot(jnp.concatenate([tri] * 3, axis=1), jnp.concatenate(_split3(g), axis=0),
                   preferred_element_type=F32)


def _causal(a):
    c = a.shape[0]
    keep = lax.broadcasted_iota(jnp.int32, (c, c), 0) >= lax.broadcasted_iota(jnp.int32, (c, c), 1)
    return jnp.where(keep, a, 0.0)


_NT = (((1,), (1,)), ((), ()))
_TN = (((0,), (0,)), ((), ()))


def _gla_prompt_body(*refs):
    proj_refs = refs[:BATCH]
    la_refs = refs[BATCH:2 * BATCH]
    nw_ref, o_ref, s_ref, st_ref = refs[2 * BATCH:]
    c = pl.program_id(0)

    @pl.when(c == 0)
    def _():
        st_ref[...] = jnp.zeros_like(st_ref)

    for bi in range(BATCH):
        p_ref = proj_refs[bi]
        b_all = _cumsum_rows(la_refs[bi][...])
        for h in range(GLA_HEADS):
            kc = slice(h * GLA_DK, (h + 1) * GLA_DK)
            v0 = 2 * GLA_KEY_DIM + h * GLA_DV
            b = b_all[:, kc]
            q = p_ref[:, kc].astype(F32)
            k = p_ref[:, GLA_KEY_DIM + h * GLA_DK:GLA_KEY_DIM + (h + 1) * GLA_DK].astype(F32)
            v = p_ref[:, v0:v0 + GLA_DV]
            gate = p_ref[:, v0 + GLA_VAL_DIM:v0 + GLA_VAL_DIM + GLA_DV].astype(F32)
            qe = (q * GLA_DK ** -0.5 * jnp.exp(b)).astype(BF16)
            ke = (k * jnp.exp(-b)).astype(BF16)
            a = _causal(lax.dot_general(qe, ke, _NT, preferred_element_type=F32))
            st = st_ref[bi, h]
            o = (jnp.dot(a.astype(BF16), v, preferred_element_type=F32)
                 + lax.dot_general(qe, st.astype(BF16), _NT, preferred_element_type=F32))
            b_last = b[GLA_C - 1:GLA_C, :]
            kd = (k * jnp.exp(b_last - b)).astype(BF16)
            st_new = st * jnp.exp(b_last) + lax.dot_general(v, kd, _TN, preferred_element_type=F32)
            st_ref[bi, h] = st_new
            o_ref[bi, :, h * GLA_DV:(h + 1) * GLA_DV] = _gla_out(o, gate, nw_ref[...]).astype(o_ref.dtype)

    @pl.when(c == pl.num_programs(0) - 1)
    def _():
        for bi in range(BATCH):
            for h in range(GLA_HEADS):
                s_ref[bi, h] = st_ref[bi, h].T


def _gla_prompt(proj, log_a, norm_w):
    nc = SEQ // GLA_C
    rows = [functools.partial(lambda bi, c: (bi * nc + c, 0), bi) for bi in range(BATCH)]
    o, s = pl.pallas_call(
        _gla_prompt_body,
        grid=(nc,),
        in_specs=([pl.BlockSpec((GLA_C, GLA_MAIN_DIM), r) for r in rows]
                  + [pl.BlockSpec((GLA_C, GLA_KEY_DIM), r) for r in rows]
                  + [pl.BlockSpec((1, GLA_DV), lambda c: (0, 0))]),
        out_specs=[
            pl.BlockSpec((BATCH, GLA_C, GLA_VAL_DIM), lambda c: (0, c, 0)),
            pl.BlockSpec((BATCH, GLA_HEADS, GLA_DK, GLA_DV), lambda c: (0, 0, 0, 0)),
        ],
        out_shape=[
            jax.ShapeDtypeStruct((BATCH, SEQ, GLA_VAL_DIM), BF16),
            jax.ShapeDtypeStruct((BATCH, GLA_HEADS, GLA_DK, GLA_DV), F32),
        ],
        scratch_shapes=[pltpu.VMEM((BATCH, GLA_HEADS, GLA_DV, GLA_DK), F32)],
        compiler_params=_cparams(("arbitrary",)),
        name="gla_prompt",
    )(*([proj] * BATCH + [log_a] * BATCH + [norm_w.reshape(1, GLA_DV)]))
    return o.reshape(N_PROMPT, GLA_VAL_DIM), s


GLA_SEQS_PER_STEP = 4


def _gla_sample_body(proj_ref, la_ref, s0_ref, nw_ref, o_ref, s_ref):
    ones = jnp.ones((DEC_SEQ, LANES), BF16)
    proj = proj_ref[...].astype(F32)
    la = la_ref[...]
    for i in range(GLA_SEQS_PER_STEP):
        rows_i = slice(i * DEC_SEQ, (i + 1) * DEC_SEQ)
        for h in range(GLA_HEADS):
            q = proj[rows_i, h * GLA_DK:(h + 1) * GLA_DK]
            k = proj[rows_i, GLA_KEY_DIM + h * GLA_DK:GLA_KEY_DIM + (h + 1) * GLA_DK]
            v0 = 2 * GLA_KEY_DIM + h * GLA_DV
            v = proj[rows_i, v0:v0 + GLA_DV].astype(BF16)
            gate = proj[rows_i, v0 + GLA_VAL_DIM:v0 + GLA_VAL_DIM + GLA_DV]
            g = la[rows_i, h * GLA_DK:(h + 1) * GLA_DK]
            rows = [g[0:1]]
            for t in range(1, DEC_SEQ):
                rows.append(rows[-1] + g[t:t + 1])
            b = jnp.concatenate(rows, axis=0)
            b_last = rows[-1]
            qe = (q * GLA_DK ** -0.5 * jnp.exp(b)).astype(BF16)
            ke = (k * jnp.exp(-b)).astype(BF16)
            a = _causal(lax.dot_general(qe, ke, _NT, preferred_element_type=F32))
            s0 = s0_ref[i, h]
            o = (jnp.dot(a.astype(BF16), v, preferred_element_type=F32)
                 + jnp.dot(qe, s0.astype(BF16), preferred_element_type=F32))
            kd = (k * jnp.exp(b_last - b)).astype(BF16)
            dsum = sum(lax.dot_general(piece, ones, _TN, preferred_element_type=F32) for piece in _split3(g))
            decay = jnp.concatenate([jnp.exp(dsum)] * (GLA_DV // LANES), axis=1)
            s_ref[i, h] = s0 * decay + lax.dot_general(kd, v, _TN, preferred_element_type=F32)
            o_ref[rows_i, h * GLA_DV:(h + 1) * GLA_DV] = _gla_out(o, gate, nw_ref[...])


def _gla_sample(proj, log_a, state, norm_w):
    n = GLA_SEQS_PER_STEP
    rows = n * DEC_SEQ
    first = N_PROMPT // rows
    seq4 = lambda s: (s, 0, 0, 0)
    return pl.pallas_call(
        _gla_sample_body,
        grid=(DEC_BATCH // n,),
        in_specs=[
            pl.BlockSpec((rows, GLA_MAIN_DIM), lambda s: (first + s, 0)),
            pl.BlockSpec((rows, GLA_KEY_DIM), lambda s: (first + s, 0)),
            pl.BlockSpec((n, GLA_HEADS, GLA_DK, GLA_DV), seq4),
            pl.BlockSpec((1, GLA_DV), lambda s: (0, 0)),
        ],
        out_specs=[
            pl.BlockSpec((rows, GLA_VAL_DIM), lambda s: (s, 0)),
            pl.BlockSpec((n, GLA_HEADS, GLA_DK, GLA_DV), seq4),
        ],
        out_shape=[
            jax.ShapeDtypeStruct((N_SAMPLE, GLA_VAL_DIM), F32),
            jax.ShapeDtypeStruct((DEC_BATCH, GLA_HEADS, GLA_DK, GLA_DV), F32),
        ],
        compiler_params=_cparams(("arbitrary",)),
        name="gla_sample",
    )(proj, log_a, state, norm_w.reshape(1, GLA_DV))


CAST_ROWS = 256


def _cast_body(w_ref, o_ref):
    o_ref[...] = w_ref[0].astype(BF16)


def _layer_bf16(w, layer):
    _, rows, cols = w.shape
    return pl.pallas_call(
        _cast_body,
        grid=(rows // CAST_ROWS,),
        in_specs=[pl.BlockSpec((1, CAST_ROWS, cols), lambda i: (layer, i, 0))],
        out_specs=pl.BlockSpec((CAST_ROWS, cols), lambda i: (i, 0)),
        out_shape=jax.ShapeDtypeStruct((rows, cols), BF16),
        compiler_params=_cparams(("arbitrary",)),
        name="cast_bf16",
    )(w)


def _slot_rows_body(*refs):
    o_ref = refs[-1]
    for u, w_ref in enumerate(refs[:-1]):
        o_ref[u * HEAD_DIM:(u + 1) * HEAD_DIM] = w_ref[0].astype(BF16)


def _slot_rows_bf16(w, layer):
    def src(u):
        g, hh = u // KV_PER_STEP, u % KV_PER_STEP
        return lambda pair: (layer, (pair * KV_PER_STEP + hh) * GROUP + g, 0)

    cols = w.shape[2]
    return pl.pallas_call(
        _slot_rows_body,
        grid=(N_PAIRS,),
        in_specs=[pl.BlockSpec((1, HEAD_DIM, cols), src(u)) for u in range(PAIR_SLOTS)],
        out_specs=pl.BlockSpec((PAIR_SLOTS * HEAD_DIM, cols), lambda pair: (pair, 0)),
        out_shape=jax.ShapeDtypeStruct((Q_DIM, cols), BF16),
        compiler_params=_cparams(("arbitrary",)),
        name="slot_rows_bf16",
    )(*([w] * PAIR_SLOTS))


def _swa_layer(x, cache_k, cache_v, norm_w, w_qkv, b_qkv, j, w_o_all, b_o, sinks, rel_bias):
    slots = (N_PAIRS, KV_PER_STEP, GROUP, HEAD_DIM)
    w_q = w_qkv[:, :Q_DIM].reshape((D_MODEL,) + slots).transpose(0, 1, 3, 2, 4).reshape(D_MODEL, Q_DIM)
    b_q = b_qkv[:Q_DIM].reshape(slots).transpose(0, 2, 1, 3).reshape(Q_DIM)
    w_qkv_s = jnp.concatenate([w_q, w_qkv[:, Q_DIM:]], axis=1).astype(BF16)
    b_qkv_s = jnp.concatenate([b_q, b_qkv[Q_DIM:]])
    w_o_s = _slot_rows_bf16(w_o_all, j)

    q, kv = _norm_proj(x, norm_w, w_qkv_s, b_qkv_s, ((Q_DIM, BF16), (2 * KV_COLS, F32)), "swa_qkv")
    bias_tbl = _bias_table(rel_bias)
    o_p = _swa_prompt(q, kv, bias_tbl, sinks)

    q_rows = q[N_PROMPT:].astype(F32).reshape(DEC_BATCH, S_ROWS, HEAD_DIM)
    kv_s = kv[N_PROMPT:].reshape(DEC_BATCH, DEC_SEQ, 2 * KV_COLS)
    pad8 = ((0, 0), (0, SUBLANES - DEC_SEQ), (0, 0))
    k_new8 = jnp.pad(kv_s[..., :KV_COLS], pad8)
    v_new8 = jnp.pad(kv_s[..., KV_COLS:], pad8)
    bias_s = bias_tbl[:, :, :DEC_SEQ].transpose(2, 0, 1).reshape(S_ROWS, 2 * WINDOW)
    slot_heads = np.array([_slot_head(s) for s in range(N_HEADS)])
    sink_col = jnp.tile(sinks[slot_heads], DEC_SEQ).reshape(S_ROWS, 1)
    o_s, k_s, v_s = _swa_sample(q_rows, k_new8, v_new8,
                                cache_k.reshape(DEC_BATCH, WINDOW, KV_COLS),
                                cache_v.reshape(DEC_BATCH, WINDOW, KV_COLS), bias_s, sink_col)
    o_s = o_s.reshape(N_SAMPLE, Q_DIM)
    x = _proj_res((o_p, o_s), w_o_s, b_o, x, "swa_out")

    kv_p = jnp.stack([kv[(b + 1) * SEQ - WINDOW:(b + 1) * SEQ] for b in range(BATCH)])
    k_p = kv_p[..., :KV_COLS].reshape(BATCH, WINDOW, N_KV_HEADS, HEAD_DIM)
    v_p = kv_p[..., KV_COLS:].reshape(BATCH, WINDOW, N_KV_HEADS, HEAD_DIM)
    shape_s = (DEC_BATCH, WINDOW, N_KV_HEADS, HEAD_DIM)
    return x, k_p, v_p, k_s.reshape(shape_s), v_s.reshape(shape_s)


def _gla_layer(x, state, norm_w, j, w_in_all, w_gk2, b_gk, gnorm, w_o_all):
    proj, log_a = _norm_proj(x, norm_w, w_in_all[j].astype(BF16), jnp.zeros((GLA_MAIN_DIM,), F32),
                             ((GLA_MAIN_DIM, BF16),), "gla_in", gate=(w_gk2.astype(BF16), b_gk))
    o_p, s_p = _gla_prompt(proj, log_a, gnorm)
    o_s, s_s = _gla_sample(proj, log_a, state, gnorm)
    x = _proj_res((o_p, o_s), _layer_bf16(w_o_all, j), jnp.zeros((D_MODEL,), F32), x, "gla_out")
    return x, s_p, s_s


def kernel(x_prompt, x_sample, cache_swa_k, cache_swa_v, state_gla, norm_ffn1, ffn1_w_gate, ffn1_w_up,
           ffn1_w_down, norm_mix, norm_ffn2, ffn2_w_gate, ffn2_w_up, ffn2_w_down, norm_final, rel_bias,
           swa_w_qkv, swa_b_qkv, swa_w_o, swa_b_o, swa_sinks, gla_w_in, gla_w_gk2, gla_b_gk, gla_norm,
           gla_w_o):
    x = (x_prompt.reshape(N_PROMPT, D_MODEL), x_sample.reshape(N_SAMPLE, D_MODEL))
    swa_kp, swa_vp, swa_ks, swa_vs, gla_sp, gla_ss = [], [], [], [], [], []
    for i in range(DEPTH):
        x = _ffn_pair(x, i, norm_ffn1[i], ffn1_w_gate, ffn1_w_up, ffn1_w_down)
        j = i // 2
        if i % 2 == 0:
            x, kp, vp, ks, vs = _swa_layer(x, cache_swa_k[j], cache_swa_v[j], norm_mix[i], swa_w_qkv[j],
                                           swa_b_qkv[j], j, swa_w_o, swa_b_o[j], swa_sinks[j], rel_bias)
            swa_kp.append(kp)
            swa_vp.append(vp)
            swa_ks.append(ks)
            swa_vs.append(vs)
        else:
            x, sp, ss = _gla_layer(x, state_gla[j], norm_mix[i], j, gla_w_in, gla_w_gk2[j], gla_b_gk[j],
                                   gla_norm[j], gla_w_o)
            gla_sp.append(sp)
            gla_ss.append(ss)
        final_w = norm_final if i == DEPTH - 1 else None
        x = _ffn_pair(x, i, norm_ffn2[i], ffn2_w_gate, ffn2_w_up, ffn2_w_down, final_w)
    y_prompt = x[0].reshape(BATCH, SEQ, D_MODEL)
    y_sample = x[1].reshape(DEC_BATCH, DEC_SEQ, D_MODEL)
    return (y_prompt, y_sample, jnp.stack(swa_kp), jnp.stack(swa_vp), jnp.stack(swa_ks), jnp.stack(swa_vs),
            jnp.stack(gla_sp), jnp.stack(gla_ss))
```

```python
import functools
import math

import numpy as np
import jax
import jax.numpy as jnp
from jax import lax
from jax.experimental import pallas as pl
from jax.experimental.pallas import tpu as pltpu

F32 = jnp.float32
BF16 = jnp.bfloat16

D_MODEL = 2048
BATCH = 2
SEQ = 4096
DEPTH = 2
DEC_BATCH = 128
DEC_SEQ = 4
RMS_EPS = 1e-6
D_FF = 5632
N_HEADS = 32
N_KV_HEADS = 8
HEAD_DIM = 64
GROUP = N_HEADS // N_KV_HEADS
WINDOW = 128
NUM_BUCKETS = 32
MAX_DISTANCE = 128
NEG_INF = -1e30
GLA_HEADS = 4
GLA_DK = 256
GLA_DV = 512
GLA_KEY_DIM = GLA_HEADS * GLA_DK
GLA_VAL_DIM = GLA_HEADS * GLA_DV
GATE_RANK = 16
GATE_NORMALIZER = 16.0
GLA_MAIN_DIM = 2 * GLA_KEY_DIM + 2 * GLA_VAL_DIM
Q_DIM = N_HEADS * HEAD_DIM
KV_COLS = N_KV_HEADS * HEAD_DIM

N_PROMPT = BATCH * SEQ
N_SAMPLE = DEC_BATCH * DEC_SEQ
N_TOK = N_PROMPT + N_SAMPLE

LANES = 128
SUBLANES = 8
VMEM_LIMIT = 56 * 1024 * 1024
BIG_VMEM_LIMIT = 60 * 1024 * 1024

TM = 512
TN = 512
GLA_C = 64
NB = SEQ // WINDOW
N_PROMPT_TILES = N_PROMPT // TM


def _rms(x, w):
    return x * lax.rsqrt(jnp.mean(x * x, axis=-1, keepdims=True) + RMS_EPS) * w


def _silu(x):
    return x * jax.nn.sigmoid(x)


def _cparams(sem, vmem_limit=VMEM_LIMIT):
    return pltpu.CompilerParams(dimension_semantics=sem, vmem_limit_bytes=vmem_limit)


def _row_pair_specs(width):
    return [pl.BlockSpec((TM, width), lambda i: (jnp.minimum(i, N_PROMPT_TILES - 1), 0)),
            pl.BlockSpec((TM, width), lambda i: (jnp.maximum(i - N_PROMPT_TILES, 0), 0))]


def _on_row_source(fn, *ref_pairs):
    i = pl.program_id(0)
    pl.when(i < N_PROMPT_TILES)(lambda: fn(*[p[0] for p in ref_pairs]))
    pl.when(i >= N_PROMPT_TILES)(lambda: fn(*[p[1] for p in ref_pairs]))


FFN_TM = 1024
FFN_TF = 256
FFN_TF_SAMPLE = 512


def _ffn_body(final_norm, emit_w, x_ref, nw_ref, wg_ref, wu_ref, wd_ref, *rest):
    rest = list(rest)
    fw_ref = rest.pop(0) if final_norm else None
    o_ref = rest.pop(0)
    h_ref = rest.pop()
    j = pl.program_id(1)

    @pl.when(j == 0)
    def _():
        x = x_ref[...]
        h_ref[...] = _rms(x, nw_ref[...]).astype(BF16)
        o_ref[...] = x

    if emit_w:
        wgo_ref, wuo_ref, wdo_ref = rest
        wgo_ref[...] = wg_ref[0].astype(BF16)
        wuo_ref[...] = wu_ref[0].astype(BF16)
        wdo_ref[...] = wd_ref[0].astype(BF16)
        wg, wu, wd = wgo_ref[...], wuo_ref[...], wdo_ref[...]
    else:
        wg, wu, wd = wg_ref[...], wu_ref[...], wd_ref[...]
    h = h_ref[...]
    g = jnp.dot(h, wg, preferred_element_type=F32)
    u = jnp.dot(h, wu, preferred_element_type=F32)
    a = (_silu(g) * (0.5 * u)).astype(BF16)
    o_ref[...] += jnp.dot(a, wd, preferred_element_type=F32)

    if final_norm:
        @pl.when(j == pl.num_programs(1) - 1)
        def _():
            o_ref[...] = _rms(o_ref[...], fw_ref[...])


def _ffn(x, layer, nw, wg, wu, wd, final_w=None):
    m = x.shape[0]
    emit_w = wg.dtype == F32
    tm = min(FFN_TM, m)
    tf = FFN_TF if emit_w else FFN_TF_SAMPLE
    nj = D_FF // tf
    final_norm = final_w is not None
    vec = pl.BlockSpec((1, D_MODEL), lambda i, j: (0, 0))
    if emit_w:
        w_specs = [
            pl.BlockSpec((1, D_MODEL, tf), lambda i, j: (layer, 0, j)),
            pl.BlockSpec((1, D_MODEL, tf), lambda i, j: (layer, 0, j)),
            pl.BlockSpec((1, tf, D_MODEL), lambda i, j: (layer, j, 0)),
        ]
    else:
        w_specs = [
            pl.BlockSpec((D_MODEL, tf), lambda i, j: (0, j)),
            pl.BlockSpec((D_MODEL, tf), lambda i, j: (0, j)),
            pl.BlockSpec((tf, D_MODEL), lambda i, j: (j, 0)),
        ]
    in_specs = [pl.BlockSpec((tm, D_MODEL), lambda i, j: (i, 0)), vec] + w_specs
    args = [x, nw.reshape(1, D_MODEL), wg, wu, wd]
    if final_norm:
        in_specs.append(vec)
        args.append(final_w.reshape(1, D_MODEL))
    out_specs = [pl.BlockSpec((tm, D_MODEL), lambda i, j: (i, 0))]
    out_shape = [jax.ShapeDtypeStruct((m, D_MODEL), F32)]
    if emit_w:
        once = lambda i, j: jnp.where(i == 0, j, nj - 1)
        out_specs += [
            pl.BlockSpec((D_MODEL, tf), lambda i, j: (0, once(i, j))),
            pl.BlockSpec((D_MODEL, tf), lambda i, j: (0, once(i, j))),
            pl.BlockSpec((tf, D_MODEL), lambda i, j: (once(i, j), 0)),
        ]
        out_shape += [
            jax.ShapeDtypeStruct((D_MODEL, D_FF), BF16),
            jax.ShapeDtypeStruct((D_MODEL, D_FF), BF16),
            jax.ShapeDtypeStruct((D_FF, D_MODEL), BF16),
        ]
    outs = pl.pallas_call(
        functools.partial(_ffn_body, final_norm, emit_w),
        grid=(m // tm, nj),
        in_specs=in_specs,
        out_specs=out_specs,
        out_shape=out_shape,
        scratch_shapes=[pltpu.VMEM((tm, D_MODEL), BF16)],
        compiler_params=_cparams(("arbitrary", "arbitrary"), BIG_VMEM_LIMIT),
        name="ffn",
    )(*args)
    return outs[0], tuple(outs[1:])


def _ffn_pair(x_pair, layer, nw, wg, wu, wd, final_w=None):
    y_prompt, w_bf16 = _ffn(x_pair[0], layer, nw, wg, wu, wd, final_w)
    y_sample, _ = _ffn(x_pair[1], layer, nw, *w_bf16, final_w)
    return y_prompt, y_sample


def _log_sigmoid(x):
    return jnp.minimum(x, 0.0) - jnp.log1p(jnp.exp(-jnp.abs(x)))


def _norm_proj_body(segments, with_gate, xp_ref, xs_ref, nw_ref, w_ref, b_ref, *rest):
    if with_gate:
        w2_ref, bg_ref = rest[:2]
        rest = rest[2:]
    o_refs = rest[:len(segments)]

    def run(x_ref):
        h = _rms(x_ref[...], nw_ref[...]).astype(BF16)
        col = 0
        for (width, _), o_ref in zip(segments, o_refs):
            for c in range(0, width, TN):
                acc = jnp.dot(h, w_ref[:, col + c:col + c + TN], preferred_element_type=F32)
                o_ref[:, c:c + TN] = (acc + b_ref[:, col + c:col + c + TN]).astype(o_ref.dtype)
            col += width
        if with_gate:
            gk = jnp.dot(h, w_ref[:, col:col + GATE_RANK], preferred_element_type=F32)
            z = jnp.dot(gk.astype(BF16), w2_ref[...], preferred_element_type=F32) + bg_ref[...]
            rest[-1][...] = _log_sigmoid(z) / GATE_NORMALIZER
    _on_row_source(run, (xp_ref, xs_ref))


def _norm_proj(xs, nw, w, b, segments, name, gate=None):
    n = sum(width for width, _ in segments)
    in_specs = _row_pair_specs(D_MODEL) + [
        pl.BlockSpec((1, D_MODEL), lambda i: (0, 0)),
        pl.BlockSpec(w.shape, lambda i: (0, 0), pipeline_mode=pl.Buffered(1)),
        pl.BlockSpec((1, n), lambda i: (0, 0)),
    ]
    args = [*xs, nw.reshape(1, D_MODEL), w, b.reshape(1, n)]
    out_specs = [pl.BlockSpec((TM, width), lambda i: (i, 0)) for width, _ in segments]
    out_shape = [jax.ShapeDtypeStruct((N_TOK, width), dtype) for width, dtype in segments]
    if gate is not None:
        w2, bg = gate
        assert w.shape[1] == n + GATE_RANK, w.shape
        in_specs += [
            pl.BlockSpec((GATE_RANK, GLA_KEY_DIM), lambda i: (0, 0)),
            pl.BlockSpec((1, GLA_KEY_DIM), lambda i: (0, 0)),
        ]
        args += [w2, bg.reshape(1, GLA_KEY_DIM)]
        out_specs.append(pl.BlockSpec((TM, GLA_KEY_DIM), lambda i: (i, 0)))
        out_shape.append(jax.ShapeDtypeStruct((N_TOK, GLA_KEY_DIM), F32))
    return pl.pallas_call(
        functools.partial(_norm_proj_body, segments, gate is not None),
        grid=(N_TOK // TM,),
        in_specs=in_specs,
        out_specs=out_specs,
        out_shape=out_shape,
        compiler_params=_cparams(("arbitrary",), BIG_VMEM_LIMIT),
        name=name,
    )(*args)


def _proj_res_body(ap_ref, as_ref, w_ref, b_ref, rp_ref, rs_ref, op_ref, os_ref):
    def run(a_ref, r_ref, o_ref):
        a = a_ref[...].astype(BF16)
        for c in range(0, D_MODEL, TN):
            acc = jnp.dot(a, w_ref[:, c:c + TN], preferred_element_type=F32)
            o_ref[:, c:c + TN] = r_ref[:, c:c + TN] + acc + b_ref[:, c:c + TN]
    _on_row_source(run, (ap_ref, as_ref), (rp_ref, rs_ref), (op_ref, os_ref))


def _proj_res(a_pair, w, b, res_pair, name):
    k = w.shape[0]
    return pl.pallas_call(
        _proj_res_body,
        grid=(N_TOK // TM,),
        in_specs=_row_pair_specs(k) + [
            pl.BlockSpec((k, D_MODEL), lambda i: (0, 0), pipeline_mode=pl.Buffered(1)),
            pl.BlockSpec((1, D_MODEL), lambda i: (0, 0)),
        ] + _row_pair_specs(D_MODEL),
        out_specs=_row_pair_specs(D_MODEL),
        out_shape=[jax.ShapeDtypeStruct((N_PROMPT, D_MODEL), F32), jax.ShapeDtypeStruct((N_SAMPLE, D_MODEL), F32)],
        compiler_params=_cparams(("arbitrary",)),
        name=name,
    )(*a_pair, w, b.reshape(1, D_MODEL), *res_pair)


def _t5_bucket_table():
    i = np.arange(WINDOW)[None, :]
    j = np.arange(2 * WINDOW)[:, None]
    n = np.maximum(WINDOW + i - j, 0)
    max_exact = NUM_BUCKETS // 2
    nf = np.maximum(n, 1).astype(np.float32)
    large = max_exact + (np.log(nf / np.float32(max_exact)) / np.float32(math.log(MAX_DISTANCE / max_exact))
                         * np.float32(NUM_BUCKETS - max_exact)).astype(np.int32)
    large = np.minimum(large, NUM_BUCKETS - 1)
    return np.where(n < max_exact, n, large).astype(np.int32)


KV_PER_STEP = LANES // HEAD_DIM
N_PAIRS = N_KV_HEADS // KV_PER_STEP
Q_COLS_PER_STEP = KV_PER_STEP * GROUP * HEAD_DIM
PAIR_SLOTS = GROUP * KV_PER_STEP


def _slot_head(slot):
    pair = slot // PAIR_SLOTS
    g = (slot // KV_PER_STEP) % GROUP
    hh = slot % KV_PER_STEP
    return (pair * KV_PER_STEP + hh) * GROUP + g


BIAS_SLOTS_PER_STEP = 8


def _bias_table_body(bucket_ref, rb_ref, o_ref):
    bucket = bucket_ref[...]
    j = lax.broadcasted_iota(jnp.int32, (2 * WINDOW, WINDOW), 0)
    i = lax.broadcasted_iota(jnp.int32, (2 * WINDOW, WINDOW), 1)
    dist = WINDOW + i - j
    in_window = (dist >= 0) & (dist < WINDOW)
    for t in range(BIAS_SLOTS_PER_STEP):
        h = _slot_head(pl.program_id(0) * BIAS_SLOTS_PER_STEP + t)
        acc = jnp.zeros((2 * WINDOW, WINDOW), F32)
        for b in range(NUM_BUCKETS):
            acc = jnp.where(bucket == b, rb_ref[b, h], acc)
        o_ref[t] = jnp.where(in_window, acc, NEG_INF)


def _bias_table(rel_bias):
    n = BIAS_SLOTS_PER_STEP
    return pl.pallas_call(
        _bias_table_body,
        grid=(N_HEADS // n,),
        in_specs=[
            pl.BlockSpec((2 * WINDOW, WINDOW), lambda h: (0, 0)),
            pl.BlockSpec(memory_space=pltpu.SMEM),
        ],
        out_specs=pl.BlockSpec((n, 2 * WINDOW, WINDOW), lambda h: (h, 0, 0)),
        out_shape=jax.ShapeDtypeStruct((N_HEADS, 2 * WINDOW, WINDOW), F32),
        name="bias_table",
    )(jnp.asarray(_t5_bucket_table()), rel_bias)


def _softmax_with_sink(s, sink_col):
    m = jnp.maximum(jnp.max(s, axis=-1, keepdims=True), sink_col)
    p = jnp.exp(s - m)
    denom = jnp.sum(p, axis=-1, keepdims=True) + jnp.exp(sink_col - m)
    return p, 1.0 / denom


PAIR_COLS = PAIR_SLOTS * WINDOW
ONES_ROWS = 16
PAIRS_PER_STEP = 4


def _swa_prompt_body(sink_ref, q_ref, kp_ref, ko_ref, vp_ref, vo_ref, bias_ref, o_ref):
    first = pl.program_id(2) == 0
    args = (sink_ref, q_ref, kp_ref, ko_ref, vp_ref, vo_ref, bias_ref, o_ref)
    pl.when(first)(lambda: _swa_prompt_block(True, *args))
    pl.when(jnp.logical_not(first))(lambda: _swa_prompt_block(False, *args))


def _swa_prompt_block(mask_prev, sink_ref, q_ref, kp_ref, ko_ref, vp_ref, vo_ref, bias_ref, o_ref):
    head_a = lax.broadcasted_iota(jnp.int32, (WINDOW, LANES), 1) < HEAD_DIM
    for t in range(PAIRS_PER_STEP):
        pair = pl.program_id(0) * PAIRS_PER_STEP + t
        lanes = slice(t * LANES, (t + 1) * LANES)
        q0 = t * Q_COLS_PER_STEP
        k = jnp.concatenate([kp_ref[:, lanes], ko_ref[:, lanes]], axis=0).astype(BF16)
        v = jnp.concatenate([vp_ref[:, lanes], vo_ref[:, lanes]], axis=0)
        vt = jnp.concatenate([v.T, jnp.ones((ONES_ROWS, 2 * WINDOW), F32)], axis=0).astype(BF16)
        parts = []
        for g in range(GROUP):
            qg = q_ref[:, q0 + g * LANES:q0 + (g + 1) * LANES]
            zero = jnp.zeros_like(qg)
            parts += [jnp.where(head_a, qg, zero), jnp.where(head_a, zero, qg)]
        qbd = jnp.concatenate(parts, axis=0)
        st = lax.dot_general(k, qbd, (((1,), (1,)), ((), ())), preferred_element_type=F32)
        st = st + jnp.concatenate([bias_ref[t * PAIR_SLOTS + u] for u in range(PAIR_SLOTS)], axis=1)
        if mask_prev:
            st = jnp.concatenate([st[:WINDOW] + NEG_INF, st[WINDOW:]], axis=0)
        sink_row = jnp.concatenate(
            [jnp.full((1, WINDOW), sink_ref[_slot_head(pair * PAIR_SLOTS + slot)], F32)
             for slot in range(PAIR_SLOTS)], axis=1)
        m = jnp.maximum(jnp.max(st, axis=0, keepdims=True), sink_row)
        pt = jnp.exp(st - m).astype(BF16)
        oa = jnp.dot(vt, pt, preferred_element_type=F32)
        inv = 1.0 / (oa[LANES:LANES + 1] + jnp.exp(sink_row - m))
        o = oa[:LANES] * inv
        for g in range(GROUP):
            c = g * KV_PER_STEP * WINDOW
            ot = jnp.concatenate([o[:HEAD_DIM, c:c + WINDOW], o[HEAD_DIM:, c + WINDOW:c + 2 * WINDOW]], axis=0)
            o_ref[:, q0 + g * LANES:q0 + (g + 1) * LANES] = ot.T.astype(o_ref.dtype)


def _swa_prompt(q, kv, bias_tbl, sinks):
    n = PAIRS_PER_STEP
    v_col0 = KV_COLS // (n * LANES)

    def prev(p, b, i):
        return b * NB + jnp.maximum(i - 1, 0)

    return pl.pallas_call(
        _swa_prompt_body,
        grid=(N_PAIRS // n, BATCH, NB),
        in_specs=[
            pl.BlockSpec(memory_space=pltpu.SMEM),
            pl.BlockSpec((WINDOW, n * Q_COLS_PER_STEP), lambda p, b, i: (b * NB + i, p)),
            pl.BlockSpec((WINDOW, n * LANES), lambda p, b, i: (prev(p, b, i), p)),
            pl.BlockSpec((WINDOW, n * LANES), lambda p, b, i: (b * NB + i, p)),
            pl.BlockSpec((WINDOW, n * LANES), lambda p, b, i: (prev(p, b, i), v_col0 + p)),
            pl.BlockSpec((WINDOW, n * LANES), lambda p, b, i: (b * NB + i, v_col0 + p)),
            pl.BlockSpec((n * PAIR_SLOTS, 2 * WINDOW, WINDOW), lambda p, b, i: (p, 0, 0)),
        ],
        out_specs=pl.BlockSpec((WINDOW, n * Q_COLS_PER_STEP), lambda p, b, i: (b * NB + i, p)),
        out_shape=jax.ShapeDtypeStruct((N_PROMPT, Q_DIM), BF16),
        compiler_params=_cparams(("arbitrary", "arbitrary", "arbitrary")),
        name="swa_prompt",
    )(sinks, q, kv, kv, kv, kv, bias_tbl)


S_ROWS = N_HEADS * DEC_SEQ
SWA_SEQS_PER_STEP = 8


def _swa_sample_body(q_ref, kn_ref, vn_ref, ck_ref, cv_ref, bias_ref, sink_ref,
                     o_ref, ko_ref, vo_ref):
    slot = lax.broadcasted_iota(jnp.int32, (S_ROWS, KV_COLS), 0) % N_HEADS
    row_kv = slot // PAIR_SLOTS * KV_PER_STEP + slot % KV_PER_STEP
    col_kv = lax.broadcasted_iota(jnp.int32, (S_ROWS, KV_COLS), 1) // HEAD_DIM
    own = row_kv == col_kv
    pad = jnp.zeros((WINDOW - SUBLANES, KV_COLS), F32)
    keep = WINDOW - DEC_SEQ
    for i in range(SWA_SEQS_PER_STEP):
        x = q_ref[i].astype(BF16)
        xt = jnp.concatenate([x] * N_KV_HEADS, axis=1)
        qbd = jnp.where(own, xt, jnp.zeros_like(xt))
        kk = jnp.concatenate([ck_ref[i], kn_ref[i], pad], axis=0).astype(BF16)
        vv = jnp.concatenate([cv_ref[i], vn_ref[i], pad], axis=0).astype(BF16)
        s = lax.dot_general(qbd, kk, (((1,), (1,)), ((), ())), preferred_element_type=F32)
        s = s + bias_ref[...]
        p, inv = _softmax_with_sink(s, sink_ref[...])
        of = jnp.dot(p.astype(BF16), vv, preferred_element_type=F32)
        of = jnp.where(own, of, 0.0)
        o = of[:, 0:HEAD_DIM]
        for c in range(1, N_KV_HEADS):
            o = o + of[:, c * HEAD_DIM:(c + 1) * HEAD_DIM]
        o_ref[i] = o * inv
        ko_ref[i, 0:keep, :] = ck_ref[i, DEC_SEQ:WINDOW, :]
        ko_ref[i, keep:WINDOW, :] = kn_ref[i, 0:DEC_SEQ, :]
        vo_ref[i, 0:keep, :] = cv_ref[i, DEC_SEQ:WINDOW, :]
        vo_ref[i, keep:WINDOW, :] = vn_ref[i, 0:DEC_SEQ, :]


def _swa_sample(q_rows, k_new8, v_new8, cache_k, cache_v, bias_s, sink_col):
    n = SWA_SEQS_PER_STEP
    seq3 = lambda s: (s, 0, 0)
    full2 = lambda s: (0, 0)
    return pl.pallas_call(
        _swa_sample_body,
        grid=(DEC_BATCH // n,),
        in_specs=[
            pl.BlockSpec((n, S_ROWS, HEAD_DIM), seq3),
            pl.BlockSpec((n, SUBLANES, KV_COLS), seq3),
            pl.BlockSpec((n, SUBLANES, KV_COLS), seq3),
            pl.BlockSpec((n, WINDOW, KV_COLS), seq3),
            pl.BlockSpec((n, WINDOW, KV_COLS), seq3),
            pl.BlockSpec((S_ROWS, 2 * WINDOW), full2),
            pl.BlockSpec((S_ROWS, 1), full2),
        ],
        out_specs=[
            pl.BlockSpec((n, S_ROWS, HEAD_DIM), seq3),
            pl.BlockSpec((n, WINDOW, KV_COLS), seq3),
            pl.BlockSpec((n, WINDOW, KV_COLS), seq3),
        ],
        out_shape=[
            jax.ShapeDtypeStruct((DEC_BATCH, S_ROWS, HEAD_DIM), F32),
            jax.ShapeDtypeStruct((DEC_BATCH, WINDOW, KV_COLS), F32),
            jax.ShapeDtypeStruct((DEC_BATCH, WINDOW, KV_COLS), F32),
        ],
        compiler_params=_cparams(("arbitrary",)),
        name="swa_sample",
    )(q_rows, k_new8, v_new8, cache_k, cache_v, bias_s, sink_col)


def _gla_out(o, gate, norm_w):
    return _rms(o, norm_w) * _silu(gate)


def _split3(x):
    hi = x.astype(BF16)
    r = x - hi.astype(F32)
    mid = r.astype(BF16)
    lo = (r - mid.astype(F32)).astype(BF16)
    return hi, mid, lo


def _cumsum_rows(g):
    c = g.shape[0]
    tri = (lax.broadcasted_iota(jnp.int32, (c, c), 0) >= lax.broadcasted_iota(jnp.int32, (c, c), 1)).astype(BF16)
    return jnp.dot(jnp.concatenate([tri] * 3, axis=1), jnp.concatenate(_split3(g), axis=0),
                   preferred_element_type=F32)


def _causal(a):
    c = a.shape[0]
    keep = lax.broadcasted_iota(jnp.int32, (c, c), 0) >= lax.broadcasted_iota(jnp.int32, (c, c), 1)
    return jnp.where(keep, a, 0.0)


_NT = (((1,), (1,)), ((), ()))
_TN = (((0,), (0,)), ((), ()))


def _gla_prompt_body(*refs):
    proj_refs = refs[:BATCH]
    la_refs = refs[BATCH:2 * BATCH]
    nw_ref, o_ref, s_ref, st_ref = refs[2 * BATCH:]
    c = pl.program_id(0)

    @pl.when(c == 0)
    def _():
        st_ref[...] = jnp.zeros_like(st_ref)

    for bi in range(BATCH):
        p_ref = proj_refs[bi]
        b_all = _cumsum_rows(la_refs[bi][...])
        for h in range(GLA_HEADS):
            kc = slice(h * GLA_DK, (h + 1) * GLA_DK)
            v0 = 2 * GLA_KEY_DIM + h * GLA_DV
            b = b_all[:, kc]
            q = p_ref[:, kc].astype(F32) * GLA_DK ** -0.5
            k = p_ref[:, GLA_KEY_DIM + h * GLA_DK:GLA_KEY_DIM + (h + 1) * GLA_DK].astype(F32)
            v = p_ref[:, v0:v0 + GLA_DV]
            gate = p_ref[:, v0 + GLA_VAL_DIM:v0 + GLA_VAL_DIM + GLA_DV].astype(F32)
            qe = (q * jnp.exp(b)).astype(BF16)
            r = b[GLA_C // 2 - 1:GLA_C // 2, :]
            qr = (q * jnp.exp(b - r)).astype(BF16)
            kr = (k * jnp.exp(r - b)).astype(BF16)
            a = _causal(lax.dot_general(qr, kr, _NT, preferred_element_type=F32))
            st = st_ref[bi, h]
            o = (jnp.dot(a.astype(BF16), v, preferred_element_type=F32)
                 + lax.dot_general(qe, st.astype(BF16), _NT, preferred_element_type=F32))
            b_last = b[GLA_C - 1:GLA_C, :]
            kd = (k * jnp.exp(b_last - b)).astype(BF16)
            st_new = st * jnp.exp(b_last) + lax.dot_general(v, kd, _TN, preferred_element_type=F32)
            st_ref[bi, h] = st_new
            o_ref[bi, :, h * GLA_DV:(h + 1) * GLA_DV] = _gla_out(o, gate, nw_ref[...]).astype(o_ref.dtype)

    @pl.when(c == pl.num_programs(0) - 1)
    def _():
        for bi in range(BATCH):
            for h in range(GLA_HEADS):
                s_ref[bi, h] = st_ref[bi, h].T


def _gla_prompt(proj, log_a, norm_w):
    nc = SEQ // GLA_C
    rows = [functools.partial(lambda bi, c: (bi * nc + c, 0), bi) for bi in range(BATCH)]
    o, s = pl.pallas_call(
        _gla_prompt_body,
        grid=(nc,),
        in_specs=([pl.BlockSpec((GLA_C, GLA_MAIN_DIM), r) for r in rows]
                  + [pl.BlockSpec((GLA_C, GLA_KEY_DIM), r) for r in rows]
                  + [pl.BlockSpec((1, GLA_DV), lambda c: (0, 0))]),
        out_specs=[
            pl.BlockSpec((BATCH, GLA_C, GLA_VAL_DIM), lambda c: (0, c, 0)),
            pl.BlockSpec((BATCH, GLA_HEADS, GLA_DK, GLA_DV), lambda c: (0, 0, 0, 0)),
        ],
        out_shape=[
            jax.ShapeDtypeStruct((BATCH, SEQ, GLA_VAL_DIM), BF16),
            jax.ShapeDtypeStruct((BATCH, GLA_HEADS, GLA_DK, GLA_DV), F32),
        ],
        scratch_shapes=[pltpu.VMEM((BATCH, GLA_HEADS, GLA_DV, GLA_DK), F32)],
        compiler_params=_cparams(("arbitrary",)),
        name="gla_prompt",
    )(*([proj] * BATCH + [log_a] * BATCH + [norm_w.reshape(1, GLA_DV)]))
    return o.reshape(N_PROMPT, GLA_VAL_DIM), s


GLA_SEQS_PER_STEP = 4


def _gla_sample_body(proj_ref, la_ref, s0_ref, nw_ref, o_ref, s_ref):
    ones = jnp.ones((DEC_SEQ, LANES), BF16)
    proj = proj_ref[...].astype(F32)
    la = la_ref[...]
    for i in range(GLA_SEQS_PER_STEP):
        rows_i = slice(i * DEC_SEQ, (i + 1) * DEC_SEQ)
        for h in range(GLA_HEADS):
            q = proj[rows_i, h * GLA_DK:(h + 1) * GLA_DK]
            k = proj[rows_i, GLA_KEY_DIM + h * GLA_DK:GLA_KEY_DIM + (h + 1) * GLA_DK]
            v0 = 2 * GLA_KEY_DIM + h * GLA_DV
            v = proj[rows_i, v0:v0 + GLA_DV].astype(BF16)
            gate = proj[rows_i, v0 + GLA_VAL_DIM:v0 + GLA_VAL_DIM + GLA_DV]
            g = la[rows_i, h * GLA_DK:(h + 1) * GLA_DK]
            rows = [g[0:1]]
            for t in range(1, DEC_SEQ):
                rows.append(rows[-1] + g[t:t + 1])
            b = jnp.concatenate(rows, axis=0)
            b_last = rows[-1]
            qe = (q * GLA_DK ** -0.5 * jnp.exp(b)).astype(BF16)
            ke = (k * jnp.exp(-b)).astype(BF16)
            a = _causal(lax.dot_general(qe, ke, _NT, preferred_element_type=F32))
            s0 = s0_ref[i, h]
            o = (jnp.dot(a.astype(BF16), v, preferred_element_type=F32)
                 + jnp.dot(qe, s0.astype(BF16), preferred_element_type=F32))
            kd = (k * jnp.exp(b_last - b)).astype(BF16)
            dsum = sum(lax.dot_general(piece, ones, _TN, preferred_element_type=F32) for piece in _split3(g))
            decay = jnp.concatenate([jnp.exp(dsum)] * (GLA_DV // LANES), axis=1)
            s_ref[i, h] = s0 * decay + lax.dot_general(kd, v, _TN, preferred_element_type=F32)
            o_ref[rows_i, h * GLA_DV:(h + 1) * GLA_DV] = _gla_out(o, gate, nw_ref[...])


def _gla_sample(proj, log_a, state, norm_w):
    n = GLA_SEQS_PER_STEP
    rows = n * DEC_SEQ
    first = N_PROMPT // rows
    seq4 = lambda s: (s, 0, 0, 0)
    return pl.pallas_call(
        _gla_sample_body,
        grid=(DEC_BATCH // n,),
        in_specs=[
            pl.BlockSpec((rows, GLA_MAIN_DIM), lambda s: (first + s, 0)),
            pl.BlockSpec((rows, GLA_KEY_DIM), lambda s: (first + s, 0)),
            pl.BlockSpec((n, GLA_HEADS, GLA_DK, GLA_DV), seq4),
            pl.BlockSpec((1, GLA_DV), lambda s: (0, 0)),
        ],
        out_specs=[
            pl.BlockSpec((rows, GLA_VAL_DIM), lambda s: (s, 0)),
            pl.BlockSpec((n, GLA_HEADS, GLA_DK, GLA_DV), seq4),
        ],
        out_shape=[
            jax.ShapeDtypeStruct((N_SAMPLE, GLA_VAL_DIM), F32),
            jax.ShapeDtypeStruct((DEC_BATCH, GLA_HEADS, GLA_DK, GLA_DV), F32),
        ],
        compiler_params=_cparams(("arbitrary",)),
        name="gla_sample",
    )(proj, log_a, state, norm_w.reshape(1, GLA_DV))


CAST_ROWS = 256


def _cast_body(w_ref, o_ref):
    o_ref[...] = w_ref[0].astype(BF16)


def _layer_bf16(w, layer):
    _, rows, cols = w.shape
    return pl.pallas_call(
        _cast_body,
        grid=(rows // CAST_ROWS,),
        in_specs=[pl.BlockSpec((1, CAST_ROWS, cols), lambda i: (layer, i, 0))],
        out_specs=pl.BlockSpec((CAST_ROWS, cols), lambda i: (i, 0)),
        out_shape=jax.ShapeDtypeStruct((rows, cols), BF16),
        compiler_params=_cparams(("arbitrary",)),
        name="cast_bf16",
    )(w)


def _slot_rows_body(*refs):
    o_ref = refs[-1]
    for u, w_ref in enumerate(refs[:-1]):
        o_ref[u * HEAD_DIM:(u + 1) * HEAD_DIM] = w_ref[0].astype(BF16)


def _slot_rows_bf16(w, layer):
    def src(u):
        g, hh = u // KV_PER_STEP, u % KV_PER_STEP
        return lambda pair: (layer, (pair * KV_PER_STEP + hh) * GROUP + g, 0)

    cols = w.shape[2]
    return pl.pallas_call(
        _slot_rows_body,
        grid=(N_PAIRS,),
        in_specs=[pl.BlockSpec((1, HEAD_DIM, cols), src(u)) for u in range(PAIR_SLOTS)],
        out_specs=pl.BlockSpec((PAIR_SLOTS * HEAD_DIM, cols), lambda pair: (pair, 0)),
        out_shape=jax.ShapeDtypeStruct((Q_DIM, cols), BF16),
        compiler_params=_cparams(("arbitrary",)),
        name="slot_rows_bf16",
    )(*([w] * PAIR_SLOTS))


def _swa_layer(x, cache_k, cache_v, norm_w, w_qkv, b_qkv, j, w_o_all, b_o, sinks, rel_bias):
    slots = (N_PAIRS, KV_PER_STEP, GROUP, HEAD_DIM)
    q_scale = HEAD_DIM ** -0.5
    assert math.frexp(q_scale)[0] == 0.5, q_scale
    w_q = (w_qkv[:, :Q_DIM] * q_scale).reshape((D_MODEL,) + slots).transpose(0, 1, 3, 2, 4).reshape(D_MODEL, Q_DIM)
    b_q = (b_qkv[:Q_DIM] * q_scale).reshape(slots).transpose(0, 2, 1, 3).reshape(Q_DIM)
    w_qkv_s = jnp.concatenate([w_q, w_qkv[:, Q_DIM:]], axis=1).astype(BF16)
    b_qkv_s = jnp.concatenate([b_q, b_qkv[Q_DIM:]])
    w_o_s = _slot_rows_bf16(w_o_all, j)

    q, kv = _norm_proj(x, norm_w, w_qkv_s, b_qkv_s, ((Q_DIM, BF16), (2 * KV_COLS, F32)), "swa_qkv")
    bias_tbl = _bias_table(rel_bias)
    o_p = _swa_prompt(q, kv, bias_tbl, sinks)

    q_rows = q[N_PROMPT:].astype(F32).reshape(DEC_BATCH, S_ROWS, HEAD_DIM)
    kv_s = kv[N_PROMPT:].reshape(DEC_BATCH, DEC_SEQ, 2 * KV_COLS)
    pad8 = ((0, 0), (0, SUBLANES - DEC_SEQ), (0, 0))
    k_new8 = jnp.pad(kv_s[..., :KV_COLS], pad8)
    v_new8 = jnp.pad(kv_s[..., KV_COLS:], pad8)
    bias_s = bias_tbl[:, :, :DEC_SEQ].transpose(2, 0, 1).reshape(S_ROWS, 2 * WINDOW)
    slot_heads = np.array([_slot_head(s) for s in range(N_HEADS)])
    sink_col = jnp.tile(sinks[slot_heads], DEC_SEQ).reshape(S_ROWS, 1)
    o_s, k_s, v_s = _swa_sample(q_rows, k_new8, v_new8,
                                cache_k.reshape(DEC_BATCH, WINDOW, KV_COLS),
                                cache_v.reshape(DEC_BATCH, WINDOW, KV_COLS), bias_s, sink_col)
    o_s = o_s.reshape(N_SAMPLE, Q_DIM)
    x = _proj_res((o_p, o_s), w_o_s, b_o, x, "swa_out")

    kv_p = jnp.stack([kv[(b + 1) * SEQ - WINDOW:(b + 1) * SEQ] for b in range(BATCH)])
    k_p = kv_p[..., :KV_COLS].reshape(BATCH, WINDOW, N_KV_HEADS, HEAD_DIM)
    v_p = kv_p[..., KV_COLS:].reshape(BATCH, WINDOW, N_KV_HEADS, HEAD_DIM)
    shape_s = (DEC_BATCH, WINDOW, N_KV_HEADS, HEAD_DIM)
    return x, k_p, v_p, k_s.reshape(shape_s), v_s.reshape(shape_s)


def _gla_layer(x, state, norm_w, j, w_in_all, w_gk2, b_gk, gnorm, w_o_all):
    proj, log_a = _norm_proj(x, norm_w, w_in_all[j].astype(BF16), jnp.zeros((GLA_MAIN_DIM,), F32),
                             ((GLA_MAIN_DIM, BF16),), "gla_in", gate=(w_gk2.astype(BF16), b_gk))
    o_p, s_p = _gla_prompt(proj, log_a, gnorm)
    o_s, s_s = _gla_sample(proj, log_a, state, gnorm)
    x = _proj_res((o_p, o_s), _layer_bf16(w_o_all, j), jnp.zeros((D_MODEL,), F32), x, "gla_out")
    return x, s_p, s_s


def kernel(x_prompt, x_sample, cache_swa_k, cache_swa_v, state_gla, norm_ffn1, ffn1_w_gate, ffn1_w_up,
           ffn1_w_down, norm_mix, norm_ffn2, ffn2_w_gate, ffn2_w_up, ffn2_w_down, norm_final, rel_bias,
           swa_w_qkv, swa_b_qkv, swa_w_o, swa_b_o, swa_sinks, gla_w_in, gla_w_gk2, gla_b_gk, gla_norm,
           gla_w_o):
    x = (x_prompt.reshape(N_PROMPT, D_MODEL), x_sample.reshape(N_SAMPLE, D_MODEL))
    swa_kp, swa_vp, swa_ks, swa_vs, gla_sp, gla_ss = [], [], [], [], [], []
    for i in range(DEPTH):
        x = _ffn_pair(x, i, norm_ffn1[i], ffn1_w_gate, ffn1_w_up, ffn1_w_down)
        j = i // 2
        if i % 2 == 0:
            x, kp, vp, ks, vs = _swa_layer(x, cache_swa_k[j], cache_swa_v[j], norm_mix[i], swa_w_qkv[j],
                                           swa_b_qkv[j], j, swa_w_o, swa_b_o[j], swa_sinks[j], rel_bias)
            swa_kp.append(kp)
            swa_vp.append(vp)
            swa_ks.append(ks)
            swa_vs.append(vs)
        else:
            x, sp, ss = _gla_layer(x, state_gla[j], norm_mix[i], j, gla_w_in, gla_w_gk2[j], gla_b_gk[j],
                                   gla_norm[j], gla_w_o)
            gla_sp.append(sp)
            gla_ss.append(ss)
        final_w = norm_final if i == DEPTH - 1 else None
        x = _ffn_pair(x, i, norm_ffn2[i], ffn2_w_gate, ffn2_w_up, ffn2_w_down, final_w)
    y_prompt = x[0].reshape(BATCH, SEQ, D_MODEL)
    y_sample = x[1].reshape(DEC_BATCH, DEC_SEQ, D_MODEL)
    return (y_prompt, y_sample, jnp.stack(swa_kp), jnp.stack(swa_vp), jnp.stack(swa_ks), jnp.stack(swa_vs),
            jnp.stack(gla_sp), jnp.stack(gla_ss))
```

```python
import functools
import math

import numpy as np
import jax
import jax.numpy as jnp
from jax import lax
from jax.experimental import pallas as pl
from jax.experimental.pallas import tpu as pltpu

F32 = jnp.float32
BF16 = jnp.bfloat16

D_MODEL = 2048
BATCH = 2
SEQ = 4096
DEPTH = 2
DEC_BATCH = 128
DEC_SEQ = 4
RMS_EPS = 1e-6
D_FF = 5632
N_HEADS = 32
N_KV_HEADS = 8
HEAD_DIM = 64
GROUP = N_HEADS // N_KV_HEADS
WINDOW = 128
NUM_BUCKETS = 32
MAX_DISTANCE = 128
NEG_INF = -1e30
GLA_HEADS = 4
GLA_DK = 256
GLA_DV = 512
GLA_KEY_DIM = GLA_HEADS * GLA_DK
GLA_VAL_DIM = GLA_HEADS * GLA_DV
GATE_RANK = 16
GATE_NORMALIZER = 16.0
GLA_MAIN_DIM = 2 * GLA_KEY_DIM + 2 * GLA_VAL_DIM
Q_DIM = N_HEADS * HEAD_DIM
KV_COLS = N_KV_HEADS * HEAD_DIM

N_PROMPT = BATCH * SEQ
N_SAMPLE = DEC_BATCH * DEC_SEQ
N_TOK = N_PROMPT + N_SAMPLE

LANES = 128
SUBLANES = 8
VMEM_LIMIT = 56 * 1024 * 1024
BIG_VMEM_LIMIT = 60 * 1024 * 1024

TM = 512
TN = 512
GLA_C = 64
NB = SEQ // WINDOW
N_PROMPT_TILES = N_PROMPT // TM


def _rms(x, w):
    return x * lax.rsqrt(jnp.mean(x * x, axis=-1, keepdims=True) + RMS_EPS) * w


def _silu(x):
    return x * jax.nn.sigmoid(x)


def _cparams(sem, vmem_limit=VMEM_LIMIT):
    return pltpu.CompilerParams(dimension_semantics=sem, vmem_limit_bytes=vmem_limit)


def _row_pair_specs(width):
    return [pl.BlockSpec((TM, width), lambda i: (jnp.minimum(i, N_PROMPT_TILES - 1), 0)),
            pl.BlockSpec((TM, width), lambda i: (jnp.maximum(i - N_PROMPT_TILES, 0), 0))]


def _on_row_source(fn, *ref_pairs):
    i = pl.program_id(0)
    pl.when(i < N_PROMPT_TILES)(lambda: fn(*[p[0] for p in ref_pairs]))
    pl.when(i >= N_PROMPT_TILES)(lambda: fn(*[p[1] for p in ref_pairs]))


FFN_TM = 1024
FFN_TF = 256
FFN_TF_SAMPLE = 512


def _ffn_body(final_norm, emit_w, x_ref, nw_ref, wg_ref, wu_ref, wd_ref, *rest):
    rest = list(rest)
    fw_ref = rest.pop(0) if final_norm else None
    o_ref = rest.pop(0)
    h_ref = rest.pop()
    j = pl.program_id(1)

    @pl.when(j == 0)
    def _():
        x = x_ref[...]
        h_ref[...] = _rms(x, nw_ref[...]).astype(BF16)
        o_ref[...] = x

    if emit_w:
        wgo_ref, wuo_ref, wdo_ref = rest
        wgo_ref[...] = wg_ref[0].astype(BF16)
        wuo_ref[...] = wu_ref[0].astype(BF16)
        wdo_ref[...] = wd_ref[0].astype(BF16)
        wg, wu, wd = wgo_ref[...], wuo_ref[...], wdo_ref[...]
    else:
        wg, wu, wd = wg_ref[...], wu_ref[...], wd_ref[...]
    h = h_ref[...]
    g = jnp.dot(h, wg, preferred_element_type=F32)
    u = jnp.dot(h, wu, preferred_element_type=F32)
    a = (_silu(g) * (0.5 * u)).astype(BF16)
    o_ref[...] += jnp.dot(a, wd, preferred_element_type=F32)

    if final_norm:
        @pl.when(j == pl.num_programs(1) - 1)
        def _():
            o_ref[...] = _rms(o_ref[...], fw_ref[...])


def _ffn(x, layer, nw, wg, wu, wd, final_w=None):
    m = x.shape[0]
    emit_w = wg.dtype == F32
    tm = min(FFN_TM, m)
    tf = FFN_TF if emit_w else FFN_TF_SAMPLE
    nj = D_FF // tf
    final_norm = final_w is not None
    vec = pl.BlockSpec((1, D_MODEL), lambda i, j: (0, 0))
    if emit_w:
        w_specs = [
            pl.BlockSpec((1, D_MODEL, tf), lambda i, j: (layer, 0, j)),
            pl.BlockSpec((1, D_MODEL, tf), lambda i, j: (layer, 0, j)),
            pl.BlockSpec((1, tf, D_MODEL), lambda i, j: (layer, j, 0)),
        ]
    else:
        w_specs = [
            pl.BlockSpec((D_MODEL, tf), lambda i, j: (0, j)),
            pl.BlockSpec((D_MODEL, tf), lambda i, j: (0, j)),
            pl.BlockSpec((tf, D_MODEL), lambda i, j: (j, 0)),
        ]
    in_specs = [pl.BlockSpec((tm, D_MODEL), lambda i, j: (i, 0)), vec] + w_specs
    args = [x, nw.reshape(1, D_MODEL), wg, wu, wd]
    if final_norm:
        in_specs.append(vec)
        args.append(final_w.reshape(1, D_MODEL))
    out_specs = [pl.BlockSpec((tm, D_MODEL), lambda i, j: (i, 0))]
    out_shape = [jax.ShapeDtypeStruct((m, D_MODEL), F32)]
    if emit_w:
        once = lambda i, j: jnp.where(i == 0, j, nj - 1)
        out_specs += [
            pl.BlockSpec((D_MODEL, tf), lambda i, j: (0, once(i, j))),
            pl.BlockSpec((D_MODEL, tf), lambda i, j: (0, once(i, j))),
            pl.BlockSpec((tf, D_MODEL), lambda i, j: (once(i, j), 0)),
        ]
        out_shape += [
            jax.ShapeDtypeStruct((D_MODEL, D_FF), BF16),
            jax.ShapeDtypeStruct((D_MODEL, D_FF), BF16),
            jax.ShapeDtypeStruct((D_FF, D_MODEL), BF16),
        ]
    outs = pl.pallas_call(
        functools.partial(_ffn_body, final_norm, emit_w),
        grid=(m // tm, nj),
        in_specs=in_specs,
        out_specs=out_specs,
        out_shape=out_shape,
        scratch_shapes=[pltpu.VMEM((tm, D_MODEL), BF16)],
        compiler_params=_cparams(("arbitrary", "arbitrary"), BIG_VMEM_LIMIT),
        name="ffn",
    )(*args)
    return outs[0], tuple(outs[1:])


def _ffn_pair(x_pair, layer, nw, wg, wu, wd, final_w=None):
    y_prompt, w_bf16 = _ffn(x_pair[0], layer, nw, wg, wu, wd, final_w)
    y_sample, _ = _ffn(x_pair[1], layer, nw, *w_bf16, final_w)
    return y_prompt, y_sample


def _log_sigmoid(x):
    return jnp.minimum(x, 0.0) - jnp.log(1.0 + jnp.exp(-jnp.abs(x)))


def _norm_proj_body(segments, with_gate, xp_ref, xs_ref, nw_ref, w_ref, b_ref, *rest):
    if with_gate:
        w2_ref, bg_ref = rest[:2]
        rest = rest[2:]
    o_refs = rest[:len(segments)]

    def run(x_ref):
        h = _rms(x_ref[...], nw_ref[...]).astype(BF16)
        col = 0
        for (width, _), o_ref in zip(segments, o_refs):
            for c in range(0, width, TN):
                acc = jnp.dot(h, w_ref[:, col + c:col + c + TN], preferred_element_type=F32)
                o_ref[:, c:c + TN] = (acc + b_ref[:, col + c:col + c + TN]).astype(o_ref.dtype)
            col += width
        if with_gate:
            gk = jnp.dot(h, w_ref[:, col:col + GATE_RANK], preferred_element_type=F32)
            z = jnp.dot(gk.astype(BF16), w2_ref[...], preferred_element_type=F32) + bg_ref[...]
            rest[-1][...] = _log_sigmoid(z) / GATE_NORMALIZER
    _on_row_source(run, (xp_ref, xs_ref))


def _norm_proj(xs, nw, w, b, segments, name, gate=None):
    n = sum(width for width, _ in segments)
    in_specs = _row_pair_specs(D_MODEL) + [
        pl.BlockSpec((1, D_MODEL), lambda i: (0, 0)),
        pl.BlockSpec(w.shape, lambda i: (0, 0), pipeline_mode=pl.Buffered(1)),
        pl.BlockSpec((1, n), lambda i: (0, 0)),
    ]
    args = [*xs, nw.reshape(1, D_MODEL), w, b.reshape(1, n)]
    out_specs = [pl.BlockSpec((TM, width), lambda i: (i, 0)) for width, _ in segments]
    out_shape = [jax.ShapeDtypeStruct((N_TOK, width), dtype) for width, dtype in segments]
    if gate is not None:
        w2, bg = gate
        assert w.shape[1] == n + GATE_RANK, w.shape
        in_specs += [
            pl.BlockSpec((GATE_RANK, GLA_KEY_DIM), lambda i: (0, 0)),
            pl.BlockSpec((1, GLA_KEY_DIM), lambda i: (0, 0)),
        ]
        args += [w2, bg.reshape(1, GLA_KEY_DIM)]
        out_specs.append(pl.BlockSpec((TM, GLA_KEY_DIM), lambda i: (i, 0)))
        out_shape.append(jax.ShapeDtypeStruct((N_TOK, GLA_KEY_DIM), F32))
    return pl.pallas_call(
        functools.partial(_norm_proj_body, segments, gate is not None),
        grid=(N_TOK // TM,),
        in_specs=in_specs,
        out_specs=out_specs,
        out_shape=out_shape,
        compiler_params=_cparams(("arbitrary",), BIG_VMEM_LIMIT),
        name=name,
    )(*args)


def _proj_res_body(ap_ref, as_ref, w_ref, b_ref, rp_ref, rs_ref, op_ref, os_ref):
    def run(a_ref, r_ref, o_ref):
        a = a_ref[...].astype(BF16)
        for c in range(0, D_MODEL, TN):
            acc = jnp.dot(a, w_ref[:, c:c + TN], preferred_element_type=F32)
            o_ref[:, c:c + TN] = r_ref[:, c:c + TN] + acc + b_ref[:, c:c + TN]
    _on_row_source(run, (ap_ref, as_ref), (rp_ref, rs_ref), (op_ref, os_ref))


def _proj_res(a_pair, w, b, res_pair, name):
    k = w.shape[0]
    return pl.pallas_call(
        _proj_res_body,
        grid=(N_TOK // TM,),
        in_specs=_row_pair_specs(k) + [
            pl.BlockSpec((k, D_MODEL), lambda i: (0, 0), pipeline_mode=pl.Buffered(1)),
            pl.BlockSpec((1, D_MODEL), lambda i: (0, 0)),
        ] + _row_pair_specs(D_MODEL),
        out_specs=_row_pair_specs(D_MODEL),
        out_shape=[jax.ShapeDtypeStruct((N_PROMPT, D_MODEL), F32), jax.ShapeDtypeStruct((N_SAMPLE, D_MODEL), F32)],
        compiler_params=_cparams(("arbitrary",)),
        name=name,
    )(*a_pair, w, b.reshape(1, D_MODEL), *res_pair)


def _t5_bucket_table():
    i = np.arange(WINDOW)[None, :]
    j = np.arange(2 * WINDOW)[:, None]
    n = np.maximum(WINDOW + i - j, 0)
    max_exact = NUM_BUCKETS // 2
    nf = np.maximum(n, 1).astype(np.float32)
    large = max_exact + (np.log(nf / np.float32(max_exact)) / np.float32(math.log(MAX_DISTANCE / max_exact))
                         * np.float32(NUM_BUCKETS - max_exact)).astype(np.int32)
    large = np.minimum(large, NUM_BUCKETS - 1)
    return np.where(n < max_exact, n, large).astype(np.int32)


KV_PER_STEP = LANES // HEAD_DIM
N_PAIRS = N_KV_HEADS // KV_PER_STEP
Q_COLS_PER_STEP = KV_PER_STEP * GROUP * HEAD_DIM
PAIR_SLOTS = GROUP * KV_PER_STEP


def _slot_head(slot):
    pair = slot // PAIR_SLOTS
    g = (slot // KV_PER_STEP) % GROUP
    hh = slot % KV_PER_STEP
    return (pair * KV_PER_STEP + hh) * GROUP + g


BIAS_SLOTS_PER_STEP = 8


def _bias_table_body(bucket_ref, rb_ref, o_ref):
    bucket = bucket_ref[...]
    j = lax.broadcasted_iota(jnp.int32, (2 * WINDOW, WINDOW), 0)
    i = lax.broadcasted_iota(jnp.int32, (2 * WINDOW, WINDOW), 1)
    dist = WINDOW + i - j
    in_window = (dist >= 0) & (dist < WINDOW)
    for t in range(BIAS_SLOTS_PER_STEP):
        h = _slot_head(pl.program_id(0) * BIAS_SLOTS_PER_STEP + t)
        acc = jnp.zeros((2 * WINDOW, WINDOW), F32)
        for b in range(NUM_BUCKETS):
            acc = jnp.where(bucket == b, rb_ref[b, h], acc)
        o_ref[t] = jnp.where(in_window, acc, NEG_INF)


def _bias_table(rel_bias):
    n = BIAS_SLOTS_PER_STEP
    return pl.pallas_call(
        _bias_table_body,
        grid=(N_HEADS // n,),
        in_specs=[
            pl.BlockSpec((2 * WINDOW, WINDOW), lambda h: (0, 0)),
            pl.BlockSpec(memory_space=pltpu.SMEM),
        ],
        out_specs=pl.BlockSpec((n, 2 * WINDOW, WINDOW), lambda h: (h, 0, 0)),
        out_shape=jax.ShapeDtypeStruct((N_HEADS, 2 * WINDOW, WINDOW), F32),
        name="bias_table",
    )(jnp.asarray(_t5_bucket_table()), rel_bias)


def _softmax_with_sink(s, sink_col):
    m = jnp.maximum(jnp.max(s, axis=-1, keepdims=True), sink_col)
    p = jnp.exp(s - m)
    denom = jnp.sum(p, axis=-1, keepdims=True) + jnp.exp(sink_col - m)
    return p, 1.0 / denom


PAIR_COLS = PAIR_SLOTS * WINDOW
ONES_ROWS = 16
PAIRS_PER_STEP = 4


def _swa_prompt_body(sink_ref, q_ref, kp_ref, ko_ref, vp_ref, vo_ref, bias_ref, o_ref):
    first = pl.program_id(2) == 0
    args = (sink_ref, q_ref, kp_ref, ko_ref, vp_ref, vo_ref, bias_ref, o_ref)
    pl.when(first)(lambda: _swa_prompt_block(True, *args))
    pl.when(jnp.logical_not(first))(lambda: _swa_prompt_block(False, *args))


def _swa_prompt_block(mask_prev, sink_ref, q_ref, kp_ref, ko_ref, vp_ref, vo_ref, bias_ref, o_ref):
    head_a = lax.broadcasted_iota(jnp.int32, (WINDOW, LANES), 1) < HEAD_DIM
    for t in range(PAIRS_PER_STEP):
        pair = pl.program_id(0) * PAIRS_PER_STEP + t
        lanes = slice(t * LANES, (t + 1) * LANES)
        q0 = t * Q_COLS_PER_STEP
        k = jnp.concatenate([kp_ref[:, lanes], ko_ref[:, lanes]], axis=0).astype(BF16)
        v = jnp.concatenate([vp_ref[:, lanes], vo_ref[:, lanes]], axis=0)
        vt = jnp.concatenate([v.T, jnp.ones((ONES_ROWS, 2 * WINDOW), F32)], axis=0).astype(BF16)
        parts = []
        for g in range(GROUP):
            qg = q_ref[:, q0 + g * LANES:q0 + (g + 1) * LANES]
            zero = jnp.zeros_like(qg)
            parts += [jnp.where(head_a, qg, zero), jnp.where(head_a, zero, qg)]
        qbd = jnp.concatenate(parts, axis=0)
        st = lax.dot_general(k, qbd, (((1,), (1,)), ((), ())), preferred_element_type=F32)
        st = st + jnp.concatenate([bias_ref[t * PAIR_SLOTS + u] for u in range(PAIR_SLOTS)], axis=1)
        if mask_prev:
            st = jnp.concatenate([st[:WINDOW] + NEG_INF, st[WINDOW:]], axis=0)
        sink_row = jnp.concatenate(
            [jnp.full((1, WINDOW), sink_ref[_slot_head(pair * PAIR_SLOTS + slot)], F32)
             for slot in range(PAIR_SLOTS)], axis=1)
        m = jnp.maximum(jnp.max(st, axis=0, keepdims=True), sink_row)
        pt = jnp.exp(st - m).astype(BF16)
        oa = jnp.dot(vt, pt, preferred_element_type=F32)
        inv = 1.0 / (oa[LANES:LANES + 1] + jnp.exp(sink_row - m))
        o = oa[:LANES] * inv
        for g in range(GROUP):
            c = g * KV_PER_STEP * WINDOW
            ot = jnp.concatenate([o[:HEAD_DIM, c:c + WINDOW], o[HEAD_DIM:, c + WINDOW:c + 2 * WINDOW]], axis=0)
            o_ref[:, q0 + g * LANES:q0 + (g + 1) * LANES] = ot.T.astype(o_ref.dtype)


def _swa_prompt(q, kv, bias_tbl, sinks):
    n = PAIRS_PER_STEP
    v_col0 = KV_COLS // (n * LANES)

    def prev(p, b, i):
        return b * NB + jnp.maximum(i - 1, 0)

    return pl.pallas_call(
        _swa_prompt_body,
        grid=(N_PAIRS // n, BATCH, NB),
        in_specs=[
            pl.BlockSpec(memory_space=pltpu.SMEM),
            pl.BlockSpec((WINDOW, n * Q_COLS_PER_STEP), lambda p, b, i: (b * NB + i, p)),
            pl.BlockSpec((WINDOW, n * LANES), lambda p, b, i: (prev(p, b, i), p)),
            pl.BlockSpec((WINDOW, n * LANES), lambda p, b, i: (b * NB + i, p)),
            pl.BlockSpec((WINDOW, n * LANES), lambda p, b, i: (prev(p, b, i), v_col0 + p)),
            pl.BlockSpec((WINDOW, n * LANES), lambda p, b, i: (b * NB + i, v_col0 + p)),
            pl.BlockSpec((n * PAIR_SLOTS, 2 * WINDOW, WINDOW), lambda p, b, i: (p, 0, 0)),
        ],
        out_specs=pl.BlockSpec((WINDOW, n * Q_COLS_PER_STEP), lambda p, b, i: (b * NB + i, p)),
        out_shape=jax.ShapeDtypeStruct((N_PROMPT, Q_DIM), BF16),
        compiler_params=_cparams(("arbitrary", "arbitrary", "arbitrary")),
        name="swa_prompt",
    )(sinks, q, kv, kv, kv, kv, bias_tbl)


S_ROWS = N_HEADS * DEC_SEQ
SWA_SEQS_PER_STEP = 8


def _swa_sample_body(q_ref, kn_ref, vn_ref, ck_ref, cv_ref, bias_ref, sink_ref,
                     o_ref, ko_ref, vo_ref):
    slot = lax.broadcasted_iota(jnp.int32, (S_ROWS, KV_COLS), 0) % N_HEADS
    row_kv = slot // PAIR_SLOTS * KV_PER_STEP + slot % KV_PER_STEP
    col_kv = lax.broadcasted_iota(jnp.int32, (S_ROWS, KV_COLS), 1) // HEAD_DIM
    own = row_kv == col_kv
    pad = jnp.zeros((WINDOW - SUBLANES, KV_COLS), F32)
    keep = WINDOW - DEC_SEQ
    for i in range(SWA_SEQS_PER_STEP):
        x = q_ref[i].astype(BF16)
        xt = jnp.concatenate([x] * N_KV_HEADS, axis=1)
        qbd = jnp.where(own, xt, jnp.zeros_like(xt))
        kk = jnp.concatenate([ck_ref[i], kn_ref[i], pad], axis=0).astype(BF16)
        vv = jnp.concatenate([cv_ref[i], vn_ref[i], pad], axis=0).astype(BF16)
        s = lax.dot_general(qbd, kk, (((1,), (1,)), ((), ())), preferred_element_type=F32)
        s = s + bias_ref[...]
        p, inv = _softmax_with_sink(s, sink_ref[...])
        of = jnp.dot(p.astype(BF16), vv, preferred_element_type=F32)
        of = jnp.where(own, of, 0.0)
        o = of[:, 0:HEAD_DIM]
        for c in range(1, N_KV_HEADS):
            o = o + of[:, c * HEAD_DIM:(c + 1) * HEAD_DIM]
        o_ref[i] = o * inv
        ko_ref[i, 0:keep, :] = ck_ref[i, DEC_SEQ:WINDOW, :]
        ko_ref[i, keep:WINDOW, :] = kn_ref[i, 0:DEC_SEQ, :]
        vo_ref[i, 0:keep, :] = cv_ref[i, DEC_SEQ:WINDOW, :]
        vo_ref[i, keep:WINDOW, :] = vn_ref[i, 0:DEC_SEQ, :]


def _swa_sample(q_rows, k_new8, v_new8, cache_k, cache_v, bias_s, sink_col):
    n = SWA_SEQS_PER_STEP
    seq3 = lambda s: (s, 0, 0)
    full2 = lambda s: (0, 0)
    return pl.pallas_call(
        _swa_sample_body,
        grid=(DEC_BATCH // n,),
        in_specs=[
            pl.BlockSpec((n, S_ROWS, HEAD_DIM), seq3),
            pl.BlockSpec((n, SUBLANES, KV_COLS), seq3),
            pl.BlockSpec((n, SUBLANES, KV_COLS), seq3),
            pl.BlockSpec((n, WINDOW, KV_COLS), seq3),
            pl.BlockSpec((n, WINDOW, KV_COLS), seq3),
            pl.BlockSpec((S_ROWS, 2 * WINDOW), full2),
            pl.BlockSpec((S_ROWS, 1), full2),
        ],
        out_specs=[
            pl.BlockSpec((n, S_ROWS, HEAD_DIM), seq3),
            pl.BlockSpec((n, WINDOW, KV_COLS), seq3),
            pl.BlockSpec((n, WINDOW, KV_COLS), seq3),
        ],
        out_shape=[
            jax.ShapeDtypeStruct((DEC_BATCH, S_ROWS, HEAD_DIM), F32),
            jax.ShapeDtypeStruct((DEC_BATCH, WINDOW, KV_COLS), F32),
            jax.ShapeDtypeStruct((DEC_BATCH, WINDOW, KV_COLS), F32),
        ],
        compiler_params=_cparams(("arbitrary",)),
        name="swa_sample",
    )(q_rows, k_new8, v_new8, cache_k, cache_v, bias_s, sink_col)


def _gla_out(o, gate, norm_w):
    return _rms(o, norm_w) * _silu(gate)


def _split3(x):
    hi = x.astype(BF16)
    r = x - hi.astype(F32)
    mid = r.astype(BF16)
    lo = (r - mid.astype(F32)).astype(BF16)
    return hi, mid, lo


def _cumsum_rows(g):
    c = g.shape[0]
    tri = (lax.broadcasted_iota(jnp.int32, (c, c), 0) >= lax.broadcasted_iota(jnp.int32, (c, c), 1)).astype(BF16)
    return jnp.dot(jnp.concatenate([tri] * 3, axis=1), jnp.concatenate(_split3(g), axis=0),
                   preferred_element_type=F32)


def _causal(a):
    c = a.shape[0]
    keep = lax.broadcasted_iota(jnp.int32, (c, c), 0) >= lax.broadcasted_iota(jnp.int32, (c, c), 1)
    return jnp.where(keep, a, 0.0)


_NT = (((1,), (1,)), ((), ()))
_TN = (((0,), (0,)), ((), ()))


def _gla_prompt_body(*refs):
    proj_refs = refs[:BATCH]
    la_refs = refs[BATCH:2 * BATCH]
    nw_ref, o_ref, s_ref, st_ref = refs[2 * BATCH:]
    c = pl.program_id(0)

    @pl.when(c == 0)
    def _():
        st_ref[...] = jnp.zeros_like(st_ref)

    for bi in range(BATCH):
        p_ref = proj_refs[bi]
        b_all = _cumsum_rows(la_refs[bi][...])
        for h in range(GLA_HEADS):
            kc = slice(h * GLA_DK, (h + 1) * GLA_DK)
            v0 = 2 * GLA_KEY_DIM + h * GLA_DV
            b = b_all[:, kc]
            q = p_ref[:, kc].astype(F32) * GLA_DK ** -0.5
            k = p_ref[:, GLA_KEY_DIM + h * GLA_DK:GLA_KEY_DIM + (h + 1) * GLA_DK].astype(F32)
            v = p_ref[:, v0:v0 + GLA_DV]
            gate = p_ref[:, v0 + GLA_VAL_DIM:v0 + GLA_VAL_DIM + GLA_DV].astype(F32)
            qe = (q * jnp.exp(b)).astype(BF16)
            r = b[GLA_C // 2 - 1:GLA_C // 2, :]
            qr = (q * jnp.exp(b - r)).astype(BF16)
            kr = (k * jnp.exp(r - b)).astype(BF16)
            a = _causal(lax.dot_general(qr, kr, _NT, preferred_element_type=F32))
            st = st_ref[bi, h]
            o = (jnp.dot(a.astype(BF16), v, preferred_element_type=F32)
                 + lax.dot_general(qe, st.astype(BF16), _NT, preferred_element_type=F32))
            b_last = b[GLA_C - 1:GLA_C, :]
            kd = (k * jnp.exp(b_last - b)).astype(BF16)
            st_new = st * jnp.exp(b_last) + lax.dot_general(v, kd, _TN, preferred_element_type=F32)
            st_ref[bi, h] = st_new
            o_ref[bi, :, h * GLA_DV:(h + 1) * GLA_DV] = _gla_out(o, gate, nw_ref[...]).astype(o_ref.dtype)

    @pl.when(c == pl.num_programs(0) - 1)
    def _():
        for bi in range(BATCH):
            for h in range(GLA_HEADS):
                s_ref[bi, h] = st_ref[bi, h].T


def _gla_prompt(proj, log_a, norm_w):
    nc = SEQ // GLA_C
    rows = [functools.partial(lambda bi, c: (bi * nc + c, 0), bi) for bi in range(BATCH)]
    o, s = pl.pallas_call(
        _gla_prompt_body,
        grid=(nc,),
        in_specs=([pl.BlockSpec((GLA_C, GLA_MAIN_DIM), r) for r in rows]
                  + [pl.BlockSpec((GLA_C, GLA_KEY_DIM), r) for r in rows]
                  + [pl.BlockSpec((1, GLA_DV), lambda c: (0, 0))]),
        out_specs=[
            pl.BlockSpec((BATCH, GLA_C, GLA_VAL_DIM), lambda c: (0, c, 0)),
            pl.BlockSpec((BATCH, GLA_HEADS, GLA_DK, GLA_DV), lambda c: (0, 0, 0, 0)),
        ],
        out_shape=[
            jax.ShapeDtypeStruct((BATCH, SEQ, GLA_VAL_DIM), BF16),
            jax.ShapeDtypeStruct((BATCH, GLA_HEADS, GLA_DK, GLA_DV), F32),
        ],
        scratch_shapes=[pltpu.VMEM((BATCH, GLA_HEADS, GLA_DV, GLA_DK), F32)],
        compiler_params=_cparams(("arbitrary",)),
        name="gla_prompt",
    )(*([proj] * BATCH + [log_a] * BATCH + [norm_w.reshape(1, GLA_DV)]))
    return o.reshape(N_PROMPT, GLA_VAL_DIM), s


GLA_SEQS_PER_STEP = 4


def _gla_sample_body(proj_ref, la_ref, s0_ref, nw_ref, o_ref, s_ref):
    ones = jnp.ones((DEC_SEQ, LANES), BF16)
    proj = proj_ref[...].astype(F32)
    la = la_ref[...]
    for i in range(GLA_SEQS_PER_STEP):
        rows_i = slice(i * DEC_SEQ, (i + 1) * DEC_SEQ)
        for h in range(GLA_HEADS):
            q = proj[rows_i, h * GLA_DK:(h + 1) * GLA_DK]
            k = proj[rows_i, GLA_KEY_DIM + h * GLA_DK:GLA_KEY_DIM + (h + 1) * GLA_DK]
            v0 = 2 * GLA_KEY_DIM + h * GLA_DV
            v = proj[rows_i, v0:v0 + GLA_DV].astype(BF16)
            gate = proj[rows_i, v0 + GLA_VAL_DIM:v0 + GLA_VAL_DIM + GLA_DV]
            g = la[rows_i, h * GLA_DK:(h + 1) * GLA_DK]
            rows = [g[0:1]]
            for t in range(1, DEC_SEQ):
                rows.append(rows[-1] + g[t:t + 1])
            b = jnp.concatenate(rows, axis=0)
            b_last = rows[-1]
            qe = (q * GLA_DK ** -0.5 * jnp.exp(b)).astype(BF16)
            ke = (k * jnp.exp(-b)).astype(BF16)
            a = _causal(lax.dot_general(qe, ke, _NT, preferred_element_type=F32))
            s0 = s0_ref[i, h]
            o = (jnp.dot(a.astype(BF16), v, preferred_element_type=F32)
                 + jnp.dot(qe, s0.astype(BF16), preferred_element_type=F32))
            kd = (k * jnp.exp(b_last - b)).astype(BF16)
            dsum = sum(lax.dot_general(piece, ones, _TN, preferred_element_type=F32) for piece in _split3(g))
            decay = jnp.concatenate([jnp.exp(dsum)] * (GLA_DV // LANES), axis=1)
            s_ref[i, h] = s0 * decay + lax.dot_general(kd, v, _TN, preferred_element_type=F32)
            o_ref[rows_i, h * GLA_DV:(h + 1) * GLA_DV] = _gla_out(o, gate, nw_ref[...])


def _gla_sample(proj, log_a, state, norm_w):
    n = GLA_SEQS_PER_STEP
    rows = n * DEC_SEQ
    first = N_PROMPT // rows
    seq4 = lambda s: (s, 0, 0, 0)
    return pl.pallas_call(
        _gla_sample_body,
        grid=(DEC_BATCH // n,),
        in_specs=[
            pl.BlockSpec((rows, GLA_MAIN_DIM), lambda s: (first + s, 0)),
            pl.BlockSpec((rows, GLA_KEY_DIM), lambda s: (first + s, 0)),
            pl.BlockSpec((n, GLA_HEADS, GLA_DK, GLA_DV), seq4),
            pl.BlockSpec((1, GLA_DV), lambda s: (0, 0)),
        ],
        out_specs=[
            pl.BlockSpec((rows, GLA_VAL_DIM), lambda s: (s, 0)),
            pl.BlockSpec((n, GLA_HEADS, GLA_DK, GLA_DV), seq4),
        ],
        out_shape=[
            jax.ShapeDtypeStruct((N_SAMPLE, GLA_VAL_DIM), F32),
            jax.ShapeDtypeStruct((DEC_BATCH, GLA_HEADS, GLA_DK, GLA_DV), F32),
        ],
        compiler_params=_cparams(("arbitrary",)),
        name="gla_sample",
    )(proj, log_a, state, norm_w.reshape(1, GLA_DV))


CAST_ROWS = 512


def _cast_body(w_ref, o_ref):
    o_ref[...] = w_ref[0].astype(BF16)


def _layer_bf16(w, layer):
    _, rows, cols = w.shape
    return pl.pallas_call(
        _cast_body,
        grid=(rows // CAST_ROWS,),
        in_specs=[pl.BlockSpec((1, CAST_ROWS, cols), lambda i: (layer, i, 0))],
        out_specs=pl.BlockSpec((CAST_ROWS, cols), lambda i: (i, 0)),
        out_shape=jax.ShapeDtypeStruct((rows, cols), BF16),
        compiler_params=_cparams(("arbitrary",)),
        name="cast_bf16",
    )(w)


def _slot_rows_body(*refs):
    o_ref = refs[-1]
    for u, w_ref in enumerate(refs[:-1]):
        o_ref[u * HEAD_DIM:(u + 1) * HEAD_DIM] = w_ref[0].astype(BF16)


def _slot_rows_bf16(w, layer):
    def src(u):
        g, hh = u // KV_PER_STEP, u % KV_PER_STEP
        return lambda pair: (layer, (pair * KV_PER_STEP + hh) * GROUP + g, 0)

    cols = w.shape[2]
    return pl.pallas_call(
        _slot_rows_body,
        grid=(N_PAIRS,),
        in_specs=[pl.BlockSpec((1, HEAD_DIM, cols), src(u)) for u in range(PAIR_SLOTS)],
        out_specs=pl.BlockSpec((PAIR_SLOTS * HEAD_DIM, cols), lambda pair: (pair, 0)),
        out_shape=jax.ShapeDtypeStruct((Q_DIM, cols), BF16),
        compiler_params=_cparams(("arbitrary",)),
        name="slot_rows_bf16",
    )(*([w] * PAIR_SLOTS))


def _swa_layer(x, cache_k, cache_v, norm_w, w_qkv, b_qkv, j, w_o_all, b_o, sinks, rel_bias):
    slots = (N_PAIRS, KV_PER_STEP, GROUP, HEAD_DIM)
    q_scale = HEAD_DIM ** -0.5
    assert math.frexp(q_scale)[0] == 0.5, q_scale
    w_q = (w_qkv[:, :Q_DIM] * q_scale).reshape((D_MODEL,) + slots).transpose(0, 1, 3, 2, 4).reshape(D_MODEL, Q_DIM)
    b_q = (b_qkv[:Q_DIM] * q_scale).reshape(slots).transpose(0, 2, 1, 3).reshape(Q_DIM)
    w_qkv_s = jnp.concatenate([w_q, w_qkv[:, Q_DIM:]], axis=1).astype(BF16)
    b_qkv_s = jnp.concatenate([b_q, b_qkv[Q_DIM:]])
    w_o_s = _slot_rows_bf16(w_o_all, j)

    q, kv = _norm_proj(x, norm_w, w_qkv_s, b_qkv_s, ((Q_DIM, BF16), (2 * KV_COLS, F32)), "swa_qkv")
    bias_tbl = _bias_table(rel_bias)
    o_p = _swa_prompt(q, kv, bias_tbl, sinks)

    q_rows = q[N_PROMPT:].astype(F32).reshape(DEC_BATCH, S_ROWS, HEAD_DIM)
    kv_s = kv[N_PROMPT:].reshape(DEC_BATCH, DEC_SEQ, 2 * KV_COLS)
    pad8 = ((0, 0), (0, SUBLANES - DEC_SEQ), (0, 0))
    k_new8 = jnp.pad(kv_s[..., :KV_COLS], pad8)
    v_new8 = jnp.pad(kv_s[..., KV_COLS:], pad8)
    bias_s = bias_tbl[:, :, :DEC_SEQ].transpose(2, 0, 1).reshape(S_ROWS, 2 * WINDOW)
    slot_heads = np.array([_slot_head(s) for s in range(N_HEADS)])
    sink_col = jnp.tile(sinks[slot_heads], DEC_SEQ).reshape(S_ROWS, 1)
    o_s, k_s, v_s = _swa_sample(q_rows, k_new8, v_new8,
                                cache_k.reshape(DEC_BATCH, WINDOW, KV_COLS),
                                cache_v.reshape(DEC_BATCH, WINDOW, KV_COLS), bias_s, sink_col)
    o_s = o_s.reshape(N_SAMPLE, Q_DIM)
    x = _proj_res((o_p, o_s), w_o_s, b_o, x, "swa_out")

    kv_p = jnp.stack([kv[(b + 1) * SEQ - WINDOW:(b + 1) * SEQ] for b in range(BATCH)])
    k_p = kv_p[..., :KV_COLS].reshape(BATCH, WINDOW, N_KV_HEADS, HEAD_DIM)
    v_p = kv_p[..., KV_COLS:].reshape(BATCH, WINDOW, N_KV_HEADS, HEAD_DIM)
    shape_s = (DEC_BATCH, WINDOW, N_KV_HEADS, HEAD_DIM)
    return x, k_p, v_p, k_s.reshape(shape_s), v_s.reshape(shape_s)


def _gla_layer(x, state, norm_w, j, w_in_all, w_gk2, b_gk, gnorm, w_o_all):
    proj, log_a = _norm_proj(x, norm_w, w_in_all[j].astype(BF16), jnp.zeros((GLA_MAIN_DIM,), F32),
                             ((GLA_MAIN_DIM, BF16),), "gla_in", gate=(w_gk2.astype(BF16), b_gk))
    o_p, s_p = _gla_prompt(proj, log_a, gnorm)
    o_s, s_s = _gla_sample(proj, log_a, state, gnorm)
    x = _proj_res((o_p, o_s), _layer_bf16(w_o_all, j), jnp.zeros((D_MODEL,), F32), x, "gla_out")
    return x, s_p, s_s


def kernel(x_prompt, x_sample, cache_swa_k, cache_swa_v, state_gla, norm_ffn1, ffn1_w_gate, ffn1_w_up,
           ffn1_w_down, norm_mix, norm_ffn2, ffn2_w_gate, ffn2_w_up, ffn2_w_down, norm_final, rel_bias,
           swa_w_qkv, swa_b_qkv, swa_w_o, swa_b_o, swa_sinks, gla_w_in, gla_w_gk2, gla_b_gk, gla_norm,
           gla_w_o):
    x = (x_prompt.reshape(N_PROMPT, D_MODEL), x_sample.reshape(N_SAMPLE, D_MODEL))
    swa_kp, swa_vp, swa_ks, swa_vs, gla_sp, gla_ss = [], [], [], [], [], []
    for i in range(DEPTH):
        x = _ffn_pair(x, i, norm_ffn1[i], ffn1_w_gate, ffn1_w_up, ffn1_w_down)
        j = i // 2
        if i % 2 == 0:
            x, kp, vp, ks, vs = _swa_layer(x, cache_swa_k[j], cache_swa_v[j], norm_mix[i], swa_w_qkv[j],
                                           swa_b_qkv[j], j, swa_w_o, swa_b_o[j], swa_sinks[j], rel_bias)
            swa_kp.append(kp)
            swa_vp.append(vp)
            swa_ks.append(ks)
            swa_vs.append(vs)
        else:
            x, sp, ss = _gla_layer(x, state_gla[j], norm_mix[i], j, gla_w_in, gla_w_gk2[j], gla_b_gk[j],
                                   gla_norm[j], gla_w_o)
            gla_sp.append(sp)
            gla_ss.append(ss)
        final_w = norm_final if i == DEPTH - 1 else None
        x = _ffn_pair(x, i, norm_ffn2[i], ffn2_w_gate, ffn2_w_up, ffn2_w_down, final_w)
    y_prompt = x[0].reshape(BATCH, SEQ, D_MODEL)
    y_sample = x[1].reshape(DEC_BATCH, DEC_SEQ, D_MODEL)
    return (y_prompt, y_sample, jnp.stack(swa_kp), jnp.stack(swa_vp), jnp.stack(swa_ks), jnp.stack(swa_vs),
            jnp.stack(gla_sp), jnp.stack(gla_ss))
```

```python
import functools
import math

import numpy as np
import jax
import jax.numpy as jnp
from jax import lax
from jax.experimental import pallas as pl
from jax.experimental.pallas import tpu as pltpu

F32 = jnp.float32
BF16 = jnp.bfloat16

D_MODEL = 2048
BATCH = 2
SEQ = 4096
DEPTH = 2
DEC_BATCH = 128
DEC_SEQ = 4
RMS_EPS = 1e-6
D_FF = 5632
N_HEADS = 32
N_KV_HEADS = 8
HEAD_DIM = 64
GROUP = N_HEADS // N_KV_HEADS
WINDOW = 128
NUM_BUCKETS = 32
MAX_DISTANCE = 128
NEG_INF = -1e30
GLA_HEADS = 4
GLA_DK = 256
GLA_DV = 512
GLA_KEY_DIM = GLA_HEADS * GLA_DK
GLA_VAL_DIM = GLA_HEADS * GLA_DV
GATE_RANK = 16
GATE_NORMALIZER = 16.0
GLA_MAIN_DIM = 2 * GLA_KEY_DIM + 2 * GLA_VAL_DIM
Q_DIM = N_HEADS * HEAD_DIM
KV_COLS = N_KV_HEADS * HEAD_DIM

N_PROMPT = BATCH * SEQ
N_SAMPLE = DEC_BATCH * DEC_SEQ
N_TOK = N_PROMPT + N_SAMPLE

LANES = 128
SUBLANES = 8
VMEM_LIMIT = 56 * 1024 * 1024
BIG_VMEM_LIMIT = 60 * 1024 * 1024

TM = 512
TN = 512
GLA_C = 64
NB = SEQ // WINDOW
N_PROMPT_TILES = N_PROMPT // TM


def _rms(x, w):
    return x * lax.rsqrt(jnp.mean(x * x, axis=-1, keepdims=True) + RMS_EPS) * w


def _silu(x):
    return x * jax.nn.sigmoid(x)


def _cparams(sem, vmem_limit=VMEM_LIMIT):
    return pltpu.CompilerParams(dimension_semantics=sem, vmem_limit_bytes=vmem_limit)


def _row_pair_specs(width):
    return [pl.BlockSpec((TM, width), lambda i: (jnp.minimum(i, N_PROMPT_TILES - 1), 0)),
            pl.BlockSpec((TM, width), lambda i: (jnp.maximum(i - N_PROMPT_TILES, 0), 0))]


def _on_row_source(fn, *ref_pairs):
    i = pl.program_id(0)
    pl.when(i < N_PROMPT_TILES)(lambda: fn(*[p[0] for p in ref_pairs]))
    pl.when(i >= N_PROMPT_TILES)(lambda: fn(*[p[1] for p in ref_pairs]))


FFN_TM = 1024
FFN_TF = 256
FFN_TF_SAMPLE = 512


def _ffn_body(final_norm, emit_w, x_ref, nw_ref, wg_ref, wu_ref, wd_ref, *rest):
    rest = list(rest)
    fw_ref = rest.pop(0) if final_norm else None
    o_ref = rest.pop(0)
    h_ref = rest.pop()
    j = pl.program_id(1)

    def step(first):
        if first:
            h_ref[...] = _rms(x_ref[...], nw_ref[...]).astype(BF16)
        if emit_w:
            wgo_ref, wuo_ref, wdo_ref = rest
            wgo_ref[...] = wg_ref[0].astype(BF16)
            wuo_ref[...] = wu_ref[0].astype(BF16)
            wdo_ref[...] = wd_ref[0].astype(BF16)
            wg, wu, wd = wgo_ref[...], wuo_ref[...], wdo_ref[...]
        else:
            wg, wu, wd = wg_ref[...], wu_ref[...], wd_ref[...]
        h = h_ref[...]
        g = jnp.dot(h, wg, preferred_element_type=F32)
        u = jnp.dot(h, wu, preferred_element_type=F32)
        a = (_silu(g) * (0.5 * u)).astype(BF16)
        d = jnp.dot(a, wd, preferred_element_type=F32)
        o_ref[...] = (x_ref[...] if first else o_ref[...]) + d

    pl.when(j == 0)(lambda: step(True))
    pl.when(j > 0)(lambda: step(False))

    if final_norm:
        @pl.when(j == pl.num_programs(1) - 1)
        def _():
            o_ref[...] = _rms(o_ref[...], fw_ref[...])


def _ffn(x, layer, nw, wg, wu, wd, final_w=None):
    m = x.shape[0]
    emit_w = wg.dtype == F32
    tm = min(FFN_TM, m)
    tf = FFN_TF if emit_w else FFN_TF_SAMPLE
    nj = D_FF // tf
    final_norm = final_w is not None
    vec = pl.BlockSpec((1, D_MODEL), lambda i, j: (0, 0))
    if emit_w:
        w_specs = [
            pl.BlockSpec((1, D_MODEL, tf), lambda i, j: (layer, 0, j)),
            pl.BlockSpec((1, D_MODEL, tf), lambda i, j: (layer, 0, j)),
            pl.BlockSpec((1, tf, D_MODEL), lambda i, j: (layer, j, 0)),
        ]
    else:
        w_specs = [
            pl.BlockSpec((D_MODEL, tf), lambda i, j: (0, j)),
            pl.BlockSpec((D_MODEL, tf), lambda i, j: (0, j)),
            pl.BlockSpec((tf, D_MODEL), lambda i, j: (j, 0)),
        ]
    in_specs = [pl.BlockSpec((tm, D_MODEL), lambda i, j: (i, 0)), vec] + w_specs
    args = [x, nw.reshape(1, D_MODEL), wg, wu, wd]
    if final_norm:
        in_specs.append(vec)
        args.append(final_w.reshape(1, D_MODEL))
    out_specs = [pl.BlockSpec((tm, D_MODEL), lambda i, j: (i, 0))]
    out_shape = [jax.ShapeDtypeStruct((m, D_MODEL), F32)]
    if emit_w:
        once = lambda i, j: jnp.where(i == 0, j, nj - 1)
        out_specs += [
            pl.BlockSpec((D_MODEL, tf), lambda i, j: (0, once(i, j))),
            pl.BlockSpec((D_MODEL, tf), lambda i, j: (0, once(i, j))),
            pl.BlockSpec((tf, D_MODEL), lambda i, j: (once(i, j), 0)),
        ]
        out_shape += [
            jax.ShapeDtypeStruct((D_MODEL, D_FF), BF16),
            jax.ShapeDtypeStruct((D_MODEL, D_FF), BF16),
            jax.ShapeDtypeStruct((D_FF, D_MODEL), BF16),
        ]
    outs = pl.pallas_call(
        functools.partial(_ffn_body, final_norm, emit_w),
        grid=(m // tm, nj),
        in_specs=in_specs,
        out_specs=out_specs,
        out_shape=out_shape,
        scratch_shapes=[pltpu.VMEM((tm, D_MODEL), BF16)],
        compiler_params=_cparams(("arbitrary", "arbitrary"), BIG_VMEM_LIMIT),
        name="ffn",
    )(*args)
    return outs[0], tuple(outs[1:])


def _ffn_pair(x_pair, layer, nw, wg, wu, wd, final_w=None):
    y_prompt, w_bf16 = _ffn(x_pair[0], layer, nw, wg, wu, wd, final_w)
    y_sample, _ = _ffn(x_pair[1], layer, nw, *w_bf16, final_w)
    return y_prompt, y_sample


def _log_sigmoid(x):
    return jnp.minimum(x, 0.0) - jnp.log(1.0 + jnp.exp(-jnp.abs(x)))


def _norm_proj_body(segments, with_gate, xp_ref, xs_ref, nw_ref, w_ref, b_ref, *rest):
    if with_gate:
        w2_ref, bg_ref = rest[:2]
        rest = rest[2:]
    o_refs = rest[:len(segments)]

    def run(x_ref):
        h = _rms(x_ref[...], nw_ref[...]).astype(BF16)
        col = 0
        for (width, _), o_ref in zip(segments, o_refs):
            for c in range(0, width, TN):
                acc = jnp.dot(h, w_ref[:, col + c:col + c + TN], preferred_element_type=F32)
                o_ref[:, c:c + TN] = (acc + b_ref[:, col + c:col + c + TN]).astype(o_ref.dtype)
            col += width
        if with_gate:
            gk = jnp.dot(h, w_ref[:, col:col + GATE_RANK], preferred_element_type=F32)
            z = jnp.dot(gk.astype(BF16), w2_ref[...], preferred_element_type=F32) + bg_ref[...]
            rest[-1][...] = _log_sigmoid(z) / GATE_NORMALIZER
    _on_row_source(run, (xp_ref, xs_ref))


def _norm_proj(xs, nw, w, b, segments, name, gate=None):
    n = sum(width for width, _ in segments)
    in_specs = _row_pair_specs(D_MODEL) + [
        pl.BlockSpec((1, D_MODEL), lambda i: (0, 0)),
        pl.BlockSpec(w.shape, lambda i: (0, 0), pipeline_mode=pl.Buffered(1)),
        pl.BlockSpec((1, n), lambda i: (0, 0)),
    ]
    args = [*xs, nw.reshape(1, D_MODEL), w, b.reshape(1, n)]
    out_specs = [pl.BlockSpec((TM, width), lambda i: (i, 0)) for width, _ in segments]
    out_shape = [jax.ShapeDtypeStruct((N_TOK, width), dtype) for width, dtype in segments]
    if gate is not None:
        w2, bg = gate
        assert w.shape[1] == n + GATE_RANK, w.shape
        in_specs += [
            pl.BlockSpec((GATE_RANK, GLA_KEY_DIM), lambda i: (0, 0)),
            pl.BlockSpec((1, GLA_KEY_DIM), lambda i: (0, 0)),
        ]
        args += [w2, bg.reshape(1, GLA_KEY_DIM)]
        out_specs.append(pl.BlockSpec((TM, GLA_KEY_DIM), lambda i: (i, 0)))
        out_shape.append(jax.ShapeDtypeStruct((N_TOK, GLA_KEY_DIM), F32))
    return pl.pallas_call(
        functools.partial(_norm_proj_body, segments, gate is not None),
        grid=(N_TOK // TM,),
        in_specs=in_specs,
        out_specs=out_specs,
        out_shape=out_shape,
        compiler_params=_cparams(("arbitrary",), BIG_VMEM_LIMIT),
        name=name,
    )(*args)


def _proj_res_body(ap_ref, as_ref, w_ref, b_ref, rp_ref, rs_ref, op_ref, os_ref):
    def run(a_ref, r_ref, o_ref):
        a = a_ref[...].astype(BF16)
        for c in range(0, D_MODEL, TN):
            acc = jnp.dot(a, w_ref[:, c:c + TN], preferred_element_type=F32)
            o_ref[:, c:c + TN] = r_ref[:, c:c + TN] + acc + b_ref[:, c:c + TN]
    _on_row_source(run, (ap_ref, as_ref), (rp_ref, rs_ref), (op_ref, os_ref))


def _proj_res(a_pair, w, b, res_pair, name):
    k = w.shape[0]
    return pl.pallas_call(
        _proj_res_body,
        grid=(N_TOK // TM,),
        in_specs=_row_pair_specs(k) + [
            pl.BlockSpec((k, D_MODEL), lambda i: (0, 0), pipeline_mode=pl.Buffered(1)),
            pl.BlockSpec((1, D_MODEL), lambda i: (0, 0)),
        ] + _row_pair_specs(D_MODEL),
        out_specs=_row_pair_specs(D_MODEL),
        out_shape=[jax.ShapeDtypeStruct((N_PROMPT, D_MODEL), F32), jax.ShapeDtypeStruct((N_SAMPLE, D_MODEL), F32)],
        compiler_params=_cparams(("arbitrary",)),
        name=name,
    )(*a_pair, w, b.reshape(1, D_MODEL), *res_pair)


def _t5_bucket_table():
    i = np.arange(WINDOW)[None, :]
    j = np.arange(2 * WINDOW)[:, None]
    n = np.maximum(WINDOW + i - j, 0)
    max_exact = NUM_BUCKETS // 2
    nf = np.maximum(n, 1).astype(np.float32)
    large = max_exact + (np.log(nf / np.float32(max_exact)) / np.float32(math.log(MAX_DISTANCE / max_exact))
                         * np.float32(NUM_BUCKETS - max_exact)).astype(np.int32)
    large = np.minimum(large, NUM_BUCKETS - 1)
    return np.where(n < max_exact, n, large).astype(np.int32)


KV_PER_STEP = LANES // HEAD_DIM
N_PAIRS = N_KV_HEADS // KV_PER_STEP
Q_COLS_PER_STEP = KV_PER_STEP * GROUP * HEAD_DIM
PAIR_SLOTS = GROUP * KV_PER_STEP


def _slot_head(slot):
    pair = slot // PAIR_SLOTS
    g = (slot // KV_PER_STEP) % GROUP
    hh = slot % KV_PER_STEP
    return (pair * KV_PER_STEP + hh) * GROUP + g


BIAS_SLOTS_PER_STEP = 8


def _bias_table_body(bucket_ref, rb_ref, o_ref):
    bucket = bucket_ref[...]
    j = lax.broadcasted_iota(jnp.int32, (2 * WINDOW, WINDOW), 0)
    i = lax.broadcasted_iota(jnp.int32, (2 * WINDOW, WINDOW), 1)
    dist = WINDOW + i - j
    in_window = (dist >= 0) & (dist < WINDOW)
    for t in range(BIAS_SLOTS_PER_STEP):
        h = _slot_head(pl.program_id(0) * BIAS_SLOTS_PER_STEP + t)
        acc = jnp.zeros((2 * WINDOW, WINDOW), F32)
        for b in range(NUM_BUCKETS):
            acc = jnp.where(bucket == b, rb_ref[b, h], acc)
        o_ref[t] = jnp.where(in_window, acc, NEG_INF)


def _bias_table(rel_bias):
    n = BIAS_SLOTS_PER_STEP
    return pl.pallas_call(
        _bias_table_body,
        grid=(N_HEADS // n,),
        in_specs=[
            pl.BlockSpec((2 * WINDOW, WINDOW), lambda h: (0, 0)),
            pl.BlockSpec(memory_space=pltpu.SMEM),
        ],
        out_specs=pl.BlockSpec((n, 2 * WINDOW, WINDOW), lambda h: (h, 0, 0)),
        out_shape=jax.ShapeDtypeStruct((N_HEADS, 2 * WINDOW, WINDOW), F32),
        name="bias_table",
    )(jnp.asarray(_t5_bucket_table()), rel_bias)


def _softmax_with_sink(s, sink_col):
    m = jnp.maximum(jnp.max(s, axis=-1, keepdims=True), sink_col)
    p = jnp.exp(s - m)
    denom = jnp.sum(p, axis=-1, keepdims=True) + jnp.exp(sink_col - m)
    return p, 1.0 / denom


PAIR_COLS = PAIR_SLOTS * WINDOW
ONES_ROWS = 16
PAIRS_PER_STEP = 4


def _swa_prompt_body(sink_ref, q_ref, kp_ref, ko_ref, vp_ref, vo_ref, bias_ref, o_ref):
    first = pl.program_id(2) == 0
    args = (sink_ref, q_ref, kp_ref, ko_ref, vp_ref, vo_ref, bias_ref, o_ref)
    pl.when(first)(lambda: _swa_prompt_block(True, *args))
    pl.when(jnp.logical_not(first))(lambda: _swa_prompt_block(False, *args))


def _swa_prompt_block(mask_prev, sink_ref, q_ref, kp_ref, ko_ref, vp_ref, vo_ref, bias_ref, o_ref):
    head_a = lax.broadcasted_iota(jnp.int32, (WINDOW, LANES), 1) < HEAD_DIM
    for t in range(PAIRS_PER_STEP):
        pair = pl.program_id(0) * PAIRS_PER_STEP + t
        lanes = slice(t * LANES, (t + 1) * LANES)
        q0 = t * Q_COLS_PER_STEP
        k = jnp.concatenate([kp_ref[:, lanes], ko_ref[:, lanes]], axis=0).astype(BF16)
        v = jnp.concatenate([vp_ref[:, lanes], vo_ref[:, lanes]], axis=0)
        vt = jnp.concatenate([v.T, jnp.ones((ONES_ROWS, 2 * WINDOW), F32)], axis=0).astype(BF16)
        parts = []
        for g in range(GROUP):
            qg = q_ref[:, q0 + g * LANES:q0 + (g + 1) * LANES]
            zero = jnp.zeros_like(qg)
            parts += [jnp.where(head_a, qg, zero), jnp.where(head_a, zero, qg)]
        qbd = jnp.concatenate(parts, axis=0)
        st = lax.dot_general(k, qbd, (((1,), (1,)), ((), ())), preferred_element_type=F32)
        st = st + jnp.concatenate([bias_ref[t * PAIR_SLOTS + u] for u in range(PAIR_SLOTS)], axis=1)
        if mask_prev:
            st = jnp.concatenate([st[:WINDOW] + NEG_INF, st[WINDOW:]], axis=0)
        sink_row = jnp.concatenate(
            [jnp.full((1, WINDOW), sink_ref[_slot_head(pair * PAIR_SLOTS + slot)], F32)
             for slot in range(PAIR_SLOTS)], axis=1)
        m = jnp.maximum(jnp.max(st, axis=0, keepdims=True), sink_row)
        pt = jnp.exp(st - m).astype(BF16)
        oa = jnp.dot(vt, pt, preferred_element_type=F32)
        inv = 1.0 / (oa[LANES:LANES + 1] + jnp.exp(sink_row - m))
        o = oa[:LANES] * inv
        for g in range(GROUP):
            c = g * KV_PER_STEP * WINDOW
            ot = jnp.concatenate([o[:HEAD_DIM, c:c + WINDOW], o[HEAD_DIM:, c + WINDOW:c + 2 * WINDOW]], axis=0)
            o_ref[:, q0 + g * LANES:q0 + (g + 1) * LANES] = ot.T.astype(o_ref.dtype)


def _swa_prompt(q, kv, bias_tbl, sinks):
    n = PAIRS_PER_STEP
    v_col0 = KV_COLS // (n * LANES)

    def prev(p, b, i):
        return b * NB + jnp.maximum(i - 1, 0)

    return pl.pallas_call(
        _swa_prompt_body,
        grid=(N_PAIRS // n, BATCH, NB),
        in_specs=[
            pl.BlockSpec(memory_space=pltpu.SMEM),
            pl.BlockSpec((WINDOW, n * Q_COLS_PER_STEP), lambda p, b, i: (b * NB + i, p)),
            pl.BlockSpec((WINDOW, n * LANES), lambda p, b, i: (prev(p, b, i), p)),
            pl.BlockSpec((WINDOW, n * LANES), lambda p, b, i: (b * NB + i, p)),
            pl.BlockSpec((WINDOW, n * LANES), lambda p, b, i: (prev(p, b, i), v_col0 + p)),
            pl.BlockSpec((WINDOW, n * LANES), lambda p, b, i: (b * NB + i, v_col0 + p)),
            pl.BlockSpec((n * PAIR_SLOTS, 2 * WINDOW, WINDOW), lambda p, b, i: (p, 0, 0)),
        ],
        out_specs=pl.BlockSpec((WINDOW, n * Q_COLS_PER_STEP), lambda p, b, i: (b * NB + i, p)),
        out_shape=jax.ShapeDtypeStruct((N_PROMPT, Q_DIM), BF16),
        compiler_params=_cparams(("arbitrary", "arbitrary", "arbitrary")),
        name="swa_prompt",
    )(sinks, q, kv, kv, kv, kv, bias_tbl)


S_ROWS = N_HEADS * DEC_SEQ
SWA_SEQS_PER_STEP = 8


def _swa_sample_body(q_ref, kn_ref, vn_ref, ck_ref, cv_ref, bias_ref, sink_ref,
                     o_ref, ko_ref, vo_ref):
    slot = lax.broadcasted_iota(jnp.int32, (S_ROWS, KV_COLS), 0) % N_HEADS
    row_kv = slot // PAIR_SLOTS * KV_PER_STEP + slot % KV_PER_STEP
    col_kv = lax.broadcasted_iota(jnp.int32, (S_ROWS, KV_COLS), 1) // HEAD_DIM
    own = row_kv == col_kv
    pad = jnp.zeros((WINDOW - SUBLANES, KV_COLS), F32)
    keep = WINDOW - DEC_SEQ
    for i in range(SWA_SEQS_PER_STEP):
        x = q_ref[i].astype(BF16)
        xt = jnp.concatenate([x] * N_KV_HEADS, axis=1)
        qbd = jnp.where(own, xt, jnp.zeros_like(xt))
        kk = jnp.concatenate([ck_ref[i], kn_ref[i], pad], axis=0).astype(BF16)
        vv = jnp.concatenate([cv_ref[i], vn_ref[i], pad], axis=0).astype(BF16)
        s = lax.dot_general(qbd, kk, (((1,), (1,)), ((), ())), preferred_element_type=F32)
        s = s + bias_ref[...]
        p, inv = _softmax_with_sink(s, sink_ref[...])
        of = jnp.dot(p.astype(BF16), vv, preferred_element_type=F32)
        of = jnp.where(own, of, 0.0)
        o = of[:, 0:HEAD_DIM]
        for c in range(1, N_KV_HEADS):
            o = o + of[:, c * HEAD_DIM:(c + 1) * HEAD_DIM]
        o_ref[i] = o * inv
        ko_ref[i, 0:keep, :] = ck_ref[i, DEC_SEQ:WINDOW, :]
        ko_ref[i, keep:WINDOW, :] = kn_ref[i, 0:DEC_SEQ, :]
        vo_ref[i, 0:keep, :] = cv_ref[i, DEC_SEQ:WINDOW, :]
        vo_ref[i, keep:WINDOW, :] = vn_ref[i, 0:DEC_SEQ, :]


def _swa_sample(q_rows, k_new8, v_new8, cache_k, cache_v, bias_s, sink_col):
    n = SWA_SEQS_PER_STEP
    seq3 = lambda s: (s, 0, 0)
    full2 = lambda s: (0, 0)
    return pl.pallas_call(
        _swa_sample_body,
        grid=(DEC_BATCH // n,),
        in_specs=[
            pl.BlockSpec((n, S_ROWS, HEAD_DIM), seq3),
            pl.BlockSpec((n, SUBLANES, KV_COLS), seq3),
            pl.BlockSpec((n, SUBLANES, KV_COLS), seq3),
            pl.BlockSpec((n, WINDOW, KV_COLS), seq3),
            pl.BlockSpec((n, WINDOW, KV_COLS), seq3),
            pl.BlockSpec((S_ROWS, 2 * WINDOW), full2),
            pl.BlockSpec((S_ROWS, 1), full2),
        ],
        out_specs=[
            pl.BlockSpec((n, S_ROWS, HEAD_DIM), seq3),
            pl.BlockSpec((n, WINDOW, KV_COLS), seq3),
            pl.BlockSpec((n, WINDOW, KV_COLS), seq3),
        ],
        out_shape=[
            jax.ShapeDtypeStruct((DEC_BATCH, S_ROWS, HEAD_DIM), F32),
            jax.ShapeDtypeStruct((DEC_BATCH, WINDOW, KV_COLS), F32),
            jax.ShapeDtypeStruct((DEC_BATCH, WINDOW, KV_COLS), F32),
        ],
        compiler_params=_cparams(("arbitrary",)),
        name="swa_sample",
    )(q_rows, k_new8, v_new8, cache_k, cache_v, bias_s, sink_col)


def _gla_out(o, gate, norm_w):
    return _rms(o, norm_w) * _silu(gate)


def _split3(x):
    hi = x.astype(BF16)
    r = x - hi.astype(F32)
    mid = r.astype(BF16)
    lo = (r - mid.astype(F32)).astype(BF16)
    return hi, mid, lo


def _cumsum_rows(g):
    c = g.shape[0]
    tri = (lax.broadcasted_iota(jnp.int32, (c, c), 0) >= lax.broadcasted_iota(jnp.int32, (c, c), 1)).astype(BF16)
    return jnp.dot(jnp.concatenate([tri] * 3, axis=1), jnp.concatenate(_split3(g), axis=0),
                   preferred_element_type=F32)


def _causal(a):
    c = a.shape[0]
    keep = lax.broadcasted_iota(jnp.int32, (c, c), 0) >= lax.broadcasted_iota(jnp.int32, (c, c), 1)
    return jnp.where(keep, a, 0.0)


_NT = (((1,), (1,)), ((), ()))
_TN = (((0,), (0,)), ((), ()))


def _gla_prompt_body(*refs):
    proj_refs = refs[:BATCH]
    la_refs = refs[BATCH:2 * BATCH]
    nw_ref, o_ref, s_ref, st_ref = refs[2 * BATCH:]
    c = pl.program_id(0)

    @pl.when(c == 0)
    def _():
        st_ref[...] = jnp.zeros_like(st_ref)

    for bi in range(BATCH):
        p_ref = proj_refs[bi]
        b_all = _cumsum_rows(la_refs[bi][...])
        for h in range(GLA_HEADS):
            kc = slice(h * GLA_DK, (h + 1) * GLA_DK)
            v0 = 2 * GLA_KEY_DIM + h * GLA_DV
            b = b_all[:, kc]
            q = p_ref[:, kc].astype(F32) * GLA_DK ** -0.5
            k = p_ref[:, GLA_KEY_DIM + h * GLA_DK:GLA_KEY_DIM + (h + 1) * GLA_DK].astype(F32)
            v = p_ref[:, v0:v0 + GLA_DV]
            gate = p_ref[:, v0 + GLA_VAL_DIM:v0 + GLA_VAL_DIM + GLA_DV].astype(F32)
            qe = (q * jnp.exp(b)).astype(BF16)
            r = b[GLA_C // 2 - 1:GLA_C // 2, :]
            qr = (q * jnp.exp(b - r)).astype(BF16)
            kr = (k * jnp.exp(r - b)).astype(BF16)
            a = _causal(lax.dot_general(qr, kr, _NT, preferred_element_type=F32))
            st = st_ref[bi, h]
            o = (jnp.dot(a.astype(BF16), v, preferred_element_type=F32)
                 + lax.dot_general(qe, st.astype(BF16), _NT, preferred_element_type=F32))
            b_last = b[GLA_C - 1:GLA_C, :]
            kd = (k * jnp.exp(b_last - b)).astype(BF16)
            st_new = st * jnp.exp(b_last) + lax.dot_general(v, kd, _TN, preferred_element_type=F32)
            st_ref[bi, h] = st_new
            o_ref[bi, :, h * GLA_DV:(h + 1) * GLA_DV] = _gla_out(o, gate, nw_ref[...]).astype(o_ref.dtype)

    @pl.when(c == pl.num_programs(0) - 1)
    def _():
        for bi in range(BATCH):
            for h in range(GLA_HEADS):
                s_ref[bi, h] = st_ref[bi, h].T


def _gla_prompt(proj, log_a, norm_w):
    nc = SEQ // GLA_C
    rows = [functools.partial(lambda bi, c: (bi * nc + c, 0), bi) for bi in range(BATCH)]
    o, s = pl.pallas_call(
        _gla_prompt_body,
        grid=(nc,),
        in_specs=([pl.BlockSpec((GLA_C, GLA_MAIN_DIM), r) for r in rows]
                  + [pl.BlockSpec((GLA_C, GLA_KEY_DIM), r) for r in rows]
                  + [pl.BlockSpec((1, GLA_DV), lambda c: (0, 0))]),
        out_specs=[
            pl.BlockSpec((BATCH, GLA_C, GLA_VAL_DIM), lambda c: (0, c, 0)),
            pl.BlockSpec((BATCH, GLA_HEADS, GLA_DK, GLA_DV), lambda c: (0, 0, 0, 0)),
        ],
        out_shape=[
            jax.ShapeDtypeStruct((BATCH, SEQ, GLA_VAL_DIM), BF16),
            jax.ShapeDtypeStruct((BATCH, GLA_HEADS, GLA_DK, GLA_DV), F32),
        ],
        scratch_shapes=[pltpu.VMEM((BATCH, GLA_HEADS, GLA_DV, GLA_DK), F32)],
        compiler_params=_cparams(("arbitrary",)),
        name="gla_prompt",
    )(*([proj] * BATCH + [log_a] * BATCH + [norm_w.reshape(1, GLA_DV)]))
    return o.reshape(N_PROMPT, GLA_VAL_DIM), s


GLA_SEQS_PER_STEP = 4


def _gla_sample_body(proj_ref, la_ref, s0_ref, nw_ref, o_ref, s_ref):
    ones = jnp.ones((DEC_SEQ, LANES), BF16)
    proj = proj_ref[...].astype(F32)
    la = la_ref[...]
    for i in range(GLA_SEQS_PER_STEP):
        rows_i = slice(i * DEC_SEQ, (i + 1) * DEC_SEQ)
        for h in range(GLA_HEADS):
            q = proj[rows_i, h * GLA_DK:(h + 1) * GLA_DK]
            k = proj[rows_i, GLA_KEY_DIM + h * GLA_DK:GLA_KEY_DIM + (h + 1) * GLA_DK]
            v0 = 2 * GLA_KEY_DIM + h * GLA_DV
            v = proj[rows_i, v0:v0 + GLA_DV].astype(BF16)
            gate = proj[rows_i, v0 + GLA_VAL_DIM:v0 + GLA_VAL_DIM + GLA_DV]
            g = la[rows_i, h * GLA_DK:(h + 1) * GLA_DK]
            rows = [g[0:1]]
            for t in range(1, DEC_SEQ):
                rows.append(rows[-1] + g[t:t + 1])
            b = jnp.concatenate(rows, axis=0)
            b_last = rows[-1]
            qe = (q * GLA_DK ** -0.5 * jnp.exp(b)).astype(BF16)
            ke = (k * jnp.exp(-b)).astype(BF16)
            a = _causal(lax.dot_general(qe, ke, _NT, preferred_element_type=F32))
            s0 = s0_ref[i, h]
            o = (jnp.dot(a.astype(BF16), v, preferred_element_type=F32)
                 + jnp.dot(qe, s0.astype(BF16), preferred_element_type=F32))
            kd = (k * jnp.exp(b_last - b)).astype(BF16)
            dsum = sum(lax.dot_general(piece, ones, _TN, preferred_element_type=F32) for piece in _split3(g))
            decay = jnp.concatenate([jnp.exp(dsum)] * (GLA_DV // LANES), axis=1)
            s_ref[i, h] = s0 * decay + lax.dot_general(kd, v, _TN, preferred_element_type=F32)
            o_ref[rows_i, h * GLA_DV:(h + 1) * GLA_DV] = _gla_out(o, gate, nw_ref[...])


def _gla_sample(proj, log_a, state, norm_w):
    n = GLA_SEQS_PER_STEP
    rows = n * DEC_SEQ
    first = N_PROMPT // rows
    seq4 = lambda s: (s, 0, 0, 0)
    return pl.pallas_call(
        _gla_sample_body,
        grid=(DEC_BATCH // n,),
        in_specs=[
            pl.BlockSpec((rows, GLA_MAIN_DIM), lambda s: (first + s, 0)),
            pl.BlockSpec((rows, GLA_KEY_DIM), lambda s: (first + s, 0)),
            pl.BlockSpec((n, GLA_HEADS, GLA_DK, GLA_DV), seq4),
            pl.BlockSpec((1, GLA_DV), lambda s: (0, 0)),
        ],
        out_specs=[
            pl.BlockSpec((rows, GLA_VAL_DIM), lambda s: (s, 0)),
            pl.BlockSpec((n, GLA_HEADS, GLA_DK, GLA_DV), seq4),
        ],
        out_shape=[
            jax.ShapeDtypeStruct((N_SAMPLE, GLA_VAL_DIM), F32),
            jax.ShapeDtypeStruct((DEC_BATCH, GLA_HEADS, GLA_DK, GLA_DV), F32),
        ],
        compiler_params=_cparams(("arbitrary",)),
        name="gla_sample",
    )(proj, log_a, state, norm_w.reshape(1, GLA_DV))


CAST_ROWS = 512


def _cast_body(w_ref, o_ref):
    o_ref[...] = w_ref[0].astype(BF16)


def _layer_bf16(w, layer):
    _, rows, cols = w.shape
    return pl.pallas_call(
        _cast_body,
        grid=(rows // CAST_ROWS,),
        in_specs=[pl.BlockSpec((1, CAST_ROWS, cols), lambda i: (layer, i, 0))],
        out_specs=pl.BlockSpec((CAST_ROWS, cols), lambda i: (i, 0)),
        out_shape=jax.ShapeDtypeStruct((rows, cols), BF16),
        compiler_params=_cparams(("arbitrary",)),
        name="cast_bf16",
    )(w)


def _slot_rows_body(*refs):
    o_ref = refs[-1]
    for u, w_ref in enumerate(refs[:-1]):
        o_ref[u * HEAD_DIM:(u + 1) * HEAD_DIM] = w_ref[0].astype(BF16)


def _slot_rows_bf16(w, layer):
    def src(u):
        g, hh = u // KV_PER_STEP, u % KV_PER_STEP
        return lambda pair: (layer, (pair * KV_PER_STEP + hh) * GROUP + g, 0)

    cols = w.shape[2]
    return pl.pallas_call(
        _slot_rows_body,
        grid=(N_PAIRS,),
        in_specs=[pl.BlockSpec((1, HEAD_DIM, cols), src(u)) for u in range(PAIR_SLOTS)],
        out_specs=pl.BlockSpec((PAIR_SLOTS * HEAD_DIM, cols), lambda pair: (pair, 0)),
        out_shape=jax.ShapeDtypeStruct((Q_DIM, cols), BF16),
        compiler_params=_cparams(("arbitrary",)),
        name="slot_rows_bf16",
    )(*([w] * PAIR_SLOTS))


def _swa_layer(x, cache_k, cache_v, norm_w, w_qkv, b_qkv, j, w_o_all, b_o, sinks, rel_bias):
    slots = (N_PAIRS, KV_PER_STEP, GROUP, HEAD_DIM)
    q_scale = HEAD_DIM ** -0.5
    assert math.frexp(q_scale)[0] == 0.5, q_scale
    w_q = (w_qkv[:, :Q_DIM] * q_scale).reshape((D_MODEL,) + slots).transpose(0, 1, 3, 2, 4).reshape(D_MODEL, Q_DIM)
    b_q = (b_qkv[:Q_DIM] * q_scale).reshape(slots).transpose(0, 2, 1, 3).reshape(Q_DIM)
    w_qkv_s = jnp.concatenate([w_q, w_qkv[:, Q_DIM:]], axis=1).astype(BF16)
    b_qkv_s = jnp.concatenate([b_q, b_qkv[Q_DIM:]])
    w_o_s = _slot_rows_bf16(w_o_all, j)

    q, kv = _norm_proj(x, norm_w, w_qkv_s, b_qkv_s, ((Q_DIM, BF16), (2 * KV_COLS, F32)), "swa_qkv")
    bias_tbl = _bias_table(rel_bias)
    o_p = _swa_prompt(q, kv, bias_tbl, sinks)

    q_rows = q[N_PROMPT:].astype(F32).reshape(DEC_BATCH, S_ROWS, HEAD_DIM)
    kv_s = kv[N_PROMPT:].reshape(DEC_BATCH, DEC_SEQ, 2 * KV_COLS)
    pad8 = ((0, 0), (0, SUBLANES - DEC_SEQ), (0, 0))
    k_new8 = jnp.pad(kv_s[..., :KV_COLS], pad8)
    v_new8 = jnp.pad(kv_s[..., KV_COLS:], pad8)
    bias_s = bias_tbl[:, :, :DEC_SEQ].transpose(2, 0, 1).reshape(S_ROWS, 2 * WINDOW)
    slot_heads = np.array([_slot_head(s) for s in range(N_HEADS)])
    sink_col = jnp.tile(sinks[slot_heads], DEC_SEQ).reshape(S_ROWS, 1)
    o_s, k_s, v_s = _swa_sample(q_rows, k_new8, v_new8,
                                cache_k.reshape(DEC_BATCH, WINDOW, KV_COLS),
                                cache_v.reshape(DEC_BATCH, WINDOW, KV_COLS), bias_s, sink_col)
    o_s = o_s.reshape(N_SAMPLE, Q_DIM)
    x = _proj_res((o_p, o_s), w_o_s, b_o, x, "swa_out")

    kv_p = jnp.stack([kv[(b + 1) * SEQ - WINDOW:(b + 1) * SEQ] for b in range(BATCH)])
    k_p = kv_p[..., :KV_COLS].reshape(BATCH, WINDOW, N_KV_HEADS, HEAD_DIM)
    v_p = kv_p[..., KV_COLS:].reshape(BATCH, WINDOW, N_KV_HEADS, HEAD_DIM)
    shape_s = (DEC_BATCH, WINDOW, N_KV_HEADS, HEAD_DIM)
    return x, k_p, v_p, k_s.reshape(shape_s), v_s.reshape(shape_s)


def _gla_layer(x, state, norm_w, j, w_in_all, w_gk2, b_gk, gnorm, w_o_all):
    proj, log_a = _norm_proj(x, norm_w, w_in_all[j].astype(BF16), jnp.zeros((GLA_MAIN_DIM,), F32),
                             ((GLA_MAIN_DIM, BF16),), "gla_in", gate=(w_gk2.astype(BF16), b_gk))
    o_p, s_p = _gla_prompt(proj, log_a, gnorm)
    o_s, s_s = _gla_sample(proj, log_a, state, gnorm)
    x = _proj_res((o_p, o_s), _layer_bf16(w_o_all, j), jnp.zeros((D_MODEL,), F32), x, "gla_out")
    return x, s_p, s_s


def kernel(x_prompt, x_sample, cache_swa_k, cache_swa_v, state_gla, norm_ffn1, ffn1_w_gate, ffn1_w_up,
           ffn1_w_down, norm_mix, norm_ffn2, ffn2_w_gate, ffn2_w_up, ffn2_w_down, norm_final, rel_bias,
           swa_w_qkv, swa_b_qkv, swa_w_o, swa_b_o, swa_sinks, gla_w_in, gla_w_gk2, gla_b_gk, gla_norm,
           gla_w_o):
    x = (x_prompt.reshape(N_PROMPT, D_MODEL), x_sample.reshape(N_SAMPLE, D_MODEL))
    swa_kp, swa_vp, swa_ks, swa_vs, gla_sp, gla_ss = [], [], [], [], [], []
    for i in range(DEPTH):
        x = _ffn_pair(x, i, norm_ffn1[i], ffn1_w_gate, ffn1_w_up, ffn1_w_down)
        j = i // 2
        if i % 2 == 0:
            x, kp, vp, ks, vs = _swa_layer(x, cache_swa_k[j], cache_swa_v[j], norm_mix[i], swa_w_qkv[j],
                                           swa_b_qkv[j], j, swa_w_o, swa_b_o[j], swa_sinks[j], rel_bias)
            swa_kp.append(kp)
            swa_vp.append(vp)
            swa_ks.append(ks)
            swa_vs.append(vs)
        else:
            x, sp, ss = _gla_layer(x, state_gla[j], norm_mix[i], j, gla_w_in, gla_w_gk2[j], gla_b_gk[j],
                                   gla_norm[j], gla_w_o)
            gla_sp.append(sp)
            gla_ss.append(ss)
        final_w = norm_final if i == DEPTH - 1 else None
        x = _ffn_pair(x, i, norm_ffn2[i], ffn2_w_gate, ffn2_w_up, ffn2_w_down, final_w)
    y_prompt = x[0].reshape(BATCH, SEQ, D_MODEL)
    y_sample = x[1].reshape(DEC_BATCH, DEC_SEQ, D_MODEL)
    return (y_prompt, y_sample, jnp.stack(swa_kp), jnp.stack(swa_vp), jnp.stack(swa_ks), jnp.stack(swa_vs),
            jnp.stack(gla_sp), jnp.stack(gla_ss))
```

```python
import functools
import math

import numpy as np
import jax
import jax.numpy as jnp
from jax import lax
from jax.experimental import pallas as pl
from jax.experimental.pallas import tpu as pltpu

F32 = jnp.float32
BF16 = jnp.bfloat16

D_MODEL = 2048
BATCH = 2
SEQ = 4096
DEPTH = 2
DEC_BATCH = 128
DEC_SEQ = 4
RMS_EPS = 1e-6
D_FF = 5632
N_HEADS = 32
N_KV_HEADS = 8
HEAD_DIM = 64
GROUP = N_HEADS // N_KV_HEADS
WINDOW = 128
NUM_BUCKETS = 32
MAX_DISTANCE = 128
NEG_INF = -1e30
GLA_HEADS = 4
GLA_DK = 256
GLA_DV = 512
GLA_KEY_DIM = GLA_HEADS * GLA_DK
GLA_VAL_DIM = GLA_HEADS * GLA_DV
GATE_RANK = 16
GATE_NORMALIZER = 16.0
GLA_MAIN_DIM = 2 * GLA_KEY_DIM + 2 * GLA_VAL_DIM
Q_DIM = N_HEADS * HEAD_DIM
KV_COLS = N_KV_HEADS * HEAD_DIM

N_PROMPT = BATCH * SEQ
N_SAMPLE = DEC_BATCH * DEC_SEQ
N_TOK = N_PROMPT + N_SAMPLE

LANES = 128
SUBLANES = 8
VMEM_LIMIT = 56 * 1024 * 1024
BIG_VMEM_LIMIT = 60 * 1024 * 1024

TM = 512
TN = 512
GLA_C = 64
NB = SEQ // WINDOW
N_PROMPT_TILES = N_PROMPT // TM


def _rms(x, w):
    return x * lax.rsqrt(jnp.mean(x * x, axis=-1, keepdims=True) + RMS_EPS) * w


def _silu(x):
    return x * jax.nn.sigmoid(x)


def _cparams(sem, vmem_limit=VMEM_LIMIT):
    return pltpu.CompilerParams(dimension_semantics=sem, vmem_limit_bytes=vmem_limit)


def _row_pair_specs(width):
    return [pl.BlockSpec((TM, width), lambda i: (jnp.minimum(i, N_PROMPT_TILES - 1), 0)),
            pl.BlockSpec((TM, width), lambda i: (jnp.maximum(i - N_PROMPT_TILES, 0), 0))]


def _on_row_source(fn, *ref_pairs):
    i = pl.program_id(0)
    pl.when(i < N_PROMPT_TILES)(lambda: fn(*[p[0] for p in ref_pairs]))
    pl.when(i >= N_PROMPT_TILES)(lambda: fn(*[p[1] for p in ref_pairs]))


FFN_TM = 1024
FFN_TF = 256
FFN_TF_SAMPLE = 512


def _ffn_body(final_norm, emit_w, has_init, x_ref, nw_ref, wg_ref, wu_ref, wd_ref, *rest):
    rest = list(rest)
    fw_ref = rest.pop(0) if final_norm else None
    if has_init:
        rest.pop(0)
    o_ref = rest.pop(0)
    h_ref = rest.pop()
    j = pl.program_id(1)

    def step(first):
        if first:
            h_ref[...] = _rms(x_ref[...], nw_ref[...]).astype(BF16)
        if emit_w:
            wgo_ref, wuo_ref, wdo_ref = rest
            wgo_ref[...] = wg_ref[0].astype(BF16)
            wuo_ref[...] = wu_ref[0].astype(BF16)
            wdo_ref[...] = wd_ref[0].astype(BF16)
            wg, wu, wd = wgo_ref[...], wuo_ref[...], wdo_ref[...]
        else:
            wg, wu, wd = wg_ref[...], wu_ref[...], wd_ref[...]
        h = h_ref[...]
        g = jnp.dot(h, wg, preferred_element_type=F32)
        u = jnp.dot(h, wu, preferred_element_type=F32)
        a = (_silu(g) * (0.5 * u)).astype(BF16)
        d = jnp.dot(a, wd, preferred_element_type=F32)
        o_ref[...] = (x_ref[...] if first else o_ref[...]) + d

    pl.when(j == 0)(lambda: step(True))
    pl.when(j > 0)(lambda: step(False))

    if final_norm:
        @pl.when(j == pl.num_programs(1) - 1)
        def _():
            o_ref[...] = _rms(o_ref[...], fw_ref[...])


def _ffn(x, layer, nw, wg, wu, wd, final_w=None, tile0=0, n_tiles=None, y_init=None):
    m = x.shape[0]
    emit_w = wg.dtype == F32
    tm = min(FFN_TM, m)
    if n_tiles is None:
        n_tiles = m // tm
    tf = FFN_TF if emit_w else FFN_TF_SAMPLE
    nj = D_FF // tf
    final_norm = final_w is not None
    vec = pl.BlockSpec((1, D_MODEL), lambda i, j: (0, 0))
    if emit_w:
        w_specs = [
            pl.BlockSpec((1, D_MODEL, tf), lambda i, j: (layer, 0, j)),
            pl.BlockSpec((1, D_MODEL, tf), lambda i, j: (layer, 0, j)),
            pl.BlockSpec((1, tf, D_MODEL), lambda i, j: (layer, j, 0)),
        ]
    else:
        w_specs = [
            pl.BlockSpec((D_MODEL, tf), lambda i, j: (0, j)),
            pl.BlockSpec((D_MODEL, tf), lambda i, j: (0, j)),
            pl.BlockSpec((tf, D_MODEL), lambda i, j: (j, 0)),
        ]
    in_specs = [pl.BlockSpec((tm, D_MODEL), lambda i, j: (tile0 + i, 0)), vec] + w_specs
    args = [x, nw.reshape(1, D_MODEL), wg, wu, wd]
    if final_norm:
        in_specs.append(vec)
        args.append(final_w.reshape(1, D_MODEL))
    aliases = {}
    if y_init is not None:
        aliases = {len(args): 0}
        in_specs.append(pl.BlockSpec(memory_space=pl.ANY))
        args.append(y_init)
    out_specs = [pl.BlockSpec((tm, D_MODEL), lambda i, j: (tile0 + i, 0))]
    out_shape = [jax.ShapeDtypeStruct((m, D_MODEL), F32)]
    if emit_w:
        once = lambda i, j: jnp.where(i == 0, j, nj - 1)
        out_specs += [
            pl.BlockSpec((D_MODEL, tf), lambda i, j: (0, once(i, j))),
            pl.BlockSpec((D_MODEL, tf), lambda i, j: (0, once(i, j))),
            pl.BlockSpec((tf, D_MODEL), lambda i, j: (once(i, j), 0)),
        ]
        out_shape += [
            jax.ShapeDtypeStruct((D_MODEL, D_FF), BF16),
            jax.ShapeDtypeStruct((D_MODEL, D_FF), BF16),
            jax.ShapeDtypeStruct((D_FF, D_MODEL), BF16),
        ]
    outs = pl.pallas_call(
        functools.partial(_ffn_body, final_norm, emit_w, y_init is not None),
        grid=(n_tiles, nj),
        in_specs=in_specs,
        out_specs=out_specs,
        out_shape=out_shape,
        scratch_shapes=[pltpu.VMEM((tm, D_MODEL), BF16)],
        input_output_aliases=aliases,
        compiler_params=_cparams(("arbitrary", "arbitrary"), BIG_VMEM_LIMIT),
        name="ffn",
    )(*args)
    return outs[0], tuple(outs[1:])


def _ffn_pair(x_pair, layer, nw, wg, wu, wd, final_w=None):
    xp, xs = x_pair
    y_prompt, w_bf16 = _ffn(xp, layer, nw, wg, wu, wd, final_w, tile0=0, n_tiles=1)
    y_prompt, _ = _ffn(xp, layer, nw, *w_bf16, final_w, tile0=1, n_tiles=xp.shape[0] // FFN_TM - 1,
                       y_init=y_prompt)
    y_sample, _ = _ffn(xs, layer, nw, *w_bf16, final_w)
    return y_prompt, y_sample


def _log_sigmoid(x):
    return jnp.minimum(x, 0.0) - jnp.log(1.0 + jnp.exp(-jnp.abs(x)))


def _norm_proj_body(segments, with_gate, xp_ref, xs_ref, nw_ref, w_ref, b_ref, *rest):
    if with_gate:
        w2_ref, bg_ref = rest[:2]
        rest = rest[2:]
    o_refs = rest[:len(segments)]

    def run(x_ref):
        h = _rms(x_ref[...], nw_ref[...]).astype(BF16)
        col = 0
        for (width, _), o_ref in zip(segments, o_refs):
            for c in range(0, width, TN):
                acc = jnp.dot(h, w_ref[:, col + c:col + c + TN], preferred_element_type=F32)
                o_ref[:, c:c + TN] = (acc + b_ref[:, col + c:col + c + TN]).astype(o_ref.dtype)
            col += width
        if with_gate:
            gk = jnp.dot(h, w_ref[:, col:col + GATE_RANK], preferred_element_type=F32)
            z = jnp.dot(gk.astype(BF16), w2_ref[...], preferred_element_type=F32) + bg_ref[...]
            rest[-1][...] = _log_sigmoid(z) / GATE_NORMALIZER
    _on_row_source(run, (xp_ref, xs_ref))


def _norm_proj(xs, nw, w, b, segments, name, gate=None):
    n = sum(width for width, _ in segments)
    in_specs = _row_pair_specs(D_MODEL) + [
        pl.BlockSpec((1, D_MODEL), lambda i: (0, 0)),
        pl.BlockSpec(w.shape, lambda i: (0, 0), pipeline_mode=pl.Buffered(1)),
        pl.BlockSpec((1, n), lambda i: (0, 0)),
    ]
    args = [*xs, nw.reshape(1, D_MODEL), w, b.reshape(1, n)]
    out_specs = [pl.BlockSpec((TM, width), lambda i: (i, 0)) for width, _ in segments]
    out_shape = [jax.ShapeDtypeStruct((N_TOK, width), dtype) for width, dtype in segments]
    if gate is not None:
        w2, bg = gate
        assert w.shape[1] == n + GATE_RANK, w.shape
        in_specs += [
            pl.BlockSpec((GATE_RANK, GLA_KEY_DIM), lambda i: (0, 0)),
            pl.BlockSpec((1, GLA_KEY_DIM), lambda i: (0, 0)),
        ]
        args += [w2, bg.reshape(1, GLA_KEY_DIM)]
        out_specs.append(pl.BlockSpec((TM, GLA_KEY_DIM), lambda i: (i, 0)))
        out_shape.append(jax.ShapeDtypeStruct((N_TOK, GLA_KEY_DIM), F32))
    return pl.pallas_call(
        functools.partial(_norm_proj_body, segments, gate is not None),
        grid=(N_TOK // TM,),
        in_specs=in_specs,
        out_specs=out_specs,
        out_shape=out_shape,
        compiler_params=_cparams(("arbitrary",), BIG_VMEM_LIMIT),
        name=name,
    )(*args)


def _proj_res_body(ap_ref, as_ref, w_ref, b_ref, rp_ref, rs_ref, op_ref, os_ref):
    def run(a_ref, r_ref, o_ref):
        a = a_ref[...].astype(BF16)
        for c in range(0, D_MODEL, TN):
            acc = jnp.dot(a, w_ref[:, c:c + TN], preferred_element_type=F32)
            o_ref[:, c:c + TN] = r_ref[:, c:c + TN] + acc + b_ref[:, c:c + TN]
    _on_row_source(run, (ap_ref, as_ref), (rp_ref, rs_ref), (op_ref, os_ref))


def _proj_res(a_pair, w, b, res_pair, name):
    k = w.shape[0]
    return pl.pallas_call(
        _proj_res_body,
        grid=(N_TOK // TM,),
        in_specs=_row_pair_specs(k) + [
            pl.BlockSpec((k, D_MODEL), lambda i: (0, 0), pipeline_mode=pl.Buffered(1)),
            pl.BlockSpec((1, D_MODEL), lambda i: (0, 0)),
        ] + _row_pair_specs(D_MODEL),
        out_specs=_row_pair_specs(D_MODEL),
        out_shape=[jax.ShapeDtypeStruct((N_PROMPT, D_MODEL), F32), jax.ShapeDtypeStruct((N_SAMPLE, D_MODEL), F32)],
        compiler_params=_cparams(("arbitrary",)),
        name=name,
    )(*a_pair, w, b.reshape(1, D_MODEL), *res_pair)


def _t5_bucket_table():
    i = np.arange(WINDOW)[None, :]
    j = np.arange(2 * WINDOW)[:, None]
    n = np.maximum(WINDOW + i - j, 0)
    max_exact = NUM_BUCKETS // 2
    nf = np.maximum(n, 1).astype(np.float32)
    large = max_exact + (np.log(nf / np.float32(max_exact)) / np.float32(math.log(MAX_DISTANCE / max_exact))
                         * np.float32(NUM_BUCKETS - max_exact)).astype(np.int32)
    large = np.minimum(large, NUM_BUCKETS - 1)
    return np.where(n < max_exact, n, large).astype(np.int32)


KV_PER_STEP = LANES // HEAD_DIM
N_PAIRS = N_KV_HEADS // KV_PER_STEP
Q_COLS_PER_STEP = KV_PER_STEP * GROUP * HEAD_DIM
PAIR_SLOTS = GROUP * KV_PER_STEP


def _slot_head(slot):
    pair = slot // PAIR_SLOTS
    g = (slot // KV_PER_STEP) % GROUP
    hh = slot % KV_PER_STEP
    return (pair * KV_PER_STEP + hh) * GROUP + g


BIAS_SLOTS_PER_STEP = 8


def _bias_table_body(bucket_ref, rb_ref, o_ref):
    bucket = bucket_ref[...]
    j = lax.broadcasted_iota(jnp.int32, (2 * WINDOW, WINDOW), 0)
    i = lax.broadcasted_iota(jnp.int32, (2 * WINDOW, WINDOW), 1)
    dist = WINDOW + i - j
    in_window = (dist >= 0) & (dist < WINDOW)
    for t in range(BIAS_SLOTS_PER_STEP):
        h = _slot_head(pl.program_id(0) * BIAS_SLOTS_PER_STEP + t)
        acc = jnp.zeros((2 * WINDOW, WINDOW), F32)
        for b in range(NUM_BUCKETS):
            acc = jnp.where(bucket == b, rb_ref[b, h], acc)
        o_ref[t] = jnp.where(in_window, acc, NEG_INF)


def _bias_table(rel_bias):
    n = BIAS_SLOTS_PER_STEP
    return pl.pallas_call(
        _bias_table_body,
        grid=(N_HEADS // n,),
        in_specs=[
            pl.BlockSpec((2 * WINDOW, WINDOW), lambda h: (0, 0)),
            pl.BlockSpec(memory_space=pltpu.SMEM),
        ],
        out_specs=pl.BlockSpec((n, 2 * WINDOW, WINDOW), lambda h: (h, 0, 0)),
        out_shape=jax.ShapeDtypeStruct((N_HEADS, 2 * WINDOW, WINDOW), F32),
        name="bias_table",
    )(jnp.asarray(_t5_bucket_table()), rel_bias)


def _softmax_with_sink(s, sink_col):
    m = jnp.maximum(jnp.max(s, axis=-1, keepdims=True), sink_col)
    p = jnp.exp(s - m)
    denom = jnp.sum(p, axis=-1, keepdims=True) + jnp.exp(sink_col - m)
    return p, 1.0 / denom


PAIR_COLS = PAIR_SLOTS * WINDOW
ONES_ROWS = 16
PAIRS_PER_STEP = 4


def _swa_prompt_body(sink_ref, q_ref, kp_ref, ko_ref, vp_ref, vo_ref, bias_ref, o_ref):
    first = pl.program_id(2) == 0
    args = (sink_ref, q_ref, kp_ref, ko_ref, vp_ref, vo_ref, bias_ref, o_ref)
    pl.when(first)(lambda: _swa_prompt_block(True, *args))
    pl.when(jnp.logical_not(first))(lambda: _swa_prompt_block(False, *args))


def _swa_prompt_block(mask_prev, sink_ref, q_ref, kp_ref, ko_ref, vp_ref, vo_ref, bias_ref, o_ref):
    head_a = lax.broadcasted_iota(jnp.int32, (WINDOW, LANES), 1) < HEAD_DIM
    for t in range(PAIRS_PER_STEP):
        pair = pl.program_id(0) * PAIRS_PER_STEP + t
        lanes = slice(t * LANES, (t + 1) * LANES)
        q0 = t * Q_COLS_PER_STEP
        k = jnp.concatenate([kp_ref[:, lanes], ko_ref[:, lanes]], axis=0).astype(BF16)
        v = jnp.concatenate([vp_ref[:, lanes], vo_ref[:, lanes]], axis=0)
        vt = jnp.concatenate([v.T, jnp.ones((ONES_ROWS, 2 * WINDOW), F32)], axis=0).astype(BF16)
        parts = []
        for g in range(GROUP):
            qg = q_ref[:, q0 + g * LANES:q0 + (g + 1) * LANES]
            zero = jnp.zeros_like(qg)
            parts += [jnp.where(head_a, qg, zero), jnp.where(head_a, zero, qg)]
        qbd = jnp.concatenate(parts, axis=0)
        st = lax.dot_general(k, qbd, (((1,), (1,)), ((), ())), preferred_element_type=F32)
        st = st + jnp.concatenate([bias_ref[t * PAIR_SLOTS + u] for u in range(PAIR_SLOTS)], axis=1)
        if mask_prev:
            st = jnp.concatenate([st[:WINDOW] + NEG_INF, st[WINDOW:]], axis=0)
        sink_row = jnp.concatenate(
            [jnp.full((1, WINDOW), sink_ref[_slot_head(pair * PAIR_SLOTS + slot)], F32)
             for slot in range(PAIR_SLOTS)], axis=1)
        m = jnp.maximum(jnp.max(st, axis=0, keepdims=True), sink_row)
        pt = jnp.exp(st - m).astype(BF16)
        oa = jnp.dot(vt, pt, preferred_element_type=F32)
        inv = 1.0 / (oa[LANES:LANES + 1] + jnp.exp(sink_row - m))
        o = oa[:LANES] * inv
        for g in range(GROUP):
            c = g * KV_PER_STEP * WINDOW
            ot = jnp.concatenate([o[:HEAD_DIM, c:c + WINDOW], o[HEAD_DIM:, c + WINDOW:c + 2 * WINDOW]], axis=0)
            o_ref[:, q0 + g * LANES:q0 + (g + 1) * LANES] = ot.T.astype(o_ref.dtype)


def _swa_prompt(q, kv, bias_tbl, sinks):
    n = PAIRS_PER_STEP
    v_col0 = KV_COLS // (n * LANES)

    def prev(p, b, i):
        return b * NB + jnp.maximum(i - 1, 0)

    return pl.pallas_call(
        _swa_prompt_body,
        grid=(N_PAIRS // n, BATCH, NB),
        in_specs=[
            pl.BlockSpec(memory_space=pltpu.SMEM),
            pl.BlockSpec((WINDOW, n * Q_COLS_PER_STEP), lambda p, b, i: (b * NB + i, p)),
            pl.BlockSpec((WINDOW, n * LANES), lambda p, b, i: (prev(p, b, i), p)),
            pl.BlockSpec((WINDOW, n * LANES), lambda p, b, i: (b * NB + i, p)),
            pl.BlockSpec((WINDOW, n * LANES), lambda p, b, i: (prev(p, b, i), v_col0 + p)),
            pl.BlockSpec((WINDOW, n * LANES), lambda p, b, i: (b * NB + i, v_col0 + p)),
            pl.BlockSpec((n * PAIR_SLOTS, 2 * WINDOW, WINDOW), lambda p, b, i: (p, 0, 0)),
        ],
        out_specs=pl.BlockSpec((WINDOW, n * Q_COLS_PER_STEP), lambda p, b, i: (b * NB + i, p)),
        out_shape=jax.ShapeDtypeStruct((N_PROMPT, Q_DIM), BF16),
        compiler_params=_cparams(("arbitrary", "arbitrary", "arbitrary")),
        name="swa_prompt",
    )(sinks, q, kv, kv, kv, kv, bias_tbl)


S_ROWS = N_HEADS * DEC_SEQ
SWA_SEQS_PER_STEP = 8


def _swa_sample_body(q_ref, kn_ref, vn_ref, ck_ref, cv_ref, bias_ref, sink_ref,
                     o_ref, ko_ref, vo_ref):
    slot = lax.broadcasted_iota(jnp.int32, (S_ROWS, KV_COLS), 0) % N_HEADS
    row_kv = slot // PAIR_SLOTS * KV_PER_STEP + slot % KV_PER_STEP
    col_kv = lax.broadcasted_iota(jnp.int32, (S_ROWS, KV_COLS), 1) // HEAD_DIM
    own = row_kv == col_kv
    pad = jnp.zeros((WINDOW - SUBLANES, KV_COLS), F32)
    keep = WINDOW - DEC_SEQ
    for i in range(SWA_SEQS_PER_STEP):
        x = q_ref[i].astype(BF16)
        xt = jnp.concatenate([x] * N_KV_HEADS, axis=1)
        qbd = jnp.where(own, xt, jnp.zeros_like(xt))
        kk = jnp.concatenate([ck_ref[i], kn_ref[i], pad], axis=0).astype(BF16)
        vv = jnp.concatenate([cv_ref[i], vn_ref[i], pad], axis=0).astype(BF16)
        s = lax.dot_general(qbd, kk, (((1,), (1,)), ((), ())), preferred_element_type=F32)
        s = s + bias_ref[...]
        p, inv = _softmax_with_sink(s, sink_ref[...])
        of = jnp.dot(p.astype(BF16), vv, preferred_element_type=F32)
        of = jnp.where(own, of, 0.0)
        o = of[:, 0:HEAD_DIM]
        for c in range(1, N_KV_HEADS):
            o = o + of[:, c * HEAD_DIM:(c + 1) * HEAD_DIM]
        o_ref[i] = o * inv
        ko_ref[i, 0:keep, :] = ck_ref[i, DEC_SEQ:WINDOW, :]
        ko_ref[i, keep:WINDOW, :] = kn_ref[i, 0:DEC_SEQ, :]
        vo_ref[i, 0:keep, :] = cv_ref[i, DEC_SEQ:WINDOW, :]
        vo_ref[i, keep:WINDOW, :] = vn_ref[i, 0:DEC_SEQ, :]


def _swa_sample(q_rows, k_new8, v_new8, cache_k, cache_v, bias_s, sink_col):
    n = SWA_SEQS_PER_STEP
    seq3 = lambda s: (s, 0, 0)
    full2 = lambda s: (0, 0)
    return pl.pallas_call(
        _swa_sample_body,
        grid=(DEC_BATCH // n,),
        in_specs=[
            pl.BlockSpec((n, S_ROWS, HEAD_DIM), seq3),
            pl.BlockSpec((n, SUBLANES, KV_COLS), seq3),
            pl.BlockSpec((n, SUBLANES, KV_COLS), seq3),
            pl.BlockSpec((n, WINDOW, KV_COLS), seq3),
            pl.BlockSpec((n, WINDOW, KV_COLS), seq3),
            pl.BlockSpec((S_ROWS, 2 * WINDOW), full2),
            pl.BlockSpec((S_ROWS, 1), full2),
        ],
        out_specs=[
            pl.BlockSpec((n, S_ROWS, HEAD_DIM), seq3),
            pl.BlockSpec((n, WINDOW, KV_COLS), seq3),
            pl.BlockSpec((n, WINDOW, KV_COLS), seq3),
        ],
        out_shape=[
            jax.ShapeDtypeStruct((DEC_BATCH, S_ROWS, HEAD_DIM), F32),
            jax.ShapeDtypeStruct((DEC_BATCH, WINDOW, KV_COLS), F32),
            jax.ShapeDtypeStruct((DEC_BATCH, WINDOW, KV_COLS), F32),
        ],
        compiler_params=_cparams(("arbitrary",)),
        name="swa_sample",
    )(q_rows, k_new8, v_new8, cache_k, cache_v, bias_s, sink_col)


def _gla_out(o, gate, norm_w):
    return _rms(o, norm_w) * _silu(gate)


def _split3(x):
    hi = x.astype(BF16)
    r = x - hi.astype(F32)
    mid = r.astype(BF16)
    lo = (r - mid.astype(F32)).astype(BF16)
    return hi, mid, lo


def _cumsum_rows(g):
    c = g.shape[0]
    tri = (lax.broadcasted_iota(jnp.int32, (c, c), 0) >= lax.broadcasted_iota(jnp.int32, (c, c), 1)).astype(BF16)
    return jnp.dot(jnp.concatenate([tri] * 3, axis=1), jnp.concatenate(_split3(g), axis=0),
                   preferred_element_type=F32)


def _causal(a):
    c = a.shape[0]
    keep = lax.broadcasted_iota(jnp.int32, (c, c), 0) >= lax.broadcasted_iota(jnp.int32, (c, c), 1)
    return jnp.where(keep, a, 0.0)


_NT = (((1,), (1,)), ((), ()))
_TN = (((0,), (0,)), ((), ()))


def _gla_prompt_body(*refs):
    proj_refs = refs[:BATCH]
    la_refs = refs[BATCH:2 * BATCH]
    nw_ref, o_ref, s_ref, st_ref = refs[2 * BATCH:]
    c = pl.program_id(0)

    @pl.when(c == 0)
    def _():
        st_ref[...] = jnp.zeros_like(st_ref)

    for bi in range(BATCH):
        p_ref = proj_refs[bi]
        b_all = _cumsum_rows(la_refs[bi][...])
        for h in range(GLA_HEADS):
            kc = slice(h * GLA_DK, (h + 1) * GLA_DK)
            v0 = 2 * GLA_KEY_DIM + h * GLA_DV
            b = b_all[:, kc]
            q = p_ref[:, kc].astype(F32) * GLA_DK ** -0.5
            k = p_ref[:, GLA_KEY_DIM + h * GLA_DK:GLA_KEY_DIM + (h + 1) * GLA_DK].astype(F32)
            v = p_ref[:, v0:v0 + GLA_DV]
            gate = p_ref[:, v0 + GLA_VAL_DIM:v0 + GLA_VAL_DIM + GLA_DV].astype(F32)
            qe = (q * jnp.exp(b)).astype(BF16)
            r = b[GLA_C // 2 - 1:GLA_C // 2, :]
            qr = (q * jnp.exp(b - r)).astype(BF16)
            kr = (k * jnp.exp(r - b)).astype(BF16)
            a = _causal(lax.dot_general(qr, kr, _NT, preferred_element_type=F32))
            st = st_ref[bi, h]
            o = (jnp.dot(a.astype(BF16), v, preferred_element_type=F32)
                 + lax.dot_general(qe, st.astype(BF16), _NT, preferred_element_type=F32))
            b_last = b[GLA_C - 1:GLA_C, :]
            kd = (k * jnp.exp(b_last - b)).astype(BF16)
            st_new = st * jnp.exp(b_last) + lax.dot_general(v, kd, _TN, preferred_element_type=F32)
            st_ref[bi, h] = st_new
            o_ref[bi, :, h * GLA_DV:(h + 1) * GLA_DV] = _gla_out(o, gate, nw_ref[...]).astype(o_ref.dtype)

    @pl.when(c == pl.num_programs(0) - 1)
    def _():
        for bi in range(BATCH):
            for h in range(GLA_HEADS):
                s_ref[bi, h] = st_ref[bi, h].T


def _gla_prompt(proj, log_a, norm_w):
    nc = SEQ // GLA_C
    rows = [functools.partial(lambda bi, c: (bi * nc + c, 0), bi) for bi in range(BATCH)]
    o, s = pl.pallas_call(
        _gla_prompt_body,
        grid=(nc,),
        in_specs=([pl.BlockSpec((GLA_C, GLA_MAIN_DIM), r) for r in rows]
                  + [pl.BlockSpec((GLA_C, GLA_KEY_DIM), r) for r in rows]
                  + [pl.BlockSpec((1, GLA_DV), lambda c: (0, 0))]),
        out_specs=[
            pl.BlockSpec((BATCH, GLA_C, GLA_VAL_DIM), lambda c: (0, c, 0)),
            pl.BlockSpec((BATCH, GLA_HEADS, GLA_DK, GLA_DV), lambda c: (0, 0, 0, 0)),
        ],
        out_shape=[
            jax.ShapeDtypeStruct((BATCH, SEQ, GLA_VAL_DIM), BF16),
            jax.ShapeDtypeStruct((BATCH, GLA_HEADS, GLA_DK, GLA_DV), F32),
        ],
        scratch_shapes=[pltpu.VMEM((BATCH, GLA_HEADS, GLA_DV, GLA_DK), F32)],
        compiler_params=_cparams(("arbitrary",)),
        name="gla_prompt",
    )(*([proj] * BATCH + [log_a] * BATCH + [norm_w.reshape(1, GLA_DV)]))
    return o.reshape(N_PROMPT, GLA_VAL_DIM), s


GLA_SEQS_PER_STEP = 4


def _gla_sample_body(proj_ref, la_ref, s0_ref, nw_ref, o_ref, s_ref):
    ones = jnp.ones((DEC_SEQ, LANES), BF16)
    proj = proj_ref[...].astype(F32)
    la = la_ref[...]
    for i in range(GLA_SEQS_PER_STEP):
        rows_i = slice(i * DEC_SEQ, (i + 1) * DEC_SEQ)
        for h in range(GLA_HEADS):
            q = proj[rows_i, h * GLA_DK:(h + 1) * GLA_DK]
            k = proj[rows_i, GLA_KEY_DIM + h * GLA_DK:GLA_KEY_DIM + (h + 1) * GLA_DK]
            v0 = 2 * GLA_KEY_DIM + h * GLA_DV
            v = proj[rows_i, v0:v0 + GLA_DV].astype(BF16)
            gate = proj[rows_i, v0 + GLA_VAL_DIM:v0 + GLA_VAL_DIM + GLA_DV]
            g = la[rows_i, h * GLA_DK:(h + 1) * GLA_DK]
            rows = [g[0:1]]
            for t in range(1, DEC_SEQ):
                rows.append(rows[-1] + g[t:t + 1])
            b = jnp.concatenate(rows, axis=0)
            b_last = rows[-1]
            qe = (q * GLA_DK ** -0.5 * jnp.exp(b)).astype(BF16)
            ke = (k * jnp.exp(-b)).astype(BF16)
            a = _causal(lax.dot_general(qe, ke, _NT, preferred_element_type=F32))
            s0 = s0_ref[i, h]
            o = (jnp.dot(a.astype(BF16), v, preferred_element_type=F32)
                 + jnp.dot(qe, s0.astype(BF16), preferred_element_type=F32))
            kd = (k * jnp.exp(b_last - b)).astype(BF16)
            dsum = sum(lax.dot_general(piece, ones, _TN, preferred_element_type=F32) for piece in _split3(g))
            decay = jnp.concatenate([jnp.exp(dsum)] * (GLA_DV // LANES), axis=1)
            s_ref[i, h] = s0 * decay + lax.dot_general(kd, v, _TN, preferred_element_type=F32)
            o_ref[rows_i, h * GLA_DV:(h + 1) * GLA_DV] = _gla_out(o, gate, nw_ref[...])


def _gla_sample(proj, log_a, state, norm_w):
    n = GLA_SEQS_PER_STEP
    rows = n * DEC_SEQ
    first = N_PROMPT // rows
    seq4 = lambda s: (s, 0, 0, 0)
    return pl.pallas_call(
        _gla_sample_body,
        grid=(DEC_BATCH // n,),
        in_specs=[
            pl.BlockSpec((rows, GLA_MAIN_DIM), lambda s: (first + s, 0)),
            pl.BlockSpec((rows, GLA_KEY_DIM), lambda s: (first + s, 0)),
            pl.BlockSpec((n, GLA_HEADS, GLA_DK, GLA_DV), seq4),
            pl.BlockSpec((1, GLA_DV), lambda s: (0, 0)),
        ],
        out_specs=[
            pl.BlockSpec((rows, GLA_VAL_DIM), lambda s: (s, 0)),
            pl.BlockSpec((n, GLA_HEADS, GLA_DK, GLA_DV), seq4),
        ],
        out_shape=[
            jax.ShapeDtypeStruct((N_SAMPLE, GLA_VAL_DIM), F32),
            jax.ShapeDtypeStruct((DEC_BATCH, GLA_HEADS, GLA_DK, GLA_DV), F32),
        ],
        compiler_params=_cparams(("arbitrary",)),
        name="gla_sample",
    )(proj, log_a, state, norm_w.reshape(1, GLA_DV))


CAST_ROWS = 512


def _cast_body(w_ref, o_ref):
    o_ref[...] = w_ref[0].astype(BF16)


def _layer_bf16(w, layer):
    _, rows, cols = w.shape
    return pl.pallas_call(
        _cast_body,
        grid=(rows // CAST_ROWS,),
        in_specs=[pl.BlockSpec((1, CAST_ROWS, cols), lambda i: (layer, i, 0))],
        out_specs=pl.BlockSpec((CAST_ROWS, cols), lambda i: (i, 0)),
        out_shape=jax.ShapeDtypeStruct((rows, cols), BF16),
        compiler_params=_cparams(("arbitrary",)),
        name="cast_bf16",
    )(w)


def _slot_rows_body(*refs):
    o_ref = refs[-1]
    for u, w_ref in enumerate(refs[:-1]):
        o_ref[u * HEAD_DIM:(u + 1) * HEAD_DIM] = w_ref[0].astype(BF16)


def _slot_rows_bf16(w, layer):
    def src(u):
        g, hh = u // KV_PER_STEP, u % KV_PER_STEP
        return lambda pair: (layer, (pair * KV_PER_STEP + hh) * GROUP + g, 0)

    cols = w.shape[2]
    return pl.pallas_call(
        _slot_rows_body,
        grid=(N_PAIRS,),
        in_specs=[pl.BlockSpec((1, HEAD_DIM, cols), src(u)) for u in range(PAIR_SLOTS)],
        out_specs=pl.BlockSpec((PAIR_SLOTS * HEAD_DIM, cols), lambda pair: (pair, 0)),
        out_shape=jax.ShapeDtypeStruct((Q_DIM, cols), BF16),
        compiler_params=_cparams(("arbitrary",)),
        name="slot_rows_bf16",
    )(*([w] * PAIR_SLOTS))


def _swa_layer(x, cache_k, cache_v, norm_w, w_qkv, b_qkv, j, w_o_all, b_o, sinks, rel_bias):
    slots = (N_PAIRS, KV_PER_STEP, GROUP, HEAD_DIM)
    q_scale = HEAD_DIM ** -0.5
    assert math.frexp(q_scale)[0] == 0.5, q_scale
    w_q = (w_qkv[:, :Q_DIM] * q_scale).reshape((D_MODEL,) + slots).transpose(0, 1, 3, 2, 4).reshape(D_MODEL, Q_DIM)
    b_q = (b_qkv[:Q_DIM] * q_scale).reshape(slots).transpose(0, 2, 1, 3).reshape(Q_DIM)
    w_qkv_s = jnp.concatenate([w_q, w_qkv[:, Q_DIM:]], axis=1).astype(BF16)
    b_qkv_s = jnp.concatenate([b_q, b_qkv[Q_DIM:]])
    w_o_s = _slot_rows_bf16(w_o_all, j)

    q, kv = _norm_proj(x, norm_w, w_qkv_s, b_qkv_s, ((Q_DIM, BF16), (2 * KV_COLS, F32)), "swa_qkv")
    bias_tbl = _bias_table(rel_bias)
    o_p = _swa_prompt(q, kv, bias_tbl, sinks)

    q_rows = q[N_PROMPT:].astype(F32).reshape(DEC_BATCH, S_ROWS, HEAD_DIM)
    kv_s = kv[N_PROMPT:].reshape(DEC_BATCH, DEC_SEQ, 2 * KV_COLS)
    pad8 = ((0, 0), (0, SUBLANES - DEC_SEQ), (0, 0))
    k_new8 = jnp.pad(kv_s[..., :KV_COLS], pad8)
    v_new8 = jnp.pad(kv_s[..., KV_COLS:], pad8)
    bias_s = bias_tbl[:, :, :DEC_SEQ].transpose(2, 0, 1).reshape(S_ROWS, 2 * WINDOW)
    slot_heads = np.array([_slot_head(s) for s in range(N_HEADS)])
    sink_col = jnp.tile(sinks[slot_heads], DEC_SEQ).reshape(S_ROWS, 1)
    o_s, k_s, v_s = _swa_sample(q_rows, k_new8, v_new8,
                                cache_k.reshape(DEC_BATCH, WINDOW, KV_COLS),
                                cache_v.reshape(DEC_BATCH, WINDOW, KV_COLS), bias_s, sink_col)
    o_s = o_s.reshape(N_SAMPLE, Q_DIM)
    x = _proj_res((o_p, o_s), w_o_s, b_o, x, "swa_out")

    kv_p = jnp.stack([kv[(b + 1) * SEQ - WINDOW:(b + 1) * SEQ] for b in range(BATCH)])
    k_p = kv_p[..., :KV_COLS].reshape(BATCH, WINDOW, N_KV_HEADS, HEAD_DIM)
    v_p = kv_p[..., KV_COLS:].reshape(BATCH, WINDOW, N_KV_HEADS, HEAD_DIM)
    shape_s = (DEC_BATCH, WINDOW, N_KV_HEADS, HEAD_DIM)
    return x, k_p, v_p, k_s.reshape(shape_s), v_s.reshape(shape_s)


def _gla_layer(x, state, norm_w, j, w_in_all, w_gk2, b_gk, gnorm, w_o_all):
    proj, log_a = _norm_proj(x, norm_w, w_in_all[j].astype(BF16), jnp.zeros((GLA_MAIN_DIM,), F32),
                             ((GLA_MAIN_DIM, BF16),), "gla_in", gate=(w_gk2.astype(BF16), b_gk))
    o_p, s_p = _gla_prompt(proj, log_a, gnorm)
    o_s, s_s = _gla_sample(proj, log_a, state, gnorm)
    x = _proj_res((o_p, o_s), _layer_bf16(w_o_all, j), jnp.zeros((D_MODEL,), F32), x, "gla_out")
    return x, s_p, s_s


def kernel(x_prompt, x_sample, cache_swa_k, cache_swa_v, state_gla, norm_ffn1, ffn1_w_gate, ffn1_w_up,
           ffn1_w_down, norm_mix, norm_ffn2, ffn2_w_gate, ffn2_w_up, ffn2_w_down, norm_final, rel_bias,
           swa_w_qkv, swa_b_qkv, swa_w_o, swa_b_o, swa_sinks, gla_w_in, gla_w_gk2, gla_b_gk, gla_norm,
           gla_w_o):
    x = (x_prompt.reshape(N_PROMPT, D_MODEL), x_sample.reshape(N_SAMPLE, D_MODEL))
    swa_kp, swa_vp, swa_ks, swa_vs, gla_sp, gla_ss = [], [], [], [], [], []
    for i in range(DEPTH):
        x = _ffn_pair(x, i, norm_ffn1[i], ffn1_w_gate, ffn1_w_up, ffn1_w_down)
        j = i // 2
        if i % 2 == 0:
            x, kp, vp, ks, vs = _swa_layer(x, cache_swa_k[j], cache_swa_v[j], norm_mix[i], swa_w_qkv[j],
                                           swa_b_qkv[j], j, swa_w_o, swa_b_o[j], swa_sinks[j], rel_bias)
            swa_kp.append(kp)
            swa_vp.append(vp)
            swa_ks.append(ks)
            swa_vs.append(vs)
        else:
            x, sp, ss = _gla_layer(x, state_gla[j], norm_mix[i], j, gla_w_in, gla_w_gk2[j], gla_b_gk[j],
                                   gla_norm[j], gla_w_o)
            gla_sp.append(sp)
            gla_ss.append(ss)
        final_w = norm_final if i == DEPTH - 1 else None
        x = _ffn_pair(x, i, norm_ffn2[i], ffn2_w_gate, ffn2_w_up, ffn2_w_down, final_w)
    y_prompt = x[0].reshape(BATCH, SEQ, D_MODEL)
    y_sample = x[1].reshape(DEC_BATCH, DEC_SEQ, D_MODEL)
    return (y_prompt, y_sample, jnp.stack(swa_kp), jnp.stack(swa_vp), jnp.stack(swa_ks), jnp.stack(swa_vs),
            jnp.stack(gla_sp), jnp.stack(gla_ss))
```

```python
import functools
import math

import numpy as np
import jax
import jax.numpy as jnp
from jax import lax
from jax.experimental import pallas as pl
from jax.experimental.pallas import tpu as pltpu

F32 = jnp.float32
BF16 = jnp.bfloat16

D_MODEL = 2048
BATCH = 2
SEQ = 4096
DEPTH = 2
DEC_BATCH = 128
DEC_SEQ = 4
RMS_EPS = 1e-6
D_FF = 5632
N_HEADS = 32
N_KV_HEADS = 8
HEAD_DIM = 64
GROUP = N_HEADS // N_KV_HEADS
WINDOW = 128
NUM_BUCKETS = 32
MAX_DISTANCE = 128
NEG_INF = -1e30
GLA_HEADS = 4
GLA_DK = 256
GLA_DV = 512
GLA_KEY_DIM = GLA_HEADS * GLA_DK
GLA_VAL_DIM = GLA_HEADS * GLA_DV
GATE_RANK = 16
GATE_NORMALIZER = 16.0
GLA_MAIN_DIM = 2 * GLA_KEY_DIM + 2 * GLA_VAL_DIM
Q_DIM = N_HEADS * HEAD_DIM
KV_COLS = N_KV_HEADS * HEAD_DIM

N_PROMPT = BATCH * SEQ
N_SAMPLE = DEC_BATCH * DEC_SEQ
N_TOK = N_PROMPT + N_SAMPLE

LANES = 128
SUBLANES = 8
VMEM_LIMIT = 56 * 1024 * 1024
BIG_VMEM_LIMIT = 60 * 1024 * 1024

TM = 512
TN = 512
GLA_C = 64
NB = SEQ // WINDOW
N_PROMPT_TILES = N_PROMPT // TM


def _rms(x, w):
    return x * lax.rsqrt(jnp.mean(x * x, axis=-1, keepdims=True) + RMS_EPS) * w


def _silu(x):
    return x * jax.nn.sigmoid(x)


def _cparams(sem, vmem_limit=VMEM_LIMIT):
    return pltpu.CompilerParams(dimension_semantics=sem, vmem_limit_bytes=vmem_limit)


def _row_pair_specs(width):
    return [pl.BlockSpec((TM, width), lambda i: (jnp.minimum(i, N_PROMPT_TILES - 1), 0)),
            pl.BlockSpec((TM, width), lambda i: (jnp.maximum(i - N_PROMPT_TILES, 0), 0))]


def _on_row_source(fn, *ref_pairs):
    i = pl.program_id(0)
    pl.when(i < N_PROMPT_TILES)(lambda: fn(*[p[0] for p in ref_pairs]))
    pl.when(i >= N_PROMPT_TILES)(lambda: fn(*[p[1] for p in ref_pairs]))


FFN_TM = 1024
FFN_TF = 256
FFN_TF_SAMPLE = 512


def _ffn_body(final_norm, emit_w, has_init, x_ref, nw_ref, wg_ref, wu_ref, wd_ref, *rest):
    rest = list(rest)
    fw_ref = rest.pop(0) if final_norm else None
    if has_init:
        rest.pop(0)
    o_ref = rest.pop(0)
    h_ref = rest.pop()
    j = pl.program_id(1)

    def step(first):
        if first:
            h_ref[...] = _rms(x_ref[...], nw_ref[...]).astype(BF16)
        if emit_w:
            wgo_ref, wuo_ref, wdo_ref = rest
            wgo_ref[...] = wg_ref[0].astype(BF16)
            wuo_ref[...] = wu_ref[0].astype(BF16)
            wdo_ref[...] = wd_ref[0].astype(BF16)
            wg, wu, wd = wgo_ref[...], wuo_ref[...], wdo_ref[...]
        else:
            wg, wu, wd = wg_ref[...], wu_ref[...], wd_ref[...]
        h = h_ref[...]
        g = jnp.dot(h, wg, preferred_element_type=F32)
        u = jnp.dot(h, wu, preferred_element_type=F32)
        a = (_silu(g) * (0.5 * u)).astype(BF16)
        d = jnp.dot(a, wd, preferred_element_type=F32)
        o_ref[...] = (x_ref[...] if first else o_ref[...]) + d

    pl.when(j == 0)(lambda: step(True))
    pl.when(j > 0)(lambda: step(False))

    if final_norm:
        @pl.when(j == pl.num_programs(1) - 1)
        def _():
            o_ref[...] = _rms(o_ref[...], fw_ref[...])


def _ffn(x, layer, nw, wg, wu, wd, final_w=None, tile0=0, n_tiles=None, y_init=None):
    m = x.shape[0]
    emit_w = wg.dtype == F32
    tm = min(FFN_TM, m)
    if n_tiles is None:
        n_tiles = m // tm
    tf = FFN_TF if emit_w else FFN_TF_SAMPLE
    nj = D_FF // tf
    final_norm = final_w is not None
    vec = pl.BlockSpec((1, D_MODEL), lambda i, j: (0, 0))
    if emit_w:
        w_specs = [
            pl.BlockSpec((1, D_MODEL, tf), lambda i, j: (layer, 0, j)),
            pl.BlockSpec((1, D_MODEL, tf), lambda i, j: (layer, 0, j)),
            pl.BlockSpec((1, tf, D_MODEL), lambda i, j: (layer, j, 0)),
        ]
    else:
        w_specs = [
            pl.BlockSpec((D_MODEL, tf), lambda i, j: (0, j)),
            pl.BlockSpec((D_MODEL, tf), lambda i, j: (0, j)),
            pl.BlockSpec((tf, D_MODEL), lambda i, j: (j, 0)),
        ]
    in_specs = [pl.BlockSpec((tm, D_MODEL), lambda i, j: (tile0 + i, 0)), vec] + w_specs
    args = [x, nw.reshape(1, D_MODEL), wg, wu, wd]
    if final_norm:
        in_specs.append(vec)
        args.append(final_w.reshape(1, D_MODEL))
    aliases = {}
    if y_init is not None:
        aliases = {len(args): 0}
        in_specs.append(pl.BlockSpec(memory_space=pl.ANY))
        args.append(y_init)
    out_specs = [pl.BlockSpec((tm, D_MODEL), lambda i, j: (tile0 + i, 0))]
    out_shape = [jax.ShapeDtypeStruct((m, D_MODEL), F32)]
    if emit_w:
        once = lambda i, j: jnp.where(i == 0, j, nj - 1)
        out_specs += [
            pl.BlockSpec((D_MODEL, tf), lambda i, j: (0, once(i, j))),
            pl.BlockSpec((D_MODEL, tf), lambda i, j: (0, once(i, j))),
            pl.BlockSpec((tf, D_MODEL), lambda i, j: (once(i, j), 0)),
        ]
        out_shape += [
            jax.ShapeDtypeStruct((D_MODEL, D_FF), BF16),
            jax.ShapeDtypeStruct((D_MODEL, D_FF), BF16),
            jax.ShapeDtypeStruct((D_FF, D_MODEL), BF16),
        ]
    outs = pl.pallas_call(
        functools.partial(_ffn_body, final_norm, emit_w, y_init is not None),
        grid=(n_tiles, nj),
        in_specs=in_specs,
        out_specs=out_specs,
        out_shape=out_shape,
        scratch_shapes=[pltpu.VMEM((tm, D_MODEL), BF16)],
        input_output_aliases=aliases,
        compiler_params=_cparams(("arbitrary", "arbitrary"), BIG_VMEM_LIMIT),
        name="ffn",
    )(*args)
    return outs[0], tuple(outs[1:])


def _ffn_head_body(final_norm, xp_ref, xs_ref, nw_ref, wg_ref, wu_ref, wd_ref, *rest):
    rest = list(rest)
    fw_ref = rest.pop(0) if final_norm else None
    op_ref, os_ref, wgo_ref, wuo_ref, wdo_ref, hp_ref, hs_ref = rest
    groups = ((xp_ref, op_ref, hp_ref), (xs_ref, os_ref, hs_ref))
    j = pl.program_id(0)

    def step(first):
        wgo_ref[...] = wg_ref[0].astype(BF16)
        wuo_ref[...] = wu_ref[0].astype(BF16)
        wdo_ref[...] = wd_ref[0].astype(BF16)
        wg, wu, wd = wgo_ref[...], wuo_ref[...], wdo_ref[...]
        for x_ref, o_ref, h_ref in groups:
            if first:
                h_ref[...] = _rms(x_ref[...], nw_ref[...]).astype(BF16)
            h = h_ref[...]
            g = jnp.dot(h, wg, preferred_element_type=F32)
            u = jnp.dot(h, wu, preferred_element_type=F32)
            a = (_silu(g) * (0.5 * u)).astype(BF16)
            d = jnp.dot(a, wd, preferred_element_type=F32)
            o_ref[...] = (x_ref[...] if first else o_ref[...]) + d

    pl.when(j == 0)(lambda: step(True))
    pl.when(j > 0)(lambda: step(False))

    if final_norm:
        @pl.when(j == pl.num_programs(0) - 1)
        def _():
            for _, o_ref, _ in groups:
                o_ref[...] = _rms(o_ref[...], fw_ref[...])


def _ffn_head(xp, xs, layer, nw, wg, wu, wd, final_w=None):
    tf = FFN_TF
    final_norm = final_w is not None
    once = pl.Buffered(1)
    vec = pl.BlockSpec((1, D_MODEL), lambda j: (0, 0))
    in_specs = [
        pl.BlockSpec((FFN_TM, D_MODEL), lambda j: (0, 0), pipeline_mode=once),
        pl.BlockSpec((N_SAMPLE, D_MODEL), lambda j: (0, 0), pipeline_mode=once),
        vec,
        pl.BlockSpec((1, D_MODEL, tf), lambda j: (layer, 0, j)),
        pl.BlockSpec((1, D_MODEL, tf), lambda j: (layer, 0, j)),
        pl.BlockSpec((1, tf, D_MODEL), lambda j: (layer, j, 0)),
    ]
    args = [xp, xs, nw.reshape(1, D_MODEL), wg, wu, wd]
    if final_norm:
        in_specs.append(vec)
        args.append(final_w.reshape(1, D_MODEL))
    outs = pl.pallas_call(
        functools.partial(_ffn_head_body, final_norm),
        grid=(D_FF // tf,),
        in_specs=in_specs,
        out_specs=[
            pl.BlockSpec((FFN_TM, D_MODEL), lambda j: (0, 0), pipeline_mode=once),
            pl.BlockSpec((N_SAMPLE, D_MODEL), lambda j: (0, 0), pipeline_mode=once),
            pl.BlockSpec((D_MODEL, tf), lambda j: (0, j)),
            pl.BlockSpec((D_MODEL, tf), lambda j: (0, j)),
            pl.BlockSpec((tf, D_MODEL), lambda j: (j, 0)),
        ],
        out_shape=[
            jax.ShapeDtypeStruct((xp.shape[0], D_MODEL), F32),
            jax.ShapeDtypeStruct((N_SAMPLE, D_MODEL), F32),
            jax.ShapeDtypeStruct((D_MODEL, D_FF), BF16),
            jax.ShapeDtypeStruct((D_MODEL, D_FF), BF16),
            jax.ShapeDtypeStruct((D_FF, D_MODEL), BF16),
        ],
        scratch_shapes=[pltpu.VMEM((FFN_TM, D_MODEL), BF16), pltpu.VMEM((N_SAMPLE, D_MODEL), BF16)],
        compiler_params=_cparams(("arbitrary",), BIG_VMEM_LIMIT),
        name="ffn_head",
    )(*args)
    return outs[0], outs[1], tuple(outs[2:])


def _ffn_pair(x_pair, layer, nw, wg, wu, wd, final_w=None):
    xp, xs = x_pair
    y_prompt, y_sample, w_bf16 = _ffn_head(xp, xs, layer, nw, wg, wu, wd, final_w)
    y_prompt, _ = _ffn(xp, layer, nw, *w_bf16, final_w, tile0=1, n_tiles=xp.shape[0] // FFN_TM - 1,
                       y_init=y_prompt)
    return y_prompt, y_sample


def _log_sigmoid(x):
    return jnp.minimum(x, 0.0) - jnp.log(1.0 + jnp.exp(-jnp.abs(x)))


def _norm_proj_body(segments, with_gate, xp_ref, xs_ref, nw_ref, w_ref, b_ref, *rest):
    if with_gate:
        w2_ref, bg_ref = rest[:2]
        rest = rest[2:]
    o_refs = rest[:len(segments)]

    def run(x_ref):
        h = _rms(x_ref[...], nw_ref[...]).astype(BF16)
        col = 0
        for (width, _), o_ref in zip(segments, o_refs):
            for c in range(0, width, TN):
                acc = jnp.dot(h, w_ref[:, col + c:col + c + TN], preferred_element_type=F32)
                o_ref[:, c:c + TN] = (acc + b_ref[:, col + c:col + c + TN]).astype(o_ref.dtype)
            col += width
        if with_gate:
            gk = jnp.dot(h, w_ref[:, col:col + GATE_RANK], preferred_element_type=F32)
            z = jnp.dot(gk.astype(BF16), w2_ref[...], preferred_element_type=F32) + bg_ref[...]
            rest[-1][...] = _log_sigmoid(z) / GATE_NORMALIZER
    _on_row_source(run, (xp_ref, xs_ref))


def _norm_proj(xs, nw, w, b, segments, name, gate=None):
    n = sum(width for width, _ in segments)
    in_specs = _row_pair_specs(D_MODEL) + [
        pl.BlockSpec((1, D_MODEL), lambda i: (0, 0)),
        pl.BlockSpec(w.shape, lambda i: (0, 0), pipeline_mode=pl.Buffered(1)),
        pl.BlockSpec((1, n), lambda i: (0, 0)),
    ]
    args = [*xs, nw.reshape(1, D_MODEL), w, b.reshape(1, n)]
    out_specs = [pl.BlockSpec((TM, width), lambda i: (i, 0)) for width, _ in segments]
    out_shape = [jax.ShapeDtypeStruct((N_TOK, width), dtype) for width, dtype in segments]
    if gate is not None:
        w2, bg = gate
        assert w.shape[1] == n + GATE_RANK, w.shape
        in_specs += [
            pl.BlockSpec((GATE_RANK, GLA_KEY_DIM), lambda i: (0, 0)),
            pl.BlockSpec((1, GLA_KEY_DIM), lambda i: (0, 0)),
        ]
        args += [w2, bg.reshape(1, GLA_KEY_DIM)]
        out_specs.append(pl.BlockSpec((TM, GLA_KEY_DIM), lambda i: (i, 0)))
        out_shape.append(jax.ShapeDtypeStruct((N_TOK, GLA_KEY_DIM), F32))
    return pl.pallas_call(
        functools.partial(_norm_proj_body, segments, gate is not None),
        grid=(N_TOK // TM,),
        in_specs=in_specs,
        out_specs=out_specs,
        out_shape=out_shape,
        compiler_params=_cparams(("arbitrary",), BIG_VMEM_LIMIT),
        name=name,
    )(*args)


def _proj_res_body(ap_ref, as_ref, w_ref, b_ref, rp_ref, rs_ref, op_ref, os_ref):
    def run(a_ref, r_ref, o_ref):
        a = a_ref[...].astype(BF16)
        for c in range(0, D_MODEL, TN):
            acc = jnp.dot(a, w_ref[:, c:c + TN], preferred_element_type=F32)
            o_ref[:, c:c + TN] = r_ref[:, c:c + TN] + acc + b_ref[:, c:c + TN]
    _on_row_source(run, (ap_ref, as_ref), (rp_ref, rs_ref), (op_ref, os_ref))


def _proj_res(a_pair, w, b, res_pair, name):
    k = w.shape[0]
    return pl.pallas_call(
        _proj_res_body,
        grid=(N_TOK // TM,),
        in_specs=_row_pair_specs(k) + [
            pl.BlockSpec((k, D_MODEL), lambda i: (0, 0), pipeline_mode=pl.Buffered(1)),
            pl.BlockSpec((1, D_MODEL), lambda i: (0, 0)),
        ] + _row_pair_specs(D_MODEL),
        out_specs=_row_pair_specs(D_MODEL),
        out_shape=[jax.ShapeDtypeStruct((N_PROMPT, D_MODEL), F32), jax.ShapeDtypeStruct((N_SAMPLE, D_MODEL), F32)],
        compiler_params=_cparams(("arbitrary",)),
        name=name,
    )(*a_pair, w, b.reshape(1, D_MODEL), *res_pair)


def _t5_bucket_table():
    i = np.arange(WINDOW)[None, :]
    j = np.arange(2 * WINDOW)[:, None]
    n = np.maximum(WINDOW + i - j, 0)
    max_exact = NUM_BUCKETS // 2
    nf = np.maximum(n, 1).astype(np.float32)
    large = max_exact + (np.log(nf / np.float32(max_exact)) / np.float32(math.log(MAX_DISTANCE / max_exact))
                         * np.float32(NUM_BUCKETS - max_exact)).astype(np.int32)
    large = np.minimum(large, NUM_BUCKETS - 1)
    return np.where(n < max_exact, n, large).astype(np.int32)


KV_PER_STEP = LANES // HEAD_DIM
N_PAIRS = N_KV_HEADS // KV_PER_STEP
Q_COLS_PER_STEP = KV_PER_STEP * GROUP * HEAD_DIM
PAIR_SLOTS = GROUP * KV_PER_STEP


def _slot_head(slot):
    pair = slot // PAIR_SLOTS
    g = (slot // KV_PER_STEP) % GROUP
    hh = slot % KV_PER_STEP
    return (pair * KV_PER_STEP + hh) * GROUP + g


BIAS_SLOTS_PER_STEP = 8


def _bias_table_body(bucket_ref, rb_ref, o_ref):
    bucket = bucket_ref[...]
    j = lax.broadcasted_iota(jnp.int32, (2 * WINDOW, WINDOW), 0)
    i = lax.broadcasted_iota(jnp.int32, (2 * WINDOW, WINDOW), 1)
    dist = WINDOW + i - j
    in_window = (dist >= 0) & (dist < WINDOW)
    for t in range(BIAS_SLOTS_PER_STEP):
        h = _slot_head(pl.program_id(0) * BIAS_SLOTS_PER_STEP + t)
        acc = jnp.zeros((2 * WINDOW, WINDOW), F32)
        for b in range(NUM_BUCKETS):
            acc = jnp.where(bucket == b, rb_ref[b, h], acc)
        o_ref[t] = jnp.where(in_window, acc, NEG_INF)


def _bias_table(rel_bias):
    n = BIAS_SLOTS_PER_STEP
    return pl.pallas_call(
        _bias_table_body,
        grid=(N_HEADS // n,),
        in_specs=[
            pl.BlockSpec((2 * WINDOW, WINDOW), lambda h: (0, 0)),
            pl.BlockSpec(memory_space=pltpu.SMEM),
        ],
        out_specs=pl.BlockSpec((n, 2 * WINDOW, WINDOW), lambda h: (h, 0, 0)),
        out_shape=jax.ShapeDtypeStruct((N_HEADS, 2 * WINDOW, WINDOW), F32),
        name="bias_table",
    )(jnp.asarray(_t5_bucket_table()), rel_bias)


def _softmax_with_sink(s, sink_col):
    m = jnp.maximum(jnp.max(s, axis=-1, keepdims=True), sink_col)
    p = jnp.exp(s - m)
    denom = jnp.sum(p, axis=-1, keepdims=True) + jnp.exp(sink_col - m)
    return p, 1.0 / denom


PAIR_COLS = PAIR_SLOTS * WINDOW
ONES_ROWS = 16
PAIRS_PER_STEP = 4


def _swa_prompt_body(sink_ref, q_ref, kp_ref, ko_ref, vp_ref, vo_ref, bias_ref, o_ref):
    first = pl.program_id(2) == 0
    args = (sink_ref, q_ref, kp_ref, ko_ref, vp_ref, vo_ref, bias_ref, o_ref)
    pl.when(first)(lambda: _swa_prompt_block(True, *args))
    pl.when(jnp.logical_not(first))(lambda: _swa_prompt_block(False, *args))


def _swa_prompt_block(mask_prev, sink_ref, q_ref, kp_ref, ko_ref, vp_ref, vo_ref, bias_ref, o_ref):
    head_a = lax.broadcasted_iota(jnp.int32, (WINDOW, LANES), 1) < HEAD_DIM
    for t in range(PAIRS_PER_STEP):
        pair = pl.program_id(0) * PAIRS_PER_STEP + t
        lanes = slice(t * LANES, (t + 1) * LANES)
        q0 = t * Q_COLS_PER_STEP
        k = jnp.concatenate([kp_ref[:, lanes], ko_ref[:, lanes]], axis=0).astype(BF16)
        v = jnp.concatenate([vp_ref[:, lanes], vo_ref[:, lanes]], axis=0)
        vt = jnp.concatenate([v.T, jnp.ones((ONES_ROWS, 2 * WINDOW), F32)], axis=0).astype(BF16)
        parts = []
        for g in range(GROUP):
            qg = q_ref[:, q0 + g * LANES:q0 + (g + 1) * LANES]
            zero = jnp.zeros_like(qg)
            parts += [jnp.where(head_a, qg, zero), jnp.where(head_a, zero, qg)]
        qbd = jnp.concatenate(parts, axis=0)
        st = lax.dot_general(k, qbd, (((1,), (1,)), ((), ())), preferred_element_type=F32)
        st = st + jnp.concatenate([bias_ref[t * PAIR_SLOTS + u] for u in range(PAIR_SLOTS)], axis=1)
        if mask_prev:
            st = jnp.concatenate([st[:WINDOW] + NEG_INF, st[WINDOW:]], axis=0)
        sink_row = jnp.concatenate(
            [jnp.full((1, WINDOW), sink_ref[_slot_head(pair * PAIR_SLOTS + slot)], F32)
             for slot in range(PAIR_SLOTS)], axis=1)
        m = jnp.maximum(jnp.max(st, axis=0, keepdims=True), sink_row)
        pt = jnp.exp(st - m).astype(BF16)
        oa = jnp.dot(vt, pt, preferred_element_type=F32)
        inv = 1.0 / (oa[LANES:LANES + 1] + jnp.exp(sink_row - m))
        o = oa[:LANES] * inv
        for g in range(GROUP):
            c = g * KV_PER_STEP * WINDOW
            ot = jnp.concatenate([o[:HEAD_DIM, c:c + WINDOW], o[HEAD_DIM:, c + WINDOW:c + 2 * WINDOW]], axis=0)
            o_ref[:, q0 + g * LANES:q0 + (g + 1) * LANES] = ot.T.astype(o_ref.dtype)


def _swa_prompt(q, kv, bias_tbl, sinks):
    n = PAIRS_PER_STEP
    v_col0 = KV_COLS // (n * LANES)

    def prev(p, b, i):
        return b * NB + jnp.maximum(i - 1, 0)

    return pl.pallas_call(
        _swa_prompt_body,
        grid=(N_PAIRS // n, BATCH, NB),
        in_specs=[
            pl.BlockSpec(memory_space=pltpu.SMEM),
            pl.BlockSpec((WINDOW, n * Q_COLS_PER_STEP), lambda p, b, i: (b * NB + i, p)),
            pl.BlockSpec((WINDOW, n * LANES), lambda p, b, i: (prev(p, b, i), p)),
            pl.BlockSpec((WINDOW, n * LANES), lambda p, b, i: (b * NB + i, p)),
            pl.BlockSpec((WINDOW, n * LANES), lambda p, b, i: (prev(p, b, i), v_col0 + p)),
            pl.BlockSpec((WINDOW, n * LANES), lambda p, b, i: (b * NB + i, v_col0 + p)),
            pl.BlockSpec((n * PAIR_SLOTS, 2 * WINDOW, WINDOW), lambda p, b, i: (p, 0, 0)),
        ],
        out_specs=pl.BlockSpec((WINDOW, n * Q_COLS_PER_STEP), lambda p, b, i: (b * NB + i, p)),
        out_shape=jax.ShapeDtypeStruct((N_PROMPT, Q_DIM), BF16),
        compiler_params=_cparams(("arbitrary", "arbitrary", "arbitrary")),
        name="swa_prompt",
    )(sinks, q, kv, kv, kv, kv, bias_tbl)


S_ROWS = N_HEADS * DEC_SEQ
SWA_SEQS_PER_STEP = 8


def _swa_sample_body(q_ref, kn_ref, vn_ref, ck_ref, cv_ref, bias_ref, sink_ref,
                     o_ref, ko_ref, vo_ref):
    slot = lax.broadcasted_iota(jnp.int32, (S_ROWS, KV_COLS), 0) % N_HEADS
    row_kv = slot // PAIR_SLOTS * KV_PER_STEP + slot % KV_PER_STEP
    col_kv = lax.broadcasted_iota(jnp.int32, (S_ROWS, KV_COLS), 1) // HEAD_DIM
    own = row_kv == col_kv
    pad = jnp.zeros((WINDOW - SUBLANES, KV_COLS), F32)
    keep = WINDOW - DEC_SEQ
    for i in range(SWA_SEQS_PER_STEP):
        x = q_ref[i].astype(BF16)
        xt = jnp.concatenate([x] * N_KV_HEADS, axis=1)
        qbd = jnp.where(own, xt, jnp.zeros_like(xt))
        kk = jnp.concatenate([ck_ref[i], kn_ref[i], pad], axis=0).astype(BF16)
        vv = jnp.concatenate([cv_ref[i], vn_ref[i], pad], axis=0).astype(BF16)
        s = lax.dot_general(qbd, kk, (((1,), (1,)), ((), ())), preferred_element_type=F32)
        s = s + bias_ref[...]
        p, inv = _softmax_with_sink(s, sink_ref[...])
        of = jnp.dot(p.astype(BF16), vv, preferred_element_type=F32)
        of = jnp.where(own, of, 0.0)
        o = of[:, 0:HEAD_DIM]
        for c in range(1, N_KV_HEADS):
            o = o + of[:, c * HEAD_DIM:(c + 1) * HEAD_DIM]
        o_ref[i] = o * inv
        ko_ref[i, 0:keep, :] = ck_ref[i, DEC_SEQ:WINDOW, :]
        ko_ref[i, keep:WINDOW, :] = kn_ref[i, 0:DEC_SEQ, :]
        vo_ref[i, 0:keep, :] = cv_ref[i, DEC_SEQ:WINDOW, :]
        vo_ref[i, keep:WINDOW, :] = vn_ref[i, 0:DEC_SEQ, :]


def _swa_sample(q_rows, k_new8, v_new8, cache_k, cache_v, bias_s, sink_col):
    n = SWA_SEQS_PER_STEP
    seq3 = lambda s: (s, 0, 0)
    full2 = lambda s: (0, 0)
    return pl.pallas_call(
        _swa_sample_body,
        grid=(DEC_BATCH // n,),
        in_specs=[
            pl.BlockSpec((n, S_ROWS, HEAD_DIM), seq3),
            pl.BlockSpec((n, SUBLANES, KV_COLS), seq3),
            pl.BlockSpec((n, SUBLANES, KV_COLS), seq3),
            pl.BlockSpec((n, WINDOW, KV_COLS), seq3),
            pl.BlockSpec((n, WINDOW, KV_COLS), seq3),
            pl.BlockSpec((S_ROWS, 2 * WINDOW), full2),
            pl.BlockSpec((S_ROWS, 1), full2),
        ],
        out_specs=[
            pl.BlockSpec((n, S_ROWS, HEAD_DIM), seq3),
            pl.BlockSpec((n, WINDOW, KV_COLS), seq3),
            pl.BlockSpec((n, WINDOW, KV_COLS), seq3),
        ],
        out_shape=[
            jax.ShapeDtypeStruct((DEC_BATCH, S_ROWS, HEAD_DIM), F32),
            jax.ShapeDtypeStruct((DEC_BATCH, WINDOW, KV_COLS), F32),
            jax.ShapeDtypeStruct((DEC_BATCH, WINDOW, KV_COLS), F32),
        ],
        compiler_params=_cparams(("arbitrary",)),
        name="swa_sample",
    )(q_rows, k_new8, v_new8, cache_k, cache_v, bias_s, sink_col)


def _gla_out(o, gate, norm_w):
    return _rms(o, norm_w) * _silu(gate)


def _split3(x):
    hi = x.astype(BF16)
    r = x - hi.astype(F32)
    mid = r.astype(BF16)
    lo = (r - mid.astype(F32)).astype(BF16)
    return hi, mid, lo


def _cumsum_rows(g):
    c = g.shape[0]
    tri = (lax.broadcasted_iota(jnp.int32, (c, c), 0) >= lax.broadcasted_iota(jnp.int32, (c, c), 1)).astype(BF16)
    return jnp.dot(jnp.concatenate([tri] * 3, axis=1), jnp.concatenate(_split3(g), axis=0),
                   preferred_element_type=F32)


def _causal(a):
    c = a.shape[0]
    keep = lax.broadcasted_iota(jnp.int32, (c, c), 0) >= lax.broadcasted_iota(jnp.int32, (c, c), 1)
    return jnp.where(keep, a, 0.0)


_NT = (((1,), (1,)), ((), ()))
_TN = (((0,), (0,)), ((), ()))


def _gla_prompt_body(*refs):
    proj_refs = refs[:BATCH]
    la_refs = refs[BATCH:2 * BATCH]
    nw_ref, o_ref, s_ref, st_ref = refs[2 * BATCH:]
    c = pl.program_id(0)

    @pl.when(c == 0)
    def _():
        st_ref[...] = jnp.zeros_like(st_ref)

    for bi in range(BATCH):
        p_ref = proj_refs[bi]
        b_all = _cumsum_rows(la_refs[bi][...])
        for h in range(GLA_HEADS):
            kc = slice(h * GLA_DK, (h + 1) * GLA_DK)
            v0 = 2 * GLA_KEY_DIM + h * GLA_DV
            b = b_all[:, kc]
            q = p_ref[:, kc].astype(F32) * GLA_DK ** -0.5
            k = p_ref[:, GLA_KEY_DIM + h * GLA_DK:GLA_KEY_DIM + (h + 1) * GLA_DK].astype(F32)
            v = p_ref[:, v0:v0 + GLA_DV]
            gate = p_ref[:, v0 + GLA_VAL_DIM:v0 + GLA_VAL_DIM + GLA_DV].astype(F32)
            qe = (q * jnp.exp(b)).astype(BF16)
            r = b[GLA_C // 2 - 1:GLA_C // 2, :]
            qr = (q * jnp.exp(b - r)).astype(BF16)
            kr = (k * jnp.exp(r - b)).astype(BF16)
            a = _causal(lax.dot_general(qr, kr, _NT, preferred_element_type=F32))
            st = st_ref[bi, h]
            o = (jnp.dot(a.astype(BF16), v, preferred_element_type=F32)
                 + lax.dot_general(qe, st.astype(BF16), _NT, preferred_element_type=F32))
            b_last = b[GLA_C - 1:GLA_C, :]
            kd = (k * jnp.exp(b_last - b)).astype(BF16)
            st_new = st * jnp.exp(b_last) + lax.dot_general(v, kd, _TN, preferred_element_type=F32)
            st_ref[bi, h] = st_new
            o_ref[bi, :, h * GLA_DV:(h + 1) * GLA_DV] = _gla_out(o, gate, nw_ref[...]).astype(o_ref.dtype)

    @pl.when(c == pl.num_programs(0) - 1)
    def _():
        for bi in range(BATCH):
            for h in range(GLA_HEADS):
                s_ref[bi, h] = st_ref[bi, h].T


def _gla_prompt(proj, log_a, norm_w):
    nc = SEQ // GLA_C
    rows = [functools.partial(lambda bi, c: (bi * nc + c, 0), bi) for bi in range(BATCH)]
    o, s = pl.pallas_call(
        _gla_prompt_body,
        grid=(nc,),
        in_specs=([pl.BlockSpec((GLA_C, GLA_MAIN_DIM), r) for r in rows]
                  + [pl.BlockSpec((GLA_C, GLA_KEY_DIM), r) for r in rows]
                  + [pl.BlockSpec((1, GLA_DV), lambda c: (0, 0))]),
        out_specs=[
            pl.BlockSpec((BATCH, GLA_C, GLA_VAL_DIM), lambda c: (0, c, 0)),
            pl.BlockSpec((BATCH, GLA_HEADS, GLA_DK, GLA_DV), lambda c: (0, 0, 0, 0)),
        ],
        out_shape=[
            jax.ShapeDtypeStruct((BATCH, SEQ, GLA_VAL_DIM), BF16),
            jax.ShapeDtypeStruct((BATCH, GLA_HEADS, GLA_DK, GLA_DV), F32),
        ],
        scratch_shapes=[pltpu.VMEM((BATCH, GLA_HEADS, GLA_DV, GLA_DK), F32)],
        compiler_params=_cparams(("arbitrary",)),
        name="gla_prompt",
    )(*([proj] * BATCH + [log_a] * BATCH + [norm_w.reshape(1, GLA_DV)]))
    return o.reshape(N_PROMPT, GLA_VAL_DIM), s


GLA_SEQS_PER_STEP = 4


def _gla_sample_body(proj_ref, la_ref, s0_ref, nw_ref, o_ref, s_ref):
    ones = jnp.ones((DEC_SEQ, LANES), BF16)
    proj = proj_ref[...].astype(F32)
    la = la_ref[...]
    for i in range(GLA_SEQS_PER_STEP):
        rows_i = slice(i * DEC_SEQ, (i + 1) * DEC_SEQ)
        for h in range(GLA_HEADS):
            q = proj[rows_i, h * GLA_DK:(h + 1) * GLA_DK]
            k = proj[rows_i, GLA_KEY_DIM + h * GLA_DK:GLA_KEY_DIM + (h + 1) * GLA_DK]
            v0 = 2 * GLA_KEY_DIM + h * GLA_DV
            v = proj[rows_i, v0:v0 + GLA_DV].astype(BF16)
            gate = proj[rows_i, v0 + GLA_VAL_DIM:v0 + GLA_VAL_DIM + GLA_DV]
            g = la[rows_i, h * GLA_DK:(h + 1) * GLA_DK]
            rows = [g[0:1]]
            for t in range(1, DEC_SEQ):
                rows.append(rows[-1] + g[t:t + 1])
            b = jnp.concatenate(rows, axis=0)
            b_last = rows[-1]
            qe = (q * GLA_DK ** -0.5 * jnp.exp(b)).astype(BF16)
            ke = (k * jnp.exp(-b)).astype(BF16)
            a = _causal(lax.dot_general(qe, ke, _NT, preferred_element_type=F32))
            s0 = s0_ref[i, h]
            o = (jnp.dot(a.astype(BF16), v, preferred_element_type=F32)
                 + jnp.dot(qe, s0.astype(BF16), preferred_element_type=F32))
            kd = (k * jnp.exp(b_last - b)).astype(BF16)
            dsum = sum(lax.dot_general(piece, ones, _TN, preferred_element_type=F32) for piece in _split3(g))
            decay = jnp.concatenate([jnp.exp(dsum)] * (GLA_DV // LANES), axis=1)
            s_ref[i, h] = s0 * decay + lax.dot_general(kd, v, _TN, preferred_element_type=F32)
            o_ref[rows_i, h * GLA_DV:(h + 1) * GLA_DV] = _gla_out(o, gate, nw_ref[...])


def _gla_sample(proj, log_a, state, norm_w):
    n = GLA_SEQS_PER_STEP
    rows = n * DEC_SEQ
    first = N_PROMPT // rows
    seq4 = lambda s: (s, 0, 0, 0)
    return pl.pallas_call(
        _gla_sample_body,
        grid=(DEC_BATCH // n,),
        in_specs=[
            pl.BlockSpec((rows, GLA_MAIN_DIM), lambda s: (first + s, 0)),
            pl.BlockSpec((rows, GLA_KEY_DIM), lambda s: (first + s, 0)),
            pl.BlockSpec((n, GLA_HEADS, GLA_DK, GLA_DV), seq4),
            pl.BlockSpec((1, GLA_DV), lambda s: (0, 0)),
        ],
        out_specs=[
            pl.BlockSpec((rows, GLA_VAL_DIM), lambda s: (s, 0)),
            pl.BlockSpec((n, GLA_HEADS, GLA_DK, GLA_DV), seq4),
        ],
        out_shape=[
            jax.ShapeDtypeStruct((N_SAMPLE, GLA_VAL_DIM), F32),
            jax.ShapeDtypeStruct((DEC_BATCH, GLA_HEADS, GLA_DK, GLA_DV), F32),
        ],
        compiler_params=_cparams(("arbitrary",)),
        name="gla_sample",
    )(proj, log_a, state, norm_w.reshape(1, GLA_DV))


CAST_ROWS = 512


def _cast_body(w_ref, o_ref):
    o_ref[...] = w_ref[0].astype(BF16)


def _layer_bf16(w, layer):
    _, rows, cols = w.shape
    return pl.pallas_call(
        _cast_body,
        grid=(rows // CAST_ROWS,),
        in_specs=[pl.BlockSpec((1, CAST_ROWS, cols), lambda i: (layer, i, 0))],
        out_specs=pl.BlockSpec((CAST_ROWS, cols), lambda i: (i, 0)),
        out_shape=jax.ShapeDtypeStruct((rows, cols), BF16),
        compiler_params=_cparams(("arbitrary",)),
        name="cast_bf16",
    )(w)


def _slot_rows_body(*refs):
    o_ref = refs[-1]
    for u, w_ref in enumerate(refs[:-1]):
        o_ref[u * HEAD_DIM:(u + 1) * HEAD_DIM] = w_ref[0].astype(BF16)


def _slot_rows_bf16(w, layer):
    def src(u):
        g, hh = u // KV_PER_STEP, u % KV_PER_STEP
        return lambda pair: (layer, (pair * KV_PER_STEP + hh) * GROUP + g, 0)

    cols = w.shape[2]
    return pl.pallas_call(
        _slot_rows_body,
        grid=(N_PAIRS,),
        in_specs=[pl.BlockSpec((1, HEAD_DIM, cols), src(u)) for u in range(PAIR_SLOTS)],
        out_specs=pl.BlockSpec((PAIR_SLOTS * HEAD_DIM, cols), lambda pair: (pair, 0)),
        out_shape=jax.ShapeDtypeStruct((Q_DIM, cols), BF16),
        compiler_params=_cparams(("arbitrary",)),
        name="slot_rows_bf16",
    )(*([w] * PAIR_SLOTS))


def _swa_layer(x, cache_k, cache_v, norm_w, w_qkv, b_qkv, j, w_o_all, b_o, sinks, rel_bias):
    slots = (N_PAIRS, KV_PER_STEP, GROUP, HEAD_DIM)
    q_scale = HEAD_DIM ** -0.5
    assert math.frexp(q_scale)[0] == 0.5, q_scale
    w_q = (w_qkv[:, :Q_DIM] * q_scale).reshape((D_MODEL,) + slots).transpose(0, 1, 3, 2, 4).reshape(D_MODEL, Q_DIM)
    b_q = (b_qkv[:Q_DIM] * q_scale).reshape(slots).transpose(0, 2, 1, 3).reshape(Q_DIM)
    w_qkv_s = jnp.concatenate([w_q, w_qkv[:, Q_DIM:]], axis=1).astype(BF16)
    b_qkv_s = jnp.concatenate([b_q, b_qkv[Q_DIM:]])
    w_o_s = _slot_rows_bf16(w_o_all, j)

    q, kv = _norm_proj(x, norm_w, w_qkv_s, b_qkv_s, ((Q_DIM, BF16), (2 * KV_COLS, F32)), "swa_qkv")
    bias_tbl = _bias_table(rel_bias)
    o_p = _swa_prompt(q, kv, bias_tbl, sinks)

    q_rows = q[N_PROMPT:].astype(F32).reshape(DEC_BATCH, S_ROWS, HEAD_DIM)
    kv_s = kv[N_PROMPT:].reshape(DEC_BATCH, DEC_SEQ, 2 * KV_COLS)
    pad8 = ((0, 0), (0, SUBLANES - DEC_SEQ), (0, 0))
    k_new8 = jnp.pad(kv_s[..., :KV_COLS], pad8)
    v_new8 = jnp.pad(kv_s[..., KV_COLS:], pad8)
    bias_s = bias_tbl[:, :, :DEC_SEQ].transpose(2, 0, 1).reshape(S_ROWS, 2 * WINDOW)
    slot_heads = np.array([_slot_head(s) for s in range(N_HEADS)])
    sink_col = jnp.tile(sinks[slot_heads], DEC_SEQ).reshape(S_ROWS, 1)
    o_s, k_s, v_s = _swa_sample(q_rows, k_new8, v_new8,
                                cache_k.reshape(DEC_BATCH, WINDOW, KV_COLS),
                                cache_v.reshape(DEC_BATCH, WINDOW, KV_COLS), bias_s, sink_col)
    o_s = o_s.reshape(N_SAMPLE, Q_DIM)
    x = _proj_res((o_p, o_s), w_o_s, b_o, x, "swa_out")

    kv_p = jnp.stack([kv[(b + 1) * SEQ - WINDOW:(b + 1) * SEQ] for b in range(BATCH)])
    k_p = kv_p[..., :KV_COLS].reshape(BATCH, WINDOW, N_KV_HEADS, HEAD_DIM)
    v_p = kv_p[..., KV_COLS:].reshape(BATCH, WINDOW, N_KV_HEADS, HEAD_DIM)
    shape_s = (DEC_BATCH, WINDOW, N_KV_HEADS, HEAD_DIM)
    return x, k_p, v_p, k_s.reshape(shape_s), v_s.reshape(shape_s)


def _gla_layer(x, state, norm_w, j, w_in_all, w_gk2, b_gk, gnorm, w_o_all):
    proj, log_a = _norm_proj(x, norm_w, w_in_all[j].astype(BF16), jnp.zeros((GLA_MAIN_DIM,), F32),
                             ((GLA_MAIN_DIM, BF16),), "gla_in", gate=(w_gk2.astype(BF16), b_gk))
    o_p, s_p = _gla_prompt(proj, log_a, gnorm)
    o_s, s_s = _gla_sample(proj, log_a, state, gnorm)
    x = _proj_res((o_p, o_s), _layer_bf16(w_o_all, j), jnp.zeros((D_MODEL,), F32), x, "gla_out")
    return x, s_p, s_s


def kernel(x_prompt, x_sample, cache_swa_k, cache_swa_v, state_gla, norm_ffn1, ffn1_w_gate, ffn1_w_up,
           ffn1_w_down, norm_mix, norm_ffn2, ffn2_w_gate, ffn2_w_up, ffn2_w_down, norm_final, rel_bias,
           swa_w_qkv, swa_b_qkv, swa_w_o, swa_b_o, swa_sinks, gla_w_in, gla_w_gk2, gla_b_gk, gla_norm,
           gla_w_o):
    x = (x_prompt.reshape(N_PROMPT, D_MODEL), x_sample.reshape(N_SAMPLE, D_MODEL))
    swa_kp, swa_vp, swa_ks, swa_vs, gla_sp, gla_ss = [], [], [], [], [], []
    for i in range(DEPTH):
        x = _ffn_pair(x, i, norm_ffn1[i], ffn1_w_gate, ffn1_w_up, ffn1_w_down)
        j = i // 2
        if i % 2 == 0:
            x, kp, vp, ks, vs = _swa_layer(x, cache_swa_k[j], cache_swa_v[j], norm_mix[i], swa_w_qkv[j],
                                           swa_b_qkv[j], j, swa_w_o, swa_b_o[j], swa_sinks[j], rel_bias)
            swa_kp.append(kp)
            swa_vp.append(vp)
            swa_ks.append(ks)
            swa_vs.append(vs)
        else:
            x, sp, ss = _gla_layer(x, state_gla[j], norm_mix[i], j, gla_w_in, gla_w_gk2[j], gla_b_gk[j],
                                   gla_norm[j], gla_w_o)
            gla_sp.append(sp)
            gla_ss.append(ss)
        final_w = norm_final if i == DEPTH - 1 else None
        x = _ffn_pair(x, i, norm_ffn2[i], ffn2_w_gate, ffn2_w_up, ffn2_w_down, final_w)
    y_prompt = x[0].reshape(BATCH, SEQ, D_MODEL)
    y_sample = x[1].reshape(DEC_BATCH, DEC_SEQ, D_MODEL)
    return (y_prompt, y_sample, jnp.stack(swa_kp), jnp.stack(swa_vp), jnp.stack(swa_ks), jnp.stack(swa_vs),
            jnp.stack(gla_sp), jnp.stack(gla_ss))
```

```python
import functools
import math

import numpy as np
import jax
import jax.numpy as jnp
from jax import lax
from jax.experimental import pallas as pl
from jax.experimental.pallas import tpu as pltpu

F32 = jnp.float32
BF16 = jnp.bfloat16

D_MODEL = 2048
BATCH = 2
SEQ = 4096
DEPTH = 2
DEC_BATCH = 128
DEC_SEQ = 4
RMS_EPS = 1e-6
D_FF = 5632
N_HEADS = 32
N_KV_HEADS = 8
HEAD_DIM = 64
GROUP = N_HEADS // N_KV_HEADS
WINDOW = 128
NUM_BUCKETS = 32
MAX_DISTANCE = 128
NEG_INF = -1e30
GLA_HEADS = 4
GLA_DK = 256
GLA_DV = 512
GLA_KEY_DIM = GLA_HEADS * GLA_DK
GLA_VAL_DIM = GLA_HEADS * GLA_DV
GATE_RANK = 16
GATE_NORMALIZER = 16.0
GLA_MAIN_DIM = 2 * GLA_KEY_DIM + 2 * GLA_VAL_DIM
Q_DIM = N_HEADS * HEAD_DIM
KV_COLS = N_KV_HEADS * HEAD_DIM

N_PROMPT = BATCH * SEQ
N_SAMPLE = DEC_BATCH * DEC_SEQ
N_TOK = N_PROMPT + N_SAMPLE

LANES = 128
SUBLANES = 8
VMEM_LIMIT = 56 * 1024 * 1024
BIG_VMEM_LIMIT = 60 * 1024 * 1024

TM = 512
TN = 512
GLA_C = 64
NB = SEQ // WINDOW
N_PROMPT_TILES = N_PROMPT // TM


def _rms(x, w):
    return x * lax.rsqrt(jnp.mean(x * x, axis=-1, keepdims=True) + RMS_EPS) * w


def _silu(x):
    return x * jax.nn.sigmoid(x)


def _cparams(sem, vmem_limit=VMEM_LIMIT):
    return pltpu.CompilerParams(dimension_semantics=sem, vmem_limit_bytes=vmem_limit)


def _row_pair_specs(width):
    return [pl.BlockSpec((TM, width), lambda i: (jnp.minimum(i, N_PROMPT_TILES - 1), 0)),
            pl.BlockSpec((TM, width), lambda i: (jnp.maximum(i - N_PROMPT_TILES, 0), 0))]


def _on_row_source(fn, *ref_pairs):
    i = pl.program_id(0)
    pl.when(i < N_PROMPT_TILES)(lambda: fn(*[p[0] for p in ref_pairs]))
    pl.when(i >= N_PROMPT_TILES)(lambda: fn(*[p[1] for p in ref_pairs]))


FFN_TM = 1024
FFN_TF_HEAD = 256
FFN_TF = 512


def _ffn_step(first, x_ref, nw_ref, h_ref, o_ref, wg, wu, wd):
    if first:
        h_ref[...] = _rms(x_ref[...], nw_ref[...]).astype(BF16)
    h = h_ref[...]
    g = jnp.dot(h, wg, preferred_element_type=F32)
    u = jnp.dot(h, wu, preferred_element_type=F32)
    a = (_silu(g) * (0.5 * u)).astype(BF16)
    d = jnp.dot(a, wd, preferred_element_type=F32)
    o_ref[...] = (x_ref[...] if first else o_ref[...]) + d


def _ffn_body(final_norm, x_ref, nw_ref, wg_ref, wu_ref, wd_ref, *rest):
    rest = list(rest)
    fw_ref = rest.pop(0) if final_norm else None
    _, o_ref, h_ref = rest
    j = pl.program_id(1)

    def step(first):
        _ffn_step(first, x_ref, nw_ref, h_ref, o_ref, wg_ref[...], wu_ref[...], wd_ref[...])

    pl.when(j == 0)(lambda: step(True))
    pl.when(j > 0)(lambda: step(False))

    if final_norm:
        @pl.when(j == pl.num_programs(1) - 1)
        def _():
            o_ref[...] = _rms(o_ref[...], fw_ref[...])


def _ffn(x, nw, wg, wu, wd, final_w, y_init):
    m = x.shape[0]
    final_norm = final_w is not None
    vec = pl.BlockSpec((1, D_MODEL), lambda i, j: (0, 0))
    rows = pl.BlockSpec((FFN_TM, D_MODEL), lambda i, j: (1 + i, 0))
    in_specs = [
        rows, vec,
        pl.BlockSpec((D_MODEL, FFN_TF), lambda i, j: (0, j)),
        pl.BlockSpec((D_MODEL, FFN_TF), lambda i, j: (0, j)),
        pl.BlockSpec((FFN_TF, D_MODEL), lambda i, j: (j, 0)),
    ]
    args = [x, nw.reshape(1, D_MODEL), wg, wu, wd]
    if final_norm:
        in_specs.append(vec)
        args.append(final_w.reshape(1, D_MODEL))
    in_specs.append(pl.BlockSpec(memory_space=pl.ANY))
    args.append(y_init)
    return pl.pallas_call(
        functools.partial(_ffn_body, final_norm),
        grid=(m // FFN_TM - 1, D_FF // FFN_TF),
        in_specs=in_specs,
        out_specs=rows,
        out_shape=jax.ShapeDtypeStruct((m, D_MODEL), F32),
        scratch_shapes=[pltpu.VMEM((FFN_TM, D_MODEL), BF16)],
        input_output_aliases={len(args) - 1: 0},
        compiler_params=_cparams(("arbitrary", "arbitrary"), BIG_VMEM_LIMIT),
        name="ffn",
    )(*args)


def _ffn_head_body(final_norm, xp_ref, xs_ref, nw_ref, wg_ref, wu_ref, wd_ref, *rest):
    rest = list(rest)
    fw_ref = rest.pop(0) if final_norm else None
    op_ref, os_ref, wgo_ref, wuo_ref, wdo_ref, hp_ref, hs_ref = rest
    groups = ((xp_ref, op_ref, hp_ref), (xs_ref, os_ref, hs_ref))
    j = pl.program_id(0)

    def step(first):
        wgo_ref[...] = wg_ref[0].astype(BF16)
        wuo_ref[...] = wu_ref[0].astype(BF16)
        wdo_ref[...] = wd_ref[0].astype(BF16)
        for x_ref, o_ref, h_ref in groups:
            _ffn_step(first, x_ref, nw_ref, h_ref, o_ref, wgo_ref[...], wuo_ref[...], wdo_ref[...])

    pl.when(j == 0)(lambda: step(True))
    pl.when(j > 0)(lambda: step(False))

    if final_norm:
        @pl.when(j == pl.num_programs(0) - 1)
        def _():
            for _, o_ref, _ in groups:
                o_ref[...] = _rms(o_ref[...], fw_ref[...])


def _ffn_head(xp, xs, layer, nw, wg, wu, wd, final_w=None):
    tf = FFN_TF_HEAD
    final_norm = final_w is not None
    once = pl.Buffered(1)
    vec = pl.BlockSpec((1, D_MODEL), lambda j: (0, 0))
    in_specs = [
        pl.BlockSpec((FFN_TM, D_MODEL), lambda j: (0, 0), pipeline_mode=once),
        pl.BlockSpec((N_SAMPLE, D_MODEL), lambda j: (0, 0), pipeline_mode=once),
        vec,
        pl.BlockSpec((1, D_MODEL, tf), lambda j: (layer, 0, j)),
        pl.BlockSpec((1, D_MODEL, tf), lambda j: (layer, 0, j)),
        pl.BlockSpec((1, tf, D_MODEL), lambda j: (layer, j, 0)),
    ]
    args = [xp, xs, nw.reshape(1, D_MODEL), wg, wu, wd]
    if final_norm:
        in_specs.append(vec)
        args.append(final_w.reshape(1, D_MODEL))
    outs = pl.pallas_call(
        functools.partial(_ffn_head_body, final_norm),
        grid=(D_FF // tf,),
        in_specs=in_specs,
        out_specs=[
            pl.BlockSpec((FFN_TM, D_MODEL), lambda j: (0, 0), pipeline_mode=once),
            pl.BlockSpec((N_SAMPLE, D_MODEL), lambda j: (0, 0), pipeline_mode=once),
            pl.BlockSpec((D_MODEL, tf), lambda j: (0, j)),
            pl.BlockSpec((D_MODEL, tf), lambda j: (0, j)),
            pl.BlockSpec((tf, D_MODEL), lambda j: (j, 0)),
        ],
        out_shape=[
            jax.ShapeDtypeStruct((xp.shape[0], D_MODEL), F32),
            jax.ShapeDtypeStruct((N_SAMPLE, D_MODEL), F32),
            jax.ShapeDtypeStruct((D_MODEL, D_FF), BF16),
            jax.ShapeDtypeStruct((D_MODEL, D_FF), BF16),
            jax.ShapeDtypeStruct((D_FF, D_MODEL), BF16),
        ],
        scratch_shapes=[pltpu.VMEM((FFN_TM, D_MODEL), BF16), pltpu.VMEM((N_SAMPLE, D_MODEL), BF16)],
        compiler_params=_cparams(("arbitrary",), BIG_VMEM_LIMIT),
        name="ffn_head",
    )(*args)
    return outs[0], outs[1], tuple(outs[2:])


def _ffn_pair(x_pair, layer, nw, wg, wu, wd, final_w=None):
    xp, xs = x_pair
    y_prompt, y_sample, w_bf16 = _ffn_head(xp, xs, layer, nw, wg, wu, wd, final_w)
    return _ffn(xp, nw, *w_bf16, final_w, y_prompt), y_sample


def _log_sigmoid(x):
    return jnp.minimum(x, 0.0) - jnp.log(1.0 + jnp.exp(-jnp.abs(x)))


def _norm_proj_body(segments, with_gate, xp_ref, xs_ref, nw_ref, w_ref, b_ref, *rest):
    if with_gate:
        w2_ref, bg_ref = rest[:2]
        rest = rest[2:]
    o_refs = rest[:len(segments)]

    def run(x_ref):
        h = _rms(x_ref[...], nw_ref[...]).astype(BF16)
        col = 0
        for (width, _), o_ref in zip(segments, o_refs):
            for c in range(0, width, TN):
                acc = jnp.dot(h, w_ref[:, col + c:col + c + TN], preferred_element_type=F32)
                o_ref[:, c:c + TN] = (acc + b_ref[:, col + c:col + c + TN]).astype(o_ref.dtype)
            col += width
        if with_gate:
            gk = jnp.dot(h, w_ref[:, col:col + GATE_RANK], preferred_element_type=F32)
            z = jnp.dot(gk.astype(BF16), w2_ref[...], preferred_element_type=F32) + bg_ref[...]
            rest[-1][...] = _log_sigmoid(z) / GATE_NORMALIZER
    _on_row_source(run, (xp_ref, xs_ref))


def _norm_proj(xs, nw, w, b, segments, name, gate=None):
    n = sum(width for width, _ in segments)
    in_specs = _row_pair_specs(D_MODEL) + [
        pl.BlockSpec((1, D_MODEL), lambda i: (0, 0)),
        pl.BlockSpec(w.shape, lambda i: (0, 0), pipeline_mode=pl.Buffered(1)),
        pl.BlockSpec((1, n), lambda i: (0, 0)),
    ]
    args = [*xs, nw.reshape(1, D_MODEL), w, b.reshape(1, n)]
    out_specs = [pl.BlockSpec((TM, width), lambda i: (i, 0)) for width, _ in segments]
    out_shape = [jax.ShapeDtypeStruct((N_TOK, width), dtype) for width, dtype in segments]
    if gate is not None:
        w2, bg = gate
        assert w.shape[1] == n + GATE_RANK, w.shape
        in_specs += [
            pl.BlockSpec((GATE_RANK, GLA_KEY_DIM), lambda i: (0, 0)),
            pl.BlockSpec((1, GLA_KEY_DIM), lambda i: (0, 0)),
        ]
        args += [w2, bg.reshape(1, GLA_KEY_DIM)]
        out_specs.append(pl.BlockSpec((TM, GLA_KEY_DIM), lambda i: (i, 0)))
        out_shape.append(jax.ShapeDtypeStruct((N_TOK, GLA_KEY_DIM), F32))
    return pl.pallas_call(
        functools.partial(_norm_proj_body, segments, gate is not None),
        grid=(N_TOK // TM,),
        in_specs=in_specs,
        out_specs=out_specs,
        out_shape=out_shape,
        compiler_params=_cparams(("arbitrary",), BIG_VMEM_LIMIT),
        name=name,
    )(*args)


def _proj_res_body(ap_ref, as_ref, w_ref, b_ref, rp_ref, rs_ref, op_ref, os_ref):
    def run(a_ref, r_ref, o_ref):
        a = a_ref[...].astype(BF16)
        for c in range(0, D_MODEL, TN):
            acc = jnp.dot(a, w_ref[:, c:c + TN], preferred_element_type=F32)
            o_ref[:, c:c + TN] = r_ref[:, c:c + TN] + acc + b_ref[:, c:c + TN]
    _on_row_source(run, (ap_ref, as_ref), (rp_ref, rs_ref), (op_ref, os_ref))


def _proj_res(a_pair, w, b, res_pair, name):
    k = w.shape[0]
    return pl.pallas_call(
        _proj_res_body,
        grid=(N_TOK // TM,),
        in_specs=_row_pair_specs(k) + [
            pl.BlockSpec((k, D_MODEL), lambda i: (0, 0), pipeline_mode=pl.Buffered(1)),
            pl.BlockSpec((1, D_MODEL), lambda i: (0, 0)),
        ] + _row_pair_specs(D_MODEL),
        out_specs=_row_pair_specs(D_MODEL),
        out_shape=[jax.ShapeDtypeStruct((N_PROMPT, D_MODEL), F32), jax.ShapeDtypeStruct((N_SAMPLE, D_MODEL), F32)],
        compiler_params=_cparams(("arbitrary",)),
        name=name,
    )(*a_pair, w, b.reshape(1, D_MODEL), *res_pair)


def _t5_bucket_table():
    i = np.arange(WINDOW)[None, :]
    j = np.arange(2 * WINDOW)[:, None]
    n = np.maximum(WINDOW + i - j, 0)
    max_exact = NUM_BUCKETS // 2
    nf = np.maximum(n, 1).astype(np.float32)
    large = max_exact + (np.log(nf / np.float32(max_exact)) / np.float32(math.log(MAX_DISTANCE / max_exact))
                         * np.float32(NUM_BUCKETS - max_exact)).astype(np.int32)
    large = np.minimum(large, NUM_BUCKETS - 1)
    return np.where(n < max_exact, n, large).astype(np.int32)


KV_PER_STEP = LANES // HEAD_DIM
N_PAIRS = N_KV_HEADS // KV_PER_STEP
Q_COLS_PER_STEP = KV_PER_STEP * GROUP * HEAD_DIM
PAIR_SLOTS = GROUP * KV_PER_STEP


def _slot_head(slot):
    pair = slot // PAIR_SLOTS
    g = (slot // KV_PER_STEP) % GROUP
    hh = slot % KV_PER_STEP
    return (pair * KV_PER_STEP + hh) * GROUP + g


BIAS_SLOTS_PER_STEP = 8


def _bias_table_body(bucket_ref, rb_ref, o_ref):
    bucket = bucket_ref[...]
    j = lax.broadcasted_iota(jnp.int32, (2 * WINDOW, WINDOW), 0)
    i = lax.broadcasted_iota(jnp.int32, (2 * WINDOW, WINDOW), 1)
    dist = WINDOW + i - j
    in_window = (dist >= 0) & (dist < WINDOW)
    for t in range(BIAS_SLOTS_PER_STEP):
        h = _slot_head(pl.program_id(0) * BIAS_SLOTS_PER_STEP + t)
        acc = jnp.zeros((2 * WINDOW, WINDOW), F32)
        for b in range(NUM_BUCKETS):
            acc = jnp.where(bucket == b, rb_ref[b, h], acc)
        o_ref[t] = jnp.where(in_window, acc, NEG_INF)


def _bias_table(rel_bias):
    n = BIAS_SLOTS_PER_STEP
    return pl.pallas_call(
        _bias_table_body,
        grid=(N_HEADS // n,),
        in_specs=[
            pl.BlockSpec((2 * WINDOW, WINDOW), lambda h: (0, 0)),
            pl.BlockSpec(memory_space=pltpu.SMEM),
        ],
        out_specs=pl.BlockSpec((n, 2 * WINDOW, WINDOW), lambda h: (h, 0, 0)),
        out_shape=jax.ShapeDtypeStruct((N_HEADS, 2 * WINDOW, WINDOW), F32),
        name="bias_table",
    )(jnp.asarray(_t5_bucket_table()), rel_bias)


def _softmax_with_sink(s, sink_col):
    m = jnp.maximum(jnp.max(s, axis=-1, keepdims=True), sink_col)
    p = jnp.exp(s - m)
    denom = jnp.sum(p, axis=-1, keepdims=True) + jnp.exp(sink_col - m)
    return p, 1.0 / denom


PAIR_COLS = PAIR_SLOTS * WINDOW
ONES_ROWS = 16
PAIRS_PER_STEP = 4


def _swa_prompt_body(sink_ref, q_ref, kp_ref, ko_ref, vp_ref, vo_ref, bias_ref, o_ref):
    first = pl.program_id(2) == 0
    args = (sink_ref, q_ref, kp_ref, ko_ref, vp_ref, vo_ref, bias_ref, o_ref)
    pl.when(first)(lambda: _swa_prompt_block(True, *args))
    pl.when(jnp.logical_not(first))(lambda: _swa_prompt_block(False, *args))


def _swa_prompt_block(mask_prev, sink_ref, q_ref, kp_ref, ko_ref, vp_ref, vo_ref, bias_ref, o_ref):
    head_a = lax.broadcasted_iota(jnp.int32, (WINDOW, LANES), 1) < HEAD_DIM
    for t in range(PAIRS_PER_STEP):
        pair = pl.program_id(0) * PAIRS_PER_STEP + t
        lanes = slice(t * LANES, (t + 1) * LANES)
        q0 = t * Q_COLS_PER_STEP
        k = jnp.concatenate([kp_ref[:, lanes], ko_ref[:, lanes]], axis=0).astype(BF16)
        v = jnp.concatenate([vp_ref[:, lanes], vo_ref[:, lanes]], axis=0)
        vt = jnp.concatenate([v.T, jnp.ones((ONES_ROWS, 2 * WINDOW), F32)], axis=0).astype(BF16)
        parts = []
        for g in range(GROUP):
            qg = q_ref[:, q0 + g * LANES:q0 + (g + 1) * LANES]
            zero = jnp.zeros_like(qg)
            parts += [jnp.where(head_a, qg, zero), jnp.where(head_a, zero, qg)]
        qbd = jnp.concatenate(parts, axis=0)
        st = lax.dot_general(k, qbd, (((1,), (1,)), ((), ())), preferred_element_type=F32)
        st = st + jnp.concatenate([bias_ref[t * PAIR_SLOTS + u] for u in range(PAIR_SLOTS)], axis=1)
        if mask_prev:
            st = jnp.concatenate([st[:WINDOW] + NEG_INF, st[WINDOW:]], axis=0)
        sink_row = jnp.concatenate(
            [jnp.full((1, WINDOW), sink_ref[_slot_head(pair * PAIR_SLOTS + slot)], F32)
             for slot in range(PAIR_SLOTS)], axis=1)
        m = jnp.maximum(jnp.max(st, axis=0, keepdims=True), sink_row)
        pt = jnp.exp(st - m).astype(BF16)
        oa = jnp.dot(vt, pt, preferred_element_type=F32)
        inv = 1.0 / (oa[LANES:LANES + 1] + jnp.exp(sink_row - m))
        o = oa[:LANES] * inv
        for g in range(GROUP):
            c = g * KV_PER_STEP * WINDOW
            ot = jnp.concatenate([o[:HEAD_DIM, c:c + WINDOW], o[HEAD_DIM:, c + WINDOW:c + 2 * WINDOW]], axis=0)
            o_ref[:, q0 + g * LANES:q0 + (g + 1) * LANES] = ot.T.astype(o_ref.dtype)


def _swa_prompt(q, kv, bias_tbl, sinks):
    n = PAIRS_PER_STEP
    v_col0 = KV_COLS // (n * LANES)

    def prev(p, b, i):
        return b * NB + jnp.maximum(i - 1, 0)

    return pl.pallas_call(
        _swa_prompt_body,
        grid=(N_PAIRS // n, BATCH, NB),
        in_specs=[
            pl.BlockSpec(memory_space=pltpu.SMEM),
            pl.BlockSpec((WINDOW, n * Q_COLS_PER_STEP), lambda p, b, i: (b * NB + i, p)),
            pl.BlockSpec((WINDOW, n * LANES), lambda p, b, i: (prev(p, b, i), p)),
            pl.BlockSpec((WINDOW, n * LANES), lambda p, b, i: (b * NB + i, p)),
            pl.BlockSpec((WINDOW, n * LANES), lambda p, b, i: (prev(p, b, i), v_col0 + p)),
            pl.BlockSpec((WINDOW, n * LANES), lambda p, b, i: (b * NB + i, v_col0 + p)),
            pl.BlockSpec((n * PAIR_SLOTS, 2 * WINDOW, WINDOW), lambda p, b, i: (p, 0, 0)),
        ],
        out_specs=pl.BlockSpec((WINDOW, n * Q_COLS_PER_STEP), lambda p, b, i: (b * NB + i, p)),
        out_shape=jax.ShapeDtypeStruct((N_PROMPT, Q_DIM), BF16),
        compiler_params=_cparams(("arbitrary", "arbitrary", "arbitrary")),
        name="swa_prompt",
    )(sinks, q, kv, kv, kv, kv, bias_tbl)


S_ROWS = N_HEADS * DEC_SEQ
SWA_SEQS_PER_STEP = 8


def _swa_sample_body(q_ref, kn_ref, vn_ref, ck_ref, cv_ref, bias_ref, sink_ref,
                     o_ref, ko_ref, vo_ref):
    slot = lax.broadcasted_iota(jnp.int32, (S_ROWS, KV_COLS), 0) % N_HEADS
    row_kv = slot // PAIR_SLOTS * KV_PER_STEP + slot % KV_PER_STEP
    col_kv = lax.broadcasted_iota(jnp.int32, (S_ROWS, KV_COLS), 1) // HEAD_DIM
    own = row_kv == col_kv
    pad = jnp.zeros((WINDOW - SUBLANES, KV_COLS), F32)
    keep = WINDOW - DEC_SEQ
    for i in range(SWA_SEQS_PER_STEP):
        x = q_ref[i].astype(BF16)
        xt = jnp.concatenate([x] * N_KV_HEADS, axis=1)
        qbd = jnp.where(own, xt, jnp.zeros_like(xt))
        kk = jnp.concatenate([ck_ref[i], kn_ref[i], pad], axis=0).astype(BF16)
        vv = jnp.concatenate([cv_ref[i], vn_ref[i], pad], axis=0).astype(BF16)
        s = lax.dot_general(qbd, kk, (((1,), (1,)), ((), ())), preferred_element_type=F32)
        s = s + bias_ref[...]
        p, inv = _softmax_with_sink(s, sink_ref[...])
        of = jnp.dot(p.astype(BF16), vv, preferred_element_type=F32)
        of = jnp.where(own, of, 0.0)
        o = of[:, 0:HEAD_DIM]
        for c in range(1, N_KV_HEADS):
            o = o + of[:, c * HEAD_DIM:(c + 1) * HEAD_DIM]
        o_ref[i] = o * inv
        ko_ref[i, 0:keep, :] = ck_ref[i, DEC_SEQ:WINDOW, :]
        ko_ref[i, keep:WINDOW, :] = kn_ref[i, 0:DEC_SEQ, :]
        vo_ref[i, 0:keep, :] = cv_ref[i, DEC_SEQ:WINDOW, :]
        vo_ref[i, keep:WINDOW, :] = vn_ref[i, 0:DEC_SEQ, :]


def _swa_sample(q_rows, k_new8, v_new8, cache_k, cache_v, bias_s, sink_col):
    n = SWA_SEQS_PER_STEP
    seq3 = lambda s: (s, 0, 0)
    full2 = lambda s: (0, 0)
    return pl.pallas_call(
        _swa_sample_body,
        grid=(DEC_BATCH // n,),
        in_specs=[
            pl.BlockSpec((n, S_ROWS, HEAD_DIM), seq3),
            pl.BlockSpec((n, SUBLANES, KV_COLS), seq3),
            pl.BlockSpec((n, SUBLANES, KV_COLS), seq3),
            pl.BlockSpec((n, WINDOW, KV_COLS), seq3),
            pl.BlockSpec((n, WINDOW, KV_COLS), seq3),
            pl.BlockSpec((S_ROWS, 2 * WINDOW), full2),
            pl.BlockSpec((S_ROWS, 1), full2),
        ],
        out_specs=[
            pl.BlockSpec((n, S_ROWS, HEAD_DIM), seq3),
            pl.BlockSpec((n, WINDOW, KV_COLS), seq3),
            pl.BlockSpec((n, WINDOW, KV_COLS), seq3),
        ],
        out_shape=[
            jax.ShapeDtypeStruct((DEC_BATCH, S_ROWS, HEAD_DIM), F32),
            jax.ShapeDtypeStruct((DEC_BATCH, WINDOW, KV_COLS), F32),
            jax.ShapeDtypeStruct((DEC_BATCH, WINDOW, KV_COLS), F32),
        ],
        compiler_params=_cparams(("arbitrary",)),
        name="swa_sample",
    )(q_rows, k_new8, v_new8, cache_k, cache_v, bias_s, sink_col)


def _gla_out(o, gate, norm_w):
    return _rms(o, norm_w) * _silu(gate)


def _split3(x):
    hi = x.astype(BF16)
    r = x - hi.astype(F32)
    mid = r.astype(BF16)
    lo = (r - mid.astype(F32)).astype(BF16)
    return hi, mid, lo


def _cumsum_rows(g):
    c = g.shape[0]
    tri = (lax.broadcasted_iota(jnp.int32, (c, c), 0) >= lax.broadcasted_iota(jnp.int32, (c, c), 1)).astype(BF16)
    return jnp.dot(jnp.concatenate([tri] * 3, axis=1), jnp.concatenate(_split3(g), axis=0),
                   preferred_element_type=F32)


def _causal(a):
    c = a.shape[0]
    keep = lax.broadcasted_iota(jnp.int32, (c, c), 0) >= lax.broadcasted_iota(jnp.int32, (c, c), 1)
    return jnp.where(keep, a, 0.0)


_NT = (((1,), (1,)), ((), ()))
_TN = (((0,), (0,)), ((), ()))


def _gla_prompt_body(*refs):
    proj_refs = refs[:BATCH]
    la_refs = refs[BATCH:2 * BATCH]
    nw_ref, o_ref, s_ref, st_ref = refs[2 * BATCH:]
    c = pl.program_id(0)

    @pl.when(c == 0)
    def _():
        st_ref[...] = jnp.zeros_like(st_ref)

    for bi in range(BATCH):
        p_ref = proj_refs[bi]
        b_all = _cumsum_rows(la_refs[bi][...])
        for h in range(GLA_HEADS):
            kc = slice(h * GLA_DK, (h + 1) * GLA_DK)
            v0 = 2 * GLA_KEY_DIM + h * GLA_DV
            b = b_all[:, kc]
            q = p_ref[:, kc].astype(F32) * GLA_DK ** -0.5
            k = p_ref[:, GLA_KEY_DIM + h * GLA_DK:GLA_KEY_DIM + (h + 1) * GLA_DK].astype(F32)
            v = p_ref[:, v0:v0 + GLA_DV]
            gate = p_ref[:, v0 + GLA_VAL_DIM:v0 + GLA_VAL_DIM + GLA_DV].astype(F32)
            qe = (q * jnp.exp(b)).astype(BF16)
            r = b[GLA_C // 2 - 1:GLA_C // 2, :]
            qr = (q * jnp.exp(b - r)).astype(BF16)
            kr = (k * jnp.exp(r - b)).astype(BF16)
            a = _causal(lax.dot_general(qr, kr, _NT, preferred_element_type=F32))
            st = st_ref[bi, h]
            o = (jnp.dot(a.astype(BF16), v, preferred_element_type=F32)
                 + lax.dot_general(qe, st.astype(BF16), _NT, preferred_element_type=F32))
            b_last = b[GLA_C - 1:GLA_C, :]
            kd = (k * jnp.exp(b_last - b)).astype(BF16)
            st_new = st * jnp.exp(b_last) + lax.dot_general(v, kd, _TN, preferred_element_type=F32)
            st_ref[bi, h] = st_new
            o_ref[bi, :, h * GLA_DV:(h + 1) * GLA_DV] = _gla_out(o, gate, nw_ref[...]).astype(o_ref.dtype)

    @pl.when(c == pl.num_programs(0) - 1)
    def _():
        for bi in range(BATCH):
            for h in range(GLA_HEADS):
                s_ref[bi, h] = st_ref[bi, h].T


def _gla_prompt(proj, log_a, norm_w):
    nc = SEQ // GLA_C
    rows = [functools.partial(lambda bi, c: (bi * nc + c, 0), bi) for bi in range(BATCH)]
    o, s = pl.pallas_call(
        _gla_prompt_body,
        grid=(nc,),
        in_specs=([pl.BlockSpec((GLA_C, GLA_MAIN_DIM), r) for r in rows]
                  + [pl.BlockSpec((GLA_C, GLA_KEY_DIM), r) for r in rows]
                  + [pl.BlockSpec((1, GLA_DV), lambda c: (0, 0))]),
        out_specs=[
            pl.BlockSpec((BATCH, GLA_C, GLA_VAL_DIM), lambda c: (0, c, 0)),
            pl.BlockSpec((BATCH, GLA_HEADS, GLA_DK, GLA_DV), lambda c: (0, 0, 0, 0)),
        ],
        out_shape=[
            jax.ShapeDtypeStruct((BATCH, SEQ, GLA_VAL_DIM), BF16),
            jax.ShapeDtypeStruct((BATCH, GLA_HEADS, GLA_DK, GLA_DV), F32),
        ],
        scratch_shapes=[pltpu.VMEM((BATCH, GLA_HEADS, GLA_DV, GLA_DK), F32)],
        compiler_params=_cparams(("arbitrary",)),
        name="gla_prompt",
    )(*([proj] * BATCH + [log_a] * BATCH + [norm_w.reshape(1, GLA_DV)]))
    return o.reshape(N_PROMPT, GLA_VAL_DIM), s


GLA_SEQS_PER_STEP = 4


def _gla_sample_body(proj_ref, la_ref, s0_ref, nw_ref, o_ref, s_ref):
    ones = jnp.ones((DEC_SEQ, LANES), BF16)
    proj = proj_ref[...].astype(F32)
    la = la_ref[...]
    for i in range(GLA_SEQS_PER_STEP):
        rows_i = slice(i * DEC_SEQ, (i + 1) * DEC_SEQ)
        for h in range(GLA_HEADS):
            q = proj[rows_i, h * GLA_DK:(h + 1) * GLA_DK]
            k = proj[rows_i, GLA_KEY_DIM + h * GLA_DK:GLA_KEY_DIM + (h + 1) * GLA_DK]
            v0 = 2 * GLA_KEY_DIM + h * GLA_DV
            v = proj[rows_i, v0:v0 + GLA_DV].astype(BF16)
            gate = proj[rows_i, v0 + GLA_VAL_DIM:v0 + GLA_VAL_DIM + GLA_DV]
            g = la[rows_i, h * GLA_DK:(h + 1) * GLA_DK]
            rows = [g[0:1]]
            for t in range(1, DEC_SEQ):
                rows.append(rows[-1] + g[t:t + 1])
            b = jnp.concatenate(rows, axis=0)
            b_last = rows[-1]
            qe = (q * GLA_DK ** -0.5 * jnp.exp(b)).astype(BF16)
            ke = (k * jnp.exp(-b)).astype(BF16)
            a = _causal(lax.dot_general(qe, ke, _NT, preferred_element_type=F32))
            s0 = s0_ref[i, h]
            o = (jnp.dot(a.astype(BF16), v, preferred_element_type=F32)
                 + jnp.dot(qe, s0.astype(BF16), preferred_element_type=F32))
            kd = (k * jnp.exp(b_last - b)).astype(BF16)
            dsum = sum(lax.dot_general(piece, ones, _TN, preferred_element_type=F32) for piece in _split3(g))
            decay = jnp.concatenate([jnp.exp(dsum)] * (GLA_DV // LANES), axis=1)
            s_ref[i, h] = s0 * decay + lax.dot_general(kd, v, _TN, preferred_element_type=F32)
            o_ref[rows_i, h * GLA_DV:(h + 1) * GLA_DV] = _gla_out(o, gate, nw_ref[...])


def _gla_sample(proj, log_a, state, norm_w):
    n = GLA_SEQS_PER_STEP
    rows = n * DEC_SEQ
    first = N_PROMPT // rows
    seq4 = lambda s: (s, 0, 0, 0)
    return pl.pallas_call(
        _gla_sample_body,
        grid=(DEC_BATCH // n,),
        in_specs=[
            pl.BlockSpec((rows, GLA_MAIN_DIM), lambda s: (first + s, 0)),
            pl.BlockSpec((rows, GLA_KEY_DIM), lambda s: (first + s, 0)),
            pl.BlockSpec((n, GLA_HEADS, GLA_DK, GLA_DV), seq4),
            pl.BlockSpec((1, GLA_DV), lambda s: (0, 0)),
        ],
        out_specs=[
            pl.BlockSpec((rows, GLA_VAL_DIM), lambda s: (s, 0)),
            pl.BlockSpec((n, GLA_HEADS, GLA_DK, GLA_DV), seq4),
        ],
        out_shape=[
            jax.ShapeDtypeStruct((N_SAMPLE, GLA_VAL_DIM), F32),
            jax.ShapeDtypeStruct((DEC_BATCH, GLA_HEADS, GLA_DK, GLA_DV), F32),
        ],
        compiler_params=_cparams(("arbitrary",)),
        name="gla_sample",
    )(proj, log_a, state, norm_w.reshape(1, GLA_DV))


CAST_ROWS = 512


def _cast_body(w_ref, o_ref):
    o_ref[...] = w_ref[0].astype(BF16)


def _layer_bf16(w, layer):
    _, rows, cols = w.shape
    return pl.pallas_call(
        _cast_body,
        grid=(rows // CAST_ROWS,),
        in_specs=[pl.BlockSpec((1, CAST_ROWS, cols), lambda i: (layer, i, 0))],
        out_specs=pl.BlockSpec((CAST_ROWS, cols), lambda i: (i, 0)),
        out_shape=jax.ShapeDtypeStruct((rows, cols), BF16),
        compiler_params=_cparams(("arbitrary",)),
        name="cast_bf16",
    )(w)


def _slot_rows_body(*refs):
    o_ref = refs[-1]
    for u, w_ref in enumerate(refs[:-1]):
        o_ref[u * HEAD_DIM:(u + 1) * HEAD_DIM] = w_ref[0].astype(BF16)


def _slot_rows_bf16(w, layer):
    def src(u):
        g, hh = u // KV_PER_STEP, u % KV_PER_STEP
        return lambda pair: (layer, (pair * KV_PER_STEP + hh) * GROUP + g, 0)

    cols = w.shape[2]
    return pl.pallas_call(
        _slot_rows_body,
        grid=(N_PAIRS,),
        in_specs=[pl.BlockSpec((1, HEAD_DIM, cols), src(u)) for u in range(PAIR_SLOTS)],
        out_specs=pl.BlockSpec((PAIR_SLOTS * HEAD_DIM, cols), lambda pair: (pair, 0)),
        out_shape=jax.ShapeDtypeStruct((Q_DIM, cols), BF16),
        compiler_params=_cparams(("arbitrary",)),
        name="slot_rows_bf16",
    )(*([w] * PAIR_SLOTS))


def _swa_layer(x, cache_k, cache_v, norm_w, w_qkv, b_qkv, j, w_o_all, b_o, sinks, rel_bias):
    slots = (N_PAIRS, KV_PER_STEP, GROUP, HEAD_DIM)
    q_scale = HEAD_DIM ** -0.5
    assert math.frexp(q_scale)[0] == 0.5, q_scale
    w_q = (w_qkv[:, :Q_DIM] * q_scale).reshape((D_MODEL,) + slots).transpose(0, 1, 3, 2, 4).reshape(D_MODEL, Q_DIM)
    b_q = (b_qkv[:Q_DIM] * q_scale).reshape(slots).transpose(0, 2, 1, 3).reshape(Q_DIM)
    w_qkv_s = jnp.concatenate([w_q, w_qkv[:, Q_DIM:]], axis=1).astype(BF16)
    b_qkv_s = jnp.concatenate([b_q, b_qkv[Q_DIM:]])
    w_o_s = _slot_rows_bf16(w_o_all, j)

    q, kv = _norm_proj(x, norm_w, w_qkv_s, b_qkv_s, ((Q_DIM, BF16), (2 * KV_COLS, F32)), "swa_qkv")
    bias_tbl = _bias_table(rel_bias)
    o_p = _swa_prompt(q, kv, bias_tbl, sinks)

    q_rows = q[N_PROMPT:].astype(F32).reshape(DEC_BATCH, S_ROWS, HEAD_DIM)
    kv_s = kv[N_PROMPT:].reshape(DEC_BATCH, DEC_SEQ, 2 * KV_COLS)
    pad8 = ((0, 0), (0, SUBLANES - DEC_SEQ), (0, 0))
    k_new8 = jnp.pad(kv_s[..., :KV_COLS], pad8)
    v_new8 = jnp.pad(kv_s[..., KV_COLS:], pad8)
    bias_s = bias_tbl[:, :, :DEC_SEQ].transpose(2, 0, 1).reshape(S_ROWS, 2 * WINDOW)
    slot_heads = np.array([_slot_head(s) for s in range(N_HEADS)])
    sink_col = jnp.tile(sinks[slot_heads], DEC_SEQ).reshape(S_ROWS, 1)
    o_s, k_s, v_s = _swa_sample(q_rows, k_new8, v_new8,
                                cache_k.reshape(DEC_BATCH, WINDOW, KV_COLS),
                                cache_v.reshape(DEC_BATCH, WINDOW, KV_COLS), bias_s, sink_col)
    o_s = o_s.reshape(N_SAMPLE, Q_DIM)
    x = _proj_res((o_p, o_s), w_o_s, b_o, x, "swa_out")

    kv_p = jnp.stack([kv[(b + 1) * SEQ - WINDOW:(b + 1) * SEQ] for b in range(BATCH)])
    k_p = kv_p[..., :KV_COLS].reshape(BATCH, WINDOW, N_KV_HEADS, HEAD_DIM)
    v_p = kv_p[..., KV_COLS:].reshape(BATCH, WINDOW, N_KV_HEADS, HEAD_DIM)
    shape_s = (DEC_BATCH, WINDOW, N_KV_HEADS, HEAD_DIM)
    return x, k_p, v_p, k_s.reshape(shape_s), v_s.reshape(shape_s)


def _gla_layer(x, state, norm_w, j, w_in_all, w_gk2, b_gk, gnorm, w_o_all):
    proj, log_a = _norm_proj(x, norm_w, w_in_all[j].astype(BF16), jnp.zeros((GLA_MAIN_DIM,), F32),
                             ((GLA_MAIN_DIM, BF16),), "gla_in", gate=(w_gk2.astype(BF16), b_gk))
    o_p, s_p = _gla_prompt(proj, log_a, gnorm)
    o_s, s_s = _gla_sample(proj, log_a, state, gnorm)
    x = _proj_res((o_p, o_s), _layer_bf16(w_o_all, j), jnp.zeros((D_MODEL,), F32), x, "gla_out")
    return x, s_p, s_s


def kernel(x_prompt, x_sample, cache_swa_k, cache_swa_v, state_gla, norm_ffn1, ffn1_w_gate, ffn1_w_up,
           ffn1_w_down, norm_mix, norm_ffn2, ffn2_w_gate, ffn2_w_up, ffn2_w_down, norm_final, rel_bias,
           swa_w_qkv, swa_b_qkv, swa_w_o, swa_b_o, swa_sinks, gla_w_in, gla_w_gk2, gla_b_gk, gla_norm,
           gla_w_o):
    x = (x_prompt.reshape(N_PROMPT, D_MODEL), x_sample.reshape(N_SAMPLE, D_MODEL))
    swa_kp, swa_vp, swa_ks, swa_vs, gla_sp, gla_ss = [], [], [], [], [], []
    for i in range(DEPTH):
        x = _ffn_pair(x, i, norm_ffn1[i], ffn1_w_gate, ffn1_w_up, ffn1_w_down)
        j = i // 2
        if i % 2 == 0:
            x, kp, vp, ks, vs = _swa_layer(x, cache_swa_k[j], cache_swa_v[j], norm_mix[i], swa_w_qkv[j],
                                           swa_b_qkv[j], j, swa_w_o, swa_b_o[j], swa_sinks[j], rel_bias)
            swa_kp.append(kp)
            swa_vp.append(vp)
            swa_ks.append(ks)
            swa_vs.append(vs)
        else:
            x, sp, ss = _gla_layer(x, state_gla[j], norm_mix[i], j, gla_w_in, gla_w_gk2[j], gla_b_gk[j],
                                   gla_norm[j], gla_w_o)
            gla_sp.append(sp)
            gla_ss.append(ss)
        final_w = norm_final if i == DEPTH - 1 else None
        x = _ffn_pair(x, i, norm_ffn2[i], ffn2_w_gate, ffn2_w_up, ffn2_w_down, final_w)
    y_prompt = x[0].reshape(BATCH, SEQ, D_MODEL)
    y_sample = x[1].reshape(DEC_BATCH, DEC_SEQ, D_MODEL)
    return (y_prompt, y_sample, jnp.stack(swa_kp), jnp.stack(swa_vp), jnp.stack(swa_ks), jnp.stack(swa_vs),
            jnp.stack(gla_sp), jnp.stack(gla_ss))
```

```python
import functools
import math

import numpy as np
import jax
import jax.numpy as jnp
from jax import lax
from jax.experimental import pallas as pl
from jax.experimental.pallas import tpu as pltpu

F32 = jnp.float32
BF16 = jnp.bfloat16

D_MODEL = 2048
BATCH = 2
SEQ = 4096
DEPTH = 2
DEC_BATCH = 128
DEC_SEQ = 4
RMS_EPS = 1e-6
D_FF = 5632
N_HEADS = 32
N_KV_HEADS = 8
HEAD_DIM = 64
GROUP = N_HEADS // N_KV_HEADS
WINDOW = 128
NUM_BUCKETS = 32
MAX_DISTANCE = 128
NEG_INF = -1e30
GLA_HEADS = 4
GLA_DK = 256
GLA_DV = 512
GLA_KEY_DIM = GLA_HEADS * GLA_DK
GLA_VAL_DIM = GLA_HEADS * GLA_DV
GATE_RANK = 16
GATE_NORMALIZER = 16.0
GLA_MAIN_DIM = 2 * GLA_KEY_DIM + 2 * GLA_VAL_DIM
Q_DIM = N_HEADS * HEAD_DIM
KV_COLS = N_KV_HEADS * HEAD_DIM

N_PROMPT = BATCH * SEQ
N_SAMPLE = DEC_BATCH * DEC_SEQ
N_TOK = N_PROMPT + N_SAMPLE

LANES = 128
SUBLANES = 8
VMEM_LIMIT = 56 * 1024 * 1024
BIG_VMEM_LIMIT = 60 * 1024 * 1024

TM = 512
TN = 512
GLA_C = 64
NB = SEQ // WINDOW
N_PROMPT_TILES = N_PROMPT // TM


def _rms(x, w):
    return x * lax.rsqrt(jnp.mean(x * x, axis=-1, keepdims=True) + RMS_EPS) * w


def _silu(x):
    return x * jax.nn.sigmoid(x)


def _cparams(sem, vmem_limit=VMEM_LIMIT):
    return pltpu.CompilerParams(dimension_semantics=sem, vmem_limit_bytes=vmem_limit)


def _row_pair_specs(width):
    return [pl.BlockSpec((TM, width), lambda i: (jnp.minimum(i, N_PROMPT_TILES - 1), 0)),
            pl.BlockSpec((TM, width), lambda i: (jnp.maximum(i - N_PROMPT_TILES, 0), 0))]


def _on_row_source(fn, *ref_pairs):
    i = pl.program_id(0)
    pl.when(i < N_PROMPT_TILES)(lambda: fn(*[p[0] for p in ref_pairs]))
    pl.when(i >= N_PROMPT_TILES)(lambda: fn(*[p[1] for p in ref_pairs]))


FFN_TM = 1024
FFN_TF_HEAD = 256
FFN_TF = 512
FFN_VMEM_LIMIT = 62 * 1024 * 1024


def _ffn_step(first, x_ref, nw_ref, h_ref, o_ref, wg, wu, wd):
    if first:
        h_ref[...] = _rms(x_ref[...], nw_ref[...]).astype(BF16)
    h = h_ref[...]
    g = jnp.dot(h, wg, preferred_element_type=F32)
    u = jnp.dot(h, wu, preferred_element_type=F32)
    a = (_silu(g) * (0.5 * u)).astype(BF16)
    d = jnp.dot(a, wd, preferred_element_type=F32)
    o_ref[...] = (x_ref[...] if first else o_ref[...]) + d


def _ffn_body(final_norm, x_ref, nw_ref, wg_ref, wu_ref, wd_ref, *rest):
    rest = list(rest)
    fw_ref = rest.pop(0) if final_norm else None
    yh_ref, o_ref, h_ref = rest
    i = pl.program_id(0)
    j = pl.program_id(1)

    def step(first):
        _ffn_step(first, x_ref, nw_ref, h_ref, o_ref, wg_ref[...], wu_ref[...], wd_ref[...])

    @pl.when((i == 0) & (j == 0))
    def _():
        o_ref[...] = yh_ref[...]

    pl.when((i > 0) & (j == 0))(lambda: step(True))
    pl.when((i > 0) & (j > 0))(lambda: step(False))

    if final_norm:
        @pl.when((i > 0) & (j == pl.num_programs(1) - 1))
        def _():
            o_ref[...] = _rms(o_ref[...], fw_ref[...])


def _ffn(x, nw, wg, wu, wd, final_w, y_head):
    m = x.shape[0]
    final_norm = final_w is not None
    vec = pl.BlockSpec((1, D_MODEL), lambda i, j: (0, 0))
    rows = pl.BlockSpec((FFN_TM, D_MODEL), lambda i, j: (i, 0))
    tile = lambda i, j: jnp.where(i == 0, 0, j)
    in_specs = [
        rows, vec,
        pl.BlockSpec((D_MODEL, FFN_TF), lambda i, j: (0, tile(i, j))),
        pl.BlockSpec((D_MODEL, FFN_TF), lambda i, j: (0, tile(i, j))),
        pl.BlockSpec((FFN_TF, D_MODEL), lambda i, j: (tile(i, j), 0)),
    ]
    args = [x, nw.reshape(1, D_MODEL), wg, wu, wd]
    if final_norm:
        in_specs.append(vec)
        args.append(final_w.reshape(1, D_MODEL))
    in_specs.append(pl.BlockSpec((FFN_TM, D_MODEL), lambda i, j: (0, 0), pipeline_mode=pl.Buffered(1)))
    args.append(y_head)
    return pl.pallas_call(
        functools.partial(_ffn_body, final_norm),
        grid=(m // FFN_TM, D_FF // FFN_TF),
        in_specs=in_specs,
        out_specs=rows,
        out_shape=jax.ShapeDtypeStruct((m, D_MODEL), F32),
        scratch_shapes=[pltpu.VMEM((FFN_TM, D_MODEL), BF16)],
        compiler_params=_cparams(("arbitrary", "arbitrary"), FFN_VMEM_LIMIT),
        name="ffn",
    )(*args)


def _ffn_head_body(final_norm, xp_ref, xs_ref, nw_ref, wg_ref, wu_ref, wd_ref, *rest):
    rest = list(rest)
    fw_ref = rest.pop(0) if final_norm else None
    op_ref, os_ref, wgo_ref, wuo_ref, wdo_ref, hp_ref, hs_ref = rest
    groups = ((xp_ref, op_ref, hp_ref), (xs_ref, os_ref, hs_ref))
    j = pl.program_id(0)

    def step(first):
        wgo_ref[...] = wg_ref[0].astype(BF16)
        wuo_ref[...] = wu_ref[0].astype(BF16)
        wdo_ref[...] = wd_ref[0].astype(BF16)
        for x_ref, o_ref, h_ref in groups:
            _ffn_step(first, x_ref, nw_ref, h_ref, o_ref, wgo_ref[...], wuo_ref[...], wdo_ref[...])

    pl.when(j == 0)(lambda: step(True))
    pl.when(j > 0)(lambda: step(False))

    if final_norm:
        @pl.when(j == pl.num_programs(0) - 1)
        def _():
            for _, o_ref, _ in groups:
                o_ref[...] = _rms(o_ref[...], fw_ref[...])


def _ffn_head(xp, xs, layer, nw, wg, wu, wd, final_w=None):
    tf = FFN_TF_HEAD
    final_norm = final_w is not None
    once = pl.Buffered(1)
    vec = pl.BlockSpec((1, D_MODEL), lambda j: (0, 0))
    in_specs = [
        pl.BlockSpec((FFN_TM, D_MODEL), lambda j: (0, 0), pipeline_mode=once),
        pl.BlockSpec((N_SAMPLE, D_MODEL), lambda j: (0, 0), pipeline_mode=once),
        vec,
        pl.BlockSpec((1, D_MODEL, tf), lambda j: (layer, 0, j)),
        pl.BlockSpec((1, D_MODEL, tf), lambda j: (layer, 0, j)),
        pl.BlockSpec((1, tf, D_MODEL), lambda j: (layer, j, 0)),
    ]
    args = [xp, xs, nw.reshape(1, D_MODEL), wg, wu, wd]
    if final_norm:
        in_specs.append(vec)
        args.append(final_w.reshape(1, D_MODEL))
    outs = pl.pallas_call(
        functools.partial(_ffn_head_body, final_norm),
        grid=(D_FF // tf,),
        in_specs=in_specs,
        out_specs=[
            pl.BlockSpec((FFN_TM, D_MODEL), lambda j: (0, 0), pipeline_mode=once),
            pl.BlockSpec((N_SAMPLE, D_MODEL), lambda j: (0, 0), pipeline_mode=once),
            pl.BlockSpec((D_MODEL, tf), lambda j: (0, j)),
            pl.BlockSpec((D_MODEL, tf), lambda j: (0, j)),
            pl.BlockSpec((tf, D_MODEL), lambda j: (j, 0)),
        ],
        out_shape=[
            jax.ShapeDtypeStruct((FFN_TM, D_MODEL), F32),
            jax.ShapeDtypeStruct((N_SAMPLE, D_MODEL), F32),
            jax.ShapeDtypeStruct((D_MODEL, D_FF), BF16),
            jax.ShapeDtypeStruct((D_MODEL, D_FF), BF16),
            jax.ShapeDtypeStruct((D_FF, D_MODEL), BF16),
        ],
        scratch_shapes=[pltpu.VMEM((FFN_TM, D_MODEL), BF16), pltpu.VMEM((N_SAMPLE, D_MODEL), BF16)],
        compiler_params=_cparams(("arbitrary",), BIG_VMEM_LIMIT),
        name="ffn_head",
    )(*args)
    return outs[0], outs[1], tuple(outs[2:])


def _ffn_pair(x_pair, layer, nw, wg, wu, wd, final_w=None):
    xp, xs = x_pair
    y_prompt, y_sample, w_bf16 = _ffn_head(xp, xs, layer, nw, wg, wu, wd, final_w)
    return _ffn(xp, nw, *w_bf16, final_w, y_prompt), y_sample


def _log_sigmoid(x):
    return jnp.minimum(x, 0.0) - jnp.log(1.0 + jnp.exp(-jnp.abs(x)))


def _norm_proj_body(segments, with_gate, xp_ref, xs_ref, nw_ref, w_ref, b_ref, *rest):
    if with_gate:
        w2_ref, bg_ref = rest[:2]
        rest = rest[2:]
    o_refs = rest[:len(segments)]

    def run(x_ref):
        h = _rms(x_ref[...], nw_ref[...]).astype(BF16)
        col = 0
        for (width, _), o_ref in zip(segments, o_refs):
            for c in range(0, width, TN):
                acc = jnp.dot(h, w_ref[:, col + c:col + c + TN], preferred_element_type=F32)
                o_ref[:, c:c + TN] = (acc + b_ref[:, col + c:col + c + TN]).astype(o_ref.dtype)
            col += width
        if with_gate:
            gk = jnp.dot(h, w_ref[:, col:col + GATE_RANK], preferred_element_type=F32)
            z = jnp.dot(gk.astype(BF16), w2_ref[...], preferred_element_type=F32) + bg_ref[...]
            rest[-1][...] = _log_sigmoid(z) / GATE_NORMALIZER
    _on_row_source(run, (xp_ref, xs_ref))


def _norm_proj(xs, nw, w, b, segments, name, gate=None):
    n = sum(width for width, _ in segments)
    in_specs = _row_pair_specs(D_MODEL) + [
        pl.BlockSpec((1, D_MODEL), lambda i: (0, 0)),
        pl.BlockSpec(w.shape, lambda i: (0, 0), pipeline_mode=pl.Buffered(1)),
        pl.BlockSpec((1, n), lambda i: (0, 0)),
    ]
    args = [*xs, nw.reshape(1, D_MODEL), w, b.reshape(1, n)]
    out_specs = [pl.BlockSpec((TM, width), lambda i: (i, 0)) for width, _ in segments]
    out_shape = [jax.ShapeDtypeStruct((N_TOK, width), dtype) for width, dtype in segments]
    if gate is not None:
        w2, bg = gate
        assert w.shape[1] == n + GATE_RANK, w.shape
        in_specs += [
            pl.BlockSpec((GATE_RANK, GLA_KEY_DIM), lambda i: (0, 0)),
            pl.BlockSpec((1, GLA_KEY_DIM), lambda i: (0, 0)),
        ]
        args += [w2, bg.reshape(1, GLA_KEY_DIM)]
        out_specs.append(pl.BlockSpec((TM, GLA_KEY_DIM), lambda i: (i, 0)))
        out_shape.append(jax.ShapeDtypeStruct((N_TOK, GLA_KEY_DIM), F32))
    return pl.pallas_call(
        functools.partial(_norm_proj_body, segments, gate is not None),
        grid=(N_TOK // TM,),
        in_specs=in_specs,
        out_specs=out_specs,
        out_shape=out_shape,
        compiler_params=_cparams(("arbitrary",), BIG_VMEM_LIMIT),
        name=name,
    )(*args)


def _proj_res_body(ap_ref, as_ref, w_ref, b_ref, rp_ref, rs_ref, op_ref, os_ref):
    def run(a_ref, r_ref, o_ref):
        a = a_ref[...].astype(BF16)
        for c in range(0, D_MODEL, TN):
            acc = jnp.dot(a, w_ref[:, c:c + TN], preferred_element_type=F32)
            o_ref[:, c:c + TN] = r_ref[:, c:c + TN] + acc + b_ref[:, c:c + TN]
    _on_row_source(run, (ap_ref, as_ref), (rp_ref, rs_ref), (op_ref, os_ref))


def _proj_res(a_pair, w, b, res_pair, name):
    k = w.shape[0]
    return pl.pallas_call(
        _proj_res_body,
        grid=(N_TOK // TM,),
        in_specs=_row_pair_specs(k) + [
            pl.BlockSpec((k, D_MODEL), lambda i: (0, 0), pipeline_mode=pl.Buffered(1)),
            pl.BlockSpec((1, D_MODEL), lambda i: (0, 0)),
        ] + _row_pair_specs(D_MODEL),
        out_specs=_row_pair_specs(D_MODEL),
        out_shape=[jax.ShapeDtypeStruct((N_PROMPT, D_MODEL), F32), jax.ShapeDtypeStruct((N_SAMPLE, D_MODEL), F32)],
        compiler_params=_cparams(("arbitrary",)),
        name=name,
    )(*a_pair, w, b.reshape(1, D_MODEL), *res_pair)


def _t5_bucket_table():
    i = np.arange(WINDOW)[None, :]
    j = np.arange(2 * WINDOW)[:, None]
    n = np.maximum(WINDOW + i - j, 0)
    max_exact = NUM_BUCKETS // 2
    nf = np.maximum(n, 1).astype(np.float32)
    large = max_exact + (np.log(nf / np.float32(max_exact)) / np.float32(math.log(MAX_DISTANCE / max_exact))
                         * np.float32(NUM_BUCKETS - max_exact)).astype(np.int32)
    large = np.minimum(large, NUM_BUCKETS - 1)
    return np.where(n < max_exact, n, large).astype(np.int32)


KV_PER_STEP = LANES // HEAD_DIM
N_PAIRS = N_KV_HEADS // KV_PER_STEP
Q_COLS_PER_STEP = KV_PER_STEP * GROUP * HEAD_DIM
PAIR_SLOTS = GROUP * KV_PER_STEP


def _slot_head(slot):
    pair = slot // PAIR_SLOTS
    g = (slot // KV_PER_STEP) % GROUP
    hh = slot % KV_PER_STEP
    return (pair * KV_PER_STEP + hh) * GROUP + g


BIAS_SLOTS_PER_STEP = 8


def _bias_table_body(bucket_ref, rb_ref, o_ref):
    bucket = bucket_ref[...]
    j = lax.broadcasted_iota(jnp.int32, (2 * WINDOW, WINDOW), 0)
    i = lax.broadcasted_iota(jnp.int32, (2 * WINDOW, WINDOW), 1)
    dist = WINDOW + i - j
    in_window = (dist >= 0) & (dist < WINDOW)
    for t in range(BIAS_SLOTS_PER_STEP):
        h = _slot_head(pl.program_id(0) * BIAS_SLOTS_PER_STEP + t)
        acc = jnp.zeros((2 * WINDOW, WINDOW), F32)
        for b in range(NUM_BUCKETS):
            acc = jnp.where(bucket == b, rb_ref[b, h], acc)
        o_ref[t] = jnp.where(in_window, acc, NEG_INF)


def _bias_table(rel_bias):
    n = BIAS_SLOTS_PER_STEP
    return pl.pallas_call(
        _bias_table_body,
        grid=(N_HEADS // n,),
        in_specs=[
            pl.BlockSpec((2 * WINDOW, WINDOW), lambda h: (0, 0)),
            pl.BlockSpec(memory_space=pltpu.SMEM),
        ],
        out_specs=pl.BlockSpec((n, 2 * WINDOW, WINDOW), lambda h: (h, 0, 0)),
        out_shape=jax.ShapeDtypeStruct((N_HEADS, 2 * WINDOW, WINDOW), F32),
        name="bias_table",
    )(jnp.asarray(_t5_bucket_table()), rel_bias)


def _softmax_with_sink(s, sink_col):
    m = jnp.maximum(jnp.max(s, axis=-1, keepdims=True), sink_col)
    p = jnp.exp(s - m)
    denom = jnp.sum(p, axis=-1, keepdims=True) + jnp.exp(sink_col - m)
    return p, 1.0 / denom


PAIR_COLS = PAIR_SLOTS * WINDOW
ONES_ROWS = 16
PAIRS_PER_STEP = 4


def _swa_prompt_body(sink_ref, q_ref, kp_ref, ko_ref, vp_ref, vo_ref, bias_ref, o_ref):
    first = pl.program_id(2) == 0
    args = (sink_ref, q_ref, kp_ref, ko_ref, vp_ref, vo_ref, bias_ref, o_ref)
    pl.when(first)(lambda: _swa_prompt_block(True, *args))
    pl.when(jnp.logical_not(first))(lambda: _swa_prompt_block(False, *args))


def _swa_prompt_block(mask_prev, sink_ref, q_ref, kp_ref, ko_ref, vp_ref, vo_ref, bias_ref, o_ref):
    head_a = lax.broadcasted_iota(jnp.int32, (WINDOW, LANES), 1) < HEAD_DIM
    for t in range(PAIRS_PER_STEP):
        pair = pl.program_id(0) * PAIRS_PER_STEP + t
        lanes = slice(t * LANES, (t + 1) * LANES)
        q0 = t * Q_COLS_PER_STEP
        k = jnp.concatenate([kp_ref[:, lanes], ko_ref[:, lanes]], axis=0).astype(BF16)
        v = jnp.concatenate([vp_ref[:, lanes], vo_ref[:, lanes]], axis=0)
        vt = jnp.concatenate([v.T, jnp.ones((ONES_ROWS, 2 * WINDOW), F32)], axis=0).astype(BF16)
        parts = []
        for g in range(GROUP):
            qg = q_ref[:, q0 + g * LANES:q0 + (g + 1) * LANES]
            zero = jnp.zeros_like(qg)
            parts += [jnp.where(head_a, qg, zero), jnp.where(head_a, zero, qg)]
        qbd = jnp.concatenate(parts, axis=0)
        st = lax.dot_general(k, qbd, (((1,), (1,)), ((), ())), preferred_element_type=F32)
        st = st + jnp.concatenate([bias_ref[t * PAIR_SLOTS + u] for u in range(PAIR_SLOTS)], axis=1)
        if mask_prev:
            st = jnp.concatenate([st[:WINDOW] + NEG_INF, st[WINDOW:]], axis=0)
        sink_row = jnp.concatenate(
            [jnp.full((1, WINDOW), sink_ref[_slot_head(pair * PAIR_SLOTS + slot)], F32)
             for slot in range(PAIR_SLOTS)], axis=1)
        m = jnp.maximum(jnp.max(st, axis=0, keepdims=True), sink_row)
        pt = jnp.exp(st - m).astype(BF16)
        oa = jnp.dot(vt, pt, preferred_element_type=F32)
        inv = 1.0 / (oa[LANES:LANES + 1] + jnp.exp(sink_row - m))
        o = oa[:LANES] * inv
        for g in range(GROUP):
            c = g * KV_PER_STEP * WINDOW
            ot = jnp.concatenate([o[:HEAD_DIM, c:c + WINDOW], o[HEAD_DIM:, c + WINDOW:c + 2 * WINDOW]], axis=0)
            o_ref[:, q0 + g * LANES:q0 + (g + 1) * LANES] = ot.T.astype(o_ref.dtype)


def _swa_prompt(q, kv, bias_tbl, sinks):
    n = PAIRS_PER_STEP
    v_col0 = KV_COLS // (n * LANES)

    def prev(p, b, i):
        return b * NB + jnp.maximum(i - 1, 0)

    return pl.pallas_call(
        _swa_prompt_body,
        grid=(N_PAIRS // n, BATCH, NB),
        in_specs=[
            pl.BlockSpec(memory_space=pltpu.SMEM),
            pl.BlockSpec((WINDOW, n * Q_COLS_PER_STEP), lambda p, b, i: (b * NB + i, p)),
            pl.BlockSpec((WINDOW, n * LANES), lambda p, b, i: (prev(p, b, i), p)),
            pl.BlockSpec((WINDOW, n * LANES), lambda p, b, i: (b * NB + i, p)),
            pl.BlockSpec((WINDOW, n * LANES), lambda p, b, i: (prev(p, b, i), v_col0 + p)),
            pl.BlockSpec((WINDOW, n * LANES), lambda p, b, i: (b * NB + i, v_col0 + p)),
            pl.BlockSpec((n * PAIR_SLOTS, 2 * WINDOW, WINDOW), lambda p, b, i: (p, 0, 0)),
        ],
        out_specs=pl.BlockSpec((WINDOW, n * Q_COLS_PER_STEP), lambda p, b, i: (b * NB + i, p)),
        out_shape=jax.ShapeDtypeStruct((N_PROMPT, Q_DIM), BF16),
        compiler_params=_cparams(("arbitrary", "arbitrary", "arbitrary")),
        name="swa_prompt",
    )(sinks, q, kv, kv, kv, kv, bias_tbl)


S_ROWS = N_HEADS * DEC_SEQ
SWA_SEQS_PER_STEP = 8


def _swa_sample_body(q_ref, kn_ref, vn_ref, ck_ref, cv_ref, bias_ref, sink_ref,
                     o_ref, ko_ref, vo_ref):
    slot = lax.broadcasted_iota(jnp.int32, (S_ROWS, KV_COLS), 0) % N_HEADS
    row_kv = slot // PAIR_SLOTS * KV_PER_STEP + slot % KV_PER_STEP
    col_kv = lax.broadcasted_iota(jnp.int32, (S_ROWS, KV_COLS), 1) // HEAD_DIM
    own = row_kv == col_kv
    pad = jnp.zeros((WINDOW - SUBLANES, KV_COLS), F32)
    keep = WINDOW - DEC_SEQ
    for i in range(SWA_SEQS_PER_STEP):
        x = q_ref[i].astype(BF16)
        xt = jnp.concatenate([x] * N_KV_HEADS, axis=1)
        qbd = jnp.where(own, xt, jnp.zeros_like(xt))
        kk = jnp.concatenate([ck_ref[i], kn_ref[i], pad], axis=0).astype(BF16)
        vv = jnp.concatenate([cv_ref[i], vn_ref[i], pad], axis=0).astype(BF16)
        s = lax.dot_general(qbd, kk, (((1,), (1,)), ((), ())), preferred_element_type=F32)
        s = s + bias_ref[...]
        p, inv = _softmax_with_sink(s, sink_ref[...])
        of = jnp.dot(p.astype(BF16), vv, preferred_element_type=F32)
        of = jnp.where(own, of, 0.0)
        o = of[:, 0:HEAD_DIM]
        for c in range(1, N_KV_HEADS):
            o = o + of[:, c * HEAD_DIM:(c + 1) * HEAD_DIM]
        o_ref[i] = o * inv
        ko_ref[i, 0:keep, :] = ck_ref[i, DEC_SEQ:WINDOW, :]
        ko_ref[i, keep:WINDOW, :] = kn_ref[i, 0:DEC_SEQ, :]
        vo_ref[i, 0:keep, :] = cv_ref[i, DEC_SEQ:WINDOW, :]
        vo_ref[i, keep:WINDOW, :] = vn_ref[i, 0:DEC_SEQ, :]


def _swa_sample(q_rows, k_new8, v_new8, cache_k, cache_v, bias_s, sink_col):
    n = SWA_SEQS_PER_STEP
    seq3 = lambda s: (s, 0, 0)
    full2 = lambda s: (0, 0)
    return pl.pallas_call(
        _swa_sample_body,
        grid=(DEC_BATCH // n,),
        in_specs=[
            pl.BlockSpec((n, S_ROWS, HEAD_DIM), seq3),
            pl.BlockSpec((n, SUBLANES, KV_COLS), seq3),
            pl.BlockSpec((n, SUBLANES, KV_COLS), seq3),
            pl.BlockSpec((n, WINDOW, KV_COLS), seq3),
            pl.BlockSpec((n, WINDOW, KV_COLS), seq3),
            pl.BlockSpec((S_ROWS, 2 * WINDOW), full2),
            pl.BlockSpec((S_ROWS, 1), full2),
        ],
        out_specs=[
            pl.BlockSpec((n, S_ROWS, HEAD_DIM), seq3),
            pl.BlockSpec((n, WINDOW, KV_COLS), seq3),
            pl.BlockSpec((n, WINDOW, KV_COLS), seq3),
        ],
        out_shape=[
            jax.ShapeDtypeStruct((DEC_BATCH, S_ROWS, HEAD_DIM), F32),
            jax.ShapeDtypeStruct((DEC_BATCH, WINDOW, KV_COLS), F32),
            jax.ShapeDtypeStruct((DEC_BATCH, WINDOW, KV_COLS), F32),
        ],
        compiler_params=_cparams(("arbitrary",)),
        name="swa_sample",
    )(q_rows, k_new8, v_new8, cache_k, cache_v, bias_s, sink_col)


def _gla_out(o, gate, norm_w):
    return _rms(o, norm_w) * _silu(gate)


def _split3(x):
    hi = x.astype(BF16)
    r = x - hi.astype(F32)
    mid = r.astype(BF16)
    lo = (r - mid.astype(F32)).astype(BF16)
    return hi, mid, lo


def _cumsum_rows(g):
    c = g.shape[0]
    tri = (lax.broadcasted_iota(jnp.int32, (c, c), 0) >= lax.broadcasted_iota(jnp.int32, (c, c), 1)).astype(BF16)
    return jnp.dot(jnp.concatenate([tri] * 3, axis=1), jnp.concatenate(_split3(g), axis=0),
                   preferred_element_type=F32)


def _causal(a):
    c = a.shape[0]
    keep = lax.broadcasted_iota(jnp.int32, (c, c), 0) >= lax.broadcasted_iota(jnp.int32, (c, c), 1)
    return jnp.where(keep, a, 0.0)


_NT = (((1,), (1,)), ((), ()))
_TN = (((0,), (0,)), ((), ()))


def _gla_prompt_body(*refs):
    proj_refs = refs[:BATCH]
    la_refs = refs[BATCH:2 * BATCH]
    nw_ref, o_ref, s_ref, st_ref = refs[2 * BATCH:]
    c = pl.program_id(0)

    @pl.when(c == 0)
    def _():
        st_ref[...] = jnp.zeros_like(st_ref)

    for bi in range(BATCH):
        p_ref = proj_refs[bi]
        b_all = _cumsum_rows(la_refs[bi][...])
        for h in range(GLA_HEADS):
            kc = slice(h * GLA_DK, (h + 1) * GLA_DK)
            v0 = 2 * GLA_KEY_DIM + h * GLA_DV
            b = b_all[:, kc]
            q = p_ref[:, kc].astype(F32) * GLA_DK ** -0.5
            k = p_ref[:, GLA_KEY_DIM + h * GLA_DK:GLA_KEY_DIM + (h + 1) * GLA_DK].astype(F32)
            v = p_ref[:, v0:v0 + GLA_DV]
            gate = p_ref[:, v0 + GLA_VAL_DIM:v0 + GLA_VAL_DIM + GLA_DV].astype(F32)
            qe = (q * jnp.exp(b)).astype(BF16)
            r = b[GLA_C // 2 - 1:GLA_C // 2, :]
            qr = (q * jnp.exp(b - r)).astype(BF16)
            kr = (k * jnp.exp(r - b)).astype(BF16)
            a = _causal(lax.dot_general(qr, kr, _NT, preferred_element_type=F32))
            st = st_ref[bi, h]
            o = (jnp.dot(a.astype(BF16), v, preferred_element_type=F32)
                 + lax.dot_general(qe, st.astype(BF16), _NT, preferred_element_type=F32))
            b_last = b[GLA_C - 1:GLA_C, :]
            kd = (k * jnp.exp(b_last - b)).astype(BF16)
            st_new = st * jnp.exp(b_last) + lax.dot_general(v, kd, _TN, preferred_element_type=F32)
            st_ref[bi, h] = st_new
            o_ref[bi, :, h * GLA_DV:(h + 1) * GLA_DV] = _gla_out(o, gate, nw_ref[...]).astype(o_ref.dtype)

    @pl.when(c == pl.num_programs(0) - 1)
    def _():
        for bi in range(BATCH):
            for h in range(GLA_HEADS):
                s_ref[bi, h] = st_ref[bi, h].T


def _gla_prompt(proj, log_a, norm_w):
    nc = SEQ // GLA_C
    rows = [functools.partial(lambda bi, c: (bi * nc + c, 0), bi) for bi in range(BATCH)]
    o, s = pl.pallas_call(
        _gla_prompt_body,
        grid=(nc,),
        in_specs=([pl.BlockSpec((GLA_C, GLA_MAIN_DIM), r) for r in rows]
                  + [pl.BlockSpec((GLA_C, GLA_KEY_DIM), r) for r in rows]
                  + [pl.BlockSpec((1, GLA_DV), lambda c: (0, 0))]),
        out_specs=[
            pl.BlockSpec((BATCH, GLA_C, GLA_VAL_DIM), lambda c: (0, c, 0)),
            pl.BlockSpec((BATCH, GLA_HEADS, GLA_DK, GLA_DV), lambda c: (0, 0, 0, 0)),
        ],
        out_shape=[
            jax.ShapeDtypeStruct((BATCH, SEQ, GLA_VAL_DIM), BF16),
            jax.ShapeDtypeStruct((BATCH, GLA_HEADS, GLA_DK, GLA_DV), F32),
        ],
        scratch_shapes=[pltpu.VMEM((BATCH, GLA_HEADS, GLA_DV, GLA_DK), F32)],
        compiler_params=_cparams(("arbitrary",)),
        name="gla_prompt",
    )(*([proj] * BATCH + [log_a] * BATCH + [norm_w.reshape(1, GLA_DV)]))
    return o.reshape(N_PROMPT, GLA_VAL_DIM), s


GLA_SEQS_PER_STEP = 4


def _gla_sample_body(proj_ref, la_ref, s0_ref, nw_ref, o_ref, s_ref):
    ones = jnp.ones((DEC_SEQ, LANES), BF16)
    proj = proj_ref[...].astype(F32)
    la = la_ref[...]
    for i in range(GLA_SEQS_PER_STEP):
        rows_i = slice(i * DEC_SEQ, (i + 1) * DEC_SEQ)
        for h in range(GLA_HEADS):
            q = proj[rows_i, h * GLA_DK:(h + 1) * GLA_DK]
            k = proj[rows_i, GLA_KEY_DIM + h * GLA_DK:GLA_KEY_DIM + (h + 1) * GLA_DK]
            v0 = 2 * GLA_KEY_DIM + h * GLA_DV
            v = proj[rows_i, v0:v0 + GLA_DV].astype(BF16)
            gate = proj[rows_i, v0 + GLA_VAL_DIM:v0 + GLA_VAL_DIM + GLA_DV]
            g = la[rows_i, h * GLA_DK:(h + 1) * GLA_DK]
            rows = [g[0:1]]
            for t in range(1, DEC_SEQ):
                rows.append(rows[-1] + g[t:t + 1])
            b = jnp.concatenate(rows, axis=0)
            b_last = rows[-1]
            qe = (q * GLA_DK ** -0.5 * jnp.exp(b)).astype(BF16)
            ke = (k * jnp.exp(-b)).astype(BF16)
            a = _causal(lax.dot_general(qe, ke, _NT, preferred_element_type=F32))
            s0 = s0_ref[i, h]
            o = (jnp.dot(a.astype(BF16), v, preferred_element_type=F32)
                 + jnp.dot(qe, s0.astype(BF16), preferred_element_type=F32))
            kd = (k * jnp.exp(b_last - b)).astype(BF16)
            dsum = sum(lax.dot_general(piece, ones, _TN, preferred_element_type=F32) for piece in _split3(g))
            decay = jnp.concatenate([jnp.exp(dsum)] * (GLA_DV // LANES), axis=1)
            s_ref[i, h] = s0 * decay + lax.dot_general(kd, v, _TN, preferred_element_type=F32)
            o_ref[rows_i, h * GLA_DV:(h + 1) * GLA_DV] = _gla_out(o, gate, nw_ref[...])


def _gla_sample(proj, log_a, state, norm_w):
    n = GLA_SEQS_PER_STEP
    rows = n * DEC_SEQ
    first = N_PROMPT // rows
    seq4 = lambda s: (s, 0, 0, 0)
    return pl.pallas_call(
        _gla_sample_body,
        grid=(DEC_BATCH // n,),
        in_specs=[
            pl.BlockSpec((rows, GLA_MAIN_DIM), lambda s: (first + s, 0)),
            pl.BlockSpec((rows, GLA_KEY_DIM), lambda s: (first + s, 0)),
            pl.BlockSpec((n, GLA_HEADS, GLA_DK, GLA_DV), seq4),
            pl.BlockSpec((1, GLA_DV), lambda s: (0, 0)),
        ],
        out_specs=[
            pl.BlockSpec((rows, GLA_VAL_DIM), lambda s: (s, 0)),
            pl.BlockSpec((n, GLA_HEADS, GLA_DK, GLA_DV), seq4),
        ],
        out_shape=[
            jax.ShapeDtypeStruct((N_SAMPLE, GLA_VAL_DIM), F32),
            jax.ShapeDtypeStruct((DEC_BATCH, GLA_HEADS, GLA_DK, GLA_DV), F32),
        ],
        compiler_params=_cparams(("arbitrary",)),
        name="gla_sample",
    )(proj, log_a, state, norm_w.reshape(1, GLA_DV))


CAST_ROWS = 512


def _cast_body(w_ref, o_ref):
    o_ref[...] = w_ref[0].astype(BF16)


def _layer_bf16(w, layer):
    _, rows, cols = w.shape
    return pl.pallas_call(
        _cast_body,
        grid=(rows // CAST_ROWS,),
        in_specs=[pl.BlockSpec((1, CAST_ROWS, cols), lambda i: (layer, i, 0))],
        out_specs=pl.BlockSpec((CAST_ROWS, cols), lambda i: (i, 0)),
        out_shape=jax.ShapeDtypeStruct((rows, cols), BF16),
        compiler_params=_cparams(("arbitrary",)),
        name="cast_bf16",
    )(w)


def _slot_rows_body(*refs):
    o_ref = refs[-1]
    for u, w_ref in enumerate(refs[:-1]):
        o_ref[u * HEAD_DIM:(u + 1) * HEAD_DIM] = w_ref[0].astype(BF16)


def _slot_rows_bf16(w, layer):
    def src(u):
        g, hh = u // KV_PER_STEP, u % KV_PER_STEP
        return lambda pair: (layer, (pair * KV_PER_STEP + hh) * GROUP + g, 0)

    cols = w.shape[2]
    return pl.pallas_call(
        _slot_rows_body,
        grid=(N_PAIRS,),
        in_specs=[pl.BlockSpec((1, HEAD_DIM, cols), src(u)) for u in range(PAIR_SLOTS)],
        out_specs=pl.BlockSpec((PAIR_SLOTS * HEAD_DIM, cols), lambda pair: (pair, 0)),
        out_shape=jax.ShapeDtypeStruct((Q_DIM, cols), BF16),
        compiler_params=_cparams(("arbitrary",)),
        name="slot_rows_bf16",
    )(*([w] * PAIR_SLOTS))


def _swa_layer(x, cache_k, cache_v, norm_w, w_qkv, b_qkv, j, w_o_all, b_o, sinks, rel_bias):
    slots = (N_PAIRS, KV_PER_STEP, GROUP, HEAD_DIM)
    q_scale = HEAD_DIM ** -0.5
    assert math.frexp(q_scale)[0] == 0.5, q_scale
    w_q = (w_qkv[:, :Q_DIM] * q_scale).reshape((D_MODEL,) + slots).transpose(0, 1, 3, 2, 4).reshape(D_MODEL, Q_DIM)
    b_q = (b_qkv[:Q_DIM] * q_scale).reshape(slots).transpose(0, 2, 1, 3).reshape(Q_DIM)
    w_qkv_s = jnp.concatenate([w_q, w_qkv[:, Q_DIM:]], axis=1).astype(BF16)
    b_qkv_s = jnp.concatenate([b_q, b_qkv[Q_DIM:]])
    w_o_s = _slot_rows_bf16(w_o_all, j)

    q, kv = _norm_proj(x, norm_w, w_qkv_s, b_qkv_s, ((Q_DIM, BF16), (2 * KV_COLS, F32)), "swa_qkv")
    bias_tbl = _bias_table(rel_bias)
    o_p = _swa_prompt(q, kv, bias_tbl, sinks)

    q_rows = q[N_PROMPT:].astype(F32).reshape(DEC_BATCH, S_ROWS, HEAD_DIM)
    kv_s = kv[N_PROMPT:].reshape(DEC_BATCH, DEC_SEQ, 2 * KV_COLS)
    pad8 = ((0, 0), (0, SUBLANES - DEC_SEQ), (0, 0))
    k_new8 = jnp.pad(kv_s[..., :KV_COLS], pad8)
    v_new8 = jnp.pad(kv_s[..., KV_COLS:], pad8)
    bias_s = bias_tbl[:, :, :DEC_SEQ].transpose(2, 0, 1).reshape(S_ROWS, 2 * WINDOW)
    slot_heads = np.array([_slot_head(s) for s in range(N_HEADS)])
    sink_col = jnp.tile(sinks[slot_heads], DEC_SEQ).reshape(S_ROWS, 1)
    o_s, k_s, v_s = _swa_sample(q_rows, k_new8, v_new8,
                                cache_k.reshape(DEC_BATCH, WINDOW, KV_COLS),
                                cache_v.reshape(DEC_BATCH, WINDOW, KV_COLS), bias_s, sink_col)
    o_s = o_s.reshape(N_SAMPLE, Q_DIM)
    x = _proj_res((o_p, o_s), w_o_s, b_o, x, "swa_out")

    kv_p = jnp.stack([kv[(b + 1) * SEQ - WINDOW:(b + 1) * SEQ] for b in range(BATCH)])
    k_p = kv_p[..., :KV_COLS].reshape(BATCH, WINDOW, N_KV_HEADS, HEAD_DIM)
    v_p = kv_p[..., KV_COLS:].reshape(BATCH, WINDOW, N_KV_HEADS, HEAD_DIM)
    shape_s = (DEC_BATCH, WINDOW, N_KV_HEADS, HEAD_DIM)
    return x, k_p, v_p, k_s.reshape(shape_s), v_s.reshape(shape_s)


def _gla_layer(x, state, norm_w, j, w_in_all, w_gk2, b_gk, gnorm, w_o_all):
    proj, log_a = _norm_proj(x, norm_w, w_in_all[j].astype(BF16), jnp.zeros((GLA_MAIN_DIM,), F32),
                             ((GLA_MAIN_DIM, BF16),), "gla_in", gate=(w_gk2.astype(BF16), b_gk))
    o_p, s_p = _gla_prompt(proj, log_a, gnorm)
    o_s, s_s = _gla_sample(proj, log_a, state, gnorm)
    x = _proj_res((o_p, o_s), _layer_bf16(w_o_all, j), jnp.zeros((D_MODEL,), F32), x, "gla_out")
    return x, s_p, s_s


def kernel(x_prompt, x_sample, cache_swa_k, cache_swa_v, state_gla, norm_ffn1, ffn1_w_gate, ffn1_w_up,
           ffn1_w_down, norm_mix, norm_ffn2, ffn2_w_gate, ffn2_w_up, ffn2_w_down, norm_final, rel_bias,
           swa_w_qkv, swa_b_qkv, swa_w_o, swa_b_o, swa_sinks, gla_w_in, gla_w_gk2, gla_b_gk, gla_norm,
           gla_w_o):
    x = (x_prompt.reshape(N_PROMPT, D_MODEL), x_sample.reshape(N_SAMPLE, D_MODEL))
    swa_kp, swa_vp, swa_ks, swa_vs, gla_sp, gla_ss = [], [], [], [], [], []
    for i in range(DEPTH):
        x = _ffn_pair(x, i, norm_ffn1[i], ffn1_w_gate, ffn1_w_up, ffn1_w_down)
        j = i // 2
        if i % 2 == 0:
            x, kp, vp, ks, vs = _swa_layer(x, cache_swa_k[j], cache_swa_v[j], norm_mix[i], swa_w_qkv[j],
                                           swa_b_qkv[j], j, swa_w_o, swa_b_o[j], swa_sinks[j], rel_bias)
            swa_kp.append(kp)
            swa_vp.append(vp)
            swa_ks.append(ks)
            swa_vs.append(vs)
        else:
            x, sp, ss = _gla_layer(x, state_gla[j], norm_mix[i], j, gla_w_in, gla_w_gk2[j], gla_b_gk[j],
                                   gla_norm[j], gla_w_o)
            gla_sp.append(sp)
            gla_ss.append(ss)
        final_w = norm_final if i == DEPTH - 1 else None
        x = _ffn_pair(x, i, norm_ffn2[i], ffn2_w_gate, ffn2_w_up, ffn2_w_down, final_w)
    y_prompt = x[0].reshape(BATCH, SEQ, D_MODEL)
    y_sample = x[1].reshape(DEC_BATCH, DEC_SEQ, D_MODEL)
    return (y_prompt, y_sample, jnp.stack(swa_kp), jnp.stack(swa_vp), jnp.stack(swa_ks), jnp.stack(swa_vs),
            jnp.stack(gla_sp), jnp.stack(gla_ss))
```
